```python
import jax, jax.numpy as jnp
from jax import lax
import numpy as np

D_MODEL = 1024
BATCH = 8
SEQ = 4096
DEPTH = 1

CHUNK = 64
D_MIX = D_MODEL
D_CONV = D_MIX // 2
CONV_HEADS = 8
CONV_WIDTH = 3
D_POOL = D_MIX - D_CONV
POOL_WINDOWS = (2, 4, 8, 16)
POOL_GROUPS = len(POOL_WINDOWS)
POOL_GC = D_POOL // POOL_GROUPS
D_FF = 2816
EPS = 1e-6

kernel_name = "hybrid_conv_pool_macaron_block"


def rms_norm(x, g):
    xf = x.astype(jnp.float32)
    y = xf * lax.rsqrt(jnp.mean(xf * xf, axis=-1, keepdims=True) + EPS)
    return (y * g.astype(jnp.float32)).astype(x.dtype)


def swiglu(h, w_gate, w_up, w_down):
    return (jax.nn.silu(h @ w_gate) * (h @ w_up)) @ w_down


def causal_depthwise_conv(z, w):
    s = z.shape[1]
    zp = jnp.pad(z, ((0, 0), (CONV_WIDTH - 1, 0), (0, 0)))
    return sum(w[k] * zp[:, k:k + s] for k in range(CONV_WIDTH))


def multiscale_pool_minus_self(u):
    s = u.shape[1]
    uf = u.astype(jnp.float32)
    c = jnp.cumsum(uf, axis=1)
    t1 = jnp.arange(1, s + 1, dtype=jnp.float32)
    outs = []
    for g, w in enumerate(POOL_WINDOWS):
        cg = c[:, :, g]
        shifted = jnp.pad(cg, ((0, 0), (w, 0), (0, 0)))[:, :s]
        count = jnp.minimum(t1, float(w))[None, :, None]
        outs.append((cg - shifted) / count - uf[:, :, g])
    return jnp.stack(outs, axis=2).astype(u.dtype)


def _fwd_setup_inputs(seed: int = 0) -> dict:
    key = jax.random.key(seed)
    ks = jax.random.split(key, 20)
    f32 = jnp.float32

    def nrm(k, shape, fan_in):
        return jax.random.normal(k, shape, f32) * (fan_in ** -0.5)

    def gain(k, shape):
        return 1.0 + 0.02 * jax.random.normal(k, shape, f32)

    L = DEPTH
    return {
        "x": jax.random.normal(ks[0], (BATCH, SEQ, D_MODEL), f32),
        "norm_ffn1": gain(ks[1], (L, D_MODEL)),
        "ffn1_w_gate": nrm(ks[2], (L, D_MODEL, D_FF), D_MODEL),
        "ffn1_w_up": nrm(ks[3], (L, D_MODEL, D_FF), D_MODEL),
        "ffn1_w_down": nrm(ks[4], (L, D_FF, D_MODEL), D_FF),
        "norm_mix": gain(ks[5], (L, D_MODEL)),
        "w_in": nrm(ks[6], (L, D_MODEL, 3 * D_CONV + D_POOL), D_MODEL),
        "conv_w": nrm(ks[7], (L, CONV_WIDTH, D_CONV), CONV_WIDTH),
        "pool_w": nrm(ks[8], (L, POOL_GROUPS, POOL_GC, POOL_GC), POOL_GC),
        "pool_scale": gain(ks[9], (L, D_POOL)),
        "w_out": nrm(ks[10], (L, D_MIX, D_MODEL), D_MIX),
        "norm_ffn2": gain(ks[11], (L, D_MODEL)),
        "ffn2_w_gate": nrm(ks[12], (L, D_MODEL, D_FF), D_MODEL),
        "ffn2_w_up": nrm(ks[13], (L, D_MODEL, D_FF), D_MODEL),
        "ffn2_w_down": nrm(ks[14], (L, D_FF, D_MODEL), D_FF),
        "norm_final": gain(ks[15], (D_MODEL,)),
    }


def _fwd_reference(x, norm_ffn1, ffn1_w_gate, ffn1_w_up, ffn1_w_down, norm_mix, w_in, conv_w,
              pool_w, pool_scale, w_out, norm_ffn2, ffn2_w_gate, ffn2_w_up, ffn2_w_down,
              norm_final):
    b, s, _ = x.shape
    for l in range(DEPTH):
        x = x + 0.5 * swiglu(rms_norm(x, norm_ffn1[l]), ffn1_w_gate[l], ffn1_w_up[l], ffn1_w_down[l])

        h = rms_norm(x, norm_mix[l])
        proj = h @ w_in[l]
        v = proj[..., :D_CONV]
        gate_b = proj[..., D_CONV:2 * D_CONV]
        gate_c = proj[..., 2 * D_CONV:3 * D_CONV]
        u = proj[..., 3 * D_CONV:]

        y_a = gate_b * causal_depthwise_conv(gate_c * v, conv_w[l])

        ug = u.reshape(b, s, POOL_GROUPS, POOL_GC)
        pooled = multiscale_pool_minus_self(ug)
        y_b = jnp.einsum("bsgc,gcd->bsgd", pooled, pool_w[l]).reshape(b, s, D_POOL) * pool_scale[l]

        x = x + jnp.concatenate([y_a, y_b], axis=-1) @ w_out[l]

        x = x + 0.5 * swiglu(rms_norm(x, norm_ffn2[l]), ffn2_w_gate[l], ffn2_w_up[l], ffn2_w_down[l])
    return rms_norm(x, norm_final)


import jax as _jax
import jax.numpy as _jnp

TWIN_FORMAT = 'train_step'
FWD_PARAMS = ['x', 'norm_ffn1', 'ffn1_w_gate', 'ffn1_w_up', 'ffn1_w_down', 'norm_mix', 'w_in', 'conv_w', 'pool_w', 'pool_scale', 'w_out', 'norm_ffn2', 'ffn2_w_gate', 'ffn2_w_up', 'ffn2_w_down', 'norm_final']
TWIN_WEIGHTS = ['norm_ffn1', 'ffn1_w_gate', 'ffn1_w_up', 'ffn1_w_down', 'norm_mix', 'w_in', 'conv_w', 'pool_w', 'pool_scale', 'w_out', 'norm_ffn2', 'ffn2_w_gate', 'ffn2_w_up', 'ffn2_w_down', 'norm_final']
TWIN_DIFF_INPUT = 'x'
TWIN_INPUTS = ['x', 'norm_ffn1', 'ffn1_w_gate', 'ffn1_w_up', 'ffn1_w_down', 'norm_mix', 'w_in', 'conv_w', 'pool_w', 'pool_scale', 'w_out', 'norm_ffn2', 'ffn2_w_gate', 'ffn2_w_up', 'ffn2_w_down', 'norm_final', 'loss_target', 'm_norm_ffn1', 'm_ffn1_w_gate', 'm_ffn1_w_up', 'm_ffn1_w_down', 'm_norm_mix', 'm_w_in', 'm_conv_w', 'm_pool_w', 'm_pool_scale', 'm_w_out', 'm_norm_ffn2', 'm_ffn2_w_gate', 'm_ffn2_w_up', 'm_ffn2_w_down', 'm_norm_final', 'v_norm_ffn1', 'v_ffn1_w_gate', 'v_ffn1_w_up', 'v_ffn1_w_down', 'v_norm_mix', 'v_w_in', 'v_conv_w', 'v_pool_w', 'v_pool_scale', 'v_w_out', 'v_norm_ffn2', 'v_ffn2_w_gate', 'v_ffn2_w_up', 'v_ffn2_w_down', 'v_norm_final']
TWIN_OUTPUTS = ['loss', 'grad_x', 'grad_norm_ffn1', 'grad_ffn1_w_gate', 'grad_ffn1_w_up', 'grad_ffn1_w_down', 'grad_norm_mix', 'grad_w_in', 'grad_conv_w', 'grad_pool_w', 'grad_pool_scale', 'grad_w_out', 'grad_norm_ffn2', 'grad_ffn2_w_gate', 'grad_ffn2_w_up', 'grad_ffn2_w_down', 'grad_norm_final', 'delta_norm_ffn1', 'delta_ffn1_w_gate', 'delta_ffn1_w_up', 'delta_ffn1_w_down', 'delta_norm_mix', 'delta_w_in', 'delta_conv_w', 'delta_pool_w', 'delta_pool_scale', 'delta_w_out', 'delta_norm_ffn2', 'delta_ffn2_w_gate', 'delta_ffn2_w_up', 'delta_ffn2_w_down', 'delta_norm_final', 'new_m_norm_ffn1', 'new_m_ffn1_w_gate', 'new_m_ffn1_w_up', 'new_m_ffn1_w_down', 'new_m_norm_mix', 'new_m_w_in', 'new_m_conv_w', 'new_m_pool_w', 'new_m_pool_scale', 'new_m_w_out', 'new_m_norm_ffn2', 'new_m_ffn2_w_gate', 'new_m_ffn2_w_up', 'new_m_ffn2_w_down', 'new_m_norm_final', 'new_v_norm_ffn1', 'new_v_ffn1_w_gate', 'new_v_ffn1_w_up', 'new_v_ffn1_w_down', 'new_v_norm_mix', 'new_v_w_in', 'new_v_conv_w', 'new_v_pool_w', 'new_v_pool_scale', 'new_v_w_out', 'new_v_norm_ffn2', 'new_v_ffn2_w_gate', 'new_v_ffn2_w_up', 'new_v_ffn2_w_down', 'new_v_norm_final']
TWIN_LEAF_KINDS = {'loss': 'loss', 'grad_x': 'grad_x', 'grad_norm_ffn1': 'grad_w', 'grad_ffn1_w_gate': 'grad_w', 'grad_ffn1_w_up': 'grad_w', 'grad_ffn1_w_down': 'grad_w', 'grad_norm_mix': 'grad_w', 'grad_w_in': 'grad_w', 'grad_conv_w': 'grad_w', 'grad_pool_w': 'grad_w', 'grad_pool_scale': 'grad_w', 'grad_w_out': 'grad_w', 'grad_norm_ffn2': 'grad_w', 'grad_ffn2_w_gate': 'grad_w', 'grad_ffn2_w_up': 'grad_w', 'grad_ffn2_w_down': 'grad_w', 'grad_norm_final': 'grad_w', 'delta_norm_ffn1': 'delta_w', 'delta_ffn1_w_gate': 'delta_w', 'delta_ffn1_w_up': 'delta_w', 'delta_ffn1_w_down': 'delta_w', 'delta_norm_mix': 'delta_w', 'delta_w_in': 'delta_w', 'delta_conv_w': 'delta_w', 'delta_pool_w': 'delta_w', 'delta_pool_scale': 'delta_w', 'delta_w_out': 'delta_w', 'delta_norm_ffn2': 'delta_w', 'delta_ffn2_w_gate': 'delta_w', 'delta_ffn2_w_up': 'delta_w', 'delta_ffn2_w_down': 'delta_w', 'delta_norm_final': 'delta_w', 'new_m_norm_ffn1': 'new_m', 'new_m_ffn1_w_gate': 'new_m', 'new_m_ffn1_w_up': 'new_m', 'new_m_ffn1_w_down': 'new_m', 'new_m_norm_mix': 'new_m', 'new_m_w_in': 'new_m', 'new_m_conv_w': 'new_m', 'new_m_pool_w': 'new_m', 'new_m_pool_scale': 'new_m', 'new_m_w_out': 'new_m', 'new_m_norm_ffn2': 'new_m', 'new_m_ffn2_w_gate': 'new_m', 'new_m_ffn2_w_up': 'new_m', 'new_m_ffn2_w_down': 'new_m', 'new_m_norm_final': 'new_m', 'new_v_norm_ffn1': 'new_v', 'new_v_ffn1_w_gate': 'new_v', 'new_v_ffn1_w_up': 'new_v', 'new_v_ffn1_w_down': 'new_v', 'new_v_norm_mix': 'new_v', 'new_v_w_in': 'new_v', 'new_v_conv_w': 'new_v', 'new_v_pool_w': 'new_v', 'new_v_pool_scale': 'new_v', 'new_v_w_out': 'new_v', 'new_v_norm_ffn2': 'new_v', 'new_v_ffn2_w_gate': 'new_v', 'new_v_ffn2_w_up': 'new_v', 'new_v_ffn2_w_down': 'new_v', 'new_v_norm_final': 'new_v'}


def _forward(args):
    return _fwd_reference(*[args[k] for k in FWD_PARAMS])


def _output_shape():
    def fwd():
        inp = _fwd_setup_inputs(0)
        return _fwd_reference(*[inp[k] for k in FWD_PARAMS])
    out = _jax.eval_shape(fwd)
    return out.shape, out.dtype

N_MICROBATCH = 1
ADAM_LR = 0.001
ADAM_B1 = 0.9
ADAM_B2 = 0.999
ADAM_EPS = 1e-08
ADAM_WD = 0.01
ADAM_STEP = 10
PER_EXAMPLE_BATCH_AXIS = {'x': 0, 'loss_target': 0}
SHARED_INPUTS = []
_WEIGHT_DTYPES = {'norm_ffn1': _jnp.float32, 'ffn1_w_gate': _jnp.float32, 'ffn1_w_up': _jnp.float32, 'ffn1_w_down': _jnp.float32, 'norm_mix': _jnp.float32, 'w_in': _jnp.float32, 'conv_w': _jnp.float32, 'pool_w': _jnp.float32, 'pool_scale': _jnp.float32, 'w_out': _jnp.float32, 'norm_ffn2': _jnp.float32, 'ffn2_w_gate': _jnp.float32, 'ffn2_w_up': _jnp.float32, 'ffn2_w_down': _jnp.float32, 'norm_final': _jnp.float32}
MOMENT_SCALE = {'norm_ffn1': 9.264786e-02, 'ffn1_w_gate': 3.949821e-02, 'ffn1_w_up': 3.819385e-02, 'ffn1_w_down': 6.338899e-02, 'norm_mix': 1.766414e-01, 'w_in': 1.245374e-01, 'conv_w': 1.301845e-01, 'pool_w': 1.149522e-01, 'pool_scale': 1.318941e-01, 'w_out': 1.222070e-01, 'norm_ffn2': 5.290302e-02, 'ffn2_w_gate': 2.306568e-02, 'ffn2_w_up': 2.233306e-02, 'ffn2_w_down': 3.704414e-02, 'norm_final': 3.201442e+01}


def _to_microbatches(a, axis):
    t = _jnp.moveaxis(a, axis, 0)
    t = t.reshape((N_MICROBATCH, t.shape[0] // N_MICROBATCH) + t.shape[1:])
    return _jnp.moveaxis(t, 1, axis + 1)


def setup_inputs(seed: int = 0) -> dict:
    inp = _fwd_setup_inputs(seed)
    key = _jax.random.fold_in(_jax.random.key(seed), 7919)
    shape, _ = _output_shape()
    out = dict(inp)
    out["loss_target"] = _jax.random.normal(_jax.random.fold_in(key, 0), shape, _jnp.float32)
    for i, name in enumerate(TWIN_WEIGHTS):
        w = inp[name].astype(_jnp.float32)
        if MOMENT_SCALE is None:
            s = _jnp.sqrt(_jnp.mean(_jnp.square(w)) + 1e-30)
        else:
            s = MOMENT_SCALE[name]
        km, kv = _jax.random.split(_jax.random.fold_in(key, i + 1))
        out[name] = w
        out["m_" + name] = s * _jax.random.normal(km, w.shape, _jnp.float32)
        out["v_" + name] = (s * s) * _jax.random.uniform(kv, w.shape, _jnp.float32, 0.5, 1.5)
    if N_MICROBATCH > 1:
        for name, axis in PER_EXAMPLE_BATCH_AXIS.items():
            out[name] = _to_microbatches(out[name], axis)
    return {'x': out['x'], 'norm_ffn1': out['norm_ffn1'], 'ffn1_w_gate': out['ffn1_w_gate'], 'ffn1_w_up': out['ffn1_w_up'], 'ffn1_w_down': out['ffn1_w_down'], 'norm_mix': out['norm_mix'], 'w_in': out['w_in'], 'conv_w': out['conv_w'], 'pool_w': out['pool_w'], 'pool_scale': out['pool_scale'], 'w_out': out['w_out'], 'norm_ffn2': out['norm_ffn2'], 'ffn2_w_gate': out['ffn2_w_gate'], 'ffn2_w_up': out['ffn2_w_up'], 'ffn2_w_down': out['ffn2_w_down'], 'norm_final': out['norm_final'], 'loss_target': out['loss_target'], 'm_norm_ffn1': out['m_norm_ffn1'], 'm_ffn1_w_gate': out['m_ffn1_w_gate'], 'm_ffn1_w_up': out['m_ffn1_w_up'], 'm_ffn1_w_down': out['m_ffn1_w_down'], 'm_norm_mix': out['m_norm_mix'], 'm_w_in': out['m_w_in'], 'm_conv_w': out['m_conv_w'], 'm_pool_w': out['m_pool_w'], 'm_pool_scale': out['m_pool_scale'], 'm_w_out': out['m_w_out'], 'm_norm_ffn2': out['m_norm_ffn2'], 'm_ffn2_w_gate': out['m_ffn2_w_gate'], 'm_ffn2_w_up': out['m_ffn2_w_up'], 'm_ffn2_w_down': out['m_ffn2_w_down'], 'm_norm_final': out['m_norm_final'], 'v_norm_ffn1': out['v_norm_ffn1'], 'v_ffn1_w_gate': out['v_ffn1_w_gate'], 'v_ffn1_w_up': out['v_ffn1_w_up'], 'v_ffn1_w_down': out['v_ffn1_w_down'], 'v_norm_mix': out['v_norm_mix'], 'v_w_in': out['v_w_in'], 'v_conv_w': out['v_conv_w'], 'v_pool_w': out['v_pool_w'], 'v_pool_scale': out['v_pool_scale'], 'v_w_out': out['v_w_out'], 'v_norm_ffn2': out['v_norm_ffn2'], 'v_ffn2_w_gate': out['v_ffn2_w_gate'], 'v_ffn2_w_up': out['v_ffn2_w_up'], 'v_ffn2_w_down': out['v_ffn2_w_down'], 'v_norm_final': out['v_norm_final']}


def _loss(weights, diff, rest, loss_target):
    with _jax.named_scope("forward"):
        args = {**rest, TWIN_DIFF_INPUT: diff, **{k: w.astype(_WEIGHT_DTYPES[k]) for k, w in weights.items()}}
        y = _forward(args)
    with _jax.named_scope("loss_head"):
        err = _jnp.square(y.astype(_jnp.float32) - loss_target)
        return 0.5 * _jnp.sum(_jnp.mean(err, axis=-1)) if err.ndim else 0.5 * err


def _adamw(w, g, m, v):
    m = ADAM_B1 * m + (1.0 - ADAM_B1) * g
    v = ADAM_B2 * v + (1.0 - ADAM_B2) * _jnp.square(g)
    m_hat = m / (1.0 - ADAM_B1 ** ADAM_STEP)
    v_hat = v / (1.0 - ADAM_B2 ** ADAM_STEP)
    delta = -ADAM_LR * (m_hat / (_jnp.sqrt(v_hat) + ADAM_EPS) + ADAM_WD * w)
    return delta, m, v


def reference(x, norm_ffn1, ffn1_w_gate, ffn1_w_up, ffn1_w_down, norm_mix, w_in, conv_w, pool_w, pool_scale, w_out, norm_ffn2, ffn2_w_gate, ffn2_w_up, ffn2_w_down, norm_final, loss_target, m_norm_ffn1, m_ffn1_w_gate, m_ffn1_w_up, m_ffn1_w_down, m_norm_mix, m_w_in, m_conv_w, m_pool_w, m_pool_scale, m_w_out, m_norm_ffn2, m_ffn2_w_gate, m_ffn2_w_up, m_ffn2_w_down, m_norm_final, v_norm_ffn1, v_ffn1_w_gate, v_ffn1_w_up, v_ffn1_w_down, v_norm_mix, v_w_in, v_conv_w, v_pool_w, v_pool_scale, v_w_out, v_norm_ffn2, v_ffn2_w_gate, v_ffn2_w_up, v_ffn2_w_down, v_norm_final):
    given = dict(x=x, norm_ffn1=norm_ffn1, ffn1_w_gate=ffn1_w_gate, ffn1_w_up=ffn1_w_up, ffn1_w_down=ffn1_w_down, norm_mix=norm_mix, w_in=w_in, conv_w=conv_w, pool_w=pool_w, pool_scale=pool_scale, w_out=w_out, norm_ffn2=norm_ffn2, ffn2_w_gate=ffn2_w_gate, ffn2_w_up=ffn2_w_up, ffn2_w_down=ffn2_w_down, norm_final=norm_final, loss_target=loss_target, m_norm_ffn1=m_norm_ffn1, m_ffn1_w_gate=m_ffn1_w_gate, m_ffn1_w_up=m_ffn1_w_up, m_ffn1_w_down=m_ffn1_w_down, m_norm_mix=m_norm_mix, m_w_in=m_w_in, m_conv_w=m_conv_w, m_pool_w=m_pool_w, m_pool_scale=m_pool_scale, m_w_out=m_w_out, m_norm_ffn2=m_norm_ffn2, m_ffn2_w_gate=m_ffn2_w_gate, m_ffn2_w_up=m_ffn2_w_up, m_ffn2_w_down=m_ffn2_w_down, m_norm_final=m_norm_final, v_norm_ffn1=v_norm_ffn1, v_ffn1_w_gate=v_ffn1_w_gate, v_ffn1_w_up=v_ffn1_w_up, v_ffn1_w_down=v_ffn1_w_down, v_norm_mix=v_norm_mix, v_w_in=v_w_in, v_conv_w=v_conv_w, v_pool_w=v_pool_w, v_pool_scale=v_pool_scale, v_w_out=v_w_out, v_norm_ffn2=v_norm_ffn2, v_ffn2_w_gate=v_ffn2_w_gate, v_ffn2_w_up=v_ffn2_w_up, v_ffn2_w_down=v_ffn2_w_down, v_norm_final=v_norm_final)
    weights = {n: given[n] for n in TWIN_WEIGHTS}
    shared = {n: given[n] for n in SHARED_INPUTS}
    per_example = {n: given[n] for n in ['x']}
    grad_fn = _jax.value_and_grad(_loss, argnums=(0, 1))

    def one_microbatch(ex, loss_target):
        ex = dict(ex)
        diff = ex.pop(TWIN_DIFF_INPUT)
        return grad_fn(weights, diff, {**shared, **ex}, loss_target)

    if N_MICROBATCH == 1:
        loss, (grad_w, grad_x) = one_microbatch(per_example, given["loss_target"])
    else:
        def body(carry, xs):
            loss_sum, grad_sum = carry
            l_k, (gw_k, gx_k) = one_microbatch(xs[0], xs[1])
            with _jax.named_scope("update"):
                return (loss_sum + l_k, _jax.tree.map(_jnp.add, grad_sum, gw_k)), gx_k

        init = (_jnp.zeros((), _jnp.float32), _jax.tree.map(_jnp.zeros_like, weights))
        (loss, grad_w), grad_x = _jax.lax.scan(body, init, (per_example, given["loss_target"]))
    with _jax.named_scope("update"):
        delta_w, new_m, new_v = {}, {}, {}
        for n in TWIN_WEIGHTS:
            delta_w[n], new_m[n], new_v[n] = _adamw(weights[n], grad_w[n], given["m_" + n], given["v_" + n])
    return (loss, grad_x, *[grad_w[n] for n in TWIN_WEIGHTS], *[delta_w[n] for n in TWIN_WEIGHTS],
            *[new_m[n] for n in TWIN_WEIGHTS], *[new_v[n] for n in TWIN_WEIGHTS])
```

```python
import functools

import jax
import jax.numpy as jnp
from jax import lax
from jax.experimental import pallas as pl
from jax.experimental.pallas import tpu as pltpu

F32 = jnp.float32
BF16 = jnp.bfloat16
MESH = pl.DeviceIdType.MESH

EPS = 1e-6
POOL_WINDOWS = (2, 4, 8, 16)
ADAM_LR = 0.001
ADAM_B1 = 0.9
ADAM_B2 = 0.999
ADAM_EPS = 1e-08
ADAM_WD = 0.01
ADAM_STEP = 10

N_CHIPS = 4
MXU_COLS_V7X = 256
VMEM_LIMIT = 56 * 1024 * 1024
TM_FFN = 512
TM_MIX = 512
TM_TN = 1024
TM_EW = 512
HALO = 16


def _nt(a, b):
    return lax.dot_general(a, b, (((1,), (1,)), ((), ())), preferred_element_type=F32)


def _tn(a, b):
    return lax.dot_general(a, b, (((0,), (0,)), ((), ())), preferred_element_type=F32)


def _nn(a, b):
    return jnp.dot(a, b, preferred_element_type=F32)


def _sigmoid(a):
    return 1.0 / (1.0 + jnp.exp(-a))


def _feature_chunks(n):
    assert n % MXU_COLS_V7X == 0
    tiles = n // MXU_COLS_V7X
    first = (tiles + 1) // 2
    sizes = [first * MXU_COLS_V7X, (tiles - first) * MXU_COLS_V7X]
    out, s0 = [], 0
    for sz in sizes:
        if sz:
            out.append((s0, sz))
            s0 += sz
    return out


def _row_block(rows, cap):
    best = 8
    for b in range(8, min(rows, cap) + 1, 8):
        if rows % b == 0:
            best = b
    assert rows % best == 0
    return best


def _my_place():
    x, y, c = lax.axis_index("x"), lax.axis_index("y"), lax.axis_index("c")
    return x, y, c


def _other_chips(x, y):
    return [(1 - x, y), (x, 1 - y), (1 - x, 1 - y)]


HBM_SPEC = pl.BlockSpec(memory_space=pltpu.HBM)


def _all_gather_shards(shards):
    n = len(shards)
    for s in shards:
        assert s.shape[0] % 32 == 0

    def body(*refs):
        ins, outs = refs[:n], refs[n:2 * n]
        send_sems, recv_sems, local_sems = refs[2 * n:]
        x, y, c = _my_place()
        sibling = (x, y, 1 - c)
        chips = _other_chips(x, y)
        mine = 2 * x + y

        def rows_of(a, chip_index, half):
            rps = shards[a].shape[0]
            hr = rps // 2
            start = pl.multiple_of(chip_index * rps + half * hr, 16)
            return outs[a].at[pl.ds(start, hr), :]

        def remote(a, slot, src, dst, to):
            return pltpu.make_async_remote_copy(
                src_ref=src, dst_ref=dst, send_sem=send_sems.at[a * 6 + slot], recv_sem=recv_sems.at[a * 6 + slot],
                device_id=to, device_id_type=MESH)

        def own_copy(a):
            rps = shards[a].shape[0]
            return pltpu.make_async_copy(ins[a], outs[a].at[pl.ds(pl.multiple_of(mine * rps, 16), rps), :], local_sems.at[a])

        def my_half(a):
            hr = shards[a].shape[0] // 2
            return ins[a].at[pl.ds(pl.multiple_of(c * hr, 16), hr), :]

        for a in range(n):
            own_copy(a).start()
            for j, chip in enumerate(chips):
                remote(a, j, my_half(a), rows_of(a, mine, c), (*chip, c)).start()
        for a in range(n):
            for j, chip in enumerate(chips):
                landed = rows_of(a, 2 * chip[0] + chip[1], c)
                remote(a, j, landed, landed, (*chip, c)).wait_recv()
                remote(a, 3 + j, landed, landed, sibling).start()
        for a in range(n):
            for j, chip in enumerate(chips):
                from_sibling = rows_of(a, 2 * chip[0] + chip[1], 1 - c)
                remote(a, 3 + j, from_sibling, from_sibling, sibling).wait_recv()
        for a in range(n):
            for j, chip in enumerate(chips):
                remote(a, j, my_half(a), rows_of(a, mine, c), (*chip, c)).wait_send()
                landed = rows_of(a, 2 * chip[0] + chip[1], c)
                remote(a, 3 + j, landed, landed, sibling).wait_send()
            own_copy(a).wait()

    return pl.pallas_call(
        body, name="gather_weights",
        out_shape=[jax.ShapeDtypeStruct((N_CHIPS * s.shape[0], s.shape[1]), s.dtype) for s in shards],
        in_specs=[HBM_SPEC] * n, out_specs=[HBM_SPEC] * n,
        scratch_shapes=[pltpu.SemaphoreType.DMA((6 * n,)), pltpu.SemaphoreType.DMA((6 * n,)), pltpu.SemaphoreType.DMA((n,))],
    )(*shards)


def _sibling_swap_halves(grads):
    n = len(grads)

    def body(*refs):
        ins, outs = refs[:n], refs[n:2 * n]
        send_sems, recv_sems = refs[2 * n:]
        x, y, c = _my_place()
        copies = []
        for a in range(n):
            cp = pltpu.make_async_remote_copy(
                src_ref=ins[a].at[:, 1 - c], dst_ref=outs[a], send_sem=send_sems.at[a], recv_sem=recv_sems.at[a],
                device_id=(x, y, 1 - c), device_id_type=MESH)
            cp.start()
            copies.append(cp)
        for cp in copies:
            cp.wait()

    return pl.pallas_call(
        body, name="grad_swap_sibling",
        out_shape=[jax.ShapeDtypeStruct((N_CHIPS,) + g.shape[2:], g.dtype) for g in grads],
        in_specs=[HBM_SPEC] * n, out_specs=[HBM_SPEC] * n,
        scratch_shapes=[pltpu.SemaphoreType.DMA((n,)), pltpu.SemaphoreType.DMA((n,))],
    )(*grads)


def _pair_sum(grad, got, c):
    _, _, hr, cols = grad.shape
    br = _row_block(hr, 256)

    def body(c_ref, g_ref, r_ref, o_ref):
        o_ref[...] = (g_ref[...].astype(F32) + r_ref[...].astype(F32)).astype(BF16)

    return pl.pallas_call(
        body, name="grad_pair_sum",
        out_shape=jax.ShapeDtypeStruct((N_CHIPS, hr, cols), BF16),
        grid_spec=pltpu.PrefetchScalarGridSpec(
            num_scalar_prefetch=1, grid=(N_CHIPS, hr // br),
            in_specs=[pl.BlockSpec((None, None, br, cols), lambda j, r, c_ref: (j, c_ref[0], r, 0)),
                      pl.BlockSpec((None, br, cols), lambda j, r, c_ref: (j, r, 0))],
            out_specs=pl.BlockSpec((None, br, cols), lambda j, r, c_ref: (j, r, 0))),
        compiler_params=pltpu.CompilerParams(dimension_semantics=("parallel", "parallel")),
    )(c, grad, got)


def _chip_exchange(pairs):
    n = len(pairs)

    def body(*refs):
        ins, outs = refs[:n], refs[n:2 * n]
        send_sems, recv_sems = refs[2 * n:]
        x, y, c = _my_place()
        copies = []
        for a in range(n):
            for j, chip in enumerate(_other_chips(x, y)):
                cp = pltpu.make_async_remote_copy(
                    src_ref=ins[a].at[2 * chip[0] + chip[1]], dst_ref=outs[a].at[j],
                    send_sem=send_sems.at[3 * a + j], recv_sem=recv_sems.at[3 * a + j],
                    device_id=(*chip, c), device_id_type=MESH)
                cp.start()
                copies.append(cp)
        for cp in copies:
            cp.wait()

    return pl.pallas_call(
        body, name="grad_exchange_chips",
        out_shape=[jax.ShapeDtypeStruct((3,) + p.shape[1:], p.dtype) for p in pairs],
        in_specs=[HBM_SPEC] * n, out_specs=[HBM_SPEC] * n,
        scratch_shapes=[pltpu.SemaphoreType.DMA((3 * n,)), pltpu.SemaphoreType.DMA((3 * n,))],
    )(*pairs)


def _chip_sum(pair, got, chip_index):
    _, hr, cols = pair.shape
    br = _row_block(hr, 256)

    def body(k_ref, p_ref, r_ref, o_ref):
        acc = p_ref[...].astype(F32)
        for j in range(3):
            acc = acc + r_ref[j].astype(F32)
        o_ref[...] = acc

    return pl.pallas_call(
        body, name="grad_chip_sum",
        out_shape=jax.ShapeDtypeStruct((hr, cols), F32),
        grid_spec=pltpu.PrefetchScalarGridSpec(
            num_scalar_prefetch=1, grid=(hr // br,),
            in_specs=[pl.BlockSpec((None, br, cols), lambda r, k_ref: (k_ref[0], r, 0)),
                      pl.BlockSpec((3, br, cols), lambda r, k_ref: (0, r, 0))],
            out_specs=pl.BlockSpec((br, cols), lambda r, k_ref: (r, 0))),
        compiler_params=pltpu.CompilerParams(dimension_semantics=("parallel",)),
    )(chip_index, pair, got)


def _sibling_share(halves):
    n = len(halves)

    def body(*refs):
        ins, outs = refs[:n], refs[n:2 * n]
        send_sems, recv_sems, local_sems = refs[2 * n:]
        x, y, c = _my_place()
        copies = []
        for a in range(n):
            own = pltpu.make_async_copy(ins[a], outs[a].at[c], local_sems.at[a])
            own.start()
            cp = pltpu.make_async_remote_copy(
                src_ref=ins[a], dst_ref=outs[a].at[c], send_sem=send_sems.at[a], recv_sem=recv_sems.at[a],
                device_id=(x, y, 1 - c), device_id_type=MESH)
            cp.start()
            copies.append((own, cp))
        for own, cp in copies:
            cp.wait()
            own.wait()

    return pl.pallas_call(
        body, name="grad_share_sibling",
        out_shape=[jax.ShapeDtypeStruct((2,) + h.shape, h.dtype) for h in halves],
        in_specs=[HBM_SPEC] * n, out_specs=[HBM_SPEC] * n,
        scratch_shapes=[pltpu.SemaphoreType.DMA((n,)), pltpu.SemaphoreType.DMA((n,)), pltpu.SemaphoreType.DMA((n,))],
    )(*halves)


def _reduce_scatter(grads):
    x, y, c = _my_place()
    c1 = jnp.reshape(c, (1,)).astype(jnp.int32)
    k1 = jnp.reshape(2 * x + y, (1,)).astype(jnp.int32)
    views = []
    for g in grads:
        rows = g.shape[0] // N_CHIPS
        assert rows % 32 == 0
        views.append(g.reshape(N_CHIPS, 2, rows // 2, g.shape[1]))
    got = _sibling_swap_halves(views)
    pairs = [_pair_sum(v, r, c1) for v, r in zip(views, got)]
    got = _chip_exchange(pairs)
    halves = [_chip_sum(p, r, k1) for p, r in zip(pairs, got)]
    both = _sibling_share(halves)
    return [b.reshape(2 * b.shape[1], b.shape[2]) for b in both]


def _all_reduce_small(pack):
    rows, cols = pack.shape

    def body(p_ref, o_ref, buf, send_sems, recv_sems):
        x, y, c = _my_place()
        me = 4 * x + 2 * y + c
        buf[me] = p_ref[...]
        copies = []
        for f in range(1, 8):
            fx, fy, fc = (f >> 2) & 1, (f >> 1) & 1, f & 1
            to = (1 - x if fx else x, 1 - y if fy else y, 1 - c if fc else c)
            cp = pltpu.make_async_remote_copy(
                src_ref=p_ref, dst_ref=buf.at[me], send_sem=send_sems.at[f - 1], recv_sem=recv_sems.at[f - 1],
                device_id=to, device_id_type=MESH)
            cp.start()
            copies.append(cp)
        for cp in copies:
            cp.wait()
        acc = buf[0]
        for d in range(1, 8):
            acc = acc + buf[d]
        o_ref[...] = acc

    return pl.pallas_call(
        body, name="small_grads_all_reduce",
        out_shape=jax.ShapeDtypeStruct((rows, cols), F32),
        in_specs=[pl.BlockSpec(memory_space=pltpu.VMEM)], out_specs=pl.BlockSpec(memory_space=pltpu.VMEM),
        scratch_shapes=[pltpu.VMEM((8, rows, cols), F32), pltpu.SemaphoreType.DMA((7,)), pltpu.SemaphoreType.DMA((7,))],
    )(pack)


def _load_rows(pairs, sems):
    cps = [pltpu.make_async_copy(src, dst, sems.at[j]) for j, (src, dst) in enumerate(pairs)]
    for cp in cps:
        cp.start()
    for cp in cps:
        cp.wait()


def _ffn_forward(x, g, wg_t, wu_t, wd, name):
    t, d = x.shape
    f = wd.shape[0]
    tm = min(TM_FFN, t)
    chunks = _feature_chunks(f)

    def body(x_ref, g_ref, wg_hbm, wu_hbm, wd_hbm, xo_ref, a_ref, b_ref, wg, wu, wdn, sems):
        @pl.when(pl.program_id(0) == 0)
        def _():
            _load_rows([(wg_hbm, wg), (wu_hbm, wu), (wd_hbm, wdn)], sems)

        xv = x_ref[...]
        r = lax.rsqrt(jnp.mean(xv * xv, axis=-1, keepdims=True) + EPS)
        h = (xv * r * g_ref[...]).astype(BF16)
        acc = jnp.zeros((tm, d), F32)
        for s0, sz in chunks:
            a = _nt(h, wg[s0:s0 + sz, :])
            b = _nt(h, wu[s0:s0 + sz, :])
            a_ref[:, s0:s0 + sz] = a.astype(BF16)
            b_ref[:, s0:s0 + sz] = b.astype(BF16)
            s = (a * _sigmoid(a) * b).astype(BF16)
            acc = acc + _nn(s, wdn[s0:s0 + sz, :])
        xo_ref[...] = xv + 0.5 * acc

    tok = lambda i: (i, 0)
    return pl.pallas_call(
        body, name=name,
        out_shape=[jax.ShapeDtypeStruct((t, d), F32), jax.ShapeDtypeStruct((t, f), BF16), jax.ShapeDtypeStruct((t, f), BF16)],
        grid=(t // tm,),
        in_specs=[pl.BlockSpec((tm, d), tok), pl.BlockSpec((1, d), lambda i: (0, 0)), HBM_SPEC, HBM_SPEC, HBM_SPEC],
        out_specs=[pl.BlockSpec((tm, d), tok), pl.BlockSpec((tm, f), tok), pl.BlockSpec((tm, f), tok)],
        scratch_shapes=[pltpu.VMEM((f, d), BF16), pltpu.VMEM((f, d), BF16), pltpu.VMEM((f, d), BF16), pltpu.SemaphoreType.DMA((3,))],
        compiler_params=pltpu.CompilerParams(dimension_semantics=("arbitrary",), vmem_limit_bytes=VMEM_LIMIT),
    )(x, g, wg_t, wu_t, wd)


def _ffn_backward(dxo, x, g, a, b, wg_t, wu_t, wd, name):
    t, d = x.shape
    f = wd.shape[0]
    tm = min(TM_FFN // 2, t)
    chunks = _feature_chunks(f)

    def body(dxo_ref, x_ref, g_ref, a_ref, b_ref, wg_hbm, wu_hbm, wd_hbm,
             dx_ref, da_ref, db_ref, s_ref, h_ref, do_ref, dg_ref, wg, wu, wdn, sems):
        @pl.when(pl.program_id(0) == 0)
        def _():
            _load_rows([(wg_hbm, wg), (wu_hbm, wu), (wd_hbm, wdn)], sems)
            dg_ref[...] = jnp.zeros_like(dg_ref)

        xv = x_ref[...]
        gv = g_ref[...]
        r = lax.rsqrt(jnp.mean(xv * xv, axis=-1, keepdims=True) + EPS)
        xhat = xv * r
        h_ref[...] = (xhat * gv).astype(BF16)
        dxo_v = dxo_ref[...]
        dout = (0.5 * dxo_v).astype(BF16)
        do_ref[...] = dout
        dh = jnp.zeros((tm, d), F32)
        for s0, sz in chunks:
            ds = _nt(dout, wdn[s0:s0 + sz, :])
            av = a_ref[:, s0:s0 + sz].astype(F32)
            bv = b_ref[:, s0:s0 + sz].astype(F32)
            sig = _sigmoid(av)
            silu = av * sig
            s_ref[:, s0:s0 + sz] = (silu * bv).astype(BF16)
            da = (ds * bv * (sig * (1.0 + av * (1.0 - sig)))).astype(BF16)
            db = (ds * silu).astype(BF16)
            da_ref[:, s0:s0 + sz] = da
            db_ref[:, s0:s0 + sz] = db
            dh = dh + _nn(da, wg[s0:s0 + sz, :]) + _nn(db, wu[s0:s0 + sz, :])
        dg_ref[...] += jnp.sum(dh * xhat, axis=0, keepdims=True)
        dxh = dh * gv
        dx_ref[...] = dxo_v + r * (dxh - xhat * jnp.mean(dxh * xhat, axis=-1, keepdims=True))

    tok = lambda i: (i, 0)
    one = lambda i: (0, 0)
    return pl.pallas_call(
        body, name=name,
        out_shape=[jax.ShapeDtypeStruct((t, d), F32), jax.ShapeDtypeStruct((t, f), BF16), jax.ShapeDtypeStruct((t, f), BF16),
                   jax.ShapeDtypeStruct((t, f), BF16), jax.ShapeDtypeStruct((t, d), BF16), jax.ShapeDtypeStruct((t, d), BF16),
                   jax.ShapeDtypeStruct((1, d), F32)],
        grid=(t // tm,),
        in_specs=[pl.BlockSpec((tm, d), tok), pl.BlockSpec((tm, d), tok), pl.BlockSpec((1, d), one),
                  pl.BlockSpec((tm, f), tok), pl.BlockSpec((tm, f), tok), HBM_SPEC, HBM_SPEC, HBM_SPEC],
        out_specs=[pl.BlockSpec((tm, d), tok), pl.BlockSpec((tm, f), tok), pl.BlockSpec((tm, f), tok), pl.BlockSpec((tm, f), tok),
                   pl.BlockSpec((tm, d), tok), pl.BlockSpec((tm, d), tok), pl.BlockSpec((1, d), one)],
        scratch_shapes=[pltpu.VMEM((f, d), BF16), pltpu.VMEM((f, d), BF16), pltpu.VMEM((f, d), BF16), pltpu.SemaphoreType.DMA((3,))],
        compiler_params=pltpu.CompilerParams(dimension_semantics=("arbitrary",), vmem_limit_bytes=VMEM_LIMIT),
    )(dxo, x, g, a, b, wg_t, wu_t, wd)


def _weight_grad(lhs, rhs, name):
    t, m = lhs.shape
    d = rhs.shape[1]
    tm = min(TM_TN, t)
    bm = m // 2 if (m // 2) % 128 == 0 and m > 1024 else m
    nt = t // tm

    def body(l_ref, r_ref, o_ref, acc):
        i = pl.program_id(1)
        part = _tn(l_ref[...], r_ref[...])

        @pl.when(i == 0)
        def _():
            acc[...] = part

        @pl.when(i > 0)
        def _():
            acc[...] += part

        @pl.when(i == nt - 1)
        def _():
            o_ref[...] = acc[...].astype(BF16)

    return pl.pallas_call(
        body, name=name,
        out_shape=jax.ShapeDtypeStruct((m, d), BF16),
        grid=(m // bm, nt),
        in_specs=[pl.BlockSpec((tm, bm), lambda j, i: (i, j)), pl.BlockSpec((tm, d), lambda j, i: (i, 0))],
        out_specs=pl.BlockSpec((bm, d), lambda j, i: (j, 0)),
        scratch_shapes=[pltpu.VMEM((bm, d), F32)],
        compiler_params=pltpu.CompilerParams(dimension_semantics=("parallel", "arbitrary"), vmem_limit_bytes=VMEM_LIMIT),
    )(lhs, rhs)


def _pool_parts(u_cols, ubuf, cols, w, row, tm):
    ws = u_cols
    for s in range(1, w):
        ws = ws + ubuf[HALO - s:HALO - s + tm, cols]
    cnt = jnp.minimum(row + 1, w).astype(F32)
    return ws / cnt - u_cols, cnt


def _mixer_forward(x, g, win_t, wout_x, conv_w, pool_w, pool_scale):
    t, d = x.shape
    dc = win_t.shape[0] // 4
    gcw = dc // len(POOL_WINDOWS)
    wo_rows = d // N_CHIPS
    wo_stride = wout_x.shape[0] // N_CHIPS
    tm = min(TM_MIX, t)

    def body(x_ref, g_ref, win_hbm, wout_hbm, cw_ref, pw_ref, ps_ref, xo_ref, proj_ref, y_ref,
             win, wout, zbuf, ubuf, sems):
        i = pl.program_id(0)

        @pl.when(i == 0)
        def _():
            pairs = [(win_hbm, win)]
            for k in range(N_CHIPS):
                pairs.append((wout_hbm.at[pl.ds(k * wo_stride, wo_rows), :], wout.at[pl.ds(k * wo_rows, wo_rows), :]))
            _load_rows(pairs, sems)
            zbuf[0:8, :] = jnp.zeros((8, dc), F32)
            ubuf[0:HALO, :] = jnp.zeros((HALO, dc), F32)

        xv = x_ref[...]
        r = lax.rsqrt(jnp.mean(xv * xv, axis=-1, keepdims=True) + EPS)
        h = (xv * r * g_ref[...]).astype(BF16)
        v = _nt(h, win[0:dc, :])
        gb = _nt(h, win[dc:2 * dc, :])
        gc = _nt(h, win[2 * dc:3 * dc, :])
        u = _nt(h, win[3 * dc:4 * dc, :])
        proj_ref[:, 0:dc] = v.astype(BF16)
        proj_ref[:, dc:2 * dc] = gb.astype(BF16)
        proj_ref[:, 2 * dc:3 * dc] = gc.astype(BF16)
        proj_ref[:, 3 * dc:4 * dc] = u.astype(BF16)

        z = gc * v
        zbuf[8:8 + tm, :] = z
        cw = cw_ref[...]
        conv = cw[2:3, :] * z + cw[1:2, :] * zbuf[7:7 + tm, :] + cw[0:1, :] * zbuf[6:6 + tm, :]
        y_ref[:, 0:dc] = (gb * conv).astype(BF16)

        ubuf[HALO:HALO + tm, :] = u
        row = i * tm + lax.broadcasted_iota(jnp.int32, (tm, 1), 0)
        for gi, w in enumerate(POOL_WINDOWS):
            cols = slice(gi * gcw, (gi + 1) * gcw)
            pooled, _ = _pool_parts(u[:, cols], ubuf, cols, w, row, tm)
            yb = _nn(pooled.astype(BF16), pw_ref[gi].astype(BF16)) * ps_ref[:, cols]
            y_ref[:, dc + gi * gcw:dc + (gi + 1) * gcw] = yb.astype(BF16)

        xo_ref[...] = xv + _nn(y_ref[...], wout[...])
        zbuf[0:8, :] = zbuf[tm:tm + 8, :]
        ubuf[0:HALO, :] = ubuf[tm:tm + HALO, :]

    tok = lambda i: (i, 0)
    one = lambda i: (0, 0)
    return pl.pallas_call(
        body, name="mixer_forward",
        out_shape=[jax.ShapeDtypeStruct((t, d), F32), jax.ShapeDtypeStruct((t, 4 * dc), BF16), jax.ShapeDtypeStruct((t, 2 * dc), BF16)],
        grid=(t // tm,),
        in_specs=[pl.BlockSpec((tm, d), tok), pl.BlockSpec((1, d), one), HBM_SPEC, HBM_SPEC,
                  pl.BlockSpec(conv_w.shape, one), pl.BlockSpec(pool_w.shape, lambda i: (0, 0, 0)), pl.BlockSpec((1, dc), one)],
        out_specs=[pl.BlockSpec((tm, d), tok), pl.BlockSpec((tm, 4 * dc), tok), pl.BlockSpec((tm, 2 * dc), tok)],
        scratch_shapes=[pltpu.VMEM((4 * dc, d), BF16), pltpu.VMEM((2 * dc, d), BF16),
                        pltpu.VMEM((tm + 8, dc), F32), pltpu.VMEM((tm + HALO, dc), F32), pltpu.SemaphoreType.DMA((1 + N_CHIPS,))],
        compiler_params=pltpu.CompilerParams(dimension_semantics=("arbitrary",), vmem_limit_bytes=VMEM_LIMIT),
    )(x, g, win_t, wout_x, conv_w, pool_w, pool_scale)


def _mixer_backward(dxo, x, g, proj, win_t, wout_x, conv_w, pool_w, pool_scale):
    t, d = x.shape
    dc = win_t.shape[0] // 4
    ng = len(POOL_WINDOWS)
    gcw = dc // ng
    wo_rows = d // N_CHIPS
    wo_stride = wout_x.shape[0] // N_CHIPS
    tm = min(TM_MIX // 2, t)
    n_tiles = t // tm
    hb = tm // HALO

    def body(dxo_ref, x_ref, g_ref, proj_ref, halo_ref, win_hbm, wout_hbm, cw_ref, pw_ref, ps_ref,
             dx_ref, dproj_ref, h_ref, dxob_ref, dg_ref, dcw_ref, dps_ref, dpw_ref,
             win, wout, zbuf, ubuf, dcbuf, ebuf, sems):
        i = pl.program_id(0)
        tile = n_tiles - 1 - i

        @pl.when(i == 0)
        def _():
            pairs = [(win_hbm, win)]
            for k in range(N_CHIPS):
                pairs.append((wout_hbm.at[pl.ds(k * wo_stride, wo_rows), :], wout.at[pl.ds(k * wo_rows, wo_rows), :]))
            _load_rows(pairs, sems)
            dcbuf[tm:tm + 8, :] = jnp.zeros((8, dc), F32)
            ebuf[tm:tm + HALO, :] = jnp.zeros((HALO, dc), F32)
            dg_ref[...] = jnp.zeros_like(dg_ref)
            dcw_ref[...] = jnp.zeros_like(dcw_ref)
            dps_ref[...] = jnp.zeros_like(dps_ref)
            dpw_ref[...] = jnp.zeros_like(dpw_ref)

        xv = x_ref[...]
        gv = g_ref[...]
        r = lax.rsqrt(jnp.mean(xv * xv, axis=-1, keepdims=True) + EPS)
        xhat = xv * r
        h_ref[...] = (xhat * gv).astype(BF16)
        dxo_v = dxo_ref[...]
        dxo_b = dxo_v.astype(BF16)
        dxob_ref[...] = dxo_b

        v = proj_ref[:, 0:dc].astype(F32)
        gb = proj_ref[:, dc:2 * dc].astype(F32)
        gc = proj_ref[:, 2 * dc:3 * dc].astype(F32)
        u = proj_ref[:, 3 * dc:4 * dc].astype(F32)
        first = jnp.where(tile > 0, 1.0, 0.0)
        zbuf[0:HALO, :] = halo_ref[:, 2 * dc:3 * dc].astype(F32) * halo_ref[:, 0:dc].astype(F32) * first
        ubuf[0:HALO, :] = halo_ref[:, 3 * dc:4 * dc].astype(F32) * first
        z = gc * v
        zbuf[HALO:HALO + tm, :] = z
        ubuf[HALO:HALO + tm, :] = u
        z1 = zbuf[HALO - 1:HALO - 1 + tm, :]
        z2 = zbuf[HALO - 2:HALO - 2 + tm, :]
        cw = cw_ref[...]
        conv = cw[2:3, :] * z + cw[1:2, :] * z1 + cw[0:1, :] * z2

        dy = _nt(dxo_b, wout[...])
        dya = dy[:, 0:dc]
        dgb = dya * conv
        dconv = dya * gb
        dcbuf[0:tm, :] = dconv
        dz = cw[2:3, :] * dconv + cw[1:2, :] * dcbuf[1:1 + tm, :] + cw[0:1, :] * dcbuf[2:2 + tm, :]
        dgc = dz * v
        dv = dz * gc
        dcw_ref[0:1, :] += jnp.sum(dconv * z2, axis=0, keepdims=True)
        dcw_ref[1:2, :] += jnp.sum(dconv * z1, axis=0, keepdims=True)
        dcw_ref[2:3, :] += jnp.sum(dconv * z, axis=0, keepdims=True)

        dproj_ref[:, 0:dc] = dv.astype(BF16)
        dproj_ref[:, dc:2 * dc] = dgb.astype(BF16)
        dproj_ref[:, 2 * dc:3 * dc] = dgc.astype(BF16)

        row = tile * tm + lax.broadcasted_iota(jnp.int32, (tm, 1), 0)
        for gi, w in enumerate(POOL_WINDOWS):
            cols = slice(gi * gcw, (gi + 1) * gcw)
            pooled, cnt = _pool_parts(u[:, cols], ubuf, cols, w, row, tm)
            pooled_b = pooled.astype(BF16)
            pw_b = pw_ref[gi].astype(BF16)
            dyb = dy[:, dc + gi * gcw:dc + (gi + 1) * gcw]
            q = _nn(pooled_b, pw_b)
            dps_ref[:, cols] += jnp.sum(q * dyb, axis=0, keepdims=True)
            dq = (dyb * ps_ref[:, cols]).astype(BF16)
            dpw_ref[gi] += _tn(pooled_b, dq)
            dpooled = _nt(dq, pw_b)
            ebuf[0:tm, cols] = dpooled / cnt
            du = -dpooled
            for s in range(w):
                du = du + ebuf[s:s + tm, cols]
            dproj_ref[:, 3 * dc + gi * gcw:3 * dc + (gi + 1) * gcw] = du.astype(BF16)

        dh = _nn(dproj_ref[...], win[...])
        dg_ref[...] += jnp.sum(dh * xhat, axis=0, keepdims=True)
        dxh = dh * gv
        dx_ref[...] = dxo_v + r * (dxh - xhat * jnp.mean(dxh * xhat, axis=-1, keepdims=True))
        dcbuf[tm:tm + 8, :] = dcbuf[0:8, :]
        ebuf[tm:tm + HALO, :] = ebuf[0:HALO, :]

    tok = lambda i: (n_tiles - 1 - i, 0)
    halo = lambda i: (jnp.maximum((n_tiles - 1 - i) * hb - 1, 0), 0)
    one = lambda i: (0, 0)
    return pl.pallas_call(
        body, name="mixer_backward",
        out_shape=[jax.ShapeDtypeStruct((t, d), F32), jax.ShapeDtypeStruct((t, 4 * dc), BF16), jax.ShapeDtypeStruct((t, d), BF16),
                   jax.ShapeDtypeStruct((t, d), BF16), jax.ShapeDtypeStruct((1, d), F32), jax.ShapeDtypeStruct(conv_w.shape, F32),
                   jax.ShapeDtypeStruct((1, dc), F32), jax.ShapeDtypeStruct(pool_w.shape, F32)],
        grid=(n_tiles,),
        in_specs=[pl.BlockSpec((tm, d), tok), pl.BlockSpec((tm, d), tok), pl.BlockSpec((1, d), one),
                  pl.BlockSpec((tm, 4 * dc), tok), pl.BlockSpec((HALO, 4 * dc), halo), HBM_SPEC, HBM_SPEC,
                  pl.BlockSpec(conv_w.shape, one), pl.BlockSpec(pool_w.shape, lambda i: (0, 0, 0)), pl.BlockSpec((1, dc), one)],
        out_specs=[pl.BlockSpec((tm, d), tok), pl.BlockSpec((tm, 4 * dc), tok), pl.BlockSpec((tm, d), tok), pl.BlockSpec((tm, d), tok),
                   pl.BlockSpec((1, d), one), pl.BlockSpec(conv_w.shape, one), pl.BlockSpec((1, dc), one),
                   pl.BlockSpec(pool_w.shape, lambda i: (0, 0, 0))],
        scratch_shapes=[pltpu.VMEM((4 * dc, d), BF16), pltpu.VMEM((2 * dc, d), BF16),
                        pltpu.VMEM((tm + HALO, dc), F32), pltpu.VMEM((tm + HALO, dc), F32),
                        pltpu.VMEM((tm + 8, dc), F32), pltpu.VMEM((tm + HALO, dc), F32), pltpu.SemaphoreType.DMA((1 + N_CHIPS,))],
        compiler_params=pltpu.CompilerParams(dimension_semantics=("arbitrary",), vmem_limit_bytes=VMEM_LIMIT),
    )(dxo, x, g, proj, proj, win_t, wout_x, conv_w, pool_w, pool_scale)


def _loss_backward(x, g, target):
    t, d = x.shape
    tm = min(TM_EW, t)

    def body(x_ref, g_ref, t_ref, dx_ref, sq_ref, dg_ref):
        @pl.when(pl.program_id(0) == 0)
        def _():
            sq_ref[...] = jnp.zeros_like(sq_ref)
            dg_ref[...] = jnp.zeros_like(dg_ref)

        xv = x_ref[...]
        gv = g_ref[...]
        r = lax.rsqrt(jnp.mean(xv * xv, axis=-1, keepdims=True) + EPS)
        xhat = xv * r
        err = xhat * gv - t_ref[...]
        sq_ref[...] += jnp.sum(err * err, axis=0, keepdims=True)
        dy = err * (1.0 / d)
        dg_ref[...] += jnp.sum(dy * xhat, axis=0, keepdims=True)
        dxh = dy * gv
        dx_ref[...] = r * (dxh - xhat * jnp.mean(dxh * xhat, axis=-1, keepdims=True))

    tok = lambda i: (i, 0)
    one = lambda i: (0, 0)
    return pl.pallas_call(
        body, name="loss_backward",
        out_shape=[jax.ShapeDtypeStruct((t, d), F32), jax.ShapeDtypeStruct((1, d), F32), jax.ShapeDtypeStruct((1, d), F32)],
        grid=(t // tm,),
        in_specs=[pl.BlockSpec((tm, d), tok), pl.BlockSpec((1, d), one), pl.BlockSpec((tm, d), tok)],
        out_specs=[pl.BlockSpec((tm, d), tok), pl.BlockSpec((1, d), one), pl.BlockSpec((1, d), one)],
        compiler_params=pltpu.CompilerParams(dimension_semantics=("arbitrary",)),
    )(x, g, target)


def _adamw(w, grad, m, v, name):
    rows, cols = w.shape
    br = _row_block(rows, 256) if rows >= 8 else rows
    bc1 = 1.0 - ADAM_B1 ** ADAM_STEP
    bc2 = 1.0 - ADAM_B2 ** ADAM_STEP

    def body(w_ref, g_ref, m_ref, v_ref, d_ref, mo_ref, vo_ref):
        gv = g_ref[...]
        m_new = ADAM_B1 * m_ref[...] + (1.0 - ADAM_B1) * gv
        v_new = ADAM_B2 * v_ref[...] + (1.0 - ADAM_B2) * (gv * gv)
        m_hat = m_new / bc1
        v_hat = v_new / bc2
        d_ref[...] = -ADAM_LR * (m_hat / (jnp.sqrt(v_hat) + ADAM_EPS) + ADAM_WD * w_ref[...])
        mo_ref[...] = m_new
        vo_ref[...] = v_new

    blk = pl.BlockSpec((br, cols), lambda i: (i, 0))
    return pl.pallas_call(
        body, name=name,
        out_shape=[jax.ShapeDtypeStruct((rows, cols), F32)] * 3,
        grid=(rows // br,), in_specs=[blk] * 4, out_specs=[blk] * 3,
        compiler_params=pltpu.CompilerParams(dimension_semantics=("parallel",)),
    )(w, grad, m, v)


def _f32_rows_as_bf16(a, rows, cols):
    bits = lax.bitcast_convert_type(a, BF16).reshape(a.shape[0], 2 * a.shape[1])
    return jnp.pad(bits, ((0, rows - bits.shape[0]), (0, cols - bits.shape[1])))


def kernel(x, norm_ffn1, ffn1_w_gate, ffn1_w_up, ffn1_w_down, norm_mix, w_in, conv_w, pool_w, pool_scale, w_out, norm_ffn2, ffn2_w_gate, ffn2_w_up, ffn2_w_down, norm_final, loss_target, m_norm_ffn1, m_ffn1_w_gate, m_ffn1_w_up, m_ffn1_w_down, m_norm_mix, m_w_in, m_conv_w, m_pool_w, m_pool_scale, m_w_out, m_norm_ffn2, m_ffn2_w_gate, m_ffn2_w_up, m_ffn2_w_down, m_norm_final, v_norm_ffn1, v_ffn1_w_gate, v_ffn1_w_up, v_ffn1_w_down, v_norm_mix, v_w_in, v_conv_w, v_pool_w, v_pool_scale, v_w_out, v_norm_ffn2, v_ffn2_w_gate, v_ffn2_w_up, v_ffn2_w_down, v_norm_final):
    weights = dict(norm_ffn1=norm_ffn1, ffn1_w_gate=ffn1_w_gate, ffn1_w_up=ffn1_w_up, ffn1_w_down=ffn1_w_down, norm_mix=norm_mix,
                   w_in=w_in, conv_w=conv_w, pool_w=pool_w, pool_scale=pool_scale, w_out=w_out, norm_ffn2=norm_ffn2,
                   ffn2_w_gate=ffn2_w_gate, ffn2_w_up=ffn2_w_up, ffn2_w_down=ffn2_w_down, norm_final=norm_final)
    first_m = dict(norm_ffn1=m_norm_ffn1, ffn1_w_gate=m_ffn1_w_gate, ffn1_w_up=m_ffn1_w_up, ffn1_w_down=m_ffn1_w_down,
                   norm_mix=m_norm_mix, w_in=m_w_in, conv_w=m_conv_w, pool_w=m_pool_w, pool_scale=m_pool_scale, w_out=m_w_out,
                   norm_ffn2=m_norm_ffn2, ffn2_w_gate=m_ffn2_w_gate, ffn2_w_up=m_ffn2_w_up, ffn2_w_down=m_ffn2_w_down,
                   norm_final=m_norm_final)
    second_m = dict(norm_ffn1=v_norm_ffn1, ffn1_w_gate=v_ffn1_w_gate, ffn1_w_up=v_ffn1_w_up, ffn1_w_down=v_ffn1_w_down,
                    norm_mix=v_norm_mix, w_in=v_w_in, conv_w=v_conv_w, pool_w=v_pool_w, pool_scale=v_pool_scale, w_out=v_w_out,
                    norm_ffn2=v_norm_ffn2, ffn2_w_gate=v_ffn2_w_gate, ffn2_w_up=v_ffn2_w_up, ffn2_w_down=v_ffn2_w_down,
                    norm_final=v_norm_final)
    names = list(weights)

    xs = x[0]
    tgt = loss_target[0]
    t, d = xs.shape
    dc = pool_scale.shape[1]
    cx, cy, cc = _my_place()
    chip = 2 * cx + cy

    conv_rows = 32
    wout_x = jnp.concatenate([w_out[0].astype(BF16), _f32_rows_as_bf16(conv_w[0], conv_rows, d)], axis=0)
    shards = [ffn1_w_gate[0].T.astype(BF16), ffn1_w_up[0].T.astype(BF16), ffn1_w_down[0].astype(BF16),
              w_in[0].T.astype(BF16), wout_x,
              ffn2_w_gate[0].T.astype(BF16), ffn2_w_up[0].T.astype(BF16), ffn2_w_down[0].astype(BF16)]
    wg1, wu1, wd1, win_t, wout_g, wg2, wu2, wd2 = _all_gather_shards(shards)
    wo_rows = w_out.shape[1]
    cshard = conv_w.shape[2]
    conv_bits = wout_g.reshape(N_CHIPS, wo_rows + conv_rows, d)[:, wo_rows:wo_rows + conv_w.shape[1], :2 * cshard]
    conv_full = lax.bitcast_convert_type(conv_bits.reshape(N_CHIPS, conv_w.shape[1], cshard, 2), F32)
    conv_full = jnp.transpose(conv_full, (1, 0, 2)).reshape(conv_w.shape[1], N_CHIPS * cshard)

    g1, gm, g2 = norm_ffn1, norm_mix, norm_ffn2
    gf = norm_final.reshape(1, d)
    pw = pool_w[0]

    x1, a1, b1 = _ffn_forward(xs, g1, wg1, wu1, wd1, "ffn1_forward")
    x2, proj, ymix = _mixer_forward(x1, gm, win_t, wout_g, conv_full, pw, pool_scale)
    x3, a2, b2 = _ffn_forward(x2, g2, wg2, wu2, wd2, "ffn2_forward")

    dx3, sq_cols, dgf = _loss_backward(x3, gf, tgt)
    loss = lax.psum(jnp.sum(sq_cols) * (0.5 / d), ("x", "y", "c"))

    dx2, da2, db2, s2, h3, do2, dg2 = _ffn_backward(dx3, x2, g2, a2, b2, wg2, wu2, wd2, "ffn2_backward")
    gwg2 = _weight_grad(da2, h3, "ffn2_gate_grad")
    gwu2 = _weight_grad(db2, h3, "ffn2_up_grad")
    gwd2 = _weight_grad(s2, do2, "ffn2_down_grad")

    dx1, dproj, h2, dx2b, dgm, dcw, dps, dpw = _mixer_backward(dx2, x1, gm, proj, win_t, wout_g, conv_full, pw, pool_scale)
    gwin = _weight_grad(dproj, h2, "w_in_grad")
    gwout = _weight_grad(ymix, dx2b, "w_out_grad")

    dx0, da1, db1, s1, h1, do1, dg1 = _ffn_backward(dx1, xs, g1, a1, b1, wg1, wu1, wd1, "ffn1_backward")
    gwg1 = _weight_grad(da1, h1, "ffn1_gate_grad")
    gwu1 = _weight_grad(db1, h1, "ffn1_up_grad")
    gwd1 = _weight_grad(s1, do1, "ffn1_down_grad")

    rwg1, rwu1, rwd1, rwin, rwout, rwg2, rwu2, rwd2 = _reduce_scatter([gwg1, gwu1, gwd1, gwin, gwout, gwg2, gwu2, gwd2])

    npw = pw.size // d
    pack = jnp.concatenate([
        dg1, dgm, dg2, dgf,
        jnp.pad(dps, ((0, 0), (0, d - dc))), jnp.pad(dcw, ((0, 0), (0, d - dc))),
        jnp.zeros((8 - (5 + dcw.shape[0]) % 8, d), F32) if (5 + dcw.shape[0]) % 8 else jnp.zeros((0, d), F32),
        dpw.reshape(npw, d)], axis=0)
    small = _all_reduce_small(pack)
    base = pack.shape[0] - npw
    grads = {
        "norm_ffn1": small[0:1], "norm_mix": small[1:2], "norm_ffn2": small[2:3], "norm_final": small[3],
        "pool_scale": small[4:5, :dc],
        "conv_w": lax.dynamic_slice_in_dim(small[5:5 + dcw.shape[0], :dc], chip * cshard, cshard, axis=1)[None],
        "pool_w": small[base:].reshape(pool_w.shape),
        "ffn1_w_gate": rwg1.T[None], "ffn1_w_up": rwu1.T[None], "ffn1_w_down": rwd1[None],
        "w_in": rwin.T[None], "w_out": rwout[None],
        "ffn2_w_gate": rwg2.T[None], "ffn2_w_up": rwu2.T[None], "ffn2_w_down": rwd2[None],
    }

    deltas, new_m, new_v = {}, {}, {}
    for n in names:
        w = weights[n]
        shape = w.shape
        as2d = (lambda a: a.reshape(-1, shape[-1]))
        dl, mo, vo = _adamw(as2d(w), as2d(grads[n]), as2d(first_m[n]), as2d(second_m[n]), "adamw_" + n)
        deltas[n], new_m[n], new_v[n] = dl.reshape(shape), mo.reshape(shape), vo.reshape(shape)
        grads[n] = grads[n].reshape(shape)

    return (loss, dx0[None], *[grads[n] for n in names], *[deltas[n] for n in names],
            *[new_m[n] for n in names], *[new_v[n] for n in names])
```

```python
import functools

import jax
import jax.numpy as jnp
from jax import lax
from jax.experimental import pallas as pl
from jax.experimental.pallas import tpu as pltpu

F32 = jnp.float32
BF16 = jnp.bfloat16
MESH = pl.DeviceIdType.MESH

EPS = 1e-6
POOL_WINDOWS = (2, 4, 8, 16)
ADAM_LR = 0.001
ADAM_B1 = 0.9
ADAM_B2 = 0.999
ADAM_EPS = 1e-08
ADAM_WD = 0.01
ADAM_STEP = 10

N_CHIPS = 4
MXU_COLS_V7X = 256
VMEM_LIMIT = 56 * 1024 * 1024
TM_FFN = 512
TM_MIX = 512
TM_TN = 1024
TM_EW = 512
HALO = 16


def _nt(a, b):
    return lax.dot_general(a, b, (((1,), (1,)), ((), ())), preferred_element_type=F32)


def _tn(a, b):
    return lax.dot_general(a, b, (((0,), (0,)), ((), ())), preferred_element_type=F32)


def _nn(a, b):
    return jnp.dot(a, b, preferred_element_type=F32)


def _sigmoid(a):
    return 1.0 / (1.0 + jnp.exp(-a))


def _feature_chunks(n):
    assert n % MXU_COLS_V7X == 0
    tiles = n // MXU_COLS_V7X
    first = (tiles + 1) // 2
    sizes = [first * MXU_COLS_V7X, (tiles - first) * MXU_COLS_V7X]
    out, s0 = [], 0
    for sz in sizes:
        if sz:
            out.append((s0, sz))
            s0 += sz
    return out


def _row_block(rows, cap):
    best = 8
    for b in range(8, min(rows, cap) + 1, 8):
        if rows % b == 0:
            best = b
    assert rows % best == 0
    return best


def _my_place():
    return lax.axis_index("x"), lax.axis_index("y"), lax.axis_index("c")


def _other_chips(x, y):
    return [(1 - x, y), (x, 1 - y), (1 - x, 1 - y)]


HBM_SPEC = pl.BlockSpec(memory_space=pltpu.HBM)


class _Cargo:
    def __init__(self, operands, out_shapes, n_sems, start, finish):
        self.operands, self.out_shapes, self.n_sems, self.start, self.finish = list(operands), list(out_shapes), n_sems, start, finish


def _launch(body, *, name, grid, in_specs, out_specs, out_shape, scratch_shapes, args, cargo=None):
    params = pltpu.CompilerParams(dimension_semantics=("arbitrary",) * len(grid), vmem_limit_bytes=VMEM_LIMIT)
    if cargo is None:
        outs = pl.pallas_call(body, name=name, grid=grid, in_specs=list(in_specs), out_specs=list(out_specs),
                              out_shape=list(out_shape), scratch_shapes=list(scratch_shapes), compiler_params=params)(*args)
        return list(outs), []
    counts = [len(in_specs), len(cargo.operands), len(out_shape), len(cargo.out_shapes), len(scratch_shapes), 2]

    def carrying(*refs):
        groups, pos = [], 0
        for k in counts:
            groups.append(refs[pos:pos + k])
            pos += k
        ins, c_ins, outs, c_outs, scratch, sems = groups
        ids = [pl.program_id(ax) for ax in range(len(grid))]
        first = functools.reduce(jnp.logical_and, [i == 0 for i in ids])
        last = functools.reduce(jnp.logical_and, [i == g - 1 for i, g in zip(ids, grid)])

        @pl.when(first)
        def _():
            cargo.start(c_ins, c_outs, *sems)

        body(*ins, *outs, *scratch)

        @pl.when(last)
        def _():
            cargo.finish(c_ins, c_outs, *sems)

    sem = pltpu.SemaphoreType.DMA((cargo.n_sems,))
    outs = pl.pallas_call(
        carrying, name=name, grid=grid,
        in_specs=list(in_specs) + [HBM_SPEC] * counts[1], out_specs=list(out_specs) + [HBM_SPEC] * counts[3],
        out_shape=list(out_shape) + cargo.out_shapes, scratch_shapes=list(scratch_shapes) + [sem, sem],
        compiler_params=params)(*args, *cargo.operands)
    return list(outs[:counts[2]]), list(outs[counts[2]:])


def _run_cargo(cargo, name):
    n_in, n_out = len(cargo.operands), len(cargo.out_shapes)

    def body(*refs):
        c_ins, c_outs, sems = refs[:n_in], refs[n_in:n_in + n_out], refs[n_in + n_out:]
        cargo.start(c_ins, c_outs, *sems)
        cargo.finish(c_ins, c_outs, *sems)

    sem = pltpu.SemaphoreType.DMA((cargo.n_sems,))
    return list(pl.pallas_call(body, name=name, out_shape=cargo.out_shapes, in_specs=[HBM_SPEC] * n_in,
                               out_specs=[HBM_SPEC] * n_out, scratch_shapes=[sem, sem])(*cargo.operands))


def _gather_cargo(shards):
    n = len(shards)
    for s in shards:
        assert s.shape[0] % 32 == 0

    def steps(ins, outs, send_sems, recv_sems):
        x, y, c = _my_place()
        sibling = (x, y, 1 - c)
        chips = _other_chips(x, y)
        mine = 2 * x + y

        def rows_of(a, chip_index, half):
            rps = shards[a].shape[0]
            hr = rps // 2
            return outs[a].at[pl.ds(pl.multiple_of(chip_index * rps + half * hr, 16), hr), :]

        def remote(a, slot, src, dst, to):
            return pltpu.make_async_remote_copy(
                src_ref=src, dst_ref=dst, send_sem=send_sems.at[a * 7 + slot], recv_sem=recv_sems.at[a * 7 + slot],
                device_id=to, device_id_type=MESH)

        def own_copy(a):
            rps = shards[a].shape[0]
            return remote(a, 6, ins[a], outs[a].at[pl.ds(pl.multiple_of(mine * rps, 16), rps), :], sibling)

        def my_half(a):
            hr = shards[a].shape[0] // 2
            return ins[a].at[pl.ds(pl.multiple_of(c * hr, 16), hr), :]

        def start():
            for a in range(n):
                own_copy(a).start()
                for j, chip in enumerate(chips):
                    remote(a, j, my_half(a), rows_of(a, mine, c), (*chip, c)).start()

        def finish():
            for a in range(n):
                for j, chip in enumerate(chips):
                    landed = rows_of(a, 2 * chip[0] + chip[1], c)
                    remote(a, j, landed, landed, (*chip, c)).wait_recv()
                    remote(a, 3 + j, landed, landed, sibling).start()
            for a in range(n):
                for j, chip in enumerate(chips):
                    from_sibling = rows_of(a, 2 * chip[0] + chip[1], 1 - c)
                    remote(a, 3 + j, from_sibling, from_sibling, sibling).wait_recv()
            for a in range(n):
                for j, chip in enumerate(chips):
                    remote(a, j, my_half(a), rows_of(a, mine, c), (*chip, c)).wait_send()
                    landed = rows_of(a, 2 * chip[0] + chip[1], c)
                    remote(a, 3 + j, landed, landed, sibling).wait_send()
                own_copy(a).wait()

        return start, finish

    return _Cargo(shards, [jax.ShapeDtypeStruct((N_CHIPS * s.shape[0], s.shape[1]), s.dtype) for s in shards], 7 * n,
                  lambda *r: steps(*r)[0](), lambda *r: steps(*r)[1]())


def _exchange_cargo(pairs):
    n = len(pairs)

    def copies(ins, outs, send_sems, recv_sems):
        x, y, c = _my_place()
        return [pltpu.make_async_remote_copy(
            src_ref=ins[a].at[2 * chip[0] + chip[1]], dst_ref=outs[a].at[j],
            send_sem=send_sems.at[3 * a + j], recv_sem=recv_sems.at[3 * a + j], device_id=(*chip, c), device_id_type=MESH)
            for a in range(n) for j, chip in enumerate(_other_chips(x, y))]

    def start(*r):
        for cp in copies(*r):
            cp.start()

    def finish(*r):
        for cp in copies(*r):
            cp.wait()

    return _Cargo(pairs, [jax.ShapeDtypeStruct((3,) + p.shape[1:], p.dtype) for p in pairs], 3 * n, start, finish)


def _sibling_swap_halves(grads, tag):
    n = len(grads)

    def body(*refs):
        ins, outs = refs[:n], refs[n:2 * n]
        send_sems, recv_sems = refs[2 * n:]
        x, y, c = _my_place()
        copies = []
        for a in range(n):
            cp = pltpu.make_async_remote_copy(
                src_ref=ins[a].at[:, 1 - c], dst_ref=outs[a], send_sem=send_sems.at[a], recv_sem=recv_sems.at[a],
                device_id=(x, y, 1 - c), device_id_type=MESH)
            cp.start()
            copies.append(cp)
        for cp in copies:
            cp.wait()

    return pl.pallas_call(
        body, name="grad_swap_sibling_" + tag,
        out_shape=[jax.ShapeDtypeStruct((N_CHIPS,) + g.shape[2:], g.dtype) for g in grads],
        in_specs=[HBM_SPEC] * n, out_specs=[HBM_SPEC] * n,
        scratch_shapes=[pltpu.SemaphoreType.DMA((n,)), pltpu.SemaphoreType.DMA((n,))],
    )(*grads)


def _pair_sum(grad, got, c, tag):
    _, _, hr, cols = grad.shape
    br = _row_block(hr, 256)

    def body(c_ref, g_ref, r_ref, o_ref):
        o_ref[...] = (g_ref[...].astype(F32) + r_ref[...].astype(F32)).astype(BF16)

    return pl.pallas_call(
        body, name="grad_pair_sum_" + tag,
        out_shape=jax.ShapeDtypeStruct((N_CHIPS, hr, cols), BF16),
        grid_spec=pltpu.PrefetchScalarGridSpec(
            num_scalar_prefetch=1, grid=(N_CHIPS, hr // br),
            in_specs=[pl.BlockSpec((None, None, br, cols), lambda j, r, c_ref: (j, c_ref[0], r, 0)),
                      pl.BlockSpec((None, br, cols), lambda j, r, c_ref: (j, r, 0))],
            out_specs=pl.BlockSpec((None, br, cols), lambda j, r, c_ref: (j, r, 0))),
        compiler_params=pltpu.CompilerParams(dimension_semantics=("parallel", "parallel")),
    )(c, grad, got)


def _pair_sums(grads, tag):
    c1 = jnp.reshape(lax.axis_index("c"), (1,)).astype(jnp.int32)
    views = []
    for g in grads:
        rows = g.shape[0] // N_CHIPS
        assert rows % 32 == 0
        views.append(g.reshape(N_CHIPS, 2, rows // 2, g.shape[1]))
    got = _sibling_swap_halves(views, tag)
    return [_pair_sum(v, r, c1, tag + str(i)) for i, (v, r) in enumerate(zip(views, got))]


def _chip_sum(pair, got, place, tag):
    _, hr, cols = pair.shape
    br = _row_block(hr, 256)

    def body(k_ref, p_ref, r_ref, o_ref):
        acc = p_ref[...].astype(F32)
        for j in range(3):
            acc = acc + r_ref[j].astype(F32)
        o_ref[...] = acc

    return pl.pallas_call(
        body, name="grad_chip_sum_" + tag,
        out_shape=jax.ShapeDtypeStruct((2, hr, cols), F32),
        grid_spec=pltpu.PrefetchScalarGridSpec(
            num_scalar_prefetch=1, grid=(hr // br,),
            in_specs=[pl.BlockSpec((None, br, cols), lambda r, k_ref: (k_ref[0], r, 0)),
                      pl.BlockSpec((3, br, cols), lambda r, k_ref: (0, r, 0))],
            out_specs=pl.BlockSpec((None, br, cols), lambda r, k_ref: (k_ref[1], r, 0))),
        compiler_params=pltpu.CompilerParams(dimension_semantics=("parallel",)),
    )(place, pair, got)


def _sibling_share(halves):
    n = len(halves)

    def body(*refs):
        outs = refs[n:2 * n]
        send_sems, recv_sems = refs[2 * n:]
        x, y, c = _my_place()
        copies = []
        for a in range(n):
            cp = pltpu.make_async_remote_copy(
                src_ref=outs[a].at[c], dst_ref=outs[a].at[c], send_sem=send_sems.at[a], recv_sem=recv_sems.at[a],
                device_id=(x, y, 1 - c), device_id_type=MESH)
            cp.start()
            copies.append(cp)
        for cp in copies:
            cp.wait()

    return pl.pallas_call(
        body, name="grad_share_sibling",
        out_shape=[jax.ShapeDtypeStruct(h.shape, h.dtype) for h in halves],
        in_specs=[HBM_SPEC] * n, out_specs=[HBM_SPEC] * n,
        input_output_aliases={a: a for a in range(n)},
        scratch_shapes=[pltpu.SemaphoreType.DMA((n,)), pltpu.SemaphoreType.DMA((n,))],
    )(*halves)


def _all_reduce_small(pack):
    rows, cols = pack.shape

    def body(p_ref, o_ref, buf, send_sems, recv_sems):
        x, y, c = _my_place()
        me = 4 * x + 2 * y + c
        buf[me] = p_ref[...]
        copies = []
        for f in range(1, 8):
            fx, fy, fc = (f >> 2) & 1, (f >> 1) & 1, f & 1
            to = (1 - x if fx else x, 1 - y if fy else y, 1 - c if fc else c)
            cp = pltpu.make_async_remote_copy(
                src_ref=p_ref, dst_ref=buf.at[me], send_sem=send_sems.at[f - 1], recv_sem=recv_sems.at[f - 1],
                device_id=to, device_id_type=MESH)
            cp.start()
            copies.append(cp)
        for cp in copies:
            cp.wait()
        acc = buf[0]
        for d in range(1, 8):
            acc = acc + buf[d]
        o_ref[...] = acc

    return pl.pallas_call(
        body, name="small_grads_all_reduce",
        out_shape=jax.ShapeDtypeStruct((rows, cols), F32),
        in_specs=[pl.BlockSpec(memory_space=pltpu.VMEM)], out_specs=pl.BlockSpec(memory_space=pltpu.VMEM),
        scratch_shapes=[pltpu.VMEM((8, rows, cols), F32), pltpu.SemaphoreType.DMA((7,)), pltpu.SemaphoreType.DMA((7,))],
    )(pack)


def _load_rows(pairs, sems):
    cps = [pltpu.make_async_copy(src, dst, sems.at[j]) for j, (src, dst) in enumerate(pairs)]
    for cp in cps:
        cp.start()
    for cp in cps:
        cp.wait()


def _ffn_forward(x, g, wg_t, wu_t, wd, name, cargo=None):
    t, d = x.shape
    f = wd.shape[0]
    tm = min(TM_FFN, t)
    chunks = _feature_chunks(f)

    def body(x_ref, g_ref, wg_hbm, wu_hbm, wd_hbm, xo_ref, a_ref, b_ref, wg, wu, wdn, sems):
        @pl.when(pl.program_id(0) == 0)
        def _():
            _load_rows([(wg_hbm, wg), (wu_hbm, wu), (wd_hbm, wdn)], sems)

        xv = x_ref[...]
        r = lax.rsqrt(jnp.mean(xv * xv, axis=-1, keepdims=True) + EPS)
        h = (xv * r * g_ref[...]).astype(BF16)
        acc = jnp.zeros((tm, d), F32)
        for s0, sz in chunks:
            a = _nt(h, wg[s0:s0 + sz, :])
            b = _nt(h, wu[s0:s0 + sz, :])
            a_ref[:, s0:s0 + sz] = a.astype(BF16)
            b_ref[:, s0:s0 + sz] = b.astype(BF16)
            s = (a * _sigmoid(a) * b).astype(BF16)
            acc = acc + _nn(s, wdn[s0:s0 + sz, :])
        xo_ref[...] = xv + 0.5 * acc

    tok = lambda i: (i, 0)
    return _launch(
        body, name=name, grid=(t // tm,),
        in_specs=[pl.BlockSpec((tm, d), tok), pl.BlockSpec((1, d), lambda i: (0, 0)), HBM_SPEC, HBM_SPEC, HBM_SPEC],
        out_specs=[pl.BlockSpec((tm, d), tok), pl.BlockSpec((tm, f), tok), pl.BlockSpec((tm, f), tok)],
        out_shape=[jax.ShapeDtypeStruct((t, d), F32), jax.ShapeDtypeStruct((t, f), BF16), jax.ShapeDtypeStruct((t, f), BF16)],
        scratch_shapes=[pltpu.VMEM((f, d), BF16), pltpu.VMEM((f, d), BF16), pltpu.VMEM((f, d), BF16), pltpu.SemaphoreType.DMA((3,))],
        args=(x, g, wg_t, wu_t, wd), cargo=cargo)


def _ffn_backward(dxo, x, g, a, b, wg_t, wu_t, wd, name, cargo=None):
    t, d = x.shape
    f = wd.shape[0]
    tm = min(TM_FFN // 2, t)
    chunks = _feature_chunks(f)

    def body(dxo_ref, x_ref, g_ref, a_ref, b_ref, wg_hbm, wu_hbm, wd_hbm,
             dx_ref, da_ref, db_ref, s_ref, h_ref, do_ref, dg_ref, wg, wu, wdn, sems):
        @pl.when(pl.program_id(0) == 0)
        def _():
            _load_rows([(wg_hbm, wg), (wu_hbm, wu), (wd_hbm, wdn)], sems)
            dg_ref[...] = jnp.zeros_like(dg_ref)

        xv = x_ref[...]
        gv = g_ref[...]
        r = lax.rsqrt(jnp.mean(xv * xv, axis=-1, keepdims=True) + EPS)
        xhat = xv * r
        h_ref[...] = (xhat * gv).astype(BF16)
        dxo_v = dxo_ref[...]
        dout = (0.5 * dxo_v).astype(BF16)
        do_ref[...] = dout
        dh = jnp.zeros((tm, d), F32)
        for s0, sz in chunks:
            ds = _nt(dout, wdn[s0:s0 + sz, :])
            av = a_ref[:, s0:s0 + sz].astype(F32)
            bv = b_ref[:, s0:s0 + sz].astype(F32)
            sig = _sigmoid(av)
            silu = av * sig
            s_ref[:, s0:s0 + sz] = (silu * bv).astype(BF16)
            da = (ds * bv * (sig * (1.0 + av * (1.0 - sig)))).astype(BF16)
            db = (ds * silu).astype(BF16)
            da_ref[:, s0:s0 + sz] = da
            db_ref[:, s0:s0 + sz] = db
            dh = dh + _nn(da, wg[s0:s0 + sz, :]) + _nn(db, wu[s0:s0 + sz, :])
        dg_ref[...] += jnp.sum(dh * xhat, axis=0, keepdims=True)
        dxh = dh * gv
        dx_ref[...] = dxo_v + r * (dxh - xhat * jnp.mean(dxh * xhat, axis=-1, keepdims=True))

    tok = lambda i: (i, 0)
    one = lambda i: (0, 0)
    return _launch(
        body, name=name, grid=(t // tm,),
        in_specs=[pl.BlockSpec((tm, d), tok), pl.BlockSpec((tm, d), tok), pl.BlockSpec((1, d), one),
                  pl.BlockSpec((tm, f), tok), pl.BlockSpec((tm, f), tok), HBM_SPEC, HBM_SPEC, HBM_SPEC],
        out_specs=[pl.BlockSpec((tm, d), tok), pl.BlockSpec((tm, f), tok), pl.BlockSpec((tm, f), tok), pl.BlockSpec((tm, f), tok),
                   pl.BlockSpec((tm, d), tok), pl.BlockSpec((tm, d), tok), pl.BlockSpec((1, d), one)],
        out_shape=[jax.ShapeDtypeStruct((t, d), F32), jax.ShapeDtypeStruct((t, f), BF16), jax.ShapeDtypeStruct((t, f), BF16),
                   jax.ShapeDtypeStruct((t, f), BF16), jax.ShapeDtypeStruct((t, d), BF16), jax.ShapeDtypeStruct((t, d), BF16),
                   jax.ShapeDtypeStruct((1, d), F32)],
        scratch_shapes=[pltpu.VMEM((f, d), BF16), pltpu.VMEM((f, d), BF16), pltpu.VMEM((f, d), BF16), pltpu.SemaphoreType.DMA((3,))],
        args=(dxo, x, g, a, b, wg_t, wu_t, wd), cargo=cargo)


def _weight_grad(lhs, rhs, name, cargo=None):
    t, m = lhs.shape
    d = rhs.shape[1]
    tm = min(TM_TN, t)
    bm = m // 2 if (m // 2) % 128 == 0 and m > 1024 else m
    nt = t // tm

    def body(l_ref, r_ref, o_ref, acc):
        i = pl.program_id(1)
        part = _tn(l_ref[...], r_ref[...])

        @pl.when(i == 0)
        def _():
            acc[...] = part

        @pl.when(i > 0)
        def _():
            acc[...] += part

        @pl.when(i == nt - 1)
        def _():
            o_ref[...] = acc[...].astype(BF16)

    outs, carried = _launch(
        body, name=name, grid=(m // bm, nt),
        in_specs=[pl.BlockSpec((tm, bm), lambda j, i: (i, j)), pl.BlockSpec((tm, d), lambda j, i: (i, 0))],
        out_specs=[pl.BlockSpec((bm, d), lambda j, i: (j, 0))],
        out_shape=[jax.ShapeDtypeStruct((m, d), BF16)],
        scratch_shapes=[pltpu.VMEM((bm, d), F32)],
        args=(lhs, rhs), cargo=cargo)
    return outs[0], carried


def _pool_parts(u_cols, ubuf, cols, w, row, tm):
    ws = u_cols
    for s in range(1, w):
        ws = ws + ubuf[HALO - s:HALO - s + tm, cols]
    cnt = jnp.minimum(row + 1, w).astype(F32)
    return ws / cnt - u_cols, cnt


def _mixer_forward(x, g, win_t, wout_x, conv_w, pool_w, pool_scale, cargo=None):
    t, d = x.shape
    dc = win_t.shape[0] // 4
    gcw = dc // len(POOL_WINDOWS)
    wo_rows = d // N_CHIPS
    wo_stride = wout_x.shape[0] // N_CHIPS
    tm = min(TM_MIX, t)

    def body(x_ref, g_ref, win_hbm, wout_hbm, cw_ref, pw_ref, ps_ref, xo_ref, proj_ref, y_ref,
             win, wout, zbuf, ubuf, sems):
        i = pl.program_id(0)

        @pl.when(i == 0)
        def _():
            pairs = [(win_hbm, win)]
            for k in range(N_CHIPS):
                pairs.append((wout_hbm.at[pl.ds(k * wo_stride, wo_rows), :], wout.at[pl.ds(k * wo_rows, wo_rows), :]))
            _load_rows(pairs, sems)
            zbuf[0:8, :] = jnp.zeros((8, dc), F32)
            ubuf[0:HALO, :] = jnp.zeros((HALO, dc), F32)

        xv = x_ref[...]
        r = lax.rsqrt(jnp.mean(xv * xv, axis=-1, keepdims=True) + EPS)
        h = (xv * r * g_ref[...]).astype(BF16)
        v = _nt(h, win[0:dc, :])
        gb = _nt(h, win[dc:2 * dc, :])
        gc = _nt(h, win[2 * dc:3 * dc, :])
        u = _nt(h, win[3 * dc:4 * dc, :])
        proj_ref[:, 0:dc] = v.astype(BF16)
        proj_ref[:, dc:2 * dc] = gb.astype(BF16)
        proj_ref[:, 2 * dc:3 * dc] = gc.astype(BF16)
        proj_ref[:, 3 * dc:4 * dc] = u.astype(BF16)

        z = gc * v
        zbuf[8:8 + tm, :] = z
        cw = cw_ref[...]
        conv = cw[2:3, :] * z + cw[1:2, :] * zbuf[7:7 + tm, :] + cw[0:1, :] * zbuf[6:6 + tm, :]
        y_ref[:, 0:dc] = (gb * conv).astype(BF16)

        ubuf[HALO:HALO + tm, :] = u
        row = i * tm + lax.broadcasted_iota(jnp.int32, (tm, 1), 0)
        for gi, w in enumerate(POOL_WINDOWS):
            cols = slice(gi * gcw, (gi + 1) * gcw)
            pooled, _ = _pool_parts(u[:, cols], ubuf, cols, w, row, tm)
            yb = _nn(pooled.astype(BF16), pw_ref[gi].astype(BF16)) * ps_ref[:, cols]
            y_ref[:, dc + gi * gcw:dc + (gi + 1) * gcw] = yb.astype(BF16)

        xo_ref[...] = xv + _nn(y_ref[...], wout[...])
        zbuf[0:8, :] = zbuf[tm:tm + 8, :]
        ubuf[0:HALO, :] = ubuf[tm:tm + HALO, :]

    tok = lambda i: (i, 0)
    one = lambda i: (0, 0)
    return _launch(
        body, name="mixer_forward", grid=(t // tm,),
        in_specs=[pl.BlockSpec((tm, d), tok), pl.BlockSpec((1, d), one), HBM_SPEC, HBM_SPEC,
                  pl.BlockSpec(conv_w.shape, one), pl.BlockSpec(pool_w.shape, lambda i: (0, 0, 0)), pl.BlockSpec((1, dc), one)],
        out_specs=[pl.BlockSpec((tm, d), tok), pl.BlockSpec((tm, 4 * dc), tok), pl.BlockSpec((tm, 2 * dc), tok)],
        out_shape=[jax.ShapeDtypeStruct((t, d), F32), jax.ShapeDtypeStruct((t, 4 * dc), BF16), jax.ShapeDtypeStruct((t, 2 * dc), BF16)],
        scratch_shapes=[pltpu.VMEM((4 * dc, d), BF16), pltpu.VMEM((2 * dc, d), BF16),
                        pltpu.VMEM((tm + 8, dc), F32), pltpu.VMEM((tm + HALO, dc), F32), pltpu.SemaphoreType.DMA((1 + N_CHIPS,))],
        args=(x, g, win_t, wout_x, conv_w, pool_w, pool_scale), cargo=cargo)


def _mixer_backward(dxo, x, g, proj, win_t, wout_x, conv_w, pool_w, pool_scale, cargo=None):
    t, d = x.shape
    dc = win_t.shape[0] // 4
    ng = len(POOL_WINDOWS)
    gcw = dc // ng
    wo_rows = d // N_CHIPS
    wo_stride = wout_x.shape[0] // N_CHIPS
    tm = min(TM_MIX // 2, t)
    n_tiles = t // tm
    hb = tm // HALO

    def body(dxo_ref, x_ref, g_ref, proj_ref, halo_ref, win_hbm, wout_hbm, cw_ref, pw_ref, ps_ref,
             dx_ref, dproj_ref, h_ref, dxob_ref, dg_ref, dcw_ref, dps_ref, dpw_ref,
             win, wout, zbuf, ubuf, dcbuf, ebuf, sems):
        i = pl.program_id(0)
        tile = n_tiles - 1 - i

        @pl.when(i == 0)
        def _():
            pairs = [(win_hbm, win)]
            for k in range(N_CHIPS):
                pairs.append((wout_hbm.at[pl.ds(k * wo_stride, wo_rows), :], wout.at[pl.ds(k * wo_rows, wo_rows), :]))
            _load_rows(pairs, sems)
            dcbuf[tm:tm + 8, :] = jnp.zeros((8, dc), F32)
            ebuf[tm:tm + HALO, :] = jnp.zeros((HALO, dc), F32)
            dg_ref[...] = jnp.zeros_like(dg_ref)
            dcw_ref[...] = jnp.zeros_like(dcw_ref)
            dps_ref[...] = jnp.zeros_like(dps_ref)
            dpw_ref[...] = jnp.zeros_like(dpw_ref)

        xv = x_ref[...]
        gv = g_ref[...]
        r = lax.rsqrt(jnp.mean(xv * xv, axis=-1, keepdims=True) + EPS)
        xhat = xv * r
        h_ref[...] = (xhat * gv).astype(BF16)
        dxo_v = dxo_ref[...]
        dxo_b = dxo_v.astype(BF16)
        dxob_ref[...] = dxo_b

        v = proj_ref[:, 0:dc].astype(F32)
        gb = proj_ref[:, dc:2 * dc].astype(F32)
        gc = proj_ref[:, 2 * dc:3 * dc].astype(F32)
        u = proj_ref[:, 3 * dc:4 * dc].astype(F32)
        first = jnp.where(tile > 0, 1.0, 0.0)
        zbuf[0:HALO, :] = halo_ref[:, 2 * dc:3 * dc].astype(F32) * halo_ref[:, 0:dc].astype(F32) * first
        ubuf[0:HALO, :] = halo_ref[:, 3 * dc:4 * dc].astype(F32) * first
        z = gc * v
        zbuf[HALO:HALO + tm, :] = z
        ubuf[HALO:HALO + tm, :] = u
        z1 = zbuf[HALO - 1:HALO - 1 + tm, :]
        z2 = zbuf[HALO - 2:HALO - 2 + tm, :]
        cw = cw_ref[...]
        conv = cw[2:3, :] * z + cw[1:2, :] * z1 + cw[0:1, :] * z2

        dy = _nt(dxo_b, wout[...])
        dya = dy[:, 0:dc]
        dgb = dya * conv
        dconv = dya * gb
        dcbuf[0:tm, :] = dconv
        dz = cw[2:3, :] * dconv + cw[1:2, :] * dcbuf[1:1 + tm, :] + cw[0:1, :] * dcbuf[2:2 + tm, :]
        dgc = dz * v
        dv = dz * gc
        dcw_ref[0:1, :] += jnp.sum(dconv * z2, axis=0, keepdims=True)
        dcw_ref[1:2, :] += jnp.sum(dconv * z1, axis=0, keepdims=True)
        dcw_ref[2:3, :] += jnp.sum(dconv * z, axis=0, keepdims=True)

        dproj_ref[:, 0:dc] = dv.astype(BF16)
        dproj_ref[:, dc:2 * dc] = dgb.astype(BF16)
        dproj_ref[:, 2 * dc:3 * dc] = dgc.astype(BF16)

        row = tile * tm + lax.broadcasted_iota(jnp.int32, (tm, 1), 0)
        for gi, w in enumerate(POOL_WINDOWS):
            cols = slice(gi * gcw, (gi + 1) * gcw)
            pooled, cnt = _pool_parts(u[:, cols], ubuf, cols, w, row, tm)
            pooled_b = pooled.astype(BF16)
            pw_b = pw_ref[gi].astype(BF16)
            dyb = dy[:, dc + gi * gcw:dc + (gi + 1) * gcw]
            q = _nn(pooled_b, pw_b)
            dps_ref[:, cols] += jnp.sum(q * dyb, axis=0, keepdims=True)
            dq = (dyb * ps_ref[:, cols]).astype(BF16)
            dpw_ref[gi] += _tn(pooled_b, dq)
            dpooled = _nt(dq, pw_b)
            ebuf[0:tm, cols] = dpooled / cnt
            du = -dpooled
            for s in range(w):
                du = du + ebuf[s:s + tm, cols]
            dproj_ref[:, 3 * dc + gi * gcw:3 * dc + (gi + 1) * gcw] = du.astype(BF16)

        dh = _nn(dproj_ref[...], win[...])
        dg_ref[...] += jnp.sum(dh * xhat, axis=0, keepdims=True)
        dxh = dh * gv
        dx_ref[...] = dxo_v + r * (dxh - xhat * jnp.mean(dxh * xhat, axis=-1, keepdims=True))
        dcbuf[tm:tm + 8, :] = dcbuf[0:8, :]
        ebuf[tm:tm + HALO, :] = ebuf[0:HALO, :]

    tok = lambda i: (n_tiles - 1 - i, 0)
    halo = lambda i: (jnp.maximum((n_tiles - 1 - i) * hb - 1, 0), 0)
    one = lambda i: (0, 0)
    return _launch(
        body, name="mixer_backward", grid=(n_tiles,),
        in_specs=[pl.BlockSpec((tm, d), tok), pl.BlockSpec((tm, d), tok), pl.BlockSpec((1, d), one),
                  pl.BlockSpec((tm, 4 * dc), tok), pl.BlockSpec((HALO, 4 * dc), halo), HBM_SPEC, HBM_SPEC,
                  pl.BlockSpec(conv_w.shape, one), pl.BlockSpec(pool_w.shape, lambda i: (0, 0, 0)), pl.BlockSpec((1, dc), one)],
        out_specs=[pl.BlockSpec((tm, d), tok), pl.BlockSpec((tm, 4 * dc), tok), pl.BlockSpec((tm, d), tok), pl.BlockSpec((tm, d), tok),
                   pl.BlockSpec((1, d), one), pl.BlockSpec(conv_w.shape, one), pl.BlockSpec((1, dc), one),
                   pl.BlockSpec(pool_w.shape, lambda i: (0, 0, 0))],
        out_shape=[jax.ShapeDtypeStruct((t, d), F32), jax.ShapeDtypeStruct((t, 4 * dc), BF16), jax.ShapeDtypeStruct((t, d), BF16),
                   jax.ShapeDtypeStruct((t, d), BF16), jax.ShapeDtypeStruct((1, d), F32), jax.ShapeDtypeStruct(conv_w.shape, F32),
                   jax.ShapeDtypeStruct((1, dc), F32), jax.ShapeDtypeStruct(pool_w.shape, F32)],
        scratch_shapes=[pltpu.VMEM((4 * dc, d), BF16), pltpu.VMEM((2 * dc, d), BF16),
                        pltpu.VMEM((tm + HALO, dc), F32), pltpu.VMEM((tm + HALO, dc), F32),
                        pltpu.VMEM((tm + 8, dc), F32), pltpu.VMEM((tm + HALO, dc), F32), pltpu.SemaphoreType.DMA((1 + N_CHIPS,))],
        args=(dxo, x, g, proj, proj, win_t, wout_x, conv_w, pool_w, pool_scale), cargo=cargo)


def _loss_backward(x, g, target):
    t, d = x.shape
    tm = min(TM_EW, t)

    def body(x_ref, g_ref, t_ref, dx_ref, sq_ref, dg_ref):
        @pl.when(pl.program_id(0) == 0)
        def _():
            sq_ref[...] = jnp.zeros_like(sq_ref)
            dg_ref[...] = jnp.zeros_like(dg_ref)

        xv = x_ref[...]
        gv = g_ref[...]
        r = lax.rsqrt(jnp.mean(xv * xv, axis=-1, keepdims=True) + EPS)
        xhat = xv * r
        err = xhat * gv - t_ref[...]
        sq_ref[...] += jnp.sum(err * err, axis=0, keepdims=True)
        dy = err * (1.0 / d)
        dg_ref[...] += jnp.sum(dy * xhat, axis=0, keepdims=True)
        dxh = dy * gv
        dx_ref[...] = r * (dxh - xhat * jnp.mean(dxh * xhat, axis=-1, keepdims=True))

    tok = lambda i: (i, 0)
    one = lambda i: (0, 0)
    return pl.pallas_call(
        body, name="loss_backward",
        out_shape=[jax.ShapeDtypeStruct((t, d), F32), jax.ShapeDtypeStruct((1, d), F32), jax.ShapeDtypeStruct((1, d), F32)],
        grid=(t // tm,),
        in_specs=[pl.BlockSpec((tm, d), tok), pl.BlockSpec((1, d), one), pl.BlockSpec((tm, d), tok)],
        out_specs=[pl.BlockSpec((tm, d), tok), pl.BlockSpec((1, d), one), pl.BlockSpec((1, d), one)],
        compiler_params=pltpu.CompilerParams(dimension_semantics=("arbitrary",)),
    )(x, g, target)


def _adamw(w, grad, m, v, name):
    rows, cols = w.shape
    br = _row_block(rows, 256) if rows >= 8 else rows
    bc1 = 1.0 - ADAM_B1 ** ADAM_STEP
    bc2 = 1.0 - ADAM_B2 ** ADAM_STEP

    def body(w_ref, g_ref, m_ref, v_ref, d_ref, mo_ref, vo_ref):
        gv = g_ref[...]
        m_new = ADAM_B1 * m_ref[...] + (1.0 - ADAM_B1) * gv
        v_new = ADAM_B2 * v_ref[...] + (1.0 - ADAM_B2) * (gv * gv)
        m_hat = m_new / bc1
        v_hat = v_new / bc2
        d_ref[...] = -ADAM_LR * (m_hat / (jnp.sqrt(v_hat) + ADAM_EPS) + ADAM_WD * w_ref[...])
        mo_ref[...] = m_new
        vo_ref[...] = v_new

    blk = pl.BlockSpec((br, cols), lambda i: (i, 0))
    return pl.pallas_call(
        body, name=name,
        out_shape=[jax.ShapeDtypeStruct((rows, cols), F32)] * 3,
        grid=(rows // br,), in_specs=[blk] * 4, out_specs=[blk] * 3,
        compiler_params=pltpu.CompilerParams(dimension_semantics=("parallel",)),
    )(w, grad, m, v)


def _f32_rows_as_bf16(a, rows, cols):
    bits = lax.bitcast_convert_type(a, BF16).reshape(a.shape[0], 2 * a.shape[1])
    return jnp.pad(bits, ((0, rows - bits.shape[0]), (0, cols - bits.shape[1])))


def kernel(x, norm_ffn1, ffn1_w_gate, ffn1_w_up, ffn1_w_down, norm_mix, w_in, conv_w, pool_w, pool_scale, w_out, norm_ffn2, ffn2_w_gate, ffn2_w_up, ffn2_w_down, norm_final, loss_target, m_norm_ffn1, m_ffn1_w_gate, m_ffn1_w_up, m_ffn1_w_down, m_norm_mix, m_w_in, m_conv_w, m_pool_w, m_pool_scale, m_w_out, m_norm_ffn2, m_ffn2_w_gate, m_ffn2_w_up, m_ffn2_w_down, m_norm_final, v_norm_ffn1, v_ffn1_w_gate, v_ffn1_w_up, v_ffn1_w_down, v_norm_mix, v_w_in, v_conv_w, v_pool_w, v_pool_scale, v_w_out, v_norm_ffn2, v_ffn2_w_gate, v_ffn2_w_up, v_ffn2_w_down, v_norm_final):
    weights = dict(norm_ffn1=norm_ffn1, ffn1_w_gate=ffn1_w_gate, ffn1_w_up=ffn1_w_up, ffn1_w_down=ffn1_w_down, norm_mix=norm_mix,
                   w_in=w_in, conv_w=conv_w, pool_w=pool_w, pool_scale=pool_scale, w_out=w_out, norm_ffn2=norm_ffn2,
                   ffn2_w_gate=ffn2_w_gate, ffn2_w_up=ffn2_w_up, ffn2_w_down=ffn2_w_down, norm_final=norm_final)
    first_m = dict(norm_ffn1=m_norm_ffn1, ffn1_w_gate=m_ffn1_w_gate, ffn1_w_up=m_ffn1_w_up, ffn1_w_down=m_ffn1_w_down,
                   norm_mix=m_norm_mix, w_in=m_w_in, conv_w=m_conv_w, pool_w=m_pool_w, pool_scale=m_pool_scale, w_out=m_w_out,
                   norm_ffn2=m_norm_ffn2, ffn2_w_gate=m_ffn2_w_gate, ffn2_w_up=m_ffn2_w_up, ffn2_w_down=m_ffn2_w_down,
                   norm_final=m_norm_final)
    second_m = dict(norm_ffn1=v_norm_ffn1, ffn1_w_gate=v_ffn1_w_gate, ffn1_w_up=v_ffn1_w_up, ffn1_w_down=v_ffn1_w_down,
                    norm_mix=v_norm_mix, w_in=v_w_in, conv_w=v_conv_w, pool_w=v_pool_w, pool_scale=v_pool_scale, w_out=v_w_out,
                    norm_ffn2=v_norm_ffn2, ffn2_w_gate=v_ffn2_w_gate, ffn2_w_up=v_ffn2_w_up, ffn2_w_down=v_ffn2_w_down,
                    norm_final=v_norm_final)
    names = list(weights)

    xs = x[0]
    tgt = loss_target[0]
    t, d = xs.shape
    dc = pool_scale.shape[1]
    cx, cy, cc = _my_place()
    chip = 2 * cx + cy
    place = jnp.stack([chip, cc]).astype(jnp.int32)

    conv_rows = 32
    wout_x = jnp.concatenate([w_out[0].astype(BF16), _f32_rows_as_bf16(conv_w[0], conv_rows, d)], axis=0)
    ffn1_shards = [ffn1_w_gate[0].T.astype(BF16), ffn1_w_up[0].T.astype(BF16), ffn1_w_down[0].astype(BF16)]
    mid_shards = [w_in[0].T.astype(BF16), wout_x, ffn2_w_gate[0].T.astype(BF16), ffn2_w_up[0].T.astype(BF16)]
    last_shards = [ffn2_w_down[0].astype(BF16)]

    g1, gm, g2 = norm_ffn1, norm_mix, norm_ffn2
    gf = norm_final.reshape(1, d)
    pw = pool_w[0]

    wg1, wu1, wd1 = _run_cargo(_gather_cargo(ffn1_shards), "gather_ffn1")
    (x1, a1, b1), (win_t, wout_g, wg2, wu2) = _ffn_forward(xs, g1, wg1, wu1, wd1, "ffn1_forward", _gather_cargo(mid_shards))
    wo_rows = w_out.shape[1]
    cshard = conv_w.shape[2]
    conv_bits = wout_g.reshape(N_CHIPS, wo_rows + conv_rows, d)[:, wo_rows:wo_rows + conv_w.shape[1], :2 * cshard]
    conv_full = lax.bitcast_convert_type(conv_bits.reshape(N_CHIPS, conv_w.shape[1], cshard, 2), F32)
    conv_full = jnp.transpose(conv_full, (1, 0, 2)).reshape(conv_w.shape[1], N_CHIPS * cshard)
    (x2, proj, ymix), (wd2,) = _mixer_forward(x1, gm, win_t, wout_g, conv_full, pw, pool_scale, _gather_cargo(last_shards))
    (x3, a2, b2), _ = _ffn_forward(x2, g2, wg2, wu2, wd2, "ffn2_forward")

    dx3, sq_cols, dgf = _loss_backward(x3, gf, tgt)
    loss = lax.psum(jnp.sum(sq_cols) * (0.5 / d), ("x", "y", "c"))

    (dx2, da2, db2, s2, h3, do2, dg2), _ = _ffn_backward(dx3, x2, g2, a2, b2, wg2, wu2, wd2, "ffn2_backward")
    gwg2, _ = _weight_grad(da2, h3, "ffn2_gate_grad")
    p_wg2, = _pair_sums([gwg2], "wg2")
    gwu2, (x_wg2,) = _weight_grad(db2, h3, "ffn2_up_grad", _exchange_cargo([p_wg2]))
    p_wu2, = _pair_sums([gwu2], "wu2")
    gwd2, (x_wu2,) = _weight_grad(s2, do2, "ffn2_down_grad", _exchange_cargo([p_wu2]))
    p_wd2, = _pair_sums([gwd2], "wd2")

    (dx1, dproj, h2, dx2b, dgm, dcw, dps, dpw), (x_wd2,) = _mixer_backward(
        dx2, x1, gm, proj, win_t, wout_g, conv_full, pw, pool_scale, _exchange_cargo([p_wd2]))
    gwin, _ = _weight_grad(dproj, h2, "w_in_grad")
    gwout, _ = _weight_grad(ymix, dx2b, "w_out_grad")
    p_win, p_wout = _pair_sums([gwin, gwout], "mix")

    (dx0, da1, db1, s1, h1, do1, dg1), (x_win, x_wout) = _ffn_backward(
        dx1, xs, g1, a1, b1, wg1, wu1, wd1, "ffn1_backward", _exchange_cargo([p_win, p_wout]))

    npw = pw.size // d
    pack = jnp.concatenate([
        dg1, dgm, dg2, dgf,
        jnp.pad(dps, ((0, 0), (0, d - dc))), jnp.pad(dcw, ((0, 0), (0, d - dc))),
        jnp.zeros((8 - (5 + dcw.shape[0]) % 8, d), F32) if (5 + dcw.shape[0]) % 8 else jnp.zeros((0, d), F32),
        dpw.reshape(npw, d)], axis=0)
    small = _all_reduce_small(pack)

    gwg1, _ = _weight_grad(da1, h1, "ffn1_gate_grad")
    p_wg1, = _pair_sums([gwg1], "wg1")
    gwu1, (x_wg1,) = _weight_grad(db1, h1, "ffn1_up_grad", _exchange_cargo([p_wg1]))
    p_wu1, = _pair_sums([gwu1], "wu1")
    gwd1, (x_wu1,) = _weight_grad(s1, do1, "ffn1_down_grad", _exchange_cargo([p_wu1]))
    p_wd1, = _pair_sums([gwd1], "wd1")
    x_wd1, = _run_cargo(_exchange_cargo([p_wd1]), "grad_exchange_last")

    order = ["wg1", "wu1", "wd1", "win", "wout", "wg2", "wu2", "wd2"]
    pairs = dict(wg1=p_wg1, wu1=p_wu1, wd1=p_wd1, win=p_win, wout=p_wout, wg2=p_wg2, wu2=p_wu2, wd2=p_wd2)
    landed = dict(wg1=x_wg1, wu1=x_wu1, wd1=x_wd1, win=x_win, wout=x_wout, wg2=x_wg2, wu2=x_wu2, wd2=x_wd2)
    both = _sibling_share([_chip_sum(pairs[k], landed[k], place, k) for k in order])
    rwg1, rwu1, rwd1, rwin, rwout, rwg2, rwu2, rwd2 = [b.reshape(2 * b.shape[1], b.shape[2]) for b in both]

    base = pack.shape[0] - npw
    grads = {
        "norm_ffn1": small[0:1], "norm_mix": small[1:2], "norm_ffn2": small[2:3], "norm_final": small[3],
        "pool_scale": small[4:5, :dc],
        "conv_w": lax.dynamic_slice_in_dim(small[5:5 + dcw.shape[0], :dc], chip * cshard, cshard, axis=1)[None],
        "pool_w": small[base:].reshape(pool_w.shape),
        "ffn1_w_gate": rwg1.T[None], "ffn1_w_up": rwu1.T[None], "ffn1_w_down": rwd1[None],
        "w_in": rwin.T[None], "w_out": rwout[None],
        "ffn2_w_gate": rwg2.T[None], "ffn2_w_up": rwu2.T[None], "ffn2_w_down": rwd2[None],
    }

    deltas, new_m, new_v = {}, {}, {}
    for n in names:
        w = weights[n]
        shape = w.shape
        as2d = (lambda a: a.reshape(-1, shape[-1]))
        dl, mo, vo = _adamw(as2d(w), as2d(grads[n]), as2d(first_m[n]), as2d(second_m[n]), "adamw_" + n)
        deltas[n], new_m[n], new_v[n] = dl.reshape(shape), mo.reshape(shape), vo.reshape(shape)
        grads[n] = grads[n].reshape(shape)

    return (loss, dx0[None], *[grads[n] for n in names], *[deltas[n] for n in names],
            *[new_m[n] for n in names], *[new_v[n] for n in names])
```

```python
import functools

import jax
import jax.numpy as jnp
from jax import lax
from jax.experimental import pallas as pl
from jax.experimental.pallas import tpu as pltpu

F32 = jnp.float32
BF16 = jnp.bfloat16
MESH = pl.DeviceIdType.MESH

EPS = 1e-6
POOL_WINDOWS = (2, 4, 8, 16)
ADAM_LR = 0.001
ADAM_B1 = 0.9
ADAM_B2 = 0.999
ADAM_EPS = 1e-08
ADAM_WD = 0.01
ADAM_STEP = 10

N_CHIPS = 4
N_DEVICES = 8
MXU_COLS_V7X = 256
VMEM_LIMIT = 56 * 1024 * 1024
TM_FFN = 512
TM_MIX = 512
TM_TN = 1024
TM_EW = 512
HALO = 16


def _nt(a, b):
    return lax.dot_general(a, b, (((1,), (1,)), ((), ())), preferred_element_type=F32)


def _tn(a, b):
    return lax.dot_general(a, b, (((0,), (0,)), ((), ())), preferred_element_type=F32)


def _nn(a, b):
    return jnp.dot(a, b, preferred_element_type=F32)


def _sigmoid(a):
    return 1.0 / (1.0 + jnp.exp(-a))


def _feature_chunks(n):
    assert n % MXU_COLS_V7X == 0
    tiles = n // MXU_COLS_V7X
    first = (tiles + 1) // 2
    sizes = [first * MXU_COLS_V7X, (tiles - first) * MXU_COLS_V7X]
    out, s0 = [], 0
    for sz in sizes:
        if sz:
            out.append((s0, sz))
            s0 += sz
    return out


def _row_block(rows, cap):
    best = 8
    for b in range(8, min(rows, cap) + 1, 8):
        if rows % b == 0:
            best = b
    assert rows % best == 0
    return best


def _my_place():
    return lax.axis_index("x"), lax.axis_index("y"), lax.axis_index("c")


def _other_chips(x, y):
    return [(1 - x, y), (x, 1 - y), (1 - x, 1 - y)]


HBM_SPEC = pl.BlockSpec(memory_space=pltpu.HBM)


class _Cargo:
    def __init__(self, operands, out_shapes, n_sems, start, finish):
        self.operands, self.out_shapes, self.n_sems, self.start, self.finish = list(operands), list(out_shapes), n_sems, start, finish


def _launch(body, *, name, grid, in_specs, out_specs, out_shape, scratch_shapes, args, cargo=()):
    params = pltpu.CompilerParams(dimension_semantics=("arbitrary",) * len(grid), vmem_limit_bytes=VMEM_LIMIT)
    cargos = list(cargo)
    c_operands = [op for cg in cargos for op in cg.operands]
    c_shapes = [sh for cg in cargos for sh in cg.out_shapes]
    counts = [len(in_specs), len(c_operands), len(out_shape), len(c_shapes), len(scratch_shapes), 2 * len(cargos)]

    def carrying(*refs):
        groups, pos = [], 0
        for k in counts:
            groups.append(refs[pos:pos + k])
            pos += k
        ins, c_ins, outs, c_outs, scratch, sems = groups
        parts, pi, po = [], 0, 0
        for n, cg in enumerate(cargos):
            parts.append((c_ins[pi:pi + len(cg.operands)], c_outs[po:po + len(cg.out_shapes)], sems[2 * n], sems[2 * n + 1]))
            pi += len(cg.operands)
            po += len(cg.out_shapes)
        ids = [pl.program_id(ax) for ax in range(len(grid))]
        first = functools.reduce(jnp.logical_and, [i == 0 for i in ids])
        last = functools.reduce(jnp.logical_and, [i == g - 1 for i, g in zip(ids, grid)])

        if cargos:
            @pl.when(first)
            def _():
                for cg, part in zip(cargos, parts):
                    cg.start(*part)

        body(*ins, *outs, *scratch)

        if cargos:
            @pl.when(last)
            def _():
                for cg, part in zip(cargos, parts):
                    cg.finish(*part)

    sems = [pltpu.SemaphoreType.DMA((cg.n_sems,)) for cg in cargos for _ in range(2)]
    outs = pl.pallas_call(
        carrying, name=name, grid=grid,
        in_specs=list(in_specs) + [HBM_SPEC] * counts[1], out_specs=list(out_specs) + [HBM_SPEC] * counts[3],
        out_shape=list(out_shape) + c_shapes, scratch_shapes=list(scratch_shapes) + sems,
        compiler_params=params)(*args, *c_operands)
    own, rest = list(outs[:counts[2]]), list(outs[counts[2]:])
    carried, po = [], 0
    for cg in cargos:
        carried.append(rest[po:po + len(cg.out_shapes)])
        po += len(cg.out_shapes)
    return own, carried


def _run_cargo(cargo, name):
    n_in, n_out = len(cargo.operands), len(cargo.out_shapes)

    def body(*refs):
        c_ins, c_outs, sems = refs[:n_in], refs[n_in:n_in + n_out], refs[n_in + n_out:]
        cargo.start(c_ins, c_outs, *sems)
        cargo.finish(c_ins, c_outs, *sems)

    sem = pltpu.SemaphoreType.DMA((cargo.n_sems,))
    return list(pl.pallas_call(body, name=name, out_shape=cargo.out_shapes, in_specs=[HBM_SPEC] * n_in,
                               out_specs=[HBM_SPEC] * n_out, scratch_shapes=[sem, sem])(*cargo.operands))


def _gather_cargo(shards):
    n = len(shards)
    for s in shards:
        assert s.shape[0] % 32 == 0

    def steps(ins, outs, send_sems, recv_sems):
        x, y, c = _my_place()
        sibling = (x, y, 1 - c)
        chips = _other_chips(x, y)
        mine = 2 * x + y

        def rows_of(a, chip_index, half):
            rps = shards[a].shape[0]
            hr = rps // 2
            return outs[a].at[pl.ds(pl.multiple_of(chip_index * rps + half * hr, 16), hr), :]

        def remote(a, slot, src, dst, to):
            return pltpu.make_async_remote_copy(
                src_ref=src, dst_ref=dst, send_sem=send_sems.at[a * 7 + slot], recv_sem=recv_sems.at[a * 7 + slot],
                device_id=to, device_id_type=MESH)

        def own_copy(a):
            rps = shards[a].shape[0]
            return remote(a, 6, ins[a], outs[a].at[pl.ds(pl.multiple_of(mine * rps, 16), rps), :], sibling)

        def my_half(a):
            hr = shards[a].shape[0] // 2
            return ins[a].at[pl.ds(pl.multiple_of(c * hr, 16), hr), :]

        def start():
            for a in range(n):
                own_copy(a).start()
                for j, chip in enumerate(chips):
                    remote(a, j, my_half(a), rows_of(a, mine, c), (*chip, c)).start()

        def finish():
            for a in range(n):
                for j, chip in enumerate(chips):
                    landed = rows_of(a, 2 * chip[0] + chip[1], c)
                    remote(a, j, landed, landed, (*chip, c)).wait_recv()
                    remote(a, 3 + j, landed, landed, sibling).start()
            for a in range(n):
                for j, chip in enumerate(chips):
                    from_sibling = rows_of(a, 2 * chip[0] + chip[1], 1 - c)
                    remote(a, 3 + j, from_sibling, from_sibling, sibling).wait_recv()
            for a in range(n):
                for j, chip in enumerate(chips):
                    remote(a, j, my_half(a), rows_of(a, mine, c), (*chip, c)).wait_send()
                    landed = rows_of(a, 2 * chip[0] + chip[1], c)
                    remote(a, 3 + j, landed, landed, sibling).wait_send()
                own_copy(a).wait()

        return start, finish

    return _Cargo(shards, [jax.ShapeDtypeStruct((N_CHIPS * s.shape[0], s.shape[1]), s.dtype) for s in shards], 7 * n,
                  lambda *r: steps(*r)[0](), lambda *r: steps(*r)[1]())


def _exchange_cargo(pairs):
    n = len(pairs)

    def copies(ins, outs, send_sems, recv_sems):
        x, y, c = _my_place()
        return [pltpu.make_async_remote_copy(
            src_ref=ins[a].at[2 * chip[0] + chip[1]], dst_ref=outs[a].at[j],
            send_sem=send_sems.at[3 * a + j], recv_sem=recv_sems.at[3 * a + j], device_id=(*chip, c), device_id_type=MESH)
            for a in range(n) for j, chip in enumerate(_other_chips(x, y))]

    def start(*r):
        for cp in copies(*r):
            cp.start()

    def finish(*r):
        for cp in copies(*r):
            cp.wait()

    return _Cargo(pairs, [jax.ShapeDtypeStruct((3,) + p.shape[1:], p.dtype) for p in pairs], 3 * n, start, finish)


def _all_gather_small_cargo(pack):
    rows, cols = pack.shape

    def copies(ins, outs, send_sems, recv_sems):
        x, y, c = _my_place()
        me = 4 * x + 2 * y + c
        remote = []
        for f in range(1, N_DEVICES):
            fx, fy, fc = (f >> 2) & 1, (f >> 1) & 1, f & 1
            to = (1 - x if fx else x, 1 - y if fy else y, 1 - c if fc else c)
            remote.append(pltpu.make_async_remote_copy(
                src_ref=ins[0], dst_ref=outs[0].at[me], send_sem=send_sems.at[f - 1], recv_sem=recv_sems.at[f - 1],
                device_id=to, device_id_type=MESH))
        own = pltpu.make_async_copy(ins[0], outs[0].at[me], send_sems.at[N_DEVICES - 1])
        return remote, own

    def start(*r):
        remote, own = copies(*r)
        own.start()
        for cp in remote:
            cp.start()

    def finish(*r):
        remote, own = copies(*r)
        for cp in remote:
            cp.wait()
        own.wait()

    return _Cargo([pack], [jax.ShapeDtypeStruct((N_DEVICES, rows, cols), F32)], N_DEVICES, start, finish)


def _sum_by_device(packs):
    n, rows, cols = packs.shape

    def body(p_ref, o_ref):
        acc = p_ref[0]
        for dev in range(1, n):
            acc = acc + p_ref[dev]
        o_ref[...] = acc

    return pl.pallas_call(body, name="small_grads_sum", out_shape=jax.ShapeDtypeStruct((rows, cols), F32))(packs)


def _chip_sum(pair, got, place, tag):
    _, hr, cols = pair.shape
    br = _row_block(hr, 256)

    def body(k_ref, p_ref, r_ref, o_ref):
        acc = p_ref[...].astype(F32)
        for j in range(3):
            acc = acc + r_ref[j].astype(F32)
        o_ref[...] = acc

    return pl.pallas_call(
        body, name="grad_chip_sum_" + tag,
        out_shape=jax.ShapeDtypeStruct((2, hr, cols), F32),
        grid_spec=pltpu.PrefetchScalarGridSpec(
            num_scalar_prefetch=1, grid=(hr // br,),
            in_specs=[pl.BlockSpec((None, br, cols), lambda r, k_ref: (k_ref[0], r, 0)),
                      pl.BlockSpec((3, br, cols), lambda r, k_ref: (0, r, 0))],
            out_specs=pl.BlockSpec((None, br, cols), lambda r, k_ref: (k_ref[1], r, 0))),
        compiler_params=pltpu.CompilerParams(dimension_semantics=("parallel",)),
    )(place, pair, got)


def _sibling_share(halves):
    n = len(halves)

    def body(*refs):
        outs = refs[n:2 * n]
        send_sems, recv_sems = refs[2 * n:]
        x, y, c = _my_place()
        copies = []
        for a in range(n):
            cp = pltpu.make_async_remote_copy(
                src_ref=outs[a].at[c], dst_ref=outs[a].at[c], send_sem=send_sems.at[a], recv_sem=recv_sems.at[a],
                device_id=(x, y, 1 - c), device_id_type=MESH)
            cp.start()
            copies.append(cp)
        for cp in copies:
            cp.wait()

    return pl.pallas_call(
        body, name="grad_share_sibling",
        out_shape=[jax.ShapeDtypeStruct(h.shape, h.dtype) for h in halves],
        in_specs=[HBM_SPEC] * n, out_specs=[HBM_SPEC] * n,
        input_output_aliases={a: a for a in range(n)},
        scratch_shapes=[pltpu.SemaphoreType.DMA((n,)), pltpu.SemaphoreType.DMA((n,))],
    )(*halves)


def _load_rows(pairs, sems):
    cps = [pltpu.make_async_copy(src, dst, sems.at[j]) for j, (src, dst) in enumerate(pairs)]
    for cp in cps:
        cp.start()
    for cp in cps:
        cp.wait()


def _ffn_forward(x, g, wg_t, wu_t, wd, name, cargo=()):
    t, d = x.shape
    f = wd.shape[0]
    tm = min(TM_FFN, t)
    chunks = _feature_chunks(f)

    def body(x_ref, g_ref, wg_hbm, wu_hbm, wd_hbm, xo_ref, a_ref, b_ref, wg, wu, wdn, sems):
        @pl.when(pl.program_id(0) == 0)
        def _():
            _load_rows([(wg_hbm, wg), (wu_hbm, wu), (wd_hbm, wdn)], sems)

        xv = x_ref[...]
        r = lax.rsqrt(jnp.mean(xv * xv, axis=-1, keepdims=True) + EPS)
        h = (xv * r * g_ref[...]).astype(BF16)
        acc = jnp.zeros((tm, d), F32)
        for s0, sz in chunks:
            a = _nt(h, wg[s0:s0 + sz, :])
            b = _nt(h, wu[s0:s0 + sz, :])
            a_ref[:, s0:s0 + sz] = a.astype(BF16)
            b_ref[:, s0:s0 + sz] = b.astype(BF16)
            s = (a * _sigmoid(a) * b).astype(BF16)
            acc = acc + _nn(s, wdn[s0:s0 + sz, :])
        xo_ref[...] = xv + 0.5 * acc

    tok = lambda i: (i, 0)
    return _launch(
        body, name=name, grid=(t // tm,),
        in_specs=[pl.BlockSpec((tm, d), tok), pl.BlockSpec((1, d), lambda i: (0, 0)), HBM_SPEC, HBM_SPEC, HBM_SPEC],
        out_specs=[pl.BlockSpec((tm, d), tok), pl.BlockSpec((tm, f), tok), pl.BlockSpec((tm, f), tok)],
        out_shape=[jax.ShapeDtypeStruct((t, d), F32), jax.ShapeDtypeStruct((t, f), BF16), jax.ShapeDtypeStruct((t, f), BF16)],
        scratch_shapes=[pltpu.VMEM((f, d), BF16), pltpu.VMEM((f, d), BF16), pltpu.VMEM((f, d), BF16), pltpu.SemaphoreType.DMA((3,))],
        args=(x, g, wg_t, wu_t, wd), cargo=cargo)


def _ffn_backward(dxo, x, g, a, b, wg_t, wu_t, wd, name, cargo=()):
    t, d = x.shape
    f = wd.shape[0]
    tm = min(TM_FFN // 2, t)
    chunks = _feature_chunks(f)

    def body(dxo_ref, x_ref, g_ref, a_ref, b_ref, wg_hbm, wu_hbm, wd_hbm,
             dx_ref, da_ref, db_ref, s_ref, h_ref, do_ref, dg_ref, wg, wu, wdn, sems):
        @pl.when(pl.program_id(0) == 0)
        def _():
            _load_rows([(wg_hbm, wg), (wu_hbm, wu), (wd_hbm, wdn)], sems)
            dg_ref[...] = jnp.zeros_like(dg_ref)

        xv = x_ref[...]
        gv = g_ref[...]
        r = lax.rsqrt(jnp.mean(xv * xv, axis=-1, keepdims=True) + EPS)
        xhat = xv * r
        h_ref[...] = (xhat * gv).astype(BF16)
        dxo_v = dxo_ref[...]
        dout = (0.5 * dxo_v).astype(BF16)
        do_ref[...] = dout
        dh = jnp.zeros((tm, d), F32)
        for s0, sz in chunks:
            ds = _nt(dout, wdn[s0:s0 + sz, :])
            av = a_ref[:, s0:s0 + sz].astype(F32)
            bv = b_ref[:, s0:s0 + sz].astype(F32)
            sig = _sigmoid(av)
            silu = av * sig
            s_ref[:, s0:s0 + sz] = (silu * bv).astype(BF16)
            da = (ds * bv * (sig * (1.0 + av * (1.0 - sig)))).astype(BF16)
            db = (ds * silu).astype(BF16)
            da_ref[:, s0:s0 + sz] = da
            db_ref[:, s0:s0 + sz] = db
            dh = dh + _nn(da, wg[s0:s0 + sz, :]) + _nn(db, wu[s0:s0 + sz, :])
        dg_ref[...] += jnp.sum(dh * xhat, axis=0, keepdims=True)
        dxh = dh * gv
        dx_ref[...] = dxo_v + r * (dxh - xhat * jnp.mean(dxh * xhat, axis=-1, keepdims=True))

    tok = lambda i: (i, 0)
    one = lambda i: (0, 0)
    return _launch(
        body, name=name, grid=(t // tm,),
        in_specs=[pl.BlockSpec((tm, d), tok), pl.BlockSpec((tm, d), tok), pl.BlockSpec((1, d), one),
                  pl.BlockSpec((tm, f), tok), pl.BlockSpec((tm, f), tok), HBM_SPEC, HBM_SPEC, HBM_SPEC],
        out_specs=[pl.BlockSpec((tm, d), tok), pl.BlockSpec((tm, f), tok), pl.BlockSpec((tm, f), tok), pl.BlockSpec((tm, f), tok),
                   pl.BlockSpec((tm, d), tok), pl.BlockSpec((tm, d), tok), pl.BlockSpec((1, d), one)],
        out_shape=[jax.ShapeDtypeStruct((t, d), F32), jax.ShapeDtypeStruct((t, f), BF16), jax.ShapeDtypeStruct((t, f), BF16),
                   jax.ShapeDtypeStruct((t, f), BF16), jax.ShapeDtypeStruct((t, d), BF16), jax.ShapeDtypeStruct((t, d), BF16),
                   jax.ShapeDtypeStruct((1, d), F32)],
        scratch_shapes=[pltpu.VMEM((f, d), BF16), pltpu.VMEM((f, d), BF16), pltpu.VMEM((f, d), BF16), pltpu.SemaphoreType.DMA((3,))],
        args=(dxo, x, g, a, b, wg_t, wu_t, wd), cargo=cargo)


def _weight_grad(lhs, rhs, name, cargo=()):
    t, m = lhs.shape
    d = rhs.shape[1]
    tm = min(TM_TN, t)
    nt = t // tm
    nj = 2 if (m // 2) % 128 == 0 and m > 1024 else 1
    bm = m // nj
    cpb = N_CHIPS // nj
    rps = m // N_CHIPS
    hr = rps // 2
    assert hr % 16 == 0

    def body(l_ref, r_ref, o_ref, acc, stage, recv, send_sems, recv_sems):
        j = pl.program_id(0)
        i = pl.program_id(1)
        part = _tn(l_ref[...], r_ref[...])

        @pl.when(i == 0)
        def _():
            acc[...] = part

        @pl.when(i > 0)
        def _():
            acc[...] += part

        def pair_sum(jj):
            x, y, c = _my_place()
            copies = []
            for q in range(cpb):
                slot = jj * cpb + q
                stage[slot] = acc[pl.ds(pl.multiple_of(q * rps + (1 - c) * hr, 16), hr), :].astype(BF16)
                cp = pltpu.make_async_remote_copy(
                    src_ref=stage.at[slot], dst_ref=recv.at[slot], send_sem=send_sems.at[slot], recv_sem=recv_sems.at[slot],
                    device_id=(x, y, 1 - c), device_id_type=MESH)
                cp.start()
                copies.append(cp)
            for q, cp in enumerate(copies):
                cp.wait_recv()
                mine = acc[pl.ds(pl.multiple_of(q * rps + c * hr, 16), hr), :]
                o_ref[q] = (mine + recv[jj * cpb + q].astype(F32)).astype(BF16)
            for cp in copies:
                cp.wait_send()

        for jj in range(nj):
            @pl.when(jnp.logical_and(i == nt - 1, j == jj))
            def _():
                pair_sum(jj)

    outs, carried = _launch(
        body, name=name, grid=(nj, nt),
        in_specs=[pl.BlockSpec((tm, bm), lambda j, i: (i, j)), pl.BlockSpec((tm, d), lambda j, i: (i, 0))],
        out_specs=[pl.BlockSpec((cpb, hr, d), lambda j, i: (j, 0, 0))],
        out_shape=[jax.ShapeDtypeStruct((N_CHIPS, hr, d), BF16)],
        scratch_shapes=[pltpu.VMEM((bm, d), F32), pltpu.VMEM((N_CHIPS, hr, d), BF16), pltpu.VMEM((N_CHIPS, hr, d), BF16),
                        pltpu.SemaphoreType.DMA((N_CHIPS,)), pltpu.SemaphoreType.DMA((N_CHIPS,))],
        args=(lhs, rhs), cargo=cargo)
    return outs[0], carried


def _pool_parts(u_cols, ubuf, cols, w, row, tm):
    ws = u_cols
    for s in range(1, w):
        ws = ws + ubuf[HALO - s:HALO - s + tm, cols]
    cnt = jnp.minimum(row + 1, w).astype(F32)
    return ws / cnt - u_cols, cnt


def _mixer_forward(x, g, win_t, wout_x, conv_w, pool_w, pool_scale, cargo=()):
    t, d = x.shape
    dc = win_t.shape[0] // 4
    gcw = dc // len(POOL_WINDOWS)
    wo_rows = d // N_CHIPS
    wo_stride = wout_x.shape[0] // N_CHIPS
    tm = min(TM_MIX, t)

    def body(x_ref, g_ref, win_hbm, wout_hbm, cw_ref, pw_ref, ps_ref, xo_ref, proj_ref, y_ref,
             win, wout, zbuf, ubuf, sems):
        i = pl.program_id(0)

        @pl.when(i == 0)
        def _():
            pairs = [(win_hbm, win)]
            for k in range(N_CHIPS):
                pairs.append((wout_hbm.at[pl.ds(k * wo_stride, wo_rows), :], wout.at[pl.ds(k * wo_rows, wo_rows), :]))
            _load_rows(pairs, sems)
            zbuf[0:8, :] = jnp.zeros((8, dc), F32)
            ubuf[0:HALO, :] = jnp.zeros((HALO, dc), F32)

        xv = x_ref[...]
        r = lax.rsqrt(jnp.mean(xv * xv, axis=-1, keepdims=True) + EPS)
        h = (xv * r * g_ref[...]).astype(BF16)
        v = _nt(h, win[0:dc, :])
        gb = _nt(h, win[dc:2 * dc, :])
        gc = _nt(h, win[2 * dc:3 * dc, :])
        u = _nt(h, win[3 * dc:4 * dc, :])
        proj_ref[:, 0:dc] = v.astype(BF16)
        proj_ref[:, dc:2 * dc] = gb.astype(BF16)
        proj_ref[:, 2 * dc:3 * dc] = gc.astype(BF16)
        proj_ref[:, 3 * dc:4 * dc] = u.astype(BF16)

        z = gc * v
        zbuf[8:8 + tm, :] = z
        cw = cw_ref[...]
        conv = cw[2:3, :] * z + cw[1:2, :] * zbuf[7:7 + tm, :] + cw[0:1, :] * zbuf[6:6 + tm, :]
        y_ref[:, 0:dc] = (gb * conv).astype(BF16)

        ubuf[HALO:HALO + tm, :] = u
        row = i * tm + lax.broadcasted_iota(jnp.int32, (tm, 1), 0)
        for gi, w in enumerate(POOL_WINDOWS):
            cols = slice(gi * gcw, (gi + 1) * gcw)
            pooled, _ = _pool_parts(u[:, cols], ubuf, cols, w, row, tm)
            yb = _nn(pooled.astype(BF16), pw_ref[gi].astype(BF16)) * ps_ref[:, cols]
            y_ref[:, dc + gi * gcw:dc + (gi + 1) * gcw] = yb.astype(BF16)

        xo_ref[...] = xv + _nn(y_ref[...], wout[...])
        zbuf[0:8, :] = zbuf[tm:tm + 8, :]
        ubuf[0:HALO, :] = ubuf[tm:tm + HALO, :]

    tok = lambda i: (i, 0)
    one = lambda i: (0, 0)
    return _launch(
        body, name="mixer_forward", grid=(t // tm,),
        in_specs=[pl.BlockSpec((tm, d), tok), pl.BlockSpec((1, d), one), HBM_SPEC, HBM_SPEC,
                  pl.BlockSpec(conv_w.shape, one), pl.BlockSpec(pool_w.shape, lambda i: (0, 0, 0)), pl.BlockSpec((1, dc), one)],
        out_specs=[pl.BlockSpec((tm, d), tok), pl.BlockSpec((tm, 4 * dc), tok), pl.BlockSpec((tm, 2 * dc), tok)],
        out_shape=[jax.ShapeDtypeStruct((t, d), F32), jax.ShapeDtypeStruct((t, 4 * dc), BF16), jax.ShapeDtypeStruct((t, 2 * dc), BF16)],
        scratch_shapes=[pltpu.VMEM((4 * dc, d), BF16), pltpu.VMEM((2 * dc, d), BF16),
                        pltpu.VMEM((tm + 8, dc), F32), pltpu.VMEM((tm + HALO, dc), F32), pltpu.SemaphoreType.DMA((1 + N_CHIPS,))],
        args=(x, g, win_t, wout_x, conv_w, pool_w, pool_scale), cargo=cargo)


def _mixer_backward(dxo, x, g, proj, win_t, wout_x, conv_w, pool_w, pool_scale, cargo=()):
    t, d = x.shape
    dc = win_t.shape[0] // 4
    ng = len(POOL_WINDOWS)
    gcw = dc // ng
    wo_rows = d // N_CHIPS
    wo_stride = wout_x.shape[0] // N_CHIPS
    tm = min(TM_MIX // 2, t)
    n_tiles = t // tm
    hb = tm // HALO

    def body(dxo_ref, x_ref, g_ref, proj_ref, halo_ref, win_hbm, wout_hbm, cw_ref, pw_ref, ps_ref,
             dx_ref, dproj_ref, h_ref, dxob_ref, dg_ref, dcw_ref, dps_ref, dpw_ref,
             win, wout, zbuf, ubuf, dcbuf, ebuf, sems):
        i = pl.program_id(0)
        tile = n_tiles - 1 - i

        @pl.when(i == 0)
        def _():
            pairs = [(win_hbm, win)]
            for k in range(N_CHIPS):
                pairs.append((wout_hbm.at[pl.ds(k * wo_stride, wo_rows), :], wout.at[pl.ds(k * wo_rows, wo_rows), :]))
            _load_rows(pairs, sems)
            dcbuf[tm:tm + 8, :] = jnp.zeros((8, dc), F32)
            ebuf[tm:tm + HALO, :] = jnp.zeros((HALO, dc), F32)
            dg_ref[...] = jnp.zeros_like(dg_ref)
            dcw_ref[...] = jnp.zeros_like(dcw_ref)
            dps_ref[...] = jnp.zeros_like(dps_ref)
            dpw_ref[...] = jnp.zeros_like(dpw_ref)

        xv = x_ref[...]
        gv = g_ref[...]
        r = lax.rsqrt(jnp.mean(xv * xv, axis=-1, keepdims=True) + EPS)
        xhat = xv * r
        h_ref[...] = (xhat * gv).astype(BF16)
        dxo_v = dxo_ref[...]
        dxo_b = dxo_v.astype(BF16)
        dxob_ref[...] = dxo_b

        v = proj_ref[:, 0:dc].astype(F32)
        gb = proj_ref[:, dc:2 * dc].astype(F32)
        gc = proj_ref[:, 2 * dc:3 * dc].astype(F32)
        u = proj_ref[:, 3 * dc:4 * dc].astype(F32)
        first = jnp.where(tile > 0, 1.0, 0.0)
        zbuf[0:HALO, :] = halo_ref[:, 2 * dc:3 * dc].astype(F32) * halo_ref[:, 0:dc].astype(F32) * first
        ubuf[0:HALO, :] = halo_ref[:, 3 * dc:4 * dc].astype(F32) * first
        z = gc * v
        zbuf[HALO:HALO + tm, :] = z
        ubuf[HALO:HALO + tm, :] = u
        z1 = zbuf[HALO - 1:HALO - 1 + tm, :]
        z2 = zbuf[HALO - 2:HALO - 2 + tm, :]
        cw = cw_ref[...]
        conv = cw[2:3, :] * z + cw[1:2, :] * z1 + cw[0:1, :] * z2

        dy = _nt(dxo_b, wout[...])
        dya = dy[:, 0:dc]
        dgb = dya * conv
        dconv = dya * gb
        dcbuf[0:tm, :] = dconv
        dz = cw[2:3, :] * dconv + cw[1:2, :] * dcbuf[1:1 + tm, :] + cw[0:1, :] * dcbuf[2:2 + tm, :]
        dgc = dz * v
        dv = dz * gc
        dcw_ref[0:1, :] += jnp.sum(dconv * z2, axis=0, keepdims=True)
        dcw_ref[1:2, :] += jnp.sum(dconv * z1, axis=0, keepdims=True)
        dcw_ref[2:3, :] += jnp.sum(dconv * z, axis=0, keepdims=True)

        dproj_ref[:, 0:dc] = dv.astype(BF16)
        dproj_ref[:, dc:2 * dc] = dgb.astype(BF16)
        dproj_ref[:, 2 * dc:3 * dc] = dgc.astype(BF16)

        row = tile * tm + lax.broadcasted_iota(jnp.int32, (tm, 1), 0)
        for gi, w in enumerate(POOL_WINDOWS):
            cols = slice(gi * gcw, (gi + 1) * gcw)
            pooled, cnt = _pool_parts(u[:, cols], ubuf, cols, w, row, tm)
            pooled_b = pooled.astype(BF16)
            pw_b = pw_ref[gi].astype(BF16)
            dyb = dy[:, dc + gi * gcw:dc + (gi + 1) * gcw]
            q = _nn(pooled_b, pw_b)
            dps_ref[:, cols] += jnp.sum(q * dyb, axis=0, keepdims=True)
            dq = (dyb * ps_ref[:, cols]).astype(BF16)
            dpw_ref[gi] += _tn(pooled_b, dq)
            dpooled = _nt(dq, pw_b)
            ebuf[0:tm, cols] = dpooled / cnt
            du = -dpooled
            for s in range(w):
                du = du + ebuf[s:s + tm, cols]
            dproj_ref[:, 3 * dc + gi * gcw:3 * dc + (gi + 1) * gcw] = du.astype(BF16)

        dh = _nn(dproj_ref[...], win[...])
        dg_ref[...] += jnp.sum(dh * xhat, axis=0, keepdims=True)
        dxh = dh * gv
        dx_ref[...] = dxo_v + r * (dxh - xhat * jnp.mean(dxh * xhat, axis=-1, keepdims=True))
        dcbuf[tm:tm + 8, :] = dcbuf[0:8, :]
        ebuf[tm:tm + HALO, :] = ebuf[0:HALO, :]

    tok = lambda i: (n_tiles - 1 - i, 0)
    halo = lambda i: (jnp.maximum((n_tiles - 1 - i) * hb - 1, 0), 0)
    one = lambda i: (0, 0)
    return _launch(
        body, name="mixer_backward", grid=(n_tiles,),
        in_specs=[pl.BlockSpec((tm, d), tok), pl.BlockSpec((tm, d), tok), pl.BlockSpec((1, d), one),
                  pl.BlockSpec((tm, 4 * dc), tok), pl.BlockSpec((HALO, 4 * dc), halo), HBM_SPEC, HBM_SPEC,
                  pl.BlockSpec(conv_w.shape, one), pl.BlockSpec(pool_w.shape, lambda i: (0, 0, 0)), pl.BlockSpec((1, dc), one)],
        out_specs=[pl.BlockSpec((tm, d), tok), pl.BlockSpec((tm, 4 * dc), tok), pl.BlockSpec((tm, d), tok), pl.BlockSpec((tm, d), tok),
                   pl.BlockSpec((1, d), one), pl.BlockSpec(conv_w.shape, one), pl.BlockSpec((1, dc), one),
                   pl.BlockSpec(pool_w.shape, lambda i: (0, 0, 0))],
        out_shape=[jax.ShapeDtypeStruct((t, d), F32), jax.ShapeDtypeStruct((t, 4 * dc), BF16), jax.ShapeDtypeStruct((t, d), BF16),
                   jax.ShapeDtypeStruct((t, d), BF16), jax.ShapeDtypeStruct((1, d), F32), jax.ShapeDtypeStruct(conv_w.shape, F32),
                   jax.ShapeDtypeStruct((1, dc), F32), jax.ShapeDtypeStruct(pool_w.shape, F32)],
        scratch_shapes=[pltpu.VMEM((4 * dc, d), BF16), pltpu.VMEM((2 * dc, d), BF16),
                        pltpu.VMEM((tm + HALO, dc), F32), pltpu.VMEM((tm + HALO, dc), F32),
                        pltpu.VMEM((tm + 8, dc), F32), pltpu.VMEM((tm + HALO, dc), F32), pltpu.SemaphoreType.DMA((1 + N_CHIPS,))],
        args=(dxo, x, g, proj, proj, win_t, wout_x, conv_w, pool_w, pool_scale), cargo=cargo)


def _loss_backward(x, g, target):
    t, d = x.shape
    tm = min(TM_EW, t)

    def body(x_ref, g_ref, t_ref, dx_ref, sq_ref, dg_ref):
        @pl.when(pl.program_id(0) == 0)
        def _():
            sq_ref[...] = jnp.zeros_like(sq_ref)
            dg_ref[...] = jnp.zeros_like(dg_ref)

        xv = x_ref[...]
        gv = g_ref[...]
        r = lax.rsqrt(jnp.mean(xv * xv, axis=-1, keepdims=True) + EPS)
        xhat = xv * r
        err = xhat * gv - t_ref[...]
        sq_ref[...] += jnp.sum(err * err, axis=0, keepdims=True)
        dy = err * (1.0 / d)
        dg_ref[...] += jnp.sum(dy * xhat, axis=0, keepdims=True)
        dxh = dy * gv
        dx_ref[...] = r * (dxh - xhat * jnp.mean(dxh * xhat, axis=-1, keepdims=True))

    tok = lambda i: (i, 0)
    one = lambda i: (0, 0)
    return pl.pallas_call(
        body, name="loss_backward",
        out_shape=[jax.ShapeDtypeStruct((t, d), F32), jax.ShapeDtypeStruct((1, d), F32), jax.ShapeDtypeStruct((1, d), F32)],
        grid=(t // tm,),
        in_specs=[pl.BlockSpec((tm, d), tok), pl.BlockSpec((1, d), one), pl.BlockSpec((tm, d), tok)],
        out_specs=[pl.BlockSpec((tm, d), tok), pl.BlockSpec((1, d), one), pl.BlockSpec((1, d), one)],
        compiler_params=pltpu.CompilerParams(dimension_semantics=("arbitrary",)),
    )(x, g, target)


def _adam_update(w, gv, m, v):
    m_new = ADAM_B1 * m + (1.0 - ADAM_B1) * gv
    v_new = ADAM_B2 * v + (1.0 - ADAM_B2) * (gv * gv)
    m_hat = m_new / (1.0 - ADAM_B1 ** ADAM_STEP)
    v_hat = v_new / (1.0 - ADAM_B2 ** ADAM_STEP)
    return -ADAM_LR * (m_hat / (jnp.sqrt(v_hat) + ADAM_EPS) + ADAM_WD * w), m_new, v_new


def _adamw(w, grad, m, v, name):
    rows, cols = w.shape
    br = _row_block(rows, 256) if rows >= 8 else rows

    def body(w_ref, g_ref, m_ref, v_ref, d_ref, mo_ref, vo_ref):
        d_ref[...], mo_ref[...], vo_ref[...] = _adam_update(w_ref[...], g_ref[...], m_ref[...], v_ref[...])

    blk = pl.BlockSpec((br, cols), lambda i: (i, 0))
    return pl.pallas_call(
        body, name=name,
        out_shape=[jax.ShapeDtypeStruct((rows, cols), F32)] * 3,
        grid=(rows // br,), in_specs=[blk] * 4, out_specs=[blk] * 3,
        compiler_params=pltpu.CompilerParams(dimension_semantics=("parallel",)),
    )(w, grad, m, v)


def _adamw_transposed(w, grad_t, m, v, name):
    _, rows, cols = w.shape
    br = 256 if rows % 256 == 0 else rows

    def body(w_ref, gt_ref, m_ref, v_ref, g_ref, d_ref, mo_ref, vo_ref):
        gv = gt_ref[...].T
        g_ref[...] = gv
        d_ref[...], mo_ref[...], vo_ref[...] = _adam_update(w_ref[...], gv, m_ref[...], v_ref[...])

    blk = pl.BlockSpec((None, br, cols), lambda i: (0, i, 0))
    return pl.pallas_call(
        body, name=name,
        out_shape=[jax.ShapeDtypeStruct((1, rows, cols), F32)] * 4,
        grid=(rows // br,), in_specs=[blk, pl.BlockSpec((cols, br), lambda i: (0, i)), blk, blk], out_specs=[blk] * 4,
        compiler_params=pltpu.CompilerParams(dimension_semantics=("parallel",)),
    )(w, grad_t, m, v)


def _f32_rows_as_bf16(a, rows, cols):
    bits = lax.bitcast_convert_type(a, BF16).reshape(a.shape[0], 2 * a.shape[1])
    return jnp.pad(bits, ((0, rows - bits.shape[0]), (0, cols - bits.shape[1])))


def kernel(x, norm_ffn1, ffn1_w_gate, ffn1_w_up, ffn1_w_down, norm_mix, w_in, conv_w, pool_w, pool_scale, w_out, norm_ffn2, ffn2_w_gate, ffn2_w_up, ffn2_w_down, norm_final, loss_target, m_norm_ffn1, m_ffn1_w_gate, m_ffn1_w_up, m_ffn1_w_down, m_norm_mix, m_w_in, m_conv_w, m_pool_w, m_pool_scale, m_w_out, m_norm_ffn2, m_ffn2_w_gate, m_ffn2_w_up, m_ffn2_w_down, m_norm_final, v_norm_ffn1, v_ffn1_w_gate, v_ffn1_w_up, v_ffn1_w_down, v_norm_mix, v_w_in, v_conv_w, v_pool_w, v_pool_scale, v_w_out, v_norm_ffn2, v_ffn2_w_gate, v_ffn2_w_up, v_ffn2_w_down, v_norm_final):
    weights = dict(norm_ffn1=norm_ffn1, ffn1_w_gate=ffn1_w_gate, ffn1_w_up=ffn1_w_up, ffn1_w_down=ffn1_w_down, norm_mix=norm_mix,
                   w_in=w_in, conv_w=conv_w, pool_w=pool_w, pool_scale=pool_scale, w_out=w_out, norm_ffn2=norm_ffn2,
                   ffn2_w_gate=ffn2_w_gate, ffn2_w_up=ffn2_w_up, ffn2_w_down=ffn2_w_down, norm_final=norm_final)
    first_m = dict(norm_ffn1=m_norm_ffn1, ffn1_w_gate=m_ffn1_w_gate, ffn1_w_up=m_ffn1_w_up, ffn1_w_down=m_ffn1_w_down,
                   norm_mix=m_norm_mix, w_in=m_w_in, conv_w=m_conv_w, pool_w=m_pool_w, pool_scale=m_pool_scale, w_out=m_w_out,
                   norm_ffn2=m_norm_ffn2, ffn2_w_gate=m_ffn2_w_gate, ffn2_w_up=m_ffn2_w_up, ffn2_w_down=m_ffn2_w_down,
                   norm_final=m_norm_final)
    second_m = dict(norm_ffn1=v_norm_ffn1, ffn1_w_gate=v_ffn1_w_gate, ffn1_w_up=v_ffn1_w_up, ffn1_w_down=v_ffn1_w_down,
                    norm_mix=v_norm_mix, w_in=v_w_in, conv_w=v_conv_w, pool_w=v_pool_w, pool_scale=v_pool_scale, w_out=v_w_out,
                    norm_ffn2=v_norm_ffn2, ffn2_w_gate=v_ffn2_w_gate, ffn2_w_up=v_ffn2_w_up, ffn2_w_down=v_ffn2_w_down,
                    norm_final=v_norm_final)
    names = list(weights)

    xs = x[0]
    tgt = loss_target[0]
    t, d = xs.shape
    dc = pool_scale.shape[1]
    cx, cy, cc = _my_place()
    chip = 2 * cx + cy
    place = jnp.stack([chip, cc]).astype(jnp.int32)

    conv_rows = 32
    wout_x = jnp.concatenate([w_out[0].astype(BF16), _f32_rows_as_bf16(conv_w[0], conv_rows, d)], axis=0)
    ffn1_shards = [ffn1_w_gate[0].T.astype(BF16), ffn1_w_up[0].T.astype(BF16), ffn1_w_down[0].astype(BF16)]
    mid_shards = [w_in[0].T.astype(BF16), wout_x, ffn2_w_gate[0].T.astype(BF16), ffn2_w_up[0].T.astype(BF16)]
    last_shards = [ffn2_w_down[0].astype(BF16)]

    g1, gm, g2 = norm_ffn1, norm_mix, norm_ffn2
    gf = norm_final.reshape(1, d)
    pw = pool_w[0]

    wg1, wu1, wd1 = _run_cargo(_gather_cargo(ffn1_shards), "gather_ffn1")
    (x1, a1, b1), [(win_t, wout_g, wg2, wu2)] = _ffn_forward(xs, g1, wg1, wu1, wd1, "ffn1_forward", [_gather_cargo(mid_shards)])
    wo_rows = w_out.shape[1]
    cshard = conv_w.shape[2]
    conv_bits = wout_g.reshape(N_CHIPS, wo_rows + conv_rows, d)[:, wo_rows:wo_rows + conv_w.shape[1], :2 * cshard]
    conv_full = lax.bitcast_convert_type(conv_bits.reshape(N_CHIPS, conv_w.shape[1], cshard, 2), F32)
    conv_full = jnp.transpose(conv_full, (1, 0, 2)).reshape(conv_w.shape[1], N_CHIPS * cshard)
    (x2, proj, ymix), [(wd2,)] = _mixer_forward(x1, gm, win_t, wout_g, conv_full, pw, pool_scale, [_gather_cargo(last_shards)])
    (x3, a2, b2), _ = _ffn_forward(x2, g2, wg2, wu2, wd2, "ffn2_forward")

    dx3, sq_cols, dgf = _loss_backward(x3, gf, tgt)

    (dx2, da2, db2, s2, h3, do2, dg2), _ = _ffn_backward(dx3, x2, g2, a2, b2, wg2, wu2, wd2, "ffn2_backward")
    p_wg2, _ = _weight_grad(da2, h3, "ffn2_gate_grad")
    p_wu2, [(x_wg2,)] = _weight_grad(db2, h3, "ffn2_up_grad", [_exchange_cargo([p_wg2])])
    p_wd2, [(x_wu2,)] = _weight_grad(s2, do2, "ffn2_down_grad", [_exchange_cargo([p_wu2])])

    (dx1, dproj, h2, dx2b, dgm, dcw, dps, dpw), [(x_wd2,)] = _mixer_backward(
        dx2, x1, gm, proj, win_t, wout_g, conv_full, pw, pool_scale, [_exchange_cargo([p_wd2])])
    p_win, _ = _weight_grad(dproj, h2, "w_in_grad")
    p_wout, [(x_win,)] = _weight_grad(ymix, dx2b, "w_out_grad", [_exchange_cargo([p_win])])

    (dx0, da1, db1, s1, h1, do1, dg1), _ = _ffn_backward(dx1, xs, g1, a1, b1, wg1, wu1, wd1, "ffn1_backward")

    npw = pw.size // d
    head = [dg1, dgm, dg2, dgf, jnp.pad(dps, ((0, 0), (0, d - dc))), jnp.pad(dcw, ((0, 0), (0, d - dc))), sq_cols]
    n_head = sum(h.shape[0] for h in head)
    base = -(-n_head // 8) * 8
    pack = jnp.concatenate(head + [jnp.zeros((base - n_head, d), F32), dpw.reshape(npw, d)], axis=0)

    p_wg1, [(x_wout,), (packs,)] = _weight_grad(da1, h1, "ffn1_gate_grad", [_exchange_cargo([p_wout]), _all_gather_small_cargo(pack)])
    p_wu1, [(x_wg1,)] = _weight_grad(db1, h1, "ffn1_up_grad", [_exchange_cargo([p_wg1])])
    p_wd1, [(x_wu1,)] = _weight_grad(s1, do1, "ffn1_down_grad", [_exchange_cargo([p_wu1])])
    x_wd1, = _run_cargo(_exchange_cargo([p_wd1]), "grad_exchange_last")
    small = _sum_by_device(packs)
    loss = jnp.sum(small[n_head - 1]) * (0.5 / d)

    order = ["wg1", "wu1", "wd1", "win", "wout", "wg2", "wu2", "wd2"]
    pairs = dict(wg1=p_wg1, wu1=p_wu1, wd1=p_wd1, win=p_win, wout=p_wout, wg2=p_wg2, wu2=p_wu2, wd2=p_wd2)
    landed = dict(wg1=x_wg1, wu1=x_wu1, wd1=x_wd1, win=x_win, wout=x_wout, wg2=x_wg2, wu2=x_wu2, wd2=x_wd2)
    both = _sibling_share([_chip_sum(pairs[k], landed[k], place, k) for k in order])
    rwg1, rwu1, rwd1, rwin, rwout, rwg2, rwu2, rwd2 = [b.reshape(2 * b.shape[1], b.shape[2]) for b in both]

    grads = {
        "norm_ffn1": small[0:1], "norm_mix": small[1:2], "norm_ffn2": small[2:3], "norm_final": small[3],
        "pool_scale": small[4:5, :dc],
        "conv_w": lax.dynamic_slice_in_dim(small[5:5 + dcw.shape[0], :dc], chip * cshard, cshard, axis=1)[None],
        "pool_w": small[base:].reshape(pool_w.shape),
        "ffn1_w_down": rwd1[None], "w_out": rwout[None], "ffn2_w_down": rwd2[None],
    }
    transposed = {"ffn1_w_gate": rwg1, "ffn1_w_up": rwu1, "w_in": rwin, "ffn2_w_gate": rwg2, "ffn2_w_up": rwu2}

    deltas, new_m, new_v = {}, {}, {}
    for n in names:
        w = weights[n]
        shape = w.shape
        if n in transposed:
            grads[n], deltas[n], new_m[n], new_v[n] = _adamw_transposed(w, transposed[n], first_m[n], second_m[n], "adamw_" + n)
            continue
        as2d = (lambda a: a.reshape(-1, shape[-1]))
        dl, mo, vo = _adamw(as2d(w), as2d(grads[n]), as2d(first_m[n]), as2d(second_m[n]), "adamw_" + n)
        deltas[n], new_m[n], new_v[n] = dl.reshape(shape), mo.reshape(shape), vo.reshape(shape)
        grads[n] = grads[n].reshape(shape)

    return (loss, dx0[None], *[grads[n] for n in names], *[deltas[n] for n in names],
            *[new_m[n] for n in names], *[new_v[n] for n in names])
```

```python
import functools

import jax
import jax.numpy as jnp
from jax import lax
from jax.experimental import pallas as pl
from jax.experimental.pallas import tpu as pltpu

F32 = jnp.float32
BF16 = jnp.bfloat16
MESH = pl.DeviceIdType.MESH

EPS = 1e-6
POOL_WINDOWS = (2, 4, 8, 16)
ADAM_LR = 0.001
ADAM_B1 = 0.9
ADAM_B2 = 0.999
ADAM_EPS = 1e-08
ADAM_WD = 0.01
ADAM_STEP = 10

N_CHIPS = 4
N_DEVICES = 8
MXU_COLS_V7X = 256
VMEM_LIMIT = 56 * 1024 * 1024
TM_FFN = 512
TM_MIX = 512
TM_TN = 1024
TM_EW = 512
HALO = 16


def _nt(a, b):
    return lax.dot_general(a, b, (((1,), (1,)), ((), ())), preferred_element_type=F32)


def _tn(a, b):
    return lax.dot_general(a, b, (((0,), (0,)), ((), ())), preferred_element_type=F32)


def _nn(a, b):
    return jnp.dot(a, b, preferred_element_type=F32)


def _sigmoid(a):
    return 1.0 / (1.0 + jnp.exp(-a))


def _feature_chunks(n):
    assert n % MXU_COLS_V7X == 0
    tiles = n // MXU_COLS_V7X
    first = (tiles + 1) // 2
    sizes = [first * MXU_COLS_V7X, (tiles - first) * MXU_COLS_V7X]
    out, s0 = [], 0
    for sz in sizes:
        if sz:
            out.append((s0, sz))
            s0 += sz
    return out


def _row_block(rows, cap):
    best = 8
    for b in range(8, min(rows, cap) + 1, 8):
        if rows % b == 0:
            best = b
    assert rows % best == 0
    return best


def _my_place():
    return lax.axis_index("x"), lax.axis_index("y"), lax.axis_index("c")


def _other_chips(x, y):
    return [(1 - x, y), (x, 1 - y), (1 - x, 1 - y)]


HBM_SPEC = pl.BlockSpec(memory_space=pltpu.HBM)


class _Cargo:
    def __init__(self, operands, out_shapes, n_sems, start, finish):
        self.operands, self.out_shapes, self.n_sems, self.start, self.finish = list(operands), list(out_shapes), n_sems, start, finish


def _launch(body, *, name, grid, in_specs, out_specs, out_shape, scratch_shapes, args, cargo=()):
    params = pltpu.CompilerParams(dimension_semantics=("arbitrary",) * len(grid), vmem_limit_bytes=VMEM_LIMIT)
    cargos = list(cargo)
    c_operands = [op for cg in cargos for op in cg.operands]
    c_shapes = [sh for cg in cargos for sh in cg.out_shapes]
    counts = [len(in_specs), len(c_operands), len(out_shape), len(c_shapes), len(scratch_shapes), 2 * len(cargos)]

    def carrying(*refs):
        groups, pos = [], 0
        for k in counts:
            groups.append(refs[pos:pos + k])
            pos += k
        ins, c_ins, outs, c_outs, scratch, sems = groups
        parts, pi, po = [], 0, 0
        for n, cg in enumerate(cargos):
            parts.append((c_ins[pi:pi + len(cg.operands)], c_outs[po:po + len(cg.out_shapes)], sems[2 * n], sems[2 * n + 1]))
            pi += len(cg.operands)
            po += len(cg.out_shapes)
        ids = [pl.program_id(ax) for ax in range(len(grid))]
        first = functools.reduce(jnp.logical_and, [i == 0 for i in ids])
        last = functools.reduce(jnp.logical_and, [i == g - 1 for i, g in zip(ids, grid)])

        if cargos:
            @pl.when(first)
            def _():
                for cg, part in zip(cargos, parts):
                    cg.start(*part)

        body(*ins, *outs, *scratch)

        if cargos:
            @pl.when(last)
            def _():
                for cg, part in zip(cargos, parts):
                    cg.finish(*part)

    sems = [pltpu.SemaphoreType.DMA((cg.n_sems,)) for cg in cargos for _ in range(2)]
    outs = pl.pallas_call(
        carrying, name=name, grid=grid,
        in_specs=list(in_specs) + [HBM_SPEC] * counts[1], out_specs=list(out_specs) + [HBM_SPEC] * counts[3],
        out_shape=list(out_shape) + c_shapes, scratch_shapes=list(scratch_shapes) + sems,
        compiler_params=params)(*args, *c_operands)
    own, rest = list(outs[:counts[2]]), list(outs[counts[2]:])
    carried, po = [], 0
    for cg in cargos:
        carried.append(rest[po:po + len(cg.out_shapes)])
        po += len(cg.out_shapes)
    return own, carried


def _run_cargo(cargo, name):
    n_in, n_out = len(cargo.operands), len(cargo.out_shapes)

    def body(*refs):
        c_ins, c_outs, sems = refs[:n_in], refs[n_in:n_in + n_out], refs[n_in + n_out:]
        cargo.start(c_ins, c_outs, *sems)
        cargo.finish(c_ins, c_outs, *sems)

    sem = pltpu.SemaphoreType.DMA((cargo.n_sems,))
    return list(pl.pallas_call(body, name=name, out_shape=cargo.out_shapes, in_specs=[HBM_SPEC] * n_in,
                               out_specs=[HBM_SPEC] * n_out, scratch_shapes=[sem, sem])(*cargo.operands))


def _gather_cargo(shards):
    n = len(shards)
    for s in shards:
        assert s.shape[0] % 32 == 0

    def steps(ins, outs, send_sems, recv_sems):
        x, y, c = _my_place()
        sibling = (x, y, 1 - c)
        chips = _other_chips(x, y)
        mine = 2 * x + y

        def rows_of(a, chip_index, half):
            rps = shards[a].shape[0]
            hr = rps // 2
            return outs[a].at[pl.ds(pl.multiple_of(chip_index * rps + half * hr, 16), hr), :]

        def remote(a, slot, src, dst, to):
            return pltpu.make_async_remote_copy(
                src_ref=src, dst_ref=dst, send_sem=send_sems.at[a * 7 + slot], recv_sem=recv_sems.at[a * 7 + slot],
                device_id=to, device_id_type=MESH)

        def own_copy(a):
            rps = shards[a].shape[0]
            return remote(a, 6, ins[a], outs[a].at[pl.ds(pl.multiple_of(mine * rps, 16), rps), :], sibling)

        def my_half(a):
            hr = shards[a].shape[0] // 2
            return ins[a].at[pl.ds(pl.multiple_of(c * hr, 16), hr), :]

        def start():
            for a in range(n):
                own_copy(a).start()
                for j, chip in enumerate(chips):
                    remote(a, j, my_half(a), rows_of(a, mine, c), (*chip, c)).start()

        def finish():
            for a in range(n):
                for j, chip in enumerate(chips):
                    landed = rows_of(a, 2 * chip[0] + chip[1], c)
                    remote(a, j, landed, landed, (*chip, c)).wait_recv()
                    remote(a, 3 + j, landed, landed, sibling).start()
            for a in range(n):
                for j, chip in enumerate(chips):
                    from_sibling = rows_of(a, 2 * chip[0] + chip[1], 1 - c)
                    remote(a, 3 + j, from_sibling, from_sibling, sibling).wait_recv()
            for a in range(n):
                for j, chip in enumerate(chips):
                    remote(a, j, my_half(a), rows_of(a, mine, c), (*chip, c)).wait_send()
                    landed = rows_of(a, 2 * chip[0] + chip[1], c)
                    remote(a, 3 + j, landed, landed, sibling).wait_send()
                own_copy(a).wait()

        return start, finish

    return _Cargo(shards, [jax.ShapeDtypeStruct((N_CHIPS * s.shape[0], s.shape[1]), s.dtype) for s in shards], 7 * n,
                  lambda *r: steps(*r)[0](), lambda *r: steps(*r)[1]())


def _exchange_cargo(pairs):
    n = len(pairs)

    def copies(ins, outs, send_sems, recv_sems):
        x, y, c = _my_place()
        return [pltpu.make_async_remote_copy(
            src_ref=ins[a].at[2 * chip[0] + chip[1]], dst_ref=outs[a].at[j],
            send_sem=send_sems.at[3 * a + j], recv_sem=recv_sems.at[3 * a + j], device_id=(*chip, c), device_id_type=MESH)
            for a in range(n) for j, chip in enumerate(_other_chips(x, y))]

    def start(*r):
        for cp in copies(*r):
            cp.start()

    def finish(*r):
        for cp in copies(*r):
            cp.wait()

    return _Cargo(pairs, [jax.ShapeDtypeStruct((3,) + p.shape[1:], p.dtype) for p in pairs], 3 * n, start, finish)


def _all_gather_small_cargo(pack):
    rows, cols = pack.shape

    def copies(ins, outs, send_sems, recv_sems):
        x, y, c = _my_place()
        me = 4 * x + 2 * y + c
        remote = []
        for f in range(1, N_DEVICES):
            fx, fy, fc = (f >> 2) & 1, (f >> 1) & 1, f & 1
            to = (1 - x if fx else x, 1 - y if fy else y, 1 - c if fc else c)
            remote.append(pltpu.make_async_remote_copy(
                src_ref=ins[0], dst_ref=outs[0].at[me], send_sem=send_sems.at[f - 1], recv_sem=recv_sems.at[f - 1],
                device_id=to, device_id_type=MESH))
        own = pltpu.make_async_copy(ins[0], outs[0].at[me], send_sems.at[N_DEVICES - 1])
        return remote, own

    def start(*r):
        remote, own = copies(*r)
        own.start()
        for cp in remote:
            cp.start()

    def finish(*r):
        remote, own = copies(*r)
        for cp in remote:
            cp.wait()
        own.wait()

    return _Cargo([pack], [jax.ShapeDtypeStruct((N_DEVICES, rows, cols), F32)], N_DEVICES, start, finish)


def _sum_by_device(packs):
    n, rows, cols = packs.shape

    def body(p_ref, o_ref):
        acc = p_ref[0]
        for dev in range(1, n):
            acc = acc + p_ref[dev]
        o_ref[...] = acc

    return pl.pallas_call(body, name="small_grads_sum", out_shape=jax.ShapeDtypeStruct((rows, cols), F32))(packs)


def _chip_sum(pair, got, place, tag):
    _, hr, cols = pair.shape
    br = _row_block(hr, 256)

    def body(k_ref, p_ref, r_ref, o_ref):
        acc = p_ref[...].astype(F32)
        for j in range(3):
            acc = acc + r_ref[j].astype(F32)
        o_ref[...] = acc

    return pl.pallas_call(
        body, name="grad_chip_sum_" + tag,
        out_shape=jax.ShapeDtypeStruct((2, hr, cols), F32),
        grid_spec=pltpu.PrefetchScalarGridSpec(
            num_scalar_prefetch=1, grid=(hr // br,),
            in_specs=[pl.BlockSpec((None, br, cols), lambda r, k_ref: (k_ref[0], r, 0)),
                      pl.BlockSpec((3, br, cols), lambda r, k_ref: (0, r, 0))],
            out_specs=pl.BlockSpec((None, br, cols), lambda r, k_ref: (k_ref[1], r, 0))),
        compiler_params=pltpu.CompilerParams(dimension_semantics=("parallel",)),
    )(place, pair, got)


def _sibling_share(halves):
    n = len(halves)

    def body(*refs):
        outs = refs[n:2 * n]
        send_sems, recv_sems = refs[2 * n:]
        x, y, c = _my_place()
        copies = []
        for a in range(n):
            cp = pltpu.make_async_remote_copy(
                src_ref=outs[a].at[c], dst_ref=outs[a].at[c], send_sem=send_sems.at[a], recv_sem=recv_sems.at[a],
                device_id=(x, y, 1 - c), device_id_type=MESH)
            cp.start()
            copies.append(cp)
        for cp in copies:
            cp.wait()

    return pl.pallas_call(
        body, name="grad_share_sibling",
        out_shape=[jax.ShapeDtypeStruct(h.shape, h.dtype) for h in halves],
        in_specs=[HBM_SPEC] * n, out_specs=[HBM_SPEC] * n,
        input_output_aliases={a: a for a in range(n)},
        scratch_shapes=[pltpu.SemaphoreType.DMA((n,)), pltpu.SemaphoreType.DMA((n,))],
    )(*halves)


def _load_rows(pairs, sems):
    cps = [pltpu.make_async_copy(src, dst, sems.at[j]) for j, (src, dst) in enumerate(pairs)]
    for cp in cps:
        cp.start()
    for cp in cps:
        cp.wait()


def _piece_rows(weights):
    flat = [p for pieces in weights for p in pieces]

    def copies(refs, mats):
        out, n = [], 0
        for pieces, mat in zip(weights, mats):
            rps = sum(p.shape[0] for p in pieces) // N_CHIPS
            off = 0
            for p in pieces:
                r = p.shape[0] // N_CHIPS
                if len(pieces) == 1:
                    out.append((refs[n], mat))
                else:
                    for k in range(N_CHIPS):
                        out.append((refs[n].at[pl.ds(k * r, r), :], mat.at[pl.ds(k * rps + off, r), :]))
                off += r
                n += 1
        return out

    n_copies = sum(1 if len(pieces) == 1 else N_CHIPS * len(pieces) for pieces in weights)
    return flat, copies, n_copies


def _ffn_forward(x, g, wg_t, wu_t, wd, name, cargo=()):
    t, d = x.shape
    f = sum(p.shape[0] for p in wd)
    tm = min(TM_FFN, t)
    chunks = _feature_chunks(f)
    flat, copies, n_copies = _piece_rows([wg_t, wu_t, wd])
    nw = len(flat)

    def body(x_ref, g_ref, *rest):
        w_hbm, (xo_ref, a_ref, b_ref, wg, wu, wdn, sems) = rest[:nw], rest[nw:]

        @pl.when(pl.program_id(0) == 0)
        def _():
            _load_rows(copies(w_hbm, [wg, wu, wdn]), sems)

        xv = x_ref[...]
        r = lax.rsqrt(jnp.mean(xv * xv, axis=-1, keepdims=True) + EPS)
        h = (xv * r * g_ref[...]).astype(BF16)
        acc = jnp.zeros((tm, d), F32)
        for s0, sz in chunks:
            a = _nt(h, wg[s0:s0 + sz, :])
            b = _nt(h, wu[s0:s0 + sz, :])
            a_ref[:, s0:s0 + sz] = a.astype(BF16)
            b_ref[:, s0:s0 + sz] = b.astype(BF16)
            s = (a * _sigmoid(a) * b).astype(BF16)
            acc = acc + _nn(s, wdn[s0:s0 + sz, :])
        xo_ref[...] = xv + 0.5 * acc

    tok = lambda i: (i, 0)
    return _launch(
        body, name=name, grid=(t // tm,),
        in_specs=[pl.BlockSpec((tm, d), tok), pl.BlockSpec((1, d), lambda i: (0, 0))] + [HBM_SPEC] * nw,
        out_specs=[pl.BlockSpec((tm, d), tok), pl.BlockSpec((tm, f), tok), pl.BlockSpec((tm, f), tok)],
        out_shape=[jax.ShapeDtypeStruct((t, d), F32), jax.ShapeDtypeStruct((t, f), BF16), jax.ShapeDtypeStruct((t, f), BF16)],
        scratch_shapes=[pltpu.VMEM((f, d), BF16), pltpu.VMEM((f, d), BF16), pltpu.VMEM((f, d), BF16), pltpu.SemaphoreType.DMA((n_copies,))],
        args=(x, g, *flat), cargo=cargo)


def _ffn_backward(dxo, x, g, a, b, wg_t, wu_t, wd, name, cargo=()):
    t, d = x.shape
    f = sum(p.shape[0] for p in wd)
    tm = min(TM_FFN // 2, t)
    chunks = _feature_chunks(f)
    flat, copies, n_copies = _piece_rows([wg_t, wu_t, wd])
    nw = len(flat)

    def body(dxo_ref, x_ref, g_ref, a_ref, b_ref, *rest):
        w_hbm, (dx_ref, da_ref, db_ref, s_ref, h_ref, do_ref, dg_ref, wg, wu, wdn, sems) = rest[:nw], rest[nw:]

        @pl.when(pl.program_id(0) == 0)
        def _():
            _load_rows(copies(w_hbm, [wg, wu, wdn]), sems)
            dg_ref[...] = jnp.zeros_like(dg_ref)

        xv = x_ref[...]
        gv = g_ref[...]
        r = lax.rsqrt(jnp.mean(xv * xv, axis=-1, keepdims=True) + EPS)
        xhat = xv * r
        h_ref[...] = (xhat * gv).astype(BF16)
        dxo_v = dxo_ref[...]
        dout = (0.5 * dxo_v).astype(BF16)
        do_ref[...] = dout
        dh = jnp.zeros((tm, d), F32)
        for s0, sz in chunks:
            ds = _nt(dout, wdn[s0:s0 + sz, :])
            av = a_ref[:, s0:s0 + sz].astype(F32)
            bv = b_ref[:, s0:s0 + sz].astype(F32)
            sig = _sigmoid(av)
            silu = av * sig
            s_ref[:, s0:s0 + sz] = (silu * bv).astype(BF16)
            da = (ds * bv * (sig * (1.0 + av * (1.0 - sig)))).astype(BF16)
            db = (ds * silu).astype(BF16)
            da_ref[:, s0:s0 + sz] = da
            db_ref[:, s0:s0 + sz] = db
            dh = dh + _nn(da, wg[s0:s0 + sz, :]) + _nn(db, wu[s0:s0 + sz, :])
        dg_ref[...] += jnp.sum(dh * xhat, axis=0, keepdims=True)
        dxh = dh * gv
        dx_ref[...] = dxo_v + r * (dxh - xhat * jnp.mean(dxh * xhat, axis=-1, keepdims=True))

    tok = lambda i: (i, 0)
    one = lambda i: (0, 0)
    return _launch(
        body, name=name, grid=(t // tm,),
        in_specs=[pl.BlockSpec((tm, d), tok), pl.BlockSpec((tm, d), tok), pl.BlockSpec((1, d), one),
                  pl.BlockSpec((tm, f), tok), pl.BlockSpec((tm, f), tok)] + [HBM_SPEC] * nw,
        out_specs=[pl.BlockSpec((tm, d), tok), pl.BlockSpec((tm, f), tok), pl.BlockSpec((tm, f), tok), pl.BlockSpec((tm, f), tok),
                   pl.BlockSpec((tm, d), tok), pl.BlockSpec((tm, d), tok), pl.BlockSpec((1, d), one)],
        out_shape=[jax.ShapeDtypeStruct((t, d), F32), jax.ShapeDtypeStruct((t, f), BF16), jax.ShapeDtypeStruct((t, f), BF16),
                   jax.ShapeDtypeStruct((t, f), BF16), jax.ShapeDtypeStruct((t, d), BF16), jax.ShapeDtypeStruct((t, d), BF16),
                   jax.ShapeDtypeStruct((1, d), F32)],
        scratch_shapes=[pltpu.VMEM((f, d), BF16), pltpu.VMEM((f, d), BF16), pltpu.VMEM((f, d), BF16), pltpu.SemaphoreType.DMA((n_copies,))],
        args=(dxo, x, g, a, b, *flat), cargo=cargo)


def _weight_grad(lhs, rhs, name, cargo=()):
    t, m = lhs.shape
    d = rhs.shape[1]
    tm = min(TM_TN, t)
    nt = t // tm
    nj = 2 if (m // 2) % 128 == 0 and m > 1024 else 1
    bm = m // nj
    cpb = N_CHIPS // nj
    rps = m // N_CHIPS
    hr = rps // 2
    assert hr % 16 == 0

    def body(l_ref, r_ref, o_ref, acc, stage, recv, send_sems, recv_sems):
        j = pl.program_id(0)
        i = pl.program_id(1)
        @pl.when(i == 0)
        def _():
            acc[...] = jnp.zeros_like(acc)

        acc[...] += _tn(l_ref[...], r_ref[...])

        def pair_sum(jj):
            x, y, c = _my_place()
            copies = []
            for q in range(cpb):
                slot = jj * cpb + q
                stage[slot] = acc[pl.ds(pl.multiple_of(q * rps + (1 - c) * hr, 16), hr), :].astype(BF16)
                cp = pltpu.make_async_remote_copy(
                    src_ref=stage.at[slot], dst_ref=recv.at[slot], send_sem=send_sems.at[slot], recv_sem=recv_sems.at[slot],
                    device_id=(x, y, 1 - c), device_id_type=MESH)
                cp.start()
                copies.append(cp)
            for q, cp in enumerate(copies):
                cp.wait_recv()
                mine = acc[pl.ds(pl.multiple_of(q * rps + c * hr, 16), hr), :]
                o_ref[q] = (mine + recv[jj * cpb + q].astype(F32)).astype(BF16)
            for cp in copies:
                cp.wait_send()

        for jj in range(nj):
            @pl.when(jnp.logical_and(i == nt - 1, j == jj))
            def _():
                pair_sum(jj)

    outs, carried = _launch(
        body, name=name, grid=(nj, nt),
        in_specs=[pl.BlockSpec((tm, bm), lambda j, i: (i, j)), pl.BlockSpec((tm, d), lambda j, i: (i, 0))],
        out_specs=[pl.BlockSpec((cpb, hr, d), lambda j, i: (j, 0, 0))],
        out_shape=[jax.ShapeDtypeStruct((N_CHIPS, hr, d), BF16)],
        scratch_shapes=[pltpu.VMEM((bm, d), F32), pltpu.VMEM((N_CHIPS, hr, d), BF16), pltpu.VMEM((N_CHIPS, hr, d), BF16),
                        pltpu.SemaphoreType.DMA((N_CHIPS,)), pltpu.SemaphoreType.DMA((N_CHIPS,))],
        args=(lhs, rhs), cargo=cargo)
    return outs[0], carried


def _pool_parts(u_cols, ubuf, cols, w, row, tm):
    ws = u_cols
    for s in range(1, w):
        ws = ws + ubuf[HALO - s:HALO - s + tm, cols]
    cnt = jnp.minimum(row + 1, w).astype(F32)
    return ws / cnt - u_cols, cnt


def _mixer_forward(x, g, win_t, wout_x, conv_w, pool_w, pool_scale, cargo=()):
    t, d = x.shape
    dc = win_t.shape[0] // 4
    gcw = dc // len(POOL_WINDOWS)
    wo_rows = d // N_CHIPS
    wo_stride = wout_x.shape[0] // N_CHIPS
    tm = min(TM_MIX, t)

    def body(x_ref, g_ref, win_hbm, wout_hbm, cw_ref, pw_ref, ps_ref, xo_ref, proj_ref, y_ref,
             win, wout, zbuf, ubuf, sems):
        i = pl.program_id(0)

        @pl.when(i == 0)
        def _():
            pairs = [(win_hbm, win)]
            for k in range(N_CHIPS):
                pairs.append((wout_hbm.at[pl.ds(k * wo_stride, wo_rows), :], wout.at[pl.ds(k * wo_rows, wo_rows), :]))
            _load_rows(pairs, sems)
            zbuf[0:8, :] = jnp.zeros((8, dc), F32)
            ubuf[0:HALO, :] = jnp.zeros((HALO, dc), F32)

        xv = x_ref[...]
        r = lax.rsqrt(jnp.mean(xv * xv, axis=-1, keepdims=True) + EPS)
        h = (xv * r * g_ref[...]).astype(BF16)
        v = _nt(h, win[0:dc, :])
        gb = _nt(h, win[dc:2 * dc, :])
        gc = _nt(h, win[2 * dc:3 * dc, :])
        u = _nt(h, win[3 * dc:4 * dc, :])
        proj_ref[:, 0:dc] = v.astype(BF16)
        proj_ref[:, dc:2 * dc] = gb.astype(BF16)
        proj_ref[:, 2 * dc:3 * dc] = gc.astype(BF16)
        proj_ref[:, 3 * dc:4 * dc] = u.astype(BF16)

        z = gc * v
        zbuf[8:8 + tm, :] = z
        cw = cw_ref[...]
        conv = cw[2:3, :] * z + cw[1:2, :] * zbuf[7:7 + tm, :] + cw[0:1, :] * zbuf[6:6 + tm, :]
        y_ref[:, 0:dc] = (gb * conv).astype(BF16)

        ubuf[HALO:HALO + tm, :] = u
        row = i * tm + lax.broadcasted_iota(jnp.int32, (tm, 1), 0)
        for gi, w in enumerate(POOL_WINDOWS):
            cols = slice(gi * gcw, (gi + 1) * gcw)
            pooled, _ = _pool_parts(u[:, cols], ubuf, cols, w, row, tm)
            yb = _nn(pooled.astype(BF16), pw_ref[gi].astype(BF16)) * ps_ref[:, cols]
            y_ref[:, dc + gi * gcw:dc + (gi + 1) * gcw] = yb.astype(BF16)

        xo_ref[...] = xv + _nn(y_ref[...], wout[...])
        zbuf[0:8, :] = zbuf[tm:tm + 8, :]
        ubuf[0:HALO, :] = ubuf[tm:tm + HALO, :]

    tok = lambda i: (i, 0)
    one = lambda i: (0, 0)
    return _launch(
        body, name="mixer_forward", grid=(t // tm,),
        in_specs=[pl.BlockSpec((tm, d), tok), pl.BlockSpec((1, d), one), HBM_SPEC, HBM_SPEC,
                  pl.BlockSpec(conv_w.shape, one), pl.BlockSpec(pool_w.shape, lambda i: (0, 0, 0)), pl.BlockSpec((1, dc), one)],
        out_specs=[pl.BlockSpec((tm, d), tok), pl.BlockSpec((tm, 4 * dc), tok), pl.BlockSpec((tm, 2 * dc), tok)],
        out_shape=[jax.ShapeDtypeStruct((t, d), F32), jax.ShapeDtypeStruct((t, 4 * dc), BF16), jax.ShapeDtypeStruct((t, 2 * dc), BF16)],
        scratch_shapes=[pltpu.VMEM((4 * dc, d), BF16), pltpu.VMEM((2 * dc, d), BF16),
                        pltpu.VMEM((tm + 8, dc), F32), pltpu.VMEM((tm + HALO, dc), F32), pltpu.SemaphoreType.DMA((1 + N_CHIPS,))],
        args=(x, g, win_t, wout_x, conv_w, pool_w, pool_scale), cargo=cargo)


def _mixer_backward(dxo, x, g, proj, win_t, wout_x, conv_w, pool_w, pool_scale, cargo=()):
    t, d = x.shape
    dc = win_t.shape[0] // 4
    ng = len(POOL_WINDOWS)
    gcw = dc // ng
    wo_rows = d // N_CHIPS
    wo_stride = wout_x.shape[0] // N_CHIPS
    tm = min(TM_MIX // 2, t)
    n_tiles = t // tm
    hb = tm // HALO

    def body(dxo_ref, x_ref, g_ref, proj_ref, halo_ref, win_hbm, wout_hbm, cw_ref, pw_ref, ps_ref,
             dx_ref, dproj_ref, h_ref, dxob_ref, dg_ref, dcw_ref, dps_ref, dpw_ref,
             win, wout, zbuf, ubuf, dcbuf, ebuf, sems):
        i = pl.program_id(0)
        tile = n_tiles - 1 - i

        @pl.when(i == 0)
        def _():
            pairs = [(win_hbm, win)]
            for k in range(N_CHIPS):
                pairs.append((wout_hbm.at[pl.ds(k * wo_stride, wo_rows), :], wout.at[pl.ds(k * wo_rows, wo_rows), :]))
            _load_rows(pairs, sems)
            dcbuf[tm:tm + 8, :] = jnp.zeros((8, dc), F32)
            ebuf[tm:tm + HALO, :] = jnp.zeros((HALO, dc), F32)
            dg_ref[...] = jnp.zeros_like(dg_ref)
            dcw_ref[...] = jnp.zeros_like(dcw_ref)
            dps_ref[...] = jnp.zeros_like(dps_ref)
            dpw_ref[...] = jnp.zeros_like(dpw_ref)

        xv = x_ref[...]
        gv = g_ref[...]
        r = lax.rsqrt(jnp.mean(xv * xv, axis=-1, keepdims=True) + EPS)
        xhat = xv * r
        h_ref[...] = (xhat * gv).astype(BF16)
        dxo_v = dxo_ref[...]
        dxo_b = dxo_v.astype(BF16)
        dxob_ref[...] = dxo_b

        v = proj_ref[:, 0:dc].astype(F32)
        gb = proj_ref[:, dc:2 * dc].astype(F32)
        gc = proj_ref[:, 2 * dc:3 * dc].astype(F32)
        u = proj_ref[:, 3 * dc:4 * dc].astype(F32)
        first = jnp.where(tile > 0, 1.0, 0.0)
        zbuf[0:HALO, :] = halo_ref[:, 2 * dc:3 * dc].astype(F32) * halo_ref[:, 0:dc].astype(F32) * first
        ubuf[0:HALO, :] = halo_ref[:, 3 * dc:4 * dc].astype(F32) * first
        z = gc * v
        zbuf[HALO:HALO + tm, :] = z
        ubuf[HALO:HALO + tm, :] = u
        z1 = zbuf[HALO - 1:HALO - 1 + tm, :]
        z2 = zbuf[HALO - 2:HALO - 2 + tm, :]
        cw = cw_ref[...]
        conv = cw[2:3, :] * z + cw[1:2, :] * z1 + cw[0:1, :] * z2

        dy = _nt(dxo_b, wout[...])
        dya = dy[:, 0:dc]
        dgb = dya * conv
        dconv = dya * gb
        dcbuf[0:tm, :] = dconv
        dz = cw[2:3, :] * dconv + cw[1:2, :] * dcbuf[1:1 + tm, :] + cw[0:1, :] * dcbuf[2:2 + tm, :]
        dgc = dz * v
        dv = dz * gc
        dcw_ref[0:1, :] += jnp.sum(dconv * z2, axis=0, keepdims=True)
        dcw_ref[1:2, :] += jnp.sum(dconv * z1, axis=0, keepdims=True)
        dcw_ref[2:3, :] += jnp.sum(dconv * z, axis=0, keepdims=True)

        dproj_ref[:, 0:dc] = dv.astype(BF16)
        dproj_ref[:, dc:2 * dc] = dgb.astype(BF16)
        dproj_ref[:, 2 * dc:3 * dc] = dgc.astype(BF16)

        row = tile * tm + lax.broadcasted_iota(jnp.int32, (tm, 1), 0)
        for gi, w in enumerate(POOL_WINDOWS):
            cols = slice(gi * gcw, (gi + 1) * gcw)
            pooled, cnt = _pool_parts(u[:, cols], ubuf, cols, w, row, tm)
            pooled_b = pooled.astype(BF16)
            pw_b = pw_ref[gi].astype(BF16)
            dyb = dy[:, dc + gi * gcw:dc + (gi + 1) * gcw]
            q = _nn(pooled_b, pw_b)
            dps_ref[:, cols] += jnp.sum(q * dyb, axis=0, keepdims=True)
            dq = (dyb * ps_ref[:, cols]).astype(BF16)
            dpw_ref[gi] += _tn(pooled_b, dq)
            dpooled = _nt(dq, pw_b)
            ebuf[0:tm, cols] = dpooled / cnt
            du = -dpooled
            for s in range(w):
                du = du + ebuf[s:s + tm, cols]
            dproj_ref[:, 3 * dc + gi * gcw:3 * dc + (gi + 1) * gcw] = du.astype(BF16)

        dh = _nn(dproj_ref[...], win[...])
        dg_ref[...] += jnp.sum(dh * xhat, axis=0, keepdims=True)
        dxh = dh * gv
        dx_ref[...] = dxo_v + r * (dxh - xhat * jnp.mean(dxh * xhat, axis=-1, keepdims=True))
        dcbuf[tm:tm + 8, :] = dcbuf[0:8, :]
        ebuf[tm:tm + HALO, :] = ebuf[0:HALO, :]

    tok = lambda i: (n_tiles - 1 - i, 0)
    halo = lambda i: (jnp.maximum((n_tiles - 1 - i) * hb - 1, 0), 0)
    one = lambda i: (0, 0)
    return _launch(
        body, name="mixer_backward", grid=(n_tiles,),
        in_specs=[pl.BlockSpec((tm, d), tok), pl.BlockSpec((tm, d), tok), pl.BlockSpec((1, d), one),
                  pl.BlockSpec((tm, 4 * dc), tok), pl.BlockSpec((HALO, 4 * dc), halo), HBM_SPEC, HBM_SPEC,
                  pl.BlockSpec(conv_w.shape, one), pl.BlockSpec(pool_w.shape, lambda i: (0, 0, 0)), pl.BlockSpec((1, dc), one)],
        out_specs=[pl.BlockSpec((tm, d), tok), pl.BlockSpec((tm, 4 * dc), tok), pl.BlockSpec((tm, d), tok), pl.BlockSpec((tm, d), tok),
                   pl.BlockSpec((1, d), one), pl.BlockSpec(conv_w.shape, one), pl.BlockSpec((1, dc), one),
                   pl.BlockSpec(pool_w.shape, lambda i: (0, 0, 0))],
        out_shape=[jax.ShapeDtypeStruct((t, d), F32), jax.ShapeDtypeStruct((t, 4 * dc), BF16), jax.ShapeDtypeStruct((t, d), BF16),
                   jax.ShapeDtypeStruct((t, d), BF16), jax.ShapeDtypeStruct((1, d), F32), jax.ShapeDtypeStruct(conv_w.shape, F32),
                   jax.ShapeDtypeStruct((1, dc), F32), jax.ShapeDtypeStruct(pool_w.shape, F32)],
        scratch_shapes=[pltpu.VMEM((4 * dc, d), BF16), pltpu.VMEM((2 * dc, d), BF16),
                        pltpu.VMEM((tm + HALO, dc), F32), pltpu.VMEM((tm + HALO, dc), F32),
                        pltpu.VMEM((tm + 8, dc), F32), pltpu.VMEM((tm + HALO, dc), F32), pltpu.SemaphoreType.DMA((1 + N_CHIPS,))],
        args=(dxo, x, g, proj, proj, win_t, wout_x, conv_w, pool_w, pool_scale), cargo=cargo)


def _loss_backward(x, g, target):
    t, d = x.shape
    tm = min(TM_EW, t)

    def body(x_ref, g_ref, t_ref, dx_ref, sq_ref, dg_ref):
        @pl.when(pl.program_id(0) == 0)
        def _():
            sq_ref[...] = jnp.zeros_like(sq_ref)
            dg_ref[...] = jnp.zeros_like(dg_ref)

        xv = x_ref[...]
        gv = g_ref[...]
        r = lax.rsqrt(jnp.mean(xv * xv, axis=-1, keepdims=True) + EPS)
        xhat = xv * r
        err = xhat * gv - t_ref[...]
        sq_ref[...] += jnp.sum(err * err, axis=0, keepdims=True)
        dy = err * (1.0 / d)
        dg_ref[...] += jnp.sum(dy * xhat, axis=0, keepdims=True)
        dxh = dy * gv
        dx_ref[...] = r * (dxh - xhat * jnp.mean(dxh * xhat, axis=-1, keepdims=True))

    tok = lambda i: (i, 0)
    one = lambda i: (0, 0)
    return pl.pallas_call(
        body, name="loss_backward",
        out_shape=[jax.ShapeDtypeStruct((t, d), F32), jax.ShapeDtypeStruct((1, d), F32), jax.ShapeDtypeStruct((1, d), F32)],
        grid=(t // tm,),
        in_specs=[pl.BlockSpec((tm, d), tok), pl.BlockSpec((1, d), one), pl.BlockSpec((tm, d), tok)],
        out_specs=[pl.BlockSpec((tm, d), tok), pl.BlockSpec((1, d), one), pl.BlockSpec((1, d), one)],
        compiler_params=pltpu.CompilerParams(dimension_semantics=("arbitrary",)),
    )(x, g, target)


def _adam_update(w, gv, m, v):
    m_new = ADAM_B1 * m + (1.0 - ADAM_B1) * gv
    v_new = ADAM_B2 * v + (1.0 - ADAM_B2) * (gv * gv)
    m_hat = m_new / (1.0 - ADAM_B1 ** ADAM_STEP)
    v_hat = v_new / (1.0 - ADAM_B2 ** ADAM_STEP)
    return -ADAM_LR * (m_hat / (jnp.sqrt(v_hat) + ADAM_EPS) + ADAM_WD * w), m_new, v_new


def _adamw(w, grad, m, v, name):
    rows, cols = w.shape
    br = _row_block(rows, 256) if rows >= 8 else rows

    def body(w_ref, g_ref, m_ref, v_ref, d_ref, mo_ref, vo_ref):
        d_ref[...], mo_ref[...], vo_ref[...] = _adam_update(w_ref[...], g_ref[...], m_ref[...], v_ref[...])

    blk = pl.BlockSpec((br, cols), lambda i: (i, 0))
    return pl.pallas_call(
        body, name=name,
        out_shape=[jax.ShapeDtypeStruct((rows, cols), F32)] * 3,
        grid=(rows // br,), in_specs=[blk] * 4, out_specs=[blk] * 3,
        compiler_params=pltpu.CompilerParams(dimension_semantics=("parallel",)),
    )(w, grad, m, v)


def _adamw_transposed(w, grad_t, m, v, name):
    _, rows, cols = w.shape
    br = 256 if rows % 256 == 0 else rows

    def body(w_ref, gt_ref, m_ref, v_ref, g_ref, d_ref, mo_ref, vo_ref):
        gv = gt_ref[...].T
        g_ref[...] = gv
        d_ref[...], mo_ref[...], vo_ref[...] = _adam_update(w_ref[...], gv, m_ref[...], v_ref[...])

    blk = pl.BlockSpec((None, br, cols), lambda i: (0, i, 0))
    return pl.pallas_call(
        body, name=name,
        out_shape=[jax.ShapeDtypeStruct((1, rows, cols), F32)] * 4,
        grid=(rows // br,), in_specs=[blk, pl.BlockSpec((cols, br), lambda i: (0, i)), blk, blk], out_specs=[blk] * 4,
        compiler_params=pltpu.CompilerParams(dimension_semantics=("parallel",)),
    )(w, grad_t, m, v)


def _f32_rows_as_bf16(a, rows, cols):
    bits = lax.bitcast_convert_type(a, BF16).reshape(a.shape[0], 2 * a.shape[1])
    return jnp.pad(bits, ((0, rows - bits.shape[0]), (0, cols - bits.shape[1])))


def kernel(x, norm_ffn1, ffn1_w_gate, ffn1_w_up, ffn1_w_down, norm_mix, w_in, conv_w, pool_w, pool_scale, w_out, norm_ffn2, ffn2_w_gate, ffn2_w_up, ffn2_w_down, norm_final, loss_target, m_norm_ffn1, m_ffn1_w_gate, m_ffn1_w_up, m_ffn1_w_down, m_norm_mix, m_w_in, m_conv_w, m_pool_w, m_pool_scale, m_w_out, m_norm_ffn2, m_ffn2_w_gate, m_ffn2_w_up, m_ffn2_w_down, m_norm_final, v_norm_ffn1, v_ffn1_w_gate, v_ffn1_w_up, v_ffn1_w_down, v_norm_mix, v_w_in, v_conv_w, v_pool_w, v_pool_scale, v_w_out, v_norm_ffn2, v_ffn2_w_gate, v_ffn2_w_up, v_ffn2_w_down, v_norm_final):
    weights = dict(norm_ffn1=norm_ffn1, ffn1_w_gate=ffn1_w_gate, ffn1_w_up=ffn1_w_up, ffn1_w_down=ffn1_w_down, norm_mix=norm_mix,
                   w_in=w_in, conv_w=conv_w, pool_w=pool_w, pool_scale=pool_scale, w_out=w_out, norm_ffn2=norm_ffn2,
                   ffn2_w_gate=ffn2_w_gate, ffn2_w_up=ffn2_w_up, ffn2_w_down=ffn2_w_down, norm_final=norm_final)
    first_m = dict(norm_ffn1=m_norm_ffn1, ffn1_w_gate=m_ffn1_w_gate, ffn1_w_up=m_ffn1_w_up, ffn1_w_down=m_ffn1_w_down,
                   norm_mix=m_norm_mix, w_in=m_w_in, conv_w=m_conv_w, pool_w=m_pool_w, pool_scale=m_pool_scale, w_out=m_w_out,
                   norm_ffn2=m_norm_ffn2, ffn2_w_gate=m_ffn2_w_gate, ffn2_w_up=m_ffn2_w_up, ffn2_w_down=m_ffn2_w_down,
                   norm_final=m_norm_final)
    second_m = dict(norm_ffn1=v_norm_ffn1, ffn1_w_gate=v_ffn1_w_gate, ffn1_w_up=v_ffn1_w_up, ffn1_w_down=v_ffn1_w_down,
                    norm_mix=v_norm_mix, w_in=v_w_in, conv_w=v_conv_w, pool_w=v_pool_w, pool_scale=v_pool_scale, w_out=v_w_out,
                    norm_ffn2=v_norm_ffn2, ffn2_w_gate=v_ffn2_w_gate, ffn2_w_up=v_ffn2_w_up, ffn2_w_down=v_ffn2_w_down,
                    norm_final=v_norm_final)
    names = list(weights)

    xs = x[0]
    tgt = loss_target[0]
    t, d = xs.shape
    dc = pool_scale.shape[1]
    cx, cy, cc = _my_place()
    chip = 2 * cx + cy
    place = jnp.stack([chip, cc]).astype(jnp.int32)

    conv_rows = 32
    wout_x = jnp.concatenate([w_out[0].astype(BF16), _f32_rows_as_bf16(conv_w[0], conv_rows, d)], axis=0)
    ffn1_shards = [ffn1_w_gate[0].T.astype(BF16), ffn1_w_up[0].T.astype(BF16), ffn1_w_down[0].astype(BF16)]
    wd2_shard = ffn2_w_down[0].astype(BF16)
    half_rows = wd2_shard.shape[0] // 2
    mid_shards = [w_in[0].T.astype(BF16), wout_x, ffn2_w_gate[0].T.astype(BF16), wd2_shard[:half_rows]]
    last_shards = [ffn2_w_up[0].T.astype(BF16), wd2_shard[half_rows:]]

    g1, gm, g2 = norm_ffn1, norm_mix, norm_ffn2
    gf = norm_final.reshape(1, d)
    pw = pool_w[0]

    wg1, wu1, wd1 = [[w] for w in _run_cargo(_gather_cargo(ffn1_shards), "gather_ffn1")]
    (x1, a1, b1), [(win_t, wout_g, wg2, wd2_a)] = _ffn_forward(xs, g1, wg1, wu1, wd1, "ffn1_forward", [_gather_cargo(mid_shards)])
    wo_rows = w_out.shape[1]
    cshard = conv_w.shape[2]
    conv_bits = wout_g.reshape(N_CHIPS, wo_rows + conv_rows, d)[:, wo_rows:wo_rows + conv_w.shape[1], :2 * cshard]
    conv_full = lax.bitcast_convert_type(conv_bits.reshape(N_CHIPS, conv_w.shape[1], cshard, 2), F32)
    conv_full = jnp.transpose(conv_full, (1, 0, 2)).reshape(conv_w.shape[1], N_CHIPS * cshard)
    (x2, proj, ymix), [(wu2, wd2_b)] = _mixer_forward(x1, gm, win_t, wout_g, conv_full, pw, pool_scale, [_gather_cargo(last_shards)])
    wg2, wu2, wd2 = [wg2], [wu2], [wd2_a, wd2_b]
    (x3, a2, b2), _ = _ffn_forward(x2, g2, wg2, wu2, wd2, "ffn2_forward")

    dx3, sq_cols, dgf = _loss_backward(x3, gf, tgt)

    (dx2, da2, db2, s2, h3, do2, dg2), _ = _ffn_backward(dx3, x2, g2, a2, b2, wg2, wu2, wd2, "ffn2_backward")
    p_wg2, _ = _weight_grad(da2, h3, "ffn2_gate_grad")
    p_wu2, [(x_wg2,)] = _weight_grad(db2, h3, "ffn2_up_grad", [_exchange_cargo([p_wg2])])
    p_wd2, [(x_wu2,)] = _weight_grad(s2, do2, "ffn2_down_grad", [_exchange_cargo([p_wu2])])

    (dx1, dproj, h2, dx2b, dgm, dcw, dps, dpw), [(x_wd2,)] = _mixer_backward(
        dx2, x1, gm, proj, win_t, wout_g, conv_full, pw, pool_scale, [_exchange_cargo([p_wd2])])
    p_win, _ = _weight_grad(dproj, h2, "w_in_grad")
    p_wout, [(x_win,)] = _weight_grad(ymix, dx2b, "w_out_grad", [_exchange_cargo([p_win])])

    (dx0, da1, db1, s1, h1, do1, dg1), _ = _ffn_backward(dx1, xs, g1, a1, b1, wg1, wu1, wd1, "ffn1_backward")

    npw = pw.size // d
    head = [dg1, dgm, dg2, dgf, jnp.pad(dps, ((0, 0), (0, d - dc))), jnp.pad(dcw, ((0, 0), (0, d - dc))), sq_cols]
    n_head = sum(h.shape[0] for h in head)
    base = -(-n_head // 8) * 8
    pack = jnp.concatenate(head + [jnp.zeros((base - n_head, d), F32), dpw.reshape(npw, d)], axis=0)

    p_wg1, [(x_wout,), (packs,)] = _weight_grad(da1, h1, "ffn1_gate_grad", [_exchange_cargo([p_wout]), _all_gather_small_cargo(pack)])
    p_wu1, [(x_wg1,)] = _weight_grad(db1, h1, "ffn1_up_grad", [_exchange_cargo([p_wg1])])
    p_wd1, [(x_wu1,)] = _weight_grad(s1, do1, "ffn1_down_grad", [_exchange_cargo([p_wu1])])
    x_wd1, = _run_cargo(_exchange_cargo([p_wd1]), "grad_exchange_last")
    small = _sum_by_device(packs)
    loss = jnp.sum(small[n_head - 1]) * (0.5 / d)

    order = ["wg1", "wu1", "wd1", "win", "wout", "wg2", "wu2", "wd2"]
    pairs = dict(wg1=p_wg1, wu1=p_wu1, wd1=p_wd1, win=p_win, wout=p_wout, wg2=p_wg2, wu2=p_wu2, wd2=p_wd2)
    landed = dict(wg1=x_wg1, wu1=x_wu1, wd1=x_wd1, win=x_win, wout=x_wout, wg2=x_wg2, wu2=x_wu2, wd2=x_wd2)
    both = _sibling_share([_chip_sum(pairs[k], landed[k], place, k) for k in order])
    rwg1, rwu1, rwd1, rwin, rwout, rwg2, rwu2, rwd2 = [b.reshape(2 * b.shape[1], b.shape[2]) for b in both]

    grads = {
        "norm_ffn1": small[0:1], "norm_mix": small[1:2], "norm_ffn2": small[2:3], "norm_final": small[3],
        "pool_scale": small[4:5, :dc],
        "conv_w": lax.dynamic_slice_in_dim(small[5:5 + dcw.shape[0], :dc], chip * cshard, cshard, axis=1)[None],
        "pool_w": small[base:].reshape(pool_w.shape),
        "ffn1_w_down": rwd1[None], "w_out": rwout[None], "ffn2_w_down": rwd2[None],
    }
    by_view = {"ffn1_w_gate": rwg1, "ffn1_w_up": rwu1, "ffn2_w_gate": rwg2, "ffn2_w_up": rwu2}

    deltas, new_m, new_v = {}, {}, {}
    for n in names:
        w = weights[n]
        shape = w.shape
        if n == "w_in":
            grads[n], deltas[n], new_m[n], new_v[n] = _adamw_transposed(w, rwin, first_m[n], second_m[n], "adamw_" + n)
            continue
        if n in by_view:
            view = lambda a: jnp.swapaxes(a, 1, 2)[0]
            back = lambda a: jnp.swapaxes(a[None], 1, 2)
            dl, mo, vo = _adamw(view(w), by_view[n], view(first_m[n]), view(second_m[n]), "adamw_" + n)
            grads[n], deltas[n], new_m[n], new_v[n] = back(by_view[n]), back(dl), back(mo), back(vo)
            continue
        as2d = (lambda a: a.reshape(-1, shape[-1]))
        dl, mo, vo = _adamw(as2d(w), as2d(grads[n]), as2d(first_m[n]), as2d(second_m[n]), "adamw_" + n)
        deltas[n], new_m[n], new_v[n] = dl.reshape(shape), mo.reshape(shape), vo.reshape(shape)
        grads[n] = grads[n].reshape(shape)

    return (loss, dx0[None], *[grads[n] for n in names], *[deltas[n] for n in names],
            *[new_m[n] for n in names], *[new_v[n] for n in names])
```

```python
import functools

import jax
import jax.numpy as jnp
from jax import lax
from jax.experimental import pallas as pl
from jax.experimental.pallas import tpu as pltpu

F32 = jnp.float32
BF16 = jnp.bfloat16
MESH = pl.DeviceIdType.MESH

EPS = 1e-6
POOL_WINDOWS = (2, 4, 8, 16)
ADAM_LR = 0.001
ADAM_B1 = 0.9
ADAM_B2 = 0.999
ADAM_EPS = 1e-08
ADAM_WD = 0.01
ADAM_STEP = 10

N_CHIPS = 4
N_DEVICES = 8
MXU_COLS_V7X = 256
VMEM_LIMIT = 56 * 1024 * 1024
TM_FFN = 512
TM_MIX = 512
TM_TN = 1024
HALO = 16
FFN_FWD_CHUNKS = 2
FFN_BWD_CHUNKS = 2


def _nt(a, b):
    return lax.dot_general(a, b, (((1,), (1,)), ((), ())), preferred_element_type=F32)


def _tn(a, b):
    return lax.dot_general(a, b, (((0,), (0,)), ((), ())), preferred_element_type=F32)


def _nn(a, b):
    return jnp.dot(a, b, preferred_element_type=F32)


def _sigmoid(a):
    return 1.0 / (1.0 + jnp.exp(-a))


def _feature_chunks(n, parts):
    assert n % MXU_COLS_V7X == 0
    tiles = n // MXU_COLS_V7X
    out, s0 = [], 0
    for p in range(parts):
        sz = (tiles // parts + (1 if p < tiles % parts else 0)) * MXU_COLS_V7X
        if sz:
            out.append((s0, sz))
            s0 += sz
    return out


def _row_block(rows, cap):
    best = 8
    for b in range(8, min(rows, cap) + 1, 8):
        if rows % b == 0:
            best = b
    assert rows % best == 0
    return best


def _my_place():
    return lax.axis_index("x"), lax.axis_index("y"), lax.axis_index("c")


def _other_chips(x, y):
    return [(1 - x, y), (x, 1 - y), (1 - x, 1 - y)]


HBM_SPEC = pl.BlockSpec(memory_space=pltpu.HBM)


class _Cargo:
    def __init__(self, operands, out_shapes, n_sems, start, finish):
        self.operands, self.out_shapes, self.n_sems, self.start, self.finish = list(operands), list(out_shapes), n_sems, start, finish


def _launch(body, *, name, grid, in_specs, out_specs, out_shape, scratch_shapes, args, cargo=()):
    params = pltpu.CompilerParams(dimension_semantics=("arbitrary",) * len(grid), vmem_limit_bytes=VMEM_LIMIT)
    cargos = list(cargo)
    c_operands = [op for cg in cargos for op in cg.operands]
    c_shapes = [sh for cg in cargos for sh in cg.out_shapes]
    counts = [len(in_specs), len(c_operands), len(out_shape), len(c_shapes), len(scratch_shapes), 2 * len(cargos)]

    def carrying(*refs):
        groups, pos = [], 0
        for k in counts:
            groups.append(refs[pos:pos + k])
            pos += k
        ins, c_ins, outs, c_outs, scratch, sems = groups
        parts, pi, po = [], 0, 0
        for n, cg in enumerate(cargos):
            parts.append((c_ins[pi:pi + len(cg.operands)], c_outs[po:po + len(cg.out_shapes)], sems[2 * n], sems[2 * n + 1]))
            pi += len(cg.operands)
            po += len(cg.out_shapes)
        ids = [pl.program_id(ax) for ax in range(len(grid))]
        first = functools.reduce(jnp.logical_and, [i == 0 for i in ids])
        last = functools.reduce(jnp.logical_and, [i == g - 1 for i, g in zip(ids, grid)])

        if cargos:
            @pl.when(first)
            def _():
                for cg, part in zip(cargos, parts):
                    cg.start(*part)

        body(*ins, *outs, *scratch)

        if cargos:
            @pl.when(last)
            def _():
                for cg, part in zip(cargos, parts):
                    cg.finish(*part)

    sems = [pltpu.SemaphoreType.DMA((cg.n_sems,)) for cg in cargos for _ in range(2)]
    outs = pl.pallas_call(
        carrying, name=name, grid=grid,
        in_specs=list(in_specs) + [HBM_SPEC] * counts[1], out_specs=list(out_specs) + [HBM_SPEC] * counts[3],
        out_shape=list(out_shape) + c_shapes, scratch_shapes=list(scratch_shapes) + sems,
        compiler_params=params)(*args, *c_operands)
    own, rest = list(outs[:counts[2]]), list(outs[counts[2]:])
    carried, po = [], 0
    for cg in cargos:
        carried.append(rest[po:po + len(cg.out_shapes)])
        po += len(cg.out_shapes)
    return own, carried


def _run_cargo(cargo, name):
    n_in, n_out = len(cargo.operands), len(cargo.out_shapes)

    def body(*refs):
        c_ins, c_outs, sems = refs[:n_in], refs[n_in:n_in + n_out], refs[n_in + n_out:]
        cargo.start(c_ins, c_outs, *sems)
        cargo.finish(c_ins, c_outs, *sems)

    sem = pltpu.SemaphoreType.DMA((cargo.n_sems,))
    return list(pl.pallas_call(body, name=name, out_shape=cargo.out_shapes, in_specs=[HBM_SPEC] * n_in,
                               out_specs=[HBM_SPEC] * n_out, scratch_shapes=[sem, sem])(*cargo.operands))


def _gather_cargo(shards):
    n = len(shards)
    for s in shards:
        assert s.shape[0] % 32 == 0

    def steps(ins, outs, send_sems, recv_sems):
        x, y, c = _my_place()
        sibling = (x, y, 1 - c)
        chips = _other_chips(x, y)
        mine = 2 * x + y

        def rows_of(a, chip_index, half):
            rps = shards[a].shape[0]
            hr = rps // 2
            return outs[a].at[pl.ds(pl.multiple_of(chip_index * rps + half * hr, 16), hr), :]

        def remote(a, slot, src, dst, to):
            return pltpu.make_async_remote_copy(
                src_ref=src, dst_ref=dst, send_sem=send_sems.at[a * 7 + slot], recv_sem=recv_sems.at[a * 7 + slot],
                device_id=to, device_id_type=MESH)

        def own_copy(a):
            rps = shards[a].shape[0]
            return remote(a, 6, ins[a], outs[a].at[pl.ds(pl.multiple_of(mine * rps, 16), rps), :], sibling)

        def my_half(a):
            hr = shards[a].shape[0] // 2
            return ins[a].at[pl.ds(pl.multiple_of(c * hr, 16), hr), :]

        def start():
            for a in range(n):
                own_copy(a).start()
                for j, chip in enumerate(chips):
                    remote(a, j, my_half(a), rows_of(a, mine, c), (*chip, c)).start()

        def finish():
            for a in range(n):
                for j, chip in enumerate(chips):
                    landed = rows_of(a, 2 * chip[0] + chip[1], c)
                    remote(a, j, landed, landed, (*chip, c)).wait_recv()
                    remote(a, 3 + j, landed, landed, sibling).start()
            for a in range(n):
                for j, chip in enumerate(chips):
                    from_sibling = rows_of(a, 2 * chip[0] + chip[1], 1 - c)
                    remote(a, 3 + j, from_sibling, from_sibling, sibling).wait_recv()
            for a in range(n):
                for j, chip in enumerate(chips):
                    remote(a, j, my_half(a), rows_of(a, mine, c), (*chip, c)).wait_send()
                    landed = rows_of(a, 2 * chip[0] + chip[1], c)
                    remote(a, 3 + j, landed, landed, sibling).wait_send()
                own_copy(a).wait()

        return start, finish

    return _Cargo(shards, [jax.ShapeDtypeStruct((N_CHIPS * s.shape[0], s.shape[1]), s.dtype) for s in shards], 7 * n,
                  lambda *r: steps(*r)[0](), lambda *r: steps(*r)[1]())


def _exchange_cargo(pairs):
    n = len(pairs)

    def copies(ins, outs, send_sems, recv_sems):
        x, y, c = _my_place()
        return [pltpu.make_async_remote_copy(
            src_ref=ins[a].at[2 * chip[0] + chip[1]], dst_ref=outs[a].at[j],
            send_sem=send_sems.at[3 * a + j], recv_sem=recv_sems.at[3 * a + j], device_id=(*chip, c), device_id_type=MESH)
            for a in range(n) for j, chip in enumerate(_other_chips(x, y))]

    def start(*r):
        for cp in copies(*r):
            cp.start()

    def finish(*r):
        for cp in copies(*r):
            cp.wait()

    return _Cargo(pairs, [jax.ShapeDtypeStruct((3,) + p.shape[1:], p.dtype) for p in pairs], 3 * n, start, finish)


def _all_gather_small_cargo(pack):
    rows, cols = pack.shape

    def copies(ins, outs, send_sems, recv_sems):
        x, y, c = _my_place()
        me = 4 * x + 2 * y + c
        remote = []
        for f in range(1, N_DEVICES):
            fx, fy, fc = (f >> 2) & 1, (f >> 1) & 1, f & 1
            to = (1 - x if fx else x, 1 - y if fy else y, 1 - c if fc else c)
            remote.append(pltpu.make_async_remote_copy(
                src_ref=ins[0], dst_ref=outs[0].at[me], send_sem=send_sems.at[f - 1], recv_sem=recv_sems.at[f - 1],
                device_id=to, device_id_type=MESH))
        own = pltpu.make_async_copy(ins[0], outs[0].at[me], send_sems.at[N_DEVICES - 1])
        return remote, own

    def start(*r):
        remote, own = copies(*r)
        own.start()
        for cp in remote:
            cp.start()

    def finish(*r):
        remote, own = copies(*r)
        for cp in remote:
            cp.wait()
        own.wait()

    return _Cargo([pack], [jax.ShapeDtypeStruct((N_DEVICES, rows, cols), F32)], N_DEVICES, start, finish)


def _sum_by_device(packs):
    n, rows, cols = packs.shape

    def body(p_ref, o_ref):
        acc = p_ref[0]
        for dev in range(1, n):
            acc = acc + p_ref[dev]
        o_ref[...] = acc

    return pl.pallas_call(body, name="small_grads_sum", out_shape=jax.ShapeDtypeStruct((rows, cols), F32))(packs)


def _chip_sum(pair, got, place, tag):
    _, hr, cols = pair.shape
    br = _row_block(hr, 256)

    def body(k_ref, p_ref, r_ref, o_ref):
        acc = p_ref[...].astype(F32)
        for j in range(3):
            acc = acc + r_ref[j].astype(F32)
        o_ref[...] = acc

    return pl.pallas_call(
        body, name="grad_chip_sum_" + tag,
        out_shape=jax.ShapeDtypeStruct((2, hr, cols), F32),
        grid_spec=pltpu.PrefetchScalarGridSpec(
            num_scalar_prefetch=1, grid=(hr // br,),
            in_specs=[pl.BlockSpec((None, br, cols), lambda r, k_ref: (k_ref[0], r, 0)),
                      pl.BlockSpec((3, br, cols), lambda r, k_ref: (0, r, 0))],
            out_specs=pl.BlockSpec((None, br, cols), lambda r, k_ref: (k_ref[1], r, 0))),
        compiler_params=pltpu.CompilerParams(dimension_semantics=("parallel",)),
    )(place, pair, got)


def _sibling_share(halves):
    n = len(halves)

    def body(*refs):
        outs = refs[n:2 * n]
        send_sems, recv_sems = refs[2 * n:]
        x, y, c = _my_place()
        copies = []
        for a in range(n):
            cp = pltpu.make_async_remote_copy(
                src_ref=outs[a].at[c], dst_ref=outs[a].at[c], send_sem=send_sems.at[a], recv_sem=recv_sems.at[a],
                device_id=(x, y, 1 - c), device_id_type=MESH)
            cp.start()
            copies.append(cp)
        for cp in copies:
            cp.wait()

    return pl.pallas_call(
        body, name="grad_share_sibling",
        out_shape=[jax.ShapeDtypeStruct(h.shape, h.dtype) for h in halves],
        in_specs=[HBM_SPEC] * n, out_specs=[HBM_SPEC] * n,
        input_output_aliases={a: a for a in range(n)},
        scratch_shapes=[pltpu.SemaphoreType.DMA((n,)), pltpu.SemaphoreType.DMA((n,))],
    )(*halves)


def _load_rows(pairs, sems):
    cps = [pltpu.make_async_copy(src, dst, sems.at[j]) for j, (src, dst) in enumerate(pairs)]
    for cp in cps:
        cp.start()
    for cp in cps:
        cp.wait()


def _piece_rows(weights):
    flat = [p for pieces in weights for p in pieces]

    def copies(refs, mats):
        out, n = [], 0
        for pieces, mat in zip(weights, mats):
            rps = sum(p.shape[0] for p in pieces) // N_CHIPS
            off = 0
            for p in pieces:
                r = p.shape[0] // N_CHIPS
                if len(pieces) == 1:
                    out.append((refs[n], mat))
                else:
                    for k in range(N_CHIPS):
                        out.append((refs[n].at[pl.ds(k * r, r), :], mat.at[pl.ds(k * rps + off, r), :]))
                off += r
                n += 1
        return out

    n_copies = sum(1 if len(pieces) == 1 else N_CHIPS * len(pieces) for pieces in weights)
    return flat, copies, n_copies


def _loss_head(xv, gv, tv):
    d = xv.shape[-1]
    r = lax.rsqrt(jnp.mean(xv * xv, axis=-1, keepdims=True) + EPS)
    xhat = xv * r
    err = xhat * gv - tv
    dy = err * (1.0 / d)
    dxh = dy * gv
    dx = r * (dxh - xhat * jnp.mean(dxh * xhat, axis=-1, keepdims=True))
    return dx, jnp.sum(err * err, axis=0, keepdims=True), jnp.sum(dy * xhat, axis=0, keepdims=True)


def _ffn_forward(x, g, wg_t, wu_t, wd, name, cargo=(), loss_head=None):
    t, d = x.shape
    f = sum(p.shape[0] for p in wd)
    tm = min(TM_FFN, t)
    chunks = _feature_chunks(f, FFN_FWD_CHUNKS)
    flat, copies, n_copies = _piece_rows([wg_t, wu_t, wd])
    nw = len(flat)
    nl = 2 if loss_head else 0

    def body(x_ref, g_ref, *rest):
        head, w_hbm = rest[:nl], rest[nl:nl + nw]
        xo_ref, a_ref, b_ref = rest[nl + nw:nl + nw + 3]
        sums, (wg, wu, wdn, sems) = rest[nl + nw + 3:nl + nw + 3 + nl], rest[nl + nw + 3 + nl:]

        @pl.when(pl.program_id(0) == 0)
        def _():
            _load_rows(copies(w_hbm, [wg, wu, wdn]), sems)
            for s_ref in sums:
                s_ref[...] = jnp.zeros_like(s_ref)

        xv = x_ref[...]
        r = lax.rsqrt(jnp.mean(xv * xv, axis=-1, keepdims=True) + EPS)
        h = (xv * r * g_ref[...]).astype(BF16)
        acc = jnp.zeros((tm, d), F32)
        for s0, sz in chunks:
            a = _nt(h, wg[s0:s0 + sz, :])
            b = _nt(h, wu[s0:s0 + sz, :])
            a_ref[:, s0:s0 + sz] = a.astype(BF16)
            b_ref[:, s0:s0 + sz] = b.astype(BF16)
            s = (a * _sigmoid(a) * b).astype(BF16)
            acc = acc + _nn(s, wdn[s0:s0 + sz, :])
        xo = xv + 0.5 * acc
        if loss_head:
            dx, sq, dgf = _loss_head(xo, head[0][...], head[1][...])
            xo_ref[...] = dx
            sums[0][...] += sq
            sums[1][...] += dgf
        else:
            xo_ref[...] = xo

    tok = lambda i: (i, 0)
    one = lambda i: (0, 0)
    row = [pl.BlockSpec((1, d), one)] * nl
    return _launch(
        body, name=name, grid=(t // tm,),
        in_specs=[pl.BlockSpec((tm, d), tok), pl.BlockSpec((1, d), one)]
        + ([pl.BlockSpec((1, d), one), pl.BlockSpec((tm, d), tok)] if loss_head else []) + [HBM_SPEC] * nw,
        out_specs=[pl.BlockSpec((tm, d), tok), pl.BlockSpec((tm, f), tok), pl.BlockSpec((tm, f), tok)] + row,
        out_shape=[jax.ShapeDtypeStruct((t, d), F32), jax.ShapeDtypeStruct((t, f), BF16), jax.ShapeDtypeStruct((t, f), BF16)]
        + [jax.ShapeDtypeStruct((1, d), F32)] * nl,
        scratch_shapes=[pltpu.VMEM((f, d), BF16), pltpu.VMEM((f, d), BF16), pltpu.VMEM((f, d), BF16), pltpu.SemaphoreType.DMA((n_copies,))],
        args=(x, g, *(loss_head or ()), *flat), cargo=cargo)


def _ffn_backward(dxo, x, g, a, b, wg_t, wu_t, wd, name, cargo=()):
    t, d = x.shape
    f = sum(p.shape[0] for p in wd)
    tm = min(TM_FFN // 2, t)
    chunks = _feature_chunks(f, FFN_BWD_CHUNKS)
    flat, copies, n_copies = _piece_rows([wg_t, wu_t, wd])
    nw = len(flat)

    def body(dxo_ref, x_ref, g_ref, a_ref, b_ref, *rest):
        w_hbm, (dx_ref, da_ref, db_ref, s_ref, h_ref, do_ref, dg_ref, wg, wu, wdn, sems) = rest[:nw], rest[nw:]

        @pl.when(pl.program_id(0) == 0)
        def _():
            _load_rows(copies(w_hbm, [wg, wu, wdn]), sems)
            dg_ref[...] = jnp.zeros_like(dg_ref)

        xv = x_ref[...]
        gv = g_ref[...]
        r = lax.rsqrt(jnp.mean(xv * xv, axis=-1, keepdims=True) + EPS)
        xhat = xv * r
        h_ref[...] = (xhat * gv).astype(BF16)
        dxo_v = dxo_ref[...]
        dout = (0.5 * dxo_v).astype(BF16)
        do_ref[...] = dout
        dh = jnp.zeros((tm, d), F32)
        for s0, sz in chunks:
            ds = _nt(dout, wdn[s0:s0 + sz, :])
            av = a_ref[:, s0:s0 + sz].astype(F32)
            bv = b_ref[:, s0:s0 + sz].astype(F32)
            sig = _sigmoid(av)
            silu = av * sig
            s_ref[:, s0:s0 + sz] = (silu * bv).astype(BF16)
            da = (ds * bv * (sig * (1.0 + av * (1.0 - sig)))).astype(BF16)
            db = (ds * silu).astype(BF16)
            da_ref[:, s0:s0 + sz] = da
            db_ref[:, s0:s0 + sz] = db
            dh = dh + _nn(da, wg[s0:s0 + sz, :]) + _nn(db, wu[s0:s0 + sz, :])
        dg_ref[...] += jnp.sum(dh * xhat, axis=0, keepdims=True)
        dxh = dh * gv
        dx_ref[...] = dxo_v + r * (dxh - xhat * jnp.mean(dxh * xhat, axis=-1, keepdims=True))

    tok = lambda i: (i, 0)
    one = lambda i: (0, 0)
    return _launch(
        body, name=name, grid=(t // tm,),
        in_specs=[pl.BlockSpec((tm, d), tok), pl.BlockSpec((tm, d), tok), pl.BlockSpec((1, d), one),
                  pl.BlockSpec((tm, f), tok), pl.BlockSpec((tm, f), tok)] + [HBM_SPEC] * nw,
        out_specs=[pl.BlockSpec((tm, d), tok), pl.BlockSpec((tm, f), tok), pl.BlockSpec((tm, f), tok), pl.BlockSpec((tm, f), tok),
                   pl.BlockSpec((tm, d), tok), pl.BlockSpec((tm, d), tok), pl.BlockSpec((1, d), one)],
        out_shape=[jax.ShapeDtypeStruct((t, d), F32), jax.ShapeDtypeStruct((t, f), BF16), jax.ShapeDtypeStruct((t, f), BF16),
                   jax.ShapeDtypeStruct((t, f), BF16), jax.ShapeDtypeStruct((t, d), BF16), jax.ShapeDtypeStruct((t, d), BF16),
                   jax.ShapeDtypeStruct((1, d), F32)],
        scratch_shapes=[pltpu.VMEM((f, d), BF16), pltpu.VMEM((f, d), BF16), pltpu.VMEM((f, d), BF16), pltpu.SemaphoreType.DMA((n_copies,))],
        args=(dxo, x, g, a, b, *flat), cargo=cargo)


def _weight_grad(lhs, rhs, name, cargo=()):
    t, m = lhs.shape
    d = rhs.shape[1]
    tm = min(TM_TN, t)
    nt = t // tm
    nj = 2 if (m // 2) % 128 == 0 and m > 1024 else 1
    bm = m // nj
    cpb = N_CHIPS // nj
    rps = m // N_CHIPS
    hr = rps // 2
    assert hr % 16 == 0

    def body(l_ref, r_ref, o_ref, acc, stage, recv, send_sems, recv_sems):
        j = pl.program_id(0)
        i = pl.program_id(1)
        @pl.when(i == 0)
        def _():
            acc[...] = jnp.zeros_like(acc)

        acc[...] += _tn(l_ref[...], r_ref[...])

        def pair_sum(jj):
            x, y, c = _my_place()
            copies = []
            for q in range(cpb):
                slot = jj * cpb + q
                stage[slot] = acc[pl.ds(pl.multiple_of(q * rps + (1 - c) * hr, 16), hr), :].astype(BF16)
                cp = pltpu.make_async_remote_copy(
                    src_ref=stage.at[slot], dst_ref=recv.at[slot], send_sem=send_sems.at[slot], recv_sem=recv_sems.at[slot],
                    device_id=(x, y, 1 - c), device_id_type=MESH)
                cp.start()
                copies.append(cp)
            for q, cp in enumerate(copies):
                cp.wait_recv()
                mine = acc[pl.ds(pl.multiple_of(q * rps + c * hr, 16), hr), :]
                o_ref[q] = (mine + recv[jj * cpb + q].astype(F32)).astype(BF16)
            for cp in copies:
                cp.wait_send()

        for jj in range(nj):
            @pl.when(jnp.logical_and(i == nt - 1, j == jj))
            def _():
                pair_sum(jj)

    outs, carried = _launch(
        body, name=name, grid=(nj, nt),
        in_specs=[pl.BlockSpec((tm, bm), lambda j, i: (i, j)), pl.BlockSpec((tm, d), lambda j, i: (i, 0))],
        out_specs=[pl.BlockSpec((cpb, hr, d), lambda j, i: (j, 0, 0))],
        out_shape=[jax.ShapeDtypeStruct((N_CHIPS, hr, d), BF16)],
        scratch_shapes=[pltpu.VMEM((bm, d), F32), pltpu.VMEM((N_CHIPS, hr, d), BF16), pltpu.VMEM((N_CHIPS, hr, d), BF16),
                        pltpu.SemaphoreType.DMA((N_CHIPS,)), pltpu.SemaphoreType.DMA((N_CHIPS,))],
        args=(lhs, rhs), cargo=cargo)
    return outs[0], carried


def _pool_parts(u_cols, ubuf, cols, w, row, tm):
    ws = u_cols
    for s in range(1, w):
        ws = ws + ubuf[HALO - s:HALO - s + tm, cols]
    cnt = jnp.minimum(row + 1, w).astype(F32)
    return ws / cnt - u_cols, cnt


def _mixer_forward(x, g, win_t, wout_x, conv_w, pool_w, pool_scale, cargo=()):
    t, d = x.shape
    dc = win_t.shape[0] // 4
    gcw = dc // len(POOL_WINDOWS)
    wo_rows = d // N_CHIPS
    wo_stride = wout_x.shape[0] // N_CHIPS
    tm = min(TM_MIX, t)

    def body(x_ref, g_ref, win_hbm, wout_hbm, cw_ref, pw_ref, ps_ref, xo_ref, proj_ref, y_ref,
             win, wout, zbuf, ubuf, sems):
        i = pl.program_id(0)

        @pl.when(i == 0)
        def _():
            pairs = [(win_hbm, win)]
            for k in range(N_CHIPS):
                pairs.append((wout_hbm.at[pl.ds(k * wo_stride, wo_rows), :], wout.at[pl.ds(k * wo_rows, wo_rows), :]))
            _load_rows(pairs, sems)
            zbuf[0:8, :] = jnp.zeros((8, dc), F32)
            ubuf[0:HALO, :] = jnp.zeros((HALO, dc), F32)

        xv = x_ref[...]
        r = lax.rsqrt(jnp.mean(xv * xv, axis=-1, keepdims=True) + EPS)
        h = (xv * r * g_ref[...]).astype(BF16)
        v = _nt(h, win[0:dc, :])
        gb = _nt(h, win[dc:2 * dc, :])
        gc = _nt(h, win[2 * dc:3 * dc, :])
        u = _nt(h, win[3 * dc:4 * dc, :])
        proj_ref[:, 0:dc] = v.astype(BF16)
        proj_ref[:, dc:2 * dc] = gb.astype(BF16)
        proj_ref[:, 2 * dc:3 * dc] = gc.astype(BF16)
        proj_ref[:, 3 * dc:4 * dc] = u.astype(BF16)

        z = gc * v
        zbuf[8:8 + tm, :] = z
        cw = cw_ref[...]
        conv = cw[2:3, :] * z + cw[1:2, :] * zbuf[7:7 + tm, :] + cw[0:1, :] * zbuf[6:6 + tm, :]
        y_ref[:, 0:dc] = (gb * conv).astype(BF16)

        ubuf[HALO:HALO + tm, :] = u
        row = i * tm + lax.broadcasted_iota(jnp.int32, (tm, 1), 0)
        for gi, w in enumerate(POOL_WINDOWS):
            cols = slice(gi * gcw, (gi + 1) * gcw)
            pooled, _ = _pool_parts(u[:, cols], ubuf, cols, w, row, tm)
            yb = _nn(pooled.astype(BF16), pw_ref[gi].astype(BF16)) * ps_ref[:, cols]
            y_ref[:, dc + gi * gcw:dc + (gi + 1) * gcw] = yb.astype(BF16)

        xo_ref[...] = xv + _nn(y_ref[...], wout[...])
        zbuf[0:8, :] = zbuf[tm:tm + 8, :]
        ubuf[0:HALO, :] = ubuf[tm:tm + HALO, :]

    tok = lambda i: (i, 0)
    one = lambda i: (0, 0)
    return _launch(
        body, name="mixer_forward", grid=(t // tm,),
        in_specs=[pl.BlockSpec((tm, d), tok), pl.BlockSpec((1, d), one), HBM_SPEC, HBM_SPEC,
                  pl.BlockSpec(conv_w.shape, one), pl.BlockSpec(pool_w.shape, lambda i: (0, 0, 0)), pl.BlockSpec((1, dc), one)],
        out_specs=[pl.BlockSpec((tm, d), tok), pl.BlockSpec((tm, 4 * dc), tok), pl.BlockSpec((tm, 2 * dc), tok)],
        out_shape=[jax.ShapeDtypeStruct((t, d), F32), jax.ShapeDtypeStruct((t, 4 * dc), BF16), jax.ShapeDtypeStruct((t, 2 * dc), BF16)],
        scratch_shapes=[pltpu.VMEM((4 * dc, d), BF16), pltpu.VMEM((2 * dc, d), BF16),
                        pltpu.VMEM((tm + 8, dc), F32), pltpu.VMEM((tm + HALO, dc), F32), pltpu.SemaphoreType.DMA((1 + N_CHIPS,))],
        args=(x, g, win_t, wout_x, conv_w, pool_w, pool_scale), cargo=cargo)


def _mixer_backward(dxo, x, g, proj, win_t, wout_x, conv_w, pool_w, pool_scale, cargo=()):
    t, d = x.shape
    dc = win_t.shape[0] // 4
    ng = len(POOL_WINDOWS)
    gcw = dc // ng
    wo_rows = d // N_CHIPS
    wo_stride = wout_x.shape[0] // N_CHIPS
    tm = min(TM_MIX // 2, t)
    n_tiles = t // tm
    hb = tm // HALO

    def body(dxo_ref, x_ref, g_ref, proj_ref, halo_ref, win_hbm, wout_hbm, cw_ref, pw_ref, ps_ref,
             dx_ref, dproj_ref, h_ref, dxob_ref, dg_ref, dcw_ref, dps_ref, dpw_ref,
             win, wout, zbuf, ubuf, dcbuf, ebuf, sems):
        i = pl.program_id(0)
        tile = n_tiles - 1 - i

        @pl.when(i == 0)
        def _():
            pairs = [(win_hbm, win)]
            for k in range(N_CHIPS):
                pairs.append((wout_hbm.at[pl.ds(k * wo_stride, wo_rows), :], wout.at[pl.ds(k * wo_rows, wo_rows), :]))
            _load_rows(pairs, sems)
            dcbuf[tm:tm + 8, :] = jnp.zeros((8, dc), F32)
            ebuf[tm:tm + HALO, :] = jnp.zeros((HALO, dc), F32)
            dg_ref[...] = jnp.zeros_like(dg_ref)
            dcw_ref[...] = jnp.zeros_like(dcw_ref)
            dps_ref[...] = jnp.zeros_like(dps_ref)
            dpw_ref[...] = jnp.zeros_like(dpw_ref)

        xv = x_ref[...]
        gv = g_ref[...]
        r = lax.rsqrt(jnp.mean(xv * xv, axis=-1, keepdims=True) + EPS)
        xhat = xv * r
        h_ref[...] = (xhat * gv).astype(BF16)
        dxo_v = dxo_ref[...]
        dxo_b = dxo_v.astype(BF16)
        dxob_ref[...] = dxo_b

        v = proj_ref[:, 0:dc].astype(F32)
        gb = proj_ref[:, dc:2 * dc].astype(F32)
        gc = proj_ref[:, 2 * dc:3 * dc].astype(F32)
        u = proj_ref[:, 3 * dc:4 * dc].astype(F32)
        first = jnp.where(tile > 0, 1.0, 0.0)
        zbuf[0:HALO, :] = halo_ref[:, 2 * dc:3 * dc].astype(F32) * halo_ref[:, 0:dc].astype(F32) * first
        ubuf[0:HALO, :] = halo_ref[:, 3 * dc:4 * dc].astype(F32) * first
        z = gc * v
        zbuf[HALO:HALO + tm, :] = z
        ubuf[HALO:HALO + tm, :] = u
        z1 = zbuf[HALO - 1:HALO - 1 + tm, :]
        z2 = zbuf[HALO - 2:HALO - 2 + tm, :]
        cw = cw_ref[...]
        conv = cw[2:3, :] * z + cw[1:2, :] * z1 + cw[0:1, :] * z2

        dy = _nt(dxo_b, wout[...])
        dya = dy[:, 0:dc]
        dgb = dya * conv
        dconv = dya * gb
        dcbuf[0:tm, :] = dconv
        dz = cw[2:3, :] * dconv + cw[1:2, :] * dcbuf[1:1 + tm, :] + cw[0:1, :] * dcbuf[2:2 + tm, :]
        dgc = dz * v
        dv = dz * gc
        dcw_ref[0:1, :] += jnp.sum(dconv * z2, axis=0, keepdims=True)
        dcw_ref[1:2, :] += jnp.sum(dconv * z1, axis=0, keepdims=True)
        dcw_ref[2:3, :] += jnp.sum(dconv * z, axis=0, keepdims=True)

        dproj_ref[:, 0:dc] = dv.astype(BF16)
        dproj_ref[:, dc:2 * dc] = dgb.astype(BF16)
        dproj_ref[:, 2 * dc:3 * dc] = dgc.astype(BF16)

        row = tile * tm + lax.broadcasted_iota(jnp.int32, (tm, 1), 0)
        for gi, w in enumerate(POOL_WINDOWS):
            cols = slice(gi * gcw, (gi + 1) * gcw)
            pooled, cnt = _pool_parts(u[:, cols], ubuf, cols, w, row, tm)
            pooled_b = pooled.astype(BF16)
            pw_b = pw_ref[gi].astype(BF16)
            dyb = dy[:, dc + gi * gcw:dc + (gi + 1) * gcw]
            q = _nn(pooled_b, pw_b)
            dps_ref[:, cols] += jnp.sum(q * dyb, axis=0, keepdims=True)
            dq = (dyb * ps_ref[:, cols]).astype(BF16)
            dpw_ref[gi] += _tn(pooled_b, dq)
            dpooled = _nt(dq, pw_b)
            ebuf[0:tm, cols] = dpooled / cnt
            du = -dpooled
            for s in range(w):
                du = du + ebuf[s:s + tm, cols]
            dproj_ref[:, 3 * dc + gi * gcw:3 * dc + (gi + 1) * gcw] = du.astype(BF16)

        dh = _nn(dproj_ref[...], win[...])
        dg_ref[...] += jnp.sum(dh * xhat, axis=0, keepdims=True)
        dxh = dh * gv
        dx_ref[...] = dxo_v + r * (dxh - xhat * jnp.mean(dxh * xhat, axis=-1, keepdims=True))
        dcbuf[tm:tm + 8, :] = dcbuf[0:8, :]
        ebuf[tm:tm + HALO, :] = ebuf[0:HALO, :]

    tok = lambda i: (n_tiles - 1 - i, 0)
    halo = lambda i: (jnp.maximum((n_tiles - 1 - i) * hb - 1, 0), 0)
    one = lambda i: (0, 0)
    return _launch(
        body, name="mixer_backward", grid=(n_tiles,),
        in_specs=[pl.BlockSpec((tm, d), tok), pl.BlockSpec((tm, d), tok), pl.BlockSpec((1, d), one),
                  pl.BlockSpec((tm, 4 * dc), tok), pl.BlockSpec((HALO, 4 * dc), halo), HBM_SPEC, HBM_SPEC,
                  pl.BlockSpec(conv_w.shape, one), pl.BlockSpec(pool_w.shape, lambda i: (0, 0, 0)), pl.BlockSpec((1, dc), one)],
        out_specs=[pl.BlockSpec((tm, d), tok), pl.BlockSpec((tm, 4 * dc), tok), pl.BlockSpec((tm, d), tok), pl.BlockSpec((tm, d), tok),
                   pl.BlockSpec((1, d), one), pl.BlockSpec(conv_w.shape, one), pl.BlockSpec((1, dc), one),
                   pl.BlockSpec(pool_w.shape, lambda i: (0, 0, 0))],
        out_shape=[jax.ShapeDtypeStruct((t, d), F32), jax.ShapeDtypeStruct((t, 4 * dc), BF16), jax.ShapeDtypeStruct((t, d), BF16),
                   jax.ShapeDtypeStruct((t, d), BF16), jax.ShapeDtypeStruct((1, d), F32), jax.ShapeDtypeStruct(conv_w.shape, F32),
                   jax.ShapeDtypeStruct((1, dc), F32), jax.ShapeDtypeStruct(pool_w.shape, F32)],
        scratch_shapes=[pltpu.VMEM((4 * dc, d), BF16), pltpu.VMEM((2 * dc, d), BF16),
                        pltpu.VMEM((tm + HALO, dc), F32), pltpu.VMEM((tm + HALO, dc), F32),
                        pltpu.VMEM((tm + 8, dc), F32), pltpu.VMEM((tm + HALO, dc), F32), pltpu.SemaphoreType.DMA((1 + N_CHIPS,))],
        args=(dxo, x, g, proj, proj, win_t, wout_x, conv_w, pool_w, pool_scale), cargo=cargo)


def _adam_update(w, gv, m, v):
    m_new = ADAM_B1 * m + (1.0 - ADAM_B1) * gv
    v_new = ADAM_B2 * v + (1.0 - ADAM_B2) * (gv * gv)
    m_hat = m_new / (1.0 - ADAM_B1 ** ADAM_STEP)
    v_hat = v_new / (1.0 - ADAM_B2 ** ADAM_STEP)
    return -ADAM_LR * (m_hat / (jnp.sqrt(v_hat) + ADAM_EPS) + ADAM_WD * w), m_new, v_new


def _adamw(w, grad, m, v, name):
    rows, cols = w.shape
    br = _row_block(rows, 256) if rows >= 8 else rows

    def body(w_ref, g_ref, m_ref, v_ref, d_ref, mo_ref, vo_ref):
        d_ref[...], mo_ref[...], vo_ref[...] = _adam_update(w_ref[...], g_ref[...], m_ref[...], v_ref[...])

    blk = pl.BlockSpec((br, cols), lambda i: (i, 0))
    return pl.pallas_call(
        body, name=name,
        out_shape=[jax.ShapeDtypeStruct((rows, cols), F32)] * 3,
        grid=(rows // br,), in_specs=[blk] * 4, out_specs=[blk] * 3,
        compiler_params=pltpu.CompilerParams(dimension_semantics=("parallel",)),
    )(w, grad, m, v)


def _adamw_transposed(w, grad_t, m, v, name):
    _, rows, cols = w.shape
    br = 256 if rows % 256 == 0 else rows

    def body(w_ref, gt_ref, m_ref, v_ref, g_ref, d_ref, mo_ref, vo_ref):
        gv = gt_ref[...].T
        g_ref[...] = gv
        d_ref[...], mo_ref[...], vo_ref[...] = _adam_update(w_ref[...], gv, m_ref[...], v_ref[...])

    blk = pl.BlockSpec((None, br, cols), lambda i: (0, i, 0))
    return pl.pallas_call(
        body, name=name,
        out_shape=[jax.ShapeDtypeStruct((1, rows, cols), F32)] * 4,
        grid=(rows // br,), in_specs=[blk, pl.BlockSpec((cols, br), lambda i: (0, i)), blk, blk], out_specs=[blk] * 4,
        compiler_params=pltpu.CompilerParams(dimension_semantics=("parallel",)),
    )(w, grad_t, m, v)


def _f32_rows_as_bf16(a, rows, cols):
    bits = lax.bitcast_convert_type(a, BF16).reshape(a.shape[0], 2 * a.shape[1])
    return jnp.pad(bits, ((0, rows - bits.shape[0]), (0, cols - bits.shape[1])))


def kernel(x, norm_ffn1, ffn1_w_gate, ffn1_w_up, ffn1_w_down, norm_mix, w_in, conv_w, pool_w, pool_scale, w_out, norm_ffn2, ffn2_w_gate, ffn2_w_up, ffn2_w_down, norm_final, loss_target, m_norm_ffn1, m_ffn1_w_gate, m_ffn1_w_up, m_ffn1_w_down, m_norm_mix, m_w_in, m_conv_w, m_pool_w, m_pool_scale, m_w_out, m_norm_ffn2, m_ffn2_w_gate, m_ffn2_w_up, m_ffn2_w_down, m_norm_final, v_norm_ffn1, v_ffn1_w_gate, v_ffn1_w_up, v_ffn1_w_down, v_norm_mix, v_w_in, v_conv_w, v_pool_w, v_pool_scale, v_w_out, v_norm_ffn2, v_ffn2_w_gate, v_ffn2_w_up, v_ffn2_w_down, v_norm_final):
    weights = dict(norm_ffn1=norm_ffn1, ffn1_w_gate=ffn1_w_gate, ffn1_w_up=ffn1_w_up, ffn1_w_down=ffn1_w_down, norm_mix=norm_mix,
                   w_in=w_in, conv_w=conv_w, pool_w=pool_w, pool_scale=pool_scale, w_out=w_out, norm_ffn2=norm_ffn2,
                   ffn2_w_gate=ffn2_w_gate, ffn2_w_up=ffn2_w_up, ffn2_w_down=ffn2_w_down, norm_final=norm_final)
    first_m = dict(norm_ffn1=m_norm_ffn1, ffn1_w_gate=m_ffn1_w_gate, ffn1_w_up=m_ffn1_w_up, ffn1_w_down=m_ffn1_w_down,
                   norm_mix=m_norm_mix, w_in=m_w_in, conv_w=m_conv_w, pool_w=m_pool_w, pool_scale=m_pool_scale, w_out=m_w_out,
                   norm_ffn2=m_norm_ffn2, ffn2_w_gate=m_ffn2_w_gate, ffn2_w_up=m_ffn2_w_up, ffn2_w_down=m_ffn2_w_down,
                   norm_final=m_norm_final)
    second_m = dict(norm_ffn1=v_norm_ffn1, ffn1_w_gate=v_ffn1_w_gate, ffn1_w_up=v_ffn1_w_up, ffn1_w_down=v_ffn1_w_down,
                    norm_mix=v_norm_mix, w_in=v_w_in, conv_w=v_conv_w, pool_w=v_pool_w, pool_scale=v_pool_scale, w_out=v_w_out,
                    norm_ffn2=v_norm_ffn2, ffn2_w_gate=v_ffn2_w_gate, ffn2_w_up=v_ffn2_w_up, ffn2_w_down=v_ffn2_w_down,
                    norm_final=v_norm_final)
    names = list(weights)

    xs = x[0]
    tgt = loss_target[0]
    t, d = xs.shape
    dc = pool_scale.shape[1]
    cx, cy, cc = _my_place()
    chip = 2 * cx + cy
    place = jnp.stack([chip, cc]).astype(jnp.int32)

    conv_rows = 32
    wout_x = jnp.concatenate([w_out[0].astype(BF16), _f32_rows_as_bf16(conv_w[0], conv_rows, d)], axis=0)
    ffn1_shards = [ffn1_w_gate[0].T.astype(BF16), ffn1_w_up[0].T.astype(BF16), ffn1_w_down[0].astype(BF16)]
    wd2_shard = ffn2_w_down[0].astype(BF16)
    half_rows = wd2_shard.shape[0] // 2
    mid_shards = [w_in[0].T.astype(BF16), wout_x, ffn2_w_gate[0].T.astype(BF16), wd2_shard[:half_rows]]
    last_shards = [ffn2_w_up[0].T.astype(BF16), wd2_shard[half_rows:]]

    g1, gm, g2 = norm_ffn1, norm_mix, norm_ffn2
    gf = norm_final.reshape(1, d)
    pw = pool_w[0]

    wg1, wu1, wd1 = [[w] for w in _run_cargo(_gather_cargo(ffn1_shards), "gather_ffn1")]
    (x1, a1, b1), [(win_t, wout_g, wg2, wd2_a)] = _ffn_forward(xs, g1, wg1, wu1, wd1, "ffn1_forward", [_gather_cargo(mid_shards)])
    wo_rows = w_out.shape[1]
    cshard = conv_w.shape[2]
    conv_bits = wout_g.reshape(N_CHIPS, wo_rows + conv_rows, d)[:, wo_rows:wo_rows + conv_w.shape[1], :2 * cshard]
    conv_full = lax.bitcast_convert_type(conv_bits.reshape(N_CHIPS, conv_w.shape[1], cshard, 2), F32)
    conv_full = jnp.transpose(conv_full, (1, 0, 2)).reshape(conv_w.shape[1], N_CHIPS * cshard)
    (x2, proj, ymix), [(wu2, wd2_b)] = _mixer_forward(x1, gm, win_t, wout_g, conv_full, pw, pool_scale, [_gather_cargo(last_shards)])
    wg2, wu2, wd2 = [wg2], [wu2], [wd2_a, wd2_b]
    (dx3, a2, b2, sq_cols, dgf), _ = _ffn_forward(x2, g2, wg2, wu2, wd2, "ffn2_forward", loss_head=(gf, tgt))

    (dx2, da2, db2, s2, h3, do2, dg2), _ = _ffn_backward(dx3, x2, g2, a2, b2, wg2, wu2, wd2, "ffn2_backward")
    p_wg2, _ = _weight_grad(da2, h3, "ffn2_gate_grad")
    p_wu2, [(x_wg2,)] = _weight_grad(db2, h3, "ffn2_up_grad", [_exchange_cargo([p_wg2])])
    p_wd2, [(x_wu2,)] = _weight_grad(s2, do2, "ffn2_down_grad", [_exchange_cargo([p_wu2])])

    (dx1, dproj, h2, dx2b, dgm, dcw, dps, dpw), [(x_wd2,)] = _mixer_backward(
        dx2, x1, gm, proj, win_t, wout_g, conv_full, pw, pool_scale, [_exchange_cargo([p_wd2])])

    (dx0, da1, db1, s1, h1, do1, dg1), _ = _ffn_backward(dx1, xs, g1, a1, b1, wg1, wu1, wd1, "ffn1_backward")

    npw = pw.size // d
    head = [dg1, dgm, dg2, dgf, jnp.pad(dps, ((0, 0), (0, d - dc))), jnp.pad(dcw, ((0, 0), (0, d - dc))), sq_cols]
    n_head = sum(h.shape[0] for h in head)
    base = -(-n_head // 8) * 8
    pack = jnp.concatenate(head + [jnp.zeros((base - n_head, d), F32), dpw.reshape(npw, d)], axis=0)

    p_wg1, [(packs,)] = _weight_grad(da1, h1, "ffn1_gate_grad", [_all_gather_small_cargo(pack)])
    p_wu1, [(x_wg1,)] = _weight_grad(db1, h1, "ffn1_up_grad", [_exchange_cargo([p_wg1])])
    p_wd1, [(x_wu1,)] = _weight_grad(s1, do1, "ffn1_down_grad", [_exchange_cargo([p_wu1])])
    p_win, [(x_wd1,)] = _weight_grad(dproj, h2, "w_in_grad", [_exchange_cargo([p_wd1])])
    p_wout, [(x_win,)] = _weight_grad(ymix, dx2b, "w_out_grad", [_exchange_cargo([p_win])])
    x_wout, = _run_cargo(_exchange_cargo([p_wout]), "grad_exchange_last")
    small = _sum_by_device(packs)
    loss = jnp.sum(small[n_head - 1]) * (0.5 / d)

    order = ["wg1", "wu1", "wd1", "win", "wout", "wg2", "wu2", "wd2"]
    pairs = dict(wg1=p_wg1, wu1=p_wu1, wd1=p_wd1, win=p_win, wout=p_wout, wg2=p_wg2, wu2=p_wu2, wd2=p_wd2)
    landed = dict(wg1=x_wg1, wu1=x_wu1, wd1=x_wd1, win=x_win, wout=x_wout, wg2=x_wg2, wu2=x_wu2, wd2=x_wd2)
    both = _sibling_share([_chip_sum(pairs[k], landed[k], place, k) for k in order])
    rwg1, rwu1, rwd1, rwin, rwout, rwg2, rwu2, rwd2 = [b.reshape(2 * b.shape[1], b.shape[2]) for b in both]

    grads = {
        "norm_ffn1": small[0:1], "norm_mix": small[1:2], "norm_ffn2": small[2:3], "norm_final": small[3],
        "pool_scale": small[4:5, :dc],
        "conv_w": lax.dynamic_slice_in_dim(small[5:5 + dcw.shape[0], :dc], chip * cshard, cshard, axis=1)[None],
        "pool_w": small[base:].reshape(pool_w.shape),
        "ffn1_w_down": rwd1[None], "w_out": rwout[None], "ffn2_w_down": rwd2[None],
    }
    by_view = {"ffn1_w_gate": rwg1, "ffn1_w_up": rwu1, "ffn2_w_gate": rwg2, "ffn2_w_up": rwu2}

    deltas, new_m, new_v = {}, {}, {}
    for n in names:
        w = weights[n]
        shape = w.shape
        if n == "w_in":
            grads[n], deltas[n], new_m[n], new_v[n] = _adamw_transposed(w, rwin, first_m[n], second_m[n], "adamw_" + n)
            continue
        if n in by_view:
            view = lambda a: jnp.swapaxes(a, 1, 2)[0]
            back = lambda a: jnp.swapaxes(a[None], 1, 2)
            dl, mo, vo = _adamw(view(w), by_view[n], view(first_m[n]), view(second_m[n]), "adamw_" + n)
            grads[n], deltas[n], new_m[n], new_v[n] = back(by_view[n]), back(dl), back(mo), back(vo)
            continue
        as2d = (lambda a: a.reshape(-1, shape[-1]))
        dl, mo, vo = _adamw(as2d(w), as2d(grads[n]), as2d(first_m[n]), as2d(second_m[n]), "adamw_" + n)
        deltas[n], new_m[n], new_v[n] = dl.reshape(shape), mo.reshape(shape), vo.reshape(shape)
        grads[n] = grads[n].reshape(shape)

    return (loss, dx0[None], *[grads[n] for n in names], *[deltas[n] for n in names],
            *[new_m[n] for n in names], *[new_v[n] for n in names])
```

```python
import functools

import jax
import jax.numpy as jnp
from jax import lax
from jax.experimental import pallas as pl
from jax.experimental.pallas import tpu as pltpu

F32 = jnp.float32
BF16 = jnp.bfloat16
MESH = pl.DeviceIdType.MESH

EPS = 1e-6
POOL_WINDOWS = (2, 4, 8, 16)
ADAM_LR = 0.001
ADAM_B1 = 0.9
ADAM_B2 = 0.999
ADAM_EPS = 1e-08
ADAM_WD = 0.01
ADAM_STEP = 10

N_CHIPS = 4
N_DEVICES = 8
MXU_COLS_V7X = 256
VMEM_LIMIT = 56 * 1024 * 1024
TM_FFN = 512
TM_MIX = 512
TM_TN = 1024
HALO = 16
FFN_FWD_CHUNKS = 2
FFN_BWD_CHUNKS = 2


def _nt(a, b):
    return lax.dot_general(a, b, (((1,), (1,)), ((), ())), preferred_element_type=F32)


def _tn(a, b):
    return lax.dot_general(a, b, (((0,), (0,)), ((), ())), preferred_element_type=F32)


def _nn(a, b):
    return jnp.dot(a, b, preferred_element_type=F32)


def _sigmoid(a):
    return 1.0 / (1.0 + jnp.exp(-a))


def _feature_chunks(n, parts):
    assert n % MXU_COLS_V7X == 0
    tiles = n // MXU_COLS_V7X
    out, s0 = [], 0
    for p in range(parts):
        sz = (tiles // parts + (1 if p < tiles % parts else 0)) * MXU_COLS_V7X
        if sz:
            out.append((s0, sz))
            s0 += sz
    return out


def _row_block(rows, cap):
    best = 8
    for b in range(8, min(rows, cap) + 1, 8):
        if rows % b == 0:
            best = b
    assert rows % best == 0
    return best


def _my_place():
    return lax.axis_index("x"), lax.axis_index("y"), lax.axis_index("c")


def _other_chips(x, y):
    return [(1 - x, y), (x, 1 - y), (1 - x, 1 - y)]


HBM_SPEC = pl.BlockSpec(memory_space=pltpu.HBM)


class _Cargo:
    def __init__(self, operands, out_shapes, n_sems, start, finish):
        self.operands, self.out_shapes, self.n_sems, self.start, self.finish = list(operands), list(out_shapes), n_sems, start, finish


def _launch(body, *, name, grid, in_specs, out_specs, out_shape, scratch_shapes, args, cargo=()):
    params = pltpu.CompilerParams(dimension_semantics=("arbitrary",) * len(grid), vmem_limit_bytes=VMEM_LIMIT)
    cargos = list(cargo)
    c_operands = [op for cg in cargos for op in cg.operands]
    c_shapes = [sh for cg in cargos for sh in cg.out_shapes]
    counts = [len(in_specs), len(c_operands), len(out_shape), len(c_shapes), len(scratch_shapes), 2 * len(cargos)]

    def carrying(*refs):
        groups, pos = [], 0
        for k in counts:
            groups.append(refs[pos:pos + k])
            pos += k
        ins, c_ins, outs, c_outs, scratch, sems = groups
        parts, pi, po = [], 0, 0
        for n, cg in enumerate(cargos):
            parts.append((c_ins[pi:pi + len(cg.operands)], c_outs[po:po + len(cg.out_shapes)], sems[2 * n], sems[2 * n + 1]))
            pi += len(cg.operands)
            po += len(cg.out_shapes)
        ids = [pl.program_id(ax) for ax in range(len(grid))]
        first = functools.reduce(jnp.logical_and, [i == 0 for i in ids])
        last = functools.reduce(jnp.logical_and, [i == g - 1 for i, g in zip(ids, grid)])

        if cargos:
            @pl.when(first)
            def _():
                for cg, part in zip(cargos, parts):
                    cg.start(*part)

        body(*ins, *outs, *scratch)

        if cargos:
            @pl.when(last)
            def _():
                for cg, part in zip(cargos, parts):
                    cg.finish(*part)

    sems = [pltpu.SemaphoreType.DMA((cg.n_sems,)) for cg in cargos for _ in range(2)]
    outs = pl.pallas_call(
        carrying, name=name, grid=grid,
        in_specs=list(in_specs) + [HBM_SPEC] * counts[1], out_specs=list(out_specs) + [HBM_SPEC] * counts[3],
        out_shape=list(out_shape) + c_shapes, scratch_shapes=list(scratch_shapes) + sems,
        compiler_params=params)(*args, *c_operands)
    own, rest = list(outs[:counts[2]]), list(outs[counts[2]:])
    carried, po = [], 0
    for cg in cargos:
        carried.append(rest[po:po + len(cg.out_shapes)])
        po += len(cg.out_shapes)
    return own, carried


def _run_cargo(cargo, name):
    n_in, n_out = len(cargo.operands), len(cargo.out_shapes)

    def body(*refs):
        c_ins, c_outs, sems = refs[:n_in], refs[n_in:n_in + n_out], refs[n_in + n_out:]
        cargo.start(c_ins, c_outs, *sems)
        cargo.finish(c_ins, c_outs, *sems)

    sem = pltpu.SemaphoreType.DMA((cargo.n_sems,))
    return list(pl.pallas_call(body, name=name, out_shape=cargo.out_shapes, in_specs=[HBM_SPEC] * n_in,
                               out_specs=[HBM_SPEC] * n_out, scratch_shapes=[sem, sem])(*cargo.operands))


def _gather_cargo(shards):
    n = len(shards)
    for s in shards:
        assert s.shape[0] % 32 == 0

    def steps(ins, outs, send_sems, recv_sems):
        x, y, c = _my_place()
        sibling = (x, y, 1 - c)
        chips = _other_chips(x, y)
        mine = 2 * x + y

        def rows_of(a, chip_index, half):
            rps = shards[a].shape[0]
            hr = rps // 2
            return outs[a].at[pl.ds(pl.multiple_of(chip_index * rps + half * hr, 16), hr), :]

        def remote(a, slot, src, dst, to):
            return pltpu.make_async_remote_copy(
                src_ref=src, dst_ref=dst, send_sem=send_sems.at[a * 7 + slot], recv_sem=recv_sems.at[a * 7 + slot],
                device_id=to, device_id_type=MESH)

        def own_copy(a):
            rps = shards[a].shape[0]
            return remote(a, 6, ins[a], outs[a].at[pl.ds(pl.multiple_of(mine * rps, 16), rps), :], sibling)

        def my_half(a):
            hr = shards[a].shape[0] // 2
            return ins[a].at[pl.ds(pl.multiple_of(c * hr, 16), hr), :]

        def start():
            for a in range(n):
                own_copy(a).start()
                for j, chip in enumerate(chips):
                    remote(a, j, my_half(a), rows_of(a, mine, c), (*chip, c)).start()

        def finish():
            for a in range(n):
                for j, chip in enumerate(chips):
                    landed = rows_of(a, 2 * chip[0] + chip[1], c)
                    remote(a, j, landed, landed, (*chip, c)).wait_recv()
                    remote(a, 3 + j, landed, landed, sibling).start()
            for a in range(n):
                for j, chip in enumerate(chips):
                    from_sibling = rows_of(a, 2 * chip[0] + chip[1], 1 - c)
                    remote(a, 3 + j, from_sibling, from_sibling, sibling).wait_recv()
            for a in range(n):
                for j, chip in enumerate(chips):
                    remote(a, j, my_half(a), rows_of(a, mine, c), (*chip, c)).wait_send()
                    landed = rows_of(a, 2 * chip[0] + chip[1], c)
                    remote(a, 3 + j, landed, landed, sibling).wait_send()
                own_copy(a).wait()

        return start, finish

    return _Cargo(shards, [jax.ShapeDtypeStruct((N_CHIPS * s.shape[0], s.shape[1]), s.dtype) for s in shards], 7 * n,
                  lambda *r: steps(*r)[0](), lambda *r: steps(*r)[1]())


def _exchange_cargo(pairs):
    n = len(pairs)

    def copies(ins, outs, send_sems, recv_sems):
        x, y, c = _my_place()
        return [pltpu.make_async_remote_copy(
            src_ref=ins[a].at[2 * chip[0] + chip[1]], dst_ref=outs[a].at[j],
            send_sem=send_sems.at[3 * a + j], recv_sem=recv_sems.at[3 * a + j], device_id=(*chip, c), device_id_type=MESH)
            for a in range(n) for j, chip in enumerate(_other_chips(x, y))]

    def start(*r):
        for cp in copies(*r):
            cp.start()

    def finish(*r):
        for cp in copies(*r):
            cp.wait()

    return _Cargo(pairs, [jax.ShapeDtypeStruct((3,) + p.shape[1:], p.dtype) for p in pairs], 3 * n, start, finish)


def _all_gather_small_cargo(pack):
    rows, cols = pack.shape

    def copies(ins, outs, send_sems, recv_sems):
        x, y, c = _my_place()
        me = 4 * x + 2 * y + c
        remote = []
        for f in range(1, N_DEVICES):
            fx, fy, fc = (f >> 2) & 1, (f >> 1) & 1, f & 1
            to = (1 - x if fx else x, 1 - y if fy else y, 1 - c if fc else c)
            remote.append(pltpu.make_async_remote_copy(
                src_ref=ins[0], dst_ref=outs[0].at[me], send_sem=send_sems.at[f - 1], recv_sem=recv_sems.at[f - 1],
                device_id=to, device_id_type=MESH))
        own = pltpu.make_async_copy(ins[0], outs[0].at[me], send_sems.at[N_DEVICES - 1])
        return remote, own

    def start(*r):
        remote, own = copies(*r)
        own.start()
        for cp in remote:
            cp.start()

    def finish(*r):
        remote, own = copies(*r)
        for cp in remote:
            cp.wait()
        own.wait()

    return _Cargo([pack], [jax.ShapeDtypeStruct((N_DEVICES, rows, cols), F32)], N_DEVICES, start, finish)


def _sum_by_device(packs):
    n, rows, cols = packs.shape

    def body(p_ref, o_ref):
        acc = p_ref[0]
        for dev in range(1, n):
            acc = acc + p_ref[dev]
        o_ref[...] = acc

    return pl.pallas_call(body, name="small_grads_sum", out_shape=jax.ShapeDtypeStruct((rows, cols), F32))(packs)


def _chip_sum(pair, got, place, tag):
    _, hr, cols = pair.shape
    br = _row_block(hr, 256)

    def body(k_ref, p_ref, r_ref, o_ref):
        acc = p_ref[...].astype(F32)
        for j in range(3):
            acc = acc + r_ref[j].astype(F32)
        o_ref[...] = acc

    return pl.pallas_call(
        body, name="grad_chip_sum_" + tag,
        out_shape=jax.ShapeDtypeStruct((2, hr, cols), F32),
        grid_spec=pltpu.PrefetchScalarGridSpec(
            num_scalar_prefetch=1, grid=(hr // br,),
            in_specs=[pl.BlockSpec((None, br, cols), lambda r, k_ref: (k_ref[0], r, 0)),
                      pl.BlockSpec((3, br, cols), lambda r, k_ref: (0, r, 0))],
            out_specs=pl.BlockSpec((None, br, cols), lambda r, k_ref: (k_ref[1], r, 0))),
        compiler_params=pltpu.CompilerParams(dimension_semantics=("parallel",)),
    )(place, pair, got)


def _sibling_share(halves):
    n = len(halves)

    def body(*refs):
        outs = refs[n:2 * n]
        send_sems, recv_sems = refs[2 * n:]
        x, y, c = _my_place()
        copies = []
        for a in range(n):
            cp = pltpu.make_async_remote_copy(
                src_ref=outs[a].at[c], dst_ref=outs[a].at[c], send_sem=send_sems.at[a], recv_sem=recv_sems.at[a],
                device_id=(x, y, 1 - c), device_id_type=MESH)
            cp.start()
            copies.append(cp)
        for cp in copies:
            cp.wait()

    return pl.pallas_call(
        body, name="grad_share_sibling",
        out_shape=[jax.ShapeDtypeStruct(h.shape, h.dtype) for h in halves],
        in_specs=[HBM_SPEC] * n, out_specs=[HBM_SPEC] * n,
        input_output_aliases={a: a for a in range(n)},
        scratch_shapes=[pltpu.SemaphoreType.DMA((n,)), pltpu.SemaphoreType.DMA((n,))],
    )(*halves)


def _load_rows(pairs, sems):
    cps = [pltpu.make_async_copy(src, dst, sems.at[j]) for j, (src, dst) in enumerate(pairs)]
    for cp in cps:
        cp.start()
    for cp in cps:
        cp.wait()


def _piece_rows(weights):
    flat = [p for pieces in weights for p in pieces]

    def copies(refs, mats):
        out, n = [], 0
        for pieces, mat in zip(weights, mats):
            rps = sum(p.shape[0] for p in pieces) // N_CHIPS
            off = 0
            for p in pieces:
                r = p.shape[0] // N_CHIPS
                if len(pieces) == 1:
                    out.append((refs[n], mat))
                else:
                    for k in range(N_CHIPS):
                        out.append((refs[n].at[pl.ds(k * r, r), :], mat.at[pl.ds(k * rps + off, r), :]))
                off += r
                n += 1
        return out

    n_copies = sum(1 if len(pieces) == 1 else N_CHIPS * len(pieces) for pieces in weights)
    return flat, copies, n_copies


def _loss_head(xv, gv, tv):
    d = xv.shape[-1]
    r = lax.rsqrt(jnp.mean(xv * xv, axis=-1, keepdims=True) + EPS)
    xhat = xv * r
    err = xhat * gv - tv
    dy = err * (1.0 / d)
    dxh = dy * gv
    dx = r * (dxh - xhat * jnp.mean(dxh * xhat, axis=-1, keepdims=True))
    return dx, jnp.sum(err * err, axis=0, keepdims=True), jnp.sum(dy * xhat, axis=0, keepdims=True)


def _ffn_up(x, g, wg_t, wu_t, name, cargo=()):
    t, d = x.shape
    f = sum(p.shape[0] for p in wg_t)
    tm = min(TM_FFN, t)
    chunks = _feature_chunks(f, FFN_FWD_CHUNKS)
    flat, copies, n_copies = _piece_rows([wg_t, wu_t])
    nw = len(flat)

    def body(x_ref, g_ref, *rest):
        w_hbm, (a_ref, b_ref, s_ref, wg, wu, sems) = rest[:nw], rest[nw:]

        @pl.when(pl.program_id(0) == 0)
        def _():
            _load_rows(copies(w_hbm, [wg, wu]), sems)

        xv = x_ref[...]
        r = lax.rsqrt(jnp.mean(xv * xv, axis=-1, keepdims=True) + EPS)
        h = (xv * r * g_ref[...]).astype(BF16)
        for s0, sz in chunks:
            a = _nt(h, wg[s0:s0 + sz, :])
            b = _nt(h, wu[s0:s0 + sz, :])
            a_ref[:, s0:s0 + sz] = a.astype(BF16)
            b_ref[:, s0:s0 + sz] = b.astype(BF16)
            s_ref[:, s0:s0 + sz] = (a * _sigmoid(a) * b).astype(BF16)

    tok = lambda i: (i, 0)
    wide = pl.BlockSpec((tm, f), tok)
    return _launch(
        body, name=name, grid=(t // tm,),
        in_specs=[pl.BlockSpec((tm, d), tok), pl.BlockSpec((1, d), lambda i: (0, 0))] + [HBM_SPEC] * nw,
        out_specs=[wide, wide, wide], out_shape=[jax.ShapeDtypeStruct((t, f), BF16)] * 3,
        scratch_shapes=[pltpu.VMEM((f, d), BF16), pltpu.VMEM((f, d), BF16), pltpu.SemaphoreType.DMA((n_copies,))],
        args=(x, g, *flat), cargo=cargo)


def _ffn_down(x, s, wd, name, cargo=(), loss_head=None):
    t, d = x.shape
    f = s.shape[1]
    tm = min(TM_FFN, t)
    flat, copies, n_copies = _piece_rows([wd])
    nw = len(flat)
    nl = 2 if loss_head else 0

    def body(x_ref, s_ref, *rest):
        head, w_hbm = rest[:nl], rest[nl:nl + nw]
        xo_ref = rest[nl + nw]
        sums, (wdn, sems) = rest[nl + nw + 1:nl + nw + 1 + nl], rest[nl + nw + 1 + nl:]

        @pl.when(pl.program_id(0) == 0)
        def _():
            _load_rows(copies(w_hbm, [wdn]), sems)
            for sum_ref in sums:
                sum_ref[...] = jnp.zeros_like(sum_ref)

        xo = x_ref[...] + 0.5 * _nn(s_ref[...], wdn[...])
        if loss_head:
            dx, sq, dgf = _loss_head(xo, head[0][...], head[1][...])
            xo_ref[...] = dx
            sums[0][...] += sq
            sums[1][...] += dgf
        else:
            xo_ref[...] = xo

    tok = lambda i: (i, 0)
    one = lambda i: (0, 0)
    return _launch(
        body, name=name, grid=(t // tm,),
        in_specs=[pl.BlockSpec((tm, d), tok), pl.BlockSpec((tm, f), tok)]
        + ([pl.BlockSpec((1, d), one), pl.BlockSpec((tm, d), tok)] if loss_head else []) + [HBM_SPEC] * nw,
        out_specs=[pl.BlockSpec((tm, d), tok)] + [pl.BlockSpec((1, d), one)] * nl,
        out_shape=[jax.ShapeDtypeStruct((t, d), F32)] + [jax.ShapeDtypeStruct((1, d), F32)] * nl,
        scratch_shapes=[pltpu.VMEM((f, d), BF16), pltpu.SemaphoreType.DMA((n_copies,))],
        args=(x, s, *(loss_head or ()), *flat), cargo=cargo)


def _ffn_backward(dxo, x, g, a, b, wg_t, wu_t, wd, name, cargo=()):
    t, d = x.shape
    f = sum(p.shape[0] for p in wd)
    tm = min(TM_FFN // 2, t)
    chunks = _feature_chunks(f, FFN_BWD_CHUNKS)
    flat, copies, n_copies = _piece_rows([wg_t, wu_t, wd])
    nw = len(flat)

    def body(dxo_ref, x_ref, g_ref, a_ref, b_ref, *rest):
        w_hbm, (dx_ref, da_ref, db_ref, h_ref, do_ref, dg_ref, wg, wu, wdn, sems) = rest[:nw], rest[nw:]

        @pl.when(pl.program_id(0) == 0)
        def _():
            _load_rows(copies(w_hbm, [wg, wu, wdn]), sems)
            dg_ref[...] = jnp.zeros_like(dg_ref)

        xv = x_ref[...]
        gv = g_ref[...]
        r = lax.rsqrt(jnp.mean(xv * xv, axis=-1, keepdims=True) + EPS)
        xhat = xv * r
        h_ref[...] = (xhat * gv).astype(BF16)
        dxo_v = dxo_ref[...]
        dout = (0.5 * dxo_v).astype(BF16)
        do_ref[...] = dout
        dh = jnp.zeros((tm, d), F32)
        for s0, sz in chunks:
            ds = _nt(dout, wdn[s0:s0 + sz, :])
            av = a_ref[:, s0:s0 + sz].astype(F32)
            bv = b_ref[:, s0:s0 + sz].astype(F32)
            sig = _sigmoid(av)
            silu = av * sig
            da =(ds * bv * (sig * (1.0 + av * (1.0 - sig)))).astype(BF16)
            db = (ds * silu).astype(BF16)
            da_ref[:, s0:s0 + sz] = da
            db_ref[:, s0:s0 + sz] = db
            dh = dh + _nn(da, wg[s0:s0 + sz, :]) + _nn(db, wu[s0:s0 + sz, :])
        dg_ref[...] += jnp.sum(dh * xhat, axis=0, keepdims=True)
        dxh = dh * gv
        dx_ref[...] = dxo_v + r * (dxh - xhat * jnp.mean(dxh * xhat, axis=-1, keepdims=True))

    tok = lambda i: (i, 0)
    one = lambda i: (0, 0)
    return _launch(
        body, name=name, grid=(t // tm,),
        in_specs=[pl.BlockSpec((tm, d), tok), pl.BlockSpec((tm, d), tok), pl.BlockSpec((1, d), one),
                  pl.BlockSpec((tm, f), tok), pl.BlockSpec((tm, f), tok)] + [HBM_SPEC] * nw,
        out_specs=[pl.BlockSpec((tm, d), tok), pl.BlockSpec((tm, f), tok), pl.BlockSpec((tm, f), tok),
                   pl.BlockSpec((tm, d), tok), pl.BlockSpec((tm, d), tok), pl.BlockSpec((1, d), one)],
        out_shape=[jax.ShapeDtypeStruct((t, d), F32), jax.ShapeDtypeStruct((t, f), BF16), jax.ShapeDtypeStruct((t, f), BF16),
                   jax.ShapeDtypeStruct((t, d), BF16), jax.ShapeDtypeStruct((t, d), BF16), jax.ShapeDtypeStruct((1, d), F32)],
        scratch_shapes=[pltpu.VMEM((f, d), BF16), pltpu.VMEM((f, d), BF16), pltpu.VMEM((f, d), BF16), pltpu.SemaphoreType.DMA((n_copies,))],
        args=(dxo, x, g, a, b, *flat), cargo=cargo)


def _weight_grad(lhs, rhs, name, cargo=()):
    t, m = lhs.shape
    d = rhs.shape[1]
    tm = min(TM_TN, t)
    nt = t // tm
    nj = 2 if (m // 2) % 128 == 0 and m > 1024 else 1
    bm = m // nj
    cpb = N_CHIPS // nj
    rps = m // N_CHIPS
    hr = rps // 2
    assert hr % 16 == 0

    def body(l_ref, r_ref, o_ref, acc, stage, recv, send_sems, recv_sems):
        j = pl.program_id(0)
        i = pl.program_id(1)
        @pl.when(i == 0)
        def _():
            acc[...] = jnp.zeros_like(acc)

        acc[...] += _tn(l_ref[...], r_ref[...])

        def pair_sum(jj):
            x, y, c = _my_place()
            copies = []
            for q in range(cpb):
                slot = jj * cpb + q
                stage[slot] = acc[pl.ds(pl.multiple_of(q * rps + (1 - c) * hr, 16), hr), :].astype(BF16)
                cp = pltpu.make_async_remote_copy(
                    src_ref=stage.at[slot], dst_ref=recv.at[slot], send_sem=send_sems.at[slot], recv_sem=recv_sems.at[slot],
                    device_id=(x, y, 1 - c), device_id_type=MESH)
                cp.start()
                copies.append(cp)
            for q, cp in enumerate(copies):
                cp.wait_recv()
                mine = acc[pl.ds(pl.multiple_of(q * rps + c * hr, 16), hr), :]
                o_ref[q] = (mine + recv[jj * cpb + q].astype(F32)).astype(BF16)
            for cp in copies:
                cp.wait_send()

        for jj in range(nj):
            @pl.when(jnp.logical_and(i == nt - 1, j == jj))
            def _():
                pair_sum(jj)

    outs, carried = _launch(
        body, name=name, grid=(nj, nt),
        in_specs=[pl.BlockSpec((tm, bm), lambda j, i: (i, j)), pl.BlockSpec((tm, d), lambda j, i: (i, 0))],
        out_specs=[pl.BlockSpec((cpb, hr, d), lambda j, i: (j, 0, 0))],
        out_shape=[jax.ShapeDtypeStruct((N_CHIPS, hr, d), BF16)],
        scratch_shapes=[pltpu.VMEM((bm, d), F32), pltpu.VMEM((N_CHIPS, hr, d), BF16), pltpu.VMEM((N_CHIPS, hr, d), BF16),
                        pltpu.SemaphoreType.DMA((N_CHIPS,)), pltpu.SemaphoreType.DMA((N_CHIPS,))],
        args=(lhs, rhs), cargo=cargo)
    return outs[0], carried


def _pool_parts(u_cols, ubuf, cols, w, row, tm):
    ws = u_cols
    for s in range(1, w):
        ws = ws + ubuf[HALO - s:HALO - s + tm, cols]
    cnt = jnp.minimum(row + 1, w).astype(F32)
    return ws / cnt - u_cols, cnt


def _mixer_forward(x, g, win_t, wout_x, conv_w, pool_w, pool_scale, cargo=()):
    t, d = x.shape
    dc = win_t.shape[0] // 4
    gcw = dc // len(POOL_WINDOWS)
    wo_rows = d // N_CHIPS
    wo_stride = wout_x.shape[0] // N_CHIPS
    tm = min(TM_MIX, t)

    def body(x_ref, g_ref, win_hbm, wout_hbm, cw_ref, pw_ref, ps_ref, xo_ref, proj_ref, y_ref,
             win, wout, zbuf, ubuf, sems):
        i = pl.program_id(0)

        @pl.when(i == 0)
        def _():
            pairs = [(win_hbm, win)]
            for k in range(N_CHIPS):
                pairs.append((wout_hbm.at[pl.ds(k * wo_stride, wo_rows), :], wout.at[pl.ds(k * wo_rows, wo_rows), :]))
            _load_rows(pairs, sems)
            zbuf[0:8, :] = jnp.zeros((8, dc), F32)
            ubuf[0:HALO, :] = jnp.zeros((HALO, dc), F32)

        xv = x_ref[...]
        r = lax.rsqrt(jnp.mean(xv * xv, axis=-1, keepdims=True) + EPS)
        h = (xv * r * g_ref[...]).astype(BF16)
        v = _nt(h, win[0:dc, :])
        gb = _nt(h, win[dc:2 * dc, :])
        gc = _nt(h, win[2 * dc:3 * dc, :])
        u = _nt(h, win[3 * dc:4 * dc, :])
        proj_ref[:, 0:dc] = v.astype(BF16)
        proj_ref[:, dc:2 * dc] = gb.astype(BF16)
        proj_ref[:, 2 * dc:3 * dc] = gc.astype(BF16)
        proj_ref[:, 3 * dc:4 * dc] = u.astype(BF16)

        z = gc * v
        zbuf[8:8 + tm, :] = z
        cw = cw_ref[...]
        conv = cw[2:3, :] * z + cw[1:2, :] * zbuf[7:7 + tm, :] + cw[0:1, :] * zbuf[6:6 + tm, :]
        y_ref[:, 0:dc] = (gb * conv).astype(BF16)

        ubuf[HALO:HALO + tm, :] = u
        row = i * tm + lax.broadcasted_iota(jnp.int32, (tm, 1), 0)
        for gi, w in enumerate(POOL_WINDOWS):
            cols = slice(gi * gcw, (gi + 1) * gcw)
            pooled, _ = _pool_parts(u[:, cols], ubuf, cols, w, row, tm)
            yb = _nn(pooled.astype(BF16), pw_ref[gi].astype(BF16)) * ps_ref[:, cols]
            y_ref[:, dc + gi * gcw:dc + (gi + 1) * gcw] = yb.astype(BF16)

        xo_ref[...] = xv + _nn(y_ref[...], wout[...])
        zbuf[0:8, :] = zbuf[tm:tm + 8, :]
        ubuf[0:HALO, :] = ubuf[tm:tm + HALO, :]

    tok = lambda i: (i, 0)
    one = lambda i: (0, 0)
    return _launch(
        body, name="mixer_forward", grid=(t // tm,),
        in_specs=[pl.BlockSpec((tm, d), tok), pl.BlockSpec((1, d), one), HBM_SPEC, HBM_SPEC,
                  pl.BlockSpec(conv_w.shape, one), pl.BlockSpec(pool_w.shape, lambda i: (0, 0, 0)), pl.BlockSpec((1, dc), one)],
        out_specs=[pl.BlockSpec((tm, d), tok), pl.BlockSpec((tm, 4 * dc), tok), pl.BlockSpec((tm, 2 * dc), tok)],
        out_shape=[jax.ShapeDtypeStruct((t, d), F32), jax.ShapeDtypeStruct((t, 4 * dc), BF16), jax.ShapeDtypeStruct((t, 2 * dc), BF16)],
        scratch_shapes=[pltpu.VMEM((4 * dc, d), BF16), pltpu.VMEM((2 * dc, d), BF16),
                        pltpu.VMEM((tm + 8, dc), F32), pltpu.VMEM((tm + HALO, dc), F32), pltpu.SemaphoreType.DMA((1 + N_CHIPS,))],
        args=(x, g, win_t, wout_x, conv_w, pool_w, pool_scale), cargo=cargo)


def _mixer_backward(dxo, x, g, proj, win_t, wout_x, conv_w, pool_w, pool_scale, cargo=()):
    t, d = x.shape
    dc = win_t.shape[0] // 4
    ng = len(POOL_WINDOWS)
    gcw = dc // ng
    wo_rows = d // N_CHIPS
    wo_stride = wout_x.shape[0] // N_CHIPS
    tm = min(TM_MIX // 2, t)
    n_tiles = t // tm
    hb = tm // HALO

    def body(dxo_ref, x_ref, g_ref, proj_ref, halo_ref, win_hbm, wout_hbm, cw_ref, pw_ref, ps_ref,
             dx_ref, dproj_ref, h_ref, dxob_ref, dg_ref, dcw_ref, dps_ref, dpw_ref,
             win, wout, zbuf, ubuf, dcbuf, ebuf, sems):
        i = pl.program_id(0)
        tile = n_tiles - 1 - i

        @pl.when(i == 0)
        def _():
            pairs = [(win_hbm, win)]
            for k in range(N_CHIPS):
                pairs.append((wout_hbm.at[pl.ds(k * wo_stride, wo_rows), :], wout.at[pl.ds(k * wo_rows, wo_rows), :]))
            _load_rows(pairs, sems)
            dcbuf[tm:tm + 8, :] = jnp.zeros((8, dc), F32)
            ebuf[tm:tm + HALO, :] = jnp.zeros((HALO, dc), F32)
            dg_ref[...] = jnp.zeros_like(dg_ref)
            dcw_ref[...] = jnp.zeros_like(dcw_ref)
            dps_ref[...] = jnp.zeros_like(dps_ref)
            dpw_ref[...] = jnp.zeros_like(dpw_ref)

        xv = x_ref[...]
        gv = g_ref[...]
        r = lax.rsqrt(jnp.mean(xv * xv, axis=-1, keepdims=True) + EPS)
        xhat = xv * r
        h_ref[...] = (xhat * gv).astype(BF16)
        dxo_v = dxo_ref[...]
        dxo_b = dxo_v.astype(BF16)
        dxob_ref[...] = dxo_b

        v = proj_ref[:, 0:dc].astype(F32)
        gb = proj_ref[:, dc:2 * dc].astype(F32)
        gc = proj_ref[:, 2 * dc:3 * dc].astype(F32)
        u = proj_ref[:, 3 * dc:4 * dc].astype(F32)
        first = jnp.where(tile > 0, 1.0, 0.0)
        zbuf[0:HALO, :] = halo_ref[:, 2 * dc:3 * dc].astype(F32) * halo_ref[:, 0:dc].astype(F32) * first
        ubuf[0:HALO, :] = halo_ref[:, 3 * dc:4 * dc].astype(F32) * first
        z = gc * v
        zbuf[HALO:HALO + tm, :] = z
        ubuf[HALO:HALO + tm, :] = u
        z1 = zbuf[HALO - 1:HALO - 1 + tm, :]
        z2 = zbuf[HALO - 2:HALO - 2 + tm, :]
        cw = cw_ref[...]
        conv = cw[2:3, :] * z + cw[1:2, :] * z1 + cw[0:1, :] * z2

        dy = _nt(dxo_b, wout[...])
        dya = dy[:, 0:dc]
        dgb = dya * conv
        dconv = dya * gb
        dcbuf[0:tm, :] = dconv
        dz = cw[2:3, :] * dconv + cw[1:2, :] * dcbuf[1:1 + tm, :] + cw[0:1, :] * dcbuf[2:2 + tm, :]
        dgc = dz * v
        dv = dz * gc
        dcw_ref[0:1, :] += jnp.sum(dconv * z2, axis=0, keepdims=True)
        dcw_ref[1:2, :] += jnp.sum(dconv * z1, axis=0, keepdims=True)
        dcw_ref[2:3, :] += jnp.sum(dconv * z, axis=0, keepdims=True)

        dproj_ref[:, 0:dc] = dv.astype(BF16)
        dproj_ref[:, dc:2 * dc] = dgb.astype(BF16)
        dproj_ref[:, 2 * dc:3 * dc] = dgc.astype(BF16)

        row = tile * tm + lax.broadcasted_iota(jnp.int32, (tm, 1), 0)
        for gi, w in enumerate(POOL_WINDOWS):
            cols = slice(gi * gcw, (gi + 1) * gcw)
            pooled, cnt = _pool_parts(u[:, cols], ubuf, cols, w, row, tm)
            pooled_b = pooled.astype(BF16)
            pw_b = pw_ref[gi].astype(BF16)
            dyb = dy[:, dc + gi * gcw:dc + (gi + 1) * gcw]
            q = _nn(pooled_b, pw_b)
            dps_ref[:, cols] += jnp.sum(q * dyb, axis=0, keepdims=True)
            dq = (dyb * ps_ref[:, cols]).astype(BF16)
            dpw_ref[gi] += _tn(pooled_b, dq)
            dpooled = _nt(dq, pw_b)
            ebuf[0:tm, cols] = dpooled / cnt
            du = -dpooled
            for s in range(w):
                du = du + ebuf[s:s + tm, cols]
            dproj_ref[:, 3 * dc + gi * gcw:3 * dc + (gi + 1) * gcw] = du.astype(BF16)

        dh = _nn(dproj_ref[...], win[...])
        dg_ref[...] += jnp.sum(dh * xhat, axis=0, keepdims=True)
        dxh = dh * gv
        dx_ref[...] = dxo_v + r * (dxh - xhat * jnp.mean(dxh * xhat, axis=-1, keepdims=True))
        dcbuf[tm:tm + 8, :] = dcbuf[0:8, :]
        ebuf[tm:tm + HALO, :] = ebuf[0:HALO, :]

    tok = lambda i: (n_tiles - 1 - i, 0)
    halo = lambda i: (jnp.maximum((n_tiles - 1 - i) * hb - 1, 0), 0)
    one = lambda i: (0, 0)
    return _launch(
        body, name="mixer_backward", grid=(n_tiles,),
        in_specs=[pl.BlockSpec((tm, d), tok), pl.BlockSpec((tm, d), tok), pl.BlockSpec((1, d), one),
                  pl.BlockSpec((tm, 4 * dc), tok), pl.BlockSpec((HALO, 4 * dc), halo), HBM_SPEC, HBM_SPEC,
                  pl.BlockSpec(conv_w.shape, one), pl.BlockSpec(pool_w.shape, lambda i: (0, 0, 0)), pl.BlockSpec((1, dc), one)],
        out_specs=[pl.BlockSpec((tm, d), tok), pl.BlockSpec((tm, 4 * dc), tok), pl.BlockSpec((tm, d), tok), pl.BlockSpec((tm, d), tok),
                   pl.BlockSpec((1, d), one), pl.BlockSpec(conv_w.shape, one), pl.BlockSpec((1, dc), one),
                   pl.BlockSpec(pool_w.shape, lambda i: (0, 0, 0))],
        out_shape=[jax.ShapeDtypeStruct((t, d), F32), jax.ShapeDtypeStruct((t, 4 * dc), BF16), jax.ShapeDtypeStruct((t, d), BF16),
                   jax.ShapeDtypeStruct((t, d), BF16), jax.ShapeDtypeStruct((1, d), F32), jax.ShapeDtypeStruct(conv_w.shape, F32),
                   jax.ShapeDtypeStruct((1, dc), F32), jax.ShapeDtypeStruct(pool_w.shape, F32)],
        scratch_shapes=[pltpu.VMEM((4 * dc, d), BF16), pltpu.VMEM((2 * dc, d), BF16),
                        pltpu.VMEM((tm + HALO, dc), F32), pltpu.VMEM((tm + HALO, dc), F32),
                        pltpu.VMEM((tm + 8, dc), F32), pltpu.VMEM((tm + HALO, dc), F32), pltpu.SemaphoreType.DMA((1 + N_CHIPS,))],
        args=(dxo, x, g, proj, proj, win_t, wout_x, conv_w, pool_w, pool_scale), cargo=cargo)


def _adam_update(w, gv, m, v):
    m_new = ADAM_B1 * m + (1.0 - ADAM_B1) * gv
    v_new = ADAM_B2 * v + (1.0 - ADAM_B2) * (gv * gv)
    m_hat = m_new / (1.0 - ADAM_B1 ** ADAM_STEP)
    v_hat = v_new / (1.0 - ADAM_B2 ** ADAM_STEP)
    return -ADAM_LR * (m_hat / (jnp.sqrt(v_hat) + ADAM_EPS) + ADAM_WD * w), m_new, v_new


def _adamw(w, grad, m, v, name):
    rows, cols = w.shape
    br = _row_block(rows, 256) if rows >= 8 else rows

    def body(w_ref, g_ref, m_ref, v_ref, d_ref, mo_ref, vo_ref):
        d_ref[...], mo_ref[...], vo_ref[...] = _adam_update(w_ref[...], g_ref[...], m_ref[...], v_ref[...])

    blk = pl.BlockSpec((br, cols), lambda i: (i, 0))
    return pl.pallas_call(
        body, name=name,
        out_shape=[jax.ShapeDtypeStruct((rows, cols), F32)] * 3,
        grid=(rows // br,), in_specs=[blk] * 4, out_specs=[blk] * 3,
        compiler_params=pltpu.CompilerParams(dimension_semantics=("parallel",)),
    )(w, grad, m, v)


def _adamw_transposed(w, grad_t, m, v, name):
    _, rows, cols = w.shape
    br = 256 if rows % 256 == 0 else rows

    def body(w_ref, gt_ref, m_ref, v_ref, g_ref, d_ref, mo_ref, vo_ref):
        gv = gt_ref[...].T
        g_ref[...] = gv
        d_ref[...], mo_ref[...], vo_ref[...] = _adam_update(w_ref[...], gv, m_ref[...], v_ref[...])

    blk = pl.BlockSpec((None, br, cols), lambda i: (0, i, 0))
    return pl.pallas_call(
        body, name=name,
        out_shape=[jax.ShapeDtypeStruct((1, rows, cols), F32)] * 4,
        grid=(rows // br,), in_specs=[blk, pl.BlockSpec((cols, br), lambda i: (0, i)), blk, blk], out_specs=[blk] * 4,
        compiler_params=pltpu.CompilerParams(dimension_semantics=("parallel",)),
    )(w, grad_t, m, v)


def _f32_rows_as_bf16(a, rows, cols):
    bits = lax.bitcast_convert_type(a, BF16).reshape(a.shape[0], 2 * a.shape[1])
    return jnp.pad(bits, ((0, rows - bits.shape[0]), (0, cols - bits.shape[1])))


def kernel(x, norm_ffn1, ffn1_w_gate, ffn1_w_up, ffn1_w_down, norm_mix, w_in, conv_w, pool_w, pool_scale, w_out, norm_ffn2, ffn2_w_gate, ffn2_w_up, ffn2_w_down, norm_final, loss_target, m_norm_ffn1, m_ffn1_w_gate, m_ffn1_w_up, m_ffn1_w_down, m_norm_mix, m_w_in, m_conv_w, m_pool_w, m_pool_scale, m_w_out, m_norm_ffn2, m_ffn2_w_gate, m_ffn2_w_up, m_ffn2_w_down, m_norm_final, v_norm_ffn1, v_ffn1_w_gate, v_ffn1_w_up, v_ffn1_w_down, v_norm_mix, v_w_in, v_conv_w, v_pool_w, v_pool_scale, v_w_out, v_norm_ffn2, v_ffn2_w_gate, v_ffn2_w_up, v_ffn2_w_down, v_norm_final):
    weights = dict(norm_ffn1=norm_ffn1, ffn1_w_gate=ffn1_w_gate, ffn1_w_up=ffn1_w_up, ffn1_w_down=ffn1_w_down, norm_mix=norm_mix,
                   w_in=w_in, conv_w=conv_w, pool_w=pool_w, pool_scale=pool_scale, w_out=w_out, norm_ffn2=norm_ffn2,
                   ffn2_w_gate=ffn2_w_gate, ffn2_w_up=ffn2_w_up, ffn2_w_down=ffn2_w_down, norm_final=norm_final)
    first_m = dict(norm_ffn1=m_norm_ffn1, ffn1_w_gate=m_ffn1_w_gate, ffn1_w_up=m_ffn1_w_up, ffn1_w_down=m_ffn1_w_down,
                   norm_mix=m_norm_mix, w_in=m_w_in, conv_w=m_conv_w, pool_w=m_pool_w, pool_scale=m_pool_scale, w_out=m_w_out,
                   norm_ffn2=m_norm_ffn2, ffn2_w_gate=m_ffn2_w_gate, ffn2_w_up=m_ffn2_w_up, ffn2_w_down=m_ffn2_w_down,
                   norm_final=m_norm_final)
    second_m = dict(norm_ffn1=v_norm_ffn1, ffn1_w_gate=v_ffn1_w_gate, ffn1_w_up=v_ffn1_w_up, ffn1_w_down=v_ffn1_w_down,
                    norm_mix=v_norm_mix, w_in=v_w_in, conv_w=v_conv_w, pool_w=v_pool_w, pool_scale=v_pool_scale, w_out=v_w_out,
                    norm_ffn2=v_norm_ffn2, ffn2_w_gate=v_ffn2_w_gate, ffn2_w_up=v_ffn2_w_up, ffn2_w_down=v_ffn2_w_down,
                    norm_final=v_norm_final)
    names = list(weights)

    xs = x[0]
    tgt = loss_target[0]
    t, d = xs.shape
    dc = pool_scale.shape[1]
    cx, cy, cc = _my_place()
    chip = 2 * cx + cy
    place = jnp.stack([chip, cc]).astype(jnp.int32)

    conv_rows = 32
    wout_x = jnp.concatenate([w_out[0].astype(BF16), _f32_rows_as_bf16(conv_w[0], conv_rows, d)], axis=0)
    wg2_shard = ffn2_w_gate[0].T.astype(BF16)
    half_rows = wg2_shard.shape[0] // 2

    g1, gm, g2 = norm_ffn1, norm_mix, norm_ffn2
    gf = norm_final.reshape(1, d)
    pw = pool_w[0]

    wg1, wu1 = [[w] for w in _run_cargo(_gather_cargo([ffn1_w_gate[0].T.astype(BF16), ffn1_w_up[0].T.astype(BF16)]), "gather_ffn1")]
    (a1, b1, s1), [(wd1, win_t)] = _ffn_up(xs, g1, wg1, wu1, "ffn1_up", [_gather_cargo([ffn1_w_down[0].astype(BF16), w_in[0].T.astype(BF16)])])
    wd1 = [wd1]
    (x1,), [(wout_g, wg2_a)] = _ffn_down(xs, s1, wd1, "ffn1_down", [_gather_cargo([wout_x, wg2_shard[:half_rows]])])
    wo_rows = w_out.shape[1]
    cshard = conv_w.shape[2]
    conv_bits = wout_g.reshape(N_CHIPS, wo_rows + conv_rows, d)[:, wo_rows:wo_rows + conv_w.shape[1], :2 * cshard]
    conv_full = lax.bitcast_convert_type(conv_bits.reshape(N_CHIPS, conv_w.shape[1], cshard, 2), F32)
    conv_full = jnp.transpose(conv_full, (1, 0, 2)).reshape(conv_w.shape[1], N_CHIPS * cshard)
    (x2, proj, ymix), [(wg2_b, wu2)] = _mixer_forward(
        x1, gm, win_t, wout_g, conv_full, pw, pool_scale, [_gather_cargo([wg2_shard[half_rows:], ffn2_w_up[0].T.astype(BF16)])])
    wg2, wu2 = [wg2_a, wg2_b], [wu2]
    (a2, b2, s2), [(wd2,)] = _ffn_up(x2, g2, wg2, wu2, "ffn2_up", [_gather_cargo([ffn2_w_down[0].astype(BF16)])])
    wd2 = [wd2]
    (dx3, sq_cols, dgf), _ = _ffn_down(x2, s2, wd2, "ffn2_down", loss_head=(gf, tgt))

    (dx2, da2, db2, h3, do2, dg2), _ = _ffn_backward(dx3, x2, g2, a2, b2, wg2, wu2, wd2, "ffn2_backward")
    p_wg2, _ = _weight_grad(da2, h3, "ffn2_gate_grad")
    p_wu2, [(x_wg2,)] = _weight_grad(db2, h3, "ffn2_up_grad", [_exchange_cargo([p_wg2])])
    p_wd2, [(x_wu2,)] = _weight_grad(s2, do2, "ffn2_down_grad", [_exchange_cargo([p_wu2])])

    (dx1, dproj, h2, dx2b, dgm, dcw, dps, dpw), [(x_wd2,)] = _mixer_backward(
        dx2, x1, gm, proj, win_t, wout_g, conv_full, pw, pool_scale, [_exchange_cargo([p_wd2])])

    (dx0, da1, db1, h1, do1, dg1), _ = _ffn_backward(dx1, xs, g1, a1, b1, wg1, wu1, wd1, "ffn1_backward")

    npw = pw.size // d
    head = [dg1, dgm, dg2, dgf, jnp.pad(dps, ((0, 0), (0, d - dc))), jnp.pad(dcw, ((0, 0), (0, d - dc))), sq_cols]
    n_head = sum(h.shape[0] for h in head)
    base = -(-n_head // 8) * 8
    pack = jnp.concatenate(head + [jnp.zeros((base - n_head, d), F32), dpw.reshape(npw, d)], axis=0)

    p_wg1, [(packs,)] = _weight_grad(da1, h1, "ffn1_gate_grad", [_all_gather_small_cargo(pack)])
    p_wu1, [(x_wg1,)] = _weight_grad(db1, h1, "ffn1_up_grad", [_exchange_cargo([p_wg1])])
    p_wd1, [(x_wu1,)] = _weight_grad(s1, do1, "ffn1_down_grad", [_exchange_cargo([p_wu1])])
    p_win, [(x_wd1,)] = _weight_grad(dproj, h2, "w_in_grad", [_exchange_cargo([p_wd1])])
    p_wout, [(x_win,)] = _weight_grad(ymix, dx2b, "w_out_grad", [_exchange_cargo([p_win])])
    x_wout, = _run_cargo(_exchange_cargo([p_wout]), "grad_exchange_last")
    small = _sum_by_device(packs)
    loss = jnp.sum(small[n_head - 1]) * (0.5 / d)

    order = ["wg1", "wu1", "wd1", "win", "wout", "wg2", "wu2", "wd2"]
    pairs = dict(wg1=p_wg1, wu1=p_wu1, wd1=p_wd1, win=p_win, wout=p_wout, wg2=p_wg2, wu2=p_wu2, wd2=p_wd2)
    landed = dict(wg1=x_wg1, wu1=x_wu1, wd1=x_wd1, win=x_win, wout=x_wout, wg2=x_wg2, wu2=x_wu2, wd2=x_wd2)
    both = _sibling_share([_chip_sum(pairs[k], landed[k], place, k) for k in order])
    rwg1, rwu1, rwd1, rwin, rwout, rwg2, rwu2, rwd2 = [b.reshape(2 * b.shape[1], b.shape[2]) for b in both]

    grads = {
        "norm_ffn1": small[0:1], "norm_mix": small[1:2], "norm_ffn2": small[2:3], "norm_final": small[3],
        "pool_scale": small[4:5, :dc],
        "conv_w": lax.dynamic_slice_in_dim(small[5:5 + dcw.shape[0], :dc], chip * cshard, cshard, axis=1)[None],
        "pool_w": small[base:].reshape(pool_w.shape),
        "ffn1_w_down": rwd1[None], "w_out": rwout[None], "ffn2_w_down": rwd2[None],
    }
    by_view = {"ffn1_w_gate": rwg1, "ffn1_w_up": rwu1, "ffn2_w_gate": rwg2, "ffn2_w_up": rwu2}

    deltas, new_m, new_v = {}, {}, {}
    for n in names:
        w = weights[n]
        shape = w.shape
        if n == "w_in":
            grads[n], deltas[n], new_m[n], new_v[n] = _adamw_transposed(w, rwin, first_m[n], second_m[n], "adamw_" + n)
            continue
        if n in by_view:
            view = lambda a: jnp.swapaxes(a, 1, 2)[0]
            back = lambda a: jnp.swapaxes(a[None], 1, 2)
            dl, mo, vo = _adamw(view(w), by_view[n], view(first_m[n]), view(second_m[n]), "adamw_" + n)
            grads[n], deltas[n], new_m[n], new_v[n] = back(by_view[n]), back(dl), back(mo), back(vo)
            continue
        as2d = (lambda a: a.reshape(-1, shape[-1]))
        dl, mo, vo = _adamw(as2d(w), as2d(grads[n]), as2d(first_m[n]), as2d(second_m[n]), "adamw_" + n)
        deltas[n], new_m[n], new_v[n] = dl.reshape(shape), mo.reshape(shape), vo.reshape(shape)
        grads[n] = grads[n].reshape(shape)

    return (loss, dx0[None], *[grads[n] for n in names], *[deltas[n] for n in names],
            *[new_m[n] for n in names], *[new_v[n] for n in names])
```

```python
import jax
import jax.numpy as jnp
from jax import lax
from jax.experimental import pallas as pl
from jax.experimental.pallas import tpu as pltpu

F32 = jnp.float32
BF16 = jnp.bfloat16
MESH = pl.DeviceIdType.MESH

EPS = 1e-6
POOL_WINDOWS = (2, 4, 8, 16)
ADAM_LR = 0.001
ADAM_B1 = 0.9
ADAM_B2 = 0.999
ADAM_EPS = 1e-08
ADAM_WD = 0.01
ADAM_STEP = 10

N_CHIPS = 4
N_DEVICES = 8
MXU_COLS_V7X = 256
VMEM_LIMIT = 56 * 1024 * 1024
TM_FFN = 512
TM_MIX = 512
TM_TN = 1024
HALO = 16
FFN_FWD_CHUNKS = 2
FFN_BWD_CHUNKS = 2


def _nt(a, b):
    return lax.dot_general(a, b, (((1,), (1,)), ((), ())), preferred_element_type=F32)


def _tn(a, b):
    return lax.dot_general(a, b, (((0,), (0,)), ((), ())), preferred_element_type=F32)


def _nn(a, b):
    return jnp.dot(a, b, preferred_element_type=F32)


def _sigmoid(a):
    return 1.0 / (1.0 + jnp.exp(-a))


def _feature_chunks(n, parts):
    assert n % MXU_COLS_V7X == 0
    tiles = n // MXU_COLS_V7X
    out, s0 = [], 0
    for p in range(parts):
        sz = (tiles // parts + (1 if p < tiles % parts else 0)) * MXU_COLS_V7X
        if sz:
            out.append((s0, sz))
            s0 += sz
    return out


def _row_block(rows, cap):
    best = 8
    for b in range(8, min(rows, cap) + 1, 8):
        if rows % b == 0:
            best = b
    assert rows % best == 0
    return best


def _my_place():
    return lax.axis_index("x"), lax.axis_index("y"), lax.axis_index("c")


def _other_chips(x, y):
    return [(1 - x, y), (x, 1 - y), (1 - x, 1 - y)]


HBM_SPEC = pl.BlockSpec(memory_space=pltpu.HBM)


class _Cargo:
    def __init__(self, operands, out_shapes, n_sems, start, finish, relay=None):
        self.operands, self.out_shapes, self.n_sems = list(operands), list(out_shapes), n_sems
        self.start, self.finish, self.relay = start, finish, relay


def _launch(body, *, name, grid, in_specs, out_specs, out_shape, scratch_shapes, args, cargo=()):
    params = pltpu.CompilerParams(dimension_semantics=("arbitrary",) * len(grid), vmem_limit_bytes=VMEM_LIMIT)
    cargos = list(cargo)
    c_operands = [op for cg in cargos for op in cg.operands]
    c_shapes = [sh for cg in cargos for sh in cg.out_shapes]
    counts = [len(in_specs), len(c_operands), len(out_shape), len(c_shapes), len(scratch_shapes), 2 * len(cargos)]

    def carrying(*refs):
        groups, pos = [], 0
        for k in counts:
            groups.append(refs[pos:pos + k])
            pos += k
        ins, c_ins, outs, c_outs, scratch, sems = groups
        parts, pi, po = [], 0, 0
        for n, cg in enumerate(cargos):
            parts.append((c_ins[pi:pi + len(cg.operands)], c_outs[po:po + len(cg.out_shapes)], sems[2 * n], sems[2 * n + 1]))
            pi += len(cg.operands)
            po += len(cg.out_shapes)
        step, steps = 0, 1
        for ax, g in enumerate(grid):
            step = step * g + pl.program_id(ax)
            steps *= g
        relayed = [cg for cg in cargos if cg.relay is not None]

        if cargos:
            @pl.when(step == 0)
            def _():
                for cg, part in zip(cargos, parts):
                    cg.start(*part)

        if relayed and steps >= 3:
            @pl.when(step == steps - 2)
            def _():
                for cg, part in zip(cargos, parts):
                    if cg.relay is not None:
                        cg.relay(*part)

        body(*ins, *outs, *scratch)

        if cargos:
            @pl.when(step == steps - 1)
            def _():
                for cg, part in zip(cargos, parts):
                    if cg.relay is not None and steps < 3:
                        cg.relay(*part)
                    cg.finish(*part)

    sems = [pltpu.SemaphoreType.DMA((cg.n_sems,)) for cg in cargos for _ in range(2)]
    outs = pl.pallas_call(
        carrying, name=name, grid=grid,
        in_specs=list(in_specs) + [HBM_SPEC] * counts[1], out_specs=list(out_specs) + [HBM_SPEC] * counts[3],
        out_shape=list(out_shape) + c_shapes, scratch_shapes=list(scratch_shapes) + sems,
        compiler_params=params)(*args, *c_operands)
    own, rest = list(outs[:counts[2]]), list(outs[counts[2]:])
    carried, po = [], 0
    for cg in cargos:
        carried.append(rest[po:po + len(cg.out_shapes)])
        po += len(cg.out_shapes)
    return own, carried


def _run_cargo(cargo, name):
    n_in, n_out = len(cargo.operands), len(cargo.out_shapes)

    def body(*refs):
        c_ins, c_outs, sems = refs[:n_in], refs[n_in:n_in + n_out], refs[n_in + n_out:]
        cargo.start(c_ins, c_outs, *sems)
        if cargo.relay is not None:
            cargo.relay(c_ins, c_outs, *sems)
        cargo.finish(c_ins, c_outs, *sems)

    sem = pltpu.SemaphoreType.DMA((cargo.n_sems,))
    return list(pl.pallas_call(body, name=name, out_shape=cargo.out_shapes, in_specs=[HBM_SPEC] * n_in,
                               out_specs=[HBM_SPEC] * n_out, scratch_shapes=[sem, sem])(*cargo.operands))


def _gather_cargo(shards):
    n = len(shards)
    for s in shards:
        assert s.shape[0] % 32 == 0

    def steps(ins, outs, send_sems, recv_sems):
        x, y, c = _my_place()
        sibling = (x, y, 1 - c)
        chips = _other_chips(x, y)
        mine = 2 * x + y

        def rows_of(a, chip_index, half):
            rps = shards[a].shape[0]
            hr = rps // 2
            return outs[a].at[pl.ds(pl.multiple_of(chip_index * rps + half * hr, 16), hr), :]

        def remote(a, slot, src, dst, to):
            return pltpu.make_async_remote_copy(
                src_ref=src, dst_ref=dst, send_sem=send_sems.at[a * 7 + slot], recv_sem=recv_sems.at[a * 7 + slot],
                device_id=to, device_id_type=MESH)

        def own_copy(a):
            rps = shards[a].shape[0]
            return remote(a, 6, ins[a], outs[a].at[pl.ds(pl.multiple_of(mine * rps, 16), rps), :], sibling)

        def my_half(a):
            hr = shards[a].shape[0] // 2
            return ins[a].at[pl.ds(pl.multiple_of(c * hr, 16), hr), :]

        def start():
            for a in range(n):
                own_copy(a).start()
                for j, chip in enumerate(chips):
                    remote(a, j, my_half(a), rows_of(a, mine, c), (*chip, c)).start()

        def relay():
            for a in range(n):
                for j, chip in enumerate(chips):
                    landed = rows_of(a, 2 * chip[0] + chip[1], c)
                    remote(a, j, landed, landed, (*chip, c)).wait_recv()
                    remote(a, 3 + j, landed, landed, sibling).start()

        def finish():
            for a in range(n):
                for j, chip in enumerate(chips):
                    from_sibling = rows_of(a, 2 * chip[0] + chip[1], 1 - c)
                    remote(a, 3 + j, from_sibling, from_sibling, sibling).wait_recv()
            for a in range(n):
                for j, chip in enumerate(chips):
                    remote(a, j, my_half(a), rows_of(a, mine, c), (*chip, c)).wait_send()
                    landed = rows_of(a, 2 * chip[0] + chip[1], c)
                    remote(a, 3 + j, landed, landed, sibling).wait_send()
                own_copy(a).wait()

        return start, relay, finish

    return _Cargo(shards, [jax.ShapeDtypeStruct((N_CHIPS * s.shape[0], s.shape[1]), s.dtype) for s in shards], 7 * n,
                  lambda *r: steps(*r)[0](), lambda *r: steps(*r)[2](), relay=lambda *r: steps(*r)[1]())


def _exchange_cargo(pairs):
    n = len(pairs)

    def copies(ins, outs, send_sems, recv_sems):
        x, y, c = _my_place()
        return [pltpu.make_async_remote_copy(
            src_ref=ins[a].at[2 * chip[0] + chip[1]], dst_ref=outs[a].at[j],
            send_sem=send_sems.at[3 * a + j], recv_sem=recv_sems.at[3 * a + j], device_id=(*chip, c), device_id_type=MESH)
            for a in range(n) for j, chip in enumerate(_other_chips(x, y))]

    def start(*r):
        for cp in copies(*r):
            cp.start()

    def finish(*r):
        for cp in copies(*r):
            cp.wait()

    return _Cargo(pairs, [jax.ShapeDtypeStruct((3,) + p.shape[1:], p.dtype) for p in pairs], 3 * n, start, finish)


def _all_gather_small_cargo(pack):
    rows, cols = pack.shape

    def copies(ins, outs, send_sems, recv_sems):
        x, y, c = _my_place()
        me = 4 * x + 2 * y + c
        remote = []
        for f in range(1, N_DEVICES):
            fx, fy, fc = (f >> 2) & 1, (f >> 1) & 1, f & 1
            to = (1 - x if fx else x, 1 - y if fy else y, 1 - c if fc else c)
            remote.append(pltpu.make_async_remote_copy(
                src_ref=ins[0], dst_ref=outs[0].at[me], send_sem=send_sems.at[f - 1], recv_sem=recv_sems.at[f - 1],
                device_id=to, device_id_type=MESH))
        own = pltpu.make_async_copy(ins[0], outs[0].at[me], send_sems.at[N_DEVICES - 1])
        return remote, own

    def start(*r):
        remote, own = copies(*r)
        own.start()
        for cp in remote:
            cp.start()

    def finish(*r):
        remote, own = copies(*r)
        for cp in remote:
            cp.wait()
        own.wait()

    return _Cargo([pack], [jax.ShapeDtypeStruct((N_DEVICES, rows, cols), F32)], N_DEVICES, start, finish)


def _sum_by_device(packs):
    n, rows, cols = packs.shape

    def body(p_ref, o_ref):
        acc = p_ref[0]
        for dev in range(1, n):
            acc = acc + p_ref[dev]
        o_ref[...] = acc

    return pl.pallas_call(body, name="small_grads_sum", out_shape=jax.ShapeDtypeStruct((rows, cols), F32))(packs)


def _chip_sum(pair, got, place, tag):
    _, hr, cols = pair.shape
    br = _row_block(hr, 256)

    def body(k_ref, p_ref, r_ref, o_ref):
        acc = p_ref[...].astype(F32)
        for j in range(3):
            acc = acc + r_ref[j].astype(F32)
        o_ref[...] = acc

    return pl.pallas_call(
        body, name="grad_chip_sum_" + tag,
        out_shape=jax.ShapeDtypeStruct((2, hr, cols), F32),
        grid_spec=pltpu.PrefetchScalarGridSpec(
            num_scalar_prefetch=1, grid=(hr // br,),
            in_specs=[pl.BlockSpec((None, br, cols), lambda r, k_ref: (k_ref[0], r, 0)),
                      pl.BlockSpec((3, br, cols), lambda r, k_ref: (0, r, 0))],
            out_specs=pl.BlockSpec((None, br, cols), lambda r, k_ref: (k_ref[1], r, 0))),
        compiler_params=pltpu.CompilerParams(dimension_semantics=("parallel",)),
    )(place, pair, got)


def _sibling_share(halves):
    n = len(halves)

    def body(*refs):
        outs = refs[n:2 * n]
        send_sems, recv_sems = refs[2 * n:]
        x, y, c = _my_place()
        copies = []
        for a in range(n):
            cp = pltpu.make_async_remote_copy(
                src_ref=outs[a].at[c], dst_ref=outs[a].at[c], send_sem=send_sems.at[a], recv_sem=recv_sems.at[a],
                device_id=(x, y, 1 - c), device_id_type=MESH)
            cp.start()
            copies.append(cp)
        for cp in copies:
            cp.wait()

    return pl.pallas_call(
        body, name="grad_share_sibling",
        out_shape=[jax.ShapeDtypeStruct(h.shape, h.dtype) for h in halves],
        in_specs=[HBM_SPEC] * n, out_specs=[HBM_SPEC] * n,
        input_output_aliases={a: a for a in range(n)},
        scratch_shapes=[pltpu.SemaphoreType.DMA((n,)), pltpu.SemaphoreType.DMA((n,))],
    )(*halves)


def _load_rows(pairs, sems):
    cps = [pltpu.make_async_copy(src, dst, sems.at[j]) for j, (src, dst) in enumerate(pairs)]
    for cp in cps:
        cp.start()
    for cp in cps:
        cp.wait()


def _piece_rows(weights):
    flat = [p for pieces in weights for p in pieces]

    def copies(refs, mats):
        out, n = [], 0
        for pieces, mat in zip(weights, mats):
            rps = sum(p.shape[0] for p in pieces) // N_CHIPS
            off = 0
            for p in pieces:
                r = p.shape[0] // N_CHIPS
                if len(pieces) == 1:
                    out.append((refs[n], mat))
                else:
                    for k in range(N_CHIPS):
                        out.append((refs[n].at[pl.ds(k * r, r), :], mat.at[pl.ds(k * rps + off, r), :]))
                off += r
                n += 1
        return out

    n_copies = sum(1 if len(pieces) == 1 else N_CHIPS * len(pieces) for pieces in weights)
    return flat, copies, n_copies


def _loss_head(xv, gv, tv):
    d = xv.shape[-1]
    r = lax.rsqrt(jnp.mean(xv * xv, axis=-1, keepdims=True) + EPS)
    xhat = xv * r
    err = xhat * gv - tv
    dy = err * (1.0 / d)
    dxh = dy * gv
    dx = r * (dxh - xhat * jnp.mean(dxh * xhat, axis=-1, keepdims=True))
    return dx, jnp.sum(err * err, axis=0, keepdims=True), jnp.sum(dy * xhat, axis=0, keepdims=True)


def _ffn_up(x, g, wg_t, wu_t, name, cargo=()):
    t, d = x.shape
    f = sum(p.shape[0] for p in wg_t)
    tm = min(TM_FFN, t)
    chunks = _feature_chunks(f, FFN_FWD_CHUNKS)
    flat, copies, n_copies = _piece_rows([wg_t, wu_t])
    nw = len(flat)

    def body(x_ref, g_ref, *rest):
        w_hbm, (a_ref, b_ref, s_ref, wg, wu, sems) = rest[:nw], rest[nw:]

        @pl.when(pl.program_id(0) == 0)
        def _():
            _load_rows(copies(w_hbm, [wg, wu]), sems)

        xv = x_ref[...]
        r = lax.rsqrt(jnp.mean(xv * xv, axis=-1, keepdims=True) + EPS)
        h = (xv * r * g_ref[...]).astype(BF16)
        for s0, sz in chunks:
            a = _nt(h, wg[s0:s0 + sz, :])
            b = _nt(h, wu[s0:s0 + sz, :])
            a_ref[:, s0:s0 + sz] = a.astype(BF16)
            b_ref[:, s0:s0 + sz] = b.astype(BF16)
            s_ref[:, s0:s0 + sz] = (a * _sigmoid(a) * b).astype(BF16)

    tok = lambda i: (i, 0)
    wide = pl.BlockSpec((tm, f), tok)
    return _launch(
        body, name=name, grid=(t // tm,),
        in_specs=[pl.BlockSpec((tm, d), tok), pl.BlockSpec((1, d), lambda i: (0, 0))] + [HBM_SPEC] * nw,
        out_specs=[wide, wide, wide], out_shape=[jax.ShapeDtypeStruct((t, f), BF16)] * 3,
        scratch_shapes=[pltpu.VMEM((f, d), BF16), pltpu.VMEM((f, d), BF16), pltpu.SemaphoreType.DMA((n_copies,))],
        args=(x, g, *flat), cargo=cargo)


def _ffn_down(x, s, wd, name, cargo=(), loss_head=None):
    t, d = x.shape
    f = s.shape[1]
    tm = min(TM_FFN, t)
    flat, copies, n_copies = _piece_rows([wd])
    nw = len(flat)
    nl = 2 if loss_head else 0

    def body(x_ref, s_ref, *rest):
        head, w_hbm = rest[:nl], rest[nl:nl + nw]
        xo_ref = rest[nl + nw]
        sums, (wdn, sems) = rest[nl + nw + 1:nl + nw + 1 + nl], rest[nl + nw + 1 + nl:]

        @pl.when(pl.program_id(0) == 0)
        def _():
            _load_rows(copies(w_hbm, [wdn]), sems)
            for sum_ref in sums:
                sum_ref[...] = jnp.zeros_like(sum_ref)

        xo = x_ref[...] + 0.5 * _nn(s_ref[...], wdn[...])
        if loss_head:
            dx, sq, dgf = _loss_head(xo, head[0][...], head[1][...])
            xo_ref[...] = dx
            sums[0][...] += sq
            sums[1][...] += dgf
        else:
            xo_ref[...] = xo

    tok = lambda i: (i, 0)
    one = lambda i: (0, 0)
    return _launch(
        body, name=name, grid=(t // tm,),
        in_specs=[pl.BlockSpec((tm, d), tok), pl.BlockSpec((tm, f), tok)]
        + ([pl.BlockSpec((1, d), one), pl.BlockSpec((tm, d), tok)] if loss_head else []) + [HBM_SPEC] * nw,
        out_specs=[pl.BlockSpec((tm, d), tok)] + [pl.BlockSpec((1, d), one)] * nl,
        out_shape=[jax.ShapeDtypeStruct((t, d), F32)] + [jax.ShapeDtypeStruct((1, d), F32)] * nl,
        scratch_shapes=[pltpu.VMEM((f, d), BF16), pltpu.SemaphoreType.DMA((n_copies,))],
        args=(x, s, *(loss_head or ()), *flat), cargo=cargo)


def _ffn_backward(dxo, x, g, a, b, wg_t, wu_t, wd, name, cargo=()):
    t, d = x.shape
    f = sum(p.shape[0] for p in wd)
    tm = min(TM_FFN // 2, t)
    chunks = _feature_chunks(f, FFN_BWD_CHUNKS)
    flat, copies, n_copies = _piece_rows([wg_t, wu_t, wd])
    nw = len(flat)

    def body(dxo_ref, x_ref, g_ref, a_ref, b_ref, *rest):
        w_hbm, (dx_ref, da_ref, db_ref, h_ref, do_ref, dg_ref, wg, wu, wdn, sems) = rest[:nw], rest[nw:]

        @pl.when(pl.program_id(0) == 0)
        def _():
            _load_rows(copies(w_hbm, [wg, wu, wdn]), sems)
            dg_ref[...] = jnp.zeros_like(dg_ref)

        xv = x_ref[...]
        gv = g_ref[...]
        r = lax.rsqrt(jnp.mean(xv * xv, axis=-1, keepdims=True) + EPS)
        xhat = xv * r
        h_ref[...] = (xhat * gv).astype(BF16)
        dxo_v = dxo_ref[...]
        dout = (0.5 * dxo_v).astype(BF16)
        do_ref[...] = dout
        dh = jnp.zeros((tm, d), F32)
        for s0, sz in chunks:
            ds = _nt(dout, wdn[s0:s0 + sz, :])
            av = a_ref[:, s0:s0 + sz].astype(F32)
            bv = b_ref[:, s0:s0 + sz].astype(F32)
            sig = _sigmoid(av)
            silu = av * sig
            da =(ds * bv * (sig * (1.0 + av * (1.0 - sig)))).astype(BF16)
            db = (ds * silu).astype(BF16)
            da_ref[:, s0:s0 + sz] = da
            db_ref[:, s0:s0 + sz] = db
            dh = dh + _nn(da, wg[s0:s0 + sz, :]) + _nn(db, wu[s0:s0 + sz, :])
        dg_ref[...] += jnp.sum(dh * xhat, axis=0, keepdims=True)
        dxh = dh * gv
        dx_ref[...] = dxo_v + r * (dxh - xhat * jnp.mean(dxh * xhat, axis=-1, keepdims=True))

    tok = lambda i: (i, 0)
    one = lambda i: (0, 0)
    return _launch(
        body, name=name, grid=(t // tm,),
        in_specs=[pl.BlockSpec((tm, d), tok), pl.BlockSpec((tm, d), tok), pl.BlockSpec((1, d), one),
                  pl.BlockSpec((tm, f), tok), pl.BlockSpec((tm, f), tok)] + [HBM_SPEC] * nw,
        out_specs=[pl.BlockSpec((tm, d), tok), pl.BlockSpec((tm, f), tok), pl.BlockSpec((tm, f), tok),
                   pl.BlockSpec((tm, d), tok), pl.BlockSpec((tm, d), tok), pl.BlockSpec((1, d), one)],
        out_shape=[jax.ShapeDtypeStruct((t, d), F32), jax.ShapeDtypeStruct((t, f), BF16), jax.ShapeDtypeStruct((t, f), BF16),
                   jax.ShapeDtypeStruct((t, d), BF16), jax.ShapeDtypeStruct((t, d), BF16), jax.ShapeDtypeStruct((1, d), F32)],
        scratch_shapes=[pltpu.VMEM((f, d), BF16), pltpu.VMEM((f, d), BF16), pltpu.VMEM((f, d), BF16), pltpu.SemaphoreType.DMA((n_copies,))],
        args=(dxo, x, g, a, b, *flat), cargo=cargo)


def _weight_grad(lhs, rhs, name, cargo=()):
    t, m = lhs.shape
    d = rhs.shape[1]
    tm = min(TM_TN, t)
    nt = t // tm
    nj = 1
    bm = m // nj
    cpb = N_CHIPS // nj
    rps = m // N_CHIPS
    hr = rps // 2
    assert hr % 16 == 0

    def body(l_ref, r_ref, o_ref, acc, stage, recv, send_sems, recv_sems):
        j = pl.program_id(0)
        i = pl.program_id(1)
        @pl.when(i == 0)
        def _():
            acc[...] = jnp.zeros_like(acc)

        acc[...] += _tn(l_ref[...], r_ref[...])

        def pair_sum(jj):
            x, y, c = _my_place()
            copies = []
            for q in range(cpb):
                slot = jj * cpb + q
                stage[slot] = acc[pl.ds(pl.multiple_of(q * rps + (1 - c) * hr, 16), hr), :].astype(BF16)
                cp = pltpu.make_async_remote_copy(
                    src_ref=stage.at[slot], dst_ref=recv.at[slot], send_sem=send_sems.at[slot], recv_sem=recv_sems.at[slot],
                    device_id=(x, y, 1 - c), device_id_type=MESH)
                cp.start()
                copies.append(cp)
            for q, cp in enumerate(copies):
                cp.wait_recv()
                mine = acc[pl.ds(pl.multiple_of(q * rps + c * hr, 16), hr), :]
                o_ref[q] = (mine + recv[jj * cpb + q].astype(F32)).astype(BF16)
            for cp in copies:
                cp.wait_send()

        for jj in range(nj):
            @pl.when(jnp.logical_and(i == nt - 1, j == jj))
            def _():
                pair_sum(jj)

    outs, carried = _launch(
        body, name=name, grid=(nj, nt),
        in_specs=[pl.BlockSpec((tm, bm), lambda j, i: (i, j)), pl.BlockSpec((tm, d), lambda j, i: (i, 0))],
        out_specs=[pl.BlockSpec((cpb, hr, d), lambda j, i: (j, 0, 0))],
        out_shape=[jax.ShapeDtypeStruct((N_CHIPS, hr, d), BF16)],
        scratch_shapes=[pltpu.VMEM((bm, d), F32), pltpu.VMEM((N_CHIPS, hr, d), BF16), pltpu.VMEM((N_CHIPS, hr, d), BF16),
                        pltpu.SemaphoreType.DMA((N_CHIPS,)), pltpu.SemaphoreType.DMA((N_CHIPS,))],
        args=(lhs, rhs), cargo=cargo)
    return outs[0], carried


def _pool_parts(u_cols, ubuf, cols, w, row, tm):
    ws = u_cols
    for s in range(1, w):
        ws = ws + ubuf[HALO - s:HALO - s + tm, cols]
    cnt = jnp.minimum(row + 1, w).astype(F32)
    return ws / cnt - u_cols, cnt


def _mixer_forward(x, g, win_t, wout_x, conv_w, pool_w, pool_scale, cargo=()):
    t, d = x.shape
    dc = win_t.shape[0] // 4
    gcw = dc // len(POOL_WINDOWS)
    wo_rows = d // N_CHIPS
    wo_stride = wout_x.shape[0] // N_CHIPS
    tm = min(TM_MIX, t)

    def body(x_ref, g_ref, win_hbm, wout_hbm, cw_ref, pw_ref, ps_ref, xo_ref, proj_ref, y_ref,
             win, wout, zbuf, ubuf, sems):
        i = pl.program_id(0)

        @pl.when(i == 0)
        def _():
            pairs = [(win_hbm, win)]
            for k in range(N_CHIPS):
                pairs.append((wout_hbm.at[pl.ds(k * wo_stride, wo_rows), :], wout.at[pl.ds(k * wo_rows, wo_rows), :]))
            _load_rows(pairs, sems)
            zbuf[0:8, :] = jnp.zeros((8, dc), F32)
            ubuf[0:HALO, :] = jnp.zeros((HALO, dc), F32)

        xv = x_ref[...]
        r = lax.rsqrt(jnp.mean(xv * xv, axis=-1, keepdims=True) + EPS)
        h = (xv * r * g_ref[...]).astype(BF16)
        v = _nt(h, win[0:dc, :])
        gb = _nt(h, win[dc:2 * dc, :])
        gc = _nt(h, win[2 * dc:3 * dc, :])
        u = _nt(h, win[3 * dc:4 * dc, :])
        proj_ref[:, 0:dc] = v.astype(BF16)
        proj_ref[:, dc:2 * dc] = gb.astype(BF16)
        proj_ref[:, 2 * dc:3 * dc] = gc.astype(BF16)
        proj_ref[:, 3 * dc:4 * dc] = u.astype(BF16)

        z = gc * v
        zbuf[8:8 + tm, :] = z
        cw = cw_ref[...]
        conv = cw[2:3, :] * z + cw[1:2, :] * zbuf[7:7 + tm, :] + cw[0:1, :] * zbuf[6:6 + tm, :]
        y_ref[:, 0:dc] = (gb * conv).astype(BF16)

        ubuf[HALO:HALO + tm, :] = u
        row = i * tm + lax.broadcasted_iota(jnp.int32, (tm, 1), 0)
        for gi, w in enumerate(POOL_WINDOWS):
            cols = slice(gi * gcw, (gi + 1) * gcw)
            pooled, _ = _pool_parts(u[:, cols], ubuf, cols, w, row, tm)
            yb = _nn(pooled.astype(BF16), pw_ref[gi].astype(BF16)) * ps_ref[:, cols]
            y_ref[:, dc + gi * gcw:dc + (gi + 1) * gcw] = yb.astype(BF16)

        xo_ref[...] = xv + _nn(y_ref[...], wout[...])
        zbuf[0:8, :] = zbuf[tm:tm + 8, :]
        ubuf[0:HALO, :] = ubuf[tm:tm + HALO, :]

    tok = lambda i: (i, 0)
    one = lambda i: (0, 0)
    return _launch(
        body, name="mixer_forward", grid=(t // tm,),
        in_specs=[pl.BlockSpec((tm, d), tok), pl.BlockSpec((1, d), one), HBM_SPEC, HBM_SPEC,
                  pl.BlockSpec(conv_w.shape, one), pl.BlockSpec(pool_w.shape, lambda i: (0, 0, 0)), pl.BlockSpec((1, dc), one)],
        out_specs=[pl.BlockSpec((tm, d), tok), pl.BlockSpec((tm, 4 * dc), tok), pl.BlockSpec((tm, 2 * dc), tok)],
        out_shape=[jax.ShapeDtypeStruct((t, d), F32), jax.ShapeDtypeStruct((t, 4 * dc), BF16), jax.ShapeDtypeStruct((t, 2 * dc), BF16)],
        scratch_shapes=[pltpu.VMEM((4 * dc, d), BF16), pltpu.VMEM((2 * dc, d), BF16),
                        pltpu.VMEM((tm + 8, dc), F32), pltpu.VMEM((tm + HALO, dc), F32), pltpu.SemaphoreType.DMA((1 + N_CHIPS,))],
        args=(x, g, win_t, wout_x, conv_w, pool_w, pool_scale), cargo=cargo)


def _mixer_backward(dxo, x, g, proj, win_t, wout_x, conv_w, pool_w, pool_scale, cargo=()):
    t, d = x.shape
    dc = win_t.shape[0] // 4
    ng = len(POOL_WINDOWS)
    gcw = dc // ng
    wo_rows = d // N_CHIPS
    wo_stride = wout_x.shape[0] // N_CHIPS
    tm = min(TM_MIX // 2, t)
    n_tiles = t // tm
    hb = tm // HALO

    def body(dxo_ref, x_ref, g_ref, proj_ref, halo_ref, win_hbm, wout_hbm, cw_ref, pw_ref, ps_ref,
             dx_ref, dproj_ref, h_ref, dxob_ref, dg_ref, dcw_ref, dps_ref, dpw_ref,
             win, wout, zbuf, ubuf, dcbuf, ebuf, sems):
        i = pl.program_id(0)
        tile = n_tiles - 1 - i

        @pl.when(i == 0)
        def _():
            pairs = [(win_hbm, win)]
            for k in range(N_CHIPS):
                pairs.append((wout_hbm.at[pl.ds(k * wo_stride, wo_rows), :], wout.at[pl.ds(k * wo_rows, wo_rows), :]))
            _load_rows(pairs, sems)
            dcbuf[tm:tm + 8, :] = jnp.zeros((8, dc), F32)
            ebuf[tm:tm + HALO, :] = jnp.zeros((HALO, dc), F32)
            dg_ref[...] = jnp.zeros_like(dg_ref)
            dcw_ref[...] = jnp.zeros_like(dcw_ref)
            dps_ref[...] = jnp.zeros_like(dps_ref)
            dpw_ref[...] = jnp.zeros_like(dpw_ref)

        xv = x_ref[...]
        gv = g_ref[...]
        r = lax.rsqrt(jnp.mean(xv * xv, axis=-1, keepdims=True) + EPS)
        xhat = xv * r
        h_ref[...] = (xhat * gv).astype(BF16)
        dxo_v = dxo_ref[...]
        dxo_b = dxo_v.astype(BF16)
        dxob_ref[...] = dxo_b

        v = proj_ref[:, 0:dc].astype(F32)
        gb = proj_ref[:, dc:2 * dc].astype(F32)
        gc = proj_ref[:, 2 * dc:3 * dc].astype(F32)
        u = proj_ref[:, 3 * dc:4 * dc].astype(F32)
        first = jnp.where(tile > 0, 1.0, 0.0)
        zbuf[0:HALO, :] = halo_ref[:, 2 * dc:3 * dc].astype(F32) * halo_ref[:, 0:dc].astype(F32) * first
        ubuf[0:HALO, :] = halo_ref[:, 3 * dc:4 * dc].astype(F32) * first
        z = gc * v
        zbuf[HALO:HALO + tm, :] = z
        ubuf[HALO:HALO + tm, :] = u
        z1 = zbuf[HALO - 1:HALO - 1 + tm, :]
        z2 = zbuf[HALO - 2:HALO - 2 + tm, :]
        cw = cw_ref[...]
        conv = cw[2:3, :] * z + cw[1:2, :] * z1 + cw[0:1, :] * z2

        dy = _nt(dxo_b, wout[...])
        dya = dy[:, 0:dc]
        dgb = dya * conv
        dconv = dya * gb
        dcbuf[0:tm, :] = dconv
        dz = cw[2:3, :] * dconv + cw[1:2, :] * dcbuf[1:1 + tm, :] + cw[0:1, :] * dcbuf[2:2 + tm, :]
        dgc = dz * v
        dv = dz * gc
        dcw_ref[0:1, :] += jnp.sum(dconv * z2, axis=0, keepdims=True)
        dcw_ref[1:2, :] += jnp.sum(dconv * z1, axis=0, keepdims=True)
        dcw_ref[2:3, :] += jnp.sum(dconv * z, axis=0, keepdims=True)

        dproj_ref[:, 0:dc] = dv.astype(BF16)
        dproj_ref[:, dc:2 * dc] = dgb.astype(BF16)
        dproj_ref[:, 2 * dc:3 * dc] = dgc.astype(BF16)

        row = tile * tm + lax.broadcasted_iota(jnp.int32, (tm, 1), 0)
        for gi, w in enumerate(POOL_WINDOWS):
            cols = slice(gi * gcw, (gi + 1) * gcw)
            pooled, cnt = _pool_parts(u[:, cols], ubuf, cols, w, row, tm)
            pooled_b = pooled.astype(BF16)
            pw_b = pw_ref[gi].astype(BF16)
            dyb = dy[:, dc + gi * gcw:dc + (gi + 1) * gcw]
            q = _nn(pooled_b, pw_b)
            dps_ref[:, cols] += jnp.sum(q * dyb, axis=0, keepdims=True)
            dq = (dyb * ps_ref[:, cols]).astype(BF16)
            dpw_ref[gi] += _tn(pooled_b, dq)
            dpooled = _nt(dq, pw_b)
            ebuf[0:tm, cols] = dpooled / cnt
            du = -dpooled
            for s in range(w):
                du = du + ebuf[s:s + tm, cols]
            dproj_ref[:, 3 * dc + gi * gcw:3 * dc + (gi + 1) * gcw] = du.astype(BF16)

        dh = _nn(dproj_ref[...], win[...])
        dg_ref[...] += jnp.sum(dh * xhat, axis=0, keepdims=True)
        dxh = dh * gv
        dx_ref[...] = dxo_v + r * (dxh - xhat * jnp.mean(dxh * xhat, axis=-1, keepdims=True))
        dcbuf[tm:tm + 8, :] = dcbuf[0:8, :]
        ebuf[tm:tm + HALO, :] = ebuf[0:HALO, :]

    tok = lambda i: (n_tiles - 1 - i, 0)
    halo = lambda i: (jnp.maximum((n_tiles - 1 - i) * hb - 1, 0), 0)
    one = lambda i: (0, 0)
    return _launch(
        body, name="mixer_backward", grid=(n_tiles,),
        in_specs=[pl.BlockSpec((tm, d), tok), pl.BlockSpec((tm, d), tok), pl.BlockSpec((1, d), one),
                  pl.BlockSpec((tm, 4 * dc), tok), pl.BlockSpec((HALO, 4 * dc), halo), HBM_SPEC, HBM_SPEC,
                  pl.BlockSpec(conv_w.shape, one), pl.BlockSpec(pool_w.shape, lambda i: (0, 0, 0)), pl.BlockSpec((1, dc), one)],
        out_specs=[pl.BlockSpec((tm, d), tok), pl.BlockSpec((tm, 4 * dc), tok), pl.BlockSpec((tm, d), tok), pl.BlockSpec((tm, d), tok),
                   pl.BlockSpec((1, d), one), pl.BlockSpec(conv_w.shape, one), pl.BlockSpec((1, dc), one),
                   pl.BlockSpec(pool_w.shape, lambda i: (0, 0, 0))],
        out_shape=[jax.ShapeDtypeStruct((t, d), F32), jax.ShapeDtypeStruct((t, 4 * dc), BF16), jax.ShapeDtypeStruct((t, d), BF16),
                   jax.ShapeDtypeStruct((t, d), BF16), jax.ShapeDtypeStruct((1, d), F32), jax.ShapeDtypeStruct(conv_w.shape, F32),
                   jax.ShapeDtypeStruct((1, dc), F32), jax.ShapeDtypeStruct(pool_w.shape, F32)],
        scratch_shapes=[pltpu.VMEM((4 * dc, d), BF16), pltpu.VMEM((2 * dc, d), BF16),
                        pltpu.VMEM((tm + HALO, dc), F32), pltpu.VMEM((tm + HALO, dc), F32),
                        pltpu.VMEM((tm + 8, dc), F32), pltpu.VMEM((tm + HALO, dc), F32), pltpu.SemaphoreType.DMA((1 + N_CHIPS,))],
        args=(dxo, x, g, proj, proj, win_t, wout_x, conv_w, pool_w, pool_scale), cargo=cargo)


def _adam_update(w, gv, m, v):
    m_new = ADAM_B1 * m + (1.0 - ADAM_B1) * gv
    v_new = ADAM_B2 * v + (1.0 - ADAM_B2) * (gv * gv)
    m_hat = m_new / (1.0 - ADAM_B1 ** ADAM_STEP)
    v_hat = v_new / (1.0 - ADAM_B2 ** ADAM_STEP)
    return -ADAM_LR * (m_hat / (jnp.sqrt(v_hat) + ADAM_EPS) + ADAM_WD * w), m_new, v_new


def _adamw(w, grad, m, v, name):
    rows, cols = w.shape
    br = _row_block(rows, 256) if rows >= 8 else rows

    def body(w_ref, g_ref, m_ref, v_ref, d_ref, mo_ref, vo_ref):
        d_ref[...], mo_ref[...], vo_ref[...] = _adam_update(w_ref[...], g_ref[...], m_ref[...], v_ref[...])

    blk = pl.BlockSpec((br, cols), lambda i: (i, 0))
    return pl.pallas_call(
        body, name=name,
        out_shape=[jax.ShapeDtypeStruct((rows, cols), F32)] * 3,
        grid=(rows // br,), in_specs=[blk] * 4, out_specs=[blk] * 3,
        compiler_params=pltpu.CompilerParams(dimension_semantics=("parallel",)),
    )(w, grad, m, v)


def _adamw_transposed(w, grad_t, m, v, name):
    _, rows, cols = w.shape
    br = 256 if rows % 256 == 0 else rows

    def body(w_ref, gt_ref, m_ref, v_ref, g_ref, d_ref, mo_ref, vo_ref):
        gv = gt_ref[...].T
        g_ref[...] = gv
        d_ref[...], mo_ref[...], vo_ref[...] = _adam_update(w_ref[...], gv, m_ref[...], v_ref[...])

    blk = pl.BlockSpec((None, br, cols), lambda i: (0, i, 0))
    return pl.pallas_call(
        body, name=name,
        out_shape=[jax.ShapeDtypeStruct((1, rows, cols), F32)] * 4,
        grid=(rows // br,), in_specs=[blk, pl.BlockSpec((cols, br), lambda i: (0, i)), blk, blk], out_specs=[blk] * 4,
        compiler_params=pltpu.CompilerParams(dimension_semantics=("parallel",)),
    )(w, grad_t, m, v)


def _f32_rows_as_bf16(a, rows, cols):
    bits = lax.bitcast_convert_type(a, BF16).reshape(a.shape[0], 2 * a.shape[1])
    return jnp.pad(bits, ((0, rows - bits.shape[0]), (0, cols - bits.shape[1])))


def kernel(x, norm_ffn1, ffn1_w_gate, ffn1_w_up, ffn1_w_down, norm_mix, w_in, conv_w, pool_w, pool_scale, w_out, norm_ffn2, ffn2_w_gate, ffn2_w_up, ffn2_w_down, norm_final, loss_target, m_norm_ffn1, m_ffn1_w_gate, m_ffn1_w_up, m_ffn1_w_down, m_norm_mix, m_w_in, m_conv_w, m_pool_w, m_pool_scale, m_w_out, m_norm_ffn2, m_ffn2_w_gate, m_ffn2_w_up, m_ffn2_w_down, m_norm_final, v_norm_ffn1, v_ffn1_w_gate, v_ffn1_w_up, v_ffn1_w_down, v_norm_mix, v_w_in, v_conv_w, v_pool_w, v_pool_scale, v_w_out, v_norm_ffn2, v_ffn2_w_gate, v_ffn2_w_up, v_ffn2_w_down, v_norm_final):
    weights = dict(norm_ffn1=norm_ffn1, ffn1_w_gate=ffn1_w_gate, ffn1_w_up=ffn1_w_up, ffn1_w_down=ffn1_w_down, norm_mix=norm_mix,
                   w_in=w_in, conv_w=conv_w, pool_w=pool_w, pool_scale=pool_scale, w_out=w_out, norm_ffn2=norm_ffn2,
                   ffn2_w_gate=ffn2_w_gate, ffn2_w_up=ffn2_w_up, ffn2_w_down=ffn2_w_down, norm_final=norm_final)
    first_m = dict(norm_ffn1=m_norm_ffn1, ffn1_w_gate=m_ffn1_w_gate, ffn1_w_up=m_ffn1_w_up, ffn1_w_down=m_ffn1_w_down,
                   norm_mix=m_norm_mix, w_in=m_w_in, conv_w=m_conv_w, pool_w=m_pool_w, pool_scale=m_pool_scale, w_out=m_w_out,
                   norm_ffn2=m_norm_ffn2, ffn2_w_gate=m_ffn2_w_gate, ffn2_w_up=m_ffn2_w_up, ffn2_w_down=m_ffn2_w_down,
                   norm_final=m_norm_final)
    second_m = dict(norm_ffn1=v_norm_ffn1, ffn1_w_gate=v_ffn1_w_gate, ffn1_w_up=v_ffn1_w_up, ffn1_w_down=v_ffn1_w_down,
                    norm_mix=v_norm_mix, w_in=v_w_in, conv_w=v_conv_w, pool_w=v_pool_w, pool_scale=v_pool_scale, w_out=v_w_out,
                    norm_ffn2=v_norm_ffn2, ffn2_w_gate=v_ffn2_w_gate, ffn2_w_up=v_ffn2_w_up, ffn2_w_down=v_ffn2_w_down,
                    norm_final=v_norm_final)
    names = list(weights)

    xs = x[0]
    tgt = loss_target[0]
    t, d = xs.shape
    dc = pool_scale.shape[1]
    cx, cy, cc = _my_place()
    chip = 2 * cx + cy
    place = jnp.stack([chip, cc]).astype(jnp.int32)

    conv_rows = 32
    wout_x = jnp.concatenate([w_out[0].astype(BF16), _f32_rows_as_bf16(conv_w[0], conv_rows, d)], axis=0)
    wg2_shard = ffn2_w_gate[0].T.astype(BF16)
    half_rows = wg2_shard.shape[0] // 2

    g1, gm, g2 = norm_ffn1, norm_mix, norm_ffn2
    gf = norm_final.reshape(1, d)
    pw = pool_w[0]

    wg1, wu1 = [[w] for w in _run_cargo(_gather_cargo([ffn1_w_gate[0].T.astype(BF16), ffn1_w_up[0].T.astype(BF16)]), "gather_ffn1")]
    (a1, b1, s1), [(wd1, win_t)] = _ffn_up(xs, g1, wg1, wu1, "ffn1_up", [_gather_cargo([ffn1_w_down[0].astype(BF16), w_in[0].T.astype(BF16)])])
    wd1 = [wd1]
    (x1,), [(wout_g, wg2_a)] = _ffn_down(xs, s1, wd1, "ffn1_down", [_gather_cargo([wout_x, wg2_shard[:half_rows]])])
    wo_rows = w_out.shape[1]
    cshard = conv_w.shape[2]
    conv_bits = wout_g.reshape(N_CHIPS, wo_rows + conv_rows, d)[:, wo_rows:wo_rows + conv_w.shape[1], :2 * cshard]
    conv_full = lax.bitcast_convert_type(conv_bits.reshape(N_CHIPS, conv_w.shape[1], cshard, 2), F32)
    conv_full = jnp.transpose(conv_full, (1, 0, 2)).reshape(conv_w.shape[1], N_CHIPS * cshard)
    (x2, proj, ymix), [(wg2_b, wu2)] = _mixer_forward(
        x1, gm, win_t, wout_g, conv_full, pw, pool_scale, [_gather_cargo([wg2_shard[half_rows:], ffn2_w_up[0].T.astype(BF16)])])
    wg2, wu2 = [wg2_a, wg2_b], [wu2]
    (a2, b2, s2), [(wd2,)] = _ffn_up(x2, g2, wg2, wu2, "ffn2_up", [_gather_cargo([ffn2_w_down[0].astype(BF16)])])
    wd2 = [wd2]
    (dx3, sq_cols, dgf), _ = _ffn_down(x2, s2, wd2, "ffn2_down", loss_head=(gf, tgt))

    (dx2, da2, db2, h3, do2, dg2), _ = _ffn_backward(dx3, x2, g2, a2, b2, wg2, wu2, wd2, "ffn2_backward")
    p_wg2, _ = _weight_grad(da2, h3, "ffn2_gate_grad")
    p_wu2, [(x_wg2,)] = _weight_grad(db2, h3, "ffn2_up_grad", [_exchange_cargo([p_wg2])])
    p_wd2, [(x_wu2,)] = _weight_grad(s2, do2, "ffn2_down_grad", [_exchange_cargo([p_wu2])])

    (dx1, dproj, h2, dx2b, dgm, dcw, dps, dpw), [(x_wd2,)] = _mixer_backward(
        dx2, x1, gm, proj, win_t, wout_g, conv_full, pw, pool_scale, [_exchange_cargo([p_wd2])])

    (dx0, da1, db1, h1, do1, dg1), _ = _ffn_backward(dx1, xs, g1, a1, b1, wg1, wu1, wd1, "ffn1_backward")

    npw = pw.size // d
    head = [dg1, dgm, dg2, dgf, jnp.pad(dps, ((0, 0), (0, d - dc))), jnp.pad(dcw, ((0, 0), (0, d - dc))), sq_cols]
    n_head = sum(h.shape[0] for h in head)
    base = -(-n_head // 8) * 8
    pack = jnp.concatenate(head + [jnp.zeros((base - n_head, d), F32), dpw.reshape(npw, d)], axis=0)

    p_wg1, [(packs,)] = _weight_grad(da1, h1, "ffn1_gate_grad", [_all_gather_small_cargo(pack)])
    p_wu1, [(x_wg1,)] = _weight_grad(db1, h1, "ffn1_up_grad", [_exchange_cargo([p_wg1])])
    p_wd1, [(x_wu1,)] = _weight_grad(s1, do1, "ffn1_down_grad", [_exchange_cargo([p_wu1])])
    p_win, [(x_wd1,)] = _weight_grad(dproj, h2, "w_in_grad", [_exchange_cargo([p_wd1])])
    p_wout, [(x_win,)] = _weight_grad(ymix, dx2b, "w_out_grad", [_exchange_cargo([p_win])])
    x_wout, = _run_cargo(_exchange_cargo([p_wout]), "grad_exchange_last")
    small = _sum_by_device(packs)
    loss = jnp.sum(small[n_head - 1]) * (0.5 / d)

    order = ["wg1", "wu1", "wd1", "win", "wout", "wg2", "wu2", "wd2"]
    pairs = dict(wg1=p_wg1, wu1=p_wu1, wd1=p_wd1, win=p_win, wout=p_wout, wg2=p_wg2, wu2=p_wu2, wd2=p_wd2)
    landed = dict(wg1=x_wg1, wu1=x_wu1, wd1=x_wd1, win=x_win, wout=x_wout, wg2=x_wg2, wu2=x_wu2, wd2=x_wd2)
    both = _sibling_share([_chip_sum(pairs[k], landed[k], place, k) for k in order])
    rwg1, rwu1, rwd1, rwin, rwout, rwg2, rwu2, rwd2 = [b.reshape(2 * b.shape[1], b.shape[2]) for b in both]

    grads = {
        "norm_ffn1": small[0:1], "norm_mix": small[1:2], "norm_ffn2": small[2:3], "norm_final": small[3],
        "pool_scale": small[4:5, :dc],
        "conv_w": lax.dynamic_slice_in_dim(small[5:5 + dcw.shape[0], :dc], chip * cshard, cshard, axis=1)[None],
        "pool_w": small[base:].reshape(pool_w.shape),
        "ffn1_w_down": rwd1[None], "w_out": rwout[None], "ffn2_w_down": rwd2[None],
    }
    by_view = {"ffn1_w_gate": rwg1, "ffn1_w_up": rwu1, "ffn2_w_gate": rwg2, "ffn2_w_up": rwu2}

    deltas, new_m, new_v = {}, {}, {}
    for n in names:
        w = weights[n]
        shape = w.shape
        if n == "w_in":
            grads[n], deltas[n], new_m[n], new_v[n] = _adamw_transposed(w, rwin, first_m[n], second_m[n], "adamw_" + n)
            continue
        if n in by_view:
            view = lambda a: jnp.swapaxes(a, 1, 2)[0]
            back = lambda a: jnp.swapaxes(a[None], 1, 2)
            dl, mo, vo = _adamw(view(w), by_view[n], view(first_m[n]), view(second_m[n]), "adamw_" + n)
            grads[n], deltas[n], new_m[n], new_v[n] = back(by_view[n]), back(dl), back(mo), back(vo)
            continue
        as2d = (lambda a: a.reshape(-1, shape[-1]))
        dl, mo, vo = _adamw(as2d(w), as2d(grads[n]), as2d(first_m[n]), as2d(second_m[n]), "adamw_" + n)
        deltas[n], new_m[n], new_v[n] = dl.reshape(shape), mo.reshape(shape), vo.reshape(shape)
        grads[n] = grads[n].reshape(shape)

    return (loss, dx0[None], *[grads[n] for n in names], *[deltas[n] for n in names],
            *[new_m[n] for n in names], *[new_v[n] for n in names])
```

```python
import jax
import jax.numpy as jnp
from jax import lax
from jax.experimental import pallas as pl
from jax.experimental.pallas import tpu as pltpu

F32 = jnp.float32
BF16 = jnp.bfloat16
MESH = pl.DeviceIdType.MESH

EPS = 1e-6
POOL_WINDOWS = (2, 4, 8, 16)
ADAM_LR = 0.001
ADAM_B1 = 0.9
ADAM_B2 = 0.999
ADAM_EPS = 1e-08
ADAM_WD = 0.01
ADAM_STEP = 10

N_CHIPS = 4
N_DEVICES = 8
MXU_COLS_V7X = 256
VMEM_LIMIT = 56 * 1024 * 1024
TM_FFN = 512
TM_MIX = 512
TM_TN = 1024
HALO = 16
FFN_FWD_CHUNKS = 2
FFN_BWD_CHUNKS = 2


def _nt(a, b):
    return lax.dot_general(a, b, (((1,), (1,)), ((), ())), preferred_element_type=F32)


def _tn(a, b):
    return lax.dot_general(a, b, (((0,), (0,)), ((), ())), preferred_element_type=F32)


def _nn(a, b):
    return jnp.dot(a, b, preferred_element_type=F32)


def _sigmoid(a):
    return 1.0 / (1.0 + jnp.exp(-a))


def _feature_chunks(n, parts):
    assert n % MXU_COLS_V7X == 0
    tiles = n // MXU_COLS_V7X
    out, s0 = [], 0
    for p in range(parts):
        sz = (tiles // parts + (1 if p < tiles % parts else 0)) * MXU_COLS_V7X
        if sz:
            out.append((s0, sz))
            s0 += sz
    return out


def _row_block(rows, cap):
    best = 8
    for b in range(8, min(rows, cap) + 1, 8):
        if rows % b == 0:
            best = b
    assert rows % best == 0
    return best


def _my_place():
    return lax.axis_index("x"), lax.axis_index("y"), lax.axis_index("c")


def _other_chips(x, y):
    return [(1 - x, y), (x, 1 - y), (1 - x, 1 - y)]


HBM_SPEC = pl.BlockSpec(memory_space=pltpu.HBM)


class _Cargo:
    def __init__(self, operands, out_shapes, n_sems, start, finish, relay=None):
        self.operands, self.out_shapes, self.n_sems = list(operands), list(out_shapes), n_sems
        self.start, self.finish, self.relay = start, finish, relay


def _launch(body, *, name, grid, in_specs, out_specs, out_shape, scratch_shapes, args, cargo=()):
    params = pltpu.CompilerParams(dimension_semantics=("arbitrary",) * len(grid), vmem_limit_bytes=VMEM_LIMIT)
    cargos = list(cargo)
    c_operands = [op for cg in cargos for op in cg.operands]
    c_shapes = [sh for cg in cargos for sh in cg.out_shapes]
    counts = [len(in_specs), len(c_operands), len(out_shape), len(c_shapes), len(scratch_shapes), 2 * len(cargos)]

    def carrying(*refs):
        groups, pos = [], 0
        for k in counts:
            groups.append(refs[pos:pos + k])
            pos += k
        ins, c_ins, outs, c_outs, scratch, sems = groups
        parts, pi, po = [], 0, 0
        for n, cg in enumerate(cargos):
            parts.append((c_ins[pi:pi + len(cg.operands)], c_outs[po:po + len(cg.out_shapes)], sems[2 * n], sems[2 * n + 1]))
            pi += len(cg.operands)
            po += len(cg.out_shapes)
        step, steps = 0, 1
        for ax, g in enumerate(grid):
            step = step * g + pl.program_id(ax)
            steps *= g
        relayed = [cg for cg in cargos if cg.relay is not None]

        if cargos:
            @pl.when(step == 0)
            def _():
                for cg, part in zip(cargos, parts):
                    cg.start(*part)

        if relayed and steps >= 3:
            @pl.when(step == steps - 2)
            def _():
                for cg, part in zip(cargos, parts):
                    if cg.relay is not None:
                        cg.relay(*part)

        body(*ins, *outs, *scratch)

        if cargos:
            @pl.when(step == steps - 1)
            def _():
                for cg, part in zip(cargos, parts):
                    if cg.relay is not None and steps < 3:
                        cg.relay(*part)
                    cg.finish(*part)

    sems = [pltpu.SemaphoreType.DMA((cg.n_sems,)) for cg in cargos for _ in range(2)]
    outs = pl.pallas_call(
        carrying, name=name, grid=grid,
        in_specs=list(in_specs) + [HBM_SPEC] * counts[1], out_specs=list(out_specs) + [HBM_SPEC] * counts[3],
        out_shape=list(out_shape) + c_shapes, scratch_shapes=list(scratch_shapes) + sems,
        compiler_params=params)(*args, *c_operands)
    own, rest = list(outs[:counts[2]]), list(outs[counts[2]:])
    carried, po = [], 0
    for cg in cargos:
        carried.append(rest[po:po + len(cg.out_shapes)])
        po += len(cg.out_shapes)
    return own, carried


def _run_cargo(cargo, name):
    n_in, n_out = len(cargo.operands), len(cargo.out_shapes)

    def body(*refs):
        c_ins, c_outs, sems = refs[:n_in], refs[n_in:n_in + n_out], refs[n_in + n_out:]
        cargo.start(c_ins, c_outs, *sems)
        if cargo.relay is not None:
            cargo.relay(c_ins, c_outs, *sems)
        cargo.finish(c_ins, c_outs, *sems)

    sem = pltpu.SemaphoreType.DMA((cargo.n_sems,))
    return list(pl.pallas_call(body, name=name, out_shape=cargo.out_shapes, in_specs=[HBM_SPEC] * n_in,
                               out_specs=[HBM_SPEC] * n_out, scratch_shapes=[sem, sem])(*cargo.operands))


def _gather_cargo(shards):
    n = len(shards)
    for s in shards:
        assert s.shape[0] % 32 == 0

    def steps(ins, outs, send_sems, recv_sems):
        x, y, c = _my_place()
        sibling = (x, y, 1 - c)
        chips = _other_chips(x, y)
        mine = 2 * x + y

        def rows_of(a, chip_index, half):
            rps = shards[a].shape[0]
            hr = rps // 2
            return outs[a].at[pl.ds(pl.multiple_of(chip_index * rps + half * hr, 16), hr), :]

        def remote(a, slot, src, dst, to):
            return pltpu.make_async_remote_copy(
                src_ref=src, dst_ref=dst, send_sem=send_sems.at[a * 7 + slot], recv_sem=recv_sems.at[a * 7 + slot],
                device_id=to, device_id_type=MESH)

        def own_copy(a):
            rps = shards[a].shape[0]
            return remote(a, 6, ins[a], outs[a].at[pl.ds(pl.multiple_of(mine * rps, 16), rps), :], sibling)

        def my_half(a):
            hr = shards[a].shape[0] // 2
            return ins[a].at[pl.ds(pl.multiple_of(c * hr, 16), hr), :]

        def start():
            for a in range(n):
                own_copy(a).start()
                for j, chip in enumerate(chips):
                    remote(a, j, my_half(a), rows_of(a, mine, c), (*chip, c)).start()

        def relay():
            for a in range(n):
                for j, chip in enumerate(chips):
                    landed = rows_of(a, 2 * chip[0] + chip[1], c)
                    remote(a, j, landed, landed, (*chip, c)).wait_recv()
                    remote(a, 3 + j, landed, landed, sibling).start()

        def finish():
            for a in range(n):
                for j, chip in enumerate(chips):
                    from_sibling = rows_of(a, 2 * chip[0] + chip[1], 1 - c)
                    remote(a, 3 + j, from_sibling, from_sibling, sibling).wait_recv()
            for a in range(n):
                for j, chip in enumerate(chips):
                    remote(a, j, my_half(a), rows_of(a, mine, c), (*chip, c)).wait_send()
                    landed = rows_of(a, 2 * chip[0] + chip[1], c)
                    remote(a, 3 + j, landed, landed, sibling).wait_send()
                own_copy(a).wait()

        return start, relay, finish

    return _Cargo(shards, [jax.ShapeDtypeStruct((N_CHIPS * s.shape[0], s.shape[1]), s.dtype) for s in shards], 7 * n,
                  lambda *r: steps(*r)[0](), lambda *r: steps(*r)[2](), relay=lambda *r: steps(*r)[1]())


def _exchange_cargo(pairs):
    n = len(pairs)

    def copies(ins, outs, send_sems, recv_sems):
        x, y, c = _my_place()
        return [pltpu.make_async_remote_copy(
            src_ref=ins[a].at[2 * chip[0] + chip[1]], dst_ref=outs[a].at[j],
            send_sem=send_sems.at[3 * a + j], recv_sem=recv_sems.at[3 * a + j], device_id=(*chip, c), device_id_type=MESH)
            for a in range(n) for j, chip in enumerate(_other_chips(x, y))]

    def start(*r):
        for cp in copies(*r):
            cp.start()

    def finish(*r):
        for cp in copies(*r):
            cp.wait()

    return _Cargo(pairs, [jax.ShapeDtypeStruct((3,) + p.shape[1:], p.dtype) for p in pairs], 3 * n, start, finish)


def _all_gather_small_cargo(pack):
    rows, cols = pack.shape

    def copies(ins, outs, send_sems, recv_sems):
        x, y, c = _my_place()
        me = 4 * x + 2 * y + c
        remote = []
        for f in range(1, N_DEVICES):
            fx, fy, fc = (f >> 2) & 1, (f >> 1) & 1, f & 1
            to = (1 - x if fx else x, 1 - y if fy else y, 1 - c if fc else c)
            remote.append(pltpu.make_async_remote_copy(
                src_ref=ins[0], dst_ref=outs[0].at[me], send_sem=send_sems.at[f - 1], recv_sem=recv_sems.at[f - 1],
                device_id=to, device_id_type=MESH))
        own = pltpu.make_async_copy(ins[0], outs[0].at[me], send_sems.at[N_DEVICES - 1])
        return remote, own

    def start(*r):
        remote, own = copies(*r)
        own.start()
        for cp in remote:
            cp.start()

    def finish(*r):
        remote, own = copies(*r)
        for cp in remote:
            cp.wait()
        own.wait()

    return _Cargo([pack], [jax.ShapeDtypeStruct((N_DEVICES, rows, cols), F32)], N_DEVICES, start, finish)


def _sum_by_device(packs):
    n, rows, cols = packs.shape

    def body(p_ref, o_ref):
        acc = p_ref[0]
        for dev in range(1, n):
            acc = acc + p_ref[dev]
        o_ref[...] = acc

    return pl.pallas_call(body, name="small_grads_sum", out_shape=jax.ShapeDtypeStruct((rows, cols), F32))(packs)


def _chip_sum(pair, got, place, tag):
    _, hr, cols = pair.shape
    br = _row_block(hr, 256)

    def body(k_ref, p_ref, r_ref, o_ref):
        acc = p_ref[...].astype(F32)
        for j in range(3):
            acc = acc + r_ref[j].astype(F32)
        o_ref[...] = acc

    return pl.pallas_call(
        body, name="grad_chip_sum_" + tag,
        out_shape=jax.ShapeDtypeStruct((2, hr, cols), F32),
        grid_spec=pltpu.PrefetchScalarGridSpec(
            num_scalar_prefetch=1, grid=(hr // br,),
            in_specs=[pl.BlockSpec((None, br, cols), lambda r, k_ref: (k_ref[0], r, 0)),
                      pl.BlockSpec((3, br, cols), lambda r, k_ref: (0, r, 0))],
            out_specs=pl.BlockSpec((None, br, cols), lambda r, k_ref: (k_ref[1], r, 0))),
        compiler_params=pltpu.CompilerParams(dimension_semantics=("parallel",)),
    )(place, pair, got)


def _sibling_share(halves):
    n = len(halves)

    def body(*refs):
        outs = refs[n:2 * n]
        send_sems, recv_sems = refs[2 * n:]
        x, y, c = _my_place()
        copies = []
        for a in range(n):
            cp = pltpu.make_async_remote_copy(
                src_ref=outs[a].at[c], dst_ref=outs[a].at[c], send_sem=send_sems.at[a], recv_sem=recv_sems.at[a],
                device_id=(x, y, 1 - c), device_id_type=MESH)
            cp.start()
            copies.append(cp)
        for cp in copies:
            cp.wait()

    return pl.pallas_call(
        body, name="grad_share_sibling",
        out_shape=[jax.ShapeDtypeStruct(h.shape, h.dtype) for h in halves],
        in_specs=[HBM_SPEC] * n, out_specs=[HBM_SPEC] * n,
        input_output_aliases={a: a for a in range(n)},
        scratch_shapes=[pltpu.SemaphoreType.DMA((n,)), pltpu.SemaphoreType.DMA((n,))],
    )(*halves)


def _load_rows(pairs, sems):
    cps = [pltpu.make_async_copy(src, dst, sems.at[j]) for j, (src, dst) in enumerate(pairs)]
    for cp in cps:
        cp.start()
    for cp in cps:
        cp.wait()


def _piece_rows(weights):
    flat = [p for pieces in weights for p in pieces]

    def copies(refs, mats):
        out, n = [], 0
        for pieces, mat in zip(weights, mats):
            rps = sum(p.shape[0] for p in pieces) // N_CHIPS
            off = 0
            for p in pieces:
                r = p.shape[0] // N_CHIPS
                if len(pieces) == 1:
                    out.append((refs[n], mat))
                else:
                    for k in range(N_CHIPS):
                        out.append((refs[n].at[pl.ds(k * r, r), :], mat.at[pl.ds(k * rps + off, r), :]))
                off += r
                n += 1
        return out

    n_copies = sum(1 if len(pieces) == 1 else N_CHIPS * len(pieces) for pieces in weights)
    return flat, copies, n_copies


def _loss_head(xv, gv, tv):
    d = xv.shape[-1]
    r = lax.rsqrt(jnp.mean(xv * xv, axis=-1, keepdims=True) + EPS)
    xhat = xv * r
    err = xhat * gv - tv
    dy = err * (1.0 / d)
    dxh = dy * gv
    dx = r * (dxh - xhat * jnp.mean(dxh * xhat, axis=-1, keepdims=True))
    return dx, jnp.sum(err * err, axis=0, keepdims=True), jnp.sum(dy * xhat, axis=0, keepdims=True)


def _ffn_up(x, g, wg_t, wu_t, name, cargo=()):
    t, d = x.shape
    f = sum(p.shape[0] for p in wg_t)
    tm = min(TM_FFN, t)
    chunks = _feature_chunks(f, FFN_FWD_CHUNKS)
    flat, copies, n_copies = _piece_rows([wg_t, wu_t])
    nw = len(flat)

    def body(x_ref, g_ref, *rest):
        w_hbm, (a_ref, b_ref, s_ref, wg, wu, sems) = rest[:nw], rest[nw:]

        @pl.when(pl.program_id(0) == 0)
        def _():
            _load_rows(copies(w_hbm, [wg, wu]), sems)

        xv = x_ref[...]
        r = lax.rsqrt(jnp.mean(xv * xv, axis=-1, keepdims=True) + EPS)
        h = (xv * r * g_ref[...]).astype(BF16)
        for s0, sz in chunks:
            a = _nt(h, wg[s0:s0 + sz, :])
            b = _nt(h, wu[s0:s0 + sz, :])
            a_ref[:, s0:s0 + sz] = a.astype(BF16)
            b_ref[:, s0:s0 + sz] = b.astype(BF16)
            s_ref[:, s0:s0 + sz] = (a * _sigmoid(a) * b).astype(BF16)

    tok = lambda i: (i, 0)
    wide = pl.BlockSpec((tm, f), tok)
    return _launch(
        body, name=name, grid=(t // tm,),
        in_specs=[pl.BlockSpec((tm, d), tok), pl.BlockSpec((1, d), lambda i: (0, 0))] + [HBM_SPEC] * nw,
        out_specs=[wide, wide, wide], out_shape=[jax.ShapeDtypeStruct((t, f), BF16)] * 3,
        scratch_shapes=[pltpu.VMEM((f, d), BF16), pltpu.VMEM((f, d), BF16), pltpu.SemaphoreType.DMA((n_copies,))],
        args=(x, g, *flat), cargo=cargo)


def _ffn_down(x, s, wd, name, cargo=(), loss_head=None):
    t, d = x.shape
    f = s.shape[1]
    tm = min(2 * TM_FFN, t)
    flat, copies, n_copies = _piece_rows([wd])
    nw = len(flat)
    nl = 2 if loss_head else 0

    def body(x_ref, s_ref, *rest):
        head, w_hbm = rest[:nl], rest[nl:nl + nw]
        xo_ref = rest[nl + nw]
        sums, (wdn, sems) = rest[nl + nw + 1:nl + nw + 1 + nl], rest[nl + nw + 1 + nl:]

        @pl.when(pl.program_id(0) == 0)
        def _():
            _load_rows(copies(w_hbm, [wdn]), sems)
            for sum_ref in sums:
                sum_ref[...] = jnp.zeros_like(sum_ref)

        xo = x_ref[...] + 0.5 * _nn(s_ref[...], wdn[...])
        if loss_head:
            dx, sq, dgf = _loss_head(xo, head[0][...], head[1][...])
            xo_ref[...] = dx
            sums[0][...] += sq
            sums[1][...] += dgf
        else:
            xo_ref[...] = xo

    tok = lambda i: (i, 0)
    one = lambda i: (0, 0)
    return _launch(
        body, name=name, grid=(t // tm,),
        in_specs=[pl.BlockSpec((tm, d), tok), pl.BlockSpec((tm, f), tok)]
        + ([pl.BlockSpec((1, d), one), pl.BlockSpec((tm, d), tok)] if loss_head else []) + [HBM_SPEC] * nw,
        out_specs=[pl.BlockSpec((tm, d), tok)] + [pl.BlockSpec((1, d), one)] * nl,
        out_shape=[jax.ShapeDtypeStruct((t, d), F32)] + [jax.ShapeDtypeStruct((1, d), F32)] * nl,
        scratch_shapes=[pltpu.VMEM((f, d), BF16), pltpu.SemaphoreType.DMA((n_copies,))],
        args=(x, s, *(loss_head or ()), *flat), cargo=cargo)


def _ffn_backward(dxo, x, g, a, b, wg_t, wu_t, wd, name, cargo=()):
    t, d = x.shape
    f = sum(p.shape[0] for p in wd)
    tm = min(TM_FFN // 2, t)
    chunks = _feature_chunks(f, FFN_BWD_CHUNKS)
    flat, copies, n_copies = _piece_rows([wg_t, wu_t, wd])
    nw = len(flat)

    def body(dxo_ref, x_ref, g_ref, a_ref, b_ref, *rest):
        w_hbm, (dx_ref, da_ref, db_ref, h_ref, do_ref, dg_ref, wg, wu, wdn, sems) = rest[:nw], rest[nw:]

        @pl.when(pl.program_id(0) == 0)
        def _():
            _load_rows(copies(w_hbm, [wg, wu, wdn]), sems)
            dg_ref[...] = jnp.zeros_like(dg_ref)

        xv = x_ref[...]
        gv = g_ref[...]
        r = lax.rsqrt(jnp.mean(xv * xv, axis=-1, keepdims=True) + EPS)
        xhat = xv * r
        h_ref[...] = (xhat * gv).astype(BF16)
        dxo_v = dxo_ref[...]
        dout = (0.5 * dxo_v).astype(BF16)
        do_ref[...] = dout
        dh = jnp.zeros((tm, d), F32)
        for s0, sz in chunks:
            ds = _nt(dout, wdn[s0:s0 + sz, :])
            av = a_ref[:, s0:s0 + sz].astype(F32)
            bv = b_ref[:, s0:s0 + sz].astype(F32)
            sig = _sigmoid(av)
            silu = av * sig
            da =(ds * bv * (sig * (1.0 + av * (1.0 - sig)))).astype(BF16)
            db = (ds * silu).astype(BF16)
            da_ref[:, s0:s0 + sz] = da
            db_ref[:, s0:s0 + sz] = db
            dh = dh + _nn(da, wg[s0:s0 + sz, :]) + _nn(db, wu[s0:s0 + sz, :])
        dg_ref[...] += jnp.sum(dh * xhat, axis=0, keepdims=True)
        dxh = dh * gv
        dx_ref[...] = dxo_v + r * (dxh - xhat * jnp.mean(dxh * xhat, axis=-1, keepdims=True))

    tok = lambda i: (i, 0)
    one = lambda i: (0, 0)
    return _launch(
        body, name=name, grid=(t // tm,),
        in_specs=[pl.BlockSpec((tm, d), tok), pl.BlockSpec((tm, d), tok), pl.BlockSpec((1, d), one),
                  pl.BlockSpec((tm, f), tok), pl.BlockSpec((tm, f), tok)] + [HBM_SPEC] * nw,
        out_specs=[pl.BlockSpec((tm, d), tok), pl.BlockSpec((tm, f), tok), pl.BlockSpec((tm, f), tok),
                   pl.BlockSpec((tm, d), tok), pl.BlockSpec((tm, d), tok), pl.BlockSpec((1, d), one)],
        out_shape=[jax.ShapeDtypeStruct((t, d), F32), jax.ShapeDtypeStruct((t, f), BF16), jax.ShapeDtypeStruct((t, f), BF16),
                   jax.ShapeDtypeStruct((t, d), BF16), jax.ShapeDtypeStruct((t, d), BF16), jax.ShapeDtypeStruct((1, d), F32)],
        scratch_shapes=[pltpu.VMEM((f, d), BF16), pltpu.VMEM((f, d), BF16), pltpu.VMEM((f, d), BF16), pltpu.SemaphoreType.DMA((n_copies,))],
        args=(dxo, x, g, a, b, *flat), cargo=cargo)


def _weight_grad(lhs, rhs, name, cargo=()):
    t, m = lhs.shape
    d = rhs.shape[1]
    tm = min(TM_TN, t)
    nt = t // tm
    nj = 1
    bm = m // nj
    cpb = N_CHIPS // nj
    rps = m // N_CHIPS
    hr = rps // 2
    assert hr % 16 == 0

    def body(l_ref, r_ref, o_ref, acc, stage, recv, send_sems, recv_sems):
        j = pl.program_id(0)
        i = pl.program_id(1)
        @pl.when(i == 0)
        def _():
            acc[...] = jnp.zeros_like(acc)

        acc[...] += _tn(l_ref[...], r_ref[...])

        def pair_sum(jj):
            x, y, c = _my_place()
            copies = []
            for q in range(cpb):
                slot = jj * cpb + q
                stage[slot] = acc[pl.ds(pl.multiple_of(q * rps + (1 - c) * hr, 16), hr), :].astype(BF16)
                cp = pltpu.make_async_remote_copy(
                    src_ref=stage.at[slot], dst_ref=recv.at[slot], send_sem=send_sems.at[slot], recv_sem=recv_sems.at[slot],
                    device_id=(x, y, 1 - c), device_id_type=MESH)
                cp.start()
                copies.append(cp)
            for q, cp in enumerate(copies):
                cp.wait_recv()
                mine = acc[pl.ds(pl.multiple_of(q * rps + c * hr, 16), hr), :]
                o_ref[q] = (mine + recv[jj * cpb + q].astype(F32)).astype(BF16)
            for cp in copies:
                cp.wait_send()

        for jj in range(nj):
            @pl.when(jnp.logical_and(i == nt - 1, j == jj))
            def _():
                pair_sum(jj)

    outs, carried = _launch(
        body, name=name, grid=(nj, nt),
        in_specs=[pl.BlockSpec((tm, bm), lambda j, i: (i, j)), pl.BlockSpec((tm, d), lambda j, i: (i, 0))],
        out_specs=[pl.BlockSpec((cpb, hr, d), lambda j, i: (j, 0, 0))],
        out_shape=[jax.ShapeDtypeStruct((N_CHIPS, hr, d), BF16)],
        scratch_shapes=[pltpu.VMEM((bm, d), F32), pltpu.VMEM((N_CHIPS, hr, d), BF16), pltpu.VMEM((N_CHIPS, hr, d), BF16),
                        pltpu.SemaphoreType.DMA((N_CHIPS,)), pltpu.SemaphoreType.DMA((N_CHIPS,))],
        args=(lhs, rhs), cargo=cargo)
    return outs[0], carried


def _pool_parts(u_cols, ubuf, cols, w, row, tm):
    ws = u_cols
    for s in range(1, w):
        ws = ws + ubuf[HALO - s:HALO - s + tm, cols]
    cnt = jnp.minimum(row + 1, w).astype(F32)
    return ws / cnt - u_cols, cnt


def _mixer_forward(x, g, win_t, wout_x, conv_w, pool_w, pool_scale, cargo=()):
    t, d = x.shape
    dc = win_t.shape[0] // 4
    gcw = dc // len(POOL_WINDOWS)
    wo_rows = d // N_CHIPS
    wo_stride = wout_x.shape[0] // N_CHIPS
    tm = min(TM_MIX, t)

    def body(x_ref, g_ref, win_hbm, wout_hbm, cw_ref, pw_ref, ps_ref, xo_ref, proj_ref, y_ref,
             win, wout, zbuf, ubuf, sems):
        i = pl.program_id(0)

        @pl.when(i == 0)
        def _():
            pairs = [(win_hbm, win)]
            for k in range(N_CHIPS):
                pairs.append((wout_hbm.at[pl.ds(k * wo_stride, wo_rows), :], wout.at[pl.ds(k * wo_rows, wo_rows), :]))
            _load_rows(pairs, sems)
            zbuf[0:8, :] = jnp.zeros((8, dc), F32)
            ubuf[0:HALO, :] = jnp.zeros((HALO, dc), F32)

        xv = x_ref[...]
        r = lax.rsqrt(jnp.mean(xv * xv, axis=-1, keepdims=True) + EPS)
        h = (xv * r * g_ref[...]).astype(BF16)
        v = _nt(h, win[0:dc, :])
        gb = _nt(h, win[dc:2 * dc, :])
        gc = _nt(h, win[2 * dc:3 * dc, :])
        u = _nt(h, win[3 * dc:4 * dc, :])
        proj_ref[:, 0:dc] = v.astype(BF16)
        proj_ref[:, dc:2 * dc] = gb.astype(BF16)
        proj_ref[:, 2 * dc:3 * dc] = gc.astype(BF16)
        proj_ref[:, 3 * dc:4 * dc] = u.astype(BF16)

        z = gc * v
        zbuf[8:8 + tm, :] = z
        cw = cw_ref[...]
        conv = cw[2:3, :] * z + cw[1:2, :] * zbuf[7:7 + tm, :] + cw[0:1, :] * zbuf[6:6 + tm, :]
        y_ref[:, 0:dc] = (gb * conv).astype(BF16)

        ubuf[HALO:HALO + tm, :] = u
        row = i * tm + lax.broadcasted_iota(jnp.int32, (tm, 1), 0)
        for gi, w in enumerate(POOL_WINDOWS):
            cols = slice(gi * gcw, (gi + 1) * gcw)
            pooled, _ = _pool_parts(u[:, cols], ubuf, cols, w, row, tm)
            yb = _nn(pooled.astype(BF16), pw_ref[gi].astype(BF16)) * ps_ref[:, cols]
            y_ref[:, dc + gi * gcw:dc + (gi + 1) * gcw] = yb.astype(BF16)

        xo_ref[...] = xv + _nn(y_ref[...], wout[...])
        zbuf[0:8, :] = zbuf[tm:tm + 8, :]
        ubuf[0:HALO, :] = ubuf[tm:tm + HALO, :]

    tok = lambda i: (i, 0)
    one = lambda i: (0, 0)
    return _launch(
        body, name="mixer_forward", grid=(t // tm,),
        in_specs=[pl.BlockSpec((tm, d), tok), pl.BlockSpec((1, d), one), HBM_SPEC, HBM_SPEC,
                  pl.BlockSpec(conv_w.shape, one), pl.BlockSpec(pool_w.shape, lambda i: (0, 0, 0)), pl.BlockSpec((1, dc), one)],
        out_specs=[pl.BlockSpec((tm, d), tok), pl.BlockSpec((tm, 4 * dc), tok), pl.BlockSpec((tm, 2 * dc), tok)],
        out_shape=[jax.ShapeDtypeStruct((t, d), F32), jax.ShapeDtypeStruct((t, 4 * dc), BF16), jax.ShapeDtypeStruct((t, 2 * dc), BF16)],
        scratch_shapes=[pltpu.VMEM((4 * dc, d), BF16), pltpu.VMEM((2 * dc, d), BF16),
                        pltpu.VMEM((tm + 8, dc), F32), pltpu.VMEM((tm + HALO, dc), F32), pltpu.SemaphoreType.DMA((1 + N_CHIPS,))],
        args=(x, g, win_t, wout_x, conv_w, pool_w, pool_scale), cargo=cargo)


def _mixer_backward(dxo, x, g, proj, win_t, wout_x, conv_w, pool_w, pool_scale, cargo=()):
    t, d = x.shape
    dc = win_t.shape[0] // 4
    ng = len(POOL_WINDOWS)
    gcw = dc // ng
    wo_rows = d // N_CHIPS
    wo_stride = wout_x.shape[0] // N_CHIPS
    tm = min(TM_MIX, t)
    n_tiles = t // tm
    hb = tm // HALO

    def body(dxo_ref, x_ref, g_ref, proj_ref, halo_ref, win_hbm, wout_hbm, cw_ref, pw_ref, ps_ref,
             dx_ref, dproj_ref, h_ref, dxob_ref, dg_ref, dcw_ref, dps_ref, dpw_ref,
             win, wout, zbuf, ubuf, dcbuf, ebuf, sems):
        i = pl.program_id(0)
        tile = n_tiles - 1 - i

        @pl.when(i == 0)
        def _():
            pairs = [(win_hbm, win)]
            for k in range(N_CHIPS):
                pairs.append((wout_hbm.at[pl.ds(k * wo_stride, wo_rows), :], wout.at[pl.ds(k * wo_rows, wo_rows), :]))
            _load_rows(pairs, sems)
            dcbuf[tm:tm + 8, :] = jnp.zeros((8, dc), F32)
            ebuf[tm:tm + HALO, :] = jnp.zeros((HALO, dc), F32)
            dg_ref[...] = jnp.zeros_like(dg_ref)
            dcw_ref[...] = jnp.zeros_like(dcw_ref)
            dps_ref[...] = jnp.zeros_like(dps_ref)
            dpw_ref[...] = jnp.zeros_like(dpw_ref)

        xv = x_ref[...]
        gv = g_ref[...]
        r = lax.rsqrt(jnp.mean(xv * xv, axis=-1, keepdims=True) + EPS)
        xhat = xv * r
        h_ref[...] = (xhat * gv).astype(BF16)
        dxo_v = dxo_ref[...]
        dxo_b = dxo_v.astype(BF16)
        dxob_ref[...] = dxo_b

        v = proj_ref[:, 0:dc].astype(F32)
        gb = proj_ref[:, dc:2 * dc].astype(F32)
        gc = proj_ref[:, 2 * dc:3 * dc].astype(F32)
        u = proj_ref[:, 3 * dc:4 * dc].astype(F32)
        first = jnp.where(tile > 0, 1.0, 0.0)
        zbuf[0:HALO, :] = halo_ref[:, 2 * dc:3 * dc].astype(F32) * halo_ref[:, 0:dc].astype(F32) * first
        ubuf[0:HALO, :] = halo_ref[:, 3 * dc:4 * dc].astype(F32) * first
        z = gc * v
        zbuf[HALO:HALO + tm, :] = z
        ubuf[HALO:HALO + tm, :] = u
        z1 = zbuf[HALO - 1:HALO - 1 + tm, :]
        z2 = zbuf[HALO - 2:HALO - 2 + tm, :]
        cw = cw_ref[...]
        conv = cw[2:3, :] * z + cw[1:2, :] * z1 + cw[0:1, :] * z2

        dy = _nt(dxo_b, wout[...])
        dya = dy[:, 0:dc]
        dgb = dya * conv
        dconv = dya * gb
        dcbuf[0:tm, :] = dconv
        dz = cw[2:3, :] * dconv + cw[1:2, :] * dcbuf[1:1 + tm, :] + cw[0:1, :] * dcbuf[2:2 + tm, :]
        dgc = dz * v
        dv = dz * gc
        dcw_ref[0:1, :] += jnp.sum(dconv * z2, axis=0, keepdims=True)
        dcw_ref[1:2, :] += jnp.sum(dconv * z1, axis=0, keepdims=True)
        dcw_ref[2:3, :] += jnp.sum(dconv * z, axis=0, keepdims=True)

        dproj_ref[:, 0:dc] = dv.astype(BF16)
        dproj_ref[:, dc:2 * dc] = dgb.astype(BF16)
        dproj_ref[:, 2 * dc:3 * dc] = dgc.astype(BF16)

        row = tile * tm + lax.broadcasted_iota(jnp.int32, (tm, 1), 0)
        for gi, w in enumerate(POOL_WINDOWS):
            cols = slice(gi * gcw, (gi + 1) * gcw)
            pooled, cnt = _pool_parts(u[:, cols], ubuf, cols, w, row, tm)
            pooled_b = pooled.astype(BF16)
            pw_b = pw_ref[gi].astype(BF16)
            dyb = dy[:, dc + gi * gcw:dc + (gi + 1) * gcw]
            q = _nn(pooled_b, pw_b)
            dps_ref[:, cols] += jnp.sum(q * dyb, axis=0, keepdims=True)
            dq = (dyb * ps_ref[:, cols]).astype(BF16)
            dpw_ref[gi] += _tn(pooled_b, dq)
            dpooled = _nt(dq, pw_b)
            ebuf[0:tm, cols] = dpooled / cnt
            du = -dpooled
            for s in range(w):
                du = du + ebuf[s:s + tm, cols]
            dproj_ref[:, 3 * dc + gi * gcw:3 * dc + (gi + 1) * gcw] = du.astype(BF16)

        dh = _nn(dproj_ref[...], win[...])
        dg_ref[...] += jnp.sum(dh * xhat, axis=0, keepdims=True)
        dxh = dh * gv
        dx_ref[...] = dxo_v + r * (dxh - xhat * jnp.mean(dxh * xhat, axis=-1, keepdims=True))
        dcbuf[tm:tm + 8, :] = dcbuf[0:8, :]
        ebuf[tm:tm + HALO, :] = ebuf[0:HALO, :]

    tok = lambda i: (n_tiles - 1 - i, 0)
    halo = lambda i: (jnp.maximum((n_tiles - 1 - i) * hb - 1, 0), 0)
    one = lambda i: (0, 0)
    return _launch(
        body, name="mixer_backward", grid=(n_tiles,),
        in_specs=[pl.BlockSpec((tm, d), tok), pl.BlockSpec((tm, d), tok), pl.BlockSpec((1, d), one),
                  pl.BlockSpec((tm, 4 * dc), tok), pl.BlockSpec((HALO, 4 * dc), halo), HBM_SPEC, HBM_SPEC,
                  pl.BlockSpec(conv_w.shape, one), pl.BlockSpec(pool_w.shape, lambda i: (0, 0, 0)), pl.BlockSpec((1, dc), one)],
        out_specs=[pl.BlockSpec((tm, d), tok), pl.BlockSpec((tm, 4 * dc), tok), pl.BlockSpec((tm, d), tok), pl.BlockSpec((tm, d), tok),
                   pl.BlockSpec((1, d), one), pl.BlockSpec(conv_w.shape, one), pl.BlockSpec((1, dc), one),
                   pl.BlockSpec(pool_w.shape, lambda i: (0, 0, 0))],
        out_shape=[jax.ShapeDtypeStruct((t, d), F32), jax.ShapeDtypeStruct((t, 4 * dc), BF16), jax.ShapeDtypeStruct((t, d), BF16),
                   jax.ShapeDtypeStruct((t, d), BF16), jax.ShapeDtypeStruct((1, d), F32), jax.ShapeDtypeStruct(conv_w.shape, F32),
                   jax.ShapeDtypeStruct((1, dc), F32), jax.ShapeDtypeStruct(pool_w.shape, F32)],
        scratch_shapes=[pltpu.VMEM((4 * dc, d), BF16), pltpu.VMEM((2 * dc, d), BF16),
                        pltpu.VMEM((tm + HALO, dc), F32), pltpu.VMEM((tm + HALO, dc), F32),
                        pltpu.VMEM((tm + 8, dc), F32), pltpu.VMEM((tm + HALO, dc), F32), pltpu.SemaphoreType.DMA((1 + N_CHIPS,))],
        args=(dxo, x, g, proj, proj, win_t, wout_x, conv_w, pool_w, pool_scale), cargo=cargo)


def _adam_update(w, gv, m, v):
    m_new = ADAM_B1 * m + (1.0 - ADAM_B1) * gv
    v_new = ADAM_B2 * v + (1.0 - ADAM_B2) * (gv * gv)
    m_hat = m_new / (1.0 - ADAM_B1 ** ADAM_STEP)
    v_hat = v_new / (1.0 - ADAM_B2 ** ADAM_STEP)
    return -ADAM_LR * (m_hat / (jnp.sqrt(v_hat) + ADAM_EPS) + ADAM_WD * w), m_new, v_new


def _adamw(w, grad, m, v, name):
    rows, cols = w.shape
    br = _row_block(rows, 256) if rows >= 8 else rows

    def body(w_ref, g_ref, m_ref, v_ref, d_ref, mo_ref, vo_ref):
        d_ref[...], mo_ref[...], vo_ref[...] = _adam_update(w_ref[...], g_ref[...], m_ref[...], v_ref[...])

    blk = pl.BlockSpec((br, cols), lambda i: (i, 0))
    return pl.pallas_call(
        body, name=name,
        out_shape=[jax.ShapeDtypeStruct((rows, cols), F32)] * 3,
        grid=(rows // br,), in_specs=[blk] * 4, out_specs=[blk] * 3,
        compiler_params=pltpu.CompilerParams(dimension_semantics=("parallel",)),
    )(w, grad, m, v)


def _adamw_transposed(w, grad_t, m, v, name):
    _, rows, cols = w.shape
    br = 256 if rows % 256 == 0 else rows

    def body(w_ref, gt_ref, m_ref, v_ref, g_ref, d_ref, mo_ref, vo_ref):
        gv = gt_ref[...].T
        g_ref[...] = gv
        d_ref[...], mo_ref[...], vo_ref[...] = _adam_update(w_ref[...], gv, m_ref[...], v_ref[...])

    blk = pl.BlockSpec((None, br, cols), lambda i: (0, i, 0))
    return pl.pallas_call(
        body, name=name,
        out_shape=[jax.ShapeDtypeStruct((1, rows, cols), F32)] * 4,
        grid=(rows // br,), in_specs=[blk, pl.BlockSpec((cols, br), lambda i: (0, i)), blk, blk], out_specs=[blk] * 4,
        compiler_params=pltpu.CompilerParams(dimension_semantics=("parallel",)),
    )(w, grad_t, m, v)


def _f32_rows_as_bf16(a, rows, cols):
    bits = lax.bitcast_convert_type(a, BF16).reshape(a.shape[0], 2 * a.shape[1])
    return jnp.pad(bits, ((0, rows - bits.shape[0]), (0, cols - bits.shape[1])))


def kernel(x, norm_ffn1, ffn1_w_gate, ffn1_w_up, ffn1_w_down, norm_mix, w_in, conv_w, pool_w, pool_scale, w_out, norm_ffn2, ffn2_w_gate, ffn2_w_up, ffn2_w_down, norm_final, loss_target, m_norm_ffn1, m_ffn1_w_gate, m_ffn1_w_up, m_ffn1_w_down, m_norm_mix, m_w_in, m_conv_w, m_pool_w, m_pool_scale, m_w_out, m_norm_ffn2, m_ffn2_w_gate, m_ffn2_w_up, m_ffn2_w_down, m_norm_final, v_norm_ffn1, v_ffn1_w_gate, v_ffn1_w_up, v_ffn1_w_down, v_norm_mix, v_w_in, v_conv_w, v_pool_w, v_pool_scale, v_w_out, v_norm_ffn2, v_ffn2_w_gate, v_ffn2_w_up, v_ffn2_w_down, v_norm_final):
    weights = dict(norm_ffn1=norm_ffn1, ffn1_w_gate=ffn1_w_gate, ffn1_w_up=ffn1_w_up, ffn1_w_down=ffn1_w_down, norm_mix=norm_mix,
                   w_in=w_in, conv_w=conv_w, pool_w=pool_w, pool_scale=pool_scale, w_out=w_out, norm_ffn2=norm_ffn2,
                   ffn2_w_gate=ffn2_w_gate, ffn2_w_up=ffn2_w_up, ffn2_w_down=ffn2_w_down, norm_final=norm_final)
    first_m = dict(norm_ffn1=m_norm_ffn1, ffn1_w_gate=m_ffn1_w_gate, ffn1_w_up=m_ffn1_w_up, ffn1_w_down=m_ffn1_w_down,
                   norm_mix=m_norm_mix, w_in=m_w_in, conv_w=m_conv_w, pool_w=m_pool_w, pool_scale=m_pool_scale, w_out=m_w_out,
                   norm_ffn2=m_norm_ffn2, ffn2_w_gate=m_ffn2_w_gate, ffn2_w_up=m_ffn2_w_up, ffn2_w_down=m_ffn2_w_down,
                   norm_final=m_norm_final)
    second_m = dict(norm_ffn1=v_norm_ffn1, ffn1_w_gate=v_ffn1_w_gate, ffn1_w_up=v_ffn1_w_up, ffn1_w_down=v_ffn1_w_down,
                    norm_mix=v_norm_mix, w_in=v_w_in, conv_w=v_conv_w, pool_w=v_pool_w, pool_scale=v_pool_scale, w_out=v_w_out,
                    norm_ffn2=v_norm_ffn2, ffn2_w_gate=v_ffn2_w_gate, ffn2_w_up=v_ffn2_w_up, ffn2_w_down=v_ffn2_w_down,
                    norm_final=v_norm_final)
    names = list(weights)

    xs = x[0]
    tgt = loss_target[0]
    t, d = xs.shape
    dc = pool_scale.shape[1]
    cx, cy, cc = _my_place()
    chip = 2 * cx + cy
    place = jnp.stack([chip, cc]).astype(jnp.int32)

    conv_rows = 32
    wout_x = jnp.concatenate([w_out[0].astype(BF16), _f32_rows_as_bf16(conv_w[0], conv_rows, d)], axis=0)
    wg2_shard = ffn2_w_gate[0].T.astype(BF16)
    half_rows = wg2_shard.shape[0] // 2

    g1, gm, g2 = norm_ffn1, norm_mix, norm_ffn2
    gf = norm_final.reshape(1, d)
    pw = pool_w[0]

    wg1, wu1 = [[w] for w in _run_cargo(_gather_cargo([ffn1_w_gate[0].T.astype(BF16), ffn1_w_up[0].T.astype(BF16)]), "gather_ffn1")]
    (a1, b1, s1), [(wd1, win_t)] = _ffn_up(xs, g1, wg1, wu1, "ffn1_up", [_gather_cargo([ffn1_w_down[0].astype(BF16), w_in[0].T.astype(BF16)])])
    wd1 = [wd1]
    (x1,), [(wout_g, wg2_a)] = _ffn_down(xs, s1, wd1, "ffn1_down", [_gather_cargo([wout_x, wg2_shard[:half_rows]])])
    wo_rows = w_out.shape[1]
    cshard = conv_w.shape[2]
    conv_bits = wout_g.reshape(N_CHIPS, wo_rows + conv_rows, d)[:, wo_rows:wo_rows + conv_w.shape[1], :2 * cshard]
    conv_full = lax.bitcast_convert_type(conv_bits.reshape(N_CHIPS, conv_w.shape[1], cshard, 2), F32)
    conv_full = jnp.transpose(conv_full, (1, 0, 2)).reshape(conv_w.shape[1], N_CHIPS * cshard)
    (x2, proj, ymix), [(wg2_b, wu2)] = _mixer_forward(
        x1, gm, win_t, wout_g, conv_full, pw, pool_scale, [_gather_cargo([wg2_shard[half_rows:], ffn2_w_up[0].T.astype(BF16)])])
    wg2, wu2 = [wg2_a, wg2_b], [wu2]
    (a2, b2, s2), [(wd2,)] = _ffn_up(x2, g2, wg2, wu2, "ffn2_up", [_gather_cargo([ffn2_w_down[0].astype(BF16)])])
    wd2 = [wd2]
    (dx3, sq_cols, dgf), _ = _ffn_down(x2, s2, wd2, "ffn2_down", loss_head=(gf, tgt))

    (dx2, da2, db2, h3, do2, dg2), _ = _ffn_backward(dx3, x2, g2, a2, b2, wg2, wu2, wd2, "ffn2_backward")
    p_wg2, _ = _weight_grad(da2, h3, "ffn2_gate_grad")
    p_wu2, [(x_wg2,)] = _weight_grad(db2, h3, "ffn2_up_grad", [_exchange_cargo([p_wg2])])
    p_wd2, [(x_wu2,)] = _weight_grad(s2, do2, "ffn2_down_grad", [_exchange_cargo([p_wu2])])

    (dx1, dproj, h2, dx2b, dgm, dcw, dps, dpw), [(x_wd2,)] = _mixer_backward(
        dx2, x1, gm, proj, win_t, wout_g, conv_full, pw, pool_scale, [_exchange_cargo([p_wd2])])

    (dx0, da1, db1, h1, do1, dg1), _ = _ffn_backward(dx1, xs, g1, a1, b1, wg1, wu1, wd1, "ffn1_backward")

    npw = pw.size // d
    head = [dg1, dgm, dg2, dgf, jnp.pad(dps, ((0, 0), (0, d - dc))), jnp.pad(dcw, ((0, 0), (0, d - dc))), sq_cols]
    n_head = sum(h.shape[0] for h in head)
    base = -(-n_head // 8) * 8
    pack = jnp.concatenate(head + [jnp.zeros((base - n_head, d), F32), dpw.reshape(npw, d)], axis=0)

    p_wg1, [(packs,)] = _weight_grad(da1, h1, "ffn1_gate_grad", [_all_gather_small_cargo(pack)])
    p_wu1, [(x_wg1,)] = _weight_grad(db1, h1, "ffn1_up_grad", [_exchange_cargo([p_wg1])])
    p_wd1, [(x_wu1,)] = _weight_grad(s1, do1, "ffn1_down_grad", [_exchange_cargo([p_wu1])])
    p_win, [(x_wd1,)] = _weight_grad(dproj, h2, "w_in_grad", [_exchange_cargo([p_wd1])])
    p_wout, [(x_win,)] = _weight_grad(ymix, dx2b, "w_out_grad", [_exchange_cargo([p_win])])
    x_wout, = _run_cargo(_exchange_cargo([p_wout]), "grad_exchange_last")
    small = _sum_by_device(packs)
    loss = jnp.sum(small[n_head - 1]) * (0.5 / d)

    order = ["wg1", "wu1", "wd1", "win", "wout", "wg2", "wu2", "wd2"]
    pairs = dict(wg1=p_wg1, wu1=p_wu1, wd1=p_wd1, win=p_win, wout=p_wout, wg2=p_wg2, wu2=p_wu2, wd2=p_wd2)
    landed = dict(wg1=x_wg1, wu1=x_wu1, wd1=x_wd1, win=x_win, wout=x_wout, wg2=x_wg2, wu2=x_wu2, wd2=x_wd2)
    both = _sibling_share([_chip_sum(pairs[k], landed[k], place, k) for k in order])
    rwg1, rwu1, rwd1, rwin, rwout, rwg2, rwu2, rwd2 = [b.reshape(2 * b.shape[1], b.shape[2]) for b in both]

    grads = {
        "norm_ffn1": small[0:1], "norm_mix": small[1:2], "norm_ffn2": small[2:3], "norm_final": small[3],
        "pool_scale": small[4:5, :dc],
        "conv_w": lax.dynamic_slice_in_dim(small[5:5 + dcw.shape[0], :dc], chip * cshard, cshard, axis=1)[None],
        "pool_w": small[base:].reshape(pool_w.shape),
        "ffn1_w_down": rwd1[None], "w_out": rwout[None], "ffn2_w_down": rwd2[None],
    }
    by_view = {"ffn1_w_gate": rwg1, "ffn1_w_up": rwu1, "ffn2_w_gate": rwg2, "ffn2_w_up": rwu2}

    deltas, new_m, new_v = {}, {}, {}
    for n in names:
        w = weights[n]
        shape = w.shape
        if n == "w_in":
            grads[n], deltas[n], new_m[n], new_v[n] = _adamw_transposed(w, rwin, first_m[n], second_m[n], "adamw_" + n)
            continue
        if n in by_view:
            view = lambda a: jnp.swapaxes(a, 1, 2)[0]
            back = lambda a: jnp.swapaxes(a[None], 1, 2)
            dl, mo, vo = _adamw(view(w), by_view[n], view(first_m[n]), view(second_m[n]), "adamw_" + n)
            grads[n], deltas[n], new_m[n], new_v[n] = back(by_view[n]), back(dl), back(mo), back(vo)
            continue
        as2d = (lambda a: a.reshape(-1, shape[-1]))
        dl, mo, vo = _adamw(as2d(w), as2d(grads[n]), as2d(first_m[n]), as2d(second_m[n]), "adamw_" + n)
        deltas[n], new_m[n], new_v[n] = dl.reshape(shape), mo.reshape(shape), vo.reshape(shape)
        grads[n] = grads[n].reshape(shape)

    return (loss, dx0[None], *[grads[n] for n in names], *[deltas[n] for n in names],
            *[new_m[n] for n in names], *[new_v[n] for n in names])
```

```python
import jax
import jax.numpy as jnp
from jax import lax
from jax.experimental import pallas as pl
from jax.experimental.pallas import tpu as pltpu

F32 = jnp.float32
BF16 = jnp.bfloat16
MESH = pl.DeviceIdType.MESH

EPS = 1e-6
POOL_WINDOWS = (2, 4, 8, 16)
ADAM_LR = 0.001
ADAM_B1 = 0.9
ADAM_B2 = 0.999
ADAM_EPS = 1e-08
ADAM_WD = 0.01
ADAM_STEP = 10

N_CHIPS = 4
N_DEVICES = 8
MXU_COLS_V7X = 256
VMEM_LIMIT = 56 * 1024 * 1024
TM_FFN = 512
TM_MIX = 512
TM_TN = 1024
HALO = 16
FFN_FWD_CHUNKS = 2
FFN_BWD_CHUNKS = 2


def _nt(a, b):
    return lax.dot_general(a, b, (((1,), (1,)), ((), ())), preferred_element_type=F32)


def _tn(a, b):
    return lax.dot_general(a, b, (((0,), (0,)), ((), ())), preferred_element_type=F32)


def _nn(a, b):
    return jnp.dot(a, b, preferred_element_type=F32)


def _sigmoid(a):
    return 1.0 / (1.0 + jnp.exp(-a))


def _feature_chunks(n, parts):
    assert n % MXU_COLS_V7X == 0
    tiles = n // MXU_COLS_V7X
    out, s0 = [], 0
    for p in range(parts):
        sz = (tiles // parts + (1 if p < tiles % parts else 0)) * MXU_COLS_V7X
        if sz:
            out.append((s0, sz))
            s0 += sz
    return out


def _row_block(rows, cap):
    best = 8
    for b in range(8, min(rows, cap) + 1, 8):
        if rows % b == 0:
            best = b
    assert rows % best == 0
    return best


def _my_place():
    return lax.axis_index("x"), lax.axis_index("y"), lax.axis_index("c")


def _other_chips(x, y):
    return [(1 - x, y), (x, 1 - y), (1 - x, 1 - y)]


HBM_SPEC = pl.BlockSpec(memory_space=pltpu.HBM)


class _Cargo:
    def __init__(self, operands, out_shapes, n_sems, phases, when):
        self.operands, self.out_shapes, self.n_sems = list(operands), list(out_shapes), n_sems
        self.phases, self.when = list(phases), list(when)
        assert len(self.phases) == len(self.when) and self.when[0] == 0.0 and self.when[-1] == 1.0


def _launch(body, *, name, grid, in_specs, out_specs, out_shape, scratch_shapes, args, cargo=()):
    params = pltpu.CompilerParams(dimension_semantics=("arbitrary",) * len(grid), vmem_limit_bytes=VMEM_LIMIT)
    cargos = list(cargo)
    c_operands = [op for cg in cargos for op in cg.operands]
    c_shapes = [sh for cg in cargos for sh in cg.out_shapes]
    counts = [len(in_specs), len(c_operands), len(out_shape), len(c_shapes), len(scratch_shapes), 2 * len(cargos)]

    def carrying(*refs):
        groups, pos = [], 0
        for k in counts:
            groups.append(refs[pos:pos + k])
            pos += k
        ins, c_ins, outs, c_outs, scratch, sems = groups
        parts, pi, po = [], 0, 0
        for n, cg in enumerate(cargos):
            parts.append((c_ins[pi:pi + len(cg.operands)], c_outs[po:po + len(cg.out_shapes)], sems[2 * n], sems[2 * n + 1]))
            pi += len(cg.operands)
            po += len(cg.out_shapes)
        step, steps = 0, 1
        for ax, g in enumerate(grid):
            step = step * g + pl.program_id(ax)
            steps *= g
        todo = {}
        for cg, part in zip(cargos, parts):
            for phase, frac in zip(cg.phases[:-1], cg.when[:-1]):
                todo.setdefault(int(round(frac * (steps - 1))), []).append((phase, part))

        for at in sorted(todo):
            @pl.when(step == at)
            def _(at=at):
                for phase, part in todo[at]:
                    phase(*part)

        body(*ins, *outs, *scratch)

        if cargos:
            @pl.when(step == steps - 1)
            def _():
                for cg, part in zip(cargos, parts):
                    cg.phases[-1](*part)

    sems = [pltpu.SemaphoreType.DMA((cg.n_sems,)) for cg in cargos for _ in range(2)]
    outs = pl.pallas_call(
        carrying, name=name, grid=grid,
        in_specs=list(in_specs) + [HBM_SPEC] * counts[1], out_specs=list(out_specs) + [HBM_SPEC] * counts[3],
        out_shape=list(out_shape) + c_shapes, scratch_shapes=list(scratch_shapes) + sems,
        compiler_params=params)(*args, *c_operands)
    own, rest = list(outs[:counts[2]]), list(outs[counts[2]:])
    carried, po = [], 0
    for cg in cargos:
        carried.append(rest[po:po + len(cg.out_shapes)])
        po += len(cg.out_shapes)
    return own, carried


def _run_cargo(cargo, name):
    n_in, n_out = len(cargo.operands), len(cargo.out_shapes)

    def body(*refs):
        c_ins, c_outs, sems = refs[:n_in], refs[n_in:n_in + n_out], refs[n_in + n_out:]
        for phase in cargo.phases:
            phase(c_ins, c_outs, *sems)

    sem = pltpu.SemaphoreType.DMA((cargo.n_sems,))
    return list(pl.pallas_call(body, name=name, out_shape=cargo.out_shapes, in_specs=[HBM_SPEC] * n_in,
                               out_specs=[HBM_SPEC] * n_out, scratch_shapes=[sem, sem])(*cargo.operands))


def _gather_cargo(shards):
    n = len(shards)
    for s in shards:
        assert s.shape[0] % 32 == 0
    slots = 8

    def steps(ins, outs, send_sems, recv_sems):
        x, y, c = _my_place()
        sibling = (x, y, 1 - c)
        over_x, over_y = (1 - x, y, c), (x, 1 - y, c)
        mine, chip_x, chip_y, chip_d = 2 * x + y, 2 * (1 - x) + y, 2 * x + (1 - y), 2 * (1 - x) + (1 - y)

        def rows_of(a, chip_index, half, part=None):
            rps = shards[a].shape[0]
            hr = rps // 2
            first = -(-hr // 32) * 16
            offset, size = {None: (0, hr), 0: (0, first), 1: (first, hr - first)}[part]
            return outs[a].at[pl.ds(pl.multiple_of(chip_index * rps + half * hr + offset, 16), size), :]

        def remote(a, slot, src, dst, to):
            return pltpu.make_async_remote_copy(
                src_ref=src, dst_ref=dst, send_sem=send_sems.at[a * slots + slot], recv_sem=recv_sems.at[a * slots + slot],
                device_id=to, device_id_type=MESH)

        def same_rows(a, slot, rows, to):
            return remote(a, slot, rows, rows, to)

        def own_copy(a):
            rps = shards[a].shape[0]
            return remote(a, 7, ins[a], outs[a].at[pl.ds(pl.multiple_of(mine * rps, 16), rps), :], sibling)

        def my_half(a):
            hr = shards[a].shape[0] // 2
            return ins[a].at[pl.ds(pl.multiple_of(c * hr, 16), hr), :]

        def start():
            for a in range(n):
                own_copy(a).start()
                remote(a, 0, my_half(a), rows_of(a, mine, c), over_x).start()
                remote(a, 1, my_half(a), rows_of(a, mine, c), over_y).start()

        def relay_neighbours():
            for a in range(n):
                same_rows(a, 0, rows_of(a, chip_x, c), over_x).wait_recv()
                same_rows(a, 4, rows_of(a, chip_x, c), sibling).start()
                same_rows(a, 2, rows_of(a, chip_x, c, 0), over_y).start()
                same_rows(a, 1, rows_of(a, chip_y, c), over_y).wait_recv()
                same_rows(a, 5, rows_of(a, chip_y, c), sibling).start()
                same_rows(a, 3, rows_of(a, chip_y, c, 1), over_x).start()

        def relay_diagonal():
            for a in range(n):
                same_rows(a, 2, rows_of(a, chip_d, c, 0), over_y).wait_recv()
                same_rows(a, 3, rows_of(a, chip_d, c, 1), over_x).wait_recv()
                same_rows(a, 6, rows_of(a, chip_d, c), sibling).start()

        def finish():
            for a in range(n):
                for slot, chip_index in ((4, chip_x), (5, chip_y), (6, chip_d)):
                    same_rows(a, slot, rows_of(a, chip_index, 1 - c), sibling).wait_recv()
            for a in range(n):
                remote(a, 0, my_half(a), rows_of(a, mine, c), over_x).wait_send()
                remote(a, 1, my_half(a), rows_of(a, mine, c), over_y).wait_send()
                same_rows(a, 2, rows_of(a, chip_x, c, 0), over_y).wait_send()
                same_rows(a, 3, rows_of(a, chip_y, c, 1), over_x).wait_send()
                for slot, chip_index in ((4, chip_x), (5, chip_y), (6, chip_d)):
                    same_rows(a, slot, rows_of(a, chip_index, c), sibling).wait_send()
                own_copy(a).wait()

        return start, relay_neighbours, relay_diagonal, finish

    phases = [lambda *r, k=k: steps(*r)[k]() for k in range(4)]
    return _Cargo(shards, [jax.ShapeDtypeStruct((N_CHIPS * s.shape[0], s.shape[1]), s.dtype) for s in shards], slots * n,
                  phases, [0.0, 0.6, 0.85, 1.0])


def _exchange_cargo(pairs):
    n = len(pairs)

    def copies(ins, outs, send_sems, recv_sems):
        x, y, c = _my_place()
        return [pltpu.make_async_remote_copy(
            src_ref=ins[a].at[2 * chip[0] + chip[1]], dst_ref=outs[a].at[j],
            send_sem=send_sems.at[3 * a + j], recv_sem=recv_sems.at[3 * a + j], device_id=(*chip, c), device_id_type=MESH)
            for a in range(n) for j, chip in enumerate(_other_chips(x, y))]

    def start(*r):
        for cp in copies(*r):
            cp.start()

    def finish(*r):
        for cp in copies(*r):
            cp.wait()

    return _Cargo(pairs, [jax.ShapeDtypeStruct((3,) + p.shape[1:], p.dtype) for p in pairs], 3 * n, [start, finish], [0.0, 1.0])


def _all_gather_small_cargo(pack):
    rows, cols = pack.shape

    def copies(ins, outs, send_sems, recv_sems):
        x, y, c = _my_place()
        me = 4 * x + 2 * y + c
        remote = []
        for f in range(1, N_DEVICES):
            fx, fy, fc = (f >> 2) & 1, (f >> 1) & 1, f & 1
            to = (1 - x if fx else x, 1 - y if fy else y, 1 - c if fc else c)
            remote.append(pltpu.make_async_remote_copy(
                src_ref=ins[0], dst_ref=outs[0].at[me], send_sem=send_sems.at[f - 1], recv_sem=recv_sems.at[f - 1],
                device_id=to, device_id_type=MESH))
        own = pltpu.make_async_copy(ins[0], outs[0].at[me], send_sems.at[N_DEVICES - 1])
        return remote, own

    def start(*r):
        remote, own = copies(*r)
        own.start()
        for cp in remote:
            cp.start()

    def finish(*r):
        remote, own = copies(*r)
        for cp in remote:
            cp.wait()
        own.wait()

    return _Cargo([pack], [jax.ShapeDtypeStruct((N_DEVICES, rows, cols), F32)], N_DEVICES, [start, finish], [0.0, 1.0])


def _sum_by_device(packs):
    n, rows, cols = packs.shape

    def body(p_ref, o_ref):
        acc = p_ref[0]
        for dev in range(1, n):
            acc = acc + p_ref[dev]
        o_ref[...] = acc

    return pl.pallas_call(body, name="small_grads_sum", out_shape=jax.ShapeDtypeStruct((rows, cols), F32))(packs)


def _chip_sum(pair, got, place, tag):
    _, hr, cols = pair.shape
    br = _row_block(hr, 256)

    def body(k_ref, p_ref, r_ref, o_ref):
        acc = p_ref[...].astype(F32)
        for j in range(3):
            acc = acc + r_ref[j].astype(F32)
        o_ref[...] = acc

    return pl.pallas_call(
        body, name="grad_chip_sum_" + tag,
        out_shape=jax.ShapeDtypeStruct((2, hr, cols), F32),
        grid_spec=pltpu.PrefetchScalarGridSpec(
            num_scalar_prefetch=1, grid=(hr // br,),
            in_specs=[pl.BlockSpec((None, br, cols), lambda r, k_ref: (k_ref[0], r, 0)),
                      pl.BlockSpec((3, br, cols), lambda r, k_ref: (0, r, 0))],
            out_specs=pl.BlockSpec((None, br, cols), lambda r, k_ref: (k_ref[1], r, 0))),
        compiler_params=pltpu.CompilerParams(dimension_semantics=("parallel",)),
    )(place, pair, got)


def _sibling_share(halves):
    n = len(halves)

    def body(*refs):
        outs = refs[n:2 * n]
        send_sems, recv_sems = refs[2 * n:]
        x, y, c = _my_place()
        copies = []
        for a in range(n):
            cp = pltpu.make_async_remote_copy(
                src_ref=outs[a].at[c], dst_ref=outs[a].at[c], send_sem=send_sems.at[a], recv_sem=recv_sems.at[a],
                device_id=(x, y, 1 - c), device_id_type=MESH)
            cp.start()
            copies.append(cp)
        for cp in copies:
            cp.wait()

    return pl.pallas_call(
        body, name="grad_share_sibling",
        out_shape=[jax.ShapeDtypeStruct(h.shape, h.dtype) for h in halves],
        in_specs=[HBM_SPEC] * n, out_specs=[HBM_SPEC] * n,
        input_output_aliases={a: a for a in range(n)},
        scratch_shapes=[pltpu.SemaphoreType.DMA((n,)), pltpu.SemaphoreType.DMA((n,))],
    )(*halves)


def _load_rows(pairs, sems):
    cps = [pltpu.make_async_copy(src, dst, sems.at[j]) for j, (src, dst) in enumerate(pairs)]
    for cp in cps:
        cp.start()
    for cp in cps:
        cp.wait()


def _piece_rows(weights):
    flat = [p for pieces in weights for p in pieces]

    def copies(refs, mats):
        out, n = [], 0
        for pieces, mat in zip(weights, mats):
            rps = sum(p.shape[0] for p in pieces) // N_CHIPS
            off = 0
            for p in pieces:
                r = p.shape[0] // N_CHIPS
                if len(pieces) == 1:
                    out.append((refs[n], mat))
                else:
                    for k in range(N_CHIPS):
                        out.append((refs[n].at[pl.ds(k * r, r), :], mat.at[pl.ds(k * rps + off, r), :]))
                off += r
                n += 1
        return out

    n_copies = sum(1 if len(pieces) == 1 else N_CHIPS * len(pieces) for pieces in weights)
    return flat, copies, n_copies


def _loss_head(xv, gv, tv):
    d = xv.shape[-1]
    r = lax.rsqrt(jnp.mean(xv * xv, axis=-1, keepdims=True) + EPS)
    xhat = xv * r
    err = xhat * gv - tv
    dy = err * (1.0 / d)
    dxh = dy * gv
    dx = r * (dxh - xhat * jnp.mean(dxh * xhat, axis=-1, keepdims=True))
    return dx, jnp.sum(err * err, axis=0, keepdims=True), jnp.sum(dy * xhat, axis=0, keepdims=True)


def _ffn_up(x, g, wg_t, wu_t, name, cargo=()):
    t, d = x.shape
    f = sum(p.shape[0] for p in wg_t)
    tm = min(TM_FFN, t)
    chunks = _feature_chunks(f, FFN_FWD_CHUNKS)
    flat, copies, n_copies = _piece_rows([wg_t, wu_t])
    nw = len(flat)

    def body(x_ref, g_ref, *rest):
        w_hbm, (a_ref, b_ref, s_ref, wg, wu, sems) = rest[:nw], rest[nw:]

        @pl.when(pl.program_id(0) == 0)
        def _():
            _load_rows(copies(w_hbm, [wg, wu]), sems)

        xv = x_ref[...]
        r = lax.rsqrt(jnp.mean(xv * xv, axis=-1, keepdims=True) + EPS)
        h = (xv * r * g_ref[...]).astype(BF16)
        for s0, sz in chunks:
            a = _nt(h, wg[s0:s0 + sz, :])
            b = _nt(h, wu[s0:s0 + sz, :])
            a_ref[:, s0:s0 + sz] = a.astype(BF16)
            b_ref[:, s0:s0 + sz] = b.astype(BF16)
            s_ref[:, s0:s0 + sz] = (a * _sigmoid(a) * b).astype(BF16)

    tok = lambda i: (i, 0)
    wide = pl.BlockSpec((tm, f), tok)
    return _launch(
        body, name=name, grid=(t // tm,),
        in_specs=[pl.BlockSpec((tm, d), tok), pl.BlockSpec((1, d), lambda i: (0, 0))] + [HBM_SPEC] * nw,
        out_specs=[wide, wide, wide], out_shape=[jax.ShapeDtypeStruct((t, f), BF16)] * 3,
        scratch_shapes=[pltpu.VMEM((f, d), BF16), pltpu.VMEM((f, d), BF16), pltpu.SemaphoreType.DMA((n_copies,))],
        args=(x, g, *flat), cargo=cargo)


def _ffn_down(x, s, wd, name, cargo=(), loss_head=None):
    t, d = x.shape
    f = s.shape[1]
    tm = min(TM_FFN, t)
    flat, copies, n_copies = _piece_rows([wd])
    nw = len(flat)
    nl = 2 if loss_head else 0

    def body(x_ref, s_ref, *rest):
        head, w_hbm = rest[:nl], rest[nl:nl + nw]
        xo_ref = rest[nl + nw]
        sums, (wdn, sems) = rest[nl + nw + 1:nl + nw + 1 + nl], rest[nl + nw + 1 + nl:]

        @pl.when(pl.program_id(0) == 0)
        def _():
            _load_rows(copies(w_hbm, [wdn]), sems)
            for sum_ref in sums:
                sum_ref[...] = jnp.zeros_like(sum_ref)

        xo = x_ref[...] + 0.5 * _nn(s_ref[...], wdn[...])
        if loss_head:
            dx, sq, dgf = _loss_head(xo, head[0][...], head[1][...])
            xo_ref[...] = dx
            sums[0][...] += sq
            sums[1][...] += dgf
        else:
            xo_ref[...] = xo

    tok = lambda i: (i, 0)
    one = lambda i: (0, 0)
    return _launch(
        body, name=name, grid=(t // tm,),
        in_specs=[pl.BlockSpec((tm, d), tok), pl.BlockSpec((tm, f), tok)]
        + ([pl.BlockSpec((1, d), one), pl.BlockSpec((tm, d), tok)] if loss_head else []) + [HBM_SPEC] * nw,
        out_specs=[pl.BlockSpec((tm, d), tok)] + [pl.BlockSpec((1, d), one)] * nl,
        out_shape=[jax.ShapeDtypeStruct((t, d), F32)] + [jax.ShapeDtypeStruct((1, d), F32)] * nl,
        scratch_shapes=[pltpu.VMEM((f, d), BF16), pltpu.SemaphoreType.DMA((n_copies,))],
        args=(x, s, *(loss_head or ()), *flat), cargo=cargo)


def _ffn_backward(dxo, x, g, a, b, wg_t, wu_t, wd, name, cargo=()):
    t, d = x.shape
    f = sum(p.shape[0] for p in wd)
    tm = min(TM_FFN // 2, t)
    chunks = _feature_chunks(f, FFN_BWD_CHUNKS)
    flat, copies, n_copies = _piece_rows([wg_t, wu_t, wd])
    nw = len(flat)

    def body(dxo_ref, x_ref, g_ref, a_ref, b_ref, *rest):
        w_hbm, (dx_ref, da_ref, db_ref, h_ref, do_ref, dg_ref, wg, wu, wdn, sems) = rest[:nw], rest[nw:]

        @pl.when(pl.program_id(0) == 0)
        def _():
            _load_rows(copies(w_hbm, [wg, wu, wdn]), sems)
            dg_ref[...] = jnp.zeros_like(dg_ref)

        xv = x_ref[...]
        gv = g_ref[...]
        r = lax.rsqrt(jnp.mean(xv * xv, axis=-1, keepdims=True) + EPS)
        xhat = xv * r
        h_ref[...] = (xhat * gv).astype(BF16)
        dxo_v = dxo_ref[...]
        dout = (0.5 * dxo_v).astype(BF16)
        do_ref[...] = dout
        dh = jnp.zeros((tm, d), F32)
        for s0, sz in chunks:
            ds = _nt(dout, wdn[s0:s0 + sz, :])
            av = a_ref[:, s0:s0 + sz].astype(F32)
            bv = b_ref[:, s0:s0 + sz].astype(F32)
            sig = _sigmoid(av)
            silu = av * sig
            da =(ds * bv * (sig * (1.0 + av * (1.0 - sig)))).astype(BF16)
            db = (ds * silu).astype(BF16)
            da_ref[:, s0:s0 + sz] = da
            db_ref[:, s0:s0 + sz] = db
            dh = dh + _nn(da, wg[s0:s0 + sz, :]) + _nn(db, wu[s0:s0 + sz, :])
        dg_ref[...] += jnp.sum(dh * xhat, axis=0, keepdims=True)
        dxh = dh * gv
        dx_ref[...] = dxo_v + r * (dxh - xhat * jnp.mean(dxh * xhat, axis=-1, keepdims=True))

    tok = lambda i: (i, 0)
    one = lambda i: (0, 0)
    return _launch(
        body, name=name, grid=(t // tm,),
        in_specs=[pl.BlockSpec((tm, d), tok), pl.BlockSpec((tm, d), tok), pl.BlockSpec((1, d), one),
                  pl.BlockSpec((tm, f), tok), pl.BlockSpec((tm, f), tok)] + [HBM_SPEC] * nw,
        out_specs=[pl.BlockSpec((tm, d), tok), pl.BlockSpec((tm, f), tok), pl.BlockSpec((tm, f), tok),
                   pl.BlockSpec((tm, d), tok), pl.BlockSpec((tm, d), tok), pl.BlockSpec((1, d), one)],
        out_shape=[jax.ShapeDtypeStruct((t, d), F32), jax.ShapeDtypeStruct((t, f), BF16), jax.ShapeDtypeStruct((t, f), BF16),
                   jax.ShapeDtypeStruct((t, d), BF16), jax.ShapeDtypeStruct((t, d), BF16), jax.ShapeDtypeStruct((1, d), F32)],
        scratch_shapes=[pltpu.VMEM((f, d), BF16), pltpu.VMEM((f, d), BF16), pltpu.VMEM((f, d), BF16), pltpu.SemaphoreType.DMA((n_copies,))],
        args=(dxo, x, g, a, b, *flat), cargo=cargo)


def _weight_grad(lhs, rhs, name, cargo=()):
    t, m = lhs.shape
    d = rhs.shape[1]
    tm = min(TM_TN, t)
    nt = t // tm
    nj = 1
    bm = m // nj
    cpb = N_CHIPS // nj
    rps = m // N_CHIPS
    hr = rps // 2
    assert hr % 16 == 0

    def body(l_ref, r_ref, o_ref, acc, stage, recv, send_sems, recv_sems):
        j = pl.program_id(0)
        i = pl.program_id(1)
        @pl.when(i == 0)
        def _():
            acc[...] = jnp.zeros_like(acc)

        acc[...] += _tn(l_ref[...], r_ref[...])

        def pair_sum(jj):
            x, y, c = _my_place()
            copies = []
            for q in range(cpb):
                slot = jj * cpb + q
                stage[slot] = acc[pl.ds(pl.multiple_of(q * rps + (1 - c) * hr, 16), hr), :].astype(BF16)
                cp = pltpu.make_async_remote_copy(
                    src_ref=stage.at[slot], dst_ref=recv.at[slot], send_sem=send_sems.at[slot], recv_sem=recv_sems.at[slot],
                    device_id=(x, y, 1 - c), device_id_type=MESH)
                cp.start()
                copies.append(cp)
            for q, cp in enumerate(copies):
                cp.wait_recv()
                mine = acc[pl.ds(pl.multiple_of(q * rps + c * hr, 16), hr), :]
                o_ref[q] = (mine + recv[jj * cpb + q].astype(F32)).astype(BF16)
            for cp in copies:
                cp.wait_send()

        for jj in range(nj):
            @pl.when(jnp.logical_and(i == nt - 1, j == jj))
            def _():
                pair_sum(jj)

    outs, carried = _launch(
        body, name=name, grid=(nj, nt),
        in_specs=[pl.BlockSpec((tm, bm), lambda j, i: (i, j)), pl.BlockSpec((tm, d), lambda j, i: (i, 0))],
        out_specs=[pl.BlockSpec((cpb, hr, d), lambda j, i: (j, 0, 0))],
        out_shape=[jax.ShapeDtypeStruct((N_CHIPS, hr, d), BF16)],
        scratch_shapes=[pltpu.VMEM((bm, d), F32), pltpu.VMEM((N_CHIPS, hr, d), BF16), pltpu.VMEM((N_CHIPS, hr, d), BF16),
                        pltpu.SemaphoreType.DMA((N_CHIPS,)), pltpu.SemaphoreType.DMA((N_CHIPS,))],
        args=(lhs, rhs), cargo=cargo)
    return outs[0], carried


def _pool_parts(u_cols, ubuf, cols, w, row, tm):
    ws = u_cols
    for s in range(1, w):
        ws = ws + ubuf[HALO - s:HALO - s + tm, cols]
    cnt = jnp.minimum(row + 1, w).astype(F32)
    return ws / cnt - u_cols, cnt


def _mixer_forward(x, g, win_t, wout_x, conv_w, pool_w, pool_scale, cargo=()):
    t, d = x.shape
    dc = win_t.shape[0] // 4
    gcw = dc // len(POOL_WINDOWS)
    wo_rows = d // N_CHIPS
    wo_stride = wout_x.shape[0] // N_CHIPS
    tm = min(TM_MIX, t)

    def body(x_ref, g_ref, win_hbm, wout_hbm, cw_ref, pw_ref, ps_ref, xo_ref, proj_ref, y_ref,
             win, wout, zbuf, ubuf, sems):
        i = pl.program_id(0)

        @pl.when(i == 0)
        def _():
            pairs = [(win_hbm, win)]
            for k in range(N_CHIPS):
                pairs.append((wout_hbm.at[pl.ds(k * wo_stride, wo_rows), :], wout.at[pl.ds(k * wo_rows, wo_rows), :]))
            _load_rows(pairs, sems)
            zbuf[0:8, :] = jnp.zeros((8, dc), F32)
            ubuf[0:HALO, :] = jnp.zeros((HALO, dc), F32)

        xv = x_ref[...]
        r = lax.rsqrt(jnp.mean(xv * xv, axis=-1, keepdims=True) + EPS)
        h = (xv * r * g_ref[...]).astype(BF16)
        v = _nt(h, win[0:dc, :])
        gb = _nt(h, win[dc:2 * dc, :])
        gc = _nt(h, win[2 * dc:3 * dc, :])
        u = _nt(h, win[3 * dc:4 * dc, :])
        proj_ref[:, 0:dc] = v.astype(BF16)
        proj_ref[:, dc:2 * dc] = gb.astype(BF16)
        proj_ref[:, 2 * dc:3 * dc] = gc.astype(BF16)
        proj_ref[:, 3 * dc:4 * dc] = u.astype(BF16)

        z = gc * v
        zbuf[8:8 + tm, :] = z
        cw = cw_ref[...]
        conv = cw[2:3, :] * z + cw[1:2, :] * zbuf[7:7 + tm, :] + cw[0:1, :] * zbuf[6:6 + tm, :]
        y_ref[:, 0:dc] = (gb * conv).astype(BF16)

        ubuf[HALO:HALO + tm, :] = u
        row = i * tm + lax.broadcasted_iota(jnp.int32, (tm, 1), 0)
        for gi, w in enumerate(POOL_WINDOWS):
            cols = slice(gi * gcw, (gi + 1) * gcw)
            pooled, _ = _pool_parts(u[:, cols], ubuf, cols, w, row, tm)
            yb = _nn(pooled.astype(BF16), pw_ref[gi].astype(BF16)) * ps_ref[:, cols]
            y_ref[:, dc + gi * gcw:dc + (gi + 1) * gcw] = yb.astype(BF16)

        xo_ref[...] = xv + _nn(y_ref[...], wout[...])
        zbuf[0:8, :] = zbuf[tm:tm + 8, :]
        ubuf[0:HALO, :] = ubuf[tm:tm + HALO, :]

    tok = lambda i: (i, 0)
    one = lambda i: (0, 0)
    return _launch(
        body, name="mixer_forward", grid=(t // tm,),
        in_specs=[pl.BlockSpec((tm, d), tok), pl.BlockSpec((1, d), one), HBM_SPEC, HBM_SPEC,
                  pl.BlockSpec(conv_w.shape, one), pl.BlockSpec(pool_w.shape, lambda i: (0, 0, 0)), pl.BlockSpec((1, dc), one)],
        out_specs=[pl.BlockSpec((tm, d), tok), pl.BlockSpec((tm, 4 * dc), tok), pl.BlockSpec((tm, 2 * dc), tok)],
        out_shape=[jax.ShapeDtypeStruct((t, d), F32), jax.ShapeDtypeStruct((t, 4 * dc), BF16), jax.ShapeDtypeStruct((t, 2 * dc), BF16)],
        scratch_shapes=[pltpu.VMEM((4 * dc, d), BF16), pltpu.VMEM((2 * dc, d), BF16),
                        pltpu.VMEM((tm + 8, dc), F32), pltpu.VMEM((tm + HALO, dc), F32), pltpu.SemaphoreType.DMA((1 + N_CHIPS,))],
        args=(x, g, win_t, wout_x, conv_w, pool_w, pool_scale), cargo=cargo)


def _mixer_backward(dxo, x, g, proj, win_t, wout_x, conv_w, pool_w, pool_scale, cargo=()):
    t, d = x.shape
    dc = win_t.shape[0] // 4
    ng = len(POOL_WINDOWS)
    gcw = dc // ng
    wo_rows = d // N_CHIPS
    wo_stride = wout_x.shape[0] // N_CHIPS
    tm = min(TM_MIX, t)
    n_tiles = t // tm
    hb = tm // HALO

    def body(dxo_ref, x_ref, g_ref, proj_ref, halo_ref, win_hbm, wout_hbm, cw_ref, pw_ref, ps_ref,
             dx_ref, dproj_ref, h_ref, dxob_ref, dg_ref, dcw_ref, dps_ref, dpw_ref,
             win, wout, zbuf, ubuf, dcbuf, ebuf, sems):
        i = pl.program_id(0)
        tile = n_tiles - 1 - i

        @pl.when(i == 0)
        def _():
            pairs = [(win_hbm, win)]
            for k in range(N_CHIPS):
                pairs.append((wout_hbm.at[pl.ds(k * wo_stride, wo_rows), :], wout.at[pl.ds(k * wo_rows, wo_rows), :]))
            _load_rows(pairs, sems)
            dcbuf[tm:tm + 8, :] = jnp.zeros((8, dc), F32)
            ebuf[tm:tm + HALO, :] = jnp.zeros((HALO, dc), F32)
            dg_ref[...] = jnp.zeros_like(dg_ref)
            dcw_ref[...] = jnp.zeros_like(dcw_ref)
            dps_ref[...] = jnp.zeros_like(dps_ref)
            dpw_ref[...] = jnp.zeros_like(dpw_ref)

        xv = x_ref[...]
        gv = g_ref[...]
        r = lax.rsqrt(jnp.mean(xv * xv, axis=-1, keepdims=True) + EPS)
        xhat = xv * r
        h_ref[...] = (xhat * gv).astype(BF16)
        dxo_v = dxo_ref[...]
        dxo_b = dxo_v.astype(BF16)
        dxob_ref[...] = dxo_b

        v = proj_ref[:, 0:dc].astype(F32)
        gb = proj_ref[:, dc:2 * dc].astype(F32)
        gc = proj_ref[:, 2 * dc:3 * dc].astype(F32)
        u = proj_ref[:, 3 * dc:4 * dc].astype(F32)
        first = jnp.where(tile > 0, 1.0, 0.0)
        zbuf[0:HALO, :] = halo_ref[:, 2 * dc:3 * dc].astype(F32) * halo_ref[:, 0:dc].astype(F32) * first
        ubuf[0:HALO, :] = halo_ref[:, 3 * dc:4 * dc].astype(F32) * first
        z = gc * v
        zbuf[HALO:HALO + tm, :] = z
        ubuf[HALO:HALO + tm, :] = u
        z1 = zbuf[HALO - 1:HALO - 1 + tm, :]
        z2 = zbuf[HALO - 2:HALO - 2 + tm, :]
        cw = cw_ref[...]
        conv = cw[2:3, :] * z + cw[1:2, :] * z1 + cw[0:1, :] * z2

        dy = _nt(dxo_b, wout[...])
        dya = dy[:, 0:dc]
        dgb = dya * conv
        dconv = dya * gb
        dcbuf[0:tm, :] = dconv
        dz = cw[2:3, :] * dconv + cw[1:2, :] * dcbuf[1:1 + tm, :] + cw[0:1, :] * dcbuf[2:2 + tm, :]
        dgc = dz * v
        dv = dz * gc
        dcw_ref[0:1, :] += jnp.sum(dconv * z2, axis=0, keepdims=True)
        dcw_ref[1:2, :] += jnp.sum(dconv * z1, axis=0, keepdims=True)
        dcw_ref[2:3, :] += jnp.sum(dconv * z, axis=0, keepdims=True)

        dproj_ref[:, 0:dc] = dv.astype(BF16)
        dproj_ref[:, dc:2 * dc] = dgb.astype(BF16)
        dproj_ref[:, 2 * dc:3 * dc] = dgc.astype(BF16)

        row = tile * tm + lax.broadcasted_iota(jnp.int32, (tm, 1), 0)
        for gi, w in enumerate(POOL_WINDOWS):
            cols = slice(gi * gcw, (gi + 1) * gcw)
            pooled, cnt = _pool_parts(u[:, cols], ubuf, cols, w, row, tm)
            pooled_b = pooled.astype(BF16)
            pw_b = pw_ref[gi].astype(BF16)
            dyb = dy[:, dc + gi * gcw:dc + (gi + 1) * gcw]
            q = _nn(pooled_b, pw_b)
            dps_ref[:, cols] += jnp.sum(q * dyb, axis=0, keepdims=True)
            dq = (dyb * ps_ref[:, cols]).astype(BF16)
            dpw_ref[gi] += _tn(pooled_b, dq)
            dpooled = _nt(dq, pw_b)
            ebuf[0:tm, cols] = dpooled / cnt
            du = -dpooled
            for s in range(w):
                du = du + ebuf[s:s + tm, cols]
            dproj_ref[:, 3 * dc + gi * gcw:3 * dc + (gi + 1) * gcw] = du.astype(BF16)

        dh = _nn(dproj_ref[...], win[...])
        dg_ref[...] += jnp.sum(dh * xhat, axis=0, keepdims=True)
        dxh = dh * gv
        dx_ref[...] = dxo_v + r * (dxh - xhat * jnp.mean(dxh * xhat, axis=-1, keepdims=True))
        dcbuf[tm:tm + 8, :] = dcbuf[0:8, :]
        ebuf[tm:tm + HALO, :] = ebuf[0:HALO, :]

    tok = lambda i: (n_tiles - 1 - i, 0)
    halo = lambda i: (jnp.maximum((n_tiles - 1 - i) * hb - 1, 0), 0)
    one = lambda i: (0, 0)
    return _launch(
        body, name="mixer_backward", grid=(n_tiles,),
        in_specs=[pl.BlockSpec((tm, d), tok), pl.BlockSpec((tm, d), tok), pl.BlockSpec((1, d), one),
                  pl.BlockSpec((tm, 4 * dc), tok), pl.BlockSpec((HALO, 4 * dc), halo), HBM_SPEC, HBM_SPEC,
                  pl.BlockSpec(conv_w.shape, one), pl.BlockSpec(pool_w.shape, lambda i: (0, 0, 0)), pl.BlockSpec((1, dc), one)],
        out_specs=[pl.BlockSpec((tm, d), tok), pl.BlockSpec((tm, 4 * dc), tok), pl.BlockSpec((tm, d), tok), pl.BlockSpec((tm, d), tok),
                   pl.BlockSpec((1, d), one), pl.BlockSpec(conv_w.shape, one), pl.BlockSpec((1, dc), one),
                   pl.BlockSpec(pool_w.shape, lambda i: (0, 0, 0))],
        out_shape=[jax.ShapeDtypeStruct((t, d), F32), jax.ShapeDtypeStruct((t, 4 * dc), BF16), jax.ShapeDtypeStruct((t, d), BF16),
                   jax.ShapeDtypeStruct((t, d), BF16), jax.ShapeDtypeStruct((1, d), F32), jax.ShapeDtypeStruct(conv_w.shape, F32),
                   jax.ShapeDtypeStruct((1, dc), F32), jax.ShapeDtypeStruct(pool_w.shape, F32)],
        scratch_shapes=[pltpu.VMEM((4 * dc, d), BF16), pltpu.VMEM((2 * dc, d), BF16),
                        pltpu.VMEM((tm + HALO, dc), F32), pltpu.VMEM((tm + HALO, dc), F32),
                        pltpu.VMEM((tm + 8, dc), F32), pltpu.VMEM((tm + HALO, dc), F32), pltpu.SemaphoreType.DMA((1 + N_CHIPS,))],
        args=(dxo, x, g, proj, proj, win_t, wout_x, conv_w, pool_w, pool_scale), cargo=cargo)


def _adam_update(w, gv, m, v):
    m_new = ADAM_B1 * m + (1.0 - ADAM_B1) * gv
    v_new = ADAM_B2 * v + (1.0 - ADAM_B2) * (gv * gv)
    m_hat = m_new / (1.0 - ADAM_B1 ** ADAM_STEP)
    v_hat = v_new / (1.0 - ADAM_B2 ** ADAM_STEP)
    return -ADAM_LR * (m_hat / (jnp.sqrt(v_hat) + ADAM_EPS) + ADAM_WD * w), m_new, v_new


def _adamw(w, grad, m, v, name):
    rows, cols = w.shape
    br = _row_block(rows, 256) if rows >= 8 else rows

    def body(w_ref, g_ref, m_ref, v_ref, d_ref, mo_ref, vo_ref):
        d_ref[...], mo_ref[...], vo_ref[...] = _adam_update(w_ref[...], g_ref[...], m_ref[...], v_ref[...])

    blk = pl.BlockSpec((br, cols), lambda i: (i, 0))
    return pl.pallas_call(
        body, name=name,
        out_shape=[jax.ShapeDtypeStruct((rows, cols), F32)] * 3,
        grid=(rows // br,), in_specs=[blk] * 4, out_specs=[blk] * 3,
        compiler_params=pltpu.CompilerParams(dimension_semantics=("parallel",)),
    )(w, grad, m, v)


def _adamw_transposed(w, grad_t, m, v, name):
    _, rows, cols = w.shape
    br = 256 if rows % 256 == 0 else rows

    def body(w_ref, gt_ref, m_ref, v_ref, g_ref, d_ref, mo_ref, vo_ref):
        gv = gt_ref[...].T
        g_ref[...] = gv
        d_ref[...], mo_ref[...], vo_ref[...] = _adam_update(w_ref[...], gv, m_ref[...], v_ref[...])

    blk = pl.BlockSpec((None, br, cols), lambda i: (0, i, 0))
    return pl.pallas_call(
        body, name=name,
        out_shape=[jax.ShapeDtypeStruct((1, rows, cols), F32)] * 4,
        grid=(rows // br,), in_specs=[blk, pl.BlockSpec((cols, br), lambda i: (0, i)), blk, blk], out_specs=[blk] * 4,
        compiler_params=pltpu.CompilerParams(dimension_semantics=("parallel",)),
    )(w, grad_t, m, v)


def _f32_rows_as_bf16(a, rows, cols):
    bits = lax.bitcast_convert_type(a, BF16).reshape(a.shape[0], 2 * a.shape[1])
    return jnp.pad(bits, ((0, rows - bits.shape[0]), (0, cols - bits.shape[1])))


def kernel(x, norm_ffn1, ffn1_w_gate, ffn1_w_up, ffn1_w_down, norm_mix, w_in, conv_w, pool_w, pool_scale, w_out, norm_ffn2, ffn2_w_gate, ffn2_w_up, ffn2_w_down, norm_final, loss_target, m_norm_ffn1, m_ffn1_w_gate, m_ffn1_w_up, m_ffn1_w_down, m_norm_mix, m_w_in, m_conv_w, m_pool_w, m_pool_scale, m_w_out, m_norm_ffn2, m_ffn2_w_gate, m_ffn2_w_up, m_ffn2_w_down, m_norm_final, v_norm_ffn1, v_ffn1_w_gate, v_ffn1_w_up, v_ffn1_w_down, v_norm_mix, v_w_in, v_conv_w, v_pool_w, v_pool_scale, v_w_out, v_norm_ffn2, v_ffn2_w_gate, v_ffn2_w_up, v_ffn2_w_down, v_norm_final):
    weights = dict(norm_ffn1=norm_ffn1, ffn1_w_gate=ffn1_w_gate, ffn1_w_up=ffn1_w_up, ffn1_w_down=ffn1_w_down, norm_mix=norm_mix,
                   w_in=w_in, conv_w=conv_w, pool_w=pool_w, pool_scale=pool_scale, w_out=w_out, norm_ffn2=norm_ffn2,
                   ffn2_w_gate=ffn2_w_gate, ffn2_w_up=ffn2_w_up, ffn2_w_down=ffn2_w_down, norm_final=norm_final)
    first_m = dict(norm_ffn1=m_norm_ffn1, ffn1_w_gate=m_ffn1_w_gate, ffn1_w_up=m_ffn1_w_up, ffn1_w_down=m_ffn1_w_down,
                   norm_mix=m_norm_mix, w_in=m_w_in, conv_w=m_conv_w, pool_w=m_pool_w, pool_scale=m_pool_scale, w_out=m_w_out,
                   norm_ffn2=m_norm_ffn2, ffn2_w_gate=m_ffn2_w_gate, ffn2_w_up=m_ffn2_w_up, ffn2_w_down=m_ffn2_w_down,
                   norm_final=m_norm_final)
    second_m = dict(norm_ffn1=v_norm_ffn1, ffn1_w_gate=v_ffn1_w_gate, ffn1_w_up=v_ffn1_w_up, ffn1_w_down=v_ffn1_w_down,
                    norm_mix=v_norm_mix, w_in=v_w_in, conv_w=v_conv_w, pool_w=v_pool_w, pool_scale=v_pool_scale, w_out=v_w_out,
                    norm_ffn2=v_norm_ffn2, ffn2_w_gate=v_ffn2_w_gate, ffn2_w_up=v_ffn2_w_up, ffn2_w_down=v_ffn2_w_down,
                    norm_final=v_norm_final)
    names = list(weights)

    xs = x[0]
    tgt = loss_target[0]
    t, d = xs.shape
    dc = pool_scale.shape[1]
    cx, cy, cc = _my_place()
    chip = 2 * cx + cy
    place = jnp.stack([chip, cc]).astype(jnp.int32)

    conv_rows = 32
    wout_x = jnp.concatenate([w_out[0].astype(BF16), _f32_rows_as_bf16(conv_w[0], conv_rows, d)], axis=0)
    wg2_shard = ffn2_w_gate[0].T.astype(BF16)
    half_rows = wg2_shard.shape[0] // 2

    g1, gm, g2 = norm_ffn1, norm_mix, norm_ffn2
    gf = norm_final.reshape(1, d)
    pw = pool_w[0]

    wg1, wu1 = [[w] for w in _run_cargo(_gather_cargo([ffn1_w_gate[0].T.astype(BF16), ffn1_w_up[0].T.astype(BF16)]), "gather_ffn1")]
    (a1, b1, s1), [(wd1, win_t)] = _ffn_up(xs, g1, wg1, wu1, "ffn1_up", [_gather_cargo([ffn1_w_down[0].astype(BF16), w_in[0].T.astype(BF16)])])
    wd1 = [wd1]
    (x1,), [(wout_g, wg2_a)] = _ffn_down(xs, s1, wd1, "ffn1_down", [_gather_cargo([wout_x, wg2_shard[:half_rows]])])
    wo_rows = w_out.shape[1]
    cshard = conv_w.shape[2]
    conv_bits = wout_g.reshape(N_CHIPS, wo_rows + conv_rows, d)[:, wo_rows:wo_rows + conv_w.shape[1], :2 * cshard]
    conv_full = lax.bitcast_convert_type(conv_bits.reshape(N_CHIPS, conv_w.shape[1], cshard, 2), F32)
    conv_full = jnp.transpose(conv_full, (1, 0, 2)).reshape(conv_w.shape[1], N_CHIPS * cshard)
    (x2, proj, ymix), [(wg2_b, wu2)] = _mixer_forward(
        x1, gm, win_t, wout_g, conv_full, pw, pool_scale, [_gather_cargo([wg2_shard[half_rows:], ffn2_w_up[0].T.astype(BF16)])])
    wg2, wu2 = [wg2_a, wg2_b], [wu2]
    (a2, b2, s2), [(wd2,)] = _ffn_up(x2, g2, wg2, wu2, "ffn2_up", [_gather_cargo([ffn2_w_down[0].astype(BF16)])])
    wd2 = [wd2]
    (dx3, sq_cols, dgf), _ = _ffn_down(x2, s2, wd2, "ffn2_down", loss_head=(gf, tgt))

    (dx2, da2, db2, h3, do2, dg2), _ = _ffn_backward(dx3, x2, g2, a2, b2, wg2, wu2, wd2, "ffn2_backward")
    p_wg2, _ = _weight_grad(da2, h3, "ffn2_gate_grad")
    p_wu2, [(x_wg2,)] = _weight_grad(db2, h3, "ffn2_up_grad", [_exchange_cargo([p_wg2])])
    p_wd2, [(x_wu2,)] = _weight_grad(s2, do2, "ffn2_down_grad", [_exchange_cargo([p_wu2])])

    (dx1, dproj, h2, dx2b, dgm, dcw, dps, dpw), [(x_wd2,)] = _mixer_backward(
        dx2, x1, gm, proj, win_t, wout_g, conv_full, pw, pool_scale, [_exchange_cargo([p_wd2])])

    (dx0, da1, db1, h1, do1, dg1), _ = _ffn_backward(dx1, xs, g1, a1, b1, wg1, wu1, wd1, "ffn1_backward")

    npw = pw.size // d
    head = [dg1, dgm, dg2, dgf, jnp.pad(dps, ((0, 0), (0, d - dc))), jnp.pad(dcw, ((0, 0), (0, d - dc))), sq_cols]
    n_head = sum(h.shape[0] for h in head)
    base = -(-n_head // 8) * 8
    pack = jnp.concatenate(head + [jnp.zeros((base - n_head, d), F32), dpw.reshape(npw, d)], axis=0)

    p_wg1, [(packs,)] = _weight_grad(da1, h1, "ffn1_gate_grad", [_all_gather_small_cargo(pack)])
    p_wu1, [(x_wg1,)] = _weight_grad(db1, h1, "ffn1_up_grad", [_exchange_cargo([p_wg1])])
    p_wd1, [(x_wu1,)] = _weight_grad(s1, do1, "ffn1_down_grad", [_exchange_cargo([p_wu1])])
    p_win, [(x_wd1,)] = _weight_grad(dproj, h2, "w_in_grad", [_exchange_cargo([p_wd1])])
    p_wout, [(x_win,)] = _weight_grad(ymix, dx2b, "w_out_grad", [_exchange_cargo([p_win])])
    x_wout, = _run_cargo(_exchange_cargo([p_wout]), "grad_exchange_last")
    small = _sum_by_device(packs)
    loss = jnp.sum(small[n_head - 1]) * (0.5 / d)

    order = ["wg1", "wu1", "wd1", "win", "wout", "wg2", "wu2", "wd2"]
    pairs = dict(wg1=p_wg1, wu1=p_wu1, wd1=p_wd1, win=p_win, wout=p_wout, wg2=p_wg2, wu2=p_wu2, wd2=p_wd2)
    landed = dict(wg1=x_wg1, wu1=x_wu1, wd1=x_wd1, win=x_win, wout=x_wout, wg2=x_wg2, wu2=x_wu2, wd2=x_wd2)
    both = _sibling_share([_chip_sum(pairs[k], landed[k], place, k) for k in order])
    rwg1, rwu1, rwd1, rwin, rwout, rwg2, rwu2, rwd2 = [b.reshape(2 * b.shape[1], b.shape[2]) for b in both]

    grads = {
        "norm_ffn1": small[0:1], "norm_mix": small[1:2], "norm_ffn2": small[2:3], "norm_final": small[3],
        "pool_scale": small[4:5, :dc],
        "conv_w": lax.dynamic_slice_in_dim(small[5:5 + dcw.shape[0], :dc], chip * cshard, cshard, axis=1)[None],
        "pool_w": small[base:].reshape(pool_w.shape),
        "ffn1_w_down": rwd1[None], "w_out": rwout[None], "ffn2_w_down": rwd2[None],
    }
    by_view = {"ffn1_w_gate": rwg1, "ffn1_w_up": rwu1, "ffn2_w_gate": rwg2, "ffn2_w_up": rwu2}

    deltas, new_m, new_v = {}, {}, {}
    for n in names:
        w = weights[n]
        shape = w.shape
        if n == "w_in":
            grads[n], deltas[n], new_m[n], new_v[n] = _adamw_transposed(w, rwin, first_m[n], second_m[n], "adamw_" + n)
            continue
        if n in by_view:
            view = lambda a: jnp.swapaxes(a, 1, 2)[0]
            back = lambda a: jnp.swapaxes(a[None], 1, 2)
            dl, mo, vo = _adamw(view(w), by_view[n], view(first_m[n]), view(second_m[n]), "adamw_" + n)
            grads[n], deltas[n], new_m[n], new_v[n] = back(by_view[n]), back(dl), back(mo), back(vo)
            continue
        as2d = (lambda a: a.reshape(-1, shape[-1]))
        dl, mo, vo = _adamw(as2d(w), as2d(grads[n]), as2d(first_m[n]), as2d(second_m[n]), "adamw_" + n)
        deltas[n], new_m[n], new_v[n] = dl.reshape(shape), mo.reshape(shape), vo.reshape(shape)
        grads[n] = grads[n].reshape(shape)

    return (loss, dx0[None], *[grads[n] for n in names], *[deltas[n] for n in names],
            *[new_m[n] for n in names], *[new_v[n] for n in names])
```

```python
import jax
import jax.numpy as jnp
from jax import lax
from jax.experimental import pallas as pl
from jax.experimental.pallas import tpu as pltpu

F32 = jnp.float32
BF16 = jnp.bfloat16
MESH = pl.DeviceIdType.MESH

EPS = 1e-6
POOL_WINDOWS = (2, 4, 8, 16)
ADAM_LR = 0.001
ADAM_B1 = 0.9
ADAM_B2 = 0.999
ADAM_EPS = 1e-08
ADAM_WD = 0.01
ADAM_STEP = 10

N_CHIPS = 4
N_DEVICES = 8
MXU_COLS_V7X = 256
VMEM_LIMIT = 56 * 1024 * 1024
TM_FFN = 512
TM_MIX = 512
TM_TN = 1024
HALO = 16
FFN_FWD_CHUNKS = 2
FFN_BWD_CHUNKS = 2


def _nt(a, b):
    return lax.dot_general(a, b, (((1,), (1,)), ((), ())), preferred_element_type=F32)


def _tn(a, b):
    return lax.dot_general(a, b, (((0,), (0,)), ((), ())), preferred_element_type=F32)


def _nn(a, b):
    return jnp.dot(a, b, preferred_element_type=F32)


def _sigmoid(a):
    return 1.0 / (1.0 + jnp.exp(-a))


def _feature_chunks(n, parts):
    assert n % MXU_COLS_V7X == 0
    tiles = n // MXU_COLS_V7X
    out, s0 = [], 0
    for p in range(parts):
        sz = (tiles // parts + (1 if p < tiles % parts else 0)) * MXU_COLS_V7X
        if sz:
            out.append((s0, sz))
            s0 += sz
    return out


def _row_block(rows, cap):
    best = 8
    for b in range(8, min(rows, cap) + 1, 8):
        if rows % b == 0:
            best = b
    assert rows % best == 0
    return best


def _my_place():
    return lax.axis_index("x"), lax.axis_index("y"), lax.axis_index("c")


def _other_chips(x, y):
    return [(1 - x, y), (x, 1 - y), (1 - x, 1 - y)]


HBM_SPEC = pl.BlockSpec(memory_space=pltpu.HBM)


class _Cargo:
    def __init__(self, operands, out_shapes, n_sems, phases, when):
        self.operands, self.out_shapes, self.n_sems = list(operands), list(out_shapes), n_sems
        self.phases, self.when = list(phases), list(when)
        assert len(self.phases) == len(self.when) and self.when[0] == 0.0 and self.when[-1] == 1.0


def _launch(body, *, name, grid, in_specs, out_specs, out_shape, scratch_shapes, args, cargo=()):
    params = pltpu.CompilerParams(dimension_semantics=("arbitrary",) * len(grid), vmem_limit_bytes=VMEM_LIMIT)
    cargos = list(cargo)
    c_operands = [op for cg in cargos for op in cg.operands]
    c_shapes = [sh for cg in cargos for sh in cg.out_shapes]
    counts = [len(in_specs), len(c_operands), len(out_shape), len(c_shapes), len(scratch_shapes), 2 * len(cargos)]

    def carrying(*refs):
        groups, pos = [], 0
        for k in counts:
            groups.append(refs[pos:pos + k])
            pos += k
        ins, c_ins, outs, c_outs, scratch, sems = groups
        parts, pi, po = [], 0, 0
        for n, cg in enumerate(cargos):
            parts.append((c_ins[pi:pi + len(cg.operands)], c_outs[po:po + len(cg.out_shapes)], sems[2 * n], sems[2 * n + 1]))
            pi += len(cg.operands)
            po += len(cg.out_shapes)
        step, steps = 0, 1
        for ax, g in enumerate(grid):
            step = step * g + pl.program_id(ax)
            steps *= g
        todo = {}
        for cg, part in zip(cargos, parts):
            for phase, frac in zip(cg.phases[:-1], cg.when[:-1]):
                todo.setdefault(int(round(frac * (steps - 1))), []).append((phase, part))

        for at in sorted(todo):
            @pl.when(step == at)
            def _(at=at):
                for phase, part in todo[at]:
                    phase(*part)

        body(*ins, *outs, *scratch)

        if cargos:
            @pl.when(step == steps - 1)
            def _():
                for cg, part in zip(cargos, parts):
                    cg.phases[-1](*part)

    sems = [pltpu.SemaphoreType.DMA((cg.n_sems,)) for cg in cargos for _ in range(2)]
    outs = pl.pallas_call(
        carrying, name=name, grid=grid,
        in_specs=list(in_specs) + [HBM_SPEC] * counts[1], out_specs=list(out_specs) + [HBM_SPEC] * counts[3],
        out_shape=list(out_shape) + c_shapes, scratch_shapes=list(scratch_shapes) + sems,
        compiler_params=params)(*args, *c_operands)
    own, rest = list(outs[:counts[2]]), list(outs[counts[2]:])
    carried, po = [], 0
    for cg in cargos:
        carried.append(rest[po:po + len(cg.out_shapes)])
        po += len(cg.out_shapes)
    return own, carried


def _run_cargo(cargo, name):
    n_in, n_out = len(cargo.operands), len(cargo.out_shapes)

    def body(*refs):
        c_ins, c_outs, sems = refs[:n_in], refs[n_in:n_in + n_out], refs[n_in + n_out:]
        for phase in cargo.phases:
            phase(c_ins, c_outs, *sems)

    sem = pltpu.SemaphoreType.DMA((cargo.n_sems,))
    return list(pl.pallas_call(body, name=name, out_shape=cargo.out_shapes, in_specs=[HBM_SPEC] * n_in,
                               out_specs=[HBM_SPEC] * n_out, scratch_shapes=[sem, sem])(*cargo.operands))


def _gather_cargo(shards):
    n = len(shards)
    for s in shards:
        assert s.shape[0] % 32 == 0
    slots = 8

    def steps(ins, outs, send_sems, recv_sems):
        x, y, c = _my_place()
        sibling = (x, y, 1 - c)
        over_x, over_y = (1 - x, y, c), (x, 1 - y, c)
        mine, chip_x, chip_y, chip_d = 2 * x + y, 2 * (1 - x) + y, 2 * x + (1 - y), 2 * (1 - x) + (1 - y)

        def rows_of(a, chip_index, half, part=None):
            rps = shards[a].shape[0]
            hr = rps // 2
            first = -(-hr // 32) * 16
            offset, size = {None: (0, hr), 0: (0, first), 1: (first, hr - first)}[part]
            return outs[a].at[pl.ds(pl.multiple_of(chip_index * rps + half * hr + offset, 16), size), :]

        def remote(a, slot, src, dst, to):
            return pltpu.make_async_remote_copy(
                src_ref=src, dst_ref=dst, send_sem=send_sems.at[a * slots + slot], recv_sem=recv_sems.at[a * slots + slot],
                device_id=to, device_id_type=MESH)

        def same_rows(a, slot, rows, to):
            return remote(a, slot, rows, rows, to)

        def own_copy(a):
            rps = shards[a].shape[0]
            return remote(a, 7, ins[a], outs[a].at[pl.ds(pl.multiple_of(mine * rps, 16), rps), :], sibling)

        def my_half(a):
            hr = shards[a].shape[0] // 2
            return ins[a].at[pl.ds(pl.multiple_of(c * hr, 16), hr), :]

        def start():
            for a in range(n):
                own_copy(a).start()
                remote(a, 0, my_half(a), rows_of(a, mine, c), over_x).start()
                remote(a, 1, my_half(a), rows_of(a, mine, c), over_y).start()

        def relay_neighbours():
            for a in range(n):
                same_rows(a, 0, rows_of(a, chip_x, c), over_x).wait_recv()
                same_rows(a, 4, rows_of(a, chip_x, c), sibling).start()
                same_rows(a, 2, rows_of(a, chip_x, c, 0), over_y).start()
                same_rows(a, 1, rows_of(a, chip_y, c), over_y).wait_recv()
                same_rows(a, 5, rows_of(a, chip_y, c), sibling).start()
                same_rows(a, 3, rows_of(a, chip_y, c, 1), over_x).start()

        def relay_diagonal():
            for a in range(n):
                same_rows(a, 2, rows_of(a, chip_d, c, 0), over_y).wait_recv()
                same_rows(a, 3, rows_of(a, chip_d, c, 1), over_x).wait_recv()
                same_rows(a, 6, rows_of(a, chip_d, c), sibling).start()

        def finish():
            for a in range(n):
                for slot, chip_index in ((4, chip_x), (5, chip_y), (6, chip_d)):
                    same_rows(a, slot, rows_of(a, chip_index, 1 - c), sibling).wait_recv()
            for a in range(n):
                remote(a, 0, my_half(a), rows_of(a, mine, c), over_x).wait_send()
                remote(a, 1, my_half(a), rows_of(a, mine, c), over_y).wait_send()
                same_rows(a, 2, rows_of(a, chip_x, c, 0), over_y).wait_send()
                same_rows(a, 3, rows_of(a, chip_y, c, 1), over_x).wait_send()
                for slot, chip_index in ((4, chip_x), (5, chip_y), (6, chip_d)):
                    same_rows(a, slot, rows_of(a, chip_index, c), sibling).wait_send()
                own_copy(a).wait()

        return start, relay_neighbours, relay_diagonal, finish

    phases = [lambda *r, k=k: steps(*r)[k]() for k in range(4)]
    return _Cargo(shards, [jax.ShapeDtypeStruct((N_CHIPS * s.shape[0], s.shape[1]), s.dtype) for s in shards], slots * n,
                  phases, [0.0, 0.6, 0.85, 1.0])


def _exchange_cargo(pairs):
    n = len(pairs)

    def copies(ins, outs, send_sems, recv_sems):
        x, y, c = _my_place()
        return [pltpu.make_async_remote_copy(
            src_ref=ins[a].at[2 * chip[0] + chip[1]], dst_ref=outs[a].at[j],
            send_sem=send_sems.at[3 * a + j], recv_sem=recv_sems.at[3 * a + j], device_id=(*chip, c), device_id_type=MESH)
            for a in range(n) for j, chip in enumerate(_other_chips(x, y))]

    def start(*r):
        for cp in copies(*r):
            cp.start()

    def finish(*r):
        for cp in copies(*r):
            cp.wait()

    return _Cargo(pairs, [jax.ShapeDtypeStruct((3,) + p.shape[1:], p.dtype) for p in pairs], 3 * n, [start, finish], [0.0, 1.0])


def _all_gather_small_cargo(pack):
    rows, cols = pack.shape

    def copies(ins, outs, send_sems, recv_sems):
        x, y, c = _my_place()
        me = 4 * x + 2 * y + c
        remote = []
        for f in range(1, N_DEVICES):
            fx, fy, fc = (f >> 2) & 1, (f >> 1) & 1, f & 1
            to = (1 - x if fx else x, 1 - y if fy else y, 1 - c if fc else c)
            remote.append(pltpu.make_async_remote_copy(
                src_ref=ins[0], dst_ref=outs[0].at[me], send_sem=send_sems.at[f - 1], recv_sem=recv_sems.at[f - 1],
                device_id=to, device_id_type=MESH))
        own = pltpu.make_async_copy(ins[0], outs[0].at[me], send_sems.at[N_DEVICES - 1])
        return remote, own

    def start(*r):
        remote, own = copies(*r)
        own.start()
        for cp in remote:
            cp.start()

    def finish(*r):
        remote, own = copies(*r)
        for cp in remote:
            cp.wait()
        own.wait()

    return _Cargo([pack], [jax.ShapeDtypeStruct((N_DEVICES, rows, cols), F32)], N_DEVICES, [start, finish], [0.0, 1.0])


def _sum_by_device(packs):
    n, rows, cols = packs.shape

    def body(p_ref, o_ref):
        acc = p_ref[0]
        for dev in range(1, n):
            acc = acc + p_ref[dev]
        o_ref[...] = acc

    return pl.pallas_call(body, name="small_grads_sum", out_shape=jax.ShapeDtypeStruct((rows, cols), F32))(packs)


def _chip_sum(pair, got, place, tag):
    _, hr, cols = pair.shape
    br = _row_block(hr, 256)

    def body(k_ref, p_ref, r_ref, o_ref):
        acc = p_ref[...].astype(F32)
        for j in range(3):
            acc = acc + r_ref[j].astype(F32)
        o_ref[...] = acc

    return pl.pallas_call(
        body, name="grad_chip_sum_" + tag,
        out_shape=jax.ShapeDtypeStruct((2, hr, cols), F32),
        grid_spec=pltpu.PrefetchScalarGridSpec(
            num_scalar_prefetch=1, grid=(hr // br,),
            in_specs=[pl.BlockSpec((None, br, cols), lambda r, k_ref: (k_ref[0], r, 0)),
                      pl.BlockSpec((3, br, cols), lambda r, k_ref: (0, r, 0))],
            out_specs=pl.BlockSpec((None, br, cols), lambda r, k_ref: (k_ref[1], r, 0))),
        compiler_params=pltpu.CompilerParams(dimension_semantics=("parallel",)),
    )(place, pair, got)


def _sibling_share(halves):
    n = len(halves)

    def body(*refs):
        outs = refs[n:2 * n]
        send_sems, recv_sems = refs[2 * n:]
        x, y, c = _my_place()
        copies = []
        for a in range(n):
            cp = pltpu.make_async_remote_copy(
                src_ref=outs[a].at[c], dst_ref=outs[a].at[c], send_sem=send_sems.at[a], recv_sem=recv_sems.at[a],
                device_id=(x, y, 1 - c), device_id_type=MESH)
            cp.start()
            copies.append(cp)
        for cp in copies:
            cp.wait()

    return pl.pallas_call(
        body, name="grad_share_sibling",
        out_shape=[jax.ShapeDtypeStruct(h.shape, h.dtype) for h in halves],
        in_specs=[HBM_SPEC] * n, out_specs=[HBM_SPEC] * n,
        input_output_aliases={a: a for a in range(n)},
        scratch_shapes=[pltpu.SemaphoreType.DMA((n,)), pltpu.SemaphoreType.DMA((n,))],
    )(*halves)


def _load_rows(pairs, sems):
    cps = [pltpu.make_async_copy(src, dst, sems.at[j]) for j, (src, dst) in enumerate(pairs)]
    for cp in cps:
        cp.start()
    for cp in cps:
        cp.wait()


def _piece_rows(weights):
    flat = [p for pieces in weights for p in pieces]

    def copies(refs, mats):
        out, n = [], 0
        for pieces, mat in zip(weights, mats):
            rps = sum(p.shape[0] for p in pieces) // N_CHIPS
            off = 0
            for p in pieces:
                r = p.shape[0] // N_CHIPS
                if len(pieces) == 1:
                    out.append((refs[n], mat))
                else:
                    for k in range(N_CHIPS):
                        out.append((refs[n].at[pl.ds(k * r, r), :], mat.at[pl.ds(k * rps + off, r), :]))
                off += r
                n += 1
        return out

    n_copies = sum(1 if len(pieces) == 1 else N_CHIPS * len(pieces) for pieces in weights)
    return flat, copies, n_copies


def _loss_head(xv, gv, tv):
    d = xv.shape[-1]
    r = lax.rsqrt(jnp.mean(xv * xv, axis=-1, keepdims=True) + EPS)
    xhat = xv * r
    err = xhat * gv - tv
    dy = err * (1.0 / d)
    dxh = dy * gv
    dx = r * (dxh - xhat * jnp.mean(dxh * xhat, axis=-1, keepdims=True))
    return dx, jnp.sum(err * err, axis=0, keepdims=True), jnp.sum(dy * xhat, axis=0, keepdims=True)


def _ffn_up(x, g, wg_t, wu_t, name, cargo=(), gate=None):
    t, d = x.shape
    mats = [w for w in (wg_t, wu_t) if w is not None]
    f = sum(p.shape[0] for p in mats[0])
    tm = min(TM_FFN, t)
    chunks = _feature_chunks(f, FFN_FWD_CHUNKS)
    flat, copies, n_copies = _piece_rows(mats)
    nw, nm = len(flat), len(mats)
    n_gate = 0 if gate is None else 1
    n_out = {(True, True): 3, (True, False): 1, (False, True): 2}[(wg_t is not None, wu_t is not None)]
    assert (wg_t is None) == (gate is not None)

    def body(x_ref, g_ref, *rest):
        gate_ref, w_hbm = rest[:n_gate], rest[n_gate:n_gate + nw]
        outs, vm = rest[n_gate + nw:n_gate + nw + n_out], rest[n_gate + nw + n_out:]
        w_vmem, sems = vm[:nm], vm[nm]

        @pl.when(pl.program_id(0) == 0)
        def _():
            _load_rows(copies(w_hbm, w_vmem), sems)

        xv = x_ref[...]
        r = lax.rsqrt(jnp.mean(xv * xv, axis=-1, keepdims=True) + EPS)
        h = (xv * r * g_ref[...]).astype(BF16)
        for s0, sz in chunks:
            if wu_t is None:
                outs[0][:, s0:s0 + sz] = _nt(h, w_vmem[0][s0:s0 + sz, :]).astype(BF16)
                continue
            if gate is None:
                a = _nt(h, w_vmem[0][s0:s0 + sz, :])
                outs[0][:, s0:s0 + sz] = a.astype(BF16)
            else:
                a = gate_ref[0][:, s0:s0 + sz].astype(F32)
            b = _nt(h, w_vmem[-1][s0:s0 + sz, :])
            outs[-2][:, s0:s0 + sz] = b.astype(BF16)
            outs[-1][:, s0:s0 + sz] = (a * _sigmoid(a) * b).astype(BF16)

    tok = lambda i: (i, 0)
    wide = pl.BlockSpec((tm, f), tok)
    return _launch(
        body, name=name, grid=(t // tm,),
        in_specs=[pl.BlockSpec((tm, d), tok), pl.BlockSpec((1, d), lambda i: (0, 0))] + [wide] * n_gate + [HBM_SPEC] * nw,
        out_specs=[wide] * n_out, out_shape=[jax.ShapeDtypeStruct((t, f), BF16)] * n_out,
        scratch_shapes=[pltpu.VMEM((f, d), BF16)] * nm + [pltpu.SemaphoreType.DMA((n_copies,))],
        args=(x, g, *([] if gate is None else [gate]), *flat), cargo=cargo)


def _ffn_down(x, s, wd, name, cargo=(), loss_head=None):
    t, d = x.shape
    f = s.shape[1]
    tm = min(TM_FFN, t)
    flat, copies, n_copies = _piece_rows([wd])
    nw = len(flat)
    nl = 2 if loss_head else 0

    def body(x_ref, s_ref, *rest):
        head, w_hbm = rest[:nl], rest[nl:nl + nw]
        xo_ref = rest[nl + nw]
        sums, (wdn, sems) = rest[nl + nw + 1:nl + nw + 1 + nl], rest[nl + nw + 1 + nl:]

        @pl.when(pl.program_id(0) == 0)
        def _():
            _load_rows(copies(w_hbm, [wdn]), sems)
            for sum_ref in sums:
                sum_ref[...] = jnp.zeros_like(sum_ref)

        xo = x_ref[...] + 0.5 * _nn(s_ref[...], wdn[...])
        if loss_head:
            dx, sq, dgf = _loss_head(xo, head[0][...], head[1][...])
            xo_ref[...] = dx
            sums[0][...] += sq
            sums[1][...] += dgf
        else:
            xo_ref[...] = xo

    tok = lambda i: (i, 0)
    one = lambda i: (0, 0)
    return _launch(
        body, name=name, grid=(t // tm,),
        in_specs=[pl.BlockSpec((tm, d), tok), pl.BlockSpec((tm, f), tok)]
        + ([pl.BlockSpec((1, d), one), pl.BlockSpec((tm, d), tok)] if loss_head else []) + [HBM_SPEC] * nw,
        out_specs=[pl.BlockSpec((tm, d), tok)] + [pl.BlockSpec((1, d), one)] * nl,
        out_shape=[jax.ShapeDtypeStruct((t, d), F32)] + [jax.ShapeDtypeStruct((1, d), F32)] * nl,
        scratch_shapes=[pltpu.VMEM((f, d), BF16), pltpu.SemaphoreType.DMA((n_copies,))],
        args=(x, s, *(loss_head or ()), *flat), cargo=cargo)


def _ffn_backward(dxo, x, g, a, b, wg_t, wu_t, wd, name, cargo=()):
    t, d = x.shape
    f = sum(p.shape[0] for p in wd)
    tm = min(TM_FFN // 2, t)
    chunks = _feature_chunks(f, FFN_BWD_CHUNKS)
    flat, copies, n_copies = _piece_rows([wg_t, wu_t, wd])
    nw = len(flat)

    def body(dxo_ref, x_ref, g_ref, a_ref, b_ref, *rest):
        w_hbm, (dx_ref, da_ref, db_ref, h_ref, do_ref, dg_ref, wg, wu, wdn, sems) = rest[:nw], rest[nw:]

        @pl.when(pl.program_id(0) == 0)
        def _():
            _load_rows(copies(w_hbm, [wg, wu, wdn]), sems)
            dg_ref[...] = jnp.zeros_like(dg_ref)

        xv = x_ref[...]
        gv = g_ref[...]
        r = lax.rsqrt(jnp.mean(xv * xv, axis=-1, keepdims=True) + EPS)
        xhat = xv * r
        h_ref[...] = (xhat * gv).astype(BF16)
        dxo_v = dxo_ref[...]
        dout = (0.5 * dxo_v).astype(BF16)
        do_ref[...] = dout
        dh = jnp.zeros((tm, d), F32)
        for s0, sz in chunks:
            ds = _nt(dout, wdn[s0:s0 + sz, :])
            av = a_ref[:, s0:s0 + sz].astype(F32)
            bv = b_ref[:, s0:s0 + sz].astype(F32)
            sig = _sigmoid(av)
            silu = av * sig
            da = (ds * bv * (sig * (1.0 + av * (1.0 - sig)))).astype(BF16)
            db = (ds * silu).astype(BF16)
            da_ref[:, s0:s0 + sz] = da
            db_ref[:, s0:s0 + sz] = db
            dh = dh + _nn(da, wg[s0:s0 + sz, :]) + _nn(db, wu[s0:s0 + sz, :])
        dg_ref[...] += jnp.sum(dh * xhat, axis=0, keepdims=True)
        dxh = dh * gv
        dx_ref[...] = dxo_v + r * (dxh - xhat * jnp.mean(dxh * xhat, axis=-1, keepdims=True))

    tok = lambda i: (i, 0)
    one = lambda i: (0, 0)
    return _launch(
        body, name=name, grid=(t // tm,),
        in_specs=[pl.BlockSpec((tm, d), tok), pl.BlockSpec((tm, d), tok), pl.BlockSpec((1, d), one),
                  pl.BlockSpec((tm, f), tok), pl.BlockSpec((tm, f), tok)] + [HBM_SPEC] * nw,
        out_specs=[pl.BlockSpec((tm, d), tok), pl.BlockSpec((tm, f), tok), pl.BlockSpec((tm, f), tok),
                   pl.BlockSpec((tm, d), tok), pl.BlockSpec((tm, d), tok), pl.BlockSpec((1, d), one)],
        out_shape=[jax.ShapeDtypeStruct((t, d), F32), jax.ShapeDtypeStruct((t, f), BF16), jax.ShapeDtypeStruct((t, f), BF16),
                   jax.ShapeDtypeStruct((t, d), BF16), jax.ShapeDtypeStruct((t, d), BF16), jax.ShapeDtypeStruct((1, d), F32)],
        scratch_shapes=[pltpu.VMEM((f, d), BF16), pltpu.VMEM((f, d), BF16), pltpu.VMEM((f, d), BF16), pltpu.SemaphoreType.DMA((n_copies,))],
        args=(dxo, x, g, a, b, *flat), cargo=cargo)


def _weight_grad(lhs, rhs, name, cargo=()):
    t, m = lhs.shape
    d = rhs.shape[1]
    tm = min(TM_TN, t)
    nt = t // tm
    nj = 1
    bm = m // nj
    cpb = N_CHIPS // nj
    rps = m // N_CHIPS
    hr = rps // 2
    assert hr % 16 == 0

    def body(l_ref, r_ref, o_ref, acc, stage, recv, send_sems, recv_sems):
        j = pl.program_id(0)
        i = pl.program_id(1)
        @pl.when(i == 0)
        def _():
            acc[...] = jnp.zeros_like(acc)

        acc[...] += _tn(l_ref[...], r_ref[...])

        def pair_sum(jj):
            x, y, c = _my_place()
            copies = []
            for q in range(cpb):
                slot = jj * cpb + q
                stage[slot] = acc[pl.ds(pl.multiple_of(q * rps + (1 - c) * hr, 16), hr), :].astype(BF16)
                cp = pltpu.make_async_remote_copy(
                    src_ref=stage.at[slot], dst_ref=recv.at[slot], send_sem=send_sems.at[slot], recv_sem=recv_sems.at[slot],
                    device_id=(x, y, 1 - c), device_id_type=MESH)
                cp.start()
                copies.append(cp)
            for q, cp in enumerate(copies):
                cp.wait_recv()
                mine = acc[pl.ds(pl.multiple_of(q * rps + c * hr, 16), hr), :]
                o_ref[q] = (mine + recv[jj * cpb + q].astype(F32)).astype(BF16)
            for cp in copies:
                cp.wait_send()

        for jj in range(nj):
            @pl.when(jnp.logical_and(i == nt - 1, j == jj))
            def _():
                pair_sum(jj)

    outs, carried = _launch(
        body, name=name, grid=(nj, nt),
        in_specs=[pl.BlockSpec((tm, bm), lambda j, i: (i, j)), pl.BlockSpec((tm, d), lambda j, i: (i, 0))],
        out_specs=[pl.BlockSpec((cpb, hr, d), lambda j, i: (j, 0, 0))],
        out_shape=[jax.ShapeDtypeStruct((N_CHIPS, hr, d), BF16)],
        scratch_shapes=[pltpu.VMEM((bm, d), F32), pltpu.VMEM((N_CHIPS, hr, d), BF16), pltpu.VMEM((N_CHIPS, hr, d), BF16),
                        pltpu.SemaphoreType.DMA((N_CHIPS,)), pltpu.SemaphoreType.DMA((N_CHIPS,))],
        args=(lhs, rhs), cargo=cargo)
    return outs[0], carried


def _pool_parts(u_cols, ubuf, cols, w, row, tm):
    ws = u_cols
    for s in range(1, w):
        ws = ws + ubuf[HALO - s:HALO - s + tm, cols]
    cnt = jnp.minimum(row + 1, w).astype(F32)
    return ws / cnt - u_cols, cnt


def _mixer_forward(x, g, win_t, wout_x, conv_w, pool_w, pool_scale, cargo=()):
    t, d = x.shape
    dc = win_t.shape[0] // 4
    gcw = dc // len(POOL_WINDOWS)
    wo_rows = d // N_CHIPS
    wo_stride = wout_x.shape[0] // N_CHIPS
    tm = min(TM_MIX, t)

    def body(x_ref, g_ref, win_hbm, wout_hbm, cw_ref, pw_ref, ps_ref, xo_ref, proj_ref, y_ref,
             win, wout, zbuf, ubuf, sems):
        i = pl.program_id(0)

        @pl.when(i == 0)
        def _():
            pairs = [(win_hbm, win)]
            for k in range(N_CHIPS):
                pairs.append((wout_hbm.at[pl.ds(k * wo_stride, wo_rows), :], wout.at[pl.ds(k * wo_rows, wo_rows), :]))
            _load_rows(pairs, sems)
            zbuf[0:8, :] = jnp.zeros((8, dc), F32)
            ubuf[0:HALO, :] = jnp.zeros((HALO, dc), F32)

        xv = x_ref[...]
        r = lax.rsqrt(jnp.mean(xv * xv, axis=-1, keepdims=True) + EPS)
        h = (xv * r * g_ref[...]).astype(BF16)
        v = _nt(h, win[0:dc, :])
        gb = _nt(h, win[dc:2 * dc, :])
        gc = _nt(h, win[2 * dc:3 * dc, :])
        u = _nt(h, win[3 * dc:4 * dc, :])
        proj_ref[:, 0:dc] = v.astype(BF16)
        proj_ref[:, dc:2 * dc] = gb.astype(BF16)
        proj_ref[:, 2 * dc:3 * dc] = gc.astype(BF16)
        proj_ref[:, 3 * dc:4 * dc] = u.astype(BF16)

        z = gc * v
        zbuf[8:8 + tm, :] = z
        cw = cw_ref[...]
        conv = cw[2:3, :] * z + cw[1:2, :] * zbuf[7:7 + tm, :] + cw[0:1, :] * zbuf[6:6 + tm, :]
        y_ref[:, 0:dc] = (gb * conv).astype(BF16)

        ubuf[HALO:HALO + tm, :] = u
        row = i * tm + lax.broadcasted_iota(jnp.int32, (tm, 1), 0)
        for gi, w in enumerate(POOL_WINDOWS):
            cols = slice(gi * gcw, (gi + 1) * gcw)
            pooled, _ = _pool_parts(u[:, cols], ubuf, cols, w, row, tm)
            yb = _nn(pooled.astype(BF16), pw_ref[gi].astype(BF16)) * ps_ref[:, cols]
            y_ref[:, dc + gi * gcw:dc + (gi + 1) * gcw] = yb.astype(BF16)

        xo_ref[...] = xv + _nn(y_ref[...], wout[...])
        zbuf[0:8, :] = zbuf[tm:tm + 8, :]
        ubuf[0:HALO, :] = ubuf[tm:tm + HALO, :]

    tok = lambda i: (i, 0)
    one = lambda i: (0, 0)
    return _launch(
        body, name="mixer_forward", grid=(t // tm,),
        in_specs=[pl.BlockSpec((tm, d), tok), pl.BlockSpec((1, d), one), HBM_SPEC, HBM_SPEC,
                  pl.BlockSpec(conv_w.shape, one), pl.BlockSpec(pool_w.shape, lambda i: (0, 0, 0)), pl.BlockSpec((1, dc), one)],
        out_specs=[pl.BlockSpec((tm, d), tok), pl.BlockSpec((tm, 4 * dc), tok), pl.BlockSpec((tm, 2 * dc), tok)],
        out_shape=[jax.ShapeDtypeStruct((t, d), F32), jax.ShapeDtypeStruct((t, 4 * dc), BF16), jax.ShapeDtypeStruct((t, 2 * dc), BF16)],
        scratch_shapes=[pltpu.VMEM((4 * dc, d), BF16), pltpu.VMEM((2 * dc, d), BF16),
                        pltpu.VMEM((tm + 8, dc), F32), pltpu.VMEM((tm + HALO, dc), F32), pltpu.SemaphoreType.DMA((1 + N_CHIPS,))],
        args=(x, g, win_t, wout_x, conv_w, pool_w, pool_scale), cargo=cargo)


def _mixer_backward(dxo, x, g, proj, win_t, wout_x, conv_w, pool_w, pool_scale, cargo=()):
    t, d = x.shape
    dc = win_t.shape[0] // 4
    ng = len(POOL_WINDOWS)
    gcw = dc // ng
    wo_rows = d // N_CHIPS
    wo_stride = wout_x.shape[0] // N_CHIPS
    tm = min(TM_MIX, t)
    n_tiles = t // tm
    hb = tm // HALO

    def body(dxo_ref, x_ref, g_ref, proj_ref, halo_ref, win_hbm, wout_hbm, cw_ref, pw_ref, ps_ref,
             dx_ref, dproj_ref, h_ref, dxob_ref, dg_ref, dcw_ref, dps_ref, dpw_ref,
             win, wout, zbuf, ubuf, dcbuf, ebuf, sems):
        i = pl.program_id(0)
        tile = n_tiles - 1 - i

        @pl.when(i == 0)
        def _():
            pairs = [(win_hbm, win)]
            for k in range(N_CHIPS):
                pairs.append((wout_hbm.at[pl.ds(k * wo_stride, wo_rows), :], wout.at[pl.ds(k * wo_rows, wo_rows), :]))
            _load_rows(pairs, sems)
            dcbuf[tm:tm + 8, :] = jnp.zeros((8, dc), F32)
            ebuf[tm:tm + HALO, :] = jnp.zeros((HALO, dc), F32)
            dg_ref[...] = jnp.zeros_like(dg_ref)
            dcw_ref[...] = jnp.zeros_like(dcw_ref)
            dps_ref[...] = jnp.zeros_like(dps_ref)
            dpw_ref[...] = jnp.zeros_like(dpw_ref)

        xv = x_ref[...]
        gv = g_ref[...]
        r = lax.rsqrt(jnp.mean(xv * xv, axis=-1, keepdims=True) + EPS)
        xhat = xv * r
        h_ref[...] = (xhat * gv).astype(BF16)
        dxo_v = dxo_ref[...]
        dxo_b = dxo_v.astype(BF16)
        dxob_ref[...] = dxo_b

        v = proj_ref[:, 0:dc].astype(F32)
        gb = proj_ref[:, dc:2 * dc].astype(F32)
        gc = proj_ref[:, 2 * dc:3 * dc].astype(F32)
        u = proj_ref[:, 3 * dc:4 * dc].astype(F32)
        first = jnp.where(tile > 0, 1.0, 0.0)
        zbuf[0:HALO, :] = halo_ref[:, 2 * dc:3 * dc].astype(F32) * halo_ref[:, 0:dc].astype(F32) * first
        ubuf[0:HALO, :] = halo_ref[:, 3 * dc:4 * dc].astype(F32) * first
        z = gc * v
        zbuf[HALO:HALO + tm, :] = z
        ubuf[HALO:HALO + tm, :] = u
        z1 = zbuf[HALO - 1:HALO - 1 + tm, :]
        z2 = zbuf[HALO - 2:HALO - 2 + tm, :]
        cw = cw_ref[...]
        conv = cw[2:3, :] * z + cw[1:2, :] * z1 + cw[0:1, :] * z2

        dy = _nt(dxo_b, wout[...])
        dya = dy[:, 0:dc]
        dgb = dya * conv
        dconv = dya * gb
        dcbuf[0:tm, :] = dconv
        dz = cw[2:3, :] * dconv + cw[1:2, :] * dcbuf[1:1 + tm, :] + cw[0:1, :] * dcbuf[2:2 + tm, :]
        dgc = dz * v
        dv = dz * gc
        dcw_ref[0:1, :] += jnp.sum(dconv * z2, axis=0, keepdims=True)
        dcw_ref[1:2, :] += jnp.sum(dconv * z1, axis=0, keepdims=True)
        dcw_ref[2:3, :] += jnp.sum(dconv * z, axis=0, keepdims=True)

        dproj_ref[:, 0:dc] = dv.astype(BF16)
        dproj_ref[:, dc:2 * dc] = dgb.astype(BF16)
        dproj_ref[:, 2 * dc:3 * dc] = dgc.astype(BF16)

        row = tile * tm + lax.broadcasted_iota(jnp.int32, (tm, 1), 0)
        for gi, w in enumerate(POOL_WINDOWS):
            cols = slice(gi * gcw, (gi + 1) * gcw)
            pooled, cnt = _pool_parts(u[:, cols], ubuf, cols, w, row, tm)
            pooled_b = pooled.astype(BF16)
            pw_b = pw_ref[gi].astype(BF16)
            dyb = dy[:, dc + gi * gcw:dc + (gi + 1) * gcw]
            q = _nn(pooled_b, pw_b)
            dps_ref[:, cols] += jnp.sum(q * dyb, axis=0, keepdims=True)
            dq = (dyb * ps_ref[:, cols]).astype(BF16)
            dpw_ref[gi] += _tn(pooled_b, dq)
            dpooled = _nt(dq, pw_b)
            ebuf[0:tm, cols] = dpooled / cnt
            du = -dpooled
            for s in range(w):
                du = du + ebuf[s:s + tm, cols]
            dproj_ref[:, 3 * dc + gi * gcw:3 * dc + (gi + 1) * gcw] = du.astype(BF16)

        dh = _nn(dproj_ref[...], win[...])
        dg_ref[...] += jnp.sum(dh * xhat, axis=0, keepdims=True)
        dxh = dh * gv
        dx_ref[...] = dxo_v + r * (dxh - xhat * jnp.mean(dxh * xhat, axis=-1, keepdims=True))
        dcbuf[tm:tm + 8, :] = dcbuf[0:8, :]
        ebuf[tm:tm + HALO, :] = ebuf[0:HALO, :]

    tok = lambda i: (n_tiles - 1 - i, 0)
    halo = lambda i: (jnp.maximum((n_tiles - 1 - i) * hb - 1, 0), 0)
    one = lambda i: (0, 0)
    return _launch(
        body, name="mixer_backward", grid=(n_tiles,),
        in_specs=[pl.BlockSpec((tm, d), tok), pl.BlockSpec((tm, d), tok), pl.BlockSpec((1, d), one),
                  pl.BlockSpec((tm, 4 * dc), tok), pl.BlockSpec((HALO, 4 * dc), halo), HBM_SPEC, HBM_SPEC,
                  pl.BlockSpec(conv_w.shape, one), pl.BlockSpec(pool_w.shape, lambda i: (0, 0, 0)), pl.BlockSpec((1, dc), one)],
        out_specs=[pl.BlockSpec((tm, d), tok), pl.BlockSpec((tm, 4 * dc), tok), pl.BlockSpec((tm, d), tok), pl.BlockSpec((tm, d), tok),
                   pl.BlockSpec((1, d), one), pl.BlockSpec(conv_w.shape, one), pl.BlockSpec((1, dc), one),
                   pl.BlockSpec(pool_w.shape, lambda i: (0, 0, 0))],
        out_shape=[jax.ShapeDtypeStruct((t, d), F32), jax.ShapeDtypeStruct((t, 4 * dc), BF16), jax.ShapeDtypeStruct((t, d), BF16),
                   jax.ShapeDtypeStruct((t, d), BF16), jax.ShapeDtypeStruct((1, d), F32), jax.ShapeDtypeStruct(conv_w.shape, F32),
                   jax.ShapeDtypeStruct((1, dc), F32), jax.ShapeDtypeStruct(pool_w.shape, F32)],
        scratch_shapes=[pltpu.VMEM((4 * dc, d), BF16), pltpu.VMEM((2 * dc, d), BF16),
                        pltpu.VMEM((tm + HALO, dc), F32), pltpu.VMEM((tm + HALO, dc), F32),
                        pltpu.VMEM((tm + 8, dc), F32), pltpu.VMEM((tm + HALO, dc), F32), pltpu.SemaphoreType.DMA((1 + N_CHIPS,))],
        args=(dxo, x, g, proj, proj, win_t, wout_x, conv_w, pool_w, pool_scale), cargo=cargo)


def _adam_update(w, gv, m, v):
    m_new = ADAM_B1 * m + (1.0 - ADAM_B1) * gv
    v_new = ADAM_B2 * v + (1.0 - ADAM_B2) * (gv * gv)
    m_hat = m_new / (1.0 - ADAM_B1 ** ADAM_STEP)
    v_hat = v_new / (1.0 - ADAM_B2 ** ADAM_STEP)
    return -ADAM_LR * (m_hat / (jnp.sqrt(v_hat) + ADAM_EPS) + ADAM_WD * w), m_new, v_new


def _adamw(w, grad, m, v, name):
    rows, cols = w.shape
    br = _row_block(rows, 256) if rows >= 8 else rows

    def body(w_ref, g_ref, m_ref, v_ref, d_ref, mo_ref, vo_ref):
        d_ref[...], mo_ref[...], vo_ref[...] = _adam_update(w_ref[...], g_ref[...], m_ref[...], v_ref[...])

    blk = pl.BlockSpec((br, cols), lambda i: (i, 0))
    return pl.pallas_call(
        body, name=name,
        out_shape=[jax.ShapeDtypeStruct((rows, cols), F32)] * 3,
        grid=(rows // br,), in_specs=[blk] * 4, out_specs=[blk] * 3,
        compiler_params=pltpu.CompilerParams(dimension_semantics=("parallel",)),
    )(w, grad, m, v)


def _adamw_transposed(w, grad_t, m, v, name):
    _, rows, cols = w.shape
    br = 256 if rows % 256 == 0 else rows

    def body(w_ref, gt_ref, m_ref, v_ref, g_ref, d_ref, mo_ref, vo_ref):
        gv = gt_ref[...].T
        g_ref[...] = gv
        d_ref[...], mo_ref[...], vo_ref[...] = _adam_update(w_ref[...], gv, m_ref[...], v_ref[...])

    blk = pl.BlockSpec((None, br, cols), lambda i: (0, i, 0))
    return pl.pallas_call(
        body, name=name,
        out_shape=[jax.ShapeDtypeStruct((1, rows, cols), F32)] * 4,
        grid=(rows // br,), in_specs=[blk, pl.BlockSpec((cols, br), lambda i: (0, i)), blk, blk], out_specs=[blk] * 4,
        compiler_params=pltpu.CompilerParams(dimension_semantics=("parallel",)),
    )(w, grad_t, m, v)


def _f32_rows_as_bf16(a, rows, cols):
    bits = lax.bitcast_convert_type(a, BF16).reshape(a.shape[0], 2 * a.shape[1])
    return jnp.pad(bits, ((0, rows - bits.shape[0]), (0, cols - bits.shape[1])))


def kernel(x, norm_ffn1, ffn1_w_gate, ffn1_w_up, ffn1_w_down, norm_mix, w_in, conv_w, pool_w, pool_scale, w_out, norm_ffn2, ffn2_w_gate, ffn2_w_up, ffn2_w_down, norm_final, loss_target, m_norm_ffn1, m_ffn1_w_gate, m_ffn1_w_up, m_ffn1_w_down, m_norm_mix, m_w_in, m_conv_w, m_pool_w, m_pool_scale, m_w_out, m_norm_ffn2, m_ffn2_w_gate, m_ffn2_w_up, m_ffn2_w_down, m_norm_final, v_norm_ffn1, v_ffn1_w_gate, v_ffn1_w_up, v_ffn1_w_down, v_norm_mix, v_w_in, v_conv_w, v_pool_w, v_pool_scale, v_w_out, v_norm_ffn2, v_ffn2_w_gate, v_ffn2_w_up, v_ffn2_w_down, v_norm_final):
    weights = dict(norm_ffn1=norm_ffn1, ffn1_w_gate=ffn1_w_gate, ffn1_w_up=ffn1_w_up, ffn1_w_down=ffn1_w_down, norm_mix=norm_mix,
                   w_in=w_in, conv_w=conv_w, pool_w=pool_w, pool_scale=pool_scale, w_out=w_out, norm_ffn2=norm_ffn2,
                   ffn2_w_gate=ffn2_w_gate, ffn2_w_up=ffn2_w_up, ffn2_w_down=ffn2_w_down, norm_final=norm_final)
    first_m = dict(norm_ffn1=m_norm_ffn1, ffn1_w_gate=m_ffn1_w_gate, ffn1_w_up=m_ffn1_w_up, ffn1_w_down=m_ffn1_w_down,
                   norm_mix=m_norm_mix, w_in=m_w_in, conv_w=m_conv_w, pool_w=m_pool_w, pool_scale=m_pool_scale, w_out=m_w_out,
                   norm_ffn2=m_norm_ffn2, ffn2_w_gate=m_ffn2_w_gate, ffn2_w_up=m_ffn2_w_up, ffn2_w_down=m_ffn2_w_down,
                   norm_final=m_norm_final)
    second_m = dict(norm_ffn1=v_norm_ffn1, ffn1_w_gate=v_ffn1_w_gate, ffn1_w_up=v_ffn1_w_up, ffn1_w_down=v_ffn1_w_down,
                    norm_mix=v_norm_mix, w_in=v_w_in, conv_w=v_conv_w, pool_w=v_pool_w, pool_scale=v_pool_scale, w_out=v_w_out,
                    norm_ffn2=v_norm_ffn2, ffn2_w_gate=v_ffn2_w_gate, ffn2_w_up=v_ffn2_w_up, ffn2_w_down=v_ffn2_w_down,
                    norm_final=v_norm_final)
    names = list(weights)

    xs = x[0]
    tgt = loss_target[0]
    t, d = xs.shape
    dc = pool_scale.shape[1]
    cx, cy, cc = _my_place()
    chip = 2 * cx + cy
    place = jnp.stack([chip, cc]).astype(jnp.int32)

    conv_rows = 32
    wout_x = jnp.concatenate([w_out[0].astype(BF16), _f32_rows_as_bf16(conv_w[0], conv_rows, d)], axis=0)

    g1, gm, g2 = norm_ffn1, norm_mix, norm_ffn2
    gf = norm_final.reshape(1, d)
    pw = pool_w[0]

    wg1, = _run_cargo(_gather_cargo([ffn1_w_gate[0].T.astype(BF16)]), "gather_ffn1")
    (a1,), [(wu1,)] = _ffn_up(xs, g1, [wg1], None, "ffn1_gate", [_gather_cargo([ffn1_w_up[0].T.astype(BF16)])])
    (b1, s1), [(wd1,)] = _ffn_up(xs, g1, None, [wu1], "ffn1_up", [_gather_cargo([ffn1_w_down[0].astype(BF16)])], gate=a1)
    wg1, wu1, wd1 = [wg1], [wu1], [wd1]
    (x1,), [(win_t, wout_g)] = _ffn_down(xs, s1, wd1, "ffn1_down", [_gather_cargo([w_in[0].T.astype(BF16), wout_x])])
    wo_rows = w_out.shape[1]
    cshard = conv_w.shape[2]
    conv_bits = wout_g.reshape(N_CHIPS, wo_rows + conv_rows, d)[:, wo_rows:wo_rows + conv_w.shape[1], :2 * cshard]
    conv_full = lax.bitcast_convert_type(conv_bits.reshape(N_CHIPS, conv_w.shape[1], cshard, 2), F32)
    conv_full = jnp.transpose(conv_full, (1, 0, 2)).reshape(conv_w.shape[1], N_CHIPS * cshard)
    (x2, proj, ymix), [(wg2, wu2)] = _mixer_forward(
        x1, gm, win_t, wout_g, conv_full, pw, pool_scale,
        [_gather_cargo([ffn2_w_gate[0].T.astype(BF16), ffn2_w_up[0].T.astype(BF16)])])
    wg2, wu2 = [wg2], [wu2]
    (a2, b2, s2), [(wd2,)] = _ffn_up(x2, g2, wg2, wu2, "ffn2_up", [_gather_cargo([ffn2_w_down[0].astype(BF16)])])
    wd2 = [wd2]
    (dx3, sq_cols, dgf), _ = _ffn_down(x2, s2, wd2, "ffn2_down", loss_head=(gf, tgt))

    (dx2, da2, db2, h3, do2, dg2), _ = _ffn_backward(dx3, x2, g2, a2, b2, wg2, wu2, wd2, "ffn2_backward")
    p_wg2, _ = _weight_grad(da2, h3, "ffn2_gate_grad")
    p_wu2, [(x_wg2,)] = _weight_grad(db2, h3, "ffn2_up_grad", [_exchange_cargo([p_wg2])])
    p_wd2, [(x_wu2,)] = _weight_grad(s2, do2, "ffn2_down_grad", [_exchange_cargo([p_wu2])])

    (dx1, dproj, h2, dx2b, dgm, dcw, dps, dpw), [(x_wd2,)] = _mixer_backward(
        dx2, x1, gm, proj, win_t, wout_g, conv_full, pw, pool_scale, [_exchange_cargo([p_wd2])])

    (dx0, da1, db1, h1, do1, dg1), _ = _ffn_backward(dx1, xs, g1, a1, b1, wg1, wu1, wd1, "ffn1_backward")

    npw = pw.size // d
    head = [dg1, dgm, dg2, dgf, jnp.pad(dps, ((0, 0), (0, d - dc))), jnp.pad(dcw, ((0, 0), (0, d - dc))), sq_cols]
    n_head = sum(h.shape[0] for h in head)
    base = -(-n_head // 8) * 8
    pack = jnp.concatenate(head + [jnp.zeros((base - n_head, d), F32), dpw.reshape(npw, d)], axis=0)

    p_wg1, [(packs,)] = _weight_grad(da1, h1, "ffn1_gate_grad", [_all_gather_small_cargo(pack)])
    p_wu1, [(x_wg1,)] = _weight_grad(db1, h1, "ffn1_up_grad", [_exchange_cargo([p_wg1])])
    p_wd1, [(x_wu1,)] = _weight_grad(s1, do1, "ffn1_down_grad", [_exchange_cargo([p_wu1])])
    p_win, [(x_wd1,)] = _weight_grad(dproj, h2, "w_in_grad", [_exchange_cargo([p_wd1])])
    p_wout, [(x_win,)] = _weight_grad(ymix, dx2b, "w_out_grad", [_exchange_cargo([p_win])])
    x_wout, = _run_cargo(_exchange_cargo([p_wout]), "grad_exchange_last")
    small = _sum_by_device(packs)
    loss = jnp.sum(small[n_head - 1]) * (0.5 / d)

    order = ["wg1", "wu1", "wd1", "win", "wout", "wg2", "wu2", "wd2"]
    pairs = dict(wg1=p_wg1, wu1=p_wu1, wd1=p_wd1, win=p_win, wout=p_wout, wg2=p_wg2, wu2=p_wu2, wd2=p_wd2)
    landed = dict(wg1=x_wg1, wu1=x_wu1, wd1=x_wd1, win=x_win, wout=x_wout, wg2=x_wg2, wu2=x_wu2, wd2=x_wd2)
    both = _sibling_share([_chip_sum(pairs[k], landed[k], place, k) for k in order])
    rwg1, rwu1, rwd1, rwin, rwout, rwg2, rwu2, rwd2 = [b.reshape(2 * b.shape[1], b.shape[2]) for b in both]

    grads = {
        "norm_ffn1": small[0:1], "norm_mix": small[1:2], "norm_ffn2": small[2:3], "norm_final": small[3],
        "pool_scale": small[4:5, :dc],
        "conv_w": lax.dynamic_slice_in_dim(small[5:5 + dcw.shape[0], :dc], chip * cshard, cshard, axis=1)[None],
        "pool_w": small[base:].reshape(pool_w.shape),
        "ffn1_w_down": rwd1[None], "w_out": rwout[None], "ffn2_w_down": rwd2[None],
    }
    by_view = {"ffn1_w_gate": rwg1, "ffn1_w_up": rwu1, "ffn2_w_gate": rwg2, "ffn2_w_up": rwu2}

    deltas, new_m, new_v = {}, {}, {}
    for n in names:
        w = weights[n]
        shape = w.shape
        if n == "w_in":
            grads[n], deltas[n], new_m[n], new_v[n] = _adamw_transposed(w, rwin, first_m[n], second_m[n], "adamw_" + n)
            continue
        if n in by_view:
            view = lambda a: jnp.swapaxes(a, 1, 2)[0]
            back = lambda a: jnp.swapaxes(a[None], 1, 2)
            dl, mo, vo = _adamw(view(w), by_view[n], view(first_m[n]), view(second_m[n]), "adamw_" + n)
            grads[n], deltas[n], new_m[n], new_v[n] = back(by_view[n]), back(dl), back(mo), back(vo)
            continue
        as2d = (lambda a: a.reshape(-1, shape[-1]))
        dl, mo, vo = _adamw(as2d(w), as2d(grads[n]), as2d(first_m[n]), as2d(second_m[n]), "adamw_" + n)
        deltas[n], new_m[n], new_v[n] = dl.reshape(shape), mo.reshape(shape), vo.reshape(shape)
        grads[n] = grads[n].reshape(shape)

    return (loss, dx0[None], *[grads[n] for n in names], *[deltas[n] for n in names],
            *[new_m[n] for n in names], *[new_v[n] for n in names])
```

```python
import jax
import jax.numpy as jnp
from jax import lax
from jax.experimental import pallas as pl
from jax.experimental.pallas import tpu as pltpu

F32 = jnp.float32
BF16 = jnp.bfloat16
MESH = pl.DeviceIdType.MESH

EPS = 1e-6
POOL_WINDOWS = (2, 4, 8, 16)
ADAM_LR = 0.001
ADAM_B1 = 0.9
ADAM_B2 = 0.999
ADAM_EPS = 1e-08
ADAM_WD = 0.01
ADAM_STEP = 10

N_CHIPS = 4
N_DEVICES = 8
MXU_COLS_V7X = 256
VMEM_LIMIT = 56 * 1024 * 1024
TM_FFN = 512
TM_MIX = 512
TM_TN = 1024
HALO = 32
WINDOW_LEVELS = 3
FFN_FWD_CHUNKS = 2
FFN_BWD_CHUNKS = 2


def _nt(a, b):
    return lax.dot_general(a, b, (((1,), (1,)), ((), ())), preferred_element_type=F32)


def _tn(a, b):
    return lax.dot_general(a, b, (((0,), (0,)), ((), ())), preferred_element_type=F32)


def _nn(a, b):
    return jnp.dot(a, b, preferred_element_type=F32)


def _sigmoid(a):
    return 1.0 / (1.0 + jnp.exp(-a))


def _feature_chunks(n, parts):
    assert n % MXU_COLS_V7X == 0
    tiles = n // MXU_COLS_V7X
    out, s0 = [], 0
    for p in range(parts):
        sz = (tiles // parts + (1 if p < tiles % parts else 0)) * MXU_COLS_V7X
        if sz:
            out.append((s0, sz))
            s0 += sz
    return out


def _row_block(rows, cap):
    best = 8
    for b in range(8, min(rows, cap) + 1, 8):
        if rows % b == 0:
            best = b
    assert rows % best == 0
    return best


def _my_place():
    return lax.axis_index("x"), lax.axis_index("y"), lax.axis_index("c")


def _other_chips(x, y):
    return [(1 - x, y), (x, 1 - y), (1 - x, 1 - y)]


HBM_SPEC = pl.BlockSpec(memory_space=pltpu.HBM)


class _Cargo:
    def __init__(self, operands, out_shapes, n_sems, phases, when):
        self.operands, self.out_shapes, self.n_sems = list(operands), list(out_shapes), n_sems
        self.phases, self.when = list(phases), list(when)
        assert len(self.phases) == len(self.when) and self.when[0] == 0.0 and self.when[-1] == 1.0


def _launch(body, *, name, grid, in_specs, out_specs, out_shape, scratch_shapes, args, cargo=()):
    params = pltpu.CompilerParams(dimension_semantics=("arbitrary",) * len(grid), vmem_limit_bytes=VMEM_LIMIT)
    cargos = list(cargo)
    c_operands = [op for cg in cargos for op in cg.operands]
    c_shapes = [sh for cg in cargos for sh in cg.out_shapes]
    counts = [len(in_specs), len(c_operands), len(out_shape), len(c_shapes), len(scratch_shapes), 2 * len(cargos)]

    def carrying(*refs):
        groups, pos = [], 0
        for k in counts:
            groups.append(refs[pos:pos + k])
            pos += k
        ins, c_ins, outs, c_outs, scratch, sems = groups
        parts, pi, po = [], 0, 0
        for n, cg in enumerate(cargos):
            parts.append((c_ins[pi:pi + len(cg.operands)], c_outs[po:po + len(cg.out_shapes)], sems[2 * n], sems[2 * n + 1]))
            pi += len(cg.operands)
            po += len(cg.out_shapes)
        step, steps = 0, 1
        for ax, g in enumerate(grid):
            step = step * g + pl.program_id(ax)
            steps *= g
        todo = {}
        for cg, part in zip(cargos, parts):
            for phase, frac in zip(cg.phases[:-1], cg.when[:-1]):
                todo.setdefault(int(round(frac * (steps - 1))), []).append((phase, part))

        for at in sorted(todo):
            @pl.when(step == at)
            def _(at=at):
                for phase, part in todo[at]:
                    phase(*part)

        body(*ins, *outs, *scratch)

        if cargos:
            @pl.when(step == steps - 1)
            def _():
                for cg, part in zip(cargos, parts):
                    cg.phases[-1](*part)

    sems = [pltpu.SemaphoreType.DMA((cg.n_sems,)) for cg in cargos for _ in range(2)]
    outs = pl.pallas_call(
        carrying, name=name, grid=grid,
        in_specs=list(in_specs) + [HBM_SPEC] * counts[1], out_specs=list(out_specs) + [HBM_SPEC] * counts[3],
        out_shape=list(out_shape) + c_shapes, scratch_shapes=list(scratch_shapes) + sems,
        compiler_params=params)(*args, *c_operands)
    own, rest = list(outs[:counts[2]]), list(outs[counts[2]:])
    carried, po = [], 0
    for cg in cargos:
        carried.append(rest[po:po + len(cg.out_shapes)])
        po += len(cg.out_shapes)
    return own, carried


def _run_cargo(cargo, name):
    n_in, n_out = len(cargo.operands), len(cargo.out_shapes)

    def body(*refs):
        c_ins, c_outs, sems = refs[:n_in], refs[n_in:n_in + n_out], refs[n_in + n_out:]
        for phase in cargo.phases:
            phase(c_ins, c_outs, *sems)

    sem = pltpu.SemaphoreType.DMA((cargo.n_sems,))
    return list(pl.pallas_call(body, name=name, out_shape=cargo.out_shapes, in_specs=[HBM_SPEC] * n_in,
                               out_specs=[HBM_SPEC] * n_out, scratch_shapes=[sem, sem])(*cargo.operands))


def _gather_cargo(shards):
    n = len(shards)
    for s in shards:
        assert s.shape[0] % 32 == 0
    slots = 8

    def steps(ins, outs, send_sems, recv_sems):
        x, y, c = _my_place()
        sibling = (x, y, 1 - c)
        over_x, over_y = (1 - x, y, c), (x, 1 - y, c)
        mine, chip_x, chip_y, chip_d = 2 * x + y, 2 * (1 - x) + y, 2 * x + (1 - y), 2 * (1 - x) + (1 - y)

        def rows_of(a, chip_index, half, part=None):
            rps = shards[a].shape[0]
            hr = rps // 2
            first = -(-hr // 32) * 16
            offset, size = {None: (0, hr), 0: (0, first), 1: (first, hr - first)}[part]
            return outs[a].at[pl.ds(pl.multiple_of(chip_index * rps + half * hr + offset, 16), size), :]

        def remote(a, slot, src, dst, to):
            return pltpu.make_async_remote_copy(
                src_ref=src, dst_ref=dst, send_sem=send_sems.at[a * slots + slot], recv_sem=recv_sems.at[a * slots + slot],
                device_id=to, device_id_type=MESH)

        def same_rows(a, slot, rows, to):
            return remote(a, slot, rows, rows, to)

        def own_copy(a):
            rps = shards[a].shape[0]
            return remote(a, 7, ins[a], outs[a].at[pl.ds(pl.multiple_of(mine * rps, 16), rps), :], sibling)

        def my_half(a):
            hr = shards[a].shape[0] // 2
            return ins[a].at[pl.ds(pl.multiple_of(c * hr, 16), hr), :]

        def start():
            for a in range(n):
                own_copy(a).start()
                remote(a, 0, my_half(a), rows_of(a, mine, c), over_x).start()
                remote(a, 1, my_half(a), rows_of(a, mine, c), over_y).start()

        def relay_neighbours():
            for a in range(n):
                same_rows(a, 0, rows_of(a, chip_x, c), over_x).wait_recv()
                same_rows(a, 4, rows_of(a, chip_x, c), sibling).start()
                same_rows(a, 2, rows_of(a, chip_x, c, 0), over_y).start()
                same_rows(a, 1, rows_of(a, chip_y, c), over_y).wait_recv()
                same_rows(a, 5, rows_of(a, chip_y, c), sibling).start()
                same_rows(a, 3, rows_of(a, chip_y, c, 1), over_x).start()

        def relay_diagonal():
            for a in range(n):
                same_rows(a, 2, rows_of(a, chip_d, c, 0), over_y).wait_recv()
                same_rows(a, 3, rows_of(a, chip_d, c, 1), over_x).wait_recv()
                same_rows(a, 6, rows_of(a, chip_d, c), sibling).start()

        def finish():
            for a in range(n):
                for slot, chip_index in ((4, chip_x), (5, chip_y), (6, chip_d)):
                    same_rows(a, slot, rows_of(a, chip_index, 1 - c), sibling).wait_recv()
            for a in range(n):
                remote(a, 0, my_half(a), rows_of(a, mine, c), over_x).wait_send()
                remote(a, 1, my_half(a), rows_of(a, mine, c), over_y).wait_send()
                same_rows(a, 2, rows_of(a, chip_x, c, 0), over_y).wait_send()
                same_rows(a, 3, rows_of(a, chip_y, c, 1), over_x).wait_send()
                for slot, chip_index in ((4, chip_x), (5, chip_y), (6, chip_d)):
                    same_rows(a, slot, rows_of(a, chip_index, c), sibling).wait_send()
                own_copy(a).wait()

        return start, relay_neighbours, relay_diagonal, finish

    phases = [lambda *r, k=k: steps(*r)[k]() for k in range(4)]
    return _Cargo(shards, [jax.ShapeDtypeStruct((N_CHIPS * s.shape[0], s.shape[1]), s.dtype) for s in shards], slots * n,
                  phases, [0.0, 0.6, 0.85, 1.0])


def _exchange_cargo(pairs):
    n = len(pairs)

    def copies(ins, outs, send_sems, recv_sems):
        x, y, c = _my_place()
        return [pltpu.make_async_remote_copy(
            src_ref=ins[a].at[2 * chip[0] + chip[1]], dst_ref=outs[a].at[j],
            send_sem=send_sems.at[3 * a + j], recv_sem=recv_sems.at[3 * a + j], device_id=(*chip, c), device_id_type=MESH)
            for a in range(n) for j, chip in enumerate(_other_chips(x, y))]

    def start(*r):
        for cp in copies(*r):
            cp.start()

    def finish(*r):
        for cp in copies(*r):
            cp.wait()

    return _Cargo(pairs, [jax.ShapeDtypeStruct((3,) + p.shape[1:], p.dtype) for p in pairs], 3 * n, [start, finish], [0.0, 1.0])


def _all_gather_small_cargo(pack):
    rows, cols = pack.shape

    def copies(ins, outs, send_sems, recv_sems):
        x, y, c = _my_place()
        me = 4 * x + 2 * y + c
        remote = []
        for f in range(1, N_DEVICES):
            fx, fy, fc = (f >> 2) & 1, (f >> 1) & 1, f & 1
            to = (1 - x if fx else x, 1 - y if fy else y, 1 - c if fc else c)
            remote.append(pltpu.make_async_remote_copy(
                src_ref=ins[0], dst_ref=outs[0].at[me], send_sem=send_sems.at[f - 1], recv_sem=recv_sems.at[f - 1],
                device_id=to, device_id_type=MESH))
        own = pltpu.make_async_copy(ins[0], outs[0].at[me], send_sems.at[N_DEVICES - 1])
        return remote, own

    def start(*r):
        remote, own = copies(*r)
        own.start()
        for cp in remote:
            cp.start()

    def finish(*r):
        remote, own = copies(*r)
        for cp in remote:
            cp.wait()
        own.wait()

    return _Cargo([pack], [jax.ShapeDtypeStruct((N_DEVICES, rows, cols), F32)], N_DEVICES, [start, finish], [0.0, 1.0])


def _sum_by_device(packs):
    n, rows, cols = packs.shape

    def body(p_ref, o_ref):
        acc = p_ref[0]
        for dev in range(1, n):
            acc = acc + p_ref[dev]
        o_ref[...] = acc

    return pl.pallas_call(body, name="small_grads_sum", out_shape=jax.ShapeDtypeStruct((rows, cols), F32))(packs)


def _chip_sum(pair, got, place, tag):
    _, hr, cols = pair.shape
    br = _row_block(hr, 256)

    def body(k_ref, p_ref, r_ref, o_ref):
        acc = p_ref[...].astype(F32)
        for j in range(3):
            acc = acc + r_ref[j].astype(F32)
        o_ref[...] = acc

    return pl.pallas_call(
        body, name="grad_chip_sum_" + tag,
        out_shape=jax.ShapeDtypeStruct((2, hr, cols), F32),
        grid_spec=pltpu.PrefetchScalarGridSpec(
            num_scalar_prefetch=1, grid=(hr // br,),
            in_specs=[pl.BlockSpec((None, br, cols), lambda r, k_ref: (k_ref[0], r, 0)),
                      pl.BlockSpec((3, br, cols), lambda r, k_ref: (0, r, 0))],
            out_specs=pl.BlockSpec((None, br, cols), lambda r, k_ref: (k_ref[1], r, 0))),
        compiler_params=pltpu.CompilerParams(dimension_semantics=("parallel",)),
    )(place, pair, got)


def _sibling_share(halves):
    n = len(halves)

    def body(*refs):
        outs = refs[n:2 * n]
        send_sems, recv_sems = refs[2 * n:]
        x, y, c = _my_place()
        copies = []
        for a in range(n):
            cp = pltpu.make_async_remote_copy(
                src_ref=outs[a].at[c], dst_ref=outs[a].at[c], send_sem=send_sems.at[a], recv_sem=recv_sems.at[a],
                device_id=(x, y, 1 - c), device_id_type=MESH)
            cp.start()
            copies.append(cp)
        for cp in copies:
            cp.wait()

    return pl.pallas_call(
        body, name="grad_share_sibling",
        out_shape=[jax.ShapeDtypeStruct(h.shape, h.dtype) for h in halves],
        in_specs=[HBM_SPEC] * n, out_specs=[HBM_SPEC] * n,
        input_output_aliases={a: a for a in range(n)},
        scratch_shapes=[pltpu.SemaphoreType.DMA((n,)), pltpu.SemaphoreType.DMA((n,))],
    )(*halves)


def _load_rows(pairs, sems):
    cps = [pltpu.make_async_copy(src, dst, sems.at[j]) for j, (src, dst) in enumerate(pairs)]
    for cp in cps:
        cp.start()
    for cp in cps:
        cp.wait()


def _piece_rows(weights):
    flat = [p for pieces in weights for p in pieces]

    def copies(refs, mats):
        out, n = [], 0
        for pieces, mat in zip(weights, mats):
            rps = sum(p.shape[0] for p in pieces) // N_CHIPS
            off = 0
            for p in pieces:
                r = p.shape[0] // N_CHIPS
                if len(pieces) == 1:
                    out.append((refs[n], mat))
                else:
                    for k in range(N_CHIPS):
                        out.append((refs[n].at[pl.ds(k * r, r), :], mat.at[pl.ds(k * rps + off, r), :]))
                off += r
                n += 1
        return out

    n_copies = sum(1 if len(pieces) == 1 else N_CHIPS * len(pieces) for pieces in weights)
    return flat, copies, n_copies


def _loss_head(xv, gv, tv):
    d = xv.shape[-1]
    r = lax.rsqrt(jnp.mean(xv * xv, axis=-1, keepdims=True) + EPS)
    xhat = xv * r
    err = xhat * gv - tv
    dy = err * (1.0 / d)
    dxh = dy * gv
    dx = r * (dxh - xhat * jnp.mean(dxh * xhat, axis=-1, keepdims=True))
    return dx, jnp.sum(err * err, axis=0, keepdims=True), jnp.sum(dy * xhat, axis=0, keepdims=True)


def _ffn_up(x, g, wg_t, wu_t, name, cargo=()):
    t, d = x.shape
    f = sum(p.shape[0] for p in wg_t)
    tm = min(TM_FFN, t)
    chunks = _feature_chunks(f, FFN_FWD_CHUNKS)
    flat, copies, n_copies = _piece_rows([wg_t, wu_t])
    nw = len(flat)

    def body(x_ref, g_ref, *rest):
        w_hbm, (a_ref, b_ref, s_ref, wg, wu, sems) = rest[:nw], rest[nw:]

        @pl.when(pl.program_id(0) == 0)
        def _():
            _load_rows(copies(w_hbm, [wg, wu]), sems)

        xv = x_ref[...]
        r = lax.rsqrt(jnp.mean(xv * xv, axis=-1, keepdims=True) + EPS)
        h = (xv * r * g_ref[...]).astype(BF16)
        for s0, sz in chunks:
            a = _nt(h, wg[s0:s0 + sz, :])
            b = _nt(h, wu[s0:s0 + sz, :])
            a_ref[:, s0:s0 + sz] = a.astype(BF16)
            b_ref[:, s0:s0 + sz] = b.astype(BF16)
            s_ref[:, s0:s0 + sz] = (a * _sigmoid(a) * b).astype(BF16)

    tok = lambda i: (i, 0)
    wide = pl.BlockSpec((tm, f), tok)
    return _launch(
        body, name=name, grid=(t // tm,),
        in_specs=[pl.BlockSpec((tm, d), tok), pl.BlockSpec((1, d), lambda i: (0, 0))] + [HBM_SPEC] * nw,
        out_specs=[wide, wide, wide], out_shape=[jax.ShapeDtypeStruct((t, f), BF16)] * 3,
        scratch_shapes=[pltpu.VMEM((f, d), BF16), pltpu.VMEM((f, d), BF16), pltpu.SemaphoreType.DMA((n_copies,))],
        args=(x, g, *flat), cargo=cargo)


def _ffn_down(x, s, wd, name, cargo=(), loss_head=None):
    t, d = x.shape
    f = s.shape[1]
    tm = min(TM_FFN, t)
    flat, copies, n_copies = _piece_rows([wd])
    nw = len(flat)
    nl = 2 if loss_head else 0

    def body(x_ref, s_ref, *rest):
        head, w_hbm = rest[:nl], rest[nl:nl + nw]
        xo_ref = rest[nl + nw]
        sums, (wdn, sems) = rest[nl + nw + 1:nl + nw + 1 + nl], rest[nl + nw + 1 + nl:]

        @pl.when(pl.program_id(0) == 0)
        def _():
            _load_rows(copies(w_hbm, [wdn]), sems)
            for sum_ref in sums:
                sum_ref[...] = jnp.zeros_like(sum_ref)

        xo = x_ref[...] + 0.5 * _nn(s_ref[...], wdn[...])
        if loss_head:
            dx, sq, dgf = _loss_head(xo, head[0][...], head[1][...])
            xo_ref[...] = dx
            sums[0][...] += sq
            sums[1][...] += dgf
        else:
            xo_ref[...] = xo

    tok = lambda i: (i, 0)
    one = lambda i: (0, 0)
    return _launch(
        body, name=name, grid=(t // tm,),
        in_specs=[pl.BlockSpec((tm, d), tok), pl.BlockSpec((tm, f), tok)]
        + ([pl.BlockSpec((1, d), one), pl.BlockSpec((tm, d), tok)] if loss_head else []) + [HBM_SPEC] * nw,
        out_specs=[pl.BlockSpec((tm, d), tok)] + [pl.BlockSpec((1, d), one)] * nl,
        out_shape=[jax.ShapeDtypeStruct((t, d), F32)] + [jax.ShapeDtypeStruct((1, d), F32)] * nl,
        scratch_shapes=[pltpu.VMEM((f, d), BF16), pltpu.SemaphoreType.DMA((n_copies,))],
        args=(x, s, *(loss_head or ()), *flat), cargo=cargo)


def _ffn_backward(dxo, x, g, a, b, wg_t, wu_t, wd, name, cargo=()):
    t, d = x.shape
    f = sum(p.shape[0] for p in wd)
    tm = min(TM_FFN // 2, t)
    chunks = _feature_chunks(f, FFN_BWD_CHUNKS)
    flat, copies, n_copies = _piece_rows([wg_t, wu_t, wd])
    nw = len(flat)

    def body(dxo_ref, x_ref, g_ref, a_ref, b_ref, *rest):
        w_hbm, (dx_ref, da_ref, db_ref, h_ref, do_ref, dg_ref, wg, wu, wdn, sems) = rest[:nw], rest[nw:]

        @pl.when(pl.program_id(0) == 0)
        def _():
            _load_rows(copies(w_hbm, [wg, wu, wdn]), sems)
            dg_ref[...] = jnp.zeros_like(dg_ref)

        xv = x_ref[...]
        gv = g_ref[...]
        r = lax.rsqrt(jnp.mean(xv * xv, axis=-1, keepdims=True) + EPS)
        xhat = xv * r
        h_ref[...] = (xhat * gv).astype(BF16)
        dxo_v = dxo_ref[...]
        dout = (0.5 * dxo_v).astype(BF16)
        do_ref[...] = dout
        dh = jnp.zeros((tm, d), F32)
        for s0, sz in chunks:
            ds = _nt(dout, wdn[s0:s0 + sz, :])
            av = a_ref[:, s0:s0 + sz].astype(F32)
            bv = b_ref[:, s0:s0 + sz].astype(F32)
            sig = _sigmoid(av)
            silu = av * sig
            da = (ds * bv * (sig * (1.0 + av * (1.0 - sig)))).astype(BF16)
            db = (ds * silu).astype(BF16)
            da_ref[:, s0:s0 + sz] = da
            db_ref[:, s0:s0 + sz] = db
            dh = dh + _nn(da, wg[s0:s0 + sz, :]) + _nn(db, wu[s0:s0 + sz, :])
        dg_ref[...] += jnp.sum(dh * xhat, axis=0, keepdims=True)
        dxh = dh * gv
        dx_ref[...] = dxo_v + r * (dxh - xhat * jnp.mean(dxh * xhat, axis=-1, keepdims=True))

    tok = lambda i: (i, 0)
    one = lambda i: (0, 0)
    return _launch(
        body, name=name, grid=(t // tm,),
        in_specs=[pl.BlockSpec((tm, d), tok), pl.BlockSpec((tm, d), tok), pl.BlockSpec((1, d), one),
                  pl.BlockSpec((tm, f), tok), pl.BlockSpec((tm, f), tok)] + [HBM_SPEC] * nw,
        out_specs=[pl.BlockSpec((tm, d), tok), pl.BlockSpec((tm, f), tok), pl.BlockSpec((tm, f), tok),
                   pl.BlockSpec((tm, d), tok), pl.BlockSpec((tm, d), tok), pl.BlockSpec((1, d), one)],
        out_shape=[jax.ShapeDtypeStruct((t, d), F32), jax.ShapeDtypeStruct((t, f), BF16), jax.ShapeDtypeStruct((t, f), BF16),
                   jax.ShapeDtypeStruct((t, d), BF16), jax.ShapeDtypeStruct((t, d), BF16), jax.ShapeDtypeStruct((1, d), F32)],
        scratch_shapes=[pltpu.VMEM((f, d), BF16), pltpu.VMEM((f, d), BF16), pltpu.VMEM((f, d), BF16), pltpu.SemaphoreType.DMA((n_copies,))],
        args=(dxo, x, g, a, b, *flat), cargo=cargo)


def _weight_grad(lhs, rhs, name, cargo=()):
    t, m = lhs.shape
    d = rhs.shape[1]
    tm = min(TM_TN, t)
    nt = t // tm
    nj = 1
    bm = m // nj
    cpb = N_CHIPS // nj
    rps = m // N_CHIPS
    hr = rps // 2
    assert hr % 16 == 0

    def body(l_ref, r_ref, o_ref, acc, stage, recv, send_sems, recv_sems):
        j = pl.program_id(0)
        i = pl.program_id(1)
        @pl.when(i == 0)
        def _():
            acc[...] = jnp.zeros_like(acc)

        acc[...] += _tn(l_ref[...], r_ref[...])

        def pair_sum(jj):
            x, y, c = _my_place()
            copies = []
            for q in range(cpb):
                slot = jj * cpb + q
                stage[slot] = acc[pl.ds(pl.multiple_of(q * rps + (1 - c) * hr, 16), hr), :].astype(BF16)
                cp = pltpu.make_async_remote_copy(
                    src_ref=stage.at[slot], dst_ref=recv.at[slot], send_sem=send_sems.at[slot], recv_sem=recv_sems.at[slot],
                    device_id=(x, y, 1 - c), device_id_type=MESH)
                cp.start()
                copies.append(cp)
            for q, cp in enumerate(copies):
                cp.wait_recv()
                mine = acc[pl.ds(pl.multiple_of(q * rps + c * hr, 16), hr), :]
                o_ref[q] = (mine + recv[jj * cpb + q].astype(F32)).astype(BF16)
            for cp in copies:
                cp.wait_send()

        for jj in range(nj):
            @pl.when(jnp.logical_and(i == nt - 1, j == jj))
            def _():
                pair_sum(jj)

    outs, carried = _launch(
        body, name=name, grid=(nj, nt),
        in_specs=[pl.BlockSpec((tm, bm), lambda j, i: (i, j)), pl.BlockSpec((tm, d), lambda j, i: (i, 0))],
        out_specs=[pl.BlockSpec((cpb, hr, d), lambda j, i: (j, 0, 0))],
        out_shape=[jax.ShapeDtypeStruct((N_CHIPS, hr, d), BF16)],
        scratch_shapes=[pltpu.VMEM((bm, d), F32), pltpu.VMEM((N_CHIPS, hr, d), BF16), pltpu.VMEM((N_CHIPS, hr, d), BF16),
                        pltpu.SemaphoreType.DMA((N_CHIPS,)), pltpu.SemaphoreType.DMA((N_CHIPS,))],
        args=(lhs, rhs), cargo=cargo)
    return outs[0], carried


def _window_sums(src, cols, w, tm, levels, trailing):
    def read_src(lo, hi):
        return src[lo:hi, cols]

    read, k, level = read_src, 1, 0
    while True:
        last = 2 * k == w
        if trailing:
            lo, hi = (HALO if last else 8 * (level + 1)), HALO + tm
            cur = read(lo, hi) + read(lo - k, hi - k)
        else:
            lo, hi = 0, (tm if last else tm + HALO - 8 * (level + 1))
            cur = read(lo, hi) + read(lo + k, hi + k)
        if last:
            return cur
        levels[level, lo:hi, :] = cur
        read = lambda a, b, level=level: levels[level, a:b, :]
        k, level = 2 * k, level + 1


def _pool_parts(u_cols, ubuf, cols, w, row, tm, levels):
    ws = _window_sums(ubuf, cols, w, tm, levels, trailing=True)
    cnt = jnp.minimum(row + 1, w).astype(F32)
    return ws / cnt - u_cols, cnt


def _mixer_forward(x, g, win_t, wout_x, conv_w, pool_w, pool_scale, cargo=()):
    t, d = x.shape
    dc = win_t.shape[0] // 4
    gcw = dc // len(POOL_WINDOWS)
    wo_rows = d // N_CHIPS
    wo_stride = wout_x.shape[0] // N_CHIPS
    tm = min(TM_MIX, t)

    def body(x_ref, g_ref, win_hbm, wout_hbm, cw_ref, pw_ref, ps_ref, xo_ref, proj_ref, y_ref,
             win, wout, zbuf, ubuf, levels, sems):
        i = pl.program_id(0)

        @pl.when(i == 0)
        def _():
            pairs = [(win_hbm, win)]
            for k in range(N_CHIPS):
                pairs.append((wout_hbm.at[pl.ds(k * wo_stride, wo_rows), :], wout.at[pl.ds(k * wo_rows, wo_rows), :]))
            _load_rows(pairs, sems)
            zbuf[0:8, :] = jnp.zeros((8, dc), F32)
            ubuf[0:HALO, :] = jnp.zeros((HALO, dc), F32)

        xv = x_ref[...]
        r = lax.rsqrt(jnp.mean(xv * xv, axis=-1, keepdims=True) + EPS)
        h = (xv * r * g_ref[...]).astype(BF16)
        v = _nt(h, win[0:dc, :])
        gb = _nt(h, win[dc:2 * dc, :])
        gc = _nt(h, win[2 * dc:3 * dc, :])
        u = _nt(h, win[3 * dc:4 * dc, :])
        proj_ref[:, 0:dc] = v.astype(BF16)
        proj_ref[:, dc:2 * dc] = gb.astype(BF16)
        proj_ref[:, 2 * dc:3 * dc] = gc.astype(BF16)
        proj_ref[:, 3 * dc:4 * dc] = u.astype(BF16)

        z = gc * v
        zbuf[8:8 + tm, :] = z
        cw = cw_ref[...]
        conv = cw[2:3, :] * z + cw[1:2, :] * zbuf[7:7 + tm, :] + cw[0:1, :] * zbuf[6:6 + tm, :]
        y_ref[:, 0:dc] = (gb * conv).astype(BF16)

        ubuf[HALO:HALO + tm, :] = u
        row = i * tm + lax.broadcasted_iota(jnp.int32, (tm, 1), 0)
        for gi, w in enumerate(POOL_WINDOWS):
            cols = slice(gi * gcw, (gi + 1) * gcw)
            pooled, _ = _pool_parts(u[:, cols], ubuf, cols, w, row, tm, levels)
            yb = _nn(pooled.astype(BF16), pw_ref[gi].astype(BF16)) * ps_ref[:, cols]
            y_ref[:, dc + gi * gcw:dc + (gi + 1) * gcw] = yb.astype(BF16)

        xo_ref[...] = xv + _nn(y_ref[...], wout[...])
        zbuf[0:8, :] = zbuf[tm:tm + 8, :]
        ubuf[0:HALO, :] = ubuf[tm:tm + HALO, :]

    tok = lambda i: (i, 0)
    one = lambda i: (0, 0)
    return _launch(
        body, name="mixer_forward", grid=(t // tm,),
        in_specs=[pl.BlockSpec((tm, d), tok), pl.BlockSpec((1, d), one), HBM_SPEC, HBM_SPEC,
                  pl.BlockSpec(conv_w.shape, one), pl.BlockSpec(pool_w.shape, lambda i: (0, 0, 0)), pl.BlockSpec((1, dc), one)],
        out_specs=[pl.BlockSpec((tm, d), tok), pl.BlockSpec((tm, 4 * dc), tok), pl.BlockSpec((tm, 2 * dc), tok)],
        out_shape=[jax.ShapeDtypeStruct((t, d), F32), jax.ShapeDtypeStruct((t, 4 * dc), BF16), jax.ShapeDtypeStruct((t, 2 * dc), BF16)],
        scratch_shapes=[pltpu.VMEM((4 * dc, d), BF16), pltpu.VMEM((2 * dc, d), BF16),
                        pltpu.VMEM((tm + 8, dc), F32), pltpu.VMEM((tm + HALO, dc), F32),
                        pltpu.VMEM((WINDOW_LEVELS, tm + HALO, gcw), F32), pltpu.SemaphoreType.DMA((1 + N_CHIPS,))],
        args=(x, g, win_t, wout_x, conv_w, pool_w, pool_scale), cargo=cargo)


def _mixer_backward(dxo, x, g, proj, win_t, wout_x, conv_w, pool_w, pool_scale, cargo=()):
    t, d = x.shape
    dc = win_t.shape[0] // 4
    ng = len(POOL_WINDOWS)
    gcw = dc // ng
    wo_rows = d // N_CHIPS
    wo_stride = wout_x.shape[0] // N_CHIPS
    tm = min(TM_MIX, t)
    n_tiles = t // tm
    hb = tm // HALO

    def body(dxo_ref, x_ref, g_ref, proj_ref, halo_ref, win_hbm, wout_hbm, cw_ref, pw_ref, ps_ref,
             dx_ref, dproj_ref, h_ref, dxob_ref, dg_ref, dcw_ref, dps_ref, dpw_ref,
             win, wout, zbuf, ubuf, dcbuf, ebuf, levels, sems):
        i = pl.program_id(0)
        tile = n_tiles - 1 - i

        @pl.when(i == 0)
        def _():
            pairs = [(win_hbm, win)]
            for k in range(N_CHIPS):
                pairs.append((wout_hbm.at[pl.ds(k * wo_stride, wo_rows), :], wout.at[pl.ds(k * wo_rows, wo_rows), :]))
            _load_rows(pairs, sems)
            dcbuf[tm:tm + 8, :] = jnp.zeros((8, dc), F32)
            ebuf[tm:tm + HALO, :] = jnp.zeros((HALO, dc), F32)
            dg_ref[...] = jnp.zeros_like(dg_ref)
            dcw_ref[...] = jnp.zeros_like(dcw_ref)
            dps_ref[...] = jnp.zeros_like(dps_ref)
            dpw_ref[...] = jnp.zeros_like(dpw_ref)

        xv = x_ref[...]
        gv = g_ref[...]
        r = lax.rsqrt(jnp.mean(xv * xv, axis=-1, keepdims=True) + EPS)
        xhat = xv * r
        h_ref[...] = (xhat * gv).astype(BF16)
        dxo_v = dxo_ref[...]
        dxo_b = dxo_v.astype(BF16)
        dxob_ref[...] = dxo_b

        v = proj_ref[:, 0:dc].astype(F32)
        gb = proj_ref[:, dc:2 * dc].astype(F32)
        gc = proj_ref[:, 2 * dc:3 * dc].astype(F32)
        u = proj_ref[:, 3 * dc:4 * dc].astype(F32)
        first = jnp.where(tile > 0, 1.0, 0.0)
        zbuf[0:HALO, :] = halo_ref[:, 2 * dc:3 * dc].astype(F32) * halo_ref[:, 0:dc].astype(F32) * first
        ubuf[0:HALO, :] = halo_ref[:, 3 * dc:4 * dc].astype(F32) * first
        z = gc * v
        zbuf[HALO:HALO + tm, :] = z
        ubuf[HALO:HALO + tm, :] = u
        z1 = zbuf[HALO - 1:HALO - 1 + tm, :]
        z2 = zbuf[HALO - 2:HALO - 2 + tm, :]
        cw = cw_ref[...]
        conv = cw[2:3, :] * z + cw[1:2, :] * z1 + cw[0:1, :] * z2

        dy = _nt(dxo_b, wout[...])
        dya = dy[:, 0:dc]
        dgb = dya * conv
        dconv = dya * gb
        dcbuf[0:tm, :] = dconv
        dz = cw[2:3, :] * dconv + cw[1:2, :] * dcbuf[1:1 + tm, :] + cw[0:1, :] * dcbuf[2:2 + tm, :]
        dgc = dz * v
        dv = dz * gc
        dcw_ref[0:1, :] += jnp.sum(dconv * z2, axis=0, keepdims=True)
        dcw_ref[1:2, :] += jnp.sum(dconv * z1, axis=0, keepdims=True)
        dcw_ref[2:3, :] += jnp.sum(dconv * z, axis=0, keepdims=True)

        dproj_ref[:, 0:dc] = dv.astype(BF16)
        dproj_ref[:, dc:2 * dc] = dgb.astype(BF16)
        dproj_ref[:, 2 * dc:3 * dc] = dgc.astype(BF16)

        row = tile * tm + lax.broadcasted_iota(jnp.int32, (tm, 1), 0)
        for gi, w in enumerate(POOL_WINDOWS):
            cols = slice(gi * gcw, (gi + 1) * gcw)
            pooled, cnt = _pool_parts(u[:, cols], ubuf, cols, w, row, tm, levels)
            pooled_b = pooled.astype(BF16)
            pw_b = pw_ref[gi].astype(BF16)
            dyb = dy[:, dc + gi * gcw:dc + (gi + 1) * gcw]
            q = _nn(pooled_b, pw_b)
            dps_ref[:, cols] += jnp.sum(q * dyb, axis=0, keepdims=True)
            dq = (dyb * ps_ref[:, cols]).astype(BF16)
            dpw_ref[gi] += _tn(pooled_b, dq)
            dpooled = _nt(dq, pw_b)
            ebuf[0:tm, cols] = dpooled / cnt
            du = _window_sums(ebuf, cols, w, tm, levels, trailing=False) - dpooled
            dproj_ref[:, 3 * dc + gi * gcw:3 * dc + (gi + 1) * gcw] = du.astype(BF16)

        dh = _nn(dproj_ref[...], win[...])
        dg_ref[...] += jnp.sum(dh * xhat, axis=0, keepdims=True)
        dxh = dh * gv
        dx_ref[...] = dxo_v + r * (dxh - xhat * jnp.mean(dxh * xhat, axis=-1, keepdims=True))
        dcbuf[tm:tm + 8, :] = dcbuf[0:8, :]
        ebuf[tm:tm + HALO, :] = ebuf[0:HALO, :]

    tok = lambda i: (n_tiles - 1 - i, 0)
    halo = lambda i: (jnp.maximum((n_tiles - 1 - i) * hb - 1, 0), 0)
    one = lambda i: (0, 0)
    return _launch(
        body, name="mixer_backward", grid=(n_tiles,),
        in_specs=[pl.BlockSpec((tm, d), tok), pl.BlockSpec((tm, d), tok), pl.BlockSpec((1, d), one),
                  pl.BlockSpec((tm, 4 * dc), tok), pl.BlockSpec((HALO, 4 * dc), halo), HBM_SPEC, HBM_SPEC,
                  pl.BlockSpec(conv_w.shape, one), pl.BlockSpec(pool_w.shape, lambda i: (0, 0, 0)), pl.BlockSpec((1, dc), one)],
        out_specs=[pl.BlockSpec((tm, d), tok), pl.BlockSpec((tm, 4 * dc), tok), pl.BlockSpec((tm, d), tok), pl.BlockSpec((tm, d), tok),
                   pl.BlockSpec((1, d), one), pl.BlockSpec(conv_w.shape, one), pl.BlockSpec((1, dc), one),
                   pl.BlockSpec(pool_w.shape, lambda i: (0, 0, 0))],
        out_shape=[jax.ShapeDtypeStruct((t, d), F32), jax.ShapeDtypeStruct((t, 4 * dc), BF16), jax.ShapeDtypeStruct((t, d), BF16),
                   jax.ShapeDtypeStruct((t, d), BF16), jax.ShapeDtypeStruct((1, d), F32), jax.ShapeDtypeStruct(conv_w.shape, F32),
                   jax.ShapeDtypeStruct((1, dc), F32), jax.ShapeDtypeStruct(pool_w.shape, F32)],
        scratch_shapes=[pltpu.VMEM((4 * dc, d), BF16), pltpu.VMEM((2 * dc, d), BF16),
                        pltpu.VMEM((tm + HALO, dc), F32), pltpu.VMEM((tm + HALO, dc), F32),
                        pltpu.VMEM((tm + 8, dc), F32), pltpu.VMEM((tm + HALO, dc), F32),
                        pltpu.VMEM((WINDOW_LEVELS, tm + HALO, gcw), F32), pltpu.SemaphoreType.DMA((1 + N_CHIPS,))],
        args=(dxo, x, g, proj, proj, win_t, wout_x, conv_w, pool_w, pool_scale), cargo=cargo)


def _adam_update(w, gv, m, v):
    m_new = ADAM_B1 * m + (1.0 - ADAM_B1) * gv
    v_new = ADAM_B2 * v + (1.0 - ADAM_B2) * (gv * gv)
    m_hat = m_new / (1.0 - ADAM_B1 ** ADAM_STEP)
    v_hat = v_new / (1.0 - ADAM_B2 ** ADAM_STEP)
    return -ADAM_LR * (m_hat / (jnp.sqrt(v_hat) + ADAM_EPS) + ADAM_WD * w), m_new, v_new


def _adamw(w, grad, m, v, name):
    rows, cols = w.shape
    br = _row_block(rows, 256) if rows >= 8 else rows

    def body(w_ref, g_ref, m_ref, v_ref, d_ref, mo_ref, vo_ref):
        d_ref[...], mo_ref[...], vo_ref[...] = _adam_update(w_ref[...], g_ref[...], m_ref[...], v_ref[...])

    blk = pl.BlockSpec((br, cols), lambda i: (i, 0))
    return pl.pallas_call(
        body, name=name,
        out_shape=[jax.ShapeDtypeStruct((rows, cols), F32)] * 3,
        grid=(rows // br,), in_specs=[blk] * 4, out_specs=[blk] * 3,
        compiler_params=pltpu.CompilerParams(dimension_semantics=("parallel",)),
    )(w, grad, m, v)


def _adamw_transposed(w, grad_t, m, v, name):
    _, rows, cols = w.shape
    br = 256 if rows % 256 == 0 else rows

    def body(w_ref, gt_ref, m_ref, v_ref, g_ref, d_ref, mo_ref, vo_ref):
        gv = gt_ref[...].T
        g_ref[...] = gv
        d_ref[...], mo_ref[...], vo_ref[...] = _adam_update(w_ref[...], gv, m_ref[...], v_ref[...])

    blk = pl.BlockSpec((None, br, cols), lambda i: (0, i, 0))
    return pl.pallas_call(
        body, name=name,
        out_shape=[jax.ShapeDtypeStruct((1, rows, cols), F32)] * 4,
        grid=(rows // br,), in_specs=[blk, pl.BlockSpec((cols, br), lambda i: (0, i)), blk, blk], out_specs=[blk] * 4,
        compiler_params=pltpu.CompilerParams(dimension_semantics=("parallel",)),
    )(w, grad_t, m, v)


def _f32_rows_as_bf16(a, rows, cols):
    bits = lax.bitcast_convert_type(a, BF16).reshape(a.shape[0], 2 * a.shape[1])
    return jnp.pad(bits, ((0, rows - bits.shape[0]), (0, cols - bits.shape[1])))


def kernel(x, norm_ffn1, ffn1_w_gate, ffn1_w_up, ffn1_w_down, norm_mix, w_in, conv_w, pool_w, pool_scale, w_out, norm_ffn2, ffn2_w_gate, ffn2_w_up, ffn2_w_down, norm_final, loss_target, m_norm_ffn1, m_ffn1_w_gate, m_ffn1_w_up, m_ffn1_w_down, m_norm_mix, m_w_in, m_conv_w, m_pool_w, m_pool_scale, m_w_out, m_norm_ffn2, m_ffn2_w_gate, m_ffn2_w_up, m_ffn2_w_down, m_norm_final, v_norm_ffn1, v_ffn1_w_gate, v_ffn1_w_up, v_ffn1_w_down, v_norm_mix, v_w_in, v_conv_w, v_pool_w, v_pool_scale, v_w_out, v_norm_ffn2, v_ffn2_w_gate, v_ffn2_w_up, v_ffn2_w_down, v_norm_final):
    weights = dict(norm_ffn1=norm_ffn1, ffn1_w_gate=ffn1_w_gate, ffn1_w_up=ffn1_w_up, ffn1_w_down=ffn1_w_down, norm_mix=norm_mix,
                   w_in=w_in, conv_w=conv_w, pool_w=pool_w, pool_scale=pool_scale, w_out=w_out, norm_ffn2=norm_ffn2,
                   ffn2_w_gate=ffn2_w_gate, ffn2_w_up=ffn2_w_up, ffn2_w_down=ffn2_w_down, norm_final=norm_final)
    first_m = dict(norm_ffn1=m_norm_ffn1, ffn1_w_gate=m_ffn1_w_gate, ffn1_w_up=m_ffn1_w_up, ffn1_w_down=m_ffn1_w_down,
                   norm_mix=m_norm_mix, w_in=m_w_in, conv_w=m_conv_w, pool_w=m_pool_w, pool_scale=m_pool_scale, w_out=m_w_out,
                   norm_ffn2=m_norm_ffn2, ffn2_w_gate=m_ffn2_w_gate, ffn2_w_up=m_ffn2_w_up, ffn2_w_down=m_ffn2_w_down,
                   norm_final=m_norm_final)
    second_m = dict(norm_ffn1=v_norm_ffn1, ffn1_w_gate=v_ffn1_w_gate, ffn1_w_up=v_ffn1_w_up, ffn1_w_down=v_ffn1_w_down,
                    norm_mix=v_norm_mix, w_in=v_w_in, conv_w=v_conv_w, pool_w=v_pool_w, pool_scale=v_pool_scale, w_out=v_w_out,
                    norm_ffn2=v_norm_ffn2, ffn2_w_gate=v_ffn2_w_gate, ffn2_w_up=v_ffn2_w_up, ffn2_w_down=v_ffn2_w_down,
                    norm_final=v_norm_final)
    names = list(weights)

    xs = x[0]
    tgt = loss_target[0]
    t, d = xs.shape
    dc = pool_scale.shape[1]
    cx, cy, cc = _my_place()
    chip = 2 * cx + cy
    place = jnp.stack([chip, cc]).astype(jnp.int32)

    conv_rows = 32
    wout_x = jnp.concatenate([w_out[0].astype(BF16), _f32_rows_as_bf16(conv_w[0], conv_rows, d)], axis=0)
    wg2_shard = ffn2_w_gate[0].T.astype(BF16)
    half_rows = wg2_shard.shape[0] // 2

    g1, gm, g2 = norm_ffn1, norm_mix, norm_ffn2
    gf = norm_final.reshape(1, d)
    pw = pool_w[0]

    wg1, wu1 = [[w] for w in _run_cargo(_gather_cargo([ffn1_w_gate[0].T.astype(BF16), ffn1_w_up[0].T.astype(BF16)]), "gather_ffn1")]
    (a1, b1, s1), [(wd1, win_t)] = _ffn_up(xs, g1, wg1, wu1, "ffn1_up", [_gather_cargo([ffn1_w_down[0].astype(BF16), w_in[0].T.astype(BF16)])])
    wd1 = [wd1]
    (x1,), [(wout_g, wg2_a)] = _ffn_down(xs, s1, wd1, "ffn1_down", [_gather_cargo([wout_x, wg2_shard[:half_rows]])])
    wo_rows = w_out.shape[1]
    cshard = conv_w.shape[2]
    conv_bits = wout_g.reshape(N_CHIPS, wo_rows + conv_rows, d)[:, wo_rows:wo_rows + conv_w.shape[1], :2 * cshard]
    conv_full = lax.bitcast_convert_type(conv_bits.reshape(N_CHIPS, conv_w.shape[1], cshard, 2), F32)
    conv_full = jnp.transpose(conv_full, (1, 0, 2)).reshape(conv_w.shape[1], N_CHIPS * cshard)
    (x2, proj, ymix), [(wg2_b, wu2)] = _mixer_forward(
        x1, gm, win_t, wout_g, conv_full, pw, pool_scale, [_gather_cargo([wg2_shard[half_rows:], ffn2_w_up[0].T.astype(BF16)])])
    wg2, wu2 = [wg2_a, wg2_b], [wu2]
    (a2, b2, s2), [(wd2,)] = _ffn_up(x2, g2, wg2, wu2, "ffn2_up", [_gather_cargo([ffn2_w_down[0].astype(BF16)])])
    wd2 = [wd2]
    (dx3, sq_cols, dgf), _ = _ffn_down(x2, s2, wd2, "ffn2_down", loss_head=(gf, tgt))

    (dx2, da2, db2, h3, do2, dg2), _ = _ffn_backward(dx3, x2, g2, a2, b2, wg2, wu2, wd2, "ffn2_backward")
    p_wg2, _ = _weight_grad(da2, h3, "ffn2_gate_grad")
    p_wu2, [(x_wg2,)] = _weight_grad(db2, h3, "ffn2_up_grad", [_exchange_cargo([p_wg2])])
    p_wd2, [(x_wu2,)] = _weight_grad(s2, do2, "ffn2_down_grad", [_exchange_cargo([p_wu2])])

    (dx1, dproj, h2, dx2b, dgm, dcw, dps, dpw), [(x_wd2,)] = _mixer_backward(
        dx2, x1, gm, proj, win_t, wout_g, conv_full, pw, pool_scale, [_exchange_cargo([p_wd2])])

    (dx0, da1, db1, h1, do1, dg1), _ = _ffn_backward(dx1, xs, g1, a1, b1, wg1, wu1, wd1, "ffn1_backward")

    npw = pw.size // d
    head = [dg1, dgm, dg2, dgf, jnp.pad(dps, ((0, 0), (0, d - dc))), jnp.pad(dcw, ((0, 0), (0, d - dc))), sq_cols]
    n_head = sum(h.shape[0] for h in head)
    base = -(-n_head // 8) * 8
    pack = jnp.concatenate(head + [jnp.zeros((base - n_head, d), F32), dpw.reshape(npw, d)], axis=0)

    p_wg1, [(packs,)] = _weight_grad(da1, h1, "ffn1_gate_grad", [_all_gather_small_cargo(pack)])
    p_wu1, [(x_wg1,)] = _weight_grad(db1, h1, "ffn1_up_grad", [_exchange_cargo([p_wg1])])
    p_wd1, [(x_wu1,)] = _weight_grad(s1, do1, "ffn1_down_grad", [_exchange_cargo([p_wu1])])
    p_win, [(x_wd1,)] = _weight_grad(dproj, h2, "w_in_grad", [_exchange_cargo([p_wd1])])
    p_wout, [(x_win,)] = _weight_grad(ymix, dx2b, "w_out_grad", [_exchange_cargo([p_win])])
    x_wout, = _run_cargo(_exchange_cargo([p_wout]), "grad_exchange_last")
    small = _sum_by_device(packs)
    loss = jnp.sum(small[n_head - 1]) * (0.5 / d)

    order = ["wg1", "wu1", "wd1", "win", "wout", "wg2", "wu2", "wd2"]
    pairs = dict(wg1=p_wg1, wu1=p_wu1, wd1=p_wd1, win=p_win, wout=p_wout, wg2=p_wg2, wu2=p_wu2, wd2=p_wd2)
    landed = dict(wg1=x_wg1, wu1=x_wu1, wd1=x_wd1, win=x_win, wout=x_wout, wg2=x_wg2, wu2=x_wu2, wd2=x_wd2)
    both = _sibling_share([_chip_sum(pairs[k], landed[k], place, k) for k in order])
    rwg1, rwu1, rwd1, rwin, rwout, rwg2, rwu2, rwd2 = [b.reshape(2 * b.shape[1], b.shape[2]) for b in both]

    grads = {
        "norm_ffn1": small[0:1], "norm_mix": small[1:2], "norm_ffn2": small[2:3], "norm_final": small[3],
        "pool_scale": small[4:5, :dc],
        "conv_w": lax.dynamic_slice_in_dim(small[5:5 + dcw.shape[0], :dc], chip * cshard, cshard, axis=1)[None],
        "pool_w": small[base:].reshape(pool_w.shape),
        "ffn1_w_down": rwd1[None], "w_out": rwout[None], "ffn2_w_down": rwd2[None],
    }
    by_view = {"ffn1_w_gate": rwg1, "ffn1_w_up": rwu1, "ffn2_w_gate": rwg2, "ffn2_w_up": rwu2}

    deltas, new_m, new_v = {}, {}, {}
    for n in names:
        w = weights[n]
        shape = w.shape
        if n == "w_in":
            grads[n], deltas[n], new_m[n], new_v[n] = _adamw_transposed(w, rwin, first_m[n], second_m[n], "adamw_" + n)
            continue
        if n in by_view:
            view = lambda a: jnp.swapaxes(a, 1, 2)[0]
            back = lambda a: jnp.swapaxes(a[None], 1, 2)
            dl, mo, vo = _adamw(view(w), by_view[n], view(first_m[n]), view(second_m[n]), "adamw_" + n)
            grads[n], deltas[n], new_m[n], new_v[n] = back(by_view[n]), back(dl), back(mo), back(vo)
            continue
        as2d = (lambda a: a.reshape(-1, shape[-1]))
        dl, mo, vo = _adamw(as2d(w), as2d(grads[n]), as2d(first_m[n]), as2d(second_m[n]), "adamw_" + n)
        deltas[n], new_m[n], new_v[n] = dl.reshape(shape), mo.reshape(shape), vo.reshape(shape)
        grads[n] = grads[n].reshape(shape)

    return (loss, dx0[None], *[grads[n] for n in names], *[deltas[n] for n in names],
            *[new_m[n] for n in names], *[new_v[n] for n in names])
```

```python
import jax
import jax.numpy as jnp
from jax import lax
from jax.experimental import pallas as pl
from jax.experimental.pallas import tpu as pltpu

F32 = jnp.float32
BF16 = jnp.bfloat16
MESH = pl.DeviceIdType.MESH

EPS = 1e-6
POOL_WINDOWS = (2, 4, 8, 16)
ADAM_LR = 0.001
ADAM_B1 = 0.9
ADAM_B2 = 0.999
ADAM_EPS = 1e-08
ADAM_WD = 0.01
ADAM_STEP = 10

N_CHIPS = 4
N_DEVICES = 8
MXU_COLS_V7X = 256
VMEM_LIMIT = 56 * 1024 * 1024
TM_FFN = 512
TM_MIX = 512
TM_TN = 1024
HALO = 32
WINDOW_LEVELS = 3
FFN_FWD_CHUNKS = 2
FFN_BWD_CHUNKS = 2


def _nt(a, b):
    return lax.dot_general(a, b, (((1,), (1,)), ((), ())), preferred_element_type=F32)


def _tn(a, b):
    return lax.dot_general(a, b, (((0,), (0,)), ((), ())), preferred_element_type=F32)


def _nn(a, b):
    return jnp.dot(a, b, preferred_element_type=F32)


def _sigmoid(a):
    return 1.0 / (1.0 + jnp.exp(-a))


def _feature_chunks(n, parts):
    assert n % MXU_COLS_V7X == 0
    tiles = n // MXU_COLS_V7X
    out, s0 = [], 0
    for p in range(parts):
        sz = (tiles // parts + (1 if p < tiles % parts else 0)) * MXU_COLS_V7X
        if sz:
            out.append((s0, sz))
            s0 += sz
    return out


def _row_block(rows, cap):
    best = 8
    for b in range(8, min(rows, cap) + 1, 8):
        if rows % b == 0:
            best = b
    assert rows % best == 0
    return best


def _my_place():
    return lax.axis_index("x"), lax.axis_index("y"), lax.axis_index("c")


def _other_chips(x, y):
    return [(1 - x, y), (x, 1 - y), (1 - x, 1 - y)]


HBM_SPEC = pl.BlockSpec(memory_space=pltpu.HBM)


class _Cargo:
    def __init__(self, operands, out_shapes, n_sems, phases, when):
        self.operands, self.out_shapes, self.n_sems = list(operands), list(out_shapes), n_sems
        self.phases, self.when = list(phases), list(when)
        assert len(self.phases) == len(self.when) and self.when[0] == 0.0 and self.when[-1] == 1.0


def _launch(body, *, name, grid, in_specs, out_specs, out_shape, scratch_shapes, args, cargo=()):
    params = pltpu.CompilerParams(dimension_semantics=("arbitrary",) * len(grid), vmem_limit_bytes=VMEM_LIMIT)
    cargos = list(cargo)
    c_operands = [op for cg in cargos for op in cg.operands]
    c_shapes = [sh for cg in cargos for sh in cg.out_shapes]
    counts = [len(in_specs), len(c_operands), len(out_shape), len(c_shapes), len(scratch_shapes), 2 * len(cargos)]

    def carrying(*refs):
        groups, pos = [], 0
        for k in counts:
            groups.append(refs[pos:pos + k])
            pos += k
        ins, c_ins, outs, c_outs, scratch, sems = groups
        parts, pi, po = [], 0, 0
        for n, cg in enumerate(cargos):
            parts.append((c_ins[pi:pi + len(cg.operands)], c_outs[po:po + len(cg.out_shapes)], sems[2 * n], sems[2 * n + 1]))
            pi += len(cg.operands)
            po += len(cg.out_shapes)
        step, steps = 0, 1
        for ax, g in enumerate(grid):
            step = step * g + pl.program_id(ax)
            steps *= g
        todo = {}
        for cg, part in zip(cargos, parts):
            for phase, frac in zip(cg.phases[:-1], cg.when[:-1]):
                todo.setdefault(int(round(frac * (steps - 1))), []).append((phase, part))

        for at in sorted(todo):
            @pl.when(step == at)
            def _(at=at):
                for phase, part in todo[at]:
                    phase(*part)

        body(*ins, *outs, *scratch)

        if cargos:
            @pl.when(step == steps - 1)
            def _():
                for cg, part in zip(cargos, parts):
                    cg.phases[-1](*part)

    sems = [pltpu.SemaphoreType.DMA((cg.n_sems,)) for cg in cargos for _ in range(2)]
    outs = pl.pallas_call(
        carrying, name=name, grid=grid,
        in_specs=list(in_specs) + [HBM_SPEC] * counts[1], out_specs=list(out_specs) + [HBM_SPEC] * counts[3],
        out_shape=list(out_shape) + c_shapes, scratch_shapes=list(scratch_shapes) + sems,
        compiler_params=params)(*args, *c_operands)
    own, rest = list(outs[:counts[2]]), list(outs[counts[2]:])
    carried, po = [], 0
    for cg in cargos:
        carried.append(rest[po:po + len(cg.out_shapes)])
        po += len(cg.out_shapes)
    return own, carried


def _run_cargo(cargo, name):
    n_in, n_out = len(cargo.operands), len(cargo.out_shapes)

    def body(*refs):
        c_ins, c_outs, sems = refs[:n_in], refs[n_in:n_in + n_out], refs[n_in + n_out:]
        for phase in cargo.phases:
            phase(c_ins, c_outs, *sems)

    sem = pltpu.SemaphoreType.DMA((cargo.n_sems,))
    return list(pl.pallas_call(body, name=name, out_shape=cargo.out_shapes, in_specs=[HBM_SPEC] * n_in,
                               out_specs=[HBM_SPEC] * n_out, scratch_shapes=[sem, sem])(*cargo.operands))


def _gather_cargo(shards):
    n = len(shards)
    for s in shards:
        assert s.shape[0] % 32 == 0
    slots = 8

    def steps(ins, outs, send_sems, recv_sems):
        x, y, c = _my_place()
        sibling = (x, y, 1 - c)
        over_x, over_y = (1 - x, y, c), (x, 1 - y, c)
        mine, chip_x, chip_y, chip_d = 2 * x + y, 2 * (1 - x) + y, 2 * x + (1 - y), 2 * (1 - x) + (1 - y)

        def rows_of(a, chip_index, half, part=None):
            rps = shards[a].shape[0]
            hr = rps // 2
            first = -(-hr // 32) * 16
            offset, size = {None: (0, hr), 0: (0, first), 1: (first, hr - first)}[part]
            return outs[a].at[pl.ds(pl.multiple_of(chip_index * rps + half * hr + offset, 16), size), :]

        def remote(a, slot, src, dst, to):
            return pltpu.make_async_remote_copy(
                src_ref=src, dst_ref=dst, send_sem=send_sems.at[a * slots + slot], recv_sem=recv_sems.at[a * slots + slot],
                device_id=to, device_id_type=MESH)

        def same_rows(a, slot, rows, to):
            return remote(a, slot, rows, rows, to)

        def own_copy(a):
            rps = shards[a].shape[0]
            return remote(a, 7, ins[a], outs[a].at[pl.ds(pl.multiple_of(mine * rps, 16), rps), :], sibling)

        def my_half(a):
            hr = shards[a].shape[0] // 2
            return ins[a].at[pl.ds(pl.multiple_of(c * hr, 16), hr), :]

        def start():
            for a in range(n):
                own_copy(a).start()
                remote(a, 0, my_half(a), rows_of(a, mine, c), over_x).start()
                remote(a, 1, my_half(a), rows_of(a, mine, c), over_y).start()

        def relay_neighbours():
            for a in range(n):
                same_rows(a, 0, rows_of(a, chip_x, c), over_x).wait_recv()
                same_rows(a, 4, rows_of(a, chip_x, c), sibling).start()
                same_rows(a, 2, rows_of(a, chip_x, c, 0), over_y).start()
                same_rows(a, 1, rows_of(a, chip_y, c), over_y).wait_recv()
                same_rows(a, 5, rows_of(a, chip_y, c), sibling).start()
                same_rows(a, 3, rows_of(a, chip_y, c, 1), over_x).start()

        def relay_diagonal():
            for a in range(n):
                same_rows(a, 2, rows_of(a, chip_d, c, 0), over_y).wait_recv()
                same_rows(a, 3, rows_of(a, chip_d, c, 1), over_x).wait_recv()
                same_rows(a, 6, rows_of(a, chip_d, c), sibling).start()

        def finish():
            for a in range(n):
                for slot, chip_index in ((4, chip_x), (5, chip_y), (6, chip_d)):
                    same_rows(a, slot, rows_of(a, chip_index, 1 - c), sibling).wait_recv()
            for a in range(n):
                remote(a, 0, my_half(a), rows_of(a, mine, c), over_x).wait_send()
                remote(a, 1, my_half(a), rows_of(a, mine, c), over_y).wait_send()
                same_rows(a, 2, rows_of(a, chip_x, c, 0), over_y).wait_send()
                same_rows(a, 3, rows_of(a, chip_y, c, 1), over_x).wait_send()
                for slot, chip_index in ((4, chip_x), (5, chip_y), (6, chip_d)):
                    same_rows(a, slot, rows_of(a, chip_index, c), sibling).wait_send()
                own_copy(a).wait()

        return start, relay_neighbours, relay_diagonal, finish

    phases = [lambda *r, k=k: steps(*r)[k]() for k in range(4)]
    return _Cargo(shards, [jax.ShapeDtypeStruct((N_CHIPS * s.shape[0], s.shape[1]), s.dtype) for s in shards], slots * n,
                  phases, [0.0, 0.6, 0.85, 1.0])


def _exchange_cargo(pairs):
    n = len(pairs)

    def copies(ins, outs, send_sems, recv_sems):
        x, y, c = _my_place()
        return [pltpu.make_async_remote_copy(
            src_ref=ins[a].at[2 * chip[0] + chip[1]], dst_ref=outs[a].at[j],
            send_sem=send_sems.at[3 * a + j], recv_sem=recv_sems.at[3 * a + j], device_id=(*chip, c), device_id_type=MESH)
            for a in range(n) for j, chip in enumerate(_other_chips(x, y))]

    def start(*r):
        for cp in copies(*r):
            cp.start()

    def finish(*r):
        for cp in copies(*r):
            cp.wait()

    return _Cargo(pairs, [jax.ShapeDtypeStruct((3,) + p.shape[1:], p.dtype) for p in pairs], 3 * n, [start, finish], [0.0, 1.0])


def _all_gather_small_cargo(pack):
    rows, cols = pack.shape

    def copies(ins, outs, send_sems, recv_sems):
        x, y, c = _my_place()
        me = 4 * x + 2 * y + c
        remote = []
        for f in range(1, N_DEVICES):
            fx, fy, fc = (f >> 2) & 1, (f >> 1) & 1, f & 1
            to = (1 - x if fx else x, 1 - y if fy else y, 1 - c if fc else c)
            remote.append(pltpu.make_async_remote_copy(
                src_ref=ins[0], dst_ref=outs[0].at[me], send_sem=send_sems.at[f - 1], recv_sem=recv_sems.at[f - 1],
                device_id=to, device_id_type=MESH))
        own = pltpu.make_async_copy(ins[0], outs[0].at[me], send_sems.at[N_DEVICES - 1])
        return remote, own

    def start(*r):
        remote, own = copies(*r)
        own.start()
        for cp in remote:
            cp.start()

    def finish(*r):
        remote, own = copies(*r)
        for cp in remote:
            cp.wait()
        own.wait()

    return _Cargo([pack], [jax.ShapeDtypeStruct((N_DEVICES, rows, cols), F32)], N_DEVICES, [start, finish], [0.0, 1.0])


def _sum_by_device(packs):
    n, rows, cols = packs.shape

    def body(p_ref, o_ref):
        acc = p_ref[0]
        for dev in range(1, n):
            acc = acc + p_ref[dev]
        o_ref[...] = acc

    return pl.pallas_call(body, name="small_grads_sum", out_shape=jax.ShapeDtypeStruct((rows, cols), F32))(packs)


def _chip_sum(pair, got, place, tag):
    _, hr, cols = pair.shape
    br = _row_block(hr, 256)

    def body(k_ref, p_ref, r_ref, o_ref):
        acc = p_ref[...].astype(F32)
        for j in range(3):
            acc = acc + r_ref[j].astype(F32)
        o_ref[...] = acc

    return pl.pallas_call(
        body, name="grad_chip_sum_" + tag,
        out_shape=jax.ShapeDtypeStruct((2, hr, cols), F32),
        grid_spec=pltpu.PrefetchScalarGridSpec(
            num_scalar_prefetch=1, grid=(hr // br,),
            in_specs=[pl.BlockSpec((None, br, cols), lambda r, k_ref: (k_ref[0], r, 0)),
                      pl.BlockSpec((3, br, cols), lambda r, k_ref: (0, r, 0))],
            out_specs=pl.BlockSpec((None, br, cols), lambda r, k_ref: (k_ref[1], r, 0))),
        compiler_params=pltpu.CompilerParams(dimension_semantics=("parallel",)),
    )(place, pair, got)


def _sibling_share(halves):
    n = len(halves)

    def body(*refs):
        outs = refs[n:2 * n]
        send_sems, recv_sems = refs[2 * n:]
        x, y, c = _my_place()
        copies = []
        for a in range(n):
            cp = pltpu.make_async_remote_copy(
                src_ref=outs[a].at[c], dst_ref=outs[a].at[c], send_sem=send_sems.at[a], recv_sem=recv_sems.at[a],
                device_id=(x, y, 1 - c), device_id_type=MESH)
            cp.start()
            copies.append(cp)
        for cp in copies:
            cp.wait()

    return pl.pallas_call(
        body, name="grad_share_sibling",
        out_shape=[jax.ShapeDtypeStruct(h.shape, h.dtype) for h in halves],
        in_specs=[HBM_SPEC] * n, out_specs=[HBM_SPEC] * n,
        input_output_aliases={a: a for a in range(n)},
        scratch_shapes=[pltpu.SemaphoreType.DMA((n,)), pltpu.SemaphoreType.DMA((n,))],
    )(*halves)


def _load_rows(pairs, sems):
    cps = [pltpu.make_async_copy(src, dst, sems.at[j]) for j, (src, dst) in enumerate(pairs)]
    for cp in cps:
        cp.start()
    for cp in cps:
        cp.wait()


def _piece_rows(weights):
    flat = [p for pieces in weights for p in pieces]

    def copies(refs, mats):
        out, n = [], 0
        for pieces, mat in zip(weights, mats):
            rps = sum(p.shape[0] for p in pieces) // N_CHIPS
            off = 0
            for p in pieces:
                r = p.shape[0] // N_CHIPS
                if len(pieces) == 1:
                    out.append((refs[n], mat))
                else:
                    for k in range(N_CHIPS):
                        out.append((refs[n].at[pl.ds(k * r, r), :], mat.at[pl.ds(k * rps + off, r), :]))
                off += r
                n += 1
        return out

    n_copies = sum(1 if len(pieces) == 1 else N_CHIPS * len(pieces) for pieces in weights)
    return flat, copies, n_copies


def _cast_to_bf16(arrays, name, cargo=()):
    rows, cols = arrays[0].shape
    n = len(arrays)
    br = _row_block(rows, 256)

    def body(*refs):
        for src, dst in zip(refs[:n], refs[n:]):
            dst[...] = src[...].astype(BF16)

    blk = pl.BlockSpec((br, cols), lambda i: (i, 0))
    return _launch(body, name=name, grid=(rows // br,), in_specs=[blk] * n, out_specs=[blk] * n,
                   out_shape=[jax.ShapeDtypeStruct((rows, cols), BF16)] * n, scratch_shapes=[], args=tuple(arrays), cargo=cargo)


def _loss_head(xv, gv, tv):
    d = xv.shape[-1]
    r = lax.rsqrt(jnp.mean(xv * xv, axis=-1, keepdims=True) + EPS)
    xhat = xv * r
    err = xhat * gv - tv
    dy = err * (1.0 / d)
    dxh = dy * gv
    dx = r * (dxh - xhat * jnp.mean(dxh * xhat, axis=-1, keepdims=True))
    return dx, jnp.sum(err * err, axis=0, keepdims=True), jnp.sum(dy * xhat, axis=0, keepdims=True)


def _ffn_up(x, g, wg_t, wu_t, name, cargo=()):
    t, d = x.shape
    f = sum(p.shape[0] for p in wg_t)
    tm = min(TM_FFN, t)
    chunks = _feature_chunks(f, FFN_FWD_CHUNKS)
    flat, copies, n_copies = _piece_rows([wg_t, wu_t])
    nw = len(flat)

    def body(x_ref, g_ref, *rest):
        w_hbm, (a_ref, b_ref, s_ref, wg, wu, sems) = rest[:nw], rest[nw:]

        @pl.when(pl.program_id(0) == 0)
        def _():
            _load_rows(copies(w_hbm, [wg, wu]), sems)

        xv = x_ref[...]
        r = lax.rsqrt(jnp.mean(xv * xv, axis=-1, keepdims=True) + EPS)
        h = (xv * r * g_ref[...]).astype(BF16)
        for s0, sz in chunks:
            a = _nt(h, wg[s0:s0 + sz, :])
            b = _nt(h, wu[s0:s0 + sz, :])
            a_ref[:, s0:s0 + sz] = a.astype(BF16)
            b_ref[:, s0:s0 + sz] = b.astype(BF16)
            s_ref[:, s0:s0 + sz] = (a * _sigmoid(a) * b).astype(BF16)

    tok = lambda i: (i, 0)
    wide = pl.BlockSpec((tm, f), tok)
    return _launch(
        body, name=name, grid=(t // tm,),
        in_specs=[pl.BlockSpec((tm, d), tok), pl.BlockSpec((1, d), lambda i: (0, 0))] + [HBM_SPEC] * nw,
        out_specs=[wide, wide, wide], out_shape=[jax.ShapeDtypeStruct((t, f), BF16)] * 3,
        scratch_shapes=[pltpu.VMEM((f, d), BF16), pltpu.VMEM((f, d), BF16), pltpu.SemaphoreType.DMA((n_copies,))],
        args=(x, g, *flat), cargo=cargo)


def _ffn_down(x, s, wd, name, cargo=(), loss_head=None):
    t, d = x.shape
    f = s.shape[1]
    tm = min(TM_FFN, t)
    flat, copies, n_copies = _piece_rows([wd])
    nw = len(flat)
    nl = 2 if loss_head else 0

    def body(x_ref, s_ref, *rest):
        head, w_hbm = rest[:nl], rest[nl:nl + nw]
        xo_ref = rest[nl + nw]
        sums, (wdn, sems) = rest[nl + nw + 1:nl + nw + 1 + nl], rest[nl + nw + 1 + nl:]

        @pl.when(pl.program_id(0) == 0)
        def _():
            _load_rows(copies(w_hbm, [wdn]), sems)
            for sum_ref in sums:
                sum_ref[...] = jnp.zeros_like(sum_ref)

        xo = x_ref[...] + 0.5 * _nn(s_ref[...], wdn[...])
        if loss_head:
            dx, sq, dgf = _loss_head(xo, head[0][...], head[1][...])
            xo_ref[...] = dx
            sums[0][...] += sq
            sums[1][...] += dgf
        else:
            xo_ref[...] = xo

    tok = lambda i: (i, 0)
    one = lambda i: (0, 0)
    return _launch(
        body, name=name, grid=(t // tm,),
        in_specs=[pl.BlockSpec((tm, d), tok), pl.BlockSpec((tm, f), tok)]
        + ([pl.BlockSpec((1, d), one), pl.BlockSpec((tm, d), tok)] if loss_head else []) + [HBM_SPEC] * nw,
        out_specs=[pl.BlockSpec((tm, d), tok)] + [pl.BlockSpec((1, d), one)] * nl,
        out_shape=[jax.ShapeDtypeStruct((t, d), F32)] + [jax.ShapeDtypeStruct((1, d), F32)] * nl,
        scratch_shapes=[pltpu.VMEM((f, d), BF16), pltpu.SemaphoreType.DMA((n_copies,))],
        args=(x, s, *(loss_head or ()), *flat), cargo=cargo)


def _ffn_backward(dxo, x, g, a, b, wg_t, wu_t, wd, name, cargo=()):
    t, d = x.shape
    f = sum(p.shape[0] for p in wd)
    tm = min(TM_FFN // 2, t)
    chunks = _feature_chunks(f, FFN_BWD_CHUNKS)
    flat, copies, n_copies = _piece_rows([wg_t, wu_t, wd])
    nw = len(flat)

    def body(dxo_ref, x_ref, g_ref, a_ref, b_ref, *rest):
        w_hbm, (dx_ref, da_ref, db_ref, h_ref, do_ref, dg_ref, wg, wu, wdn, sems) = rest[:nw], rest[nw:]

        @pl.when(pl.program_id(0) == 0)
        def _():
            _load_rows(copies(w_hbm, [wg, wu, wdn]), sems)
            dg_ref[...] = jnp.zeros_like(dg_ref)

        xv = x_ref[...]
        gv = g_ref[...]
        r = lax.rsqrt(jnp.mean(xv * xv, axis=-1, keepdims=True) + EPS)
        xhat = xv * r
        h_ref[...] = (xhat * gv).astype(BF16)
        dxo_v = dxo_ref[...]
        dout = (0.5 * dxo_v).astype(BF16)
        do_ref[...] = dout
        dh = jnp.zeros((tm, d), F32)
        for s0, sz in chunks:
            ds = _nt(dout, wdn[s0:s0 + sz, :])
            av = a_ref[:, s0:s0 + sz].astype(F32)
            bv = b_ref[:, s0:s0 + sz].astype(F32)
            sig = _sigmoid(av)
            silu = av * sig
            da = (ds * bv * (sig * (1.0 + av * (1.0 - sig)))).astype(BF16)
            db = (ds * silu).astype(BF16)
            da_ref[:, s0:s0 + sz] = da
            db_ref[:, s0:s0 + sz] = db
            dh = dh + _nn(da, wg[s0:s0 + sz, :]) + _nn(db, wu[s0:s0 + sz, :])
        dg_ref[...] += jnp.sum(dh * xhat, axis=0, keepdims=True)
        dxh = dh * gv
        dx_ref[...] = dxo_v + r * (dxh - xhat * jnp.mean(dxh * xhat, axis=-1, keepdims=True))

    tok = lambda i: (i, 0)
    one = lambda i: (0, 0)
    return _launch(
        body, name=name, grid=(t // tm,),
        in_specs=[pl.BlockSpec((tm, d), tok), pl.BlockSpec((tm, d), tok), pl.BlockSpec((1, d), one),
                  pl.BlockSpec((tm, f), tok), pl.BlockSpec((tm, f), tok)] + [HBM_SPEC] * nw,
        out_specs=[pl.BlockSpec((tm, d), tok), pl.BlockSpec((tm, f), tok), pl.BlockSpec((tm, f), tok),
                   pl.BlockSpec((tm, d), tok), pl.BlockSpec((tm, d), tok), pl.BlockSpec((1, d), one)],
        out_shape=[jax.ShapeDtypeStruct((t, d), F32), jax.ShapeDtypeStruct((t, f), BF16), jax.ShapeDtypeStruct((t, f), BF16),
                   jax.ShapeDtypeStruct((t, d), BF16), jax.ShapeDtypeStruct((t, d), BF16), jax.ShapeDtypeStruct((1, d), F32)],
        scratch_shapes=[pltpu.VMEM((f, d), BF16), pltpu.VMEM((f, d), BF16), pltpu.VMEM((f, d), BF16), pltpu.SemaphoreType.DMA((n_copies,))],
        args=(dxo, x, g, a, b, *flat), cargo=cargo)


def _weight_grad(lhs, rhs, name, cargo=()):
    t, m = lhs.shape
    d = rhs.shape[1]
    tm = min(TM_TN, t)
    nt = t // tm
    nj = 1
    bm = m // nj
    cpb = N_CHIPS // nj
    rps = m // N_CHIPS
    hr = rps // 2
    assert hr % 16 == 0

    def body(l_ref, r_ref, o_ref, acc, stage, recv, send_sems, recv_sems):
        j = pl.program_id(0)
        i = pl.program_id(1)
        @pl.when(i == 0)
        def _():
            acc[...] = jnp.zeros_like(acc)

        acc[...] += _tn(l_ref[...], r_ref[...])

        def pair_sum(jj):
            x, y, c = _my_place()
            copies = []
            for q in range(cpb):
                slot = jj * cpb + q
                stage[slot] = acc[pl.ds(pl.multiple_of(q * rps + (1 - c) * hr, 16), hr), :].astype(BF16)
                cp = pltpu.make_async_remote_copy(
                    src_ref=stage.at[slot], dst_ref=recv.at[slot], send_sem=send_sems.at[slot], recv_sem=recv_sems.at[slot],
                    device_id=(x, y, 1 - c), device_id_type=MESH)
                cp.start()
                copies.append(cp)
            for q, cp in enumerate(copies):
                cp.wait_recv()
                mine = acc[pl.ds(pl.multiple_of(q * rps + c * hr, 16), hr), :]
                o_ref[q] = (mine + recv[jj * cpb + q].astype(F32)).astype(BF16)
            for cp in copies:
                cp.wait_send()

        for jj in range(nj):
            @pl.when(jnp.logical_and(i == nt - 1, j == jj))
            def _():
                pair_sum(jj)

    outs, carried = _launch(
        body, name=name, grid=(nj, nt),
        in_specs=[pl.BlockSpec((tm, bm), lambda j, i: (i, j)), pl.BlockSpec((tm, d), lambda j, i: (i, 0))],
        out_specs=[pl.BlockSpec((cpb, hr, d), lambda j, i: (j, 0, 0))],
        out_shape=[jax.ShapeDtypeStruct((N_CHIPS, hr, d), BF16)],
        scratch_shapes=[pltpu.VMEM((bm, d), F32), pltpu.VMEM((N_CHIPS, hr, d), BF16), pltpu.VMEM((N_CHIPS, hr, d), BF16),
                        pltpu.SemaphoreType.DMA((N_CHIPS,)), pltpu.SemaphoreType.DMA((N_CHIPS,))],
        args=(lhs, rhs), cargo=cargo)
    return outs[0], carried


def _window_sums(src, cols, w, tm, levels, trailing):
    def read_src(lo, hi):
        return src[lo:hi, cols]

    read, k, level = read_src, 1, 0
    while True:
        last = 2 * k == w
        if trailing:
            lo, hi = (HALO if last else 8 * (level + 1)), HALO + tm
            cur = read(lo, hi) + read(lo - k, hi - k)
        else:
            lo, hi = 0, (tm if last else tm + HALO - 8 * (level + 1))
            cur = read(lo, hi) + read(lo + k, hi + k)
        if last:
            return cur
        levels[level, lo:hi, :] = cur
        read = lambda a, b, level=level: levels[level, a:b, :]
        k, level = 2 * k, level + 1


def _pool_parts(u_cols, ubuf, cols, w, row, tm, levels):
    ws = _window_sums(ubuf, cols, w, tm, levels, trailing=True)
    cnt = jnp.minimum(row + 1, w).astype(F32)
    return ws / cnt - u_cols, cnt


def _mixer_forward(x, g, win_t, wout_x, conv_w, pool_w, pool_scale, cargo=()):
    t, d = x.shape
    dc = win_t.shape[0] // 4
    gcw = dc // len(POOL_WINDOWS)
    wo_rows = d // N_CHIPS
    wo_stride = wout_x.shape[0] // N_CHIPS
    tm = min(TM_MIX, t)

    def body(x_ref, g_ref, win_hbm, wout_hbm, cw_ref, pw_ref, ps_ref, xo_ref, proj_ref, y_ref,
             win, wout, zbuf, ubuf, levels, sems):
        i = pl.program_id(0)

        @pl.when(i == 0)
        def _():
            pairs = [(win_hbm, win)]
            for k in range(N_CHIPS):
                pairs.append((wout_hbm.at[pl.ds(k * wo_stride, wo_rows), :], wout.at[pl.ds(k * wo_rows, wo_rows), :]))
            _load_rows(pairs, sems)
            zbuf[0:8, :] = jnp.zeros((8, dc), F32)
            ubuf[0:HALO, :] = jnp.zeros((HALO, dc), F32)

        xv = x_ref[...]
        r = lax.rsqrt(jnp.mean(xv * xv, axis=-1, keepdims=True) + EPS)
        h = (xv * r * g_ref[...]).astype(BF16)
        v = _nt(h, win[0:dc, :])
        gb = _nt(h, win[dc:2 * dc, :])
        gc = _nt(h, win[2 * dc:3 * dc, :])
        u = _nt(h, win[3 * dc:4 * dc, :])
        proj_ref[:, 0:dc] = v.astype(BF16)
        proj_ref[:, dc:2 * dc] = gb.astype(BF16)
        proj_ref[:, 2 * dc:3 * dc] = gc.astype(BF16)
        proj_ref[:, 3 * dc:4 * dc] = u.astype(BF16)

        z = gc * v
        zbuf[8:8 + tm, :] = z
        cw = cw_ref[...]
        conv = cw[2:3, :] * z + cw[1:2, :] * zbuf[7:7 + tm, :] + cw[0:1, :] * zbuf[6:6 + tm, :]
        y_ref[:, 0:dc] = (gb * conv).astype(BF16)

        ubuf[HALO:HALO + tm, :] = u
        row = i * tm + lax.broadcasted_iota(jnp.int32, (tm, 1), 0)
        for gi, w in enumerate(POOL_WINDOWS):
            cols = slice(gi * gcw, (gi + 1) * gcw)
            pooled, _ = _pool_parts(u[:, cols], ubuf, cols, w, row, tm, levels)
            yb = _nn(pooled.astype(BF16), pw_ref[gi].astype(BF16)) * ps_ref[:, cols]
            y_ref[:, dc + gi * gcw:dc + (gi + 1) * gcw] = yb.astype(BF16)

        xo_ref[...] = xv + _nn(y_ref[...], wout[...])
        zbuf[0:8, :] = zbuf[tm:tm + 8, :]
        ubuf[0:HALO, :] = ubuf[tm:tm + HALO, :]

    tok = lambda i: (i, 0)
    one = lambda i: (0, 0)
    return _launch(
        body, name="mixer_forward", grid=(t // tm,),
        in_specs=[pl.BlockSpec((tm, d), tok), pl.BlockSpec((1, d), one), HBM_SPEC, HBM_SPEC,
                  pl.BlockSpec(conv_w.shape, one), pl.BlockSpec(pool_w.shape, lambda i: (0, 0, 0)), pl.BlockSpec((1, dc), one)],
        out_specs=[pl.BlockSpec((tm, d), tok), pl.BlockSpec((tm, 4 * dc), tok), pl.BlockSpec((tm, 2 * dc), tok)],
        out_shape=[jax.ShapeDtypeStruct((t, d), F32), jax.ShapeDtypeStruct((t, 4 * dc), BF16), jax.ShapeDtypeStruct((t, 2 * dc), BF16)],
        scratch_shapes=[pltpu.VMEM((4 * dc, d), BF16), pltpu.VMEM((2 * dc, d), BF16),
                        pltpu.VMEM((tm + 8, dc), F32), pltpu.VMEM((tm + HALO, dc), F32),
                        pltpu.VMEM((WINDOW_LEVELS, tm + HALO, gcw), F32), pltpu.SemaphoreType.DMA((1 + N_CHIPS,))],
        args=(x, g, win_t, wout_x, conv_w, pool_w, pool_scale), cargo=cargo)


def _mixer_backward(dxo, x, g, proj, win_t, wout_x, conv_w, pool_w, pool_scale, cargo=()):
    t, d = x.shape
    dc = win_t.shape[0] // 4
    ng = len(POOL_WINDOWS)
    gcw = dc // ng
    wo_rows = d // N_CHIPS
    wo_stride = wout_x.shape[0] // N_CHIPS
    tm = min(TM_MIX, t)
    n_tiles = t // tm
    hb = tm // HALO

    def body(dxo_ref, x_ref, g_ref, proj_ref, halo_ref, win_hbm, wout_hbm, cw_ref, pw_ref, ps_ref,
             dx_ref, dproj_ref, h_ref, dxob_ref, dg_ref, dcw_ref, dps_ref, dpw_ref,
             win, wout, zbuf, ubuf, dcbuf, ebuf, levels, sems):
        i = pl.program_id(0)
        tile = n_tiles - 1 - i

        @pl.when(i == 0)
        def _():
            pairs = [(win_hbm, win)]
            for k in range(N_CHIPS):
                pairs.append((wout_hbm.at[pl.ds(k * wo_stride, wo_rows), :], wout.at[pl.ds(k * wo_rows, wo_rows), :]))
            _load_rows(pairs, sems)
            dcbuf[tm:tm + 8, :] = jnp.zeros((8, dc), F32)
            ebuf[tm:tm + HALO, :] = jnp.zeros((HALO, dc), F32)
            dg_ref[...] = jnp.zeros_like(dg_ref)
            dcw_ref[...] = jnp.zeros_like(dcw_ref)
            dps_ref[...] = jnp.zeros_like(dps_ref)
            dpw_ref[...] = jnp.zeros_like(dpw_ref)

        xv = x_ref[...]
        gv = g_ref[...]
        r = lax.rsqrt(jnp.mean(xv * xv, axis=-1, keepdims=True) + EPS)
        xhat = xv * r
        h_ref[...] = (xhat * gv).astype(BF16)
        dxo_v = dxo_ref[...]
        dxo_b = dxo_v.astype(BF16)
        dxob_ref[...] = dxo_b

        v = proj_ref[:, 0:dc].astype(F32)
        gb = proj_ref[:, dc:2 * dc].astype(F32)
        gc = proj_ref[:, 2 * dc:3 * dc].astype(F32)
        u = proj_ref[:, 3 * dc:4 * dc].astype(F32)
        first = jnp.where(tile > 0, 1.0, 0.0)
        zbuf[0:HALO, :] = halo_ref[:, 2 * dc:3 * dc].astype(F32) * halo_ref[:, 0:dc].astype(F32) * first
        ubuf[0:HALO, :] = halo_ref[:, 3 * dc:4 * dc].astype(F32) * first
        z = gc * v
        zbuf[HALO:HALO + tm, :] = z
        ubuf[HALO:HALO + tm, :] = u
        z1 = zbuf[HALO - 1:HALO - 1 + tm, :]
        z2 = zbuf[HALO - 2:HALO - 2 + tm, :]
        cw = cw_ref[...]
        conv = cw[2:3, :] * z + cw[1:2, :] * z1 + cw[0:1, :] * z2

        dy = _nt(dxo_b, wout[...])
        dya = dy[:, 0:dc]
        dgb = dya * conv
        dconv = dya * gb
        dcbuf[0:tm, :] = dconv
        dz = cw[2:3, :] * dconv + cw[1:2, :] * dcbuf[1:1 + tm, :] + cw[0:1, :] * dcbuf[2:2 + tm, :]
        dgc = dz * v
        dv = dz * gc
        dcw_ref[0:1, :] += jnp.sum(dconv * z2, axis=0, keepdims=True)
        dcw_ref[1:2, :] += jnp.sum(dconv * z1, axis=0, keepdims=True)
        dcw_ref[2:3, :] += jnp.sum(dconv * z, axis=0, keepdims=True)

        dproj_ref[:, 0:dc] = dv.astype(BF16)
        dproj_ref[:, dc:2 * dc] = dgb.astype(BF16)
        dproj_ref[:, 2 * dc:3 * dc] = dgc.astype(BF16)

        row = tile * tm + lax.broadcasted_iota(jnp.int32, (tm, 1), 0)
        for gi, w in enumerate(POOL_WINDOWS):
            cols = slice(gi * gcw, (gi + 1) * gcw)
            pooled, cnt = _pool_parts(u[:, cols], ubuf, cols, w, row, tm, levels)
            pooled_b = pooled.astype(BF16)
            pw_b = pw_ref[gi].astype(BF16)
            dyb = dy[:, dc + gi * gcw:dc + (gi + 1) * gcw]
            q = _nn(pooled_b, pw_b)
            dps_ref[:, cols] += jnp.sum(q * dyb, axis=0, keepdims=True)
            dq = (dyb * ps_ref[:, cols]).astype(BF16)
            dpw_ref[gi] += _tn(pooled_b, dq)
            dpooled = _nt(dq, pw_b)
            ebuf[0:tm, cols] = dpooled / cnt
            du = _window_sums(ebuf, cols, w, tm, levels, trailing=False) - dpooled
            dproj_ref[:, 3 * dc + gi * gcw:3 * dc + (gi + 1) * gcw] = du.astype(BF16)

        dh = _nn(dproj_ref[...], win[...])
        dg_ref[...] += jnp.sum(dh * xhat, axis=0, keepdims=True)
        dxh = dh * gv
        dx_ref[...] = dxo_v + r * (dxh - xhat * jnp.mean(dxh * xhat, axis=-1, keepdims=True))
        dcbuf[tm:tm + 8, :] = dcbuf[0:8, :]
        ebuf[tm:tm + HALO, :] = ebuf[0:HALO, :]

    tok = lambda i: (n_tiles - 1 - i, 0)
    halo = lambda i: (jnp.maximum((n_tiles - 1 - i) * hb - 1, 0), 0)
    one = lambda i: (0, 0)
    return _launch(
        body, name="mixer_backward", grid=(n_tiles,),
        in_specs=[pl.BlockSpec((tm, d), tok), pl.BlockSpec((tm, d), tok), pl.BlockSpec((1, d), one),
                  pl.BlockSpec((tm, 4 * dc), tok), pl.BlockSpec((HALO, 4 * dc), halo), HBM_SPEC, HBM_SPEC,
                  pl.BlockSpec(conv_w.shape, one), pl.BlockSpec(pool_w.shape, lambda i: (0, 0, 0)), pl.BlockSpec((1, dc), one)],
        out_specs=[pl.BlockSpec((tm, d), tok), pl.BlockSpec((tm, 4 * dc), tok), pl.BlockSpec((tm, d), tok), pl.BlockSpec((tm, d), tok),
                   pl.BlockSpec((1, d), one), pl.BlockSpec(conv_w.shape, one), pl.BlockSpec((1, dc), one),
                   pl.BlockSpec(pool_w.shape, lambda i: (0, 0, 0))],
        out_shape=[jax.ShapeDtypeStruct((t, d), F32), jax.ShapeDtypeStruct((t, 4 * dc), BF16), jax.ShapeDtypeStruct((t, d), BF16),
                   jax.ShapeDtypeStruct((t, d), BF16), jax.ShapeDtypeStruct((1, d), F32), jax.ShapeDtypeStruct(conv_w.shape, F32),
                   jax.ShapeDtypeStruct((1, dc), F32), jax.ShapeDtypeStruct(pool_w.shape, F32)],
        scratch_shapes=[pltpu.VMEM((4 * dc, d), BF16), pltpu.VMEM((2 * dc, d), BF16),
                        pltpu.VMEM((tm + HALO, dc), F32), pltpu.VMEM((tm + HALO, dc), F32),
                        pltpu.VMEM((tm + 8, dc), F32), pltpu.VMEM((tm + HALO, dc), F32),
                        pltpu.VMEM((WINDOW_LEVELS, tm + HALO, gcw), F32), pltpu.SemaphoreType.DMA((1 + N_CHIPS,))],
        args=(dxo, x, g, proj, proj, win_t, wout_x, conv_w, pool_w, pool_scale), cargo=cargo)


def _adam_update(w, gv, m, v):
    m_new = ADAM_B1 * m + (1.0 - ADAM_B1) * gv
    v_new = ADAM_B2 * v + (1.0 - ADAM_B2) * (gv * gv)
    m_hat = m_new / (1.0 - ADAM_B1 ** ADAM_STEP)
    v_hat = v_new / (1.0 - ADAM_B2 ** ADAM_STEP)
    return -ADAM_LR * (m_hat / (jnp.sqrt(v_hat) + ADAM_EPS) + ADAM_WD * w), m_new, v_new


def _adamw(w, grad, m, v, name):
    rows, cols = w.shape
    br = _row_block(rows, 256) if rows >= 8 else rows

    def body(w_ref, g_ref, m_ref, v_ref, d_ref, mo_ref, vo_ref):
        d_ref[...], mo_ref[...], vo_ref[...] = _adam_update(w_ref[...], g_ref[...], m_ref[...], v_ref[...])

    blk = pl.BlockSpec((br, cols), lambda i: (i, 0))
    return pl.pallas_call(
        body, name=name,
        out_shape=[jax.ShapeDtypeStruct((rows, cols), F32)] * 3,
        grid=(rows // br,), in_specs=[blk] * 4, out_specs=[blk] * 3,
        compiler_params=pltpu.CompilerParams(dimension_semantics=("parallel",)),
    )(w, grad, m, v)


def _adamw_transposed(w, grad_t, m, v, name):
    _, rows, cols = w.shape
    br = 256 if rows % 256 == 0 else rows

    def body(w_ref, gt_ref, m_ref, v_ref, g_ref, d_ref, mo_ref, vo_ref):
        gv = gt_ref[...].T
        g_ref[...] = gv
        d_ref[...], mo_ref[...], vo_ref[...] = _adam_update(w_ref[...], gv, m_ref[...], v_ref[...])

    blk = pl.BlockSpec((None, br, cols), lambda i: (0, i, 0))
    return pl.pallas_call(
        body, name=name,
        out_shape=[jax.ShapeDtypeStruct((1, rows, cols), F32)] * 4,
        grid=(rows // br,), in_specs=[blk, pl.BlockSpec((cols, br), lambda i: (0, i)), blk, blk], out_specs=[blk] * 4,
        compiler_params=pltpu.CompilerParams(dimension_semantics=("parallel",)),
    )(w, grad_t, m, v)


def _f32_rows_as_bf16(a, rows, cols):
    bits = lax.bitcast_convert_type(a, BF16).reshape(a.shape[0], 2 * a.shape[1])
    return jnp.pad(bits, ((0, rows - bits.shape[0]), (0, cols - bits.shape[1])))


def kernel(x, norm_ffn1, ffn1_w_gate, ffn1_w_up, ffn1_w_down, norm_mix, w_in, conv_w, pool_w, pool_scale, w_out, norm_ffn2, ffn2_w_gate, ffn2_w_up, ffn2_w_down, norm_final, loss_target, m_norm_ffn1, m_ffn1_w_gate, m_ffn1_w_up, m_ffn1_w_down, m_norm_mix, m_w_in, m_conv_w, m_pool_w, m_pool_scale, m_w_out, m_norm_ffn2, m_ffn2_w_gate, m_ffn2_w_up, m_ffn2_w_down, m_norm_final, v_norm_ffn1, v_ffn1_w_gate, v_ffn1_w_up, v_ffn1_w_down, v_norm_mix, v_w_in, v_conv_w, v_pool_w, v_pool_scale, v_w_out, v_norm_ffn2, v_ffn2_w_gate, v_ffn2_w_up, v_ffn2_w_down, v_norm_final):
    weights = dict(norm_ffn1=norm_ffn1, ffn1_w_gate=ffn1_w_gate, ffn1_w_up=ffn1_w_up, ffn1_w_down=ffn1_w_down, norm_mix=norm_mix,
                   w_in=w_in, conv_w=conv_w, pool_w=pool_w, pool_scale=pool_scale, w_out=w_out, norm_ffn2=norm_ffn2,
                   ffn2_w_gate=ffn2_w_gate, ffn2_w_up=ffn2_w_up, ffn2_w_down=ffn2_w_down, norm_final=norm_final)
    first_m = dict(norm_ffn1=m_norm_ffn1, ffn1_w_gate=m_ffn1_w_gate, ffn1_w_up=m_ffn1_w_up, ffn1_w_down=m_ffn1_w_down,
                   norm_mix=m_norm_mix, w_in=m_w_in, conv_w=m_conv_w, pool_w=m_pool_w, pool_scale=m_pool_scale, w_out=m_w_out,
                   norm_ffn2=m_norm_ffn2, ffn2_w_gate=m_ffn2_w_gate, ffn2_w_up=m_ffn2_w_up, ffn2_w_down=m_ffn2_w_down,
                   norm_final=m_norm_final)
    second_m = dict(norm_ffn1=v_norm_ffn1, ffn1_w_gate=v_ffn1_w_gate, ffn1_w_up=v_ffn1_w_up, ffn1_w_down=v_ffn1_w_down,
                    norm_mix=v_norm_mix, w_in=v_w_in, conv_w=v_conv_w, pool_w=v_pool_w, pool_scale=v_pool_scale, w_out=v_w_out,
                    norm_ffn2=v_norm_ffn2, ffn2_w_gate=v_ffn2_w_gate, ffn2_w_up=v_ffn2_w_up, ffn2_w_down=v_ffn2_w_down,
                    norm_final=v_norm_final)
    names = list(weights)

    xs = x[0]
    tgt = loss_target[0]
    t, d = xs.shape
    dc = pool_scale.shape[1]
    cx, cy, cc = _my_place()
    chip = 2 * cx + cy
    place = jnp.stack([chip, cc]).astype(jnp.int32)

    conv_rows = 32
    wout_x = jnp.concatenate([w_out[0].astype(BF16), _f32_rows_as_bf16(conv_w[0], conv_rows, d)], axis=0)

    g1, gm, g2 = norm_ffn1, norm_mix, norm_ffn2
    gf = norm_final.reshape(1, d)
    pw = pool_w[0]

    (wd1_shard, wg2_shard, wu2_shard, wd2_shard), [(wg1, wu1)] = _cast_to_bf16(
        [ffn1_w_down[0], ffn2_w_gate[0].T, ffn2_w_up[0].T, ffn2_w_down[0]], "gather_ffn1",
        [_gather_cargo([ffn1_w_gate[0].T.astype(BF16), ffn1_w_up[0].T.astype(BF16)])])
    wg1, wu1 = [wg1], [wu1]
    half_rows = wg2_shard.shape[0] // 2
    (a1, b1, s1), [(wd1, win_t)] = _ffn_up(xs, g1, wg1, wu1, "ffn1_up", [_gather_cargo([wd1_shard, w_in[0].T.astype(BF16)])])
    wd1 = [wd1]
    (x1,), [(wout_g, wg2_a)] = _ffn_down(xs, s1, wd1, "ffn1_down", [_gather_cargo([wout_x, wg2_shard[:half_rows]])])
    wo_rows = w_out.shape[1]
    cshard = conv_w.shape[2]
    conv_bits = wout_g.reshape(N_CHIPS, wo_rows + conv_rows, d)[:, wo_rows:wo_rows + conv_w.shape[1], :2 * cshard]
    conv_full = lax.bitcast_convert_type(conv_bits.reshape(N_CHIPS, conv_w.shape[1], cshard, 2), F32)
    conv_full = jnp.transpose(conv_full, (1, 0, 2)).reshape(conv_w.shape[1], N_CHIPS * cshard)
    (x2, proj, ymix), [(wg2_b, wu2)] = _mixer_forward(
        x1, gm, win_t, wout_g, conv_full, pw, pool_scale, [_gather_cargo([wg2_shard[half_rows:], wu2_shard])])
    wg2, wu2 = [wg2_a, wg2_b], [wu2]
    (a2, b2, s2), [(wd2,)] = _ffn_up(x2, g2, wg2, wu2, "ffn2_up", [_gather_cargo([wd2_shard])])
    wd2 = [wd2]
    (dx3, sq_cols, dgf), _ = _ffn_down(x2, s2, wd2, "ffn2_down", loss_head=(gf, tgt))

    (dx2, da2, db2, h3, do2, dg2), _ = _ffn_backward(dx3, x2, g2, a2, b2, wg2, wu2, wd2, "ffn2_backward")
    p_wg2, _ = _weight_grad(da2, h3, "ffn2_gate_grad")
    p_wu2, [(x_wg2,)] = _weight_grad(db2, h3, "ffn2_up_grad", [_exchange_cargo([p_wg2])])
    p_wd2, [(x_wu2,)] = _weight_grad(s2, do2, "ffn2_down_grad", [_exchange_cargo([p_wu2])])

    (dx1, dproj, h2, dx2b, dgm, dcw, dps, dpw), [(x_wd2,)] = _mixer_backward(
        dx2, x1, gm, proj, win_t, wout_g, conv_full, pw, pool_scale, [_exchange_cargo([p_wd2])])

    (dx0, da1, db1, h1, do1, dg1), _ = _ffn_backward(dx1, xs, g1, a1, b1, wg1, wu1, wd1, "ffn1_backward")

    npw = pw.size // d
    head = [dg1, dgm, dg2, dgf, jnp.pad(dps, ((0, 0), (0, d - dc))), jnp.pad(dcw, ((0, 0), (0, d - dc))), sq_cols]
    n_head = sum(h.shape[0] for h in head)
    base = -(-n_head // 8) * 8
    pack = jnp.concatenate(head + [jnp.zeros((base - n_head, d), F32), dpw.reshape(npw, d)], axis=0)

    p_wg1, [(packs,)] = _weight_grad(da1, h1, "ffn1_gate_grad", [_all_gather_small_cargo(pack)])
    p_wu1, [(x_wg1,)] = _weight_grad(db1, h1, "ffn1_up_grad", [_exchange_cargo([p_wg1])])
    p_wd1, [(x_wu1,)] = _weight_grad(s1, do1, "ffn1_down_grad", [_exchange_cargo([p_wu1])])
    p_win, [(x_wd1,)] = _weight_grad(dproj, h2, "w_in_grad", [_exchange_cargo([p_wd1])])
    p_wout, [(x_win,)] = _weight_grad(ymix, dx2b, "w_out_grad", [_exchange_cargo([p_win])])
    x_wout, = _run_cargo(_exchange_cargo([p_wout]), "grad_exchange_last")
    small = _sum_by_device(packs)
    loss = jnp.sum(small[n_head - 1]) * (0.5 / d)

    order = ["wg1", "wu1", "wd1", "win", "wout", "wg2", "wu2", "wd2"]
    pairs = dict(wg1=p_wg1, wu1=p_wu1, wd1=p_wd1, win=p_win, wout=p_wout, wg2=p_wg2, wu2=p_wu2, wd2=p_wd2)
    landed = dict(wg1=x_wg1, wu1=x_wu1, wd1=x_wd1, win=x_win, wout=x_wout, wg2=x_wg2, wu2=x_wu2, wd2=x_wd2)
    both = _sibling_share([_chip_sum(pairs[k], landed[k], place, k) for k in order])
    rwg1, rwu1, rwd1, rwin, rwout, rwg2, rwu2, rwd2 = [b.reshape(2 * b.shape[1], b.shape[2]) for b in both]

    grads = {
        "norm_ffn1": small[0:1], "norm_mix": small[1:2], "norm_ffn2": small[2:3], "norm_final": small[3],
        "pool_scale": small[4:5, :dc],
        "conv_w": lax.dynamic_slice_in_dim(small[5:5 + dcw.shape[0], :dc], chip * cshard, cshard, axis=1)[None],
        "pool_w": small[base:].reshape(pool_w.shape),
        "ffn1_w_down": rwd1[None], "w_out": rwout[None], "ffn2_w_down": rwd2[None],
    }
    by_view = {"ffn1_w_gate": rwg1, "ffn1_w_up": rwu1, "ffn2_w_gate": rwg2, "ffn2_w_up": rwu2}

    deltas, new_m, new_v = {}, {}, {}
    for n in names:
        w = weights[n]
        shape = w.shape
        if n == "w_in":
            grads[n], deltas[n], new_m[n], new_v[n] = _adamw_transposed(w, rwin, first_m[n], second_m[n], "adamw_" + n)
            continue
        if n in by_view:
            view = lambda a: jnp.swapaxes(a, 1, 2)[0]
            back = lambda a: jnp.swapaxes(a[None], 1, 2)
            dl, mo, vo = _adamw(view(w), by_view[n], view(first_m[n]), view(second_m[n]), "adamw_" + n)
            grads[n], deltas[n], new_m[n], new_v[n] = back(by_view[n]), back(dl), back(mo), back(vo)
            continue
        as2d = (lambda a: a.reshape(-1, shape[-1]))
        dl, mo, vo = _adamw(as2d(w), as2d(grads[n]), as2d(first_m[n]), as2d(second_m[n]), "adamw_" + n)
        deltas[n], new_m[n], new_v[n] = dl.reshape(shape), mo.reshape(shape), vo.reshape(shape)
        grads[n] = grads[n].reshape(shape)

    return (loss, dx0[None], *[grads[n] for n in names], *[deltas[n] for n in names],
            *[new_m[n] for n in names], *[new_v[n] for n in names])
```

```python
import jax
import jax.numpy as jnp
from jax import lax
from jax.experimental import pallas as pl
from jax.experimental.pallas import tpu as pltpu

F32 = jnp.float32
BF16 = jnp.bfloat16
MESH = pl.DeviceIdType.MESH

EPS = 1e-6
POOL_WINDOWS = (2, 4, 8, 16)
ADAM_LR = 0.001
ADAM_B1 = 0.9
ADAM_B2 = 0.999
ADAM_EPS = 1e-08
ADAM_WD = 0.01
ADAM_STEP = 10

N_CHIPS = 4
N_DEVICES = 8
MXU_COLS_V7X = 256
VMEM_LIMIT = 56 * 1024 * 1024
TM_FFN = 512
TM_MIX = 512
TM_TN = 1024
HALO = 32
WINDOW_LEVELS = 3
FFN_FWD_CHUNKS = 2
FFN_BWD_CHUNKS = 2


def _nt(a, b):
    return lax.dot_general(a, b, (((1,), (1,)), ((), ())), preferred_element_type=F32)


def _tn(a, b):
    return lax.dot_general(a, b, (((0,), (0,)), ((), ())), preferred_element_type=F32)


def _nn(a, b):
    return jnp.dot(a, b, preferred_element_type=F32)


def _sigmoid(a):
    return 1.0 / (1.0 + jnp.exp(-a))


def _feature_chunks(n, parts):
    assert n % MXU_COLS_V7X == 0
    tiles = n // MXU_COLS_V7X
    out, s0 = [], 0
    for p in range(parts):
        sz = (tiles // parts + (1 if p < tiles % parts else 0)) * MXU_COLS_V7X
        if sz:
            out.append((s0, sz))
            s0 += sz
    return out


def _row_block(rows, cap):
    best = 8
    for b in range(8, min(rows, cap) + 1, 8):
        if rows % b == 0:
            best = b
    assert rows % best == 0
    return best


def _my_place():
    return lax.axis_index("x"), lax.axis_index("y"), lax.axis_index("c")


def _other_chips(x, y):
    return [(1 - x, y), (x, 1 - y), (1 - x, 1 - y)]


HBM_SPEC = pl.BlockSpec(memory_space=pltpu.HBM)


class _Cargo:
    def __init__(self, operands, out_shapes, n_sems, phases, when):
        self.operands, self.out_shapes, self.n_sems = list(operands), list(out_shapes), n_sems
        self.phases, self.when = list(phases), list(when)
        assert len(self.phases) == len(self.when) and self.when[0] == 0.0 and self.when[-1] == 1.0


def _launch(body, *, name, grid, in_specs, out_specs, out_shape, scratch_shapes, args, cargo=()):
    params = pltpu.CompilerParams(dimension_semantics=("arbitrary",) * len(grid), vmem_limit_bytes=VMEM_LIMIT)
    cargos = list(cargo)
    c_operands = [op for cg in cargos for op in cg.operands]
    c_shapes = [sh for cg in cargos for sh in cg.out_shapes]
    counts = [len(in_specs), len(c_operands), len(out_shape), len(c_shapes), len(scratch_shapes), 2 * len(cargos)]

    def carrying(*refs):
        groups, pos = [], 0
        for k in counts:
            groups.append(refs[pos:pos + k])
            pos += k
        ins, c_ins, outs, c_outs, scratch, sems = groups
        parts, pi, po = [], 0, 0
        for n, cg in enumerate(cargos):
            parts.append((c_ins[pi:pi + len(cg.operands)], c_outs[po:po + len(cg.out_shapes)], sems[2 * n], sems[2 * n + 1]))
            pi += len(cg.operands)
            po += len(cg.out_shapes)
        step, steps = 0, 1
        for ax, g in enumerate(grid):
            step = step * g + pl.program_id(ax)
            steps *= g
        todo = {}
        for cg, part in zip(cargos, parts):
            for phase, frac in zip(cg.phases[:-1], cg.when[:-1]):
                todo.setdefault(int(round(frac * (steps - 1))), []).append((phase, part))

        for at in sorted(todo):
            @pl.when(step == at)
            def _(at=at):
                for phase, part in todo[at]:
                    phase(*part)

        body(*ins, *outs, *scratch)

        if cargos:
            @pl.when(step == steps - 1)
            def _():
                for cg, part in zip(cargos, parts):
                    cg.phases[-1](*part)

    sems = [pltpu.SemaphoreType.DMA((cg.n_sems,)) for cg in cargos for _ in range(2)]
    outs = pl.pallas_call(
        carrying, name=name, grid=grid,
        in_specs=list(in_specs) + [HBM_SPEC] * counts[1], out_specs=list(out_specs) + [HBM_SPEC] * counts[3],
        out_shape=list(out_shape) + c_shapes, scratch_shapes=list(scratch_shapes) + sems,
        compiler_params=params)(*args, *c_operands)
    own, rest = list(outs[:counts[2]]), list(outs[counts[2]:])
    carried, po = [], 0
    for cg in cargos:
        carried.append(rest[po:po + len(cg.out_shapes)])
        po += len(cg.out_shapes)
    return own, carried


def _run_cargo(cargo, name):
    n_in, n_out = len(cargo.operands), len(cargo.out_shapes)

    def body(*refs):
        c_ins, c_outs, sems = refs[:n_in], refs[n_in:n_in + n_out], refs[n_in + n_out:]
        for phase in cargo.phases:
            phase(c_ins, c_outs, *sems)

    sem = pltpu.SemaphoreType.DMA((cargo.n_sems,))
    return list(pl.pallas_call(body, name=name, out_shape=cargo.out_shapes, in_specs=[HBM_SPEC] * n_in,
                               out_specs=[HBM_SPEC] * n_out, scratch_shapes=[sem, sem])(*cargo.operands))


def _gather_cargo(shards):
    n = len(shards)
    for s in shards:
        assert s.shape[0] % 32 == 0
    slots = 8

    def steps(ins, outs, send_sems, recv_sems):
        x, y, c = _my_place()
        sibling = (x, y, 1 - c)
        over_x, over_y = (1 - x, y, c), (x, 1 - y, c)
        mine, chip_x, chip_y, chip_d = 2 * x + y, 2 * (1 - x) + y, 2 * x + (1 - y), 2 * (1 - x) + (1 - y)

        def rows_of(a, chip_index, half, part=None):
            rps = shards[a].shape[0]
            hr = rps // 2
            first = -(-hr // 32) * 16
            offset, size = {None: (0, hr), 0: (0, first), 1: (first, hr - first)}[part]
            return outs[a].at[pl.ds(pl.multiple_of(chip_index * rps + half * hr + offset, 16), size), :]

        def remote(a, slot, src, dst, to):
            return pltpu.make_async_remote_copy(
                src_ref=src, dst_ref=dst, send_sem=send_sems.at[a * slots + slot], recv_sem=recv_sems.at[a * slots + slot],
                device_id=to, device_id_type=MESH)

        def same_rows(a, slot, rows, to):
            return remote(a, slot, rows, rows, to)

        def own_copy(a):
            rps = shards[a].shape[0]
            return remote(a, 7, ins[a], outs[a].at[pl.ds(pl.multiple_of(mine * rps, 16), rps), :], sibling)

        def my_half(a):
            hr = shards[a].shape[0] // 2
            return ins[a].at[pl.ds(pl.multiple_of(c * hr, 16), hr), :]

        def start():
            for a in range(n):
                own_copy(a).start()
                remote(a, 0, my_half(a), rows_of(a, mine, c), over_x).start()
                remote(a, 1, my_half(a), rows_of(a, mine, c), over_y).start()

        def relay_neighbours():
            for a in range(n):
                same_rows(a, 0, rows_of(a, chip_x, c), over_x).wait_recv()
                same_rows(a, 4, rows_of(a, chip_x, c), sibling).start()
                same_rows(a, 2, rows_of(a, chip_x, c, 0), over_y).start()
                same_rows(a, 1, rows_of(a, chip_y, c), over_y).wait_recv()
                same_rows(a, 5, rows_of(a, chip_y, c), sibling).start()
                same_rows(a, 3, rows_of(a, chip_y, c, 1), over_x).start()

        def relay_diagonal():
            for a in range(n):
                same_rows(a, 2, rows_of(a, chip_d, c, 0), over_y).wait_recv()
                same_rows(a, 3, rows_of(a, chip_d, c, 1), over_x).wait_recv()
                same_rows(a, 6, rows_of(a, chip_d, c), sibling).start()

        def finish():
            for a in range(n):
                for slot, chip_index in ((4, chip_x), (5, chip_y), (6, chip_d)):
                    same_rows(a, slot, rows_of(a, chip_index, 1 - c), sibling).wait_recv()
            for a in range(n):
                remote(a, 0, my_half(a), rows_of(a, mine, c), over_x).wait_send()
                remote(a, 1, my_half(a), rows_of(a, mine, c), over_y).wait_send()
                same_rows(a, 2, rows_of(a, chip_x, c, 0), over_y).wait_send()
                same_rows(a, 3, rows_of(a, chip_y, c, 1), over_x).wait_send()
                for slot, chip_index in ((4, chip_x), (5, chip_y), (6, chip_d)):
                    same_rows(a, slot, rows_of(a, chip_index, c), sibling).wait_send()
                own_copy(a).wait()

        return start, relay_neighbours, relay_diagonal, finish

    phases = [lambda *r, k=k: steps(*r)[k]() for k in range(4)]
    return _Cargo(shards, [jax.ShapeDtypeStruct((N_CHIPS * s.shape[0], s.shape[1]), s.dtype) for s in shards], slots * n,
                  phases, [0.0, 0.6, 0.85, 1.0])


def _exchange_cargo(pairs):
    n = len(pairs)

    def copies(ins, outs, send_sems, recv_sems):
        x, y, c = _my_place()
        return [pltpu.make_async_remote_copy(
            src_ref=ins[a].at[2 * chip[0] + chip[1]], dst_ref=outs[a].at[j],
            send_sem=send_sems.at[3 * a + j], recv_sem=recv_sems.at[3 * a + j], device_id=(*chip, c), device_id_type=MESH)
            for a in range(n) for j, chip in enumerate(_other_chips(x, y))]

    def start(*r):
        for cp in copies(*r):
            cp.start()

    def finish(*r):
        for cp in copies(*r):
            cp.wait()

    return _Cargo(pairs, [jax.ShapeDtypeStruct((3,) + p.shape[1:], p.dtype) for p in pairs], 3 * n, [start, finish], [0.0, 1.0])


def _all_gather_small_cargo(pack):
    rows, cols = pack.shape

    def copies(ins, outs, send_sems, recv_sems):
        x, y, c = _my_place()
        me = 4 * x + 2 * y + c
        remote = []
        for f in range(1, N_DEVICES):
            fx, fy, fc = (f >> 2) & 1, (f >> 1) & 1, f & 1
            to = (1 - x if fx else x, 1 - y if fy else y, 1 - c if fc else c)
            remote.append(pltpu.make_async_remote_copy(
                src_ref=ins[0], dst_ref=outs[0].at[me], send_sem=send_sems.at[f - 1], recv_sem=recv_sems.at[f - 1],
                device_id=to, device_id_type=MESH))
        own = pltpu.make_async_copy(ins[0], outs[0].at[me], send_sems.at[N_DEVICES - 1])
        return remote, own

    def start(*r):
        remote, own = copies(*r)
        own.start()
        for cp in remote:
            cp.start()

    def finish(*r):
        remote, own = copies(*r)
        for cp in remote:
            cp.wait()
        own.wait()

    return _Cargo([pack], [jax.ShapeDtypeStruct((N_DEVICES, rows, cols), F32)], N_DEVICES, [start, finish], [0.0, 1.0])


def _sum_by_device(packs):
    n, rows, cols = packs.shape

    def body(p_ref, o_ref):
        acc = p_ref[0]
        for dev in range(1, n):
            acc = acc + p_ref[dev]
        o_ref[...] = acc

    return pl.pallas_call(body, name="small_grads_sum", out_shape=jax.ShapeDtypeStruct((rows, cols), F32))(packs)


def _chip_sum(pair, got, place, tag):
    _, hr, cols = pair.shape
    br = _row_block(hr, 256)

    def body(k_ref, p_ref, r_ref, o_ref):
        acc = p_ref[...].astype(F32)
        for j in range(3):
            acc = acc + r_ref[j].astype(F32)
        o_ref[...] = acc

    return pl.pallas_call(
        body, name="grad_chip_sum_" + tag,
        out_shape=jax.ShapeDtypeStruct((2, hr, cols), F32),
        grid_spec=pltpu.PrefetchScalarGridSpec(
            num_scalar_prefetch=1, grid=(hr // br,),
            in_specs=[pl.BlockSpec((None, br, cols), lambda r, k_ref: (k_ref[0], r, 0)),
                      pl.BlockSpec((3, br, cols), lambda r, k_ref: (0, r, 0))],
            out_specs=pl.BlockSpec((None, br, cols), lambda r, k_ref: (k_ref[1], r, 0))),
        compiler_params=pltpu.CompilerParams(dimension_semantics=("parallel",)),
    )(place, pair, got)


def _sibling_share(halves):
    n = len(halves)

    def body(*refs):
        outs = refs[n:2 * n]
        send_sems, recv_sems = refs[2 * n:]
        x, y, c = _my_place()
        copies = []
        for a in range(n):
            cp = pltpu.make_async_remote_copy(
                src_ref=outs[a].at[c], dst_ref=outs[a].at[c], send_sem=send_sems.at[a], recv_sem=recv_sems.at[a],
                device_id=(x, y, 1 - c), device_id_type=MESH)
            cp.start()
            copies.append(cp)
        for cp in copies:
            cp.wait()

    return pl.pallas_call(
        body, name="grad_share_sibling",
        out_shape=[jax.ShapeDtypeStruct(h.shape, h.dtype) for h in halves],
        in_specs=[HBM_SPEC] * n, out_specs=[HBM_SPEC] * n,
        input_output_aliases={a: a for a in range(n)},
        scratch_shapes=[pltpu.SemaphoreType.DMA((n,)), pltpu.SemaphoreType.DMA((n,))],
    )(*halves)


def _load_rows(pairs, sems):
    cps = [pltpu.make_async_copy(src, dst, sems.at[j]) for j, (src, dst) in enumerate(pairs)]
    for cp in cps:
        cp.start()
    for cp in cps:
        cp.wait()


def _piece_rows(weights):
    flat = [p for pieces in weights for p in pieces]

    def copies(refs, mats):
        out, n = [], 0
        for pieces, mat in zip(weights, mats):
            rps = sum(p.shape[0] for p in pieces) // N_CHIPS
            off = 0
            for p in pieces:
                r = p.shape[0] // N_CHIPS
                if len(pieces) == 1:
                    out.append((refs[n], mat))
                else:
                    for k in range(N_CHIPS):
                        out.append((refs[n].at[pl.ds(k * r, r), :], mat.at[pl.ds(k * rps + off, r), :]))
                off += r
                n += 1
        return out

    n_copies = sum(1 if len(pieces) == 1 else N_CHIPS * len(pieces) for pieces in weights)
    return flat, copies, n_copies


def _cast_to_bf16(arrays, name, cargo=()):
    rows, cols = arrays[0].shape
    n = len(arrays)
    br = _row_block(rows, 256)

    def body(*refs):
        for src, dst in zip(refs[:n], refs[n:]):
            dst[...] = src[...].astype(BF16)

    blk = pl.BlockSpec((br, cols), lambda i: (i, 0))
    return _launch(body, name=name, grid=(rows // br,), in_specs=[blk] * n, out_specs=[blk] * n,
                   out_shape=[jax.ShapeDtypeStruct((rows, cols), BF16)] * n, scratch_shapes=[], args=tuple(arrays), cargo=cargo)


def _loss_head(xv, gv, tv):
    d = xv.shape[-1]
    r = lax.rsqrt(jnp.mean(xv * xv, axis=-1, keepdims=True) + EPS)
    xhat = xv * r
    err = xhat * gv - tv
    dy = err * (1.0 / d)
    dxh = dy * gv
    dx = r * (dxh - xhat * jnp.mean(dxh * xhat, axis=-1, keepdims=True))
    return dx, jnp.sum(err * err, axis=0, keepdims=True), jnp.sum(dy * xhat, axis=0, keepdims=True)


def _ffn_up(x, g, wg_t, wu_t, name, cargo=()):
    t, d = x.shape
    f = sum(p.shape[0] for p in wg_t)
    tm = min(TM_FFN, t)
    chunks = _feature_chunks(f, FFN_FWD_CHUNKS)
    flat, copies, n_copies = _piece_rows([wg_t, wu_t])
    nw = len(flat)

    def body(x_ref, g_ref, *rest):
        w_hbm, (a_ref, b_ref, s_ref, wg, wu, sems) = rest[:nw], rest[nw:]

        @pl.when(pl.program_id(0) == 0)
        def _():
            _load_rows(copies(w_hbm, [wg, wu]), sems)

        xv = x_ref[...]
        r = lax.rsqrt(jnp.mean(xv * xv, axis=-1, keepdims=True) + EPS)
        h = (xv * r * g_ref[...]).astype(BF16)
        for s0, sz in chunks:
            a = _nt(h, wg[s0:s0 + sz, :])
            b = _nt(h, wu[s0:s0 + sz, :])
            a_ref[:, s0:s0 + sz] = a.astype(BF16)
            b_ref[:, s0:s0 + sz] = b.astype(BF16)
            s_ref[:, s0:s0 + sz] = (a * _sigmoid(a) * b).astype(BF16)

    tok = lambda i: (i, 0)
    wide = pl.BlockSpec((tm, f), tok)
    return _launch(
        body, name=name, grid=(t // tm,),
        in_specs=[pl.BlockSpec((tm, d), tok), pl.BlockSpec((1, d), lambda i: (0, 0))] + [HBM_SPEC] * nw,
        out_specs=[wide, wide, wide], out_shape=[jax.ShapeDtypeStruct((t, f), BF16)] * 3,
        scratch_shapes=[pltpu.VMEM((f, d), BF16), pltpu.VMEM((f, d), BF16), pltpu.SemaphoreType.DMA((n_copies,))],
        args=(x, g, *flat), cargo=cargo)


def _ffn_down(x, s, wd, name, cargo=(), loss_head=None):
    t, d = x.shape
    f = s.shape[1]
    tm = min(TM_FFN, t)
    flat, copies, n_copies = _piece_rows([wd])
    nw = len(flat)
    nl = 2 if loss_head else 0

    def body(x_ref, s_ref, *rest):
        head, w_hbm = rest[:nl], rest[nl:nl + nw]
        xo_ref = rest[nl + nw]
        sums, (wdn, sems) = rest[nl + nw + 1:nl + nw + 1 + nl], rest[nl + nw + 1 + nl:]

        @pl.when(pl.program_id(0) == 0)
        def _():
            _load_rows(copies(w_hbm, [wdn]), sems)
            for sum_ref in sums:
                sum_ref[...] = jnp.zeros_like(sum_ref)

        xo = x_ref[...] + 0.5 * _nn(s_ref[...], wdn[...])
        if loss_head:
            dx, sq, dgf = _loss_head(xo, head[0][...], head[1][...])
            xo_ref[...] = dx
            sums[0][...] += sq
            sums[1][...] += dgf
        else:
            xo_ref[...] = xo

    tok = lambda i: (i, 0)
    one = lambda i: (0, 0)
    return _launch(
        body, name=name, grid=(t // tm,),
        in_specs=[pl.BlockSpec((tm, d), tok), pl.BlockSpec((tm, f), tok)]
        + ([pl.BlockSpec((1, d), one), pl.BlockSpec((tm, d), tok)] if loss_head else []) + [HBM_SPEC] * nw,
        out_specs=[pl.BlockSpec((tm, d), tok)] + [pl.BlockSpec((1, d), one)] * nl,
        out_shape=[jax.ShapeDtypeStruct((t, d), F32)] + [jax.ShapeDtypeStruct((1, d), F32)] * nl,
        scratch_shapes=[pltpu.VMEM((f, d), BF16), pltpu.SemaphoreType.DMA((n_copies,))],
        args=(x, s, *(loss_head or ()), *flat), cargo=cargo)


def _ffn_backward(dxo, x, g, a, b, wg_t, wu_t, wd, name, cargo=()):
    t, d = x.shape
    f = sum(p.shape[0] for p in wd)
    tm = min(TM_FFN // 2, t)
    chunks = _feature_chunks(f, FFN_BWD_CHUNKS)
    flat, copies, n_copies = _piece_rows([wg_t, wu_t, wd])
    nw = len(flat)

    def body(dxo_ref, x_ref, g_ref, a_ref, b_ref, *rest):
        w_hbm, (dx_ref, da_ref, db_ref, h_ref, do_ref, dg_ref, wg, wu, wdn, sems) = rest[:nw], rest[nw:]

        @pl.when(pl.program_id(0) == 0)
        def _():
            _load_rows(copies(w_hbm, [wg, wu, wdn]), sems)
            dg_ref[...] = jnp.zeros_like(dg_ref)

        xv = x_ref[...]
        gv = g_ref[...]
        r = lax.rsqrt(jnp.mean(xv * xv, axis=-1, keepdims=True) + EPS)
        xhat = xv * r
        h_ref[...] = (xhat * gv).astype(BF16)
        dxo_v = dxo_ref[...]
        dout = (0.5 * dxo_v).astype(BF16)
        do_ref[...] = dout
        dh = jnp.zeros((tm, d), F32)
        for s0, sz in chunks:
            ds = _nt(dout, wdn[s0:s0 + sz, :])
            av = a_ref[:, s0:s0 + sz].astype(F32)
            bv = b_ref[:, s0:s0 + sz].astype(F32)
            sig = _sigmoid(av)
            silu = av * sig
            da = (ds * bv * (sig * (1.0 + av * (1.0 - sig)))).astype(BF16)
            db = (ds * silu).astype(BF16)
            da_ref[:, s0:s0 + sz] = da
            db_ref[:, s0:s0 + sz] = db
            dh = dh + _nn(da, wg[s0:s0 + sz, :]) + _nn(db, wu[s0:s0 + sz, :])
        dg_ref[...] += jnp.sum(dh * xhat, axis=0, keepdims=True)
        dxh = dh * gv
        dx_ref[...] = dxo_v + r * (dxh - xhat * jnp.mean(dxh * xhat, axis=-1, keepdims=True))

    tok = lambda i: (i, 0)
    one = lambda i: (0, 0)
    return _launch(
        body, name=name, grid=(t // tm,),
        in_specs=[pl.BlockSpec((tm, d), tok), pl.BlockSpec((tm, d), tok), pl.BlockSpec((1, d), one),
                  pl.BlockSpec((tm, f), tok), pl.BlockSpec((tm, f), tok)] + [HBM_SPEC] * nw,
        out_specs=[pl.BlockSpec((tm, d), tok), pl.BlockSpec((tm, f), tok), pl.BlockSpec((tm, f), tok),
                   pl.BlockSpec((tm, d), tok), pl.BlockSpec((tm, d), tok), pl.BlockSpec((1, d), one)],
        out_shape=[jax.ShapeDtypeStruct((t, d), F32), jax.ShapeDtypeStruct((t, f), BF16), jax.ShapeDtypeStruct((t, f), BF16),
                   jax.ShapeDtypeStruct((t, d), BF16), jax.ShapeDtypeStruct((t, d), BF16), jax.ShapeDtypeStruct((1, d), F32)],
        scratch_shapes=[pltpu.VMEM((f, d), BF16), pltpu.VMEM((f, d), BF16), pltpu.VMEM((f, d), BF16), pltpu.SemaphoreType.DMA((n_copies,))],
        args=(dxo, x, g, a, b, *flat), cargo=cargo)


def _weight_grad(lhs, rhs, name, cargo=()):
    t, m = lhs.shape
    d = rhs.shape[1]
    tm = min(TM_TN, t)
    nt = t // tm
    nj = 1
    bm = m // nj
    cpb = N_CHIPS // nj
    rps = m // N_CHIPS
    hr = rps // 2
    assert hr % 16 == 0

    def body(l_ref, r_ref, o_ref, acc, stage, recv, send_sems, recv_sems):
        j = pl.program_id(0)
        i = pl.program_id(1)
        @pl.when(i == 0)
        def _():
            acc[...] = jnp.zeros_like(acc)

        acc[...] += _tn(l_ref[...], r_ref[...])

        def pair_sum(jj):
            x, y, c = _my_place()
            copies = []
            for q in range(cpb):
                slot = jj * cpb + q
                stage[slot] = acc[pl.ds(pl.multiple_of(q * rps + (1 - c) * hr, 16), hr), :].astype(BF16)
                cp = pltpu.make_async_remote_copy(
                    src_ref=stage.at[slot], dst_ref=recv.at[slot], send_sem=send_sems.at[slot], recv_sem=recv_sems.at[slot],
                    device_id=(x, y, 1 - c), device_id_type=MESH)
                cp.start()
                copies.append(cp)
            for q, cp in enumerate(copies):
                cp.wait_recv()
                mine = acc[pl.ds(pl.multiple_of(q * rps + c * hr, 16), hr), :]
                o_ref[q] = (mine + recv[jj * cpb + q].astype(F32)).astype(BF16)
            for cp in copies:
                cp.wait_send()

        for jj in range(nj):
            @pl.when(jnp.logical_and(i == nt - 1, j == jj))
            def _():
                pair_sum(jj)

    outs, carried = _launch(
        body, name=name, grid=(nj, nt),
        in_specs=[pl.BlockSpec((tm, bm), lambda j, i: (i, j)), pl.BlockSpec((tm, d), lambda j, i: (i, 0))],
        out_specs=[pl.BlockSpec((cpb, hr, d), lambda j, i: (j, 0, 0))],
        out_shape=[jax.ShapeDtypeStruct((N_CHIPS, hr, d), BF16)],
        scratch_shapes=[pltpu.VMEM((bm, d), F32), pltpu.VMEM((N_CHIPS, hr, d), BF16), pltpu.VMEM((N_CHIPS, hr, d), BF16),
                        pltpu.SemaphoreType.DMA((N_CHIPS,)), pltpu.SemaphoreType.DMA((N_CHIPS,))],
        args=(lhs, rhs), cargo=cargo)
    return outs[0], carried


def _window_sums(src, cols, w, tm, levels, trailing):
    def read_src(lo, hi):
        return src[lo:hi, cols]

    read, k, level = read_src, 1, 0
    while True:
        last = 2 * k == w
        if trailing:
            lo, hi = (HALO if last else 8 * (level + 1)), HALO + tm
            cur = read(lo, hi) + read(lo - k, hi - k)
        else:
            lo, hi = 0, (tm if last else tm + HALO - 8 * (level + 1))
            cur = read(lo, hi) + read(lo + k, hi + k)
        if last:
            return cur
        levels[level, lo:hi, :] = cur
        read = lambda a, b, level=level: levels[level, a:b, :]
        k, level = 2 * k, level + 1


def _pool_parts(u_cols, ubuf, cols, w, row, tm, levels):
    ws = _window_sums(ubuf, cols, w, tm, levels, trailing=True)
    cnt = jnp.minimum(row + 1, w).astype(F32)
    return ws / cnt - u_cols, cnt


def _mixer_forward(x, g, win_t, wout_x, conv_w, pool_w, pool_scale, cargo=()):
    t, d = x.shape
    dc = win_t.shape[0] // 4
    gcw = dc // len(POOL_WINDOWS)
    wo_rows = d // N_CHIPS
    wo_stride = wout_x.shape[0] // N_CHIPS
    tm = min(TM_MIX, t)

    def body(x_ref, g_ref, win_hbm, wout_hbm, cw_ref, pw_ref, ps_ref, xo_ref, proj_ref, y_ref,
             win, wout, zbuf, ubuf, levels, sems):
        i = pl.program_id(0)

        @pl.when(i == 0)
        def _():
            pairs = [(win_hbm, win)]
            for k in range(N_CHIPS):
                pairs.append((wout_hbm.at[pl.ds(k * wo_stride, wo_rows), :], wout.at[pl.ds(k * wo_rows, wo_rows), :]))
            _load_rows(pairs, sems)
            zbuf[0:8, :] = jnp.zeros((8, dc), F32)
            ubuf[0:HALO, :] = jnp.zeros((HALO, dc), F32)

        xv = x_ref[...]
        r = lax.rsqrt(jnp.mean(xv * xv, axis=-1, keepdims=True) + EPS)
        h = (xv * r * g_ref[...]).astype(BF16)
        v = _nt(h, win[0:dc, :])
        gb = _nt(h, win[dc:2 * dc, :])
        gc = _nt(h, win[2 * dc:3 * dc, :])
        u = _nt(h, win[3 * dc:4 * dc, :])
        proj_ref[:, 0:dc] = v.astype(BF16)
        proj_ref[:, dc:2 * dc] = gb.astype(BF16)
        proj_ref[:, 2 * dc:3 * dc] = gc.astype(BF16)
        proj_ref[:, 3 * dc:4 * dc] = u.astype(BF16)

        z = gc * v
        zbuf[8:8 + tm, :] = z
        cw = cw_ref[...]
        conv = cw[2:3, :] * z + cw[1:2, :] * zbuf[7:7 + tm, :] + cw[0:1, :] * zbuf[6:6 + tm, :]
        y_ref[:, 0:dc] = (gb * conv).astype(BF16)

        ubuf[HALO:HALO + tm, :] = u
        row = i * tm + lax.broadcasted_iota(jnp.int32, (tm, 1), 0)
        for gi, w in enumerate(POOL_WINDOWS):
            cols = slice(gi * gcw, (gi + 1) * gcw)
            pooled, _ = _pool_parts(u[:, cols], ubuf, cols, w, row, tm, levels)
            yb = _nn(pooled.astype(BF16), pw_ref[gi].astype(BF16)) * ps_ref[:, cols]
            y_ref[:, dc + gi * gcw:dc + (gi + 1) * gcw] = yb.astype(BF16)

        xo_ref[...] = xv + _nn(y_ref[...], wout[...])
        zbuf[0:8, :] = zbuf[tm:tm + 8, :]
        ubuf[0:HALO, :] = ubuf[tm:tm + HALO, :]

    tok = lambda i: (i, 0)
    one = lambda i: (0, 0)
    return _launch(
        body, name="mixer_forward", grid=(t // tm,),
        in_specs=[pl.BlockSpec((tm, d), tok), pl.BlockSpec((1, d), one), HBM_SPEC, HBM_SPEC,
                  pl.BlockSpec(conv_w.shape, one), pl.BlockSpec(pool_w.shape, lambda i: (0, 0, 0)), pl.BlockSpec((1, dc), one)],
        out_specs=[pl.BlockSpec((tm, d), tok), pl.BlockSpec((tm, 4 * dc), tok), pl.BlockSpec((tm, 2 * dc), tok)],
        out_shape=[jax.ShapeDtypeStruct((t, d), F32), jax.ShapeDtypeStruct((t, 4 * dc), BF16), jax.ShapeDtypeStruct((t, 2 * dc), BF16)],
        scratch_shapes=[pltpu.VMEM((4 * dc, d), BF16), pltpu.VMEM((2 * dc, d), BF16),
                        pltpu.VMEM((tm + 8, dc), F32), pltpu.VMEM((tm + HALO, dc), F32),
                        pltpu.VMEM((WINDOW_LEVELS, tm + HALO, gcw), F32), pltpu.SemaphoreType.DMA((1 + N_CHIPS,))],
        args=(x, g, win_t, wout_x, conv_w, pool_w, pool_scale), cargo=cargo)


def _mixer_backward(dxo, x, g, proj, win_t, wout_x, conv_w, pool_w, pool_scale, cargo=()):
    t, d = x.shape
    dc = win_t.shape[0] // 4
    ng = len(POOL_WINDOWS)
    gcw = dc // ng
    wo_rows = d // N_CHIPS
    wo_stride = wout_x.shape[0] // N_CHIPS
    tm = min(TM_MIX, t)
    n_tiles = t // tm
    hb = tm // HALO

    def body(dxo_ref, x_ref, g_ref, proj_ref, halo_ref, win_hbm, wout_hbm, cw_ref, pw_ref, ps_ref,
             dx_ref, dproj_ref, h_ref, dxob_ref, dg_ref, dcw_ref, dps_ref, dpw_ref,
             win, wout, zbuf, ubuf, dcbuf, ebuf, levels, sems):
        i = pl.program_id(0)
        tile = n_tiles - 1 - i

        @pl.when(i == 0)
        def _():
            pairs = [(win_hbm, win)]
            for k in range(N_CHIPS):
                pairs.append((wout_hbm.at[pl.ds(k * wo_stride, wo_rows), :], wout.at[pl.ds(k * wo_rows, wo_rows), :]))
            _load_rows(pairs, sems)
            dcbuf[tm:tm + 8, :] = jnp.zeros((8, dc), F32)
            ebuf[tm:tm + HALO, :] = jnp.zeros((HALO, dc), F32)
            dg_ref[...] = jnp.zeros_like(dg_ref)
            dcw_ref[...] = jnp.zeros_like(dcw_ref)
            dps_ref[...] = jnp.zeros_like(dps_ref)
            dpw_ref[...] = jnp.zeros_like(dpw_ref)

        xv = x_ref[...]
        gv = g_ref[...]
        r = lax.rsqrt(jnp.mean(xv * xv, axis=-1, keepdims=True) + EPS)
        xhat = xv * r
        h_ref[...] = (xhat * gv).astype(BF16)
        dxo_v = dxo_ref[...]
        dxo_b = dxo_v.astype(BF16)
        dxob_ref[...] = dxo_b

        v = proj_ref[:, 0:dc].astype(F32)
        gb = proj_ref[:, dc:2 * dc].astype(F32)
        gc = proj_ref[:, 2 * dc:3 * dc].astype(F32)
        u = proj_ref[:, 3 * dc:4 * dc].astype(F32)
        first = jnp.where(tile > 0, 1.0, 0.0)
        zbuf[0:HALO, :] = halo_ref[:, 2 * dc:3 * dc].astype(F32) * halo_ref[:, 0:dc].astype(F32) * first
        ubuf[0:HALO, :] = halo_ref[:, 3 * dc:4 * dc].astype(F32) * first
        z = gc * v
        zbuf[HALO:HALO + tm, :] = z
        ubuf[HALO:HALO + tm, :] = u
        z1 = zbuf[HALO - 1:HALO - 1 + tm, :]
        z2 = zbuf[HALO - 2:HALO - 2 + tm, :]
        cw = cw_ref[...]
        conv = cw[2:3, :] * z + cw[1:2, :] * z1 + cw[0:1, :] * z2

        dy = _nt(dxo_b, wout[...])
        dya = dy[:, 0:dc]
        dgb = dya * conv
        dconv = dya * gb
        dcbuf[0:tm, :] = dconv
        dz = cw[2:3, :] * dconv + cw[1:2, :] * dcbuf[1:1 + tm, :] + cw[0:1, :] * dcbuf[2:2 + tm, :]
        dgc = dz * v
        dv = dz * gc
        dcw_ref[0:1, :] += jnp.sum(dconv * z2, axis=0, keepdims=True)
        dcw_ref[1:2, :] += jnp.sum(dconv * z1, axis=0, keepdims=True)
        dcw_ref[2:3, :] += jnp.sum(dconv * z, axis=0, keepdims=True)

        dproj_ref[:, 0:dc] = dv.astype(BF16)
        dproj_ref[:, dc:2 * dc] = dgb.astype(BF16)
        dproj_ref[:, 2 * dc:3 * dc] = dgc.astype(BF16)

        row = tile * tm + lax.broadcasted_iota(jnp.int32, (tm, 1), 0)
        for gi, w in enumerate(POOL_WINDOWS):
            cols = slice(gi * gcw, (gi + 1) * gcw)
            pooled, cnt = _pool_parts(u[:, cols], ubuf, cols, w, row, tm, levels)
            pooled_b = pooled.astype(BF16)
            pw_b = pw_ref[gi].astype(BF16)
            dyb = dy[:, dc + gi * gcw:dc + (gi + 1) * gcw]
            q = _nn(pooled_b, pw_b)
            dps_ref[:, cols] += jnp.sum(q * dyb, axis=0, keepdims=True)
            dq = (dyb * ps_ref[:, cols]).astype(BF16)
            dpw_ref[gi] += _tn(pooled_b, dq)
            dpooled = _nt(dq, pw_b)
            ebuf[0:tm, cols] = dpooled / cnt
            du = _window_sums(ebuf, cols, w, tm, levels, trailing=False) - dpooled
            dproj_ref[:, 3 * dc + gi * gcw:3 * dc + (gi + 1) * gcw] = du.astype(BF16)

        dh = _nn(dproj_ref[...], win[...])
        dg_ref[...] += jnp.sum(dh * xhat, axis=0, keepdims=True)
        dxh = dh * gv
        dx_ref[...] = dxo_v + r * (dxh - xhat * jnp.mean(dxh * xhat, axis=-1, keepdims=True))
        dcbuf[tm:tm + 8, :] = dcbuf[0:8, :]
        ebuf[tm:tm + HALO, :] = ebuf[0:HALO, :]

    tok = lambda i: (n_tiles - 1 - i, 0)
    halo = lambda i: (jnp.maximum((n_tiles - 1 - i) * hb - 1, 0), 0)
    one = lambda i: (0, 0)
    return _launch(
        body, name="mixer_backward", grid=(n_tiles,),
        in_specs=[pl.BlockSpec((tm, d), tok), pl.BlockSpec((tm, d), tok), pl.BlockSpec((1, d), one),
                  pl.BlockSpec((tm, 4 * dc), tok), pl.BlockSpec((HALO, 4 * dc), halo), HBM_SPEC, HBM_SPEC,
                  pl.BlockSpec(conv_w.shape, one), pl.BlockSpec(pool_w.shape, lambda i: (0, 0, 0)), pl.BlockSpec((1, dc), one)],
        out_specs=[pl.BlockSpec((tm, d), tok), pl.BlockSpec((tm, 4 * dc), tok), pl.BlockSpec((tm, d), tok), pl.BlockSpec((tm, d), tok),
                   pl.BlockSpec((1, d), one), pl.BlockSpec(conv_w.shape, one), pl.BlockSpec((1, dc), one),
                   pl.BlockSpec(pool_w.shape, lambda i: (0, 0, 0))],
        out_shape=[jax.ShapeDtypeStruct((t, d), F32), jax.ShapeDtypeStruct((t, 4 * dc), BF16), jax.ShapeDtypeStruct((t, d), BF16),
                   jax.ShapeDtypeStruct((t, d), BF16), jax.ShapeDtypeStruct((1, d), F32), jax.ShapeDtypeStruct(conv_w.shape, F32),
                   jax.ShapeDtypeStruct((1, dc), F32), jax.ShapeDtypeStruct(pool_w.shape, F32)],
        scratch_shapes=[pltpu.VMEM((4 * dc, d), BF16), pltpu.VMEM((2 * dc, d), BF16),
                        pltpu.VMEM((tm + HALO, dc), F32), pltpu.VMEM((tm + HALO, dc), F32),
                        pltpu.VMEM((tm + 8, dc), F32), pltpu.VMEM((tm + HALO, dc), F32),
                        pltpu.VMEM((WINDOW_LEVELS, tm + HALO, gcw), F32), pltpu.SemaphoreType.DMA((1 + N_CHIPS,))],
        args=(dxo, x, g, proj, proj, win_t, wout_x, conv_w, pool_w, pool_scale), cargo=cargo)


def _adam_update(w, gv, m, v):
    m_new = ADAM_B1 * m + (1.0 - ADAM_B1) * gv
    v_new = ADAM_B2 * v + (1.0 - ADAM_B2) * (gv * gv)
    m_hat = m_new / (1.0 - ADAM_B1 ** ADAM_STEP)
    v_hat = v_new / (1.0 - ADAM_B2 ** ADAM_STEP)
    return -ADAM_LR * (m_hat / (jnp.sqrt(v_hat) + ADAM_EPS) + ADAM_WD * w), m_new, v_new


def _adamw(w, grad, m, v, name):
    rows, cols = w.shape
    br = _row_block(rows, 256) if rows >= 8 else rows

    def body(w_ref, g_ref, m_ref, v_ref, d_ref, mo_ref, vo_ref):
        d_ref[...], mo_ref[...], vo_ref[...] = _adam_update(w_ref[...], g_ref[...], m_ref[...], v_ref[...])

    blk = pl.BlockSpec((br, cols), lambda i: (i, 0))
    return pl.pallas_call(
        body, name=name,
        out_shape=[jax.ShapeDtypeStruct((rows, cols), F32)] * 3,
        grid=(rows // br,), in_specs=[blk] * 4, out_specs=[blk] * 3,
        compiler_params=pltpu.CompilerParams(dimension_semantics=("parallel",)),
    )(w, grad, m, v)


def _adamw_transposed(w, grad_t, m, v, name):
    _, rows, cols = w.shape
    br = 256 if rows % 256 == 0 else rows

    def body(w_ref, gt_ref, m_ref, v_ref, g_ref, d_ref, mo_ref, vo_ref):
        gv = gt_ref[...].T
        g_ref[...] = gv
        d_ref[...], mo_ref[...], vo_ref[...] = _adam_update(w_ref[...], gv, m_ref[...], v_ref[...])

    blk = pl.BlockSpec((None, br, cols), lambda i: (0, i, 0))
    return pl.pallas_call(
        body, name=name,
        out_shape=[jax.ShapeDtypeStruct((1, rows, cols), F32)] * 4,
        grid=(rows // br,), in_specs=[blk, pl.BlockSpec((cols, br), lambda i: (0, i)), blk, blk], out_specs=[blk] * 4,
        compiler_params=pltpu.CompilerParams(dimension_semantics=("parallel",)),
    )(w, grad_t, m, v)


def _f32_rows_as_bf16(a, rows, cols):
    bits = lax.bitcast_convert_type(a, BF16).reshape(a.shape[0], 2 * a.shape[1])
    return jnp.pad(bits, ((0, rows - bits.shape[0]), (0, cols - bits.shape[1])))


def kernel(x, norm_ffn1, ffn1_w_gate, ffn1_w_up, ffn1_w_down, norm_mix, w_in, conv_w, pool_w, pool_scale, w_out, norm_ffn2, ffn2_w_gate, ffn2_w_up, ffn2_w_down, norm_final, loss_target, m_norm_ffn1, m_ffn1_w_gate, m_ffn1_w_up, m_ffn1_w_down, m_norm_mix, m_w_in, m_conv_w, m_pool_w, m_pool_scale, m_w_out, m_norm_ffn2, m_ffn2_w_gate, m_ffn2_w_up, m_ffn2_w_down, m_norm_final, v_norm_ffn1, v_ffn1_w_gate, v_ffn1_w_up, v_ffn1_w_down, v_norm_mix, v_w_in, v_conv_w, v_pool_w, v_pool_scale, v_w_out, v_norm_ffn2, v_ffn2_w_gate, v_ffn2_w_up, v_ffn2_w_down, v_norm_final):
    weights = dict(norm_ffn1=norm_ffn1, ffn1_w_gate=ffn1_w_gate, ffn1_w_up=ffn1_w_up, ffn1_w_down=ffn1_w_down, norm_mix=norm_mix,
                   w_in=w_in, conv_w=conv_w, pool_w=pool_w, pool_scale=pool_scale, w_out=w_out, norm_ffn2=norm_ffn2,
                   ffn2_w_gate=ffn2_w_gate, ffn2_w_up=ffn2_w_up, ffn2_w_down=ffn2_w_down, norm_final=norm_final)
    first_m = dict(norm_ffn1=m_norm_ffn1, ffn1_w_gate=m_ffn1_w_gate, ffn1_w_up=m_ffn1_w_up, ffn1_w_down=m_ffn1_w_down,
                   norm_mix=m_norm_mix, w_in=m_w_in, conv_w=m_conv_w, pool_w=m_pool_w, pool_scale=m_pool_scale, w_out=m_w_out,
                   norm_ffn2=m_norm_ffn2, ffn2_w_gate=m_ffn2_w_gate, ffn2_w_up=m_ffn2_w_up, ffn2_w_down=m_ffn2_w_down,
                   norm_final=m_norm_final)
    second_m = dict(norm_ffn1=v_norm_ffn1, ffn1_w_gate=v_ffn1_w_gate, ffn1_w_up=v_ffn1_w_up, ffn1_w_down=v_ffn1_w_down,
                    norm_mix=v_norm_mix, w_in=v_w_in, conv_w=v_conv_w, pool_w=v_pool_w, pool_scale=v_pool_scale, w_out=v_w_out,
                    norm_ffn2=v_norm_ffn2, ffn2_w_gate=v_ffn2_w_gate, ffn2_w_up=v_ffn2_w_up, ffn2_w_down=v_ffn2_w_down,
                    norm_final=v_norm_final)
    names = list(weights)

    xs = x[0]
    tgt = loss_target[0]
    t, d = xs.shape
    dc = pool_scale.shape[1]
    cx, cy, cc = _my_place()
    chip = 2 * cx + cy
    place = jnp.stack([chip, cc]).astype(jnp.int32)

    conv_rows = 32
    wout_x = jnp.concatenate([w_out[0].astype(BF16), _f32_rows_as_bf16(conv_w[0], conv_rows, d)], axis=0)

    g1, gm, g2 = norm_ffn1, norm_mix, norm_ffn2
    gf = norm_final.reshape(1, d)
    pw = pool_w[0]

    (wd1_shard, wg2_shard, wu2_shard, wd2_shard), [(wg1, wu1)] = _cast_to_bf16(
        [ffn1_w_down[0], ffn2_w_gate[0].T, ffn2_w_up[0].T, ffn2_w_down[0]], "gather_ffn1",
        [_gather_cargo([ffn1_w_gate[0].T.astype(BF16), ffn1_w_up[0].T.astype(BF16)])])
    wg1, wu1 = [wg1], [wu1]
    (a1, b1, s1), [(wd1, win_t, wout_g)] = _ffn_up(
        xs, g1, wg1, wu1, "ffn1_up", [_gather_cargo([wd1_shard, w_in[0].T.astype(BF16), wout_x])])
    wd1 = [wd1]
    (x1,), [(wg2,)] = _ffn_down(xs, s1, wd1, "ffn1_down", [_gather_cargo([wg2_shard])])
    wo_rows = w_out.shape[1]
    cshard = conv_w.shape[2]
    conv_bits = wout_g.reshape(N_CHIPS, wo_rows + conv_rows, d)[:, wo_rows:wo_rows + conv_w.shape[1], :2 * cshard]
    conv_full = lax.bitcast_convert_type(conv_bits.reshape(N_CHIPS, conv_w.shape[1], cshard, 2), F32)
    conv_full = jnp.transpose(conv_full, (1, 0, 2)).reshape(conv_w.shape[1], N_CHIPS * cshard)
    (x2, proj, ymix), [(wu2,)] = _mixer_forward(x1, gm, win_t, wout_g, conv_full, pw, pool_scale, [_gather_cargo([wu2_shard])])
    wg2, wu2 = [wg2], [wu2]
    (a2, b2, s2), [(wd2,)] = _ffn_up(x2, g2, wg2, wu2, "ffn2_up", [_gather_cargo([wd2_shard])])
    wd2 = [wd2]
    (dx3, sq_cols, dgf), _ = _ffn_down(x2, s2, wd2, "ffn2_down", loss_head=(gf, tgt))

    (dx2, da2, db2, h3, do2, dg2), _ = _ffn_backward(dx3, x2, g2, a2, b2, wg2, wu2, wd2, "ffn2_backward")
    p_wg2, _ = _weight_grad(da2, h3, "ffn2_gate_grad")
    p_wu2, [(x_wg2,)] = _weight_grad(db2, h3, "ffn2_up_grad", [_exchange_cargo([p_wg2])])
    p_wd2, [(x_wu2,)] = _weight_grad(s2, do2, "ffn2_down_grad", [_exchange_cargo([p_wu2])])

    (dx1, dproj, h2, dx2b, dgm, dcw, dps, dpw), [(x_wd2,)] = _mixer_backward(
        dx2, x1, gm, proj, win_t, wout_g, conv_full, pw, pool_scale, [_exchange_cargo([p_wd2])])

    (dx0, da1, db1, h1, do1, dg1), _ = _ffn_backward(dx1, xs, g1, a1, b1, wg1, wu1, wd1, "ffn1_backward")

    npw = pw.size // d
    head = [dg1, dgm, dg2, dgf, jnp.pad(dps, ((0, 0), (0, d - dc))), jnp.pad(dcw, ((0, 0), (0, d - dc))), sq_cols]
    n_head = sum(h.shape[0] for h in head)
    base = -(-n_head // 8) * 8
    pack = jnp.concatenate(head + [jnp.zeros((base - n_head, d), F32), dpw.reshape(npw, d)], axis=0)

    p_wg1, [(packs,)] = _weight_grad(da1, h1, "ffn1_gate_grad", [_all_gather_small_cargo(pack)])
    p_wu1, [(x_wg1,)] = _weight_grad(db1, h1, "ffn1_up_grad", [_exchange_cargo([p_wg1])])
    p_wd1, [(x_wu1,)] = _weight_grad(s1, do1, "ffn1_down_grad", [_exchange_cargo([p_wu1])])
    p_win, [(x_wd1,)] = _weight_grad(dproj, h2, "w_in_grad", [_exchange_cargo([p_wd1])])
    p_wout, [(x_win,)] = _weight_grad(ymix, dx2b, "w_out_grad", [_exchange_cargo([p_win])])
    x_wout, = _run_cargo(_exchange_cargo([p_wout]), "grad_exchange_last")
    small = _sum_by_device(packs)
    loss = jnp.sum(small[n_head - 1]) * (0.5 / d)

    order = ["wg1", "wu1", "wd1", "win", "wout", "wg2", "wu2", "wd2"]
    pairs = dict(wg1=p_wg1, wu1=p_wu1, wd1=p_wd1, win=p_win, wout=p_wout, wg2=p_wg2, wu2=p_wu2, wd2=p_wd2)
    landed = dict(wg1=x_wg1, wu1=x_wu1, wd1=x_wd1, win=x_win, wout=x_wout, wg2=x_wg2, wu2=x_wu2, wd2=x_wd2)
    both = _sibling_share([_chip_sum(pairs[k], landed[k], place, k) for k in order])
    rwg1, rwu1, rwd1, rwin, rwout, rwg2, rwu2, rwd2 = [b.reshape(2 * b.shape[1], b.shape[2]) for b in both]

    grads = {
        "norm_ffn1": small[0:1], "norm_mix": small[1:2], "norm_ffn2": small[2:3], "norm_final": small[3],
        "pool_scale": small[4:5, :dc],
        "conv_w": lax.dynamic_slice_in_dim(small[5:5 + dcw.shape[0], :dc], chip * cshard, cshard, axis=1)[None],
        "pool_w": small[base:].reshape(pool_w.shape),
        "ffn1_w_down": rwd1[None], "w_out": rwout[None], "ffn2_w_down": rwd2[None],
    }
    by_view = {"ffn1_w_gate": rwg1, "ffn1_w_up": rwu1, "ffn2_w_gate": rwg2, "ffn2_w_up": rwu2}

    deltas, new_m, new_v = {}, {}, {}
    for n in names:
        w = weights[n]
        shape = w.shape
        if n == "w_in":
            grads[n], deltas[n], new_m[n], new_v[n] = _adamw_transposed(w, rwin, first_m[n], second_m[n], "adamw_" + n)
            continue
        if n in by_view:
            view = lambda a: jnp.swapaxes(a, 1, 2)[0]
            back = lambda a: jnp.swapaxes(a[None], 1, 2)
            dl, mo, vo = _adamw(view(w), by_view[n], view(first_m[n]), view(second_m[n]), "adamw_" + n)
            grads[n], deltas[n], new_m[n], new_v[n] = back(by_view[n]), back(dl), back(mo), back(vo)
            continue
        as2d = (lambda a: a.reshape(-1, shape[-1]))
        dl, mo, vo = _adamw(as2d(w), as2d(grads[n]), as2d(first_m[n]), as2d(second_m[n]), "adamw_" + n)
        deltas[n], new_m[n], new_v[n] = dl.reshape(shape), mo.reshape(shape), vo.reshape(shape)
        grads[n] = grads[n].reshape(shape)

    return (loss, dx0[None], *[grads[n] for n in names], *[deltas[n] for n in names],
            *[new_m[n] for n in names], *[new_v[n] for n in names])
```

```python
import jax
import jax.numpy as jnp
from jax import lax
from jax.experimental import pallas as pl
from jax.experimental.pallas import tpu as pltpu

F32 = jnp.float32
BF16 = jnp.bfloat16
MESH = pl.DeviceIdType.MESH

EPS = 1e-6
POOL_WINDOWS = (2, 4, 8, 16)
ADAM_LR = 0.001
ADAM_B1 = 0.9
ADAM_B2 = 0.999
ADAM_EPS = 1e-08
ADAM_WD = 0.01
ADAM_STEP = 10

N_CHIPS = 4
N_DEVICES = 8
MXU_COLS_V7X = 256
VMEM_LIMIT = 56 * 1024 * 1024
TM_FFN = 512
TM_MIX = 512
TM_TN = 1024
HALO = 32
WINDOW_LEVELS = 3
FFN_FWD_CHUNKS = 2
FFN_BWD_CHUNKS = 2


def _nt(a, b):
    return lax.dot_general(a, b, (((1,), (1,)), ((), ())), preferred_element_type=F32)


def _tn(a, b):
    return lax.dot_general(a, b, (((0,), (0,)), ((), ())), preferred_element_type=F32)


def _nn(a, b):
    return jnp.dot(a, b, preferred_element_type=F32)


def _sigmoid(a):
    return 1.0 / (1.0 + jnp.exp(-a))


def _feature_chunks(n, parts):
    assert n % MXU_COLS_V7X == 0
    tiles = n // MXU_COLS_V7X
    out, s0 = [], 0
    for p in range(parts):
        sz = (tiles // parts + (1 if p < tiles % parts else 0)) * MXU_COLS_V7X
        if sz:
            out.append((s0, sz))
            s0 += sz
    return out


def _row_block(rows, cap):
    best = 8
    for b in range(8, min(rows, cap) + 1, 8):
        if rows % b == 0:
            best = b
    assert rows % best == 0
    return best


def _my_place():
    return lax.axis_index("x"), lax.axis_index("y"), lax.axis_index("c")


def _other_chips(x, y):
    return [(1 - x, y), (x, 1 - y), (1 - x, 1 - y)]


HBM_SPEC = pl.BlockSpec(memory_space=pltpu.HBM)


class _Cargo:
    def __init__(self, operands, out_shapes, n_sems, phases, when):
        self.operands, self.out_shapes, self.n_sems = list(operands), list(out_shapes), n_sems
        self.phases, self.when = list(phases), list(when)
        assert len(self.phases) == len(self.when) and self.when[0] == 0.0 and self.when[-1] == 1.0


def _launch(body, *, name, grid, in_specs, out_specs, out_shape, scratch_shapes, args, cargo=()):
    params = pltpu.CompilerParams(dimension_semantics=("arbitrary",) * len(grid), vmem_limit_bytes=VMEM_LIMIT)
    cargos = list(cargo)
    c_operands = [op for cg in cargos for op in cg.operands]
    c_shapes = [sh for cg in cargos for sh in cg.out_shapes]
    counts = [len(in_specs), len(c_operands), len(out_shape), len(c_shapes), len(scratch_shapes), 2 * len(cargos)]

    def carrying(*refs):
        groups, pos = [], 0
        for k in counts:
            groups.append(refs[pos:pos + k])
            pos += k
        ins, c_ins, outs, c_outs, scratch, sems = groups
        parts, pi, po = [], 0, 0
        for n, cg in enumerate(cargos):
            parts.append((c_ins[pi:pi + len(cg.operands)], c_outs[po:po + len(cg.out_shapes)], sems[2 * n], sems[2 * n + 1]))
            pi += len(cg.operands)
            po += len(cg.out_shapes)
        step, steps = 0, 1
        for ax, g in enumerate(grid):
            step = step * g + pl.program_id(ax)
            steps *= g
        todo = {}
        for cg, part in zip(cargos, parts):
            for phase, frac in zip(cg.phases[:-1], cg.when[:-1]):
                todo.setdefault(int(round(frac * (steps - 1))), []).append((phase, part))

        for at in sorted(todo):
            @pl.when(step == at)
            def _(at=at):
                for phase, part in todo[at]:
                    phase(*part)

        body(*ins, *outs, *scratch)

        if cargos:
            @pl.when(step == steps - 1)
            def _():
                for cg, part in zip(cargos, parts):
                    cg.phases[-1](*part)

    sems = [pltpu.SemaphoreType.DMA((cg.n_sems,)) for cg in cargos for _ in range(2)]
    outs = pl.pallas_call(
        carrying, name=name, grid=grid,
        in_specs=list(in_specs) + [HBM_SPEC] * counts[1], out_specs=list(out_specs) + [HBM_SPEC] * counts[3],
        out_shape=list(out_shape) + c_shapes, scratch_shapes=list(scratch_shapes) + sems,
        compiler_params=params)(*args, *c_operands)
    own, rest = list(outs[:counts[2]]), list(outs[counts[2]:])
    carried, po = [], 0
    for cg in cargos:
        carried.append(rest[po:po + len(cg.out_shapes)])
        po += len(cg.out_shapes)
    return own, carried


def _run_cargo(cargo, name):
    n_in, n_out = len(cargo.operands), len(cargo.out_shapes)

    def body(*refs):
        c_ins, c_outs, sems = refs[:n_in], refs[n_in:n_in + n_out], refs[n_in + n_out:]
        for phase in cargo.phases:
            phase(c_ins, c_outs, *sems)

    sem = pltpu.SemaphoreType.DMA((cargo.n_sems,))
    return list(pl.pallas_call(body, name=name, out_shape=cargo.out_shapes, in_specs=[HBM_SPEC] * n_in,
                               out_specs=[HBM_SPEC] * n_out, scratch_shapes=[sem, sem])(*cargo.operands))


def _gather_cargo(shards):
    n = len(shards)
    for s in shards:
        assert s.shape[0] % 32 == 0
    slots = 8

    def steps(ins, outs, send_sems, recv_sems):
        x, y, c = _my_place()
        sibling = (x, y, 1 - c)
        over_x, over_y = (1 - x, y, c), (x, 1 - y, c)
        mine, chip_x, chip_y, chip_d = 2 * x + y, 2 * (1 - x) + y, 2 * x + (1 - y), 2 * (1 - x) + (1 - y)

        def rows_of(a, chip_index, half, part=None):
            rps = shards[a].shape[0]
            hr = rps // 2
            first = -(-hr // 32) * 16
            offset, size = {None: (0, hr), 0: (0, first), 1: (first, hr - first)}[part]
            return outs[a].at[pl.ds(pl.multiple_of(chip_index * rps + half * hr + offset, 16), size), :]

        def remote(a, slot, src, dst, to):
            return pltpu.make_async_remote_copy(
                src_ref=src, dst_ref=dst, send_sem=send_sems.at[a * slots + slot], recv_sem=recv_sems.at[a * slots + slot],
                device_id=to, device_id_type=MESH)

        def same_rows(a, slot, rows, to):
            return remote(a, slot, rows, rows, to)

        def own_copy(a):
            rps = shards[a].shape[0]
            return remote(a, 7, ins[a], outs[a].at[pl.ds(pl.multiple_of(mine * rps, 16), rps), :], sibling)

        def my_half(a):
            hr = shards[a].shape[0] // 2
            return ins[a].at[pl.ds(pl.multiple_of(c * hr, 16), hr), :]

        def start():
            for a in range(n):
                own_copy(a).start()
                remote(a, 0, my_half(a), rows_of(a, mine, c), over_x).start()
                remote(a, 1, my_half(a), rows_of(a, mine, c), over_y).start()

        def relay_neighbours():
            for a in range(n):
                same_rows(a, 0, rows_of(a, chip_x, c), over_x).wait_recv()
                same_rows(a, 4, rows_of(a, chip_x, c), sibling).start()
                same_rows(a, 2, rows_of(a, chip_x, c, 0), over_y).start()
                same_rows(a, 1, rows_of(a, chip_y, c), over_y).wait_recv()
                same_rows(a, 5, rows_of(a, chip_y, c), sibling).start()
                same_rows(a, 3, rows_of(a, chip_y, c, 1), over_x).start()

        def relay_diagonal():
            for a in range(n):
                same_rows(a, 2, rows_of(a, chip_d, c, 0), over_y).wait_recv()
                same_rows(a, 3, rows_of(a, chip_d, c, 1), over_x).wait_recv()
                same_rows(a, 6, rows_of(a, chip_d, c), sibling).start()

        def finish():
            for a in range(n):
                for slot, chip_index in ((4, chip_x), (5, chip_y), (6, chip_d)):
                    same_rows(a, slot, rows_of(a, chip_index, 1 - c), sibling).wait_recv()
            for a in range(n):
                remote(a, 0, my_half(a), rows_of(a, mine, c), over_x).wait_send()
                remote(a, 1, my_half(a), rows_of(a, mine, c), over_y).wait_send()
                same_rows(a, 2, rows_of(a, chip_x, c, 0), over_y).wait_send()
                same_rows(a, 3, rows_of(a, chip_y, c, 1), over_x).wait_send()
                for slot, chip_index in ((4, chip_x), (5, chip_y), (6, chip_d)):
                    same_rows(a, slot, rows_of(a, chip_index, c), sibling).wait_send()
                own_copy(a).wait()

        return start, relay_neighbours, relay_diagonal, finish

    phases = [lambda *r, k=k: steps(*r)[k]() for k in range(4)]
    return _Cargo(shards, [jax.ShapeDtypeStruct((N_CHIPS * s.shape[0], s.shape[1]), s.dtype) for s in shards], slots * n,
                  phases, [0.0, 0.6, 0.85, 1.0])


def _exchange_cargo(pairs):
    n = len(pairs)

    def copies(ins, outs, send_sems, recv_sems):
        x, y, c = _my_place()
        return [pltpu.make_async_remote_copy(
            src_ref=ins[a].at[2 * chip[0] + chip[1]], dst_ref=outs[a].at[j],
            send_sem=send_sems.at[3 * a + j], recv_sem=recv_sems.at[3 * a + j], device_id=(*chip, c), device_id_type=MESH)
            for a in range(n) for j, chip in enumerate(_other_chips(x, y))]

    def start(*r):
        for cp in copies(*r):
            cp.start()

    def finish(*r):
        for cp in copies(*r):
            cp.wait()

    return _Cargo(pairs, [jax.ShapeDtypeStruct((3,) + p.shape[1:], p.dtype) for p in pairs], 3 * n, [start, finish], [0.0, 1.0])


def _all_gather_small_cargo(pack):
    rows, cols = pack.shape

    def copies(ins, outs, send_sems, recv_sems):
        x, y, c = _my_place()
        me = 4 * x + 2 * y + c
        remote = []
        for f in range(1, N_DEVICES):
            fx, fy, fc = (f >> 2) & 1, (f >> 1) & 1, f & 1
            to = (1 - x if fx else x, 1 - y if fy else y, 1 - c if fc else c)
            remote.append(pltpu.make_async_remote_copy(
                src_ref=ins[0], dst_ref=outs[0].at[me], send_sem=send_sems.at[f - 1], recv_sem=recv_sems.at[f - 1],
                device_id=to, device_id_type=MESH))
        own = pltpu.make_async_copy(ins[0], outs[0].at[me], send_sems.at[N_DEVICES - 1])
        return remote, own

    def start(*r):
        remote, own = copies(*r)
        own.start()
        for cp in remote:
            cp.start()

    def finish(*r):
        remote, own = copies(*r)
        for cp in remote:
            cp.wait()
        own.wait()

    return _Cargo([pack], [jax.ShapeDtypeStruct((N_DEVICES, rows, cols), F32)], N_DEVICES, [start, finish], [0.0, 1.0])


def _sum_by_device(packs):
    n, rows, cols = packs.shape

    def body(p_ref, o_ref):
        acc = p_ref[0]
        for dev in range(1, n):
            acc = acc + p_ref[dev]
        o_ref[...] = acc

    return pl.pallas_call(body, name="small_grads_sum", out_shape=jax.ShapeDtypeStruct((rows, cols), F32))(packs)


def _chip_sum(pair, got, place, tag):
    _, hr, cols = pair.shape
    br = _row_block(hr, 256)

    def body(k_ref, p_ref, r_ref, o_ref):
        acc = p_ref[...].astype(F32)
        for j in range(3):
            acc = acc + r_ref[j].astype(F32)
        o_ref[...] = acc

    return pl.pallas_call(
        body, name="grad_chip_sum_" + tag,
        out_shape=jax.ShapeDtypeStruct((2, hr, cols), F32),
        grid_spec=pltpu.PrefetchScalarGridSpec(
            num_scalar_prefetch=1, grid=(hr // br,),
            in_specs=[pl.BlockSpec((None, br, cols), lambda r, k_ref: (k_ref[0], r, 0)),
                      pl.BlockSpec((3, br, cols), lambda r, k_ref: (0, r, 0))],
            out_specs=pl.BlockSpec((None, br, cols), lambda r, k_ref: (k_ref[1], r, 0))),
        compiler_params=pltpu.CompilerParams(dimension_semantics=("parallel",)),
    )(place, pair, got)


def _sibling_share(halves):
    n = len(halves)

    def body(*refs):
        outs = refs[n:2 * n]
        send_sems, recv_sems = refs[2 * n:]
        x, y, c = _my_place()
        copies = []
        for a in range(n):
            cp = pltpu.make_async_remote_copy(
                src_ref=outs[a].at[c], dst_ref=outs[a].at[c], send_sem=send_sems.at[a], recv_sem=recv_sems.at[a],
                device_id=(x, y, 1 - c), device_id_type=MESH)
            cp.start()
            copies.append(cp)
        for cp in copies:
            cp.wait()

    return pl.pallas_call(
        body, name="grad_share_sibling",
        out_shape=[jax.ShapeDtypeStruct(h.shape, h.dtype) for h in halves],
        in_specs=[HBM_SPEC] * n, out_specs=[HBM_SPEC] * n,
        input_output_aliases={a: a for a in range(n)},
        scratch_shapes=[pltpu.SemaphoreType.DMA((n,)), pltpu.SemaphoreType.DMA((n,))],
    )(*halves)


def _load_rows(pairs, sems):
    cps = [pltpu.make_async_copy(src, dst, sems.at[j]) for j, (src, dst) in enumerate(pairs)]
    for cp in cps:
        cp.start()
    for cp in cps:
        cp.wait()


def _piece_rows(weights):
    return list(weights), (lambda refs, mats: list(zip(refs, mats))), len(weights)


def _cast_to_bf16(arrays, name, cargo=()):
    rows, cols = arrays[0].shape
    n = len(arrays)
    br = _row_block(rows, 256)

    def body(*refs):
        for src, dst in zip(refs[:n], refs[n:]):
            dst[...] = src[...].astype(BF16)

    blk = pl.BlockSpec((br, cols), lambda i: (i, 0))
    return _launch(body, name=name, grid=(rows // br,), in_specs=[blk] * n, out_specs=[blk] * n,
                   out_shape=[jax.ShapeDtypeStruct((rows, cols), BF16)] * n, scratch_shapes=[], args=tuple(arrays), cargo=cargo)


def _loss_head(xv, gv, tv):
    d = xv.shape[-1]
    r = lax.rsqrt(jnp.mean(xv * xv, axis=-1, keepdims=True) + EPS)
    xhat = xv * r
    err = xhat * gv - tv
    dy = err * (1.0 / d)
    dxh = dy * gv
    dx = r * (dxh - xhat * jnp.mean(dxh * xhat, axis=-1, keepdims=True))
    return dx, jnp.sum(err * err, axis=0, keepdims=True), jnp.sum(dy * xhat, axis=0, keepdims=True)


def _ffn_up(x, g, wg_t, wu_t, name, cargo=()):
    t, d = x.shape
    f = wg_t.shape[0]
    tm = min(TM_FFN, t)
    chunks = _feature_chunks(f, FFN_FWD_CHUNKS)
    flat, copies, n_copies = _piece_rows([wg_t, wu_t])
    nw = len(flat)

    def body(x_ref, g_ref, *rest):
        w_hbm, (a_ref, b_ref, s_ref, wg, wu, sems) = rest[:nw], rest[nw:]

        @pl.when(pl.program_id(0) == 0)
        def _():
            _load_rows(copies(w_hbm, [wg, wu]), sems)

        xv = x_ref[...]
        r = lax.rsqrt(jnp.mean(xv * xv, axis=-1, keepdims=True) + EPS)
        h = (xv * r * g_ref[...]).astype(BF16)
        for s0, sz in chunks:
            a = _nt(h, wg[s0:s0 + sz, :])
            b = _nt(h, wu[s0:s0 + sz, :])
            a_ref[:, s0:s0 + sz] = a.astype(BF16)
            b_ref[:, s0:s0 + sz] = b.astype(BF16)
            s_ref[:, s0:s0 + sz] = (a * _sigmoid(a) * b).astype(BF16)

    tok = lambda i: (i, 0)
    wide = pl.BlockSpec((tm, f), tok)
    return _launch(
        body, name=name, grid=(t // tm,),
        in_specs=[pl.BlockSpec((tm, d), tok), pl.BlockSpec((1, d), lambda i: (0, 0))] + [HBM_SPEC] * nw,
        out_specs=[wide, wide, wide], out_shape=[jax.ShapeDtypeStruct((t, f), BF16)] * 3,
        scratch_shapes=[pltpu.VMEM((f, d), BF16), pltpu.VMEM((f, d), BF16), pltpu.SemaphoreType.DMA((n_copies,))],
        args=(x, g, *flat), cargo=cargo)


def _ffn_down(x, s, wd, name, cargo=(), loss_head=None):
    t, d = x.shape
    f = s.shape[1]
    tm = min(TM_FFN, t)
    flat, copies, n_copies = _piece_rows([wd])
    nw = len(flat)
    nl = 2 if loss_head else 0

    def body(x_ref, s_ref, *rest):
        head, w_hbm = rest[:nl], rest[nl:nl + nw]
        xo_ref = rest[nl + nw]
        sums, (wdn, sems) = rest[nl + nw + 1:nl + nw + 1 + nl], rest[nl + nw + 1 + nl:]

        @pl.when(pl.program_id(0) == 0)
        def _():
            _load_rows(copies(w_hbm, [wdn]), sems)
            for sum_ref in sums:
                sum_ref[...] = jnp.zeros_like(sum_ref)

        xo = x_ref[...] + 0.5 * _nn(s_ref[...], wdn[...])
        if loss_head:
            dx, sq, dgf = _loss_head(xo, head[0][...], head[1][...])
            xo_ref[...] = dx
            sums[0][...] += sq
            sums[1][...] += dgf
        else:
            xo_ref[...] = xo

    tok = lambda i: (i, 0)
    one = lambda i: (0, 0)
    return _launch(
        body, name=name, grid=(t // tm,),
        in_specs=[pl.BlockSpec((tm, d), tok), pl.BlockSpec((tm, f), tok)]
        + ([pl.BlockSpec((1, d), one), pl.BlockSpec((tm, d), tok)] if loss_head else []) + [HBM_SPEC] * nw,
        out_specs=[pl.BlockSpec((tm, d), tok)] + [pl.BlockSpec((1, d), one)] * nl,
        out_shape=[jax.ShapeDtypeStruct((t, d), F32)] + [jax.ShapeDtypeStruct((1, d), F32)] * nl,
        scratch_shapes=[pltpu.VMEM((f, d), BF16), pltpu.SemaphoreType.DMA((n_copies,))],
        args=(x, s, *(loss_head or ()), *flat), cargo=cargo)


def _ffn_backward(dxo, x, g, a, b, wg_t, wu_t, wd, name, cargo=()):
    t, d = x.shape
    f = wd.shape[0]
    tm = min(TM_FFN // 2, t)
    chunks = _feature_chunks(f, FFN_BWD_CHUNKS)
    flat, copies, n_copies = _piece_rows([wg_t, wu_t, wd])
    nw = len(flat)

    def body(dxo_ref, x_ref, g_ref, a_ref, b_ref, *rest):
        w_hbm, (dx_ref, da_ref, db_ref, h_ref, do_ref, dg_ref, wg, wu, wdn, sems) = rest[:nw], rest[nw:]

        @pl.when(pl.program_id(0) == 0)
        def _():
            _load_rows(copies(w_hbm, [wg, wu, wdn]), sems)
            dg_ref[...] = jnp.zeros_like(dg_ref)

        xv = x_ref[...]
        gv = g_ref[...]
        r = lax.rsqrt(jnp.mean(xv * xv, axis=-1, keepdims=True) + EPS)
        xhat = xv * r
        h_ref[...] = (xhat * gv).astype(BF16)
        dxo_v = dxo_ref[...]
        dout = (0.5 * dxo_v).astype(BF16)
        do_ref[...] = dout
        dh = jnp.zeros((tm, d), F32)
        for s0, sz in chunks:
            ds = _nt(dout, wdn[s0:s0 + sz, :])
            av = a_ref[:, s0:s0 + sz].astype(F32)
            bv = b_ref[:, s0:s0 + sz].astype(F32)
            sig = _sigmoid(av)
            silu = av * sig
            da = (ds * bv * (sig * (1.0 + av * (1.0 - sig)))).astype(BF16)
            db = (ds * silu).astype(BF16)
            da_ref[:, s0:s0 + sz] = da
            db_ref[:, s0:s0 + sz] = db
            dh = dh + _nn(da, wg[s0:s0 + sz, :]) + _nn(db, wu[s0:s0 + sz, :])
        dg_ref[...] += jnp.sum(dh * xhat, axis=0, keepdims=True)
        dxh = dh * gv
        dx_ref[...] = dxo_v + r * (dxh - xhat * jnp.mean(dxh * xhat, axis=-1, keepdims=True))

    tok = lambda i: (i, 0)
    one = lambda i: (0, 0)
    return _launch(
        body, name=name, grid=(t // tm,),
        in_specs=[pl.BlockSpec((tm, d), tok), pl.BlockSpec((tm, d), tok), pl.BlockSpec((1, d), one),
                  pl.BlockSpec((tm, f), tok), pl.BlockSpec((tm, f), tok)] + [HBM_SPEC] * nw,
        out_specs=[pl.BlockSpec((tm, d), tok), pl.BlockSpec((tm, f), tok), pl.BlockSpec((tm, f), tok),
                   pl.BlockSpec((tm, d), tok), pl.BlockSpec((tm, d), tok), pl.BlockSpec((1, d), one)],
        out_shape=[jax.ShapeDtypeStruct((t, d), F32), jax.ShapeDtypeStruct((t, f), BF16), jax.ShapeDtypeStruct((t, f), BF16),
                   jax.ShapeDtypeStruct((t, d), BF16), jax.ShapeDtypeStruct((t, d), BF16), jax.ShapeDtypeStruct((1, d), F32)],
        scratch_shapes=[pltpu.VMEM((f, d), BF16), pltpu.VMEM((f, d), BF16), pltpu.VMEM((f, d), BF16), pltpu.SemaphoreType.DMA((n_copies,))],
        args=(dxo, x, g, a, b, *flat), cargo=cargo)


def _weight_grad(lhs, rhs, name, cargo=()):
    t, m = lhs.shape
    d = rhs.shape[1]
    tm = min(TM_TN, t)
    nt = t // tm
    nj = 1
    bm = m // nj
    cpb = N_CHIPS // nj
    rps = m // N_CHIPS
    hr = rps // 2
    assert hr % 16 == 0

    def body(l_ref, r_ref, o_ref, acc, stage, recv, send_sems, recv_sems):
        j = pl.program_id(0)
        i = pl.program_id(1)
        @pl.when(i == 0)
        def _():
            acc[...] = jnp.zeros_like(acc)

        acc[...] += _tn(l_ref[...], r_ref[...])

        def pair_sum(jj):
            x, y, c = _my_place()
            copies = []
            for q in range(cpb):
                slot = jj * cpb + q
                stage[slot] = acc[pl.ds(pl.multiple_of(q * rps + (1 - c) * hr, 16), hr), :].astype(BF16)
                cp = pltpu.make_async_remote_copy(
                    src_ref=stage.at[slot], dst_ref=recv.at[slot], send_sem=send_sems.at[slot], recv_sem=recv_sems.at[slot],
                    device_id=(x, y, 1 - c), device_id_type=MESH)
                cp.start()
                copies.append(cp)
            for q, cp in enumerate(copies):
                cp.wait_recv()
                mine = acc[pl.ds(pl.multiple_of(q * rps + c * hr, 16), hr), :]
                o_ref[q] = (mine + recv[jj * cpb + q].astype(F32)).astype(BF16)
            for cp in copies:
                cp.wait_send()

        for jj in range(nj):
            @pl.when(jnp.logical_and(i == nt - 1, j == jj))
            def _():
                pair_sum(jj)

    outs, carried = _launch(
        body, name=name, grid=(nj, nt),
        in_specs=[pl.BlockSpec((tm, bm), lambda j, i: (i, j)), pl.BlockSpec((tm, d), lambda j, i: (i, 0))],
        out_specs=[pl.BlockSpec((cpb, hr, d), lambda j, i: (j, 0, 0))],
        out_shape=[jax.ShapeDtypeStruct((N_CHIPS, hr, d), BF16)],
        scratch_shapes=[pltpu.VMEM((bm, d), F32), pltpu.VMEM((N_CHIPS, hr, d), BF16), pltpu.VMEM((N_CHIPS, hr, d), BF16),
                        pltpu.SemaphoreType.DMA((N_CHIPS,)), pltpu.SemaphoreType.DMA((N_CHIPS,))],
        args=(lhs, rhs), cargo=cargo)
    return outs[0], carried


def _window_sums(src, cols, w, tm, levels, trailing):
    def read_src(lo, hi):
        return src[lo:hi, cols]

    read, k, level = read_src, 1, 0
    while True:
        last = 2 * k == w
        if trailing:
            lo, hi = (HALO if last else 8 * (level + 1)), HALO + tm
            cur = read(lo, hi) + read(lo - k, hi - k)
        else:
            lo, hi = 0, (tm if last else tm + HALO - 8 * (level + 1))
            cur = read(lo, hi) + read(lo + k, hi + k)
        if last:
            return cur
        levels[level, lo:hi, :] = cur
        read = lambda a, b, level=level: levels[level, a:b, :]
        k, level = 2 * k, level + 1


def _pool_parts(u_cols, ubuf, cols, w, row, tm, levels):
    ws = _window_sums(ubuf, cols, w, tm, levels, trailing=True)
    inv = 1.0 / jnp.minimum(row + 1, w).astype(F32)
    return ws * inv - u_cols, inv


def _mixer_forward(x, g, win_t, wout_x, conv_w, pool_w, pool_scale, cargo=()):
    t, d = x.shape
    dc = win_t.shape[0] // 4
    gcw = dc // len(POOL_WINDOWS)
    wo_rows = d // N_CHIPS
    wo_stride = wout_x.shape[0] // N_CHIPS
    tm = min(TM_MIX, t)

    def body(x_ref, g_ref, win_hbm, wout_hbm, cw_ref, pw_ref, ps_ref, xo_ref, proj_ref, y_ref,
             win, wout, zbuf, ubuf, levels, sems):
        i = pl.program_id(0)

        @pl.when(i == 0)
        def _():
            pairs = [(win_hbm, win)]
            for k in range(N_CHIPS):
                pairs.append((wout_hbm.at[pl.ds(k * wo_stride, wo_rows), :], wout.at[pl.ds(k * wo_rows, wo_rows), :]))
            _load_rows(pairs, sems)
            zbuf[0:8, :] = jnp.zeros((8, dc), F32)
            ubuf[0:HALO, :] = jnp.zeros((HALO, dc), F32)

        xv = x_ref[...]
        r = lax.rsqrt(jnp.mean(xv * xv, axis=-1, keepdims=True) + EPS)
        h = (xv * r * g_ref[...]).astype(BF16)
        v = _nt(h, win[0:dc, :])
        gb = _nt(h, win[dc:2 * dc, :])
        gc = _nt(h, win[2 * dc:3 * dc, :])
        u = _nt(h, win[3 * dc:4 * dc, :])
        proj_ref[:, 0:dc] = v.astype(BF16)
        proj_ref[:, dc:2 * dc] = gb.astype(BF16)
        proj_ref[:, 2 * dc:3 * dc] = gc.astype(BF16)
        proj_ref[:, 3 * dc:4 * dc] = u.astype(BF16)

        z = gc * v
        zbuf[8:8 + tm, :] = z
        cw = cw_ref[...]
        conv = cw[2:3, :] * z + cw[1:2, :] * zbuf[7:7 + tm, :] + cw[0:1, :] * zbuf[6:6 + tm, :]
        y_ref[:, 0:dc] = (gb * conv).astype(BF16)

        ubuf[HALO:HALO + tm, :] = u
        row = i * tm + lax.broadcasted_iota(jnp.int32, (tm, 1), 0)
        for gi, w in enumerate(POOL_WINDOWS):
            cols = slice(gi * gcw, (gi + 1) * gcw)
            pooled, _ = _pool_parts(u[:, cols], ubuf, cols, w, row, tm, levels)
            yb = _nn(pooled.astype(BF16), pw_ref[gi].astype(BF16)) * ps_ref[:, cols]
            y_ref[:, dc + gi * gcw:dc + (gi + 1) * gcw] = yb.astype(BF16)

        xo_ref[...] = xv + _nn(y_ref[...], wout[...])
        zbuf[0:8, :] = zbuf[tm:tm + 8, :]
        ubuf[0:HALO, :] = ubuf[tm:tm + HALO, :]

    tok = lambda i: (i, 0)
    one = lambda i: (0, 0)
    return _launch(
        body, name="mixer_forward", grid=(t // tm,),
        in_specs=[pl.BlockSpec((tm, d), tok), pl.BlockSpec((1, d), one), HBM_SPEC, HBM_SPEC,
                  pl.BlockSpec(conv_w.shape, one), pl.BlockSpec(pool_w.shape, lambda i: (0, 0, 0)), pl.BlockSpec((1, dc), one)],
        out_specs=[pl.BlockSpec((tm, d), tok), pl.BlockSpec((tm, 4 * dc), tok), pl.BlockSpec((tm, 2 * dc), tok)],
        out_shape=[jax.ShapeDtypeStruct((t, d), F32), jax.ShapeDtypeStruct((t, 4 * dc), BF16), jax.ShapeDtypeStruct((t, 2 * dc), BF16)],
        scratch_shapes=[pltpu.VMEM((4 * dc, d), BF16), pltpu.VMEM((2 * dc, d), BF16),
                        pltpu.VMEM((tm + 8, dc), F32), pltpu.VMEM((tm + HALO, dc), F32),
                        pltpu.VMEM((WINDOW_LEVELS, tm + HALO, gcw), F32), pltpu.SemaphoreType.DMA((1 + N_CHIPS,))],
        args=(x, g, win_t, wout_x, conv_w, pool_w, pool_scale), cargo=cargo)


def _mixer_backward(dxo, x, g, proj, win_t, wout_x, conv_w, pool_w, pool_scale, cargo=()):
    t, d = x.shape
    dc = win_t.shape[0] // 4
    ng = len(POOL_WINDOWS)
    gcw = dc // ng
    wo_rows = d // N_CHIPS
    wo_stride = wout_x.shape[0] // N_CHIPS
    tm = min(TM_MIX, t)
    n_tiles = t // tm
    hb = tm // HALO

    def body(dxo_ref, x_ref, g_ref, proj_ref, halo_ref, win_hbm, wout_hbm, cw_ref, pw_ref, ps_ref,
             dx_ref, dproj_ref, h_ref, dxob_ref, dg_ref, dcw_ref, dps_ref, dpw_ref,
             win, wout, zbuf, ubuf, dcbuf, ebuf, levels, sems):
        i = pl.program_id(0)
        tile = n_tiles - 1 - i

        @pl.when(i == 0)
        def _():
            pairs = [(win_hbm, win)]
            for k in range(N_CHIPS):
                pairs.append((wout_hbm.at[pl.ds(k * wo_stride, wo_rows), :], wout.at[pl.ds(k * wo_rows, wo_rows), :]))
            _load_rows(pairs, sems)
            dcbuf[tm:tm + 8, :] = jnp.zeros((8, dc), F32)
            ebuf[tm:tm + HALO, :] = jnp.zeros((HALO, dc), F32)
            dg_ref[...] = jnp.zeros_like(dg_ref)
            dcw_ref[...] = jnp.zeros_like(dcw_ref)
            dps_ref[...] = jnp.zeros_like(dps_ref)
            dpw_ref[...] = jnp.zeros_like(dpw_ref)

        xv = x_ref[...]
        gv = g_ref[...]
        r = lax.rsqrt(jnp.mean(xv * xv, axis=-1, keepdims=True) + EPS)
        xhat = xv * r
        h_ref[...] = (xhat * gv).astype(BF16)
        dxo_v = dxo_ref[...]
        dxo_b = dxo_v.astype(BF16)
        dxob_ref[...] = dxo_b

        v = proj_ref[:, 0:dc].astype(F32)
        gb = proj_ref[:, dc:2 * dc].astype(F32)
        gc = proj_ref[:, 2 * dc:3 * dc].astype(F32)
        u = proj_ref[:, 3 * dc:4 * dc].astype(F32)
        first = jnp.where(tile > 0, 1.0, 0.0)
        zbuf[0:HALO, :] = halo_ref[:, 2 * dc:3 * dc].astype(F32) * halo_ref[:, 0:dc].astype(F32) * first
        ubuf[0:HALO, :] = halo_ref[:, 3 * dc:4 * dc].astype(F32) * first
        z = gc * v
        zbuf[HALO:HALO + tm, :] = z
        ubuf[HALO:HALO + tm, :] = u
        z1 = zbuf[HALO - 1:HALO - 1 + tm, :]
        z2 = zbuf[HALO - 2:HALO - 2 + tm, :]
        cw = cw_ref[...]
        conv = cw[2:3, :] * z + cw[1:2, :] * z1 + cw[0:1, :] * z2

        dy = _nt(dxo_b, wout[...])
        dya = dy[:, 0:dc]
        dgb = dya * conv
        dconv = dya * gb
        dcbuf[0:tm, :] = dconv
        dz = cw[2:3, :] * dconv + cw[1:2, :] * dcbuf[1:1 + tm, :] + cw[0:1, :] * dcbuf[2:2 + tm, :]
        dgc = dz * v
        dv = dz * gc
        dcw_ref[0:1, :] += jnp.sum(dconv * z2, axis=0, keepdims=True)
        dcw_ref[1:2, :] += jnp.sum(dconv * z1, axis=0, keepdims=True)
        dcw_ref[2:3, :] += jnp.sum(dconv * z, axis=0, keepdims=True)

        dproj_ref[:, 0:dc] = dv.astype(BF16)
        dproj_ref[:, dc:2 * dc] = dgb.astype(BF16)
        dproj_ref[:, 2 * dc:3 * dc] = dgc.astype(BF16)

        row = tile * tm + lax.broadcasted_iota(jnp.int32, (tm, 1), 0)
        for gi, w in enumerate(POOL_WINDOWS):
            cols = slice(gi * gcw, (gi + 1) * gcw)
            pooled, inv_cnt = _pool_parts(u[:, cols], ubuf, cols, w, row, tm, levels)
            pooled_b = pooled.astype(BF16)
            pw_b = pw_ref[gi].astype(BF16)
            dyb = dy[:, dc + gi * gcw:dc + (gi + 1) * gcw]
            q = _nn(pooled_b, pw_b)
            dps_ref[:, cols] += jnp.sum(q * dyb, axis=0, keepdims=True)
            dq = (dyb * ps_ref[:, cols]).astype(BF16)
            dpw_ref[gi] += _tn(pooled_b, dq)
            dpooled = _nt(dq, pw_b)
            ebuf[0:tm, cols] = dpooled * inv_cnt
            du = _window_sums(ebuf, cols, w, tm, levels, trailing=False) - dpooled
            dproj_ref[:, 3 * dc + gi * gcw:3 * dc + (gi + 1) * gcw] = du.astype(BF16)

        dh = _nn(dproj_ref[...], win[...])
        dg_ref[...] += jnp.sum(dh * xhat, axis=0, keepdims=True)
        dxh = dh * gv
        dx_ref[...] = dxo_v + r * (dxh - xhat * jnp.mean(dxh * xhat, axis=-1, keepdims=True))
        dcbuf[tm:tm + 8, :] = dcbuf[0:8, :]
        ebuf[tm:tm + HALO, :] = ebuf[0:HALO, :]

    tok = lambda i: (n_tiles - 1 - i, 0)
    halo = lambda i: (jnp.maximum((n_tiles - 1 - i) * hb - 1, 0), 0)
    one = lambda i: (0, 0)
    return _launch(
        body, name="mixer_backward", grid=(n_tiles,),
        in_specs=[pl.BlockSpec((tm, d), tok), pl.BlockSpec((tm, d), tok), pl.BlockSpec((1, d), one),
                  pl.BlockSpec((tm, 4 * dc), tok), pl.BlockSpec((HALO, 4 * dc), halo), HBM_SPEC, HBM_SPEC,
                  pl.BlockSpec(conv_w.shape, one), pl.BlockSpec(pool_w.shape, lambda i: (0, 0, 0)), pl.BlockSpec((1, dc), one)],
        out_specs=[pl.BlockSpec((tm, d), tok), pl.BlockSpec((tm, 4 * dc), tok), pl.BlockSpec((tm, d), tok), pl.BlockSpec((tm, d), tok),
                   pl.BlockSpec((1, d), one), pl.BlockSpec(conv_w.shape, one), pl.BlockSpec((1, dc), one),
                   pl.BlockSpec(pool_w.shape, lambda i: (0, 0, 0))],
        out_shape=[jax.ShapeDtypeStruct((t, d), F32), jax.ShapeDtypeStruct((t, 4 * dc), BF16), jax.ShapeDtypeStruct((t, d), BF16),
                   jax.ShapeDtypeStruct((t, d), BF16), jax.ShapeDtypeStruct((1, d), F32), jax.ShapeDtypeStruct(conv_w.shape, F32),
                   jax.ShapeDtypeStruct((1, dc), F32), jax.ShapeDtypeStruct(pool_w.shape, F32)],
        scratch_shapes=[pltpu.VMEM((4 * dc, d), BF16), pltpu.VMEM((2 * dc, d), BF16),
                        pltpu.VMEM((tm + HALO, dc), F32), pltpu.VMEM((tm + HALO, dc), F32),
                        pltpu.VMEM((tm + 8, dc), F32), pltpu.VMEM((tm + HALO, dc), F32),
                        pltpu.VMEM((WINDOW_LEVELS, tm + HALO, gcw), F32), pltpu.SemaphoreType.DMA((1 + N_CHIPS,))],
        args=(dxo, x, g, proj, proj, win_t, wout_x, conv_w, pool_w, pool_scale), cargo=cargo)


def _adam_update(w, gv, m, v):
    m_new = ADAM_B1 * m + (1.0 - ADAM_B1) * gv
    v_new = ADAM_B2 * v + (1.0 - ADAM_B2) * (gv * gv)
    m_hat = m_new / (1.0 - ADAM_B1 ** ADAM_STEP)
    v_hat = v_new / (1.0 - ADAM_B2 ** ADAM_STEP)
    return -ADAM_LR * (m_hat / (jnp.sqrt(v_hat) + ADAM_EPS) + ADAM_WD * w), m_new, v_new


def _adamw(w, grad, m, v, name):
    rows, cols = w.shape
    br = _row_block(rows, 256) if rows >= 8 else rows

    def body(w_ref, g_ref, m_ref, v_ref, go_ref, d_ref, mo_ref, vo_ref):
        gv = g_ref[...]
        go_ref[...] = gv
        d_ref[...], mo_ref[...], vo_ref[...] = _adam_update(w_ref[...], gv, m_ref[...], v_ref[...])

    blk = pl.BlockSpec((br, cols), lambda i: (i, 0))
    return pl.pallas_call(
        body, name=name,
        out_shape=[jax.ShapeDtypeStruct((rows, cols), F32)] * 4,
        grid=(rows // br,), in_specs=[blk] * 4, out_specs=[blk] * 4,
        compiler_params=pltpu.CompilerParams(dimension_semantics=("parallel",)),
    )(w, grad, m, v)


def _adamw_transposed(w, grad_t, m, v, name):
    _, rows, cols = w.shape
    br = 256 if rows % 256 == 0 else rows

    def body(w_ref, gt_ref, m_ref, v_ref, g_ref, d_ref, mo_ref, vo_ref):
        gv = gt_ref[...].T
        g_ref[...] = gv
        d_ref[...], mo_ref[...], vo_ref[...] = _adam_update(w_ref[...], gv, m_ref[...], v_ref[...])

    blk = pl.BlockSpec((None, br, cols), lambda i: (0, i, 0))
    return pl.pallas_call(
        body, name=name,
        out_shape=[jax.ShapeDtypeStruct((1, rows, cols), F32)] * 4,
        grid=(rows // br,), in_specs=[blk, pl.BlockSpec((cols, br), lambda i: (0, i)), blk, blk], out_specs=[blk] * 4,
        compiler_params=pltpu.CompilerParams(dimension_semantics=("parallel",)),
    )(w, grad_t, m, v)


def _f32_rows_as_bf16(a, rows, cols):
    bits = lax.bitcast_convert_type(a, BF16).reshape(a.shape[0], 2 * a.shape[1])
    return jnp.pad(bits, ((0, rows - bits.shape[0]), (0, cols - bits.shape[1])))


def kernel(x, norm_ffn1, ffn1_w_gate, ffn1_w_up, ffn1_w_down, norm_mix, w_in, conv_w, pool_w, pool_scale, w_out, norm_ffn2, ffn2_w_gate, ffn2_w_up, ffn2_w_down, norm_final, loss_target, m_norm_ffn1, m_ffn1_w_gate, m_ffn1_w_up, m_ffn1_w_down, m_norm_mix, m_w_in, m_conv_w, m_pool_w, m_pool_scale, m_w_out, m_norm_ffn2, m_ffn2_w_gate, m_ffn2_w_up, m_ffn2_w_down, m_norm_final, v_norm_ffn1, v_ffn1_w_gate, v_ffn1_w_up, v_ffn1_w_down, v_norm_mix, v_w_in, v_conv_w, v_pool_w, v_pool_scale, v_w_out, v_norm_ffn2, v_ffn2_w_gate, v_ffn2_w_up, v_ffn2_w_down, v_norm_final):
    weights = dict(norm_ffn1=norm_ffn1, ffn1_w_gate=ffn1_w_gate, ffn1_w_up=ffn1_w_up, ffn1_w_down=ffn1_w_down, norm_mix=norm_mix,
                   w_in=w_in, conv_w=conv_w, pool_w=pool_w, pool_scale=pool_scale, w_out=w_out, norm_ffn2=norm_ffn2,
                   ffn2_w_gate=ffn2_w_gate, ffn2_w_up=ffn2_w_up, ffn2_w_down=ffn2_w_down, norm_final=norm_final)
    first_m = dict(norm_ffn1=m_norm_ffn1, ffn1_w_gate=m_ffn1_w_gate, ffn1_w_up=m_ffn1_w_up, ffn1_w_down=m_ffn1_w_down,
                   norm_mix=m_norm_mix, w_in=m_w_in, conv_w=m_conv_w, pool_w=m_pool_w, pool_scale=m_pool_scale, w_out=m_w_out,
                   norm_ffn2=m_norm_ffn2, ffn2_w_gate=m_ffn2_w_gate, ffn2_w_up=m_ffn2_w_up, ffn2_w_down=m_ffn2_w_down,
                   norm_final=m_norm_final)
    second_m = dict(norm_ffn1=v_norm_ffn1, ffn1_w_gate=v_ffn1_w_gate, ffn1_w_up=v_ffn1_w_up, ffn1_w_down=v_ffn1_w_down,
                    norm_mix=v_norm_mix, w_in=v_w_in, conv_w=v_conv_w, pool_w=v_pool_w, pool_scale=v_pool_scale, w_out=v_w_out,
                    norm_ffn2=v_norm_ffn2, ffn2_w_gate=v_ffn2_w_gate, ffn2_w_up=v_ffn2_w_up, ffn2_w_down=v_ffn2_w_down,
                    norm_final=v_norm_final)
    names = list(weights)

    xs = x[0]
    tgt = loss_target[0]
    t, d = xs.shape
    dc = pool_scale.shape[1]
    cx, cy, cc = _my_place()
    chip = 2 * cx + cy
    place = jnp.stack([chip, cc]).astype(jnp.int32)

    conv_rows = 32
    wout_x = jnp.concatenate([w_out[0].astype(BF16), _f32_rows_as_bf16(conv_w[0], conv_rows, d)], axis=0)

    g1, gm, g2 = norm_ffn1, norm_mix, norm_ffn2
    gf = norm_final.reshape(1, d)
    pw = pool_w[0]

    (wd1_shard, wg2_shard, wu2_shard, wd2_shard), [(wg1, wu1)] = _cast_to_bf16(
        [ffn1_w_down[0], ffn2_w_gate[0].T, ffn2_w_up[0].T, ffn2_w_down[0]], "gather_ffn1",
        [_gather_cargo([ffn1_w_gate[0].T.astype(BF16), ffn1_w_up[0].T.astype(BF16)])])
    (a1, b1, s1), [(wd1, win_t, wout_g)] = _ffn_up(
        xs, g1, wg1, wu1, "ffn1_up", [_gather_cargo([wd1_shard, w_in[0].T.astype(BF16), wout_x])])
    (x1,), [(wg2,)] = _ffn_down(xs, s1, wd1, "ffn1_down", [_gather_cargo([wg2_shard])])
    wo_rows = w_out.shape[1]
    cshard = conv_w.shape[2]
    conv_bits = wout_g.reshape(N_CHIPS, wo_rows + conv_rows, d)[:, wo_rows:wo_rows + conv_w.shape[1], :2 * cshard]
    conv_full = lax.bitcast_convert_type(conv_bits.reshape(N_CHIPS, conv_w.shape[1], cshard, 2), F32)
    conv_full = jnp.transpose(conv_full, (1, 0, 2)).reshape(conv_w.shape[1], N_CHIPS * cshard)
    (x2, proj, ymix), [(wu2,)] = _mixer_forward(x1, gm, win_t, wout_g, conv_full, pw, pool_scale, [_gather_cargo([wu2_shard])])
    (a2, b2, s2), [(wd2,)] = _ffn_up(x2, g2, wg2, wu2, "ffn2_up", [_gather_cargo([wd2_shard])])
    (dx3, sq_cols, dgf), _ = _ffn_down(x2, s2, wd2, "ffn2_down", loss_head=(gf, tgt))

    (dx2, da2, db2, h3, do2, dg2), _ = _ffn_backward(dx3, x2, g2, a2, b2, wg2, wu2, wd2, "ffn2_backward")
    p_wg2, _ = _weight_grad(da2, h3, "ffn2_gate_grad")
    p_wu2, [(x_wg2,)] = _weight_grad(db2, h3, "ffn2_up_grad", [_exchange_cargo([p_wg2])])
    p_wd2, [(x_wu2,)] = _weight_grad(s2, do2, "ffn2_down_grad", [_exchange_cargo([p_wu2])])

    (dx1, dproj, h2, dx2b, dgm, dcw, dps, dpw), [(x_wd2,)] = _mixer_backward(
        dx2, x1, gm, proj, win_t, wout_g, conv_full, pw, pool_scale, [_exchange_cargo([p_wd2])])

    (dx0, da1, db1, h1, do1, dg1), _ = _ffn_backward(dx1, xs, g1, a1, b1, wg1, wu1, wd1, "ffn1_backward")

    npw = pw.size // d
    head = [dg1, dgm, dg2, dgf, jnp.pad(dps, ((0, 0), (0, d - dc))), jnp.pad(dcw, ((0, 0), (0, d - dc))), sq_cols]
    n_head = sum(h.shape[0] for h in head)
    base = -(-n_head // 8) * 8
    pack = jnp.concatenate(head + [jnp.zeros((base - n_head, d), F32), dpw.reshape(npw, d)], axis=0)

    p_wg1, [(packs,)] = _weight_grad(da1, h1, "ffn1_gate_grad", [_all_gather_small_cargo(pack)])
    p_wu1, [(x_wg1,)] = _weight_grad(db1, h1, "ffn1_up_grad", [_exchange_cargo([p_wg1])])
    p_wd1, [(x_wu1,)] = _weight_grad(s1, do1, "ffn1_down_grad", [_exchange_cargo([p_wu1])])
    p_win, [(x_wd1,)] = _weight_grad(dproj, h2, "w_in_grad", [_exchange_cargo([p_wd1])])
    p_wout, [(x_win,)] = _weight_grad(ymix, dx2b, "w_out_grad", [_exchange_cargo([p_win])])
    x_wout, = _run_cargo(_exchange_cargo([p_wout]), "grad_exchange_last")
    small = _sum_by_device(packs)
    loss = jnp.sum(small[n_head - 1]) * (0.5 / d)

    order = ["wg1", "wu1", "wd1", "win", "wout", "wg2", "wu2", "wd2"]
    pairs = dict(wg1=p_wg1, wu1=p_wu1, wd1=p_wd1, win=p_win, wout=p_wout, wg2=p_wg2, wu2=p_wu2, wd2=p_wd2)
    landed = dict(wg1=x_wg1, wu1=x_wu1, wd1=x_wd1, win=x_win, wout=x_wout, wg2=x_wg2, wu2=x_wu2, wd2=x_wd2)
    both = _sibling_share([_chip_sum(pairs[k], landed[k], place, k) for k in order])
    rwg1, rwu1, rwd1, rwin, rwout, rwg2, rwu2, rwd2 = [b.reshape(2 * b.shape[1], b.shape[2]) for b in both]

    grads = {
        "norm_ffn1": small[0:1], "norm_mix": small[1:2], "norm_ffn2": small[2:3], "norm_final": small[3],
        "pool_scale": small[4:5, :dc],
        "conv_w": lax.dynamic_slice_in_dim(small[5:5 + dcw.shape[0], :dc], chip * cshard, cshard, axis=1)[None],
        "pool_w": small[base:].reshape(pool_w.shape),
        "ffn1_w_down": rwd1[None], "w_out": rwout[None], "ffn2_w_down": rwd2[None],
    }
    by_view = {"ffn1_w_gate": rwg1, "ffn1_w_up": rwu1, "ffn2_w_gate": rwg2, "ffn2_w_up": rwu2}

    deltas, new_m, new_v = {}, {}, {}
    for n in names:
        w = weights[n]
        shape = w.shape
        if n == "w_in":
            grads[n], deltas[n], new_m[n], new_v[n] = _adamw_transposed(w, rwin, first_m[n], second_m[n], "adamw_" + n)
            continue
        if n in by_view:
            view = lambda a: jnp.swapaxes(a, 1, 2)[0]
            back = lambda a: jnp.swapaxes(a[None], 1, 2)
            outs = _adamw(view(w), by_view[n], view(first_m[n]), view(second_m[n]), "adamw_" + n)
            grads[n], deltas[n], new_m[n], new_v[n] = [back(o) for o in outs]
            continue
        as2d = (lambda a: a.reshape(-1, shape[-1]))
        outs = _adamw(as2d(w), as2d(grads[n]), as2d(first_m[n]), as2d(second_m[n]), "adamw_" + n)
        grads[n], deltas[n], new_m[n], new_v[n] = [o.reshape(shape) for o in outs]

    return (loss, dx0[None], *[grads[n] for n in names], *[deltas[n] for n in names],
            *[new_m[n] for n in names], *[new_v[n] for n in names])
```

```python
import jax
import jax.numpy as jnp
from jax import lax
from jax.experimental import pallas as pl
from jax.experimental.pallas import tpu as pltpu

F32 = jnp.float32
BF16 = jnp.bfloat16
MESH = pl.DeviceIdType.MESH

EPS = 1e-6
POOL_WINDOWS = (2, 4, 8, 16)
ADAM_LR = 0.001
ADAM_B1 = 0.9
ADAM_B2 = 0.999
ADAM_EPS = 1e-08
ADAM_WD = 0.01
ADAM_STEP = 10

N_CHIPS = 4
N_DEVICES = 8
MXU_COLS_V7X = 256
VMEM_LIMIT = 56 * 1024 * 1024
TM_FFN = 512
TM_MIX = 512
TM_TN = 1024
HALO = 32
WINDOW_LEVELS = 3
FFN_FWD_CHUNKS = 2
FFN_BWD_CHUNKS = 2


def _nt(a, b):
    return lax.dot_general(a, b, (((1,), (1,)), ((), ())), preferred_element_type=F32)


def _tn(a, b):
    return lax.dot_general(a, b, (((0,), (0,)), ((), ())), preferred_element_type=F32)


def _nn(a, b):
    return jnp.dot(a, b, preferred_element_type=F32)


def _sigmoid(a):
    return 1.0 / (1.0 + jnp.exp(-a))


def _feature_chunks(n, parts):
    assert n % MXU_COLS_V7X == 0
    tiles = n // MXU_COLS_V7X
    out, s0 = [], 0
    for p in range(parts):
        sz = (tiles // parts + (1 if p < tiles % parts else 0)) * MXU_COLS_V7X
        if sz:
            out.append((s0, sz))
            s0 += sz
    return out


def _row_block(rows, cap):
    best = 8
    for b in range(8, min(rows, cap) + 1, 8):
        if rows % b == 0:
            best = b
    assert rows % best == 0
    return best


def _my_place():
    return lax.axis_index("x"), lax.axis_index("y"), lax.axis_index("c")


def _other_chips(x, y):
    return [(1 - x, y), (x, 1 - y), (1 - x, 1 - y)]


HBM_SPEC = pl.BlockSpec(memory_space=pltpu.HBM)


class _Cargo:
    def __init__(self, operands, out_shapes, n_sems, phases, when):
        self.operands, self.out_shapes, self.n_sems = list(operands), list(out_shapes), n_sems
        self.phases, self.when = list(phases), list(when)
        assert len(self.phases) == len(self.when) and self.when[0] == 0.0 and self.when[-1] == 1.0


def _launch(body, *, name, grid, in_specs, out_specs, out_shape, scratch_shapes, args, cargo=()):
    params = pltpu.CompilerParams(dimension_semantics=("arbitrary",) * len(grid), vmem_limit_bytes=VMEM_LIMIT)
    cargos = list(cargo)
    c_operands = [op for cg in cargos for op in cg.operands]
    c_shapes = [sh for cg in cargos for sh in cg.out_shapes]
    counts = [len(in_specs), len(c_operands), len(out_shape), len(c_shapes), len(scratch_shapes), 2 * len(cargos)]

    def carrying(*refs):
        groups, pos = [], 0
        for k in counts:
            groups.append(refs[pos:pos + k])
            pos += k
        ins, c_ins, outs, c_outs, scratch, sems = groups
        parts, pi, po = [], 0, 0
        for n, cg in enumerate(cargos):
            parts.append((c_ins[pi:pi + len(cg.operands)], c_outs[po:po + len(cg.out_shapes)], sems[2 * n], sems[2 * n + 1]))
            pi += len(cg.operands)
            po += len(cg.out_shapes)
        step, steps = 0, 1
        for ax, g in enumerate(grid):
            step = step * g + pl.program_id(ax)
            steps *= g
        todo = {}
        for cg, part in zip(cargos, parts):
            for phase, frac in zip(cg.phases[:-1], cg.when[:-1]):
                todo.setdefault(int(round(frac * (steps - 1))), []).append((phase, part))

        for at in sorted(todo):
            @pl.when(step == at)
            def _(at=at):
                for phase, part in todo[at]:
                    phase(*part)

        body(*ins, *outs, *scratch)

        if cargos:
            @pl.when(step == steps - 1)
            def _():
                for cg, part in zip(cargos, parts):
                    cg.phases[-1](*part)

    sems = [pltpu.SemaphoreType.DMA((cg.n_sems,)) for cg in cargos for _ in range(2)]
    outs = pl.pallas_call(
        carrying, name=name, grid=grid,
        in_specs=list(in_specs) + [HBM_SPEC] * counts[1], out_specs=list(out_specs) + [HBM_SPEC] * counts[3],
        out_shape=list(out_shape) + c_shapes, scratch_shapes=list(scratch_shapes) + sems,
        compiler_params=params)(*args, *c_operands)
    own, rest = list(outs[:counts[2]]), list(outs[counts[2]:])
    carried, po = [], 0
    for cg in cargos:
        carried.append(rest[po:po + len(cg.out_shapes)])
        po += len(cg.out_shapes)
    return own, carried


def _run_cargo(cargo, name):
    n_in, n_out = len(cargo.operands), len(cargo.out_shapes)

    def body(*refs):
        c_ins, c_outs, sems = refs[:n_in], refs[n_in:n_in + n_out], refs[n_in + n_out:]
        for phase in cargo.phases:
            phase(c_ins, c_outs, *sems)

    sem = pltpu.SemaphoreType.DMA((cargo.n_sems,))
    return list(pl.pallas_call(body, name=name, out_shape=cargo.out_shapes, in_specs=[HBM_SPEC] * n_in,
                               out_specs=[HBM_SPEC] * n_out, scratch_shapes=[sem, sem])(*cargo.operands))


def _gather_cargo(shards):
    n = len(shards)
    for s in shards:
        assert s.shape[0] % 32 == 0
    slots = 8

    def steps(ins, outs, send_sems, recv_sems):
        x, y, c = _my_place()
        sibling = (x, y, 1 - c)
        over_x, over_y = (1 - x, y, c), (x, 1 - y, c)
        mine, chip_x, chip_y, chip_d = 2 * x + y, 2 * (1 - x) + y, 2 * x + (1 - y), 2 * (1 - x) + (1 - y)

        def rows_of(a, chip_index, half, part=None):
            rps = shards[a].shape[0]
            hr = rps // 2
            first = -(-hr // 32) * 16
            offset, size = {None: (0, hr), 0: (0, first), 1: (first, hr - first)}[part]
            return outs[a].at[pl.ds(pl.multiple_of(chip_index * rps + half * hr + offset, 16), size), :]

        def remote(a, slot, src, dst, to):
            return pltpu.make_async_remote_copy(
                src_ref=src, dst_ref=dst, send_sem=send_sems.at[a * slots + slot], recv_sem=recv_sems.at[a * slots + slot],
                device_id=to, device_id_type=MESH)

        def same_rows(a, slot, rows, to):
            return remote(a, slot, rows, rows, to)

        def own_copy(a):
            rps = shards[a].shape[0]
            return remote(a, 7, ins[a], outs[a].at[pl.ds(pl.multiple_of(mine * rps, 16), rps), :], sibling)

        def my_half(a):
            hr = shards[a].shape[0] // 2
            return ins[a].at[pl.ds(pl.multiple_of(c * hr, 16), hr), :]

        def start():
            for a in range(n):
                own_copy(a).start()
                remote(a, 0, my_half(a), rows_of(a, mine, c), over_x).start()
                remote(a, 1, my_half(a), rows_of(a, mine, c), over_y).start()

        def relay_neighbours():
            for a in range(n):
                same_rows(a, 0, rows_of(a, chip_x, c), over_x).wait_recv()
                same_rows(a, 4, rows_of(a, chip_x, c), sibling).start()
                same_rows(a, 2, rows_of(a, chip_x, c, 0), over_y).start()
                same_rows(a, 1, rows_of(a, chip_y, c), over_y).wait_recv()
                same_rows(a, 5, rows_of(a, chip_y, c), sibling).start()
                same_rows(a, 3, rows_of(a, chip_y, c, 1), over_x).start()

        def relay_diagonal():
            for a in range(n):
                same_rows(a, 2, rows_of(a, chip_d, c, 0), over_y).wait_recv()
                same_rows(a, 3, rows_of(a, chip_d, c, 1), over_x).wait_recv()
                same_rows(a, 6, rows_of(a, chip_d, c), sibling).start()

        def finish():
            for a in range(n):
                for slot, chip_index in ((4, chip_x), (5, chip_y), (6, chip_d)):
                    same_rows(a, slot, rows_of(a, chip_index, 1 - c), sibling).wait_recv()
            for a in range(n):
                remote(a, 0, my_half(a), rows_of(a, mine, c), over_x).wait_send()
                remote(a, 1, my_half(a), rows_of(a, mine, c), over_y).wait_send()
                same_rows(a, 2, rows_of(a, chip_x, c, 0), over_y).wait_send()
                same_rows(a, 3, rows_of(a, chip_y, c, 1), over_x).wait_send()
                for slot, chip_index in ((4, chip_x), (5, chip_y), (6, chip_d)):
                    same_rows(a, slot, rows_of(a, chip_index, c), sibling).wait_send()
                own_copy(a).wait()

        return start, relay_neighbours, relay_diagonal, finish

    phases = [lambda *r, k=k: steps(*r)[k]() for k in range(4)]
    return _Cargo(shards, [jax.ShapeDtypeStruct((N_CHIPS * s.shape[0], s.shape[1]), s.dtype) for s in shards], slots * n,
                  phases, [0.0, 0.6, 0.85, 1.0])


def _exchange_cargo(pairs):
    n = len(pairs)

    def copies(ins, outs, send_sems, recv_sems):
        x, y, c = _my_place()
        return [pltpu.make_async_remote_copy(
            src_ref=ins[a].at[2 * chip[0] + chip[1]], dst_ref=outs[a].at[j],
            send_sem=send_sems.at[3 * a + j], recv_sem=recv_sems.at[3 * a + j], device_id=(*chip, c), device_id_type=MESH)
            for a in range(n) for j, chip in enumerate(_other_chips(x, y))]

    def start(*r):
        for cp in copies(*r):
            cp.start()

    def finish(*r):
        for cp in copies(*r):
            cp.wait()

    return _Cargo(pairs, [jax.ShapeDtypeStruct((3,) + p.shape[1:], p.dtype) for p in pairs], 3 * n, [start, finish], [0.0, 1.0])


def _all_gather_small_cargo(pack):
    rows, cols = pack.shape

    def copies(ins, outs, send_sems, recv_sems):
        x, y, c = _my_place()
        me = 4 * x + 2 * y + c
        remote = []
        for f in range(1, N_DEVICES):
            fx, fy, fc = (f >> 2) & 1, (f >> 1) & 1, f & 1
            to = (1 - x if fx else x, 1 - y if fy else y, 1 - c if fc else c)
            remote.append(pltpu.make_async_remote_copy(
                src_ref=ins[0], dst_ref=outs[0].at[me], send_sem=send_sems.at[f - 1], recv_sem=recv_sems.at[f - 1],
                device_id=to, device_id_type=MESH))
        own = pltpu.make_async_copy(ins[0], outs[0].at[me], send_sems.at[N_DEVICES - 1])
        return remote, own

    def start(*r):
        remote, own = copies(*r)
        own.start()
        for cp in remote:
            cp.start()

    def finish(*r):
        remote, own = copies(*r)
        for cp in remote:
            cp.wait()
        own.wait()

    return _Cargo([pack], [jax.ShapeDtypeStruct((N_DEVICES, rows, cols), F32)], N_DEVICES, [start, finish], [0.0, 1.0])


def _sum_by_device(packs):
    n, rows, cols = packs.shape

    def body(p_ref, o_ref):
        acc = p_ref[0]
        for dev in range(1, n):
            acc = acc + p_ref[dev]
        o_ref[...] = acc

    return pl.pallas_call(body, name="small_grads_sum", out_shape=jax.ShapeDtypeStruct((rows, cols), F32))(packs)


def _chip_sum(pair, got, place, tag):
    _, hr, cols = pair.shape
    br = _row_block(hr, 256)

    def body(k_ref, p_ref, r_ref, o_ref):
        acc = p_ref[...].astype(F32)
        for j in range(3):
            acc = acc + r_ref[j].astype(F32)
        o_ref[...] = acc

    return pl.pallas_call(
        body, name="grad_chip_sum_" + tag,
        out_shape=jax.ShapeDtypeStruct((2, hr, cols), F32),
        grid_spec=pltpu.PrefetchScalarGridSpec(
            num_scalar_prefetch=1, grid=(hr // br,),
            in_specs=[pl.BlockSpec((None, br, cols), lambda r, k_ref: (k_ref[0], r, 0)),
                      pl.BlockSpec((3, br, cols), lambda r, k_ref: (0, r, 0))],
            out_specs=pl.BlockSpec((None, br, cols), lambda r, k_ref: (k_ref[1], r, 0))),
        compiler_params=pltpu.CompilerParams(dimension_semantics=("parallel",)),
    )(place, pair, got)


def _sibling_share(halves):
    n = len(halves)

    def body(*refs):
        outs = refs[n:2 * n]
        send_sems, recv_sems = refs[2 * n:]
        x, y, c = _my_place()
        copies = []
        for a in range(n):
            cp = pltpu.make_async_remote_copy(
                src_ref=outs[a].at[c], dst_ref=outs[a].at[c], send_sem=send_sems.at[a], recv_sem=recv_sems.at[a],
                device_id=(x, y, 1 - c), device_id_type=MESH)
            cp.start()
            copies.append(cp)
        for cp in copies:
            cp.wait()

    return pl.pallas_call(
        body, name="grad_share_sibling",
        out_shape=[jax.ShapeDtypeStruct(h.shape, h.dtype) for h in halves],
        in_specs=[HBM_SPEC] * n, out_specs=[HBM_SPEC] * n,
        input_output_aliases={a: a for a in range(n)},
        scratch_shapes=[pltpu.SemaphoreType.DMA((n,)), pltpu.SemaphoreType.DMA((n,))],
    )(*halves)


def _load_rows(pairs, sems):
    cps = [pltpu.make_async_copy(src, dst, sems.at[j]) for j, (src, dst) in enumerate(pairs)]
    for cp in cps:
        cp.start()
    for cp in cps:
        cp.wait()


def _piece_rows(weights):
    return list(weights), (lambda refs, mats: list(zip(refs, mats))), len(weights)


def _cast_to_bf16(arrays, to_transpose, name, cargo=()):
    rows, cols = arrays[0].shape
    n = len(arrays)
    br = _row_block(rows, 256)
    steps = rows // br
    rows_t, cols_t = to_transpose.shape
    bt = rows_t // steps
    assert rows_t % steps == 0 and bt % 128 == 0

    def body(*refs):
        for src, dst in zip(refs[:n], refs[n + 1:2 * n + 1]):
            dst[...] = src[...].astype(BF16)
        refs[2 * n + 1][...] = refs[n][...].T.astype(BF16)

    blk = pl.BlockSpec((br, cols), lambda i: (i, 0))
    return _launch(body, name=name, grid=(steps,),
                   in_specs=[blk] * n + [pl.BlockSpec((bt, cols_t), lambda i: (i, 0))],
                   out_specs=[blk] * n + [pl.BlockSpec((cols_t, bt), lambda i: (0, i))],
                   out_shape=[jax.ShapeDtypeStruct((rows, cols), BF16)] * n + [jax.ShapeDtypeStruct((cols_t, rows_t), BF16)],
                   scratch_shapes=[], args=(*arrays, to_transpose), cargo=cargo)


def _loss_head(xv, gv, tv):
    d = xv.shape[-1]
    r = lax.rsqrt(jnp.mean(xv * xv, axis=-1, keepdims=True) + EPS)
    xhat = xv * r
    err = xhat * gv - tv
    dy = err * (1.0 / d)
    dxh = dy * gv
    dx = r * (dxh - xhat * jnp.mean(dxh * xhat, axis=-1, keepdims=True))
    return dx, jnp.sum(err * err, axis=0, keepdims=True), jnp.sum(dy * xhat, axis=0, keepdims=True)


def _ffn_up(x, g, wg_t, wu_t, name, cargo=()):
    t, d = x.shape
    f = wg_t.shape[0]
    tm = min(TM_FFN, t)
    chunks = _feature_chunks(f, FFN_FWD_CHUNKS)
    flat, copies, n_copies = _piece_rows([wg_t, wu_t])
    nw = len(flat)

    def body(x_ref, g_ref, *rest):
        w_hbm, (a_ref, b_ref, s_ref, wg, wu, sems) = rest[:nw], rest[nw:]

        @pl.when(pl.program_id(0) == 0)
        def _():
            _load_rows(copies(w_hbm, [wg, wu]), sems)

        xv = x_ref[...]
        r = lax.rsqrt(jnp.mean(xv * xv, axis=-1, keepdims=True) + EPS)
        h = (xv * r * g_ref[...]).astype(BF16)
        for s0, sz in chunks:
            a = _nt(h, wg[s0:s0 + sz, :])
            b = _nt(h, wu[s0:s0 + sz, :])
            a_ref[:, s0:s0 + sz] = a.astype(BF16)
            b_ref[:, s0:s0 + sz] = b.astype(BF16)
            s_ref[:, s0:s0 + sz] = (a * _sigmoid(a) * b).astype(BF16)

    tok = lambda i: (i, 0)
    wide = pl.BlockSpec((tm, f), tok)
    return _launch(
        body, name=name, grid=(t // tm,),
        in_specs=[pl.BlockSpec((tm, d), tok), pl.BlockSpec((1, d), lambda i: (0, 0))] + [HBM_SPEC] * nw,
        out_specs=[wide, wide, wide], out_shape=[jax.ShapeDtypeStruct((t, f), BF16)] * 3,
        scratch_shapes=[pltpu.VMEM((f, d), BF16), pltpu.VMEM((f, d), BF16), pltpu.SemaphoreType.DMA((n_copies,))],
        args=(x, g, *flat), cargo=cargo)


def _ffn_down(x, s, wd, name, cargo=(), loss_head=None):
    t, d = x.shape
    f = s.shape[1]
    tm = min(TM_FFN, t)
    flat, copies, n_copies = _piece_rows([wd])
    nw = len(flat)
    nl = 2 if loss_head else 0

    def body(x_ref, s_ref, *rest):
        head, w_hbm = rest[:nl], rest[nl:nl + nw]
        xo_ref = rest[nl + nw]
        sums, (wdn, sems) = rest[nl + nw + 1:nl + nw + 1 + nl], rest[nl + nw + 1 + nl:]

        @pl.when(pl.program_id(0) == 0)
        def _():
            _load_rows(copies(w_hbm, [wdn]), sems)
            for sum_ref in sums:
                sum_ref[...] = jnp.zeros_like(sum_ref)

        xo = x_ref[...] + 0.5 * _nn(s_ref[...], wdn[...])
        if loss_head:
            dx, sq, dgf = _loss_head(xo, head[0][...], head[1][...])
            xo_ref[...] = dx
            sums[0][...] += sq
            sums[1][...] += dgf
        else:
            xo_ref[...] = xo

    tok = lambda i: (i, 0)
    one = lambda i: (0, 0)
    return _launch(
        body, name=name, grid=(t // tm,),
        in_specs=[pl.BlockSpec((tm, d), tok), pl.BlockSpec((tm, f), tok)]
        + ([pl.BlockSpec((1, d), one), pl.BlockSpec((tm, d), tok)] if loss_head else []) + [HBM_SPEC] * nw,
        out_specs=[pl.BlockSpec((tm, d), tok)] + [pl.BlockSpec((1, d), one)] * nl,
        out_shape=[jax.ShapeDtypeStruct((t, d), F32)] + [jax.ShapeDtypeStruct((1, d), F32)] * nl,
        scratch_shapes=[pltpu.VMEM((f, d), BF16), pltpu.SemaphoreType.DMA((n_copies,))],
        args=(x, s, *(loss_head or ()), *flat), cargo=cargo)


def _ffn_backward(dxo, x, g, a, b, wg_t, wu_t, wd, name, cargo=()):
    t, d = x.shape
    f = wd.shape[0]
    tm = min(TM_FFN // 2, t)
    chunks = _feature_chunks(f, FFN_BWD_CHUNKS)
    flat, copies, n_copies = _piece_rows([wg_t, wu_t, wd])
    nw = len(flat)

    def body(dxo_ref, x_ref, g_ref, a_ref, b_ref, *rest):
        w_hbm, (dx_ref, da_ref, db_ref, h_ref, do_ref, dg_ref, wg, wu, wdn, sems) = rest[:nw], rest[nw:]

        @pl.when(pl.program_id(0) == 0)
        def _():
            _load_rows(copies(w_hbm, [wg, wu, wdn]), sems)
            dg_ref[...] = jnp.zeros_like(dg_ref)

        xv = x_ref[...]
        gv = g_ref[...]
        r = lax.rsqrt(jnp.mean(xv * xv, axis=-1, keepdims=True) + EPS)
        xhat = xv * r
        h_ref[...] = (xhat * gv).astype(BF16)
        dxo_v = dxo_ref[...]
        dout = (0.5 * dxo_v).astype(BF16)
        do_ref[...] = dout
        dh = jnp.zeros((tm, d), F32)
        for s0, sz in chunks:
            ds = _nt(dout, wdn[s0:s0 + sz, :])
            av = a_ref[:, s0:s0 + sz].astype(F32)
            bv = b_ref[:, s0:s0 + sz].astype(F32)
            sig = _sigmoid(av)
            silu = av * sig
            da = (ds * bv * (sig * (1.0 + av * (1.0 - sig)))).astype(BF16)
            db = (ds * silu).astype(BF16)
            da_ref[:, s0:s0 + sz] = da
            db_ref[:, s0:s0 + sz] = db
            dh = dh + _nn(da, wg[s0:s0 + sz, :]) + _nn(db, wu[s0:s0 + sz, :])
        dg_ref[...] += jnp.sum(dh * xhat, axis=0, keepdims=True)
        dxh = dh * gv
        dx_ref[...] = dxo_v + r * (dxh - xhat * jnp.mean(dxh * xhat, axis=-1, keepdims=True))

    tok = lambda i: (i, 0)
    one = lambda i: (0, 0)
    return _launch(
        body, name=name, grid=(t // tm,),
        in_specs=[pl.BlockSpec((tm, d), tok), pl.BlockSpec((tm, d), tok), pl.BlockSpec((1, d), one),
                  pl.BlockSpec((tm, f), tok), pl.BlockSpec((tm, f), tok)] + [HBM_SPEC] * nw,
        out_specs=[pl.BlockSpec((tm, d), tok), pl.BlockSpec((tm, f), tok), pl.BlockSpec((tm, f), tok),
                   pl.BlockSpec((tm, d), tok), pl.BlockSpec((tm, d), tok), pl.BlockSpec((1, d), one)],
        out_shape=[jax.ShapeDtypeStruct((t, d), F32), jax.ShapeDtypeStruct((t, f), BF16), jax.ShapeDtypeStruct((t, f), BF16),
                   jax.ShapeDtypeStruct((t, d), BF16), jax.ShapeDtypeStruct((t, d), BF16), jax.ShapeDtypeStruct((1, d), F32)],
        scratch_shapes=[pltpu.VMEM((f, d), BF16), pltpu.VMEM((f, d), BF16), pltpu.VMEM((f, d), BF16), pltpu.SemaphoreType.DMA((n_copies,))],
        args=(dxo, x, g, a, b, *flat), cargo=cargo)


def _weight_grad(lhs, rhs, name, cargo=()):
    t, m = lhs.shape
    d = rhs.shape[1]
    tm = min(TM_TN, t)
    nt = t // tm
    nj = 1
    bm = m // nj
    cpb = N_CHIPS // nj
    rps = m // N_CHIPS
    hr = rps // 2
    assert hr % 16 == 0

    def body(l_ref, r_ref, o_ref, acc, stage, recv, send_sems, recv_sems):
        j = pl.program_id(0)
        i = pl.program_id(1)
        @pl.when(i == 0)
        def _():
            acc[...] = jnp.zeros_like(acc)

        acc[...] += _tn(l_ref[...], r_ref[...])

        def pair_sum(jj):
            x, y, c = _my_place()
            copies = []
            for q in range(cpb):
                slot = jj * cpb + q
                stage[slot] = acc[pl.ds(pl.multiple_of(q * rps + (1 - c) * hr, 16), hr), :].astype(BF16)
                cp = pltpu.make_async_remote_copy(
                    src_ref=stage.at[slot], dst_ref=recv.at[slot], send_sem=send_sems.at[slot], recv_sem=recv_sems.at[slot],
                    device_id=(x, y, 1 - c), device_id_type=MESH)
                cp.start()
                copies.append(cp)
            for q, cp in enumerate(copies):
                cp.wait_recv()
                mine = acc[pl.ds(pl.multiple_of(q * rps + c * hr, 16), hr), :]
                o_ref[q] = (mine + recv[jj * cpb + q].astype(F32)).astype(BF16)
            for cp in copies:
                cp.wait_send()

        for jj in range(nj):
            @pl.when(jnp.logical_and(i == nt - 1, j == jj))
            def _():
                pair_sum(jj)

    outs, carried = _launch(
        body, name=name, grid=(nj, nt),
        in_specs=[pl.BlockSpec((tm, bm), lambda j, i: (i, j)), pl.BlockSpec((tm, d), lambda j, i: (i, 0))],
        out_specs=[pl.BlockSpec((cpb, hr, d), lambda j, i: (j, 0, 0))],
        out_shape=[jax.ShapeDtypeStruct((N_CHIPS, hr, d), BF16)],
        scratch_shapes=[pltpu.VMEM((bm, d), F32), pltpu.VMEM((N_CHIPS, hr, d), BF16), pltpu.VMEM((N_CHIPS, hr, d), BF16),
                        pltpu.SemaphoreType.DMA((N_CHIPS,)), pltpu.SemaphoreType.DMA((N_CHIPS,))],
        args=(lhs, rhs), cargo=cargo)
    return outs[0], carried


def _window_sums(src, cols, w, tm, levels, trailing):
    def read_src(lo, hi):
        return src[lo:hi, cols]

    read, k, level = read_src, 1, 0
    while True:
        last = 2 * k == w
        if trailing:
            lo, hi = (HALO if last else 8 * (level + 1)), HALO + tm
            cur = read(lo, hi) + read(lo - k, hi - k)
        else:
            lo, hi = 0, (tm if last else tm + HALO - 8 * (level + 1))
            cur = read(lo, hi) + read(lo + k, hi + k)
        if last:
            return cur
        levels[level, lo:hi, :] = cur
        read = lambda a, b, level=level: levels[level, a:b, :]
        k, level = 2 * k, level + 1


def _pool_parts(u_cols, ubuf, cols, w, row, tm, levels):
    ws = _window_sums(ubuf, cols, w, tm, levels, trailing=True)
    inv = 1.0 / jnp.minimum(row + 1, w).astype(F32)
    return ws * inv - u_cols, inv


def _mixer_forward(x, g, win_t, wout_x, conv_w, pool_w, pool_scale, cargo=()):
    t, d = x.shape
    dc = win_t.shape[0] // 4
    gcw = dc // len(POOL_WINDOWS)
    wo_rows = d // N_CHIPS
    wo_stride = wout_x.shape[0] // N_CHIPS
    tm = min(TM_MIX, t)

    def body(x_ref, g_ref, win_hbm, wout_hbm, cw_ref, pw_ref, ps_ref, xo_ref, proj_ref, y_ref,
             win, wout, zbuf, ubuf, levels, sems):
        i = pl.program_id(0)

        @pl.when(i == 0)
        def _():
            pairs = [(win_hbm, win)]
            for k in range(N_CHIPS):
                pairs.append((wout_hbm.at[pl.ds(k * wo_stride, wo_rows), :], wout.at[pl.ds(k * wo_rows, wo_rows), :]))
            _load_rows(pairs, sems)
            zbuf[0:8, :] = jnp.zeros((8, dc), F32)
            ubuf[0:HALO, :] = jnp.zeros((HALO, dc), F32)

        xv = x_ref[...]
        r = lax.rsqrt(jnp.mean(xv * xv, axis=-1, keepdims=True) + EPS)
        h = (xv * r * g_ref[...]).astype(BF16)
        v = _nt(h, win[0:dc, :])
        gb = _nt(h, win[dc:2 * dc, :])
        gc = _nt(h, win[2 * dc:3 * dc, :])
        u = _nt(h, win[3 * dc:4 * dc, :])
        proj_ref[:, 0:dc] = v.astype(BF16)
        proj_ref[:, dc:2 * dc] = gb.astype(BF16)
        proj_ref[:, 2 * dc:3 * dc] = gc.astype(BF16)
        proj_ref[:, 3 * dc:4 * dc] = u.astype(BF16)

        z = gc * v
        zbuf[8:8 + tm, :] = z
        cw = cw_ref[...]
        conv = cw[2:3, :] * z + cw[1:2, :] * zbuf[7:7 + tm, :] + cw[0:1, :] * zbuf[6:6 + tm, :]
        y_ref[:, 0:dc] = (gb * conv).astype(BF16)

        ubuf[HALO:HALO + tm, :] = u
        row = i * tm + lax.broadcasted_iota(jnp.int32, (tm, 1), 0)
        for gi, w in enumerate(POOL_WINDOWS):
            cols = slice(gi * gcw, (gi + 1) * gcw)
            pooled, _ = _pool_parts(u[:, cols], ubuf, cols, w, row, tm, levels)
            yb = _nn(pooled.astype(BF16), pw_ref[gi].astype(BF16)) * ps_ref[:, cols]
            y_ref[:, dc + gi * gcw:dc + (gi + 1) * gcw] = yb.astype(BF16)

        xo_ref[...] = xv + _nn(y_ref[...], wout[...])
        zbuf[0:8, :] = zbuf[tm:tm + 8, :]
        ubuf[0:HALO, :] = ubuf[tm:tm + HALO, :]

    tok = lambda i: (i, 0)
    one = lambda i: (0, 0)
    return _launch(
        body, name="mixer_forward", grid=(t // tm,),
        in_specs=[pl.BlockSpec((tm, d), tok), pl.BlockSpec((1, d), one), HBM_SPEC, HBM_SPEC,
                  pl.BlockSpec(conv_w.shape, one), pl.BlockSpec(pool_w.shape, lambda i: (0, 0, 0)), pl.BlockSpec((1, dc), one)],
        out_specs=[pl.BlockSpec((tm, d), tok), pl.BlockSpec((tm, 4 * dc), tok), pl.BlockSpec((tm, 2 * dc), tok)],
        out_shape=[jax.ShapeDtypeStruct((t, d), F32), jax.ShapeDtypeStruct((t, 4 * dc), BF16), jax.ShapeDtypeStruct((t, 2 * dc), BF16)],
        scratch_shapes=[pltpu.VMEM((4 * dc, d), BF16), pltpu.VMEM((2 * dc, d), BF16),
                        pltpu.VMEM((tm + 8, dc), F32), pltpu.VMEM((tm + HALO, dc), F32),
                        pltpu.VMEM((WINDOW_LEVELS, tm + HALO, gcw), F32), pltpu.SemaphoreType.DMA((1 + N_CHIPS,))],
        args=(x, g, win_t, wout_x, conv_w, pool_w, pool_scale), cargo=cargo)


def _mixer_backward(dxo, x, g, proj, win_t, wout_x, conv_w, pool_w, pool_scale, cargo=()):
    t, d = x.shape
    dc = win_t.shape[0] // 4
    ng = len(POOL_WINDOWS)
    gcw = dc // ng
    wo_rows = d // N_CHIPS
    wo_stride = wout_x.shape[0] // N_CHIPS
    tm = min(TM_MIX, t)
    n_tiles = t // tm
    hb = tm // HALO

    def body(dxo_ref, x_ref, g_ref, proj_ref, halo_ref, win_hbm, wout_hbm, cw_ref, pw_ref, ps_ref,
             dx_ref, dproj_ref, h_ref, dxob_ref, dg_ref, dcw_ref, dps_ref, dpw_ref,
             win, wout, zbuf, ubuf, dcbuf, ebuf, levels, sems):
        i = pl.program_id(0)
        tile = n_tiles - 1 - i

        @pl.when(i == 0)
        def _():
            pairs = [(win_hbm, win)]
            for k in range(N_CHIPS):
                pairs.append((wout_hbm.at[pl.ds(k * wo_stride, wo_rows), :], wout.at[pl.ds(k * wo_rows, wo_rows), :]))
            _load_rows(pairs, sems)
            dcbuf[tm:tm + 8, :] = jnp.zeros((8, dc), F32)
            ebuf[tm:tm + HALO, :] = jnp.zeros((HALO, dc), F32)
            dg_ref[...] = jnp.zeros_like(dg_ref)
            dcw_ref[...] = jnp.zeros_like(dcw_ref)
            dps_ref[...] = jnp.zeros_like(dps_ref)
            dpw_ref[...] = jnp.zeros_like(dpw_ref)

        xv = x_ref[...]
        gv = g_ref[...]
        r = lax.rsqrt(jnp.mean(xv * xv, axis=-1, keepdims=True) + EPS)
        xhat = xv * r
        h_ref[...] = (xhat * gv).astype(BF16)
        dxo_v = dxo_ref[...]
        dxo_b = dxo_v.astype(BF16)
        dxob_ref[...] = dxo_b

        v = proj_ref[:, 0:dc].astype(F32)
        gb = proj_ref[:, dc:2 * dc].astype(F32)
        gc = proj_ref[:, 2 * dc:3 * dc].astype(F32)
        u = proj_ref[:, 3 * dc:4 * dc].astype(F32)
        first = jnp.where(tile > 0, 1.0, 0.0)
        zbuf[0:HALO, :] = halo_ref[:, 2 * dc:3 * dc].astype(F32) * halo_ref[:, 0:dc].astype(F32) * first
        ubuf[0:HALO, :] = halo_ref[:, 3 * dc:4 * dc].astype(F32) * first
        z = gc * v
        zbuf[HALO:HALO + tm, :] = z
        ubuf[HALO:HALO + tm, :] = u
        z1 = zbuf[HALO - 1:HALO - 1 + tm, :]
        z2 = zbuf[HALO - 2:HALO - 2 + tm, :]
        cw = cw_ref[...]
        conv = cw[2:3, :] * z + cw[1:2, :] * z1 + cw[0:1, :] * z2

        dy = _nt(dxo_b, wout[...])
        dya = dy[:, 0:dc]
        dgb = dya * conv
        dconv = dya * gb
        dcbuf[0:tm, :] = dconv
        dz = cw[2:3, :] * dconv + cw[1:2, :] * dcbuf[1:1 + tm, :] + cw[0:1, :] * dcbuf[2:2 + tm, :]
        dgc = dz * v
        dv = dz * gc
        dcw_ref[0:1, :] += jnp.sum(dconv * z2, axis=0, keepdims=True)
        dcw_ref[1:2, :] += jnp.sum(dconv * z1, axis=0, keepdims=True)
        dcw_ref[2:3, :] += jnp.sum(dconv * z, axis=0, keepdims=True)

        dproj_ref[:, 0:dc] = dv.astype(BF16)
        dproj_ref[:, dc:2 * dc] = dgb.astype(BF16)
        dproj_ref[:, 2 * dc:3 * dc] = dgc.astype(BF16)

        row = tile * tm + lax.broadcasted_iota(jnp.int32, (tm, 1), 0)
        for gi, w in enumerate(POOL_WINDOWS):
            cols = slice(gi * gcw, (gi + 1) * gcw)
            pooled, inv_cnt = _pool_parts(u[:, cols], ubuf, cols, w, row, tm, levels)
            pooled_b = pooled.astype(BF16)
            pw_b = pw_ref[gi].astype(BF16)
            dyb = dy[:, dc + gi * gcw:dc + (gi + 1) * gcw]
            q = _nn(pooled_b, pw_b)
            dps_ref[:, cols] += jnp.sum(q * dyb, axis=0, keepdims=True)
            dq = (dyb * ps_ref[:, cols]).astype(BF16)
            dpw_ref[gi] += _tn(pooled_b, dq)
            dpooled = _nt(dq, pw_b)
            ebuf[0:tm, cols] = dpooled * inv_cnt
            du = _window_sums(ebuf, cols, w, tm, levels, trailing=False) - dpooled
            dproj_ref[:, 3 * dc + gi * gcw:3 * dc + (gi + 1) * gcw] = du.astype(BF16)

        dh = _nn(dproj_ref[...], win[...])
        dg_ref[...] += jnp.sum(dh * xhat, axis=0, keepdims=True)
        dxh = dh * gv
        dx_ref[...] = dxo_v + r * (dxh - xhat * jnp.mean(dxh * xhat, axis=-1, keepdims=True))
        dcbuf[tm:tm + 8, :] = dcbuf[0:8, :]
        ebuf[tm:tm + HALO, :] = ebuf[0:HALO, :]

    tok = lambda i: (n_tiles - 1 - i, 0)
    halo = lambda i: (jnp.maximum((n_tiles - 1 - i) * hb - 1, 0), 0)
    one = lambda i: (0, 0)
    return _launch(
        body, name="mixer_backward", grid=(n_tiles,),
        in_specs=[pl.BlockSpec((tm, d), tok), pl.BlockSpec((tm, d), tok), pl.BlockSpec((1, d), one),
                  pl.BlockSpec((tm, 4 * dc), tok), pl.BlockSpec((HALO, 4 * dc), halo), HBM_SPEC, HBM_SPEC,
                  pl.BlockSpec(conv_w.shape, one), pl.BlockSpec(pool_w.shape, lambda i: (0, 0, 0)), pl.BlockSpec((1, dc), one)],
        out_specs=[pl.BlockSpec((tm, d), tok), pl.BlockSpec((tm, 4 * dc), tok), pl.BlockSpec((tm, d), tok), pl.BlockSpec((tm, d), tok),
                   pl.BlockSpec((1, d), one), pl.BlockSpec(conv_w.shape, one), pl.BlockSpec((1, dc), one),
                   pl.BlockSpec(pool_w.shape, lambda i: (0, 0, 0))],
        out_shape=[jax.ShapeDtypeStruct((t, d), F32), jax.ShapeDtypeStruct((t, 4 * dc), BF16), jax.ShapeDtypeStruct((t, d), BF16),
                   jax.ShapeDtypeStruct((t, d), BF16), jax.ShapeDtypeStruct((1, d), F32), jax.ShapeDtypeStruct(conv_w.shape, F32),
                   jax.ShapeDtypeStruct((1, dc), F32), jax.ShapeDtypeStruct(pool_w.shape, F32)],
        scratch_shapes=[pltpu.VMEM((4 * dc, d), BF16), pltpu.VMEM((2 * dc, d), BF16),
                        pltpu.VMEM((tm + HALO, dc), F32), pltpu.VMEM((tm + HALO, dc), F32),
                        pltpu.VMEM((tm + 8, dc), F32), pltpu.VMEM((tm + HALO, dc), F32),
                        pltpu.VMEM((WINDOW_LEVELS, tm + HALO, gcw), F32), pltpu.SemaphoreType.DMA((1 + N_CHIPS,))],
        args=(dxo, x, g, proj, proj, win_t, wout_x, conv_w, pool_w, pool_scale), cargo=cargo)


def _adam_update(w, gv, m, v):
    m_new = ADAM_B1 * m + (1.0 - ADAM_B1) * gv
    v_new = ADAM_B2 * v + (1.0 - ADAM_B2) * (gv * gv)
    m_hat = m_new / (1.0 - ADAM_B1 ** ADAM_STEP)
    v_hat = v_new / (1.0 - ADAM_B2 ** ADAM_STEP)
    return -ADAM_LR * (m_hat / (jnp.sqrt(v_hat) + ADAM_EPS) + ADAM_WD * w), m_new, v_new


def _adamw(w, grad, m, v, name):
    rows, cols = w.shape
    br = _row_block(rows, 256) if rows >= 8 else rows

    def body(w_ref, g_ref, m_ref, v_ref, go_ref, d_ref, mo_ref, vo_ref):
        gv = g_ref[...]
        go_ref[...] = gv
        d_ref[...], mo_ref[...], vo_ref[...] = _adam_update(w_ref[...], gv, m_ref[...], v_ref[...])

    blk = pl.BlockSpec((br, cols), lambda i: (i, 0))
    return pl.pallas_call(
        body, name=name,
        out_shape=[jax.ShapeDtypeStruct((rows, cols), F32)] * 4,
        grid=(rows // br,), in_specs=[blk] * 4, out_specs=[blk] * 4,
        compiler_params=pltpu.CompilerParams(dimension_semantics=("parallel",)),
    )(w, grad, m, v)


def _adamw_transposed(w, grad_t, m, v, name):
    _, rows, cols = w.shape
    br = 256 if rows % 256 == 0 else rows

    def body(w_ref, gt_ref, m_ref, v_ref, g_ref, d_ref, mo_ref, vo_ref):
        gv = gt_ref[...].T
        g_ref[...] = gv
        d_ref[...], mo_ref[...], vo_ref[...] = _adam_update(w_ref[...], gv, m_ref[...], v_ref[...])

    blk = pl.BlockSpec((None, br, cols), lambda i: (0, i, 0))
    return pl.pallas_call(
        body, name=name,
        out_shape=[jax.ShapeDtypeStruct((1, rows, cols), F32)] * 4,
        grid=(rows // br,), in_specs=[blk, pl.BlockSpec((cols, br), lambda i: (0, i)), blk, blk], out_specs=[blk] * 4,
        compiler_params=pltpu.CompilerParams(dimension_semantics=("parallel",)),
    )(w, grad_t, m, v)


def _f32_rows_as_bf16(a, rows, cols):
    bits = lax.bitcast_convert_type(a, BF16).reshape(a.shape[0], 2 * a.shape[1])
    return jnp.pad(bits, ((0, rows - bits.shape[0]), (0, cols - bits.shape[1])))


def kernel(x, norm_ffn1, ffn1_w_gate, ffn1_w_up, ffn1_w_down, norm_mix, w_in, conv_w, pool_w, pool_scale, w_out, norm_ffn2, ffn2_w_gate, ffn2_w_up, ffn2_w_down, norm_final, loss_target, m_norm_ffn1, m_ffn1_w_gate, m_ffn1_w_up, m_ffn1_w_down, m_norm_mix, m_w_in, m_conv_w, m_pool_w, m_pool_scale, m_w_out, m_norm_ffn2, m_ffn2_w_gate, m_ffn2_w_up, m_ffn2_w_down, m_norm_final, v_norm_ffn1, v_ffn1_w_gate, v_ffn1_w_up, v_ffn1_w_down, v_norm_mix, v_w_in, v_conv_w, v_pool_w, v_pool_scale, v_w_out, v_norm_ffn2, v_ffn2_w_gate, v_ffn2_w_up, v_ffn2_w_down, v_norm_final):
    weights = dict(norm_ffn1=norm_ffn1, ffn1_w_gate=ffn1_w_gate, ffn1_w_up=ffn1_w_up, ffn1_w_down=ffn1_w_down, norm_mix=norm_mix,
                   w_in=w_in, conv_w=conv_w, pool_w=pool_w, pool_scale=pool_scale, w_out=w_out, norm_ffn2=norm_ffn2,
                   ffn2_w_gate=ffn2_w_gate, ffn2_w_up=ffn2_w_up, ffn2_w_down=ffn2_w_down, norm_final=norm_final)
    first_m = dict(norm_ffn1=m_norm_ffn1, ffn1_w_gate=m_ffn1_w_gate, ffn1_w_up=m_ffn1_w_up, ffn1_w_down=m_ffn1_w_down,
                   norm_mix=m_norm_mix, w_in=m_w_in, conv_w=m_conv_w, pool_w=m_pool_w, pool_scale=m_pool_scale, w_out=m_w_out,
                   norm_ffn2=m_norm_ffn2, ffn2_w_gate=m_ffn2_w_gate, ffn2_w_up=m_ffn2_w_up, ffn2_w_down=m_ffn2_w_down,
                   norm_final=m_norm_final)
    second_m = dict(norm_ffn1=v_norm_ffn1, ffn1_w_gate=v_ffn1_w_gate, ffn1_w_up=v_ffn1_w_up, ffn1_w_down=v_ffn1_w_down,
                    norm_mix=v_norm_mix, w_in=v_w_in, conv_w=v_conv_w, pool_w=v_pool_w, pool_scale=v_pool_scale, w_out=v_w_out,
                    norm_ffn2=v_norm_ffn2, ffn2_w_gate=v_ffn2_w_gate, ffn2_w_up=v_ffn2_w_up, ffn2_w_down=v_ffn2_w_down,
                    norm_final=v_norm_final)
    names = list(weights)

    xs = x[0]
    tgt = loss_target[0]
    t, d = xs.shape
    dc = pool_scale.shape[1]
    cx, cy, cc = _my_place()
    chip = 2 * cx + cy
    place = jnp.stack([chip, cc]).astype(jnp.int32)

    conv_rows = 32
    wout_x = jnp.concatenate([w_out[0].astype(BF16), _f32_rows_as_bf16(conv_w[0], conv_rows, d)], axis=0)

    g1, gm, g2 = norm_ffn1, norm_mix, norm_ffn2
    gf = norm_final.reshape(1, d)
    pw = pool_w[0]

    (wd1_shard, wg2_shard, wu2_shard, wd2_shard, win_shard), [(wg1, wu1)] = _cast_to_bf16(
        [ffn1_w_down[0], ffn2_w_gate[0].T, ffn2_w_up[0].T, ffn2_w_down[0]], w_in[0], "gather_ffn1",
        [_gather_cargo([ffn1_w_gate[0].T.astype(BF16), ffn1_w_up[0].T.astype(BF16)])])
    (a1, b1, s1), [(wd1, win_t, wout_g)] = _ffn_up(xs, g1, wg1, wu1, "ffn1_up", [_gather_cargo([wd1_shard, win_shard, wout_x])])
    (x1,), [(wg2,)] = _ffn_down(xs, s1, wd1, "ffn1_down", [_gather_cargo([wg2_shard])])
    wo_rows = w_out.shape[1]
    cshard = conv_w.shape[2]
    conv_bits = wout_g.reshape(N_CHIPS, wo_rows + conv_rows, d)[:, wo_rows:wo_rows + conv_w.shape[1], :2 * cshard]
    conv_full = lax.bitcast_convert_type(conv_bits.reshape(N_CHIPS, conv_w.shape[1], cshard, 2), F32)
    conv_full = jnp.transpose(conv_full, (1, 0, 2)).reshape(conv_w.shape[1], N_CHIPS * cshard)
    (x2, proj, ymix), [(wu2,)] = _mixer_forward(x1, gm, win_t, wout_g, conv_full, pw, pool_scale, [_gather_cargo([wu2_shard])])
    (a2, b2, s2), [(wd2,)] = _ffn_up(x2, g2, wg2, wu2, "ffn2_up", [_gather_cargo([wd2_shard])])
    (dx3, sq_cols, dgf), _ = _ffn_down(x2, s2, wd2, "ffn2_down", loss_head=(gf, tgt))

    (dx2, da2, db2, h3, do2, dg2), _ = _ffn_backward(dx3, x2, g2, a2, b2, wg2, wu2, wd2, "ffn2_backward")
    p_wg2, _ = _weight_grad(da2, h3, "ffn2_gate_grad")
    p_wu2, [(x_wg2,)] = _weight_grad(db2, h3, "ffn2_up_grad", [_exchange_cargo([p_wg2])])
    p_wd2, [(x_wu2,)] = _weight_grad(s2, do2, "ffn2_down_grad", [_exchange_cargo([p_wu2])])

    (dx1, dproj, h2, dx2b, dgm, dcw, dps, dpw), [(x_wd2,)] = _mixer_backward(
        dx2, x1, gm, proj, win_t, wout_g, conv_full, pw, pool_scale, [_exchange_cargo([p_wd2])])

    (dx0, da1, db1, h1, do1, dg1), _ = _ffn_backward(dx1, xs, g1, a1, b1, wg1, wu1, wd1, "ffn1_backward")

    npw = pw.size // d
    head = [dg1, dgm, dg2, dgf, jnp.pad(dps, ((0, 0), (0, d - dc))), jnp.pad(dcw, ((0, 0), (0, d - dc))), sq_cols]
    n_head = sum(h.shape[0] for h in head)
    base = -(-n_head // 8) * 8
    pack = jnp.concatenate(head + [jnp.zeros((base - n_head, d), F32), dpw.reshape(npw, d)], axis=0)

    p_wg1, [(packs,)] = _weight_grad(da1, h1, "ffn1_gate_grad", [_all_gather_small_cargo(pack)])
    p_wu1, [(x_wg1,)] = _weight_grad(db1, h1, "ffn1_up_grad", [_exchange_cargo([p_wg1])])
    p_wd1, [(x_wu1,)] = _weight_grad(s1, do1, "ffn1_down_grad", [_exchange_cargo([p_wu1])])
    p_win, [(x_wd1,)] = _weight_grad(dproj, h2, "w_in_grad", [_exchange_cargo([p_wd1])])
    p_wout, [(x_win,)] = _weight_grad(ymix, dx2b, "w_out_grad", [_exchange_cargo([p_win])])
    x_wout, = _run_cargo(_exchange_cargo([p_wout]), "grad_exchange_last")
    small = _sum_by_device(packs)
    loss = jnp.sum(small[n_head - 1]) * (0.5 / d)

    order = ["wg1", "wu1", "wd1", "win", "wout", "wg2", "wu2", "wd2"]
    pairs = dict(wg1=p_wg1, wu1=p_wu1, wd1=p_wd1, win=p_win, wout=p_wout, wg2=p_wg2, wu2=p_wu2, wd2=p_wd2)
    landed = dict(wg1=x_wg1, wu1=x_wu1, wd1=x_wd1, win=x_win, wout=x_wout, wg2=x_wg2, wu2=x_wu2, wd2=x_wd2)
    both = _sibling_share([_chip_sum(pairs[k], landed[k], place, k) for k in order])
    rwg1, rwu1, rwd1, rwin, rwout, rwg2, rwu2, rwd2 = [b.reshape(2 * b.shape[1], b.shape[2]) for b in both]

    grads = {
        "norm_ffn1": small[0:1], "norm_mix": small[1:2], "norm_ffn2": small[2:3], "norm_final": small[3],
        "pool_scale": small[4:5, :dc],
        "conv_w": lax.dynamic_slice_in_dim(small[5:5 + dcw.shape[0], :dc], chip * cshard, cshard, axis=1)[None],
        "pool_w": small[base:].reshape(pool_w.shape),
        "ffn1_w_down": rwd1[None], "w_out": rwout[None], "ffn2_w_down": rwd2[None],
    }
    by_view = {"ffn1_w_gate": rwg1, "ffn1_w_up": rwu1, "ffn2_w_gate": rwg2, "ffn2_w_up": rwu2}

    deltas, new_m, new_v = {}, {}, {}
    for n in names:
        w = weights[n]
        shape = w.shape
        if n == "w_in":
            grads[n], deltas[n], new_m[n], new_v[n] = _adamw_transposed(w, rwin, first_m[n], second_m[n], "adamw_" + n)
            continue
        if n in by_view:
            view = lambda a: jnp.swapaxes(a, 1, 2)[0]
            back = lambda a: jnp.swapaxes(a[None], 1, 2)
            outs = _adamw(view(w), by_view[n], view(first_m[n]), view(second_m[n]), "adamw_" + n)
            grads[n], deltas[n], new_m[n], new_v[n] = [back(o) for o in outs]
            continue
        as2d = (lambda a: a.reshape(-1, shape[-1]))
        outs = _adamw(as2d(w), as2d(grads[n]), as2d(first_m[n]), as2d(second_m[n]), "adamw_" + n)
        grads[n], deltas[n], new_m[n], new_v[n] = [o.reshape(shape) for o in outs]

    return (loss, dx0[None], *[grads[n] for n in names], *[deltas[n] for n in names],
            *[new_m[n] for n in names], *[new_v[n] for n in names])
```

```python
import jax
import jax.numpy as jnp
from jax import lax
from jax.experimental import pallas as pl
from jax.experimental.pallas import tpu as pltpu

F32 = jnp.float32
BF16 = jnp.bfloat16
MESH = pl.DeviceIdType.MESH

EPS = 1e-6
POOL_WINDOWS = (2, 4, 8, 16)
ADAM_LR = 0.001
ADAM_B1 = 0.9
ADAM_B2 = 0.999
ADAM_EPS = 1e-08
ADAM_WD = 0.01
ADAM_STEP = 10

N_CHIPS = 4
N_DEVICES = 8
MXU_COLS_V7X = 256
VMEM_LIMIT = 56 * 1024 * 1024
TM_FFN = 512
TM_MIX = 512
TM_TN = 1024
HALO = 32
WINDOW_LEVELS = 3
FFN_FWD_CHUNKS = 2
FFN_BWD_CHUNKS = 2


def _nt(a, b):
    return lax.dot_general(a, b, (((1,), (1,)), ((), ())), preferred_element_type=F32)


def _tn(a, b):
    return lax.dot_general(a, b, (((0,), (0,)), ((), ())), preferred_element_type=F32)


def _nn(a, b):
    return jnp.dot(a, b, preferred_element_type=F32)


def _sigmoid(a):
    return 1.0 / (1.0 + jnp.exp(-a))


def _feature_chunks(n, parts):
    assert n % MXU_COLS_V7X == 0
    tiles = n // MXU_COLS_V7X
    out, s0 = [], 0
    for p in range(parts):
        sz = (tiles // parts + (1 if p < tiles % parts else 0)) * MXU_COLS_V7X
        if sz:
            out.append((s0, sz))
            s0 += sz
    return out


def _row_block(rows, cap):
    best = 8
    for b in range(8, min(rows, cap) + 1, 8):
        if rows % b == 0:
            best = b
    assert rows % best == 0
    return best


def _my_place():
    return lax.axis_index("x"), lax.axis_index("y"), lax.axis_index("c")


def _other_chips(x, y):
    return [(1 - x, y), (x, 1 - y), (1 - x, 1 - y)]


HBM_SPEC = pl.BlockSpec(memory_space=pltpu.HBM)


class _Cargo:
    def __init__(self, operands, out_shapes, n_sems, phases, when):
        self.operands, self.out_shapes, self.n_sems = list(operands), list(out_shapes), n_sems
        self.phases, self.when = list(phases), list(when)
        assert len(self.phases) == len(self.when) and self.when[0] == 0.0 and self.when[-1] == 1.0


def _launch(body, *, name, grid, in_specs, out_specs, out_shape, scratch_shapes, args, cargo=()):
    params = pltpu.CompilerParams(dimension_semantics=("arbitrary",) * len(grid), vmem_limit_bytes=VMEM_LIMIT)
    cargos = list(cargo)
    c_operands = [op for cg in cargos for op in cg.operands]
    c_shapes = [sh for cg in cargos for sh in cg.out_shapes]
    counts = [len(in_specs), len(c_operands), len(out_shape), len(c_shapes), len(scratch_shapes), 2 * len(cargos)]

    def carrying(*refs):
        groups, pos = [], 0
        for k in counts:
            groups.append(refs[pos:pos + k])
            pos += k
        ins, c_ins, outs, c_outs, scratch, sems = groups
        parts, pi, po = [], 0, 0
        for n, cg in enumerate(cargos):
            parts.append((c_ins[pi:pi + len(cg.operands)], c_outs[po:po + len(cg.out_shapes)], sems[2 * n], sems[2 * n + 1]))
            pi += len(cg.operands)
            po += len(cg.out_shapes)
        step, steps = 0, 1
        for ax, g in enumerate(grid):
            step = step * g + pl.program_id(ax)
            steps *= g
        todo = {}
        for cg, part in zip(cargos, parts):
            for phase, frac in zip(cg.phases[:-1], cg.when[:-1]):
                todo.setdefault(int(round(frac * (steps - 1))), []).append((phase, part))

        for at in sorted(todo):
            @pl.when(step == at)
            def _(at=at):
                for phase, part in todo[at]:
                    phase(*part)

        body(*ins, *outs, *scratch)

        if cargos:
            @pl.when(step == steps - 1)
            def _():
                for cg, part in zip(cargos, parts):
                    cg.phases[-1](*part)

    sems = [pltpu.SemaphoreType.DMA((cg.n_sems,)) for cg in cargos for _ in range(2)]
    outs = pl.pallas_call(
        carrying, name=name, grid=grid,
        in_specs=list(in_specs) + [HBM_SPEC] * counts[1], out_specs=list(out_specs) + [HBM_SPEC] * counts[3],
        out_shape=list(out_shape) + c_shapes, scratch_shapes=list(scratch_shapes) + sems,
        compiler_params=params)(*args, *c_operands)
    own, rest = list(outs[:counts[2]]), list(outs[counts[2]:])
    carried, po = [], 0
    for cg in cargos:
        carried.append(rest[po:po + len(cg.out_shapes)])
        po += len(cg.out_shapes)
    return own, carried


def _run_cargo(cargo, name):
    n_in, n_out = len(cargo.operands), len(cargo.out_shapes)

    def body(*refs):
        c_ins, c_outs, sems = refs[:n_in], refs[n_in:n_in + n_out], refs[n_in + n_out:]
        for phase in cargo.phases:
            phase(c_ins, c_outs, *sems)

    sem = pltpu.SemaphoreType.DMA((cargo.n_sems,))
    return list(pl.pallas_call(body, name=name, out_shape=cargo.out_shapes, in_specs=[HBM_SPEC] * n_in,
                               out_specs=[HBM_SPEC] * n_out, scratch_shapes=[sem, sem])(*cargo.operands))


def _gather_cargo(shards):
    n = len(shards)
    for s in shards:
        assert s.shape[0] % 32 == 0
    slots = 8

    def steps(ins, outs, send_sems, recv_sems):
        x, y, c = _my_place()
        sibling = (x, y, 1 - c)
        over_x, over_y = (1 - x, y, c), (x, 1 - y, c)
        mine, chip_x, chip_y, chip_d = 2 * x + y, 2 * (1 - x) + y, 2 * x + (1 - y), 2 * (1 - x) + (1 - y)

        def rows_of(a, chip_index, half, part=None):
            rps = shards[a].shape[0]
            hr = rps // 2
            first = -(-hr // 32) * 16
            offset, size = {None: (0, hr), 0: (0, first), 1: (first, hr - first)}[part]
            return outs[a].at[pl.ds(pl.multiple_of(chip_index * rps + half * hr + offset, 16), size), :]

        def remote(a, slot, src, dst, to):
            return pltpu.make_async_remote_copy(
                src_ref=src, dst_ref=dst, send_sem=send_sems.at[a * slots + slot], recv_sem=recv_sems.at[a * slots + slot],
                device_id=to, device_id_type=MESH)

        def same_rows(a, slot, rows, to):
            return remote(a, slot, rows, rows, to)

        def own_copy(a):
            rps = shards[a].shape[0]
            return remote(a, 7, ins[a], outs[a].at[pl.ds(pl.multiple_of(mine * rps, 16), rps), :], sibling)

        def my_half(a):
            hr = shards[a].shape[0] // 2
            return ins[a].at[pl.ds(pl.multiple_of(c * hr, 16), hr), :]

        def start():
            for a in range(n):
                own_copy(a).start()
                remote(a, 0, my_half(a), rows_of(a, mine, c), over_x).start()
                remote(a, 1, my_half(a), rows_of(a, mine, c), over_y).start()

        def relay_neighbours():
            for a in range(n):
                same_rows(a, 0, rows_of(a, chip_x, c), over_x).wait_recv()
                same_rows(a, 4, rows_of(a, chip_x, c), sibling).start()
                same_rows(a, 2, rows_of(a, chip_x, c, 0), over_y).start()
                same_rows(a, 1, rows_of(a, chip_y, c), over_y).wait_recv()
                same_rows(a, 5, rows_of(a, chip_y, c), sibling).start()
                same_rows(a, 3, rows_of(a, chip_y, c, 1), over_x).start()

        def relay_diagonal():
            for a in range(n):
                same_rows(a, 2, rows_of(a, chip_d, c, 0), over_y).wait_recv()
                same_rows(a, 3, rows_of(a, chip_d, c, 1), over_x).wait_recv()
                same_rows(a, 6, rows_of(a, chip_d, c), sibling).start()

        def finish():
            for a in range(n):
                for slot, chip_index in ((4, chip_x), (5, chip_y), (6, chip_d)):
                    same_rows(a, slot, rows_of(a, chip_index, 1 - c), sibling).wait_recv()
            for a in range(n):
                remote(a, 0, my_half(a), rows_of(a, mine, c), over_x).wait_send()
                remote(a, 1, my_half(a), rows_of(a, mine, c), over_y).wait_send()
                same_rows(a, 2, rows_of(a, chip_x, c, 0), over_y).wait_send()
                same_rows(a, 3, rows_of(a, chip_y, c, 1), over_x).wait_send()
                for slot, chip_index in ((4, chip_x), (5, chip_y), (6, chip_d)):
                    same_rows(a, slot, rows_of(a, chip_index, c), sibling).wait_send()
                own_copy(a).wait()

        return start, relay_neighbours, relay_diagonal, finish

    phases = [lambda *r, k=k: steps(*r)[k]() for k in range(4)]
    return _Cargo(shards, [jax.ShapeDtypeStruct((N_CHIPS * s.shape[0], s.shape[1]), s.dtype) for s in shards], slots * n,
                  phases, [0.0, 0.6, 0.85, 1.0])


def _exchange_cargo(pairs):
    n = len(pairs)

    def copies(ins, outs, send_sems, recv_sems):
        x, y, c = _my_place()
        return [pltpu.make_async_remote_copy(
            src_ref=ins[a].at[2 * chip[0] + chip[1]], dst_ref=outs[a].at[j],
            send_sem=send_sems.at[3 * a + j], recv_sem=recv_sems.at[3 * a + j], device_id=(*chip, c), device_id_type=MESH)
            for a in range(n) for j, chip in enumerate(_other_chips(x, y))]

    def start(*r):
        for cp in copies(*r):
            cp.start()

    def finish(*r):
        for cp in copies(*r):
            cp.wait()

    return _Cargo(pairs, [jax.ShapeDtypeStruct((3,) + p.shape[1:], p.dtype) for p in pairs], 3 * n, [start, finish], [0.0, 1.0])


def _all_gather_small_cargo(pack):
    rows, cols = pack.shape

    def copies(ins, outs, send_sems, recv_sems):
        x, y, c = _my_place()
        me = 4 * x + 2 * y + c
        remote = []
        for f in range(1, N_DEVICES):
            fx, fy, fc = (f >> 2) & 1, (f >> 1) & 1, f & 1
            to = (1 - x if fx else x, 1 - y if fy else y, 1 - c if fc else c)
            remote.append(pltpu.make_async_remote_copy(
                src_ref=ins[0], dst_ref=outs[0].at[me], send_sem=send_sems.at[f - 1], recv_sem=recv_sems.at[f - 1],
                device_id=to, device_id_type=MESH))
        own = pltpu.make_async_copy(ins[0], outs[0].at[me], send_sems.at[N_DEVICES - 1])
        return remote, own

    def start(*r):
        remote, own = copies(*r)
        own.start()
        for cp in remote:
            cp.start()

    def finish(*r):
        remote, own = copies(*r)
        for cp in remote:
            cp.wait()
        own.wait()

    return _Cargo([pack], [jax.ShapeDtypeStruct((N_DEVICES, rows, cols), F32)], N_DEVICES, [start, finish], [0.0, 1.0])


def _sum_by_device(packs):
    n, rows, cols = packs.shape

    def body(p_ref, o_ref):
        acc = p_ref[0]
        for dev in range(1, n):
            acc = acc + p_ref[dev]
        o_ref[...] = acc

    return pl.pallas_call(body, name="small_grads_sum", out_shape=jax.ShapeDtypeStruct((rows, cols), F32))(packs)


def _chip_sum(pair, got, place, tag):
    _, hr, cols = pair.shape
    br = _row_block(hr, 256)

    def body(k_ref, p_ref, r_ref, o_ref):
        acc = p_ref[...].astype(F32)
        for j in range(3):
            acc = acc + r_ref[j].astype(F32)
        o_ref[...] = acc

    return pl.pallas_call(
        body, name="grad_chip_sum_" + tag,
        out_shape=jax.ShapeDtypeStruct((2, hr, cols), F32),
        grid_spec=pltpu.PrefetchScalarGridSpec(
            num_scalar_prefetch=1, grid=(hr // br,),
            in_specs=[pl.BlockSpec((None, br, cols), lambda r, k_ref: (k_ref[0], r, 0)),
                      pl.BlockSpec((3, br, cols), lambda r, k_ref: (0, r, 0))],
            out_specs=pl.BlockSpec((None, br, cols), lambda r, k_ref: (k_ref[1], r, 0))),
        compiler_params=pltpu.CompilerParams(dimension_semantics=("parallel",)),
    )(place, pair, got)


def _sibling_share(halves):
    n = len(halves)

    def body(*refs):
        outs = refs[n:2 * n]
        send_sems, recv_sems = refs[2 * n:]
        x, y, c = _my_place()
        copies = []
        for a in range(n):
            cp = pltpu.make_async_remote_copy(
                src_ref=outs[a].at[c], dst_ref=outs[a].at[c], send_sem=send_sems.at[a], recv_sem=recv_sems.at[a],
                device_id=(x, y, 1 - c), device_id_type=MESH)
            cp.start()
            copies.append(cp)
        for cp in copies:
            cp.wait()

    return pl.pallas_call(
        body, name="grad_share_sibling",
        out_shape=[jax.ShapeDtypeStruct(h.shape, h.dtype) for h in halves],
        in_specs=[HBM_SPEC] * n, out_specs=[HBM_SPEC] * n,
        input_output_aliases={a: a for a in range(n)},
        scratch_shapes=[pltpu.SemaphoreType.DMA((n,)), pltpu.SemaphoreType.DMA((n,))],
    )(*halves)


def _load_rows(pairs, sems):
    cps = [pltpu.make_async_copy(src, dst, sems.at[j]) for j, (src, dst) in enumerate(pairs)]
    for cp in cps:
        cp.start()
    for cp in cps:
        cp.wait()


def _piece_rows(weights):
    return list(weights), (lambda refs, mats: list(zip(refs, mats))), len(weights)


def _cast_to_bf16(arrays, name, cargo=()):
    rows, cols = arrays[0].shape
    n = len(arrays)
    br = _row_block(rows, 256)

    def body(*refs):
        for src, dst in zip(refs[:n], refs[n:]):
            dst[...] = src[...].astype(BF16)

    blk = pl.BlockSpec((br, cols), lambda i: (i, 0))
    return _launch(body, name=name, grid=(rows // br,), in_specs=[blk] * n, out_specs=[blk] * n,
                   out_shape=[jax.ShapeDtypeStruct((rows, cols), BF16)] * n, scratch_shapes=[], args=tuple(arrays), cargo=cargo)


def _loss_head(xv, gv, tv):
    d = xv.shape[-1]
    r = lax.rsqrt(jnp.mean(xv * xv, axis=-1, keepdims=True) + EPS)
    xhat = xv * r
    err = xhat * gv - tv
    dy = err * (1.0 / d)
    dxh = dy * gv
    dx = r * (dxh - xhat * jnp.mean(dxh * xhat, axis=-1, keepdims=True))
    return dx, jnp.sum(err * err, axis=0, keepdims=True), jnp.sum(dy * xhat, axis=0, keepdims=True)


def _ffn_up(x, g, wg_t, wu_t, name, cargo=()):
    t, d = x.shape
    f = wg_t.shape[0]
    tm = min(TM_FFN, t)
    chunks = _feature_chunks(f, FFN_FWD_CHUNKS)
    flat, copies, n_copies = _piece_rows([wg_t, wu_t])
    nw = len(flat)

    def body(x_ref, g_ref, *rest):
        w_hbm, (a_ref, b_ref, s_ref, wg, wu, sems) = rest[:nw], rest[nw:]

        @pl.when(pl.program_id(0) == 0)
        def _():
            _load_rows(copies(w_hbm, [wg, wu]), sems)

        xv = x_ref[...]
        r = lax.rsqrt(jnp.mean(xv * xv, axis=-1, keepdims=True) + EPS)
        h = (xv * r * g_ref[...]).astype(BF16)
        for s0, sz in chunks:
            a = _nt(h, wg[s0:s0 + sz, :])
            b = _nt(h, wu[s0:s0 + sz, :])
            a_ref[:, s0:s0 + sz] = a.astype(BF16)
            b_ref[:, s0:s0 + sz] = b.astype(BF16)
            s_ref[:, s0:s0 + sz] = (a * _sigmoid(a) * b).astype(BF16)

    tok = lambda i: (i, 0)
    wide = pl.BlockSpec((tm, f), tok)
    return _launch(
        body, name=name, grid=(t // tm,),
        in_specs=[pl.BlockSpec((tm, d), tok), pl.BlockSpec((1, d), lambda i: (0, 0))] + [HBM_SPEC] * nw,
        out_specs=[wide, wide, wide], out_shape=[jax.ShapeDtypeStruct((t, f), BF16)] * 3,
        scratch_shapes=[pltpu.VMEM((f, d), BF16), pltpu.VMEM((f, d), BF16), pltpu.SemaphoreType.DMA((n_copies,))],
        args=(x, g, *flat), cargo=cargo)


def _ffn_down(x, s, wd, name, cargo=(), loss_head=None):
    t, d = x.shape
    f = s.shape[1]
    tm = min(TM_FFN, t)
    flat, copies, n_copies = _piece_rows([wd])
    nw = len(flat)
    nl = 2 if loss_head else 0

    def body(x_ref, s_ref, *rest):
        head, w_hbm = rest[:nl], rest[nl:nl + nw]
        xo_ref = rest[nl + nw]
        sums, (wdn, sems) = rest[nl + nw + 1:nl + nw + 1 + nl], rest[nl + nw + 1 + nl:]

        @pl.when(pl.program_id(0) == 0)
        def _():
            _load_rows(copies(w_hbm, [wdn]), sems)
            for sum_ref in sums:
                sum_ref[...] = jnp.zeros_like(sum_ref)

        xo = x_ref[...] + 0.5 * _nn(s_ref[...], wdn[...])
        if loss_head:
            dx, sq, dgf = _loss_head(xo, head[0][...], head[1][...])
            xo_ref[...] = dx
            sums[0][...] += sq
            sums[1][...] += dgf
        else:
            xo_ref[...] = xo

    tok = lambda i: (i, 0)
    one = lambda i: (0, 0)
    return _launch(
        body, name=name, grid=(t // tm,),
        in_specs=[pl.BlockSpec((tm, d), tok), pl.BlockSpec((tm, f), tok)]
        + ([pl.BlockSpec((1, d), one), pl.BlockSpec((tm, d), tok)] if loss_head else []) + [HBM_SPEC] * nw,
        out_specs=[pl.BlockSpec((tm, d), tok)] + [pl.BlockSpec((1, d), one)] * nl,
        out_shape=[jax.ShapeDtypeStruct((t, d), F32)] + [jax.ShapeDtypeStruct((1, d), F32)] * nl,
        scratch_shapes=[pltpu.VMEM((f, d), BF16), pltpu.SemaphoreType.DMA((n_copies,))],
        args=(x, s, *(loss_head or ()), *flat), cargo=cargo)


def _ffn_backward(dxo, x, g, a, b, wg_t, wu_t, wd, name, cargo=()):
    t, d = x.shape
    f = wd.shape[0]
    tm = min(TM_FFN // 2, t)
    chunks = _feature_chunks(f, FFN_BWD_CHUNKS)
    flat, copies, n_copies = _piece_rows([wg_t, wu_t, wd])
    nw = len(flat)

    def body(dxo_ref, x_ref, g_ref, a_ref, b_ref, *rest):
        w_hbm, (dx_ref, da_ref, db_ref, h_ref, do_ref, dg_ref, wg, wu, wdn, sems) = rest[:nw], rest[nw:]

        @pl.when(pl.program_id(0) == 0)
        def _():
            _load_rows(copies(w_hbm, [wg, wu, wdn]), sems)
            dg_ref[...] = jnp.zeros_like(dg_ref)

        xv = x_ref[...]
        gv = g_ref[...]
        r = lax.rsqrt(jnp.mean(xv * xv, axis=-1, keepdims=True) + EPS)
        xhat = xv * r
        h_ref[...] = (xhat * gv).astype(BF16)
        dxo_v = dxo_ref[...]
        dout = (0.5 * dxo_v).astype(BF16)
        do_ref[...] = dout
        dh = jnp.zeros((tm, d), F32)
        for s0, sz in chunks:
            ds = _nt(dout, wdn[s0:s0 + sz, :])
            av = a_ref[:, s0:s0 + sz].astype(F32)
            bv = b_ref[:, s0:s0 + sz].astype(F32)
            sig = _sigmoid(av)
            silu = av * sig
            da = (ds * bv * (sig * (1.0 + av * (1.0 - sig)))).astype(BF16)
            db = (ds * silu).astype(BF16)
            da_ref[:, s0:s0 + sz] = da
            db_ref[:, s0:s0 + sz] = db
            dh = dh + _nn(da, wg[s0:s0 + sz, :]) + _nn(db, wu[s0:s0 + sz, :])
        dg_ref[...] += jnp.sum(dh * xhat, axis=0, keepdims=True)
        dxh = dh * gv
        dx_ref[...] = dxo_v + r * (dxh - xhat * jnp.mean(dxh * xhat, axis=-1, keepdims=True))

    tok = lambda i: (i, 0)
    one = lambda i: (0, 0)
    return _launch(
        body, name=name, grid=(t // tm,),
        in_specs=[pl.BlockSpec((tm, d), tok), pl.BlockSpec((tm, d), tok), pl.BlockSpec((1, d), one),
                  pl.BlockSpec((tm, f), tok), pl.BlockSpec((tm, f), tok)] + [HBM_SPEC] * nw,
        out_specs=[pl.BlockSpec((tm, d), tok), pl.BlockSpec((tm, f), tok), pl.BlockSpec((tm, f), tok),
                   pl.BlockSpec((tm, d), tok), pl.BlockSpec((tm, d), tok), pl.BlockSpec((1, d), one)],
        out_shape=[jax.ShapeDtypeStruct((t, d), F32), jax.ShapeDtypeStruct((t, f), BF16), jax.ShapeDtypeStruct((t, f), BF16),
                   jax.ShapeDtypeStruct((t, d), BF16), jax.ShapeDtypeStruct((t, d), BF16), jax.ShapeDtypeStruct((1, d), F32)],
        scratch_shapes=[pltpu.VMEM((f, d), BF16), pltpu.VMEM((f, d), BF16), pltpu.VMEM((f, d), BF16), pltpu.SemaphoreType.DMA((n_copies,))],
        args=(dxo, x, g, a, b, *flat), cargo=cargo)


def _weight_grad(lhs, rhs, name, cargo=()):
    t, m = lhs.shape
    d = rhs.shape[1]
    tm = min(TM_TN, t)
    nt = t // tm
    nj = 1
    bm = m // nj
    cpb = N_CHIPS // nj
    rps = m // N_CHIPS
    hr = rps // 2
    assert hr % 16 == 0

    def body(l_ref, r_ref, o_ref, acc, stage, recv, send_sems, recv_sems):
        j = pl.program_id(0)
        i = pl.program_id(1)
        @pl.when(i == 0)
        def _():
            acc[...] = jnp.zeros_like(acc)

        acc[...] += _tn(l_ref[...], r_ref[...])

        def pair_sum(jj):
            x, y, c = _my_place()
            copies = []
            for q in range(cpb):
                slot = jj * cpb + q
                stage[slot] = acc[pl.ds(pl.multiple_of(q * rps + (1 - c) * hr, 16), hr), :].astype(BF16)
                cp = pltpu.make_async_remote_copy(
                    src_ref=stage.at[slot], dst_ref=recv.at[slot], send_sem=send_sems.at[slot], recv_sem=recv_sems.at[slot],
                    device_id=(x, y, 1 - c), device_id_type=MESH)
                cp.start()
                copies.append(cp)
            for q, cp in enumerate(copies):
                cp.wait_recv()
                mine = acc[pl.ds(pl.multiple_of(q * rps + c * hr, 16), hr), :]
                o_ref[q] = (mine + recv[jj * cpb + q].astype(F32)).astype(BF16)
            for cp in copies:
                cp.wait_send()

        for jj in range(nj):
            @pl.when(jnp.logical_and(i == nt - 1, j == jj))
            def _():
                pair_sum(jj)

    outs, carried = _launch(
        body, name=name, grid=(nj, nt),
        in_specs=[pl.BlockSpec((tm, bm), lambda j, i: (i, j)), pl.BlockSpec((tm, d), lambda j, i: (i, 0))],
        out_specs=[pl.BlockSpec((cpb, hr, d), lambda j, i: (j, 0, 0))],
        out_shape=[jax.ShapeDtypeStruct((N_CHIPS, hr, d), BF16)],
        scratch_shapes=[pltpu.VMEM((bm, d), F32), pltpu.VMEM((N_CHIPS, hr, d), BF16), pltpu.VMEM((N_CHIPS, hr, d), BF16),
                        pltpu.SemaphoreType.DMA((N_CHIPS,)), pltpu.SemaphoreType.DMA((N_CHIPS,))],
        args=(lhs, rhs), cargo=cargo)
    return outs[0], carried


def _window_sums(src, cols, w, tm, levels, trailing):
    def read_src(lo, hi):
        return src[lo:hi, cols]

    read, k, level = read_src, 1, 0
    while True:
        last = 2 * k == w
        if trailing:
            lo, hi = (HALO if last else 8 * (level + 1)), HALO + tm
            cur = read(lo, hi) + read(lo - k, hi - k)
        else:
            lo, hi = 0, (tm if last else tm + HALO - 8 * (level + 1))
            cur = read(lo, hi) + read(lo + k, hi + k)
        if last:
            return cur
        levels[level, lo:hi, :] = cur
        read = lambda a, b, level=level: levels[level, a:b, :]
        k, level = 2 * k, level + 1


def _pool_parts(u_cols, ubuf, cols, w, row, tm, levels):
    ws = _window_sums(ubuf, cols, w, tm, levels, trailing=True)
    inv = 1.0 / jnp.minimum(row + 1, w).astype(F32)
    return ws * inv - u_cols, inv


def _mixer_forward(x, g, win_t, wout_x, conv_w, pool_w, pool_scale, cargo=()):
    t, d = x.shape
    dc = win_t.shape[0] // 4
    gcw = dc // len(POOL_WINDOWS)
    wo_rows = d // N_CHIPS
    wo_stride = wout_x.shape[0] // N_CHIPS
    tm = min(TM_MIX, t)

    def body(x_ref, g_ref, win_hbm, wout_hbm, cw_ref, pw_ref, ps_ref, xo_ref, proj_ref, y_ref,
             win, wout, zbuf, ubuf, levels, sems):
        i = pl.program_id(0)

        @pl.when(i == 0)
        def _():
            pairs = [(win_hbm, win)]
            for k in range(N_CHIPS):
                pairs.append((wout_hbm.at[pl.ds(k * wo_stride, wo_rows), :], wout.at[pl.ds(k * wo_rows, wo_rows), :]))
            _load_rows(pairs, sems)
            zbuf[0:8, :] = jnp.zeros((8, dc), F32)
            ubuf[0:HALO, :] = jnp.zeros((HALO, dc), F32)

        xv = x_ref[...]
        r = lax.rsqrt(jnp.mean(xv * xv, axis=-1, keepdims=True) + EPS)
        h = (xv * r * g_ref[...]).astype(BF16)
        v = _nt(h, win[0:dc, :])
        gb = _nt(h, win[dc:2 * dc, :])
        gc = _nt(h, win[2 * dc:3 * dc, :])
        u = _nt(h, win[3 * dc:4 * dc, :])
        proj_ref[:, 0:dc] = v.astype(BF16)
        proj_ref[:, dc:2 * dc] = gb.astype(BF16)
        proj_ref[:, 2 * dc:3 * dc] = gc.astype(BF16)
        proj_ref[:, 3 * dc:4 * dc] = u.astype(BF16)

        z = gc * v
        zbuf[8:8 + tm, :] = z
        cw = cw_ref[...]
        conv = cw[2:3, :] * z + cw[1:2, :] * zbuf[7:7 + tm, :] + cw[0:1, :] * zbuf[6:6 + tm, :]
        y_ref[:, 0:dc] = (gb * conv).astype(BF16)

        ubuf[HALO:HALO + tm, :] = u
        row = i * tm + lax.broadcasted_iota(jnp.int32, (tm, 1), 0)
        for gi, w in enumerate(POOL_WINDOWS):
            cols = slice(gi * gcw, (gi + 1) * gcw)
            pooled, _ = _pool_parts(u[:, cols], ubuf, cols, w, row, tm, levels)
            yb = _nn(pooled.astype(BF16), pw_ref[gi].astype(BF16)) * ps_ref[:, cols]
            y_ref[:, dc + gi * gcw:dc + (gi + 1) * gcw] = yb.astype(BF16)

        xo_ref[...] = xv + _nn(y_ref[...], wout[...])
        zbuf[0:8, :] = zbuf[tm:tm + 8, :]
        ubuf[0:HALO, :] = ubuf[tm:tm + HALO, :]

    tok = lambda i: (i, 0)
    one = lambda i: (0, 0)
    return _launch(
        body, name="mixer_forward", grid=(t // tm,),
        in_specs=[pl.BlockSpec((tm, d), tok), pl.BlockSpec((1, d), one), HBM_SPEC, HBM_SPEC,
                  pl.BlockSpec(conv_w.shape, one), pl.BlockSpec(pool_w.shape, lambda i: (0, 0, 0)), pl.BlockSpec((1, dc), one)],
        out_specs=[pl.BlockSpec((tm, d), tok), pl.BlockSpec((tm, 4 * dc), tok), pl.BlockSpec((tm, 2 * dc), tok)],
        out_shape=[jax.ShapeDtypeStruct((t, d), F32), jax.ShapeDtypeStruct((t, 4 * dc), BF16), jax.ShapeDtypeStruct((t, 2 * dc), BF16)],
        scratch_shapes=[pltpu.VMEM((4 * dc, d), BF16), pltpu.VMEM((2 * dc, d), BF16),
                        pltpu.VMEM((tm + 8, dc), F32), pltpu.VMEM((tm + HALO, dc), F32),
                        pltpu.VMEM((WINDOW_LEVELS, tm + HALO, gcw), F32), pltpu.SemaphoreType.DMA((1 + N_CHIPS,))],
        args=(x, g, win_t, wout_x, conv_w, pool_w, pool_scale), cargo=cargo)


def _mixer_backward(dxo, x, g, proj, win_t, wout_x, conv_w, pool_w, pool_scale, cargo=()):
    t, d = x.shape
    dc = win_t.shape[0] // 4
    ng = len(POOL_WINDOWS)
    gcw = dc // ng
    wo_rows = d // N_CHIPS
    wo_stride = wout_x.shape[0] // N_CHIPS
    tm = min(TM_MIX, t)
    n_tiles = t // tm
    hb = tm // HALO

    def body(dxo_ref, x_ref, g_ref, proj_ref, halo_ref, win_hbm, wout_hbm, cw_ref, pw_ref, ps_ref,
             dx_ref, dproj_ref, h_ref, dxob_ref, dg_ref, dcw_ref, dps_ref, dpw_ref,
             win, wout, zbuf, ubuf, dcbuf, ebuf, levels, sems):
        i = pl.program_id(0)
        tile = n_tiles - 1 - i

        @pl.when(i == 0)
        def _():
            pairs = [(win_hbm, win)]
            for k in range(N_CHIPS):
                pairs.append((wout_hbm.at[pl.ds(k * wo_stride, wo_rows), :], wout.at[pl.ds(k * wo_rows, wo_rows), :]))
            _load_rows(pairs, sems)
            dcbuf[tm:tm + 8, :] = jnp.zeros((8, dc), F32)
            ebuf[tm:tm + HALO, :] = jnp.zeros((HALO, dc), F32)
            dg_ref[...] = jnp.zeros_like(dg_ref)
            dcw_ref[...] = jnp.zeros_like(dcw_ref)
            dps_ref[...] = jnp.zeros_like(dps_ref)
            dpw_ref[...] = jnp.zeros_like(dpw_ref)

        xv = x_ref[...]
        gv = g_ref[...]
        r = lax.rsqrt(jnp.mean(xv * xv, axis=-1, keepdims=True) + EPS)
        xhat = xv * r
        h_ref[...] = (xhat * gv).astype(BF16)
        dxo_v = dxo_ref[...]
        dxo_b = dxo_v.astype(BF16)
        dxob_ref[...] = dxo_b

        v = proj_ref[:, 0:dc].astype(F32)
        gb = proj_ref[:, dc:2 * dc].astype(F32)
        gc = proj_ref[:, 2 * dc:3 * dc].astype(F32)
        u = proj_ref[:, 3 * dc:4 * dc].astype(F32)
        first = jnp.where(tile > 0, 1.0, 0.0)
        zbuf[0:HALO, :] = halo_ref[:, 2 * dc:3 * dc].astype(F32) * halo_ref[:, 0:dc].astype(F32) * first
        ubuf[0:HALO, :] = halo_ref[:, 3 * dc:4 * dc].astype(F32) * first
        z = gc * v
        zbuf[HALO:HALO + tm, :] = z
        ubuf[HALO:HALO + tm, :] = u
        z1 = zbuf[HALO - 1:HALO - 1 + tm, :]
        z2 = zbuf[HALO - 2:HALO - 2 + tm, :]
        cw = cw_ref[...]
        conv = cw[2:3, :] * z + cw[1:2, :] * z1 + cw[0:1, :] * z2

        dy = _nt(dxo_b, wout[...])
        dya = dy[:, 0:dc]
        dgb = dya * conv
        dconv = dya * gb
        dcbuf[0:tm, :] = dconv
        dz = cw[2:3, :] * dconv + cw[1:2, :] * dcbuf[1:1 + tm, :] + cw[0:1, :] * dcbuf[2:2 + tm, :]
        dgc = dz * v
        dv = dz * gc
        dcw_ref[0:1, :] += jnp.sum(dconv * z2, axis=0, keepdims=True)
        dcw_ref[1:2, :] += jnp.sum(dconv * z1, axis=0, keepdims=True)
        dcw_ref[2:3, :] += jnp.sum(dconv * z, axis=0, keepdims=True)

        dproj_ref[:, 0:dc] = dv.astype(BF16)
        dproj_ref[:, dc:2 * dc] = dgb.astype(BF16)
        dproj_ref[:, 2 * dc:3 * dc] = dgc.astype(BF16)

        row = tile * tm + lax.broadcasted_iota(jnp.int32, (tm, 1), 0)
        for gi, w in enumerate(POOL_WINDOWS):
            cols = slice(gi * gcw, (gi + 1) * gcw)
            pooled, inv_cnt = _pool_parts(u[:, cols], ubuf, cols, w, row, tm, levels)
            pooled_b = pooled.astype(BF16)
            pw_b = pw_ref[gi].astype(BF16)
            dyb = dy[:, dc + gi * gcw:dc + (gi + 1) * gcw]
            q = _nn(pooled_b, pw_b)
            dps_ref[:, cols] += jnp.sum(q * dyb, axis=0, keepdims=True)
            dq = (dyb * ps_ref[:, cols]).astype(BF16)
            dpw_ref[gi] += _tn(pooled_b, dq)
            dpooled = _nt(dq, pw_b)
            ebuf[0:tm, cols] = dpooled * inv_cnt
            du = _window_sums(ebuf, cols, w, tm, levels, trailing=False) - dpooled
            dproj_ref[:, 3 * dc + gi * gcw:3 * dc + (gi + 1) * gcw] = du.astype(BF16)

        dh = _nn(dproj_ref[...], win[...])
        dg_ref[...] += jnp.sum(dh * xhat, axis=0, keepdims=True)
        dxh = dh * gv
        dx_ref[...] = dxo_v + r * (dxh - xhat * jnp.mean(dxh * xhat, axis=-1, keepdims=True))
        dcbuf[tm:tm + 8, :] = dcbuf[0:8, :]
        ebuf[tm:tm + HALO, :] = ebuf[0:HALO, :]

    tok = lambda i: (n_tiles - 1 - i, 0)
    halo = lambda i: (jnp.maximum((n_tiles - 1 - i) * hb - 1, 0), 0)
    one = lambda i: (0, 0)
    return _launch(
        body, name="mixer_backward", grid=(n_tiles,),
        in_specs=[pl.BlockSpec((tm, d), tok), pl.BlockSpec((tm, d), tok), pl.BlockSpec((1, d), one),
                  pl.BlockSpec((tm, 4 * dc), tok), pl.BlockSpec((HALO, 4 * dc), halo), HBM_SPEC, HBM_SPEC,
                  pl.BlockSpec(conv_w.shape, one), pl.BlockSpec(pool_w.shape, lambda i: (0, 0, 0)), pl.BlockSpec((1, dc), one)],
        out_specs=[pl.BlockSpec((tm, d), tok), pl.BlockSpec((tm, 4 * dc), tok), pl.BlockSpec((tm, d), tok), pl.BlockSpec((tm, d), tok),
                   pl.BlockSpec((1, d), one), pl.BlockSpec(conv_w.shape, one), pl.BlockSpec((1, dc), one),
                   pl.BlockSpec(pool_w.shape, lambda i: (0, 0, 0))],
        out_shape=[jax.ShapeDtypeStruct((t, d), F32), jax.ShapeDtypeStruct((t, 4 * dc), BF16), jax.ShapeDtypeStruct((t, d), BF16),
                   jax.ShapeDtypeStruct((t, d), BF16), jax.ShapeDtypeStruct((1, d), F32), jax.ShapeDtypeStruct(conv_w.shape, F32),
                   jax.ShapeDtypeStruct((1, dc), F32), jax.ShapeDtypeStruct(pool_w.shape, F32)],
        scratch_shapes=[pltpu.VMEM((4 * dc, d), BF16), pltpu.VMEM((2 * dc, d), BF16),
                        pltpu.VMEM((tm + HALO, dc), F32), pltpu.VMEM((tm + HALO, dc), F32),
                        pltpu.VMEM((tm + 8, dc), F32), pltpu.VMEM((tm + HALO, dc), F32),
                        pltpu.VMEM((WINDOW_LEVELS, tm + HALO, gcw), F32), pltpu.SemaphoreType.DMA((1 + N_CHIPS,))],
        args=(dxo, x, g, proj, proj, win_t, wout_x, conv_w, pool_w, pool_scale), cargo=cargo)


def _adam_update(w, gv, m, v):
    m_new = ADAM_B1 * m + (1.0 - ADAM_B1) * gv
    v_new = ADAM_B2 * v + (1.0 - ADAM_B2) * (gv * gv)
    m_hat = m_new / (1.0 - ADAM_B1 ** ADAM_STEP)
    v_hat = v_new / (1.0 - ADAM_B2 ** ADAM_STEP)
    return -ADAM_LR * (m_hat / (jnp.sqrt(v_hat) + ADAM_EPS) + ADAM_WD * w), m_new, v_new


def _adamw(w, grad, m, v, name):
    rows, cols = w.shape
    br = _row_block(rows, 512) if rows >= 8 else rows

    def body(w_ref, g_ref, m_ref, v_ref, go_ref, d_ref, mo_ref, vo_ref):
        gv = g_ref[...]
        go_ref[...] = gv
        d_ref[...], mo_ref[...], vo_ref[...] = _adam_update(w_ref[...], gv, m_ref[...], v_ref[...])

    blk = pl.BlockSpec((br, cols), lambda i: (i, 0))
    return pl.pallas_call(
        body, name=name,
        out_shape=[jax.ShapeDtypeStruct((rows, cols), F32)] * 4,
        grid=(rows // br,), in_specs=[blk] * 4, out_specs=[blk] * 4,
        compiler_params=pltpu.CompilerParams(dimension_semantics=("parallel",), vmem_limit_bytes=VMEM_LIMIT),
    )(w, grad, m, v)


def _adamw_transposed(w, grad_t, m, v, name):
    _, rows, cols = w.shape
    br = 256 if rows % 256 == 0 else rows

    def body(w_ref, gt_ref, m_ref, v_ref, g_ref, d_ref, mo_ref, vo_ref):
        gv = gt_ref[...].T
        g_ref[...] = gv
        d_ref[...], mo_ref[...], vo_ref[...] = _adam_update(w_ref[...], gv, m_ref[...], v_ref[...])

    blk = pl.BlockSpec((None, br, cols), lambda i: (0, i, 0))
    return pl.pallas_call(
        body, name=name,
        out_shape=[jax.ShapeDtypeStruct((1, rows, cols), F32)] * 4,
        grid=(rows // br,), in_specs=[blk, pl.BlockSpec((cols, br), lambda i: (0, i)), blk, blk], out_specs=[blk] * 4,
        compiler_params=pltpu.CompilerParams(dimension_semantics=("parallel",)),
    )(w, grad_t, m, v)


def _f32_rows_as_bf16(a, rows, cols):
    bits = lax.bitcast_convert_type(a, BF16).reshape(a.shape[0], 2 * a.shape[1])
    return jnp.pad(bits, ((0, rows - bits.shape[0]), (0, cols - bits.shape[1])))


def kernel(x, norm_ffn1, ffn1_w_gate, ffn1_w_up, ffn1_w_down, norm_mix, w_in, conv_w, pool_w, pool_scale, w_out, norm_ffn2, ffn2_w_gate, ffn2_w_up, ffn2_w_down, norm_final, loss_target, m_norm_ffn1, m_ffn1_w_gate, m_ffn1_w_up, m_ffn1_w_down, m_norm_mix, m_w_in, m_conv_w, m_pool_w, m_pool_scale, m_w_out, m_norm_ffn2, m_ffn2_w_gate, m_ffn2_w_up, m_ffn2_w_down, m_norm_final, v_norm_ffn1, v_ffn1_w_gate, v_ffn1_w_up, v_ffn1_w_down, v_norm_mix, v_w_in, v_conv_w, v_pool_w, v_pool_scale, v_w_out, v_norm_ffn2, v_ffn2_w_gate, v_ffn2_w_up, v_ffn2_w_down, v_norm_final):
    weights = dict(norm_ffn1=norm_ffn1, ffn1_w_gate=ffn1_w_gate, ffn1_w_up=ffn1_w_up, ffn1_w_down=ffn1_w_down, norm_mix=norm_mix,
                   w_in=w_in, conv_w=conv_w, pool_w=pool_w, pool_scale=pool_scale, w_out=w_out, norm_ffn2=norm_ffn2,
                   ffn2_w_gate=ffn2_w_gate, ffn2_w_up=ffn2_w_up, ffn2_w_down=ffn2_w_down, norm_final=norm_final)
    first_m = dict(norm_ffn1=m_norm_ffn1, ffn1_w_gate=m_ffn1_w_gate, ffn1_w_up=m_ffn1_w_up, ffn1_w_down=m_ffn1_w_down,
                   norm_mix=m_norm_mix, w_in=m_w_in, conv_w=m_conv_w, pool_w=m_pool_w, pool_scale=m_pool_scale, w_out=m_w_out,
                   norm_ffn2=m_norm_ffn2, ffn2_w_gate=m_ffn2_w_gate, ffn2_w_up=m_ffn2_w_up, ffn2_w_down=m_ffn2_w_down,
                   norm_final=m_norm_final)
    second_m = dict(norm_ffn1=v_norm_ffn1, ffn1_w_gate=v_ffn1_w_gate, ffn1_w_up=v_ffn1_w_up, ffn1_w_down=v_ffn1_w_down,
                    norm_mix=v_norm_mix, w_in=v_w_in, conv_w=v_conv_w, pool_w=v_pool_w, pool_scale=v_pool_scale, w_out=v_w_out,
                    norm_ffn2=v_norm_ffn2, ffn2_w_gate=v_ffn2_w_gate, ffn2_w_up=v_ffn2_w_up, ffn2_w_down=v_ffn2_w_down,
                    norm_final=v_norm_final)
    names = list(weights)

    xs = x[0]
    tgt = loss_target[0]
    t, d = xs.shape
    dc = pool_scale.shape[1]
    cx, cy, cc = _my_place()
    chip = 2 * cx + cy
    place = jnp.stack([chip, cc]).astype(jnp.int32)

    conv_rows = 32
    wout_x = jnp.concatenate([w_out[0].astype(BF16), _f32_rows_as_bf16(conv_w[0], conv_rows, d)], axis=0)

    g1, gm, g2 = norm_ffn1, norm_mix, norm_ffn2
    gf = norm_final.reshape(1, d)
    pw = pool_w[0]

    (wd1_shard, wg2_shard, wu2_shard, wd2_shard), [(wg1, wu1)] = _cast_to_bf16(
        [ffn1_w_down[0], ffn2_w_gate[0].T, ffn2_w_up[0].T, ffn2_w_down[0]], "gather_ffn1",
        [_gather_cargo([ffn1_w_gate[0].T.astype(BF16), ffn1_w_up[0].T.astype(BF16)])])
    (a1, b1, s1), [(wd1, win_t, wout_g)] = _ffn_up(
        xs, g1, wg1, wu1, "ffn1_up", [_gather_cargo([wd1_shard, w_in[0].T.astype(BF16), wout_x])])
    (x1,), [(wg2,)] = _ffn_down(xs, s1, wd1, "ffn1_down", [_gather_cargo([wg2_shard])])
    wo_rows = w_out.shape[1]
    cshard = conv_w.shape[2]
    conv_bits = wout_g.reshape(N_CHIPS, wo_rows + conv_rows, d)[:, wo_rows:wo_rows + conv_w.shape[1], :2 * cshard]
    conv_full = lax.bitcast_convert_type(conv_bits.reshape(N_CHIPS, conv_w.shape[1], cshard, 2), F32)
    conv_full = jnp.transpose(conv_full, (1, 0, 2)).reshape(conv_w.shape[1], N_CHIPS * cshard)
    (x2, proj, ymix), [(wu2,)] = _mixer_forward(x1, gm, win_t, wout_g, conv_full, pw, pool_scale, [_gather_cargo([wu2_shard])])
    (a2, b2, s2), [(wd2,)] = _ffn_up(x2, g2, wg2, wu2, "ffn2_up", [_gather_cargo([wd2_shard])])
    (dx3, sq_cols, dgf), _ = _ffn_down(x2, s2, wd2, "ffn2_down", loss_head=(gf, tgt))

    (dx2, da2, db2, h3, do2, dg2), _ = _ffn_backward(dx3, x2, g2, a2, b2, wg2, wu2, wd2, "ffn2_backward")
    p_wg2, _ = _weight_grad(da2, h3, "ffn2_gate_grad")
    p_wu2, [(x_wg2,)] = _weight_grad(db2, h3, "ffn2_up_grad", [_exchange_cargo([p_wg2])])
    p_wd2, [(x_wu2,)] = _weight_grad(s2, do2, "ffn2_down_grad", [_exchange_cargo([p_wu2])])

    (dx1, dproj, h2, dx2b, dgm, dcw, dps, dpw), [(x_wd2,)] = _mixer_backward(
        dx2, x1, gm, proj, win_t, wout_g, conv_full, pw, pool_scale, [_exchange_cargo([p_wd2])])

    (dx0, da1, db1, h1, do1, dg1), _ = _ffn_backward(dx1, xs, g1, a1, b1, wg1, wu1, wd1, "ffn1_backward")

    npw = pw.size // d
    head = [dg1, dgm, dg2, dgf, jnp.pad(dps, ((0, 0), (0, d - dc))), jnp.pad(dcw, ((0, 0), (0, d - dc))), sq_cols]
    n_head = sum(h.shape[0] for h in head)
    base = -(-n_head // 8) * 8
    pack = jnp.concatenate(head + [jnp.zeros((base - n_head, d), F32), dpw.reshape(npw, d)], axis=0)

    p_wg1, [(packs,)] = _weight_grad(da1, h1, "ffn1_gate_grad", [_all_gather_small_cargo(pack)])
    p_wu1, [(x_wg1,)] = _weight_grad(db1, h1, "ffn1_up_grad", [_exchange_cargo([p_wg1])])
    p_wd1, [(x_wu1,)] = _weight_grad(s1, do1, "ffn1_down_grad", [_exchange_cargo([p_wu1])])
    p_win, [(x_wd1,)] = _weight_grad(dproj, h2, "w_in_grad", [_exchange_cargo([p_wd1])])
    p_wout, [(x_win,)] = _weight_grad(ymix, dx2b, "w_out_grad", [_exchange_cargo([p_win])])
    x_wout, = _run_cargo(_exchange_cargo([p_wout]), "grad_exchange_last")
    small = _sum_by_device(packs)
    loss = jnp.sum(small[n_head - 1]) * (0.5 / d)

    order = ["wg1", "wu1", "wd1", "win", "wout", "wg2", "wu2", "wd2"]
    pairs = dict(wg1=p_wg1, wu1=p_wu1, wd1=p_wd1, win=p_win, wout=p_wout, wg2=p_wg2, wu2=p_wu2, wd2=p_wd2)
    landed = dict(wg1=x_wg1, wu1=x_wu1, wd1=x_wd1, win=x_win, wout=x_wout, wg2=x_wg2, wu2=x_wu2, wd2=x_wd2)
    both = _sibling_share([_chip_sum(pairs[k], landed[k], place, k) for k in order])
    rwg1, rwu1, rwd1, rwin, rwout, rwg2, rwu2, rwd2 = [b.reshape(2 * b.shape[1], b.shape[2]) for b in both]

    grads = {
        "norm_ffn1": small[0:1], "norm_mix": small[1:2], "norm_ffn2": small[2:3], "norm_final": small[3],
        "pool_scale": small[4:5, :dc],
        "conv_w": lax.dynamic_slice_in_dim(small[5:5 + dcw.shape[0], :dc], chip * cshard, cshard, axis=1)[None],
        "pool_w": small[base:].reshape(pool_w.shape),
        "ffn1_w_down": rwd1[None], "w_out": rwout[None], "ffn2_w_down": rwd2[None],
    }
    by_view = {"ffn1_w_gate": rwg1, "ffn1_w_up": rwu1, "ffn2_w_gate": rwg2, "ffn2_w_up": rwu2}

    deltas, new_m, new_v = {}, {}, {}
    for n in names:
        w = weights[n]
        shape = w.shape
        if n == "w_in":
            grads[n], deltas[n], new_m[n], new_v[n] = _adamw_transposed(w, rwin, first_m[n], second_m[n], "adamw_" + n)
            continue
        if n in by_view:
            view = lambda a: jnp.swapaxes(a, 1, 2)[0]
            back = lambda a: jnp.swapaxes(a[None], 1, 2)
            outs = _adamw(view(w), by_view[n], view(first_m[n]), view(second_m[n]), "adamw_" + n)
            grads[n], deltas[n], new_m[n], new_v[n] = [back(o) for o in outs]
            continue
        as2d = (lambda a: a.reshape(-1, shape[-1]))
        outs = _adamw(as2d(w), as2d(grads[n]), as2d(first_m[n]), as2d(second_m[n]), "adamw_" + n)
        grads[n], deltas[n], new_m[n], new_v[n] = [o.reshape(shape) for o in outs]

    return (loss, dx0[None], *[grads[n] for n in names], *[deltas[n] for n in names],
            *[new_m[n] for n in names], *[new_v[n] for n in names])
```

```python
import jax
import jax.numpy as jnp
from jax import lax
from jax.experimental import pallas as pl
from jax.experimental.pallas import tpu as pltpu

F32 = jnp.float32
BF16 = jnp.bfloat16
MESH = pl.DeviceIdType.MESH

EPS = 1e-6
POOL_WINDOWS = (2, 4, 8, 16)
ADAM_LR = 0.001
ADAM_B1 = 0.9
ADAM_B2 = 0.999
ADAM_EPS = 1e-08
ADAM_WD = 0.01
ADAM_STEP = 10

N_CHIPS = 4
N_DEVICES = 8
MXU_COLS_V7X = 256
VMEM_LIMIT = 56 * 1024 * 1024
TM_FFN = 512
TM_MIX = 512
TM_TN = 1024
HALO = 32
WINDOW_LEVELS = 3
FFN_FWD_CHUNKS = 2
FFN_BWD_CHUNKS = 2


def _nt(a, b):
    return lax.dot_general(a, b, (((1,), (1,)), ((), ())), preferred_element_type=F32)


def _tn(a, b):
    return lax.dot_general(a, b, (((0,), (0,)), ((), ())), preferred_element_type=F32)


def _nn(a, b):
    return jnp.dot(a, b, preferred_element_type=F32)


def _sigmoid(a):
    return 1.0 / (1.0 + jnp.exp(-a))


def _feature_chunks(n, parts):
    assert n % MXU_COLS_V7X == 0
    tiles = n // MXU_COLS_V7X
    out, s0 = [], 0
    for p in range(parts):
        sz = (tiles // parts + (1 if p < tiles % parts else 0)) * MXU_COLS_V7X
        if sz:
            out.append((s0, sz))
            s0 += sz
    return out


def _row_block(rows, cap):
    best = 8
    for b in range(8, min(rows, cap) + 1, 8):
        if rows % b == 0:
            best = b
    assert rows % best == 0
    return best


def _my_place():
    return lax.axis_index("x"), lax.axis_index("y"), lax.axis_index("c")


def _other_chips(x, y):
    return [(1 - x, y), (x, 1 - y), (1 - x, 1 - y)]


HBM_SPEC = pl.BlockSpec(memory_space=pltpu.HBM)


class _Cargo:
    def __init__(self, operands, out_shapes, n_sems, phases, when):
        self.operands, self.out_shapes, self.n_sems = list(operands), list(out_shapes), n_sems
        self.phases, self.when = list(phases), list(when)
        assert len(self.phases) == len(self.when) and self.when[0] == 0.0 and self.when[-1] == 1.0


def _launch(body, *, name, grid, in_specs, out_specs, out_shape, scratch_shapes, args, cargo=()):
    params = pltpu.CompilerParams(dimension_semantics=("arbitrary",) * len(grid), vmem_limit_bytes=VMEM_LIMIT)
    cargos = list(cargo)
    c_operands = [op for cg in cargos for op in cg.operands]
    c_shapes = [sh for cg in cargos for sh in cg.out_shapes]
    counts = [len(in_specs), len(c_operands), len(out_shape), len(c_shapes), len(scratch_shapes), 2 * len(cargos)]

    def carrying(*refs):
        groups, pos = [], 0
        for k in counts:
            groups.append(refs[pos:pos + k])
            pos += k
        ins, c_ins, outs, c_outs, scratch, sems = groups
        parts, pi, po = [], 0, 0
        for n, cg in enumerate(cargos):
            parts.append((c_ins[pi:pi + len(cg.operands)], c_outs[po:po + len(cg.out_shapes)], sems[2 * n], sems[2 * n + 1]))
            pi += len(cg.operands)
            po += len(cg.out_shapes)
        step, steps = 0, 1
        for ax, g in enumerate(grid):
            step = step * g + pl.program_id(ax)
            steps *= g
        todo = {}
        for cg, part in zip(cargos, parts):
            for phase, frac in zip(cg.phases[:-1], cg.when[:-1]):
                todo.setdefault(int(round(frac * (steps - 1))), []).append((phase, part))

        for at in sorted(todo):
            @pl.when(step == at)
            def _(at=at):
                for phase, part in todo[at]:
                    phase(*part)

        body(*ins, *outs, *scratch)

        if cargos:
            @pl.when(step == steps - 1)
            def _():
                for cg, part in zip(cargos, parts):
                    cg.phases[-1](*part)

    sems = [pltpu.SemaphoreType.DMA((cg.n_sems,)) for cg in cargos for _ in range(2)]
    outs = pl.pallas_call(
        carrying, name=name, grid=grid,
        in_specs=list(in_specs) + [HBM_SPEC] * counts[1], out_specs=list(out_specs) + [HBM_SPEC] * counts[3],
        out_shape=list(out_shape) + c_shapes, scratch_shapes=list(scratch_shapes) + sems,
        compiler_params=params)(*args, *c_operands)
    own, rest = list(outs[:counts[2]]), list(outs[counts[2]:])
    carried, po = [], 0
    for cg in cargos:
        carried.append(rest[po:po + len(cg.out_shapes)])
        po += len(cg.out_shapes)
    return own, carried


def _run_cargo(cargo, name):
    n_in, n_out = len(cargo.operands), len(cargo.out_shapes)

    def body(*refs):
        c_ins, c_outs, sems = refs[:n_in], refs[n_in:n_in + n_out], refs[n_in + n_out:]
        for phase in cargo.phases:
            phase(c_ins, c_outs, *sems)

    sem = pltpu.SemaphoreType.DMA((cargo.n_sems,))
    return list(pl.pallas_call(body, name=name, out_shape=cargo.out_shapes, in_specs=[HBM_SPEC] * n_in,
                               out_specs=[HBM_SPEC] * n_out, scratch_shapes=[sem, sem])(*cargo.operands))


def _gather_cargo(shards):
    n = len(shards)
    for s in shards:
        assert s.shape[0] % 32 == 0
    slots = 8

    def steps(ins, outs, send_sems, recv_sems):
        x, y, c = _my_place()
        sibling = (x, y, 1 - c)
        over_x, over_y = (1 - x, y, c), (x, 1 - y, c)
        mine, chip_x, chip_y, chip_d = 2 * x + y, 2 * (1 - x) + y, 2 * x + (1 - y), 2 * (1 - x) + (1 - y)

        def rows_of(a, chip_index, half, part=None):
            rps = shards[a].shape[0]
            hr = rps // 2
            first = -(-hr // 32) * 16
            offset, size = {None: (0, hr), 0: (0, first), 1: (first, hr - first)}[part]
            return outs[a].at[pl.ds(pl.multiple_of(chip_index * rps + half * hr + offset, 16), size), :]

        def remote(a, slot, src, dst, to):
            return pltpu.make_async_remote_copy(
                src_ref=src, dst_ref=dst, send_sem=send_sems.at[a * slots + slot], recv_sem=recv_sems.at[a * slots + slot],
                device_id=to, device_id_type=MESH)

        def same_rows(a, slot, rows, to):
            return remote(a, slot, rows, rows, to)

        def own_copy(a):
            rps = shards[a].shape[0]
            return remote(a, 7, ins[a], outs[a].at[pl.ds(pl.multiple_of(mine * rps, 16), rps), :], sibling)

        def my_half(a):
            hr = shards[a].shape[0] // 2
            return ins[a].at[pl.ds(pl.multiple_of(c * hr, 16), hr), :]

        def start():
            for a in range(n):
                own_copy(a).start()
                remote(a, 0, my_half(a), rows_of(a, mine, c), over_x).start()
                remote(a, 1, my_half(a), rows_of(a, mine, c), over_y).start()

        def relay_neighbours():
            for a in range(n):
                same_rows(a, 0, rows_of(a, chip_x, c), over_x).wait_recv()
                same_rows(a, 4, rows_of(a, chip_x, c), sibling).start()
                same_rows(a, 2, rows_of(a, chip_x, c, 0), over_y).start()
                same_rows(a, 1, rows_of(a, chip_y, c), over_y).wait_recv()
                same_rows(a, 5, rows_of(a, chip_y, c), sibling).start()
                same_rows(a, 3, rows_of(a, chip_y, c, 1), over_x).start()

        def relay_diagonal():
            for a in range(n):
                same_rows(a, 2, rows_of(a, chip_d, c, 0), over_y).wait_recv()
                same_rows(a, 3, rows_of(a, chip_d, c, 1), over_x).wait_recv()
                same_rows(a, 6, rows_of(a, chip_d, c), sibling).start()

        def finish():
            for a in range(n):
                for slot, chip_index in ((4, chip_x), (5, chip_y), (6, chip_d)):
                    same_rows(a, slot, rows_of(a, chip_index, 1 - c), sibling).wait_recv()
            for a in range(n):
                remote(a, 0, my_half(a), rows_of(a, mine, c), over_x).wait_send()
                remote(a, 1, my_half(a), rows_of(a, mine, c), over_y).wait_send()
                same_rows(a, 2, rows_of(a, chip_x, c, 0), over_y).wait_send()
                same_rows(a, 3, rows_of(a, chip_y, c, 1), over_x).wait_send()
                for slot, chip_index in ((4, chip_x), (5, chip_y), (6, chip_d)):
                    same_rows(a, slot, rows_of(a, chip_index, c), sibling).wait_send()
                own_copy(a).wait()

        return start, relay_neighbours, relay_diagonal, finish

    phases = [lambda *r, k=k: steps(*r)[k]() for k in range(4)]
    return _Cargo(shards, [jax.ShapeDtypeStruct((N_CHIPS * s.shape[0], s.shape[1]), s.dtype) for s in shards], slots * n,
                  phases, [0.0, 0.6, 0.85, 1.0])


def _exchange_cargo(pairs):
    n = len(pairs)

    def copies(ins, outs, send_sems, recv_sems):
        x, y, c = _my_place()
        return [pltpu.make_async_remote_copy(
            src_ref=ins[a].at[2 * chip[0] + chip[1]], dst_ref=outs[a].at[j],
            send_sem=send_sems.at[3 * a + j], recv_sem=recv_sems.at[3 * a + j], device_id=(*chip, c), device_id_type=MESH)
            for a in range(n) for j, chip in enumerate(_other_chips(x, y))]

    def start(*r):
        for cp in copies(*r):
            cp.start()

    def finish(*r):
        for cp in copies(*r):
            cp.wait()

    return _Cargo(pairs, [jax.ShapeDtypeStruct((3,) + p.shape[1:], p.dtype) for p in pairs], 3 * n, [start, finish], [0.0, 1.0])


def _all_gather_small_cargo(pack):
    rows, cols = pack.shape

    def copies(ins, outs, send_sems, recv_sems):
        x, y, c = _my_place()
        me = 4 * x + 2 * y + c
        remote = []
        for f in range(1, N_DEVICES):
            fx, fy, fc = (f >> 2) & 1, (f >> 1) & 1, f & 1
            to = (1 - x if fx else x, 1 - y if fy else y, 1 - c if fc else c)
            remote.append(pltpu.make_async_remote_copy(
                src_ref=ins[0], dst_ref=outs[0].at[me], send_sem=send_sems.at[f - 1], recv_sem=recv_sems.at[f - 1],
                device_id=to, device_id_type=MESH))
        own = pltpu.make_async_copy(ins[0], outs[0].at[me], send_sems.at[N_DEVICES - 1])
        return remote, own

    def start(*r):
        remote, own = copies(*r)
        own.start()
        for cp in remote:
            cp.start()

    def finish(*r):
        remote, own = copies(*r)
        for cp in remote:
            cp.wait()
        own.wait()

    return _Cargo([pack], [jax.ShapeDtypeStruct((N_DEVICES, rows, cols), F32)], N_DEVICES, [start, finish], [0.0, 1.0])


def _sum_by_device(packs):
    n, rows, cols = packs.shape

    def body(p_ref, o_ref):
        acc = p_ref[0]
        for dev in range(1, n):
            acc = acc + p_ref[dev]
        o_ref[...] = acc

    return pl.pallas_call(body, name="small_grads_sum", out_shape=jax.ShapeDtypeStruct((rows, cols), F32))(packs)


def _chip_sum(pair, got, place, tag):
    _, hr, cols = pair.shape
    br = _row_block(hr, 256)

    def body(k_ref, p_ref, r_ref, o_ref):
        acc = p_ref[...].astype(F32)
        for j in range(3):
            acc = acc + r_ref[j].astype(F32)
        o_ref[...] = acc

    return pl.pallas_call(
        body, name="grad_chip_sum_" + tag,
        out_shape=jax.ShapeDtypeStruct((2, hr, cols), F32),
        grid_spec=pltpu.PrefetchScalarGridSpec(
            num_scalar_prefetch=1, grid=(hr // br,),
            in_specs=[pl.BlockSpec((None, br, cols), lambda r, k_ref: (k_ref[0], r, 0)),
                      pl.BlockSpec((3, br, cols), lambda r, k_ref: (0, r, 0))],
            out_specs=pl.BlockSpec((None, br, cols), lambda r, k_ref: (k_ref[1], r, 0))),
        compiler_params=pltpu.CompilerParams(dimension_semantics=("parallel",)),
    )(place, pair, got)


def _sibling_share(halves):
    n = len(halves)

    def body(*refs):
        outs = refs[n:2 * n]
        send_sems, recv_sems = refs[2 * n:]
        x, y, c = _my_place()
        copies = []
        for a in range(n):
            cp = pltpu.make_async_remote_copy(
                src_ref=outs[a].at[c], dst_ref=outs[a].at[c], send_sem=send_sems.at[a], recv_sem=recv_sems.at[a],
                device_id=(x, y, 1 - c), device_id_type=MESH)
            cp.start()
            copies.append(cp)
        for cp in copies:
            cp.wait()

    return pl.pallas_call(
        body, name="grad_share_sibling",
        out_shape=[jax.ShapeDtypeStruct(h.shape, h.dtype) for h in halves],
        in_specs=[HBM_SPEC] * n, out_specs=[HBM_SPEC] * n,
        input_output_aliases={a: a for a in range(n)},
        scratch_shapes=[pltpu.SemaphoreType.DMA((n,)), pltpu.SemaphoreType.DMA((n,))],
    )(*halves)


def _load_rows(pairs, sems):
    cps = [pltpu.make_async_copy(src, dst, sems.at[j]) for j, (src, dst) in enumerate(pairs)]
    for cp in cps:
        cp.start()
    for cp in cps:
        cp.wait()


def _piece_rows(weights):
    return list(weights), (lambda refs, mats: list(zip(refs, mats))), len(weights)


def _cast_to_bf16(arrays, name, cargo=()):
    rows, cols = arrays[0].shape
    n = len(arrays)
    br = _row_block(rows, 256)

    def body(*refs):
        for src, dst in zip(refs[:n], refs[n:]):
            dst[...] = src[...].astype(BF16)

    blk = pl.BlockSpec((br, cols), lambda i: (i, 0))
    return _launch(body, name=name, grid=(rows // br,), in_specs=[blk] * n, out_specs=[blk] * n,
                   out_shape=[jax.ShapeDtypeStruct((rows, cols), BF16)] * n, scratch_shapes=[], args=tuple(arrays), cargo=cargo)


def _loss_head(xv, gv, tv):
    d = xv.shape[-1]
    r = lax.rsqrt(jnp.mean(xv * xv, axis=-1, keepdims=True) + EPS)
    xhat = xv * r
    err = xhat * gv - tv
    dy = err * (1.0 / d)
    dxh = dy * gv
    dx = r * (dxh - xhat * jnp.mean(dxh * xhat, axis=-1, keepdims=True))
    return dx, jnp.sum(err * err, axis=0, keepdims=True), jnp.sum(dy * xhat, axis=0, keepdims=True)


def _ffn_up(x, g, wg_t, wu_t, name, cargo=()):
    t, d = x.shape
    f = wg_t.shape[0]
    tm = min(TM_FFN, t)
    chunks = _feature_chunks(f, FFN_FWD_CHUNKS)
    flat, copies, n_copies = _piece_rows([wg_t, wu_t])
    nw = len(flat)

    def body(x_ref, g_ref, *rest):
        w_hbm, (a_ref, b_ref, s_ref, wg, wu, sems) = rest[:nw], rest[nw:]

        @pl.when(pl.program_id(0) == 0)
        def _():
            _load_rows(copies(w_hbm, [wg, wu]), sems)

        xv = x_ref[...]
        r = lax.rsqrt(jnp.mean(xv * xv, axis=-1, keepdims=True) + EPS)
        h = (xv * r * g_ref[...]).astype(BF16)
        for s0, sz in chunks:
            a = _nt(h, wg[s0:s0 + sz, :])
            b = _nt(h, wu[s0:s0 + sz, :])
            a_ref[:, s0:s0 + sz] = a.astype(BF16)
            b_ref[:, s0:s0 + sz] = b.astype(BF16)
            s_ref[:, s0:s0 + sz] = (a * _sigmoid(a) * b).astype(BF16)

    tok = lambda i: (i, 0)
    wide = pl.BlockSpec((tm, f), tok)
    return _launch(
        body, name=name, grid=(t // tm,),
        in_specs=[pl.BlockSpec((tm, d), tok), pl.BlockSpec((1, d), lambda i: (0, 0))] + [HBM_SPEC] * nw,
        out_specs=[wide, wide, wide], out_shape=[jax.ShapeDtypeStruct((t, f), BF16)] * 3,
        scratch_shapes=[pltpu.VMEM((f, d), BF16), pltpu.VMEM((f, d), BF16), pltpu.SemaphoreType.DMA((n_copies,))],
        args=(x, g, *flat), cargo=cargo)


def _ffn_down(x, s, wd, name, cargo=(), loss_head=None):
    t, d = x.shape
    f = s.shape[1]
    tm = min(TM_FFN, t)
    flat, copies, n_copies = _piece_rows([wd])
    nw = len(flat)
    nl = 2 if loss_head else 0

    def body(x_ref, s_ref, *rest):
        head, w_hbm = rest[:nl], rest[nl:nl + nw]
        xo_ref = rest[nl + nw]
        sums, (wdn, sems) = rest[nl + nw + 1:nl + nw + 1 + nl], rest[nl + nw + 1 + nl:]

        @pl.when(pl.program_id(0) == 0)
        def _():
            _load_rows(copies(w_hbm, [wdn]), sems)
            for sum_ref in sums:
                sum_ref[...] = jnp.zeros_like(sum_ref)

        xo = x_ref[...] + 0.5 * _nn(s_ref[...], wdn[...])
        if loss_head:
            dx, sq, dgf = _loss_head(xo, head[0][...], head[1][...])
            xo_ref[...] = dx
            sums[0][...] += sq
            sums[1][...] += dgf
        else:
            xo_ref[...] = xo

    tok = lambda i: (i, 0)
    one = lambda i: (0, 0)
    return _launch(
        body, name=name, grid=(t // tm,),
        in_specs=[pl.BlockSpec((tm, d), tok), pl.BlockSpec((tm, f), tok)]
        + ([pl.BlockSpec((1, d), one), pl.BlockSpec((tm, d), tok)] if loss_head else []) + [HBM_SPEC] * nw,
        out_specs=[pl.BlockSpec((tm, d), tok)] + [pl.BlockSpec((1, d), one)] * nl,
        out_shape=[jax.ShapeDtypeStruct((t, d), F32)] + [jax.ShapeDtypeStruct((1, d), F32)] * nl,
        scratch_shapes=[pltpu.VMEM((f, d), BF16), pltpu.SemaphoreType.DMA((n_copies,))],
        args=(x, s, *(loss_head or ()), *flat), cargo=cargo)


def _ffn_backward(dxo, x, g, a, b, wg_t, wu_t, wd, name, cargo=()):
    t, d = x.shape
    f = wd.shape[0]
    tm = min(TM_FFN // 2, t)
    chunks = _feature_chunks(f, FFN_BWD_CHUNKS)
    flat, copies, n_copies = _piece_rows([wg_t, wu_t, wd])
    nw = len(flat)

    def body(dxo_ref, x_ref, g_ref, a_ref, b_ref, *rest):
        w_hbm, (dx_ref, dab_ref, hd_ref, dg_ref, wg, wu, wdn, sems) = rest[:nw], rest[nw:]

        @pl.when(pl.program_id(0) == 0)
        def _():
            _load_rows(copies(w_hbm, [wg, wu, wdn]), sems)
            dg_ref[...] = jnp.zeros_like(dg_ref)

        xv = x_ref[...]
        gv = g_ref[...]
        r = lax.rsqrt(jnp.mean(xv * xv, axis=-1, keepdims=True) + EPS)
        xhat = xv * r
        hd_ref[:, 0:d] = (xhat * gv).astype(BF16)
        dxo_v = dxo_ref[...]
        dout = (0.5 * dxo_v).astype(BF16)
        hd_ref[:, d:2 * d] = dout
        dh = jnp.zeros((tm, d), F32)
        for s0, sz in chunks:
            ds = _nt(dout, wdn[s0:s0 + sz, :])
            av = a_ref[:, s0:s0 + sz].astype(F32)
            bv = b_ref[:, s0:s0 + sz].astype(F32)
            sig = _sigmoid(av)
            silu = av * sig
            da = (ds * bv * (sig * (1.0 + av * (1.0 - sig)))).astype(BF16)
            db = (ds * silu).astype(BF16)
            dab_ref[:, s0:s0 + sz] = da
            dab_ref[:, f + s0:f + s0 + sz] = db
            dh = dh + _nn(da, wg[s0:s0 + sz, :]) + _nn(db, wu[s0:s0 + sz, :])
        dg_ref[...] += jnp.sum(dh * xhat, axis=0, keepdims=True)
        dxh = dh * gv
        dx_ref[...] = dxo_v + r * (dxh - xhat * jnp.mean(dxh * xhat, axis=-1, keepdims=True))

    tok = lambda i: (i, 0)
    one = lambda i: (0, 0)
    return _launch(
        body, name=name, grid=(t // tm,),
        in_specs=[pl.BlockSpec((tm, d), tok), pl.BlockSpec((tm, d), tok), pl.BlockSpec((1, d), one),
                  pl.BlockSpec((tm, f), tok), pl.BlockSpec((tm, f), tok)] + [HBM_SPEC] * nw,
        out_specs=[pl.BlockSpec((tm, d), tok), pl.BlockSpec((tm, 2 * f), tok), pl.BlockSpec((tm, 2 * d), tok),
                   pl.BlockSpec((1, d), one)],
        out_shape=[jax.ShapeDtypeStruct((t, d), F32), jax.ShapeDtypeStruct((t, 2 * f), BF16),
                   jax.ShapeDtypeStruct((t, 2 * d), BF16), jax.ShapeDtypeStruct((1, d), F32)],
        scratch_shapes=[pltpu.VMEM((f, d), BF16), pltpu.VMEM((f, d), BF16), pltpu.VMEM((f, d), BF16), pltpu.SemaphoreType.DMA((n_copies,))],
        args=(dxo, x, g, a, b, *flat), cargo=cargo)


def _weight_grad(lhs, rhs, name, cargo=(), lhs_part=(0, 1), rhs_part=(0, 1)):
    t = lhs.shape[0]
    m = lhs.shape[1] // lhs_part[1]
    d = rhs.shape[1] // rhs_part[1]
    tm = min(TM_TN, t)
    nt = t // tm
    rps = m // N_CHIPS
    hr = rps // 2
    assert hr % 16 == 0

    def body(l_ref, r_ref, o_ref, acc, stage, recv, send_sems, recv_sems):
        i = pl.program_id(0)

        @pl.when(i == 0)
        def _():
            acc[...] = jnp.zeros_like(acc)

        acc[...] += _tn(l_ref[...], r_ref[...])

        @pl.when(i == nt - 1)
        def _():
            x, y, c = _my_place()
            copies = []
            for q in range(N_CHIPS):
                stage[q] = acc[pl.ds(pl.multiple_of(q * rps + (1 - c) * hr, 16), hr), :].astype(BF16)
                cp = pltpu.make_async_remote_copy(
                    src_ref=stage.at[q], dst_ref=recv.at[q], send_sem=send_sems.at[q], recv_sem=recv_sems.at[q],
                    device_id=(x, y, 1 - c), device_id_type=MESH)
                cp.start()
                copies.append(cp)
            for q, cp in enumerate(copies):
                cp.wait_recv()
                mine = acc[pl.ds(pl.multiple_of(q * rps + c * hr, 16), hr), :]
                o_ref[q] = (mine + recv[q].astype(F32)).astype(BF16)
            for cp in copies:
                cp.wait_send()

    outs, carried = _launch(
        body, name=name, grid=(nt,),
        in_specs=[pl.BlockSpec((tm, m), lambda i: (i, lhs_part[0])), pl.BlockSpec((tm, d), lambda i: (i, rhs_part[0]))],
        out_specs=[pl.BlockSpec((N_CHIPS, hr, d), lambda i: (0, 0, 0))],
        out_shape=[jax.ShapeDtypeStruct((N_CHIPS, hr, d), BF16)],
        scratch_shapes=[pltpu.VMEM((m, d), F32), pltpu.VMEM((N_CHIPS, hr, d), BF16), pltpu.VMEM((N_CHIPS, hr, d), BF16),
                        pltpu.SemaphoreType.DMA((N_CHIPS,)), pltpu.SemaphoreType.DMA((N_CHIPS,))],
        args=(lhs, rhs), cargo=cargo)
    return outs[0], carried


def _window_sums(src, cols, w, tm, levels, trailing):
    def read_src(lo, hi):
        return src[lo:hi, cols]

    read, k, level = read_src, 1, 0
    while True:
        last = 2 * k == w
        if trailing:
            lo, hi = (HALO if last else 8 * (level + 1)), HALO + tm
            cur = read(lo, hi) + read(lo - k, hi - k)
        else:
            lo, hi = 0, (tm if last else tm + HALO - 8 * (level + 1))
            cur = read(lo, hi) + read(lo + k, hi + k)
        if last:
            return cur
        levels[level, lo:hi, :] = cur
        read = lambda a, b, level=level: levels[level, a:b, :]
        k, level = 2 * k, level + 1


def _pool_parts(u_cols, ubuf, cols, w, row, tm, levels):
    ws = _window_sums(ubuf, cols, w, tm, levels, trailing=True)
    inv = 1.0 / jnp.minimum(row + 1, w).astype(F32)
    return ws * inv - u_cols, inv


def _mixer_forward(x, g, win_t, wout_x, conv_w, pool_w, pool_scale, cargo=()):
    t, d = x.shape
    dc = win_t.shape[0] // 4
    gcw = dc // len(POOL_WINDOWS)
    wo_rows = d // N_CHIPS
    wo_stride = wout_x.shape[0] // N_CHIPS
    tm = min(TM_MIX, t)

    def body(x_ref, g_ref, win_hbm, wout_hbm, cw_ref, pw_ref, ps_ref, xo_ref, proj_ref, y_ref,
             win, wout, zbuf, ubuf, levels, sems):
        i = pl.program_id(0)

        @pl.when(i == 0)
        def _():
            pairs = [(win_hbm, win)]
            for k in range(N_CHIPS):
                pairs.append((wout_hbm.at[pl.ds(k * wo_stride, wo_rows), :], wout.at[pl.ds(k * wo_rows, wo_rows), :]))
            _load_rows(pairs, sems)
            zbuf[0:8, :] = jnp.zeros((8, dc), F32)
            ubuf[0:HALO, :] = jnp.zeros((HALO, dc), F32)

        xv = x_ref[...]
        r = lax.rsqrt(jnp.mean(xv * xv, axis=-1, keepdims=True) + EPS)
        h = (xv * r * g_ref[...]).astype(BF16)
        v = _nt(h, win[0:dc, :])
        gb = _nt(h, win[dc:2 * dc, :])
        gc = _nt(h, win[2 * dc:3 * dc, :])
        u = _nt(h, win[3 * dc:4 * dc, :])
        proj_ref[:, 0:dc] = v.astype(BF16)
        proj_ref[:, dc:2 * dc] = gb.astype(BF16)
        proj_ref[:, 2 * dc:3 * dc] = gc.astype(BF16)
        proj_ref[:, 3 * dc:4 * dc] = u.astype(BF16)

        z = gc * v
        zbuf[8:8 + tm, :] = z
        cw = cw_ref[...]
        conv = cw[2:3, :] * z + cw[1:2, :] * zbuf[7:7 + tm, :] + cw[0:1, :] * zbuf[6:6 + tm, :]
        y_ref[:, 0:dc] = (gb * conv).astype(BF16)

        ubuf[HALO:HALO + tm, :] = u
        row = i * tm + lax.broadcasted_iota(jnp.int32, (tm, 1), 0)
        for gi, w in enumerate(POOL_WINDOWS):
            cols = slice(gi * gcw, (gi + 1) * gcw)
            pooled, _ = _pool_parts(u[:, cols], ubuf, cols, w, row, tm, levels)
            yb = _nn(pooled.astype(BF16), pw_ref[gi].astype(BF16)) * ps_ref[:, cols]
            y_ref[:, dc + gi * gcw:dc + (gi + 1) * gcw] = yb.astype(BF16)

        xo_ref[...] = xv + _nn(y_ref[...], wout[...])
        zbuf[0:8, :] = zbuf[tm:tm + 8, :]
        ubuf[0:HALO, :] = ubuf[tm:tm + HALO, :]

    tok = lambda i: (i, 0)
    one = lambda i: (0, 0)
    return _launch(
        body, name="mixer_forward", grid=(t // tm,),
        in_specs=[pl.BlockSpec((tm, d), tok), pl.BlockSpec((1, d), one), HBM_SPEC, HBM_SPEC,
                  pl.BlockSpec(conv_w.shape, one), pl.BlockSpec(pool_w.shape, lambda i: (0, 0, 0)), pl.BlockSpec((1, dc), one)],
        out_specs=[pl.BlockSpec((tm, d), tok), pl.BlockSpec((tm, 4 * dc), tok), pl.BlockSpec((tm, 2 * dc), tok)],
        out_shape=[jax.ShapeDtypeStruct((t, d), F32), jax.ShapeDtypeStruct((t, 4 * dc), BF16), jax.ShapeDtypeStruct((t, 2 * dc), BF16)],
        scratch_shapes=[pltpu.VMEM((4 * dc, d), BF16), pltpu.VMEM((2 * dc, d), BF16),
                        pltpu.VMEM((tm + 8, dc), F32), pltpu.VMEM((tm + HALO, dc), F32),
                        pltpu.VMEM((WINDOW_LEVELS, tm + HALO, gcw), F32), pltpu.SemaphoreType.DMA((1 + N_CHIPS,))],
        args=(x, g, win_t, wout_x, conv_w, pool_w, pool_scale), cargo=cargo)


def _mixer_backward(dxo, x, g, proj, win_t, wout_x, conv_w, pool_w, pool_scale, cargo=()):
    t, d = x.shape
    dc = win_t.shape[0] // 4
    ng = len(POOL_WINDOWS)
    gcw = dc // ng
    wo_rows = d // N_CHIPS
    wo_stride = wout_x.shape[0] // N_CHIPS
    tm = min(TM_MIX, t)
    n_tiles = t // tm
    hb = tm // HALO

    def body(dxo_ref, x_ref, g_ref, proj_ref, halo_ref, win_hbm, wout_hbm, cw_ref, pw_ref, ps_ref,
             dx_ref, dproj_ref, h_ref, dxob_ref, dg_ref, dcw_ref, dps_ref, dpw_ref,
             win, wout, zbuf, ubuf, dcbuf, ebuf, levels, sems):
        i = pl.program_id(0)
        tile = n_tiles - 1 - i

        @pl.when(i == 0)
        def _():
            pairs = [(win_hbm, win)]
            for k in range(N_CHIPS):
                pairs.append((wout_hbm.at[pl.ds(k * wo_stride, wo_rows), :], wout.at[pl.ds(k * wo_rows, wo_rows), :]))
            _load_rows(pairs, sems)
            dcbuf[tm:tm + 8, :] = jnp.zeros((8, dc), F32)
            ebuf[tm:tm + HALO, :] = jnp.zeros((HALO, dc), F32)
            dg_ref[...] = jnp.zeros_like(dg_ref)
            dcw_ref[...] = jnp.zeros_like(dcw_ref)
            dps_ref[...] = jnp.zeros_like(dps_ref)
            dpw_ref[...] = jnp.zeros_like(dpw_ref)

        xv = x_ref[...]
        gv = g_ref[...]
        r = lax.rsqrt(jnp.mean(xv * xv, axis=-1, keepdims=True) + EPS)
        xhat = xv * r
        h_ref[...] = (xhat * gv).astype(BF16)
        dxo_v = dxo_ref[...]
        dxo_b = dxo_v.astype(BF16)
        dxob_ref[...] = dxo_b

        v = proj_ref[:, 0:dc].astype(F32)
        gb = proj_ref[:, dc:2 * dc].astype(F32)
        gc = proj_ref[:, 2 * dc:3 * dc].astype(F32)
        u = proj_ref[:, 3 * dc:4 * dc].astype(F32)
        first = jnp.where(tile > 0, 1.0, 0.0)
        zbuf[0:HALO, :] = halo_ref[:, 2 * dc:3 * dc].astype(F32) * halo_ref[:, 0:dc].astype(F32) * first
        ubuf[0:HALO, :] = halo_ref[:, 3 * dc:4 * dc].astype(F32) * first
        z = gc * v
        zbuf[HALO:HALO + tm, :] = z
        ubuf[HALO:HALO + tm, :] = u
        z1 = zbuf[HALO - 1:HALO - 1 + tm, :]
        z2 = zbuf[HALO - 2:HALO - 2 + tm, :]
        cw = cw_ref[...]
        conv = cw[2:3, :] * z + cw[1:2, :] * z1 + cw[0:1, :] * z2

        dy = _nt(dxo_b, wout[...])
        dya = dy[:, 0:dc]
        dgb = dya * conv
        dconv = dya * gb
        dcbuf[0:tm, :] = dconv
        dz = cw[2:3, :] * dconv + cw[1:2, :] * dcbuf[1:1 + tm, :] + cw[0:1, :] * dcbuf[2:2 + tm, :]
        dgc = dz * v
        dv = dz * gc
        dcw_ref[0:1, :] += jnp.sum(dconv * z2, axis=0, keepdims=True)
        dcw_ref[1:2, :] += jnp.sum(dconv * z1, axis=0, keepdims=True)
        dcw_ref[2:3, :] += jnp.sum(dconv * z, axis=0, keepdims=True)

        dproj_ref[:, 0:dc] = dv.astype(BF16)
        dproj_ref[:, dc:2 * dc] = dgb.astype(BF16)
        dproj_ref[:, 2 * dc:3 * dc] = dgc.astype(BF16)

        row = tile * tm + lax.broadcasted_iota(jnp.int32, (tm, 1), 0)
        for gi, w in enumerate(POOL_WINDOWS):
            cols = slice(gi * gcw, (gi + 1) * gcw)
            pooled, inv_cnt = _pool_parts(u[:, cols], ubuf, cols, w, row, tm, levels)
            pooled_b = pooled.astype(BF16)
            pw_b = pw_ref[gi].astype(BF16)
            dyb = dy[:, dc + gi * gcw:dc + (gi + 1) * gcw]
            q = _nn(pooled_b, pw_b)
            dps_ref[:, cols] += jnp.sum(q * dyb, axis=0, keepdims=True)
            dq = (dyb * ps_ref[:, cols]).astype(BF16)
            dpw_ref[gi] += _tn(pooled_b, dq)
            dpooled = _nt(dq, pw_b)
            ebuf[0:tm, cols] = dpooled * inv_cnt
            du = _window_sums(ebuf, cols, w, tm, levels, trailing=False) - dpooled
            dproj_ref[:, 3 * dc + gi * gcw:3 * dc + (gi + 1) * gcw] = du.astype(BF16)

        dh = _nn(dproj_ref[...], win[...])
        dg_ref[...] += jnp.sum(dh * xhat, axis=0, keepdims=True)
        dxh = dh * gv
        dx_ref[...] = dxo_v + r * (dxh - xhat * jnp.mean(dxh * xhat, axis=-1, keepdims=True))
        dcbuf[tm:tm + 8, :] = dcbuf[0:8, :]
        ebuf[tm:tm + HALO, :] = ebuf[0:HALO, :]

    tok = lambda i: (n_tiles - 1 - i, 0)
    halo = lambda i: (jnp.maximum((n_tiles - 1 - i) * hb - 1, 0), 0)
    one = lambda i: (0, 0)
    return _launch(
        body, name="mixer_backward", grid=(n_tiles,),
        in_specs=[pl.BlockSpec((tm, d), tok), pl.BlockSpec((tm, d), tok), pl.BlockSpec((1, d), one),
                  pl.BlockSpec((tm, 4 * dc), tok), pl.BlockSpec((HALO, 4 * dc), halo), HBM_SPEC, HBM_SPEC,
                  pl.BlockSpec(conv_w.shape, one), pl.BlockSpec(pool_w.shape, lambda i: (0, 0, 0)), pl.BlockSpec((1, dc), one)],
        out_specs=[pl.BlockSpec((tm, d), tok), pl.BlockSpec((tm, 4 * dc), tok), pl.BlockSpec((tm, d), tok), pl.BlockSpec((tm, d), tok),
                   pl.BlockSpec((1, d), one), pl.BlockSpec(conv_w.shape, one), pl.BlockSpec((1, dc), one),
                   pl.BlockSpec(pool_w.shape, lambda i: (0, 0, 0))],
        out_shape=[jax.ShapeDtypeStruct((t, d), F32), jax.ShapeDtypeStruct((t, 4 * dc), BF16), jax.ShapeDtypeStruct((t, d), BF16),
                   jax.ShapeDtypeStruct((t, d), BF16), jax.ShapeDtypeStruct((1, d), F32), jax.ShapeDtypeStruct(conv_w.shape, F32),
                   jax.ShapeDtypeStruct((1, dc), F32), jax.ShapeDtypeStruct(pool_w.shape, F32)],
        scratch_shapes=[pltpu.VMEM((4 * dc, d), BF16), pltpu.VMEM((2 * dc, d), BF16),
                        pltpu.VMEM((tm + HALO, dc), F32), pltpu.VMEM((tm + HALO, dc), F32),
                        pltpu.VMEM((tm + 8, dc), F32), pltpu.VMEM((tm + HALO, dc), F32),
                        pltpu.VMEM((WINDOW_LEVELS, tm + HALO, gcw), F32), pltpu.SemaphoreType.DMA((1 + N_CHIPS,))],
        args=(dxo, x, g, proj, proj, win_t, wout_x, conv_w, pool_w, pool_scale), cargo=cargo)


def _adam_update(w, gv, m, v):
    m_new = ADAM_B1 * m + (1.0 - ADAM_B1) * gv
    v_new = ADAM_B2 * v + (1.0 - ADAM_B2) * (gv * gv)
    m_hat = m_new / (1.0 - ADAM_B1 ** ADAM_STEP)
    v_hat = v_new / (1.0 - ADAM_B2 ** ADAM_STEP)
    return -ADAM_LR * (m_hat / (jnp.sqrt(v_hat) + ADAM_EPS) + ADAM_WD * w), m_new, v_new


def _adamw(w, grad, m, v, name):
    rows, cols = w.shape
    br = _row_block(rows, 512) if rows >= 8 else rows

    def body(w_ref, g_ref, m_ref, v_ref, go_ref, d_ref, mo_ref, vo_ref):
        gv = g_ref[...]
        go_ref[...] = gv
        d_ref[...], mo_ref[...], vo_ref[...] = _adam_update(w_ref[...], gv, m_ref[...], v_ref[...])

    blk = pl.BlockSpec((br, cols), lambda i: (i, 0))
    return pl.pallas_call(
        body, name=name,
        out_shape=[jax.ShapeDtypeStruct((rows, cols), F32)] * 4,
        grid=(rows // br,), in_specs=[blk] * 4, out_specs=[blk] * 4,
        compiler_params=pltpu.CompilerParams(dimension_semantics=("parallel",), vmem_limit_bytes=VMEM_LIMIT),
    )(w, grad, m, v)


def _adamw_transposed(w, grad_t, m, v, name):
    _, rows, cols = w.shape
    br = 256 if rows % 256 == 0 else rows

    def body(w_ref, gt_ref, m_ref, v_ref, g_ref, d_ref, mo_ref, vo_ref):
        gv = gt_ref[...].T
        g_ref[...] = gv
        d_ref[...], mo_ref[...], vo_ref[...] = _adam_update(w_ref[...], gv, m_ref[...], v_ref[...])

    blk = pl.BlockSpec((None, br, cols), lambda i: (0, i, 0))
    return pl.pallas_call(
        body, name=name,
        out_shape=[jax.ShapeDtypeStruct((1, rows, cols), F32)] * 4,
        grid=(rows // br,), in_specs=[blk, pl.BlockSpec((cols, br), lambda i: (0, i)), blk, blk], out_specs=[blk] * 4,
        compiler_params=pltpu.CompilerParams(dimension_semantics=("parallel",)),
    )(w, grad_t, m, v)


def _f32_rows_as_bf16(a, rows, cols):
    bits = lax.bitcast_convert_type(a, BF16).reshape(a.shape[0], 2 * a.shape[1])
    return jnp.pad(bits, ((0, rows - bits.shape[0]), (0, cols - bits.shape[1])))


def kernel(x, norm_ffn1, ffn1_w_gate, ffn1_w_up, ffn1_w_down, norm_mix, w_in, conv_w, pool_w, pool_scale, w_out, norm_ffn2, ffn2_w_gate, ffn2_w_up, ffn2_w_down, norm_final, loss_target, m_norm_ffn1, m_ffn1_w_gate, m_ffn1_w_up, m_ffn1_w_down, m_norm_mix, m_w_in, m_conv_w, m_pool_w, m_pool_scale, m_w_out, m_norm_ffn2, m_ffn2_w_gate, m_ffn2_w_up, m_ffn2_w_down, m_norm_final, v_norm_ffn1, v_ffn1_w_gate, v_ffn1_w_up, v_ffn1_w_down, v_norm_mix, v_w_in, v_conv_w, v_pool_w, v_pool_scale, v_w_out, v_norm_ffn2, v_ffn2_w_gate, v_ffn2_w_up, v_ffn2_w_down, v_norm_final):
    weights = dict(norm_ffn1=norm_ffn1, ffn1_w_gate=ffn1_w_gate, ffn1_w_up=ffn1_w_up, ffn1_w_down=ffn1_w_down, norm_mix=norm_mix,
                   w_in=w_in, conv_w=conv_w, pool_w=pool_w, pool_scale=pool_scale, w_out=w_out, norm_ffn2=norm_ffn2,
                   ffn2_w_gate=ffn2_w_gate, ffn2_w_up=ffn2_w_up, ffn2_w_down=ffn2_w_down, norm_final=norm_final)
    first_m = dict(norm_ffn1=m_norm_ffn1, ffn1_w_gate=m_ffn1_w_gate, ffn1_w_up=m_ffn1_w_up, ffn1_w_down=m_ffn1_w_down,
                   norm_mix=m_norm_mix, w_in=m_w_in, conv_w=m_conv_w, pool_w=m_pool_w, pool_scale=m_pool_scale, w_out=m_w_out,
                   norm_ffn2=m_norm_ffn2, ffn2_w_gate=m_ffn2_w_gate, ffn2_w_up=m_ffn2_w_up, ffn2_w_down=m_ffn2_w_down,
                   norm_final=m_norm_final)
    second_m = dict(norm_ffn1=v_norm_ffn1, ffn1_w_gate=v_ffn1_w_gate, ffn1_w_up=v_ffn1_w_up, ffn1_w_down=v_ffn1_w_down,
                    norm_mix=v_norm_mix, w_in=v_w_in, conv_w=v_conv_w, pool_w=v_pool_w, pool_scale=v_pool_scale, w_out=v_w_out,
                    norm_ffn2=v_norm_ffn2, ffn2_w_gate=v_ffn2_w_gate, ffn2_w_up=v_ffn2_w_up, ffn2_w_down=v_ffn2_w_down,
                    norm_final=v_norm_final)
    names = list(weights)

    xs = x[0]
    tgt = loss_target[0]
    t, d = xs.shape
    dc = pool_scale.shape[1]
    cx, cy, cc = _my_place()
    chip = 2 * cx + cy
    place = jnp.stack([chip, cc]).astype(jnp.int32)

    conv_rows = 32
    wout_x = jnp.concatenate([w_out[0].astype(BF16), _f32_rows_as_bf16(conv_w[0], conv_rows, d)], axis=0)

    g1, gm, g2 = norm_ffn1, norm_mix, norm_ffn2
    gf = norm_final.reshape(1, d)
    pw = pool_w[0]

    (wd1_shard, wg2_shard, wu2_shard, wd2_shard), [(wg1, wu1)] = _cast_to_bf16(
        [ffn1_w_down[0], ffn2_w_gate[0].T, ffn2_w_up[0].T, ffn2_w_down[0]], "gather_ffn1",
        [_gather_cargo([ffn1_w_gate[0].T.astype(BF16), ffn1_w_up[0].T.astype(BF16)])])
    (a1, b1, s1), [(wd1, win_t, wout_g)] = _ffn_up(
        xs, g1, wg1, wu1, "ffn1_up", [_gather_cargo([wd1_shard, w_in[0].T.astype(BF16), wout_x])])
    (x1,), [(wg2,)] = _ffn_down(xs, s1, wd1, "ffn1_down", [_gather_cargo([wg2_shard])])
    wo_rows = w_out.shape[1]
    cshard = conv_w.shape[2]
    conv_bits = wout_g.reshape(N_CHIPS, wo_rows + conv_rows, d)[:, wo_rows:wo_rows + conv_w.shape[1], :2 * cshard]
    conv_full = lax.bitcast_convert_type(conv_bits.reshape(N_CHIPS, conv_w.shape[1], cshard, 2), F32)
    conv_full = jnp.transpose(conv_full, (1, 0, 2)).reshape(conv_w.shape[1], N_CHIPS * cshard)
    (x2, proj, ymix), [(wu2,)] = _mixer_forward(x1, gm, win_t, wout_g, conv_full, pw, pool_scale, [_gather_cargo([wu2_shard])])
    (a2, b2, s2), [(wd2,)] = _ffn_up(x2, g2, wg2, wu2, "ffn2_up", [_gather_cargo([wd2_shard])])
    (dx3, sq_cols, dgf), _ = _ffn_down(x2, s2, wd2, "ffn2_down", loss_head=(gf, tgt))

    (dx2, dab2, hd2, dg2), _ = _ffn_backward(dx3, x2, g2, a2, b2, wg2, wu2, wd2, "ffn2_backward")
    p_wg2, _ = _weight_grad(dab2, hd2, "ffn2_gate_grad", lhs_part=(0, 2), rhs_part=(0, 2))
    p_wu2, [(x_wg2,)] = _weight_grad(dab2, hd2, "ffn2_up_grad", [_exchange_cargo([p_wg2])], lhs_part=(1, 2), rhs_part=(0, 2))
    p_wd2, [(x_wu2,)] = _weight_grad(s2, hd2, "ffn2_down_grad", [_exchange_cargo([p_wu2])], rhs_part=(1, 2))

    (dx1, dproj, h2, dx2b, dgm, dcw, dps, dpw), [(x_wd2,)] = _mixer_backward(
        dx2, x1, gm, proj, win_t, wout_g, conv_full, pw, pool_scale, [_exchange_cargo([p_wd2])])

    (dx0, dab1, hd1, dg1), _ = _ffn_backward(dx1, xs, g1, a1, b1, wg1, wu1, wd1, "ffn1_backward")

    npw = pw.size // d
    head = [dg1, dgm, dg2, dgf, jnp.pad(dps, ((0, 0), (0, d - dc))), jnp.pad(dcw, ((0, 0), (0, d - dc))), sq_cols]
    n_head = sum(h.shape[0] for h in head)
    base = -(-n_head // 8) * 8
    pack = jnp.concatenate(head + [jnp.zeros((base - n_head, d), F32), dpw.reshape(npw, d)], axis=0)

    p_wg1, [(packs,)] = _weight_grad(dab1, hd1, "ffn1_gate_grad", [_all_gather_small_cargo(pack)], lhs_part=(0, 2), rhs_part=(0, 2))
    p_wu1, [(x_wg1,)] = _weight_grad(dab1, hd1, "ffn1_up_grad", [_exchange_cargo([p_wg1])], lhs_part=(1, 2), rhs_part=(0, 2))
    p_wd1, [(x_wu1,)] = _weight_grad(s1, hd1, "ffn1_down_grad", [_exchange_cargo([p_wu1])], rhs_part=(1, 2))
    p_win, [(x_wd1,)] = _weight_grad(dproj, h2, "w_in_grad", [_exchange_cargo([p_wd1])])
    p_wout, [(x_win,)] = _weight_grad(ymix, dx2b, "w_out_grad", [_exchange_cargo([p_win])])
    x_wout, = _run_cargo(_exchange_cargo([p_wout]), "grad_exchange_last")
    small = _sum_by_device(packs)
    loss = jnp.sum(small[n_head - 1]) * (0.5 / d)

    order = ["wg1", "wu1", "wd1", "win", "wout", "wg2", "wu2", "wd2"]
    pairs = dict(wg1=p_wg1, wu1=p_wu1, wd1=p_wd1, win=p_win, wout=p_wout, wg2=p_wg2, wu2=p_wu2, wd2=p_wd2)
    landed = dict(wg1=x_wg1, wu1=x_wu1, wd1=x_wd1, win=x_win, wout=x_wout, wg2=x_wg2, wu2=x_wu2, wd2=x_wd2)
    both = _sibling_share([_chip_sum(pairs[k], landed[k], place, k) for k in order])
    rwg1, rwu1, rwd1, rwin, rwout, rwg2, rwu2, rwd2 = [b.reshape(2 * b.shape[1], b.shape[2]) for b in both]

    grads = {
        "norm_ffn1": small[0:1], "norm_mix": small[1:2], "norm_ffn2": small[2:3], "norm_final": small[3],
        "pool_scale": small[4:5, :dc],
        "conv_w": lax.dynamic_slice_in_dim(small[5:5 + dcw.shape[0], :dc], chip * cshard, cshard, axis=1)[None],
        "pool_w": small[base:].reshape(pool_w.shape),
        "ffn1_w_down": rwd1[None], "w_out": rwout[None], "ffn2_w_down": rwd2[None],
    }
    by_view = {"ffn1_w_gate": rwg1, "ffn1_w_up": rwu1, "ffn2_w_gate": rwg2, "ffn2_w_up": rwu2}

    deltas, new_m, new_v = {}, {}, {}
    for n in names:
        w = weights[n]
        shape = w.shape
        if n == "w_in":
            grads[n], deltas[n], new_m[n], new_v[n] = _adamw_transposed(w, rwin, first_m[n], second_m[n], "adamw_" + n)
            continue
        if n in by_view:
            view = lambda a: jnp.swapaxes(a, 1, 2)[0]
            back = lambda a: jnp.swapaxes(a[None], 1, 2)
            outs = _adamw(view(w), by_view[n], view(first_m[n]), view(second_m[n]), "adamw_" + n)
            grads[n], deltas[n], new_m[n], new_v[n] = [back(o) for o in outs]
            continue
        as2d = (lambda a: a.reshape(-1, shape[-1]))
        outs = _adamw(as2d(w), as2d(grads[n]), as2d(first_m[n]), as2d(second_m[n]), "adamw_" + n)
        grads[n], deltas[n], new_m[n], new_v[n] = [o.reshape(shape) for o in outs]

    return (loss, dx0[None], *[grads[n] for n in names], *[deltas[n] for n in names],
            *[new_m[n] for n in names], *[new_v[n] for n in names])
```

```python
import jax
import jax.numpy as jnp
from jax import lax
from jax.experimental import pallas as pl
from jax.experimental.pallas import tpu as pltpu

F32 = jnp.float32
BF16 = jnp.bfloat16
MESH = pl.DeviceIdType.MESH

EPS = 1e-6
POOL_WINDOWS = (2, 4, 8, 16)
ADAM_LR = 0.001
ADAM_B1 = 0.9
ADAM_B2 = 0.999
ADAM_EPS = 1e-08
ADAM_WD = 0.01
ADAM_STEP = 10

N_CHIPS = 4
N_DEVICES = 8
MXU_COLS_V7X = 256
VMEM_LIMIT = 56 * 1024 * 1024
TM_FFN = 512
TM_MIX = 512
TM_TN = 1024
HALO = 32
WINDOW_LEVELS = 3
FFN_FWD_CHUNKS = 2
FFN_BWD_CHUNKS = 2


def _nt(a, b):
    return lax.dot_general(a, b, (((1,), (1,)), ((), ())), preferred_element_type=F32)


def _tn(a, b):
    return lax.dot_general(a, b, (((0,), (0,)), ((), ())), preferred_element_type=F32)


def _nn(a, b):
    return jnp.dot(a, b, preferred_element_type=F32)


def _sigmoid(a):
    return 1.0 / (1.0 + jnp.exp(-a))


def _feature_chunks(n, parts):
    assert n % MXU_COLS_V7X == 0
    tiles = n // MXU_COLS_V7X
    out, s0 = [], 0
    for p in range(parts):
        sz = (tiles // parts + (1 if p < tiles % parts else 0)) * MXU_COLS_V7X
        if sz:
            out.append((s0, sz))
            s0 += sz
    return out


def _row_block(rows, cap):
    best = 8
    for b in range(8, min(rows, cap) + 1, 8):
        if rows % b == 0:
            best = b
    assert rows % best == 0
    return best


def _my_place():
    return lax.axis_index("x"), lax.axis_index("y"), lax.axis_index("c")


def _other_chips(x, y):
    return [(1 - x, y), (x, 1 - y), (1 - x, 1 - y)]


HBM_SPEC = pl.BlockSpec(memory_space=pltpu.HBM)


class _Cargo:
    def __init__(self, operands, out_shapes, n_sems, phases, when):
        self.operands, self.out_shapes, self.n_sems = list(operands), list(out_shapes), n_sems
        self.phases, self.when = list(phases), list(when)
        assert len(self.phases) == len(self.when) and self.when[0] == 0.0 and self.when[-1] == 1.0


def _launch(body, *, name, grid, in_specs, out_specs, out_shape, scratch_shapes, args, cargo=()):
    params = pltpu.CompilerParams(dimension_semantics=("arbitrary",) * len(grid), vmem_limit_bytes=VMEM_LIMIT)
    cargos = list(cargo)
    c_operands = [op for cg in cargos for op in cg.operands]
    c_shapes = [sh for cg in cargos for sh in cg.out_shapes]
    counts = [len(in_specs), len(c_operands), len(out_shape), len(c_shapes), len(scratch_shapes), 2 * len(cargos)]

    def carrying(*refs):
        groups, pos = [], 0
        for k in counts:
            groups.append(refs[pos:pos + k])
            pos += k
        ins, c_ins, outs, c_outs, scratch, sems = groups
        parts, pi, po = [], 0, 0
        for n, cg in enumerate(cargos):
            parts.append((c_ins[pi:pi + len(cg.operands)], c_outs[po:po + len(cg.out_shapes)], sems[2 * n], sems[2 * n + 1]))
            pi += len(cg.operands)
            po += len(cg.out_shapes)
        step, steps = 0, 1
        for ax, g in enumerate(grid):
            step = step * g + pl.program_id(ax)
            steps *= g
        todo = {}
        for cg, part in zip(cargos, parts):
            for phase, frac in zip(cg.phases[:-1], cg.when[:-1]):
                todo.setdefault(int(round(frac * (steps - 1))), []).append((phase, part))

        for at in sorted(todo):
            @pl.when(step == at)
            def _(at=at):
                for phase, part in todo[at]:
                    phase(*part)

        body(*ins, *outs, *scratch)

        if cargos:
            @pl.when(step == steps - 1)
            def _():
                for cg, part in zip(cargos, parts):
                    cg.phases[-1](*part)

    sems = [pltpu.SemaphoreType.DMA((cg.n_sems,)) for cg in cargos for _ in range(2)]
    outs = pl.pallas_call(
        carrying, name=name, grid=grid,
        in_specs=list(in_specs) + [HBM_SPEC] * counts[1], out_specs=list(out_specs) + [HBM_SPEC] * counts[3],
        out_shape=list(out_shape) + c_shapes, scratch_shapes=list(scratch_shapes) + sems,
        compiler_params=params)(*args, *c_operands)
    own, rest = list(outs[:counts[2]]), list(outs[counts[2]:])
    carried, po = [], 0
    for cg in cargos:
        carried.append(rest[po:po + len(cg.out_shapes)])
        po += len(cg.out_shapes)
    return own, carried


def _run_cargo(cargo, name):
    n_in, n_out = len(cargo.operands), len(cargo.out_shapes)

    def body(*refs):
        c_ins, c_outs, sems = refs[:n_in], refs[n_in:n_in + n_out], refs[n_in + n_out:]
        for phase in cargo.phases:
            phase(c_ins, c_outs, *sems)

    sem = pltpu.SemaphoreType.DMA((cargo.n_sems,))
    return list(pl.pallas_call(body, name=name, out_shape=cargo.out_shapes, in_specs=[HBM_SPEC] * n_in,
                               out_specs=[HBM_SPEC] * n_out, scratch_shapes=[sem, sem])(*cargo.operands))


def _gather_cargo(shards, relay_at=(0.6, 0.85)):
    n = len(shards)
    for s in shards:
        assert s.shape[0] % 32 == 0
    slots = 8

    def steps(ins, outs, send_sems, recv_sems):
        x, y, c = _my_place()
        sibling = (x, y, 1 - c)
        over_x, over_y = (1 - x, y, c), (x, 1 - y, c)
        mine, chip_x, chip_y, chip_d = 2 * x + y, 2 * (1 - x) + y, 2 * x + (1 - y), 2 * (1 - x) + (1 - y)

        def rows_of(a, chip_index, half, part=None):
            rps = shards[a].shape[0]
            hr = rps // 2
            first = -(-hr // 32) * 16
            offset, size = {None: (0, hr), 0: (0, first), 1: (first, hr - first)}[part]
            return outs[a].at[pl.ds(pl.multiple_of(chip_index * rps + half * hr + offset, 16), size), :]

        def remote(a, slot, src, dst, to):
            return pltpu.make_async_remote_copy(
                src_ref=src, dst_ref=dst, send_sem=send_sems.at[a * slots + slot], recv_sem=recv_sems.at[a * slots + slot],
                device_id=to, device_id_type=MESH)

        def same_rows(a, slot, rows, to):
            return remote(a, slot, rows, rows, to)

        def own_copy(a):
            rps = shards[a].shape[0]
            return remote(a, 7, ins[a], outs[a].at[pl.ds(pl.multiple_of(mine * rps, 16), rps), :], sibling)

        def my_half(a):
            hr = shards[a].shape[0] // 2
            return ins[a].at[pl.ds(pl.multiple_of(c * hr, 16), hr), :]

        def start():
            for a in range(n):
                own_copy(a).start()
                remote(a, 0, my_half(a), rows_of(a, mine, c), over_x).start()
                remote(a, 1, my_half(a), rows_of(a, mine, c), over_y).start()

        def relay_neighbours():
            for a in range(n):
                same_rows(a, 0, rows_of(a, chip_x, c), over_x).wait_recv()
                same_rows(a, 4, rows_of(a, chip_x, c), sibling).start()
                same_rows(a, 2, rows_of(a, chip_x, c, 0), over_y).start()
                same_rows(a, 1, rows_of(a, chip_y, c), over_y).wait_recv()
                same_rows(a, 5, rows_of(a, chip_y, c), sibling).start()
                same_rows(a, 3, rows_of(a, chip_y, c, 1), over_x).start()

        def relay_diagonal():
            for a in range(n):
                same_rows(a, 2, rows_of(a, chip_d, c, 0), over_y).wait_recv()
                same_rows(a, 3, rows_of(a, chip_d, c, 1), over_x).wait_recv()
                same_rows(a, 6, rows_of(a, chip_d, c), sibling).start()

        def finish():
            for a in range(n):
                for slot, chip_index in ((4, chip_x), (5, chip_y), (6, chip_d)):
                    same_rows(a, slot, rows_of(a, chip_index, 1 - c), sibling).wait_recv()
            for a in range(n):
                remote(a, 0, my_half(a), rows_of(a, mine, c), over_x).wait_send()
                remote(a, 1, my_half(a), rows_of(a, mine, c), over_y).wait_send()
                same_rows(a, 2, rows_of(a, chip_x, c, 0), over_y).wait_send()
                same_rows(a, 3, rows_of(a, chip_y, c, 1), over_x).wait_send()
                for slot, chip_index in ((4, chip_x), (5, chip_y), (6, chip_d)):
                    same_rows(a, slot, rows_of(a, chip_index, c), sibling).wait_send()
                own_copy(a).wait()

        return start, relay_neighbours, relay_diagonal, finish

    phases = [lambda *r, k=k: steps(*r)[k]() for k in range(4)]
    return _Cargo(shards, [jax.ShapeDtypeStruct((N_CHIPS * s.shape[0], s.shape[1]), s.dtype) for s in shards], slots * n,
                  phases, [0.0, relay_at[0], relay_at[1], 1.0])


def _exchange_cargo(pairs):
    n = len(pairs)

    def copies(ins, outs, send_sems, recv_sems):
        x, y, c = _my_place()
        return [pltpu.make_async_remote_copy(
            src_ref=ins[a].at[2 * chip[0] + chip[1]], dst_ref=outs[a].at[j],
            send_sem=send_sems.at[3 * a + j], recv_sem=recv_sems.at[3 * a + j], device_id=(*chip, c), device_id_type=MESH)
            for a in range(n) for j, chip in enumerate(_other_chips(x, y))]

    def start(*r):
        for cp in copies(*r):
            cp.start()

    def finish(*r):
        for cp in copies(*r):
            cp.wait()

    return _Cargo(pairs, [jax.ShapeDtypeStruct((3,) + p.shape[1:], p.dtype) for p in pairs], 3 * n, [start, finish], [0.0, 1.0])


def _all_gather_small_cargo(pack):
    rows, cols = pack.shape

    def copies(ins, outs, send_sems, recv_sems):
        x, y, c = _my_place()
        me = 4 * x + 2 * y + c
        remote = []
        for f in range(1, N_DEVICES):
            fx, fy, fc = (f >> 2) & 1, (f >> 1) & 1, f & 1
            to = (1 - x if fx else x, 1 - y if fy else y, 1 - c if fc else c)
            remote.append(pltpu.make_async_remote_copy(
                src_ref=ins[0], dst_ref=outs[0].at[me], send_sem=send_sems.at[f - 1], recv_sem=recv_sems.at[f - 1],
                device_id=to, device_id_type=MESH))
        own = pltpu.make_async_copy(ins[0], outs[0].at[me], send_sems.at[N_DEVICES - 1])
        return remote, own

    def start(*r):
        remote, own = copies(*r)
        own.start()
        for cp in remote:
            cp.start()

    def finish(*r):
        remote, own = copies(*r)
        for cp in remote:
            cp.wait()
        own.wait()

    return _Cargo([pack], [jax.ShapeDtypeStruct((N_DEVICES, rows, cols), F32)], N_DEVICES, [start, finish], [0.0, 1.0])


def _sum_by_device(packs):
    n, rows, cols = packs.shape

    def body(p_ref, o_ref):
        acc = p_ref[0]
        for dev in range(1, n):
            acc = acc + p_ref[dev]
        o_ref[...] = acc

    return pl.pallas_call(body, name="small_grads_sum", out_shape=jax.ShapeDtypeStruct((rows, cols), F32))(packs)


def _chip_sum(pair, got, place, tag):
    _, hr, cols = pair.shape
    br = _row_block(hr, 256)

    def body(k_ref, p_ref, r_ref, o_ref):
        acc = p_ref[...].astype(F32)
        for j in range(3):
            acc = acc + r_ref[j].astype(F32)
        o_ref[...] = acc

    return pl.pallas_call(
        body, name="grad_chip_sum_" + tag,
        out_shape=jax.ShapeDtypeStruct((2, hr, cols), F32),
        grid_spec=pltpu.PrefetchScalarGridSpec(
            num_scalar_prefetch=1, grid=(hr // br,),
            in_specs=[pl.BlockSpec((None, br, cols), lambda r, k_ref: (k_ref[0], r, 0)),
                      pl.BlockSpec((3, br, cols), lambda r, k_ref: (0, r, 0))],
            out_specs=pl.BlockSpec((None, br, cols), lambda r, k_ref: (k_ref[1], r, 0))),
        compiler_params=pltpu.CompilerParams(dimension_semantics=("parallel",)),
    )(place, pair, got)


def _sibling_share(halves):
    n = len(halves)

    def body(*refs):
        outs = refs[n:2 * n]
        send_sems, recv_sems = refs[2 * n:]
        x, y, c = _my_place()
        copies = []
        for a in range(n):
            cp = pltpu.make_async_remote_copy(
                src_ref=outs[a].at[c], dst_ref=outs[a].at[c], send_sem=send_sems.at[a], recv_sem=recv_sems.at[a],
                device_id=(x, y, 1 - c), device_id_type=MESH)
            cp.start()
            copies.append(cp)
        for cp in copies:
            cp.wait()

    return pl.pallas_call(
        body, name="grad_share_sibling",
        out_shape=[jax.ShapeDtypeStruct(h.shape, h.dtype) for h in halves],
        in_specs=[HBM_SPEC] * n, out_specs=[HBM_SPEC] * n,
        input_output_aliases={a: a for a in range(n)},
        scratch_shapes=[pltpu.SemaphoreType.DMA((n,)), pltpu.SemaphoreType.DMA((n,))],
    )(*halves)


def _load_rows(pairs, sems):
    cps = [pltpu.make_async_copy(src, dst, sems.at[j]) for j, (src, dst) in enumerate(pairs)]
    for cp in cps:
        cp.start()
    for cp in cps:
        cp.wait()


def _piece_rows(weights):
    return list(weights), (lambda refs, mats: list(zip(refs, mats))), len(weights)


def _cast_to_bf16(arrays, name, cargo=()):
    rows, cols = arrays[0].shape
    n = len(arrays)
    br = _row_block(rows, 256)

    def body(*refs):
        for src, dst in zip(refs[:n], refs[n:]):
            dst[...] = src[...].astype(BF16)

    blk = pl.BlockSpec((br, cols), lambda i: (i, 0))
    return _launch(body, name=name, grid=(rows // br,), in_specs=[blk] * n, out_specs=[blk] * n,
                   out_shape=[jax.ShapeDtypeStruct((rows, cols), BF16)] * n, scratch_shapes=[], args=tuple(arrays), cargo=cargo)


def _loss_head(xv, gv, tv):
    d = xv.shape[-1]
    r = lax.rsqrt(jnp.mean(xv * xv, axis=-1, keepdims=True) + EPS)
    xhat = xv * r
    err = xhat * gv - tv
    dy = err * (1.0 / d)
    dxh = dy * gv
    dx = r * (dxh - xhat * jnp.mean(dxh * xhat, axis=-1, keepdims=True))
    return dx, jnp.sum(err * err, axis=0, keepdims=True), jnp.sum(dy * xhat, axis=0, keepdims=True)


def _ffn_up(x, g, wg_t, wu_t, name, cargo=()):
    t, d = x.shape
    f = wg_t.shape[0]
    tm = min(TM_FFN, t)
    chunks = _feature_chunks(f, FFN_FWD_CHUNKS)
    flat, copies, n_copies = _piece_rows([wg_t, wu_t])
    nw = len(flat)

    def body(x_ref, g_ref, *rest):
        w_hbm, (a_ref, b_ref, s_ref, wg, wu, sems) = rest[:nw], rest[nw:]

        @pl.when(pl.program_id(0) == 0)
        def _():
            _load_rows(copies(w_hbm, [wg, wu]), sems)

        xv = x_ref[...]
        r = lax.rsqrt(jnp.mean(xv * xv, axis=-1, keepdims=True) + EPS)
        h = (xv * r * g_ref[...]).astype(BF16)
        for s0, sz in chunks:
            a = _nt(h, wg[s0:s0 + sz, :])
            b = _nt(h, wu[s0:s0 + sz, :])
            a_ref[:, s0:s0 + sz] = a.astype(BF16)
            b_ref[:, s0:s0 + sz] = b.astype(BF16)
            s_ref[:, s0:s0 + sz] = (a * _sigmoid(a) * b).astype(BF16)

    tok = lambda i: (i, 0)
    wide = pl.BlockSpec((tm, f), tok)
    return _launch(
        body, name=name, grid=(t // tm,),
        in_specs=[pl.BlockSpec((tm, d), tok), pl.BlockSpec((1, d), lambda i: (0, 0))] + [HBM_SPEC] * nw,
        out_specs=[wide, wide, wide], out_shape=[jax.ShapeDtypeStruct((t, f), BF16)] * 3,
        scratch_shapes=[pltpu.VMEM((f, d), BF16), pltpu.VMEM((f, d), BF16), pltpu.SemaphoreType.DMA((n_copies,))],
        args=(x, g, *flat), cargo=cargo)


def _ffn_down(x, s, wd, name, cargo=(), loss_head=None):
    t, d = x.shape
    f = s.shape[1]
    tm = min(TM_FFN, t)
    flat, copies, n_copies = _piece_rows([wd])
    nw = len(flat)
    nl = 2 if loss_head else 0

    def body(x_ref, s_ref, *rest):
        head, w_hbm = rest[:nl], rest[nl:nl + nw]
        xo_ref = rest[nl + nw]
        sums, (wdn, sems) = rest[nl + nw + 1:nl + nw + 1 + nl], rest[nl + nw + 1 + nl:]

        @pl.when(pl.program_id(0) == 0)
        def _():
            _load_rows(copies(w_hbm, [wdn]), sems)
            for sum_ref in sums:
                sum_ref[...] = jnp.zeros_like(sum_ref)

        xo = x_ref[...] + 0.5 * _nn(s_ref[...], wdn[...])
        if loss_head:
            dx, sq, dgf = _loss_head(xo, head[0][...], head[1][...])
            xo_ref[...] = dx
            sums[0][...] += sq
            sums[1][...] += dgf
        else:
            xo_ref[...] = xo

    tok = lambda i: (i, 0)
    one = lambda i: (0, 0)
    return _launch(
        body, name=name, grid=(t // tm,),
        in_specs=[pl.BlockSpec((tm, d), tok), pl.BlockSpec((tm, f), tok)]
        + ([pl.BlockSpec((1, d), one), pl.BlockSpec((tm, d), tok)] if loss_head else []) + [HBM_SPEC] * nw,
        out_specs=[pl.BlockSpec((tm, d), tok)] + [pl.BlockSpec((1, d), one)] * nl,
        out_shape=[jax.ShapeDtypeStruct((t, d), F32)] + [jax.ShapeDtypeStruct((1, d), F32)] * nl,
        scratch_shapes=[pltpu.VMEM((f, d), BF16), pltpu.SemaphoreType.DMA((n_copies,))],
        args=(x, s, *(loss_head or ()), *flat), cargo=cargo)


def _ffn_backward(dxo, x, g, a, b, wg_t, wu_t, wd, name, cargo=()):
    t, d = x.shape
    f = wd.shape[0]
    tm = min(TM_FFN // 2, t)
    chunks = _feature_chunks(f, FFN_BWD_CHUNKS)
    flat, copies, n_copies = _piece_rows([wg_t, wu_t, wd])
    nw = len(flat)

    def body(dxo_ref, x_ref, g_ref, a_ref, b_ref, *rest):
        w_hbm, (dx_ref, dab_ref, hd_ref, dg_ref, wg, wu, wdn, sems) = rest[:nw], rest[nw:]

        @pl.when(pl.program_id(0) == 0)
        def _():
            _load_rows(copies(w_hbm, [wg, wu, wdn]), sems)
            dg_ref[...] = jnp.zeros_like(dg_ref)

        xv = x_ref[...]
        gv = g_ref[...]
        r = lax.rsqrt(jnp.mean(xv * xv, axis=-1, keepdims=True) + EPS)
        xhat = xv * r
        hd_ref[:, 0:d] = (xhat * gv).astype(BF16)
        dxo_v = dxo_ref[...]
        dout = (0.5 * dxo_v).astype(BF16)
        hd_ref[:, d:2 * d] = dout
        dh = jnp.zeros((tm, d), F32)
        for s0, sz in chunks:
            ds = _nt(dout, wdn[s0:s0 + sz, :])
            av = a_ref[:, s0:s0 + sz].astype(F32)
            bv = b_ref[:, s0:s0 + sz].astype(F32)
            sig = _sigmoid(av)
            silu = av * sig
            da = (ds * bv * (sig * (1.0 + av * (1.0 - sig)))).astype(BF16)
            db = (ds * silu).astype(BF16)
            dab_ref[:, s0:s0 + sz] = da
            dab_ref[:, f + s0:f + s0 + sz] = db
            dh = dh + _nn(da, wg[s0:s0 + sz, :]) + _nn(db, wu[s0:s0 + sz, :])
        dg_ref[...] += jnp.sum(dh * xhat, axis=0, keepdims=True)
        dxh = dh * gv
        dx_ref[...] = dxo_v + r * (dxh - xhat * jnp.mean(dxh * xhat, axis=-1, keepdims=True))

    tok = lambda i: (i, 0)
    one = lambda i: (0, 0)
    return _launch(
        body, name=name, grid=(t // tm,),
        in_specs=[pl.BlockSpec((tm, d), tok), pl.BlockSpec((tm, d), tok), pl.BlockSpec((1, d), one),
                  pl.BlockSpec((tm, f), tok), pl.BlockSpec((tm, f), tok)] + [HBM_SPEC] * nw,
        out_specs=[pl.BlockSpec((tm, d), tok), pl.BlockSpec((tm, 2 * f), tok), pl.BlockSpec((tm, 2 * d), tok),
                   pl.BlockSpec((1, d), one)],
        out_shape=[jax.ShapeDtypeStruct((t, d), F32), jax.ShapeDtypeStruct((t, 2 * f), BF16),
                   jax.ShapeDtypeStruct((t, 2 * d), BF16), jax.ShapeDtypeStruct((1, d), F32)],
        scratch_shapes=[pltpu.VMEM((f, d), BF16), pltpu.VMEM((f, d), BF16), pltpu.VMEM((f, d), BF16), pltpu.SemaphoreType.DMA((n_copies,))],
        args=(dxo, x, g, a, b, *flat), cargo=cargo)


def _weight_grad(lhs, rhs, name, cargo=(), lhs_part=(0, 1), rhs_part=(0, 1)):
    t = lhs.shape[0]
    m = lhs.shape[1] // lhs_part[1]
    d = rhs.shape[1] // rhs_part[1]
    tm = min(TM_TN, t)
    nt = t // tm
    rps = m // N_CHIPS
    hr = rps // 2
    assert hr % 16 == 0

    def body(l_ref, r_ref, o_ref, acc, stage, recv, send_sems, recv_sems):
        i = pl.program_id(0)

        @pl.when(i == 0)
        def _():
            acc[...] = jnp.zeros_like(acc)

        acc[...] += _tn(l_ref[...], r_ref[...])

        @pl.when(i == nt - 1)
        def _():
            x, y, c = _my_place()
            copies = []
            for q in range(N_CHIPS):
                stage[q] = acc[pl.ds(pl.multiple_of(q * rps + (1 - c) * hr, 16), hr), :].astype(BF16)
                cp = pltpu.make_async_remote_copy(
                    src_ref=stage.at[q], dst_ref=recv.at[q], send_sem=send_sems.at[q], recv_sem=recv_sems.at[q],
                    device_id=(x, y, 1 - c), device_id_type=MESH)
                cp.start()
                copies.append(cp)
            for q, cp in enumerate(copies):
                cp.wait_recv()
                mine = acc[pl.ds(pl.multiple_of(q * rps + c * hr, 16), hr), :]
                o_ref[q] = (mine + recv[q].astype(F32)).astype(BF16)
            for cp in copies:
                cp.wait_send()

    outs, carried = _launch(
        body, name=name, grid=(nt,),
        in_specs=[pl.BlockSpec((tm, m), lambda i: (i, lhs_part[0])), pl.BlockSpec((tm, d), lambda i: (i, rhs_part[0]))],
        out_specs=[pl.BlockSpec((N_CHIPS, hr, d), lambda i: (0, 0, 0))],
        out_shape=[jax.ShapeDtypeStruct((N_CHIPS, hr, d), BF16)],
        scratch_shapes=[pltpu.VMEM((m, d), F32), pltpu.VMEM((N_CHIPS, hr, d), BF16), pltpu.VMEM((N_CHIPS, hr, d), BF16),
                        pltpu.SemaphoreType.DMA((N_CHIPS,)), pltpu.SemaphoreType.DMA((N_CHIPS,))],
        args=(lhs, rhs), cargo=cargo)
    return outs[0], carried


def _window_sums(src, cols, w, tm, levels, trailing):
    def read_src(lo, hi):
        return src[lo:hi, cols]

    read, k, level = read_src, 1, 0
    while True:
        last = 2 * k == w
        if trailing:
            lo, hi = (HALO if last else 8 * (level + 1)), HALO + tm
            cur = read(lo, hi) + read(lo - k, hi - k)
        else:
            lo, hi = 0, (tm if last else tm + HALO - 8 * (level + 1))
            cur = read(lo, hi) + read(lo + k, hi + k)
        if last:
            return cur
        levels[level, lo:hi, :] = cur
        read = lambda a, b, level=level: levels[level, a:b, :]
        k, level = 2 * k, level + 1


def _pool_parts(u_cols, ubuf, cols, w, row, tm, levels):
    ws = _window_sums(ubuf, cols, w, tm, levels, trailing=True)
    inv = 1.0 / jnp.minimum(row + 1, w).astype(F32)
    return ws * inv - u_cols, inv


def _mixer_forward(x, g, win_t, wout_x, conv_w, pool_w, pool_scale, cargo=()):
    t, d = x.shape
    dc = win_t.shape[0] // 4
    gcw = dc // len(POOL_WINDOWS)
    wo_rows = d // N_CHIPS
    wo_stride = wout_x.shape[0] // N_CHIPS
    tm = min(TM_MIX, t)

    def body(x_ref, g_ref, win_hbm, wout_hbm, cw_ref, pw_ref, ps_ref, xo_ref, proj_ref, y_ref,
             win, wout, zbuf, ubuf, levels, sems):
        i = pl.program_id(0)

        @pl.when(i == 0)
        def _():
            pairs = [(win_hbm, win)]
            for k in range(N_CHIPS):
                pairs.append((wout_hbm.at[pl.ds(k * wo_stride, wo_rows), :], wout.at[pl.ds(k * wo_rows, wo_rows), :]))
            _load_rows(pairs, sems)
            zbuf[0:8, :] = jnp.zeros((8, dc), F32)
            ubuf[0:HALO, :] = jnp.zeros((HALO, dc), F32)

        xv = x_ref[...]
        r = lax.rsqrt(jnp.mean(xv * xv, axis=-1, keepdims=True) + EPS)
        h = (xv * r * g_ref[...]).astype(BF16)
        v = _nt(h, win[0:dc, :])
        gb = _nt(h, win[dc:2 * dc, :])
        gc = _nt(h, win[2 * dc:3 * dc, :])
        u = _nt(h, win[3 * dc:4 * dc, :])
        proj_ref[:, 0:dc] = v.astype(BF16)
        proj_ref[:, dc:2 * dc] = gb.astype(BF16)
        proj_ref[:, 2 * dc:3 * dc] = gc.astype(BF16)
        proj_ref[:, 3 * dc:4 * dc] = u.astype(BF16)

        z = gc * v
        zbuf[8:8 + tm, :] = z
        cw = cw_ref[...]
        conv = cw[2:3, :] * z + cw[1:2, :] * zbuf[7:7 + tm, :] + cw[0:1, :] * zbuf[6:6 + tm, :]
        y_ref[:, 0:dc] = (gb * conv).astype(BF16)

        ubuf[HALO:HALO + tm, :] = u
        row = i * tm + lax.broadcasted_iota(jnp.int32, (tm, 1), 0)
        for gi, w in enumerate(POOL_WINDOWS):
            cols = slice(gi * gcw, (gi + 1) * gcw)
            pooled, _ = _pool_parts(u[:, cols], ubuf, cols, w, row, tm, levels)
            yb = _nn(pooled.astype(BF16), pw_ref[gi].astype(BF16)) * ps_ref[:, cols]
            y_ref[:, dc + gi * gcw:dc + (gi + 1) * gcw] = yb.astype(BF16)

        xo_ref[...] = xv + _nn(y_ref[...], wout[...])
        zbuf[0:8, :] = zbuf[tm:tm + 8, :]
        ubuf[0:HALO, :] = ubuf[tm:tm + HALO, :]

    tok = lambda i: (i, 0)
    one = lambda i: (0, 0)
    return _launch(
        body, name="mixer_forward", grid=(t // tm,),
        in_specs=[pl.BlockSpec((tm, d), tok), pl.BlockSpec((1, d), one), HBM_SPEC, HBM_SPEC,
                  pl.BlockSpec(conv_w.shape, one), pl.BlockSpec(pool_w.shape, lambda i: (0, 0, 0)), pl.BlockSpec((1, dc), one)],
        out_specs=[pl.BlockSpec((tm, d), tok), pl.BlockSpec((tm, 4 * dc), tok), pl.BlockSpec((tm, 2 * dc), tok)],
        out_shape=[jax.ShapeDtypeStruct((t, d), F32), jax.ShapeDtypeStruct((t, 4 * dc), BF16), jax.ShapeDtypeStruct((t, 2 * dc), BF16)],
        scratch_shapes=[pltpu.VMEM((4 * dc, d), BF16), pltpu.VMEM((2 * dc, d), BF16),
                        pltpu.VMEM((tm + 8, dc), F32), pltpu.VMEM((tm + HALO, dc), F32),
                        pltpu.VMEM((WINDOW_LEVELS, tm + HALO, gcw), F32), pltpu.SemaphoreType.DMA((1 + N_CHIPS,))],
        args=(x, g, win_t, wout_x, conv_w, pool_w, pool_scale), cargo=cargo)


def _mixer_backward(dxo, x, g, proj, win_t, wout_x, conv_w, pool_w, pool_scale, cargo=()):
    t, d = x.shape
    dc = win_t.shape[0] // 4
    ng = len(POOL_WINDOWS)
    gcw = dc // ng
    wo_rows = d // N_CHIPS
    wo_stride = wout_x.shape[0] // N_CHIPS
    tm = min(TM_MIX, t)
    n_tiles = t // tm
    hb = tm // HALO

    def body(dxo_ref, x_ref, g_ref, proj_ref, halo_ref, win_hbm, wout_hbm, cw_ref, pw_ref, ps_ref,
             dx_ref, dproj_ref, h_ref, dxob_ref, dg_ref, dcw_ref, dps_ref, dpw_ref,
             win, wout, zbuf, ubuf, dcbuf, ebuf, levels, sems):
        i = pl.program_id(0)
        tile = n_tiles - 1 - i

        @pl.when(i == 0)
        def _():
            pairs = [(win_hbm, win)]
            for k in range(N_CHIPS):
                pairs.append((wout_hbm.at[pl.ds(k * wo_stride, wo_rows), :], wout.at[pl.ds(k * wo_rows, wo_rows), :]))
            _load_rows(pairs, sems)
            dcbuf[tm:tm + 8, :] = jnp.zeros((8, dc), F32)
            ebuf[tm:tm + HALO, :] = jnp.zeros((HALO, dc), F32)
            dg_ref[...] = jnp.zeros_like(dg_ref)
            dcw_ref[...] = jnp.zeros_like(dcw_ref)
            dps_ref[...] = jnp.zeros_like(dps_ref)
            dpw_ref[...] = jnp.zeros_like(dpw_ref)

        xv = x_ref[...]
        gv = g_ref[...]
        r = lax.rsqrt(jnp.mean(xv * xv, axis=-1, keepdims=True) + EPS)
        xhat = xv * r
        h_ref[...] = (xhat * gv).astype(BF16)
        dxo_v = dxo_ref[...]
        dxo_b = dxo_v.astype(BF16)
        dxob_ref[...] = dxo_b

        v = proj_ref[:, 0:dc].astype(F32)
        gb = proj_ref[:, dc:2 * dc].astype(F32)
        gc = proj_ref[:, 2 * dc:3 * dc].astype(F32)
        u = proj_ref[:, 3 * dc:4 * dc].astype(F32)
        first = jnp.where(tile > 0, 1.0, 0.0)
        zbuf[0:HALO, :] = halo_ref[:, 2 * dc:3 * dc].astype(F32) * halo_ref[:, 0:dc].astype(F32) * first
        ubuf[0:HALO, :] = halo_ref[:, 3 * dc:4 * dc].astype(F32) * first
        z = gc * v
        zbuf[HALO:HALO + tm, :] = z
        ubuf[HALO:HALO + tm, :] = u
        z1 = zbuf[HALO - 1:HALO - 1 + tm, :]
        z2 = zbuf[HALO - 2:HALO - 2 + tm, :]
        cw = cw_ref[...]
        conv = cw[2:3, :] * z + cw[1:2, :] * z1 + cw[0:1, :] * z2

        dy = _nt(dxo_b, wout[...])
        dya = dy[:, 0:dc]
        dgb = dya * conv
        dconv = dya * gb
        dcbuf[0:tm, :] = dconv
        dz = cw[2:3, :] * dconv + cw[1:2, :] * dcbuf[1:1 + tm, :] + cw[0:1, :] * dcbuf[2:2 + tm, :]
        dgc = dz * v
        dv = dz * gc
        dcw_ref[0:1, :] += jnp.sum(dconv * z2, axis=0, keepdims=True)
        dcw_ref[1:2, :] += jnp.sum(dconv * z1, axis=0, keepdims=True)
        dcw_ref[2:3, :] += jnp.sum(dconv * z, axis=0, keepdims=True)

        dproj_ref[:, 0:dc] = dv.astype(BF16)
        dproj_ref[:, dc:2 * dc] = dgb.astype(BF16)
        dproj_ref[:, 2 * dc:3 * dc] = dgc.astype(BF16)

        row = tile * tm + lax.broadcasted_iota(jnp.int32, (tm, 1), 0)
        for gi, w in enumerate(POOL_WINDOWS):
            cols = slice(gi * gcw, (gi + 1) * gcw)
            pooled, inv_cnt = _pool_parts(u[:, cols], ubuf, cols, w, row, tm, levels)
            pooled_b = pooled.astype(BF16)
            pw_b = pw_ref[gi].astype(BF16)
            dyb = dy[:, dc + gi * gcw:dc + (gi + 1) * gcw]
            q = _nn(pooled_b, pw_b)
            dps_ref[:, cols] += jnp.sum(q * dyb, axis=0, keepdims=True)
            dq = (dyb * ps_ref[:, cols]).astype(BF16)
            dpw_ref[gi] += _tn(pooled_b, dq)
            dpooled = _nt(dq, pw_b)
            ebuf[0:tm, cols] = dpooled * inv_cnt
            du = _window_sums(ebuf, cols, w, tm, levels, trailing=False) - dpooled
            dproj_ref[:, 3 * dc + gi * gcw:3 * dc + (gi + 1) * gcw] = du.astype(BF16)

        dh = _nn(dproj_ref[...], win[...])
        dg_ref[...] += jnp.sum(dh * xhat, axis=0, keepdims=True)
        dxh = dh * gv
        dx_ref[...] = dxo_v + r * (dxh - xhat * jnp.mean(dxh * xhat, axis=-1, keepdims=True))
        dcbuf[tm:tm + 8, :] = dcbuf[0:8, :]
        ebuf[tm:tm + HALO, :] = ebuf[0:HALO, :]

    tok = lambda i: (n_tiles - 1 - i, 0)
    halo = lambda i: (jnp.maximum((n_tiles - 1 - i) * hb - 1, 0), 0)
    one = lambda i: (0, 0)
    return _launch(
        body, name="mixer_backward", grid=(n_tiles,),
        in_specs=[pl.BlockSpec((tm, d), tok), pl.BlockSpec((tm, d), tok), pl.BlockSpec((1, d), one),
                  pl.BlockSpec((tm, 4 * dc), tok), pl.BlockSpec((HALO, 4 * dc), halo), HBM_SPEC, HBM_SPEC,
                  pl.BlockSpec(conv_w.shape, one), pl.BlockSpec(pool_w.shape, lambda i: (0, 0, 0)), pl.BlockSpec((1, dc), one)],
        out_specs=[pl.BlockSpec((tm, d), tok), pl.BlockSpec((tm, 4 * dc), tok), pl.BlockSpec((tm, d), tok), pl.BlockSpec((tm, d), tok),
                   pl.BlockSpec((1, d), one), pl.BlockSpec(conv_w.shape, one), pl.BlockSpec((1, dc), one),
                   pl.BlockSpec(pool_w.shape, lambda i: (0, 0, 0))],
        out_shape=[jax.ShapeDtypeStruct((t, d), F32), jax.ShapeDtypeStruct((t, 4 * dc), BF16), jax.ShapeDtypeStruct((t, d), BF16),
                   jax.ShapeDtypeStruct((t, d), BF16), jax.ShapeDtypeStruct((1, d), F32), jax.ShapeDtypeStruct(conv_w.shape, F32),
                   jax.ShapeDtypeStruct((1, dc), F32), jax.ShapeDtypeStruct(pool_w.shape, F32)],
        scratch_shapes=[pltpu.VMEM((4 * dc, d), BF16), pltpu.VMEM((2 * dc, d), BF16),
                        pltpu.VMEM((tm + HALO, dc), F32), pltpu.VMEM((tm + HALO, dc), F32),
                        pltpu.VMEM((tm + 8, dc), F32), pltpu.VMEM((tm + HALO, dc), F32),
                        pltpu.VMEM((WINDOW_LEVELS, tm + HALO, gcw), F32), pltpu.SemaphoreType.DMA((1 + N_CHIPS,))],
        args=(dxo, x, g, proj, proj, win_t, wout_x, conv_w, pool_w, pool_scale), cargo=cargo)


def _adam_update(w, gv, m, v):
    m_new = ADAM_B1 * m + (1.0 - ADAM_B1) * gv
    v_new = ADAM_B2 * v + (1.0 - ADAM_B2) * (gv * gv)
    m_hat = m_new / (1.0 - ADAM_B1 ** ADAM_STEP)
    v_hat = v_new / (1.0 - ADAM_B2 ** ADAM_STEP)
    return -ADAM_LR * (m_hat / (jnp.sqrt(v_hat) + ADAM_EPS) + ADAM_WD * w), m_new, v_new


def _adamw(w, grad, m, v, name):
    rows, cols = w.shape
    br = _row_block(rows, 512) if rows >= 8 else rows

    def body(w_ref, g_ref, m_ref, v_ref, go_ref, d_ref, mo_ref, vo_ref):
        gv = g_ref[...]
        go_ref[...] = gv
        d_ref[...], mo_ref[...], vo_ref[...] = _adam_update(w_ref[...], gv, m_ref[...], v_ref[...])

    blk = pl.BlockSpec((br, cols), lambda i: (i, 0))
    return pl.pallas_call(
        body, name=name,
        out_shape=[jax.ShapeDtypeStruct((rows, cols), F32)] * 4,
        grid=(rows // br,), in_specs=[blk] * 4, out_specs=[blk] * 4,
        compiler_params=pltpu.CompilerParams(dimension_semantics=("parallel",), vmem_limit_bytes=VMEM_LIMIT),
    )(w, grad, m, v)


def _adamw_transposed(w, grad_t, m, v, name):
    _, rows, cols = w.shape
    br = 256 if rows % 256 == 0 else rows

    def body(w_ref, gt_ref, m_ref, v_ref, g_ref, d_ref, mo_ref, vo_ref):
        gv = gt_ref[...].T
        g_ref[...] = gv
        d_ref[...], mo_ref[...], vo_ref[...] = _adam_update(w_ref[...], gv, m_ref[...], v_ref[...])

    blk = pl.BlockSpec((None, br, cols), lambda i: (0, i, 0))
    return pl.pallas_call(
        body, name=name,
        out_shape=[jax.ShapeDtypeStruct((1, rows, cols), F32)] * 4,
        grid=(rows // br,), in_specs=[blk, pl.BlockSpec((cols, br), lambda i: (0, i)), blk, blk], out_specs=[blk] * 4,
        compiler_params=pltpu.CompilerParams(dimension_semantics=("parallel",)),
    )(w, grad_t, m, v)


def _f32_rows_as_bf16(a, rows, cols):
    bits = lax.bitcast_convert_type(a, BF16).reshape(a.shape[0], 2 * a.shape[1])
    return jnp.pad(bits, ((0, rows - bits.shape[0]), (0, cols - bits.shape[1])))


def kernel(x, norm_ffn1, ffn1_w_gate, ffn1_w_up, ffn1_w_down, norm_mix, w_in, conv_w, pool_w, pool_scale, w_out, norm_ffn2, ffn2_w_gate, ffn2_w_up, ffn2_w_down, norm_final, loss_target, m_norm_ffn1, m_ffn1_w_gate, m_ffn1_w_up, m_ffn1_w_down, m_norm_mix, m_w_in, m_conv_w, m_pool_w, m_pool_scale, m_w_out, m_norm_ffn2, m_ffn2_w_gate, m_ffn2_w_up, m_ffn2_w_down, m_norm_final, v_norm_ffn1, v_ffn1_w_gate, v_ffn1_w_up, v_ffn1_w_down, v_norm_mix, v_w_in, v_conv_w, v_pool_w, v_pool_scale, v_w_out, v_norm_ffn2, v_ffn2_w_gate, v_ffn2_w_up, v_ffn2_w_down, v_norm_final):
    weights = dict(norm_ffn1=norm_ffn1, ffn1_w_gate=ffn1_w_gate, ffn1_w_up=ffn1_w_up, ffn1_w_down=ffn1_w_down, norm_mix=norm_mix,
                   w_in=w_in, conv_w=conv_w, pool_w=pool_w, pool_scale=pool_scale, w_out=w_out, norm_ffn2=norm_ffn2,
                   ffn2_w_gate=ffn2_w_gate, ffn2_w_up=ffn2_w_up, ffn2_w_down=ffn2_w_down, norm_final=norm_final)
    first_m = dict(norm_ffn1=m_norm_ffn1, ffn1_w_gate=m_ffn1_w_gate, ffn1_w_up=m_ffn1_w_up, ffn1_w_down=m_ffn1_w_down,
                   norm_mix=m_norm_mix, w_in=m_w_in, conv_w=m_conv_w, pool_w=m_pool_w, pool_scale=m_pool_scale, w_out=m_w_out,
                   norm_ffn2=m_norm_ffn2, ffn2_w_gate=m_ffn2_w_gate, ffn2_w_up=m_ffn2_w_up, ffn2_w_down=m_ffn2_w_down,
                   norm_final=m_norm_final)
    second_m = dict(norm_ffn1=v_norm_ffn1, ffn1_w_gate=v_ffn1_w_gate, ffn1_w_up=v_ffn1_w_up, ffn1_w_down=v_ffn1_w_down,
                    norm_mix=v_norm_mix, w_in=v_w_in, conv_w=v_conv_w, pool_w=v_pool_w, pool_scale=v_pool_scale, w_out=v_w_out,
                    norm_ffn2=v_norm_ffn2, ffn2_w_gate=v_ffn2_w_gate, ffn2_w_up=v_ffn2_w_up, ffn2_w_down=v_ffn2_w_down,
                    norm_final=v_norm_final)
    names = list(weights)

    xs = x[0]
    tgt = loss_target[0]
    t, d = xs.shape
    dc = pool_scale.shape[1]
    cx, cy, cc = _my_place()
    chip = 2 * cx + cy
    place = jnp.stack([chip, cc]).astype(jnp.int32)

    conv_rows = 32
    wout_x = jnp.concatenate([w_out[0].astype(BF16), _f32_rows_as_bf16(conv_w[0], conv_rows, d)], axis=0)

    g1, gm, g2 = norm_ffn1, norm_mix, norm_ffn2
    gf = norm_final.reshape(1, d)
    pw = pool_w[0]

    (wd1_shard, wg2_shard, wu2_shard, wd2_shard), [(wg1, wu1)] = _cast_to_bf16(
        [ffn1_w_down[0], ffn2_w_gate[0].T, ffn2_w_up[0].T, ffn2_w_down[0]], "gather_ffn1",
        [_gather_cargo([ffn1_w_gate[0].T.astype(BF16), ffn1_w_up[0].T.astype(BF16)])])
    (a1, b1, s1), [(wd1, win_t, wout_g)] = _ffn_up(
        xs, g1, wg1, wu1, "ffn1_up", [_gather_cargo([wd1_shard, w_in[0].T.astype(BF16), wout_x], relay_at=(0.72, 1.0))])
    (x1,), [(wg2,)] = _ffn_down(xs, s1, wd1, "ffn1_down", [_gather_cargo([wg2_shard])])
    wo_rows = w_out.shape[1]
    cshard = conv_w.shape[2]
    conv_bits = wout_g.reshape(N_CHIPS, wo_rows + conv_rows, d)[:, wo_rows:wo_rows + conv_w.shape[1], :2 * cshard]
    conv_full = lax.bitcast_convert_type(conv_bits.reshape(N_CHIPS, conv_w.shape[1], cshard, 2), F32)
    conv_full = jnp.transpose(conv_full, (1, 0, 2)).reshape(conv_w.shape[1], N_CHIPS * cshard)
    (x2, proj, ymix), [(wu2,)] = _mixer_forward(x1, gm, win_t, wout_g, conv_full, pw, pool_scale, [_gather_cargo([wu2_shard])])
    (a2, b2, s2), [(wd2,)] = _ffn_up(x2, g2, wg2, wu2, "ffn2_up", [_gather_cargo([wd2_shard])])
    (dx3, sq_cols, dgf), _ = _ffn_down(x2, s2, wd2, "ffn2_down", loss_head=(gf, tgt))

    (dx2, dab2, hd2, dg2), _ = _ffn_backward(dx3, x2, g2, a2, b2, wg2, wu2, wd2, "ffn2_backward")
    p_wg2, _ = _weight_grad(dab2, hd2, "ffn2_gate_grad", lhs_part=(0, 2), rhs_part=(0, 2))
    p_wu2, [(x_wg2,)] = _weight_grad(dab2, hd2, "ffn2_up_grad", [_exchange_cargo([p_wg2])], lhs_part=(1, 2), rhs_part=(0, 2))
    p_wd2, [(x_wu2,)] = _weight_grad(s2, hd2, "ffn2_down_grad", [_exchange_cargo([p_wu2])], rhs_part=(1, 2))

    (dx1, dproj, h2, dx2b, dgm, dcw, dps, dpw), [(x_wd2,)] = _mixer_backward(
        dx2, x1, gm, proj, win_t, wout_g, conv_full, pw, pool_scale, [_exchange_cargo([p_wd2])])

    (dx0, dab1, hd1, dg1), _ = _ffn_backward(dx1, xs, g1, a1, b1, wg1, wu1, wd1, "ffn1_backward")

    npw = pw.size // d
    head = [dg1, dgm, dg2, dgf, jnp.pad(dps, ((0, 0), (0, d - dc))), jnp.pad(dcw, ((0, 0), (0, d - dc))), sq_cols]
    n_head = sum(h.shape[0] for h in head)
    base = -(-n_head // 8) * 8
    pack = jnp.concatenate(head + [jnp.zeros((base - n_head, d), F32), dpw.reshape(npw, d)], axis=0)

    p_wg1, [(packs,)] = _weight_grad(dab1, hd1, "ffn1_gate_grad", [_all_gather_small_cargo(pack)], lhs_part=(0, 2), rhs_part=(0, 2))
    p_wu1, [(x_wg1,)] = _weight_grad(dab1, hd1, "ffn1_up_grad", [_exchange_cargo([p_wg1])], lhs_part=(1, 2), rhs_part=(0, 2))
    p_wd1, [(x_wu1,)] = _weight_grad(s1, hd1, "ffn1_down_grad", [_exchange_cargo([p_wu1])], rhs_part=(1, 2))
    p_win, [(x_wd1,)] = _weight_grad(dproj, h2, "w_in_grad", [_exchange_cargo([p_wd1])])
    p_wout, [(x_win,)] = _weight_grad(ymix, dx2b, "w_out_grad", [_exchange_cargo([p_win])])
    x_wout, = _run_cargo(_exchange_cargo([p_wout]), "grad_exchange_last")
    small = _sum_by_device(packs)
    loss = jnp.sum(small[n_head - 1]) * (0.5 / d)

    order = ["wg1", "wu1", "wd1", "win", "wout", "wg2", "wu2", "wd2"]
    pairs = dict(wg1=p_wg1, wu1=p_wu1, wd1=p_wd1, win=p_win, wout=p_wout, wg2=p_wg2, wu2=p_wu2, wd2=p_wd2)
    landed = dict(wg1=x_wg1, wu1=x_wu1, wd1=x_wd1, win=x_win, wout=x_wout, wg2=x_wg2, wu2=x_wu2, wd2=x_wd2)
    both = _sibling_share([_chip_sum(pairs[k], landed[k], place, k) for k in order])
    rwg1, rwu1, rwd1, rwin, rwout, rwg2, rwu2, rwd2 = [b.reshape(2 * b.shape[1], b.shape[2]) for b in both]

    grads = {
        "norm_ffn1": small[0:1], "norm_mix": small[1:2], "norm_ffn2": small[2:3], "norm_final": small[3],
        "pool_scale": small[4:5, :dc],
        "conv_w": lax.dynamic_slice_in_dim(small[5:5 + dcw.shape[0], :dc], chip * cshard, cshard, axis=1)[None],
        "pool_w": small[base:].reshape(pool_w.shape),
        "ffn1_w_down": rwd1[None], "w_out": rwout[None], "ffn2_w_down": rwd2[None],
    }
    by_view = {"ffn1_w_gate": rwg1, "ffn1_w_up": rwu1, "ffn2_w_gate": rwg2, "ffn2_w_up": rwu2}

    deltas, new_m, new_v = {}, {}, {}
    for n in names:
        w = weights[n]
        shape = w.shape
        if n == "w_in":
            grads[n], deltas[n], new_m[n], new_v[n] = _adamw_transposed(w, rwin, first_m[n], second_m[n], "adamw_" + n)
            continue
        if n in by_view:
            view = lambda a: jnp.swapaxes(a, 1, 2)[0]
            back = lambda a: jnp.swapaxes(a[None], 1, 2)
            outs = _adamw(view(w), by_view[n], view(first_m[n]), view(second_m[n]), "adamw_" + n)
            grads[n], deltas[n], new_m[n], new_v[n] = [back(o) for o in outs]
            continue
        as2d = (lambda a: a.reshape(-1, shape[-1]))
        outs = _adamw(as2d(w), as2d(grads[n]), as2d(first_m[n]), as2d(second_m[n]), "adamw_" + n)
        grads[n], deltas[n], new_m[n], new_v[n] = [o.reshape(shape) for o in outs]

    return (loss, dx0[None], *[grads[n] for n in names], *[deltas[n] for n in names],
            *[new_m[n] for n in names], *[new_v[n] for n in names])
```

```python
import jax
import jax.numpy as jnp
from jax import lax
from jax.experimental import pallas as pl
from jax.experimental.pallas import tpu as pltpu

F32 = jnp.float32
BF16 = jnp.bfloat16
MESH = pl.DeviceIdType.MESH

EPS = 1e-6
POOL_WINDOWS = (2, 4, 8, 16)
ADAM_LR = 0.001
ADAM_B1 = 0.9
ADAM_B2 = 0.999
ADAM_EPS = 1e-08
ADAM_WD = 0.01
ADAM_STEP = 10

N_CHIPS = 4
N_DEVICES = 8
MXU_COLS_V7X = 256
VMEM_LIMIT = 56 * 1024 * 1024
TM_FFN = 512
TM_MIX = 512
TM_TN = 1024
HALO = 32
WINDOW_LEVELS = 3
FFN_FWD_CHUNKS = 2
FFN_BWD_CHUNKS = 2


def _nt(a, b):
    return lax.dot_general(a, b, (((1,), (1,)), ((), ())), preferred_element_type=F32)


def _tn(a, b):
    return lax.dot_general(a, b, (((0,), (0,)), ((), ())), preferred_element_type=F32)


def _nn(a, b):
    return jnp.dot(a, b, preferred_element_type=F32)


def _sigmoid(a):
    return 1.0 / (1.0 + jnp.exp(-a))


def _feature_chunks(n, parts):
    assert n % MXU_COLS_V7X == 0
    tiles = n // MXU_COLS_V7X
    out, s0 = [], 0
    for p in range(parts):
        sz = (tiles // parts + (1 if p < tiles % parts else 0)) * MXU_COLS_V7X
        if sz:
            out.append((s0, sz))
            s0 += sz
    return out


def _row_block(rows, cap):
    best = 8
    for b in range(8, min(rows, cap) + 1, 8):
        if rows % b == 0:
            best = b
    assert rows % best == 0
    return best


def _my_place():
    return lax.axis_index("x"), lax.axis_index("y"), lax.axis_index("c")


def _other_chips(x, y):
    return [(1 - x, y), (x, 1 - y), (1 - x, 1 - y)]


HBM_SPEC = pl.BlockSpec(memory_space=pltpu.HBM)


class _Cargo:
    def __init__(self, operands, out_shapes, n_sems, phases, when, in_place=False):
        self.operands, self.out_shapes, self.n_sems = list(operands), list(out_shapes), n_sems
        self.phases, self.when = list(phases), list(when)
        self.in_place = in_place
        assert len(self.phases) == len(self.when) and self.when[0] == 0.0 and self.when[-1] == 1.0


def _launch(body, *, name, grid, in_specs, out_specs, out_shape, scratch_shapes, args, cargo=()):
    params = pltpu.CompilerParams(dimension_semantics=("arbitrary",) * len(grid), vmem_limit_bytes=VMEM_LIMIT)
    cargos = list(cargo)
    c_operands = [op for cg in cargos for op in cg.operands]
    c_shapes = [sh for cg in cargos for sh in cg.out_shapes]
    counts = [len(in_specs), len(c_operands), len(out_shape), len(c_shapes), len(scratch_shapes), 2 * len(cargos)]

    def carrying(*refs):
        groups, pos = [], 0
        for k in counts:
            groups.append(refs[pos:pos + k])
            pos += k
        ins, c_ins, outs, c_outs, scratch, sems = groups
        parts, pi, po = [], 0, 0
        for n, cg in enumerate(cargos):
            parts.append((c_ins[pi:pi + len(cg.operands)], c_outs[po:po + len(cg.out_shapes)], sems[2 * n], sems[2 * n + 1]))
            pi += len(cg.operands)
            po += len(cg.out_shapes)
        step, steps = 0, 1
        for ax, g in enumerate(grid):
            step = step * g + pl.program_id(ax)
            steps *= g
        todo = {}
        for cg, part in zip(cargos, parts):
            for phase, frac in zip(cg.phases[:-1], cg.when[:-1]):
                todo.setdefault(int(round(frac * (steps - 1))), []).append((phase, part))

        for at in sorted(todo):
            @pl.when(step == at)
            def _(at=at):
                for phase, part in todo[at]:
                    phase(*part)

        body(*ins, *outs, *scratch)

        if cargos:
            @pl.when(step == steps - 1)
            def _():
                for cg, part in zip(cargos, parts):
                    cg.phases[-1](*part)

    sems = [pltpu.SemaphoreType.DMA((cg.n_sems,)) for cg in cargos for _ in range(2)]
    aliases, pi, po = {}, counts[0], counts[2]
    for cg in cargos:
        if cg.in_place:
            aliases.update({pi + k: po + k for k in range(len(cg.operands))})
        pi += len(cg.operands)
        po += len(cg.out_shapes)
    outs = pl.pallas_call(
        carrying, name=name, grid=grid,
        in_specs=list(in_specs) + [HBM_SPEC] * counts[1], out_specs=list(out_specs) + [HBM_SPEC] * counts[3],
        out_shape=list(out_shape) + c_shapes, scratch_shapes=list(scratch_shapes) + sems,
        input_output_aliases=aliases, compiler_params=params)(*args, *c_operands)
    own, rest = list(outs[:counts[2]]), list(outs[counts[2]:])
    carried, po = [], 0
    for cg in cargos:
        carried.append(rest[po:po + len(cg.out_shapes)])
        po += len(cg.out_shapes)
    return own, carried


def _run_cargo(cargo, name):
    n_in, n_out = len(cargo.operands), len(cargo.out_shapes)

    def body(*refs):
        c_ins, c_outs, sems = refs[:n_in], refs[n_in:n_in + n_out], refs[n_in + n_out:]
        for phase in cargo.phases:
            phase(c_ins, c_outs, *sems)

    sem = pltpu.SemaphoreType.DMA((cargo.n_sems,))
    return list(pl.pallas_call(body, name=name, out_shape=cargo.out_shapes, in_specs=[HBM_SPEC] * n_in,
                               out_specs=[HBM_SPEC] * n_out, scratch_shapes=[sem, sem])(*cargo.operands))


def _gather_cargo(shards):
    n = len(shards)
    for s in shards:
        assert s.shape[0] % 32 == 0
    slots = 8

    def steps(ins, outs, send_sems, recv_sems):
        x, y, c = _my_place()
        sibling = (x, y, 1 - c)
        over_x, over_y = (1 - x, y, c), (x, 1 - y, c)
        mine, chip_x, chip_y, chip_d = 2 * x + y, 2 * (1 - x) + y, 2 * x + (1 - y), 2 * (1 - x) + (1 - y)

        def rows_of(a, chip_index, half, part=None):
            rps = shards[a].shape[0]
            hr = rps // 2
            first = -(-hr // 32) * 16
            offset, size = {None: (0, hr), 0: (0, first), 1: (first, hr - first)}[part]
            return outs[a].at[pl.ds(pl.multiple_of(chip_index * rps + half * hr + offset, 16), size), :]

        def remote(a, slot, src, dst, to):
            return pltpu.make_async_remote_copy(
                src_ref=src, dst_ref=dst, send_sem=send_sems.at[a * slots + slot], recv_sem=recv_sems.at[a * slots + slot],
                device_id=to, device_id_type=MESH)

        def same_rows(a, slot, rows, to):
            return remote(a, slot, rows, rows, to)

        def own_copy(a):
            rps = shards[a].shape[0]
            return remote(a, 7, ins[a], outs[a].at[pl.ds(pl.multiple_of(mine * rps, 16), rps), :], sibling)

        def my_half(a):
            hr = shards[a].shape[0] // 2
            return ins[a].at[pl.ds(pl.multiple_of(c * hr, 16), hr), :]

        def start():
            for a in range(n):
                own_copy(a).start()
                remote(a, 0, my_half(a), rows_of(a, mine, c), over_x).start()
                remote(a, 1, my_half(a), rows_of(a, mine, c), over_y).start()

        def relay_neighbours():
            for a in range(n):
                same_rows(a, 0, rows_of(a, chip_x, c), over_x).wait_recv()
                same_rows(a, 4, rows_of(a, chip_x, c), sibling).start()
                same_rows(a, 2, rows_of(a, chip_x, c, 0), over_y).start()
                same_rows(a, 1, rows_of(a, chip_y, c), over_y).wait_recv()
                same_rows(a, 5, rows_of(a, chip_y, c), sibling).start()
                same_rows(a, 3, rows_of(a, chip_y, c, 1), over_x).start()

        def relay_diagonal():
            for a in range(n):
                same_rows(a, 2, rows_of(a, chip_d, c, 0), over_y).wait_recv()
                same_rows(a, 3, rows_of(a, chip_d, c, 1), over_x).wait_recv()
                same_rows(a, 6, rows_of(a, chip_d, c), sibling).start()

        def finish():
            for a in range(n):
                for slot, chip_index in ((4, chip_x), (5, chip_y), (6, chip_d)):
                    same_rows(a, slot, rows_of(a, chip_index, 1 - c), sibling).wait_recv()
            for a in range(n):
                remote(a, 0, my_half(a), rows_of(a, mine, c), over_x).wait_send()
                remote(a, 1, my_half(a), rows_of(a, mine, c), over_y).wait_send()
                same_rows(a, 2, rows_of(a, chip_x, c, 0), over_y).wait_send()
                same_rows(a, 3, rows_of(a, chip_y, c, 1), over_x).wait_send()
                for slot, chip_index in ((4, chip_x), (5, chip_y), (6, chip_d)):
                    same_rows(a, slot, rows_of(a, chip_index, c), sibling).wait_send()
                own_copy(a).wait()

        return start, relay_neighbours, relay_diagonal, finish

    phases = [lambda *r, k=k: steps(*r)[k]() for k in range(4)]
    return _Cargo(shards, [jax.ShapeDtypeStruct((N_CHIPS * s.shape[0], s.shape[1]), s.dtype) for s in shards], slots * n,
                  phases, [0.0, 0.6, 0.85, 1.0])


def _exchange_cargo(pairs):
    n = len(pairs)

    def copies(ins, outs, send_sems, recv_sems):
        x, y, c = _my_place()
        return [pltpu.make_async_remote_copy(
            src_ref=ins[a].at[2 * chip[0] + chip[1]], dst_ref=outs[a].at[j],
            send_sem=send_sems.at[3 * a + j], recv_sem=recv_sems.at[3 * a + j], device_id=(*chip, c), device_id_type=MESH)
            for a in range(n) for j, chip in enumerate(_other_chips(x, y))]

    def start(*r):
        for cp in copies(*r):
            cp.start()

    def finish(*r):
        for cp in copies(*r):
            cp.wait()

    return _Cargo(pairs, [jax.ShapeDtypeStruct((3,) + p.shape[1:], p.dtype) for p in pairs], 3 * n, [start, finish], [0.0, 1.0])


def _all_gather_small_cargo(pack):
    rows, cols = pack.shape

    def copies(ins, outs, send_sems, recv_sems):
        x, y, c = _my_place()
        me = 4 * x + 2 * y + c
        remote = []
        for f in range(1, N_DEVICES):
            fx, fy, fc = (f >> 2) & 1, (f >> 1) & 1, f & 1
            to = (1 - x if fx else x, 1 - y if fy else y, 1 - c if fc else c)
            remote.append(pltpu.make_async_remote_copy(
                src_ref=ins[0], dst_ref=outs[0].at[me], send_sem=send_sems.at[f - 1], recv_sem=recv_sems.at[f - 1],
                device_id=to, device_id_type=MESH))
        own = pltpu.make_async_copy(ins[0], outs[0].at[me], send_sems.at[N_DEVICES - 1])
        return remote, own

    def start(*r):
        remote, own = copies(*r)
        own.start()
        for cp in remote:
            cp.start()

    def finish(*r):
        remote, own = copies(*r)
        for cp in remote:
            cp.wait()
        own.wait()

    return _Cargo([pack], [jax.ShapeDtypeStruct((N_DEVICES, rows, cols), F32)], N_DEVICES, [start, finish], [0.0, 1.0])


def _sum_by_device(packs):
    n, rows, cols = packs.shape

    def body(p_ref, o_ref):
        acc = p_ref[0]
        for dev in range(1, n):
            acc = acc + p_ref[dev]
        o_ref[...] = acc

    return pl.pallas_call(body, name="small_grads_sum", out_shape=jax.ShapeDtypeStruct((rows, cols), F32))(packs)


def _chip_sum(pair, got, place, tag):
    _, hr, cols = pair.shape
    br = _row_block(hr, 256)

    def body(k_ref, p_ref, r_ref, o_ref):
        acc = p_ref[...].astype(F32)
        for j in range(3):
            acc = acc + r_ref[j].astype(F32)
        o_ref[...] = acc

    return pl.pallas_call(
        body, name="grad_chip_sum_" + tag,
        out_shape=jax.ShapeDtypeStruct((2, hr, cols), F32),
        grid_spec=pltpu.PrefetchScalarGridSpec(
            num_scalar_prefetch=1, grid=(hr // br,),
            in_specs=[pl.BlockSpec((None, br, cols), lambda r, k_ref: (k_ref[0], r, 0)),
                      pl.BlockSpec((3, br, cols), lambda r, k_ref: (0, r, 0))],
            out_specs=pl.BlockSpec((None, br, cols), lambda r, k_ref: (k_ref[1], r, 0))),
        compiler_params=pltpu.CompilerParams(dimension_semantics=("parallel",)),
    )(place, pair, got)


def _share_cargo(halves):
    n = len(halves)

    def copies(ins, outs, send_sems, recv_sems):
        x, y, c = _my_place()
        return [pltpu.make_async_remote_copy(
            src_ref=outs[a].at[c], dst_ref=outs[a].at[c], send_sem=send_sems.at[a], recv_sem=recv_sems.at[a],
            device_id=(x, y, 1 - c), device_id_type=MESH) for a in range(n)]

    def start(*r):
        for cp in copies(*r):
            cp.start()

    def finish(*r):
        for cp in copies(*r):
            cp.wait()

    return _Cargo(halves, [jax.ShapeDtypeStruct(h.shape, h.dtype) for h in halves], n, [start, finish], [0.0, 1.0], in_place=True)


def _sibling_share(halves):
    n = len(halves)

    def body(*refs):
        outs = refs[n:2 * n]
        send_sems, recv_sems = refs[2 * n:]
        x, y, c = _my_place()
        copies = []
        for a in range(n):
            cp = pltpu.make_async_remote_copy(
                src_ref=outs[a].at[c], dst_ref=outs[a].at[c], send_sem=send_sems.at[a], recv_sem=recv_sems.at[a],
                device_id=(x, y, 1 - c), device_id_type=MESH)
            cp.start()
            copies.append(cp)
        for cp in copies:
            cp.wait()

    return pl.pallas_call(
        body, name="grad_share_sibling",
        out_shape=[jax.ShapeDtypeStruct(h.shape, h.dtype) for h in halves],
        in_specs=[HBM_SPEC] * n, out_specs=[HBM_SPEC] * n,
        input_output_aliases={a: a for a in range(n)},
        scratch_shapes=[pltpu.SemaphoreType.DMA((n,)), pltpu.SemaphoreType.DMA((n,))],
    )(*halves)


def _load_rows(pairs, sems):
    cps = [pltpu.make_async_copy(src, dst, sems.at[j]) for j, (src, dst) in enumerate(pairs)]
    for cp in cps:
        cp.start()
    for cp in cps:
        cp.wait()


def _piece_rows(weights):
    return list(weights), (lambda refs, mats: list(zip(refs, mats))), len(weights)


def _cast_to_bf16(arrays, name, cargo=()):
    rows, cols = arrays[0].shape
    n = len(arrays)
    br = _row_block(rows, 256)

    def body(*refs):
        for src, dst in zip(refs[:n], refs[n:]):
            dst[...] = src[...].astype(BF16)

    blk = pl.BlockSpec((br, cols), lambda i: (i, 0))
    return _launch(body, name=name, grid=(rows // br,), in_specs=[blk] * n, out_specs=[blk] * n,
                   out_shape=[jax.ShapeDtypeStruct((rows, cols), BF16)] * n, scratch_shapes=[], args=tuple(arrays), cargo=cargo)


def _loss_head(xv, gv, tv):
    d = xv.shape[-1]
    r = lax.rsqrt(jnp.mean(xv * xv, axis=-1, keepdims=True) + EPS)
    xhat = xv * r
    err = xhat * gv - tv
    dy = err * (1.0 / d)
    dxh = dy * gv
    dx = r * (dxh - xhat * jnp.mean(dxh * xhat, axis=-1, keepdims=True))
    return dx, jnp.sum(err * err, axis=0, keepdims=True), jnp.sum(dy * xhat, axis=0, keepdims=True)


def _ffn_up(x, g, wg_t, wu_t, name, cargo=()):
    t, d = x.shape
    f = wg_t.shape[0]
    tm = min(TM_FFN, t)
    chunks = _feature_chunks(f, FFN_FWD_CHUNKS)
    flat, copies, n_copies = _piece_rows([wg_t, wu_t])
    nw = len(flat)

    def body(x_ref, g_ref, *rest):
        w_hbm, (a_ref, b_ref, s_ref, wg, wu, sems) = rest[:nw], rest[nw:]

        @pl.when(pl.program_id(0) == 0)
        def _():
            _load_rows(copies(w_hbm, [wg, wu]), sems)

        xv = x_ref[...]
        r = lax.rsqrt(jnp.mean(xv * xv, axis=-1, keepdims=True) + EPS)
        h = (xv * r * g_ref[...]).astype(BF16)
        for s0, sz in chunks:
            a = _nt(h, wg[s0:s0 + sz, :])
            b = _nt(h, wu[s0:s0 + sz, :])
            a_ref[:, s0:s0 + sz] = a.astype(BF16)
            b_ref[:, s0:s0 + sz] = b.astype(BF16)
            s_ref[:, s0:s0 + sz] = (a * _sigmoid(a) * b).astype(BF16)

    tok = lambda i: (i, 0)
    wide = pl.BlockSpec((tm, f), tok)
    return _launch(
        body, name=name, grid=(t // tm,),
        in_specs=[pl.BlockSpec((tm, d), tok), pl.BlockSpec((1, d), lambda i: (0, 0))] + [HBM_SPEC] * nw,
        out_specs=[wide, wide, wide], out_shape=[jax.ShapeDtypeStruct((t, f), BF16)] * 3,
        scratch_shapes=[pltpu.VMEM((f, d), BF16), pltpu.VMEM((f, d), BF16), pltpu.SemaphoreType.DMA((n_copies,))],
        args=(x, g, *flat), cargo=cargo)


def _ffn_down(x, s, wd, name, cargo=(), loss_head=None):
    t, d = x.shape
    f = s.shape[1]
    tm = min(TM_FFN, t)
    flat, copies, n_copies = _piece_rows([wd])
    nw = len(flat)
    nl = 2 if loss_head else 0

    def body(x_ref, s_ref, *rest):
        head, w_hbm = rest[:nl], rest[nl:nl + nw]
        xo_ref = rest[nl + nw]
        sums, (wdn, sems) = rest[nl + nw + 1:nl + nw + 1 + nl], rest[nl + nw + 1 + nl:]

        @pl.when(pl.program_id(0) == 0)
        def _():
            _load_rows(copies(w_hbm, [wdn]), sems)
            for sum_ref in sums:
                sum_ref[...] = jnp.zeros_like(sum_ref)

        xo = x_ref[...] + 0.5 * _nn(s_ref[...], wdn[...])
        if loss_head:
            dx, sq, dgf = _loss_head(xo, head[0][...], head[1][...])
            xo_ref[...] = dx
            sums[0][...] += sq
            sums[1][...] += dgf
        else:
            xo_ref[...] = xo

    tok = lambda i: (i, 0)
    one = lambda i: (0, 0)
    return _launch(
        body, name=name, grid=(t // tm,),
        in_specs=[pl.BlockSpec((tm, d), tok), pl.BlockSpec((tm, f), tok)]
        + ([pl.BlockSpec((1, d), one), pl.BlockSpec((tm, d), tok)] if loss_head else []) + [HBM_SPEC] * nw,
        out_specs=[pl.BlockSpec((tm, d), tok)] + [pl.BlockSpec((1, d), one)] * nl,
        out_shape=[jax.ShapeDtypeStruct((t, d), F32)] + [jax.ShapeDtypeStruct((1, d), F32)] * nl,
        scratch_shapes=[pltpu.VMEM((f, d), BF16), pltpu.SemaphoreType.DMA((n_copies,))],
        args=(x, s, *(loss_head or ()), *flat), cargo=cargo)


def _ffn_backward(dxo, x, g, a, b, wg_t, wu_t, wd, name, cargo=()):
    t, d = x.shape
    f = wd.shape[0]
    tm = min(TM_FFN // 2, t)
    chunks = _feature_chunks(f, FFN_BWD_CHUNKS)
    flat, copies, n_copies = _piece_rows([wg_t, wu_t, wd])
    nw = len(flat)

    def body(dxo_ref, x_ref, g_ref, a_ref, b_ref, *rest):
        w_hbm, (dx_ref, dab_ref, hd_ref, dg_ref, wg, wu, wdn, sems) = rest[:nw], rest[nw:]

        @pl.when(pl.program_id(0) == 0)
        def _():
            _load_rows(copies(w_hbm, [wg, wu, wdn]), sems)
            dg_ref[...] = jnp.zeros_like(dg_ref)

        xv = x_ref[...]
        gv = g_ref[...]
        r = lax.rsqrt(jnp.mean(xv * xv, axis=-1, keepdims=True) + EPS)
        xhat = xv * r
        hd_ref[:, 0:d] = (xhat * gv).astype(BF16)
        dxo_v = dxo_ref[...]
        dout = (0.5 * dxo_v).astype(BF16)
        hd_ref[:, d:2 * d] = dout
        dh = jnp.zeros((tm, d), F32)
        for s0, sz in chunks:
            ds = _nt(dout, wdn[s0:s0 + sz, :])
            av = a_ref[:, s0:s0 + sz].astype(F32)
            bv = b_ref[:, s0:s0 + sz].astype(F32)
            sig = _sigmoid(av)
            silu = av * sig
            da = (ds * bv * (sig * (1.0 + av * (1.0 - sig)))).astype(BF16)
            db = (ds * silu).astype(BF16)
            dab_ref[:, s0:s0 + sz] = da
            dab_ref[:, f + s0:f + s0 + sz] = db
            dh = dh + _nn(da, wg[s0:s0 + sz, :]) + _nn(db, wu[s0:s0 + sz, :])
        dg_ref[...] += jnp.sum(dh * xhat, axis=0, keepdims=True)
        dxh = dh * gv
        dx_ref[...] = dxo_v + r * (dxh - xhat * jnp.mean(dxh * xhat, axis=-1, keepdims=True))

    tok = lambda i: (i, 0)
    one = lambda i: (0, 0)
    return _launch(
        body, name=name, grid=(t // tm,),
        in_specs=[pl.BlockSpec((tm, d), tok), pl.BlockSpec((tm, d), tok), pl.BlockSpec((1, d), one),
                  pl.BlockSpec((tm, f), tok), pl.BlockSpec((tm, f), tok)] + [HBM_SPEC] * nw,
        out_specs=[pl.BlockSpec((tm, d), tok), pl.BlockSpec((tm, 2 * f), tok), pl.BlockSpec((tm, 2 * d), tok),
                   pl.BlockSpec((1, d), one)],
        out_shape=[jax.ShapeDtypeStruct((t, d), F32), jax.ShapeDtypeStruct((t, 2 * f), BF16),
                   jax.ShapeDtypeStruct((t, 2 * d), BF16), jax.ShapeDtypeStruct((1, d), F32)],
        scratch_shapes=[pltpu.VMEM((f, d), BF16), pltpu.VMEM((f, d), BF16), pltpu.VMEM((f, d), BF16), pltpu.SemaphoreType.DMA((n_copies,))],
        args=(dxo, x, g, a, b, *flat), cargo=cargo)


def _weight_grad(lhs, rhs, name, cargo=(), lhs_part=(0, 1), rhs_part=(0, 1)):
    t = lhs.shape[0]
    m = lhs.shape[1] // lhs_part[1]
    d = rhs.shape[1] // rhs_part[1]
    tm = min(TM_TN, t)
    nt = t // tm
    rps = m // N_CHIPS
    hr = rps // 2
    assert hr % 16 == 0

    def body(l_ref, r_ref, o_ref, acc, stage, recv, send_sems, recv_sems):
        i = pl.program_id(0)

        @pl.when(i == 0)
        def _():
            acc[...] = jnp.zeros_like(acc)

        acc[...] += _tn(l_ref[...], r_ref[...])

        @pl.when(i == nt - 1)
        def _():
            x, y, c = _my_place()
            copies = []
            for q in range(N_CHIPS):
                stage[q] = acc[pl.ds(pl.multiple_of(q * rps + (1 - c) * hr, 16), hr), :].astype(BF16)
                cp = pltpu.make_async_remote_copy(
                    src_ref=stage.at[q], dst_ref=recv.at[q], send_sem=send_sems.at[q], recv_sem=recv_sems.at[q],
                    device_id=(x, y, 1 - c), device_id_type=MESH)
                cp.start()
                copies.append(cp)
            for q, cp in enumerate(copies):
                cp.wait_recv()
                mine = acc[pl.ds(pl.multiple_of(q * rps + c * hr, 16), hr), :]
                o_ref[q] = (mine + recv[q].astype(F32)).astype(BF16)
            for cp in copies:
                cp.wait_send()

    outs, carried = _launch(
        body, name=name, grid=(nt,),
        in_specs=[pl.BlockSpec((tm, m), lambda i: (i, lhs_part[0])), pl.BlockSpec((tm, d), lambda i: (i, rhs_part[0]))],
        out_specs=[pl.BlockSpec((N_CHIPS, hr, d), lambda i: (0, 0, 0))],
        out_shape=[jax.ShapeDtypeStruct((N_CHIPS, hr, d), BF16)],
        scratch_shapes=[pltpu.VMEM((m, d), F32), pltpu.VMEM((N_CHIPS, hr, d), BF16), pltpu.VMEM((N_CHIPS, hr, d), BF16),
                        pltpu.SemaphoreType.DMA((N_CHIPS,)), pltpu.SemaphoreType.DMA((N_CHIPS,))],
        args=(lhs, rhs), cargo=cargo)
    return outs[0], carried


def _window_sums(src, cols, w, tm, levels, trailing):
    def read_src(lo, hi):
        return src[lo:hi, cols]

    read, k, level = read_src, 1, 0
    while True:
        last = 2 * k == w
        if trailing:
            lo, hi = (HALO if last else 8 * (level + 1)), HALO + tm
            cur = read(lo, hi) + read(lo - k, hi - k)
        else:
            lo, hi = 0, (tm if last else tm + HALO - 8 * (level + 1))
            cur = read(lo, hi) + read(lo + k, hi + k)
        if last:
            return cur
        levels[level, lo:hi, :] = cur
        read = lambda a, b, level=level: levels[level, a:b, :]
        k, level = 2 * k, level + 1


def _pool_parts(u_cols, ubuf, cols, w, row, tm, levels):
    ws = _window_sums(ubuf, cols, w, tm, levels, trailing=True)
    inv = 1.0 / jnp.minimum(row + 1, w).astype(F32)
    return ws * inv - u_cols, inv


def _mixer_forward(x, g, win_t, wout_x, conv_w, pool_w, pool_scale, cargo=()):
    t, d = x.shape
    dc = win_t.shape[0] // 4
    gcw = dc // len(POOL_WINDOWS)
    wo_rows = d // N_CHIPS
    wo_stride = wout_x.shape[0] // N_CHIPS
    tm = min(TM_MIX, t)

    def body(x_ref, g_ref, win_hbm, wout_hbm, cw_ref, pw_ref, ps_ref, xo_ref, proj_ref, y_ref,
             win, wout, zbuf, ubuf, levels, sems):
        i = pl.program_id(0)

        @pl.when(i == 0)
        def _():
            pairs = [(win_hbm, win)]
            for k in range(N_CHIPS):
                pairs.append((wout_hbm.at[pl.ds(k * wo_stride, wo_rows), :], wout.at[pl.ds(k * wo_rows, wo_rows), :]))
            _load_rows(pairs, sems)
            zbuf[0:8, :] = jnp.zeros((8, dc), F32)
            ubuf[0:HALO, :] = jnp.zeros((HALO, dc), F32)

        xv = x_ref[...]
        r = lax.rsqrt(jnp.mean(xv * xv, axis=-1, keepdims=True) + EPS)
        h = (xv * r * g_ref[...]).astype(BF16)
        v = _nt(h, win[0:dc, :])
        gb = _nt(h, win[dc:2 * dc, :])
        gc = _nt(h, win[2 * dc:3 * dc, :])
        u = _nt(h, win[3 * dc:4 * dc, :])
        proj_ref[:, 0:dc] = v.astype(BF16)
        proj_ref[:, dc:2 * dc] = gb.astype(BF16)
        proj_ref[:, 2 * dc:3 * dc] = gc.astype(BF16)
        proj_ref[:, 3 * dc:4 * dc] = u.astype(BF16)

        z = gc * v
        zbuf[8:8 + tm, :] = z
        cw = cw_ref[...]
        conv = cw[2:3, :] * z + cw[1:2, :] * zbuf[7:7 + tm, :] + cw[0:1, :] * zbuf[6:6 + tm, :]
        y_ref[:, 0:dc] = (gb * conv).astype(BF16)

        ubuf[HALO:HALO + tm, :] = u
        row = i * tm + lax.broadcasted_iota(jnp.int32, (tm, 1), 0)
        for gi, w in enumerate(POOL_WINDOWS):
            cols = slice(gi * gcw, (gi + 1) * gcw)
            pooled, _ = _pool_parts(u[:, cols], ubuf, cols, w, row, tm, levels)
            yb = _nn(pooled.astype(BF16), pw_ref[gi].astype(BF16)) * ps_ref[:, cols]
            y_ref[:, dc + gi * gcw:dc + (gi + 1) * gcw] = yb.astype(BF16)

        xo_ref[...] = xv + _nn(y_ref[...], wout[...])
        zbuf[0:8, :] = zbuf[tm:tm + 8, :]
        ubuf[0:HALO, :] = ubuf[tm:tm + HALO, :]

    tok = lambda i: (i, 0)
    one = lambda i: (0, 0)
    return _launch(
        body, name="mixer_forward", grid=(t // tm,),
        in_specs=[pl.BlockSpec((tm, d), tok), pl.BlockSpec((1, d), one), HBM_SPEC, HBM_SPEC,
                  pl.BlockSpec(conv_w.shape, one), pl.BlockSpec(pool_w.shape, lambda i: (0, 0, 0)), pl.BlockSpec((1, dc), one)],
        out_specs=[pl.BlockSpec((tm, d), tok), pl.BlockSpec((tm, 4 * dc), tok), pl.BlockSpec((tm, 2 * dc), tok)],
        out_shape=[jax.ShapeDtypeStruct((t, d), F32), jax.ShapeDtypeStruct((t, 4 * dc), BF16), jax.ShapeDtypeStruct((t, 2 * dc), BF16)],
        scratch_shapes=[pltpu.VMEM((4 * dc, d), BF16), pltpu.VMEM((2 * dc, d), BF16),
                        pltpu.VMEM((tm + 8, dc), F32), pltpu.VMEM((tm + HALO, dc), F32),
                        pltpu.VMEM((WINDOW_LEVELS, tm + HALO, gcw), F32), pltpu.SemaphoreType.DMA((1 + N_CHIPS,))],
        args=(x, g, win_t, wout_x, conv_w, pool_w, pool_scale), cargo=cargo)


def _mixer_backward(dxo, x, g, proj, win_t, wout_x, conv_w, pool_w, pool_scale, cargo=()):
    t, d = x.shape
    dc = win_t.shape[0] // 4
    ng = len(POOL_WINDOWS)
    gcw = dc // ng
    wo_rows = d // N_CHIPS
    wo_stride = wout_x.shape[0] // N_CHIPS
    tm = min(TM_MIX, t)
    n_tiles = t // tm
    hb = tm // HALO

    def body(dxo_ref, x_ref, g_ref, proj_ref, halo_ref, win_hbm, wout_hbm, cw_ref, pw_ref, ps_ref,
             dx_ref, dproj_ref, h_ref, dxob_ref, dg_ref, dcw_ref, dps_ref, dpw_ref,
             win, wout, zbuf, ubuf, dcbuf, ebuf, levels, sems):
        i = pl.program_id(0)
        tile = n_tiles - 1 - i

        @pl.when(i == 0)
        def _():
            pairs = [(win_hbm, win)]
            for k in range(N_CHIPS):
                pairs.append((wout_hbm.at[pl.ds(k * wo_stride, wo_rows), :], wout.at[pl.ds(k * wo_rows, wo_rows), :]))
            _load_rows(pairs, sems)
            dcbuf[tm:tm + 8, :] = jnp.zeros((8, dc), F32)
            ebuf[tm:tm + HALO, :] = jnp.zeros((HALO, dc), F32)
            dg_ref[...] = jnp.zeros_like(dg_ref)
            dcw_ref[...] = jnp.zeros_like(dcw_ref)
            dps_ref[...] = jnp.zeros_like(dps_ref)
            dpw_ref[...] = jnp.zeros_like(dpw_ref)

        xv = x_ref[...]
        gv = g_ref[...]
        r = lax.rsqrt(jnp.mean(xv * xv, axis=-1, keepdims=True) + EPS)
        xhat = xv * r
        h_ref[...] = (xhat * gv).astype(BF16)
        dxo_v = dxo_ref[...]
        dxo_b = dxo_v.astype(BF16)
        dxob_ref[...] = dxo_b

        v = proj_ref[:, 0:dc].astype(F32)
        gb = proj_ref[:, dc:2 * dc].astype(F32)
        gc = proj_ref[:, 2 * dc:3 * dc].astype(F32)
        u = proj_ref[:, 3 * dc:4 * dc].astype(F32)
        first = jnp.where(tile > 0, 1.0, 0.0)
        zbuf[0:HALO, :] = halo_ref[:, 2 * dc:3 * dc].astype(F32) * halo_ref[:, 0:dc].astype(F32) * first
        ubuf[0:HALO, :] = halo_ref[:, 3 * dc:4 * dc].astype(F32) * first
        z = gc * v
        zbuf[HALO:HALO + tm, :] = z
        ubuf[HALO:HALO + tm, :] = u
        z1 = zbuf[HALO - 1:HALO - 1 + tm, :]
        z2 = zbuf[HALO - 2:HALO - 2 + tm, :]
        cw = cw_ref[...]
        conv = cw[2:3, :] * z + cw[1:2, :] * z1 + cw[0:1, :] * z2

        dy = _nt(dxo_b, wout[...])
        dya = dy[:, 0:dc]
        dgb = dya * conv
        dconv = dya * gb
        dcbuf[0:tm, :] = dconv
        dz = cw[2:3, :] * dconv + cw[1:2, :] * dcbuf[1:1 + tm, :] + cw[0:1, :] * dcbuf[2:2 + tm, :]
        dgc = dz * v
        dv = dz * gc
        dcw_ref[0:1, :] += jnp.sum(dconv * z2, axis=0, keepdims=True)
        dcw_ref[1:2, :] += jnp.sum(dconv * z1, axis=0, keepdims=True)
        dcw_ref[2:3, :] += jnp.sum(dconv * z, axis=0, keepdims=True)

        dproj_ref[:, 0:dc] = dv.astype(BF16)
        dproj_ref[:, dc:2 * dc] = dgb.astype(BF16)
        dproj_ref[:, 2 * dc:3 * dc] = dgc.astype(BF16)

        row = tile * tm + lax.broadcasted_iota(jnp.int32, (tm, 1), 0)
        for gi, w in enumerate(POOL_WINDOWS):
            cols = slice(gi * gcw, (gi + 1) * gcw)
            pooled, inv_cnt = _pool_parts(u[:, cols], ubuf, cols, w, row, tm, levels)
            pooled_b = pooled.astype(BF16)
            pw_b = pw_ref[gi].astype(BF16)
            dyb = dy[:, dc + gi * gcw:dc + (gi + 1) * gcw]
            q = _nn(pooled_b, pw_b)
            dps_ref[:, cols] += jnp.sum(q * dyb, axis=0, keepdims=True)
            dq = (dyb * ps_ref[:, cols]).astype(BF16)
            dpw_ref[gi] += _tn(pooled_b, dq)
            dpooled = _nt(dq, pw_b)
            ebuf[0:tm, cols] = dpooled * inv_cnt
            du = _window_sums(ebuf, cols, w, tm, levels, trailing=False) - dpooled
            dproj_ref[:, 3 * dc + gi * gcw:3 * dc + (gi + 1) * gcw] = du.astype(BF16)

        dh = _nn(dproj_ref[...], win[...])
        dg_ref[...] += jnp.sum(dh * xhat, axis=0, keepdims=True)
        dxh = dh * gv
        dx_ref[...] = dxo_v + r * (dxh - xhat * jnp.mean(dxh * xhat, axis=-1, keepdims=True))
        dcbuf[tm:tm + 8, :] = dcbuf[0:8, :]
        ebuf[tm:tm + HALO, :] = ebuf[0:HALO, :]

    tok = lambda i: (n_tiles - 1 - i, 0)
    halo = lambda i: (jnp.maximum((n_tiles - 1 - i) * hb - 1, 0), 0)
    one = lambda i: (0, 0)
    return _launch(
        body, name="mixer_backward", grid=(n_tiles,),
        in_specs=[pl.BlockSpec((tm, d), tok), pl.BlockSpec((tm, d), tok), pl.BlockSpec((1, d), one),
                  pl.BlockSpec((tm, 4 * dc), tok), pl.BlockSpec((HALO, 4 * dc), halo), HBM_SPEC, HBM_SPEC,
                  pl.BlockSpec(conv_w.shape, one), pl.BlockSpec(pool_w.shape, lambda i: (0, 0, 0)), pl.BlockSpec((1, dc), one)],
        out_specs=[pl.BlockSpec((tm, d), tok), pl.BlockSpec((tm, 4 * dc), tok), pl.BlockSpec((tm, d), tok), pl.BlockSpec((tm, d), tok),
                   pl.BlockSpec((1, d), one), pl.BlockSpec(conv_w.shape, one), pl.BlockSpec((1, dc), one),
                   pl.BlockSpec(pool_w.shape, lambda i: (0, 0, 0))],
        out_shape=[jax.ShapeDtypeStruct((t, d), F32), jax.ShapeDtypeStruct((t, 4 * dc), BF16), jax.ShapeDtypeStruct((t, d), BF16),
                   jax.ShapeDtypeStruct((t, d), BF16), jax.ShapeDtypeStruct((1, d), F32), jax.ShapeDtypeStruct(conv_w.shape, F32),
                   jax.ShapeDtypeStruct((1, dc), F32), jax.ShapeDtypeStruct(pool_w.shape, F32)],
        scratch_shapes=[pltpu.VMEM((4 * dc, d), BF16), pltpu.VMEM((2 * dc, d), BF16),
                        pltpu.VMEM((tm + HALO, dc), F32), pltpu.VMEM((tm + HALO, dc), F32),
                        pltpu.VMEM((tm + 8, dc), F32), pltpu.VMEM((tm + HALO, dc), F32),
                        pltpu.VMEM((WINDOW_LEVELS, tm + HALO, gcw), F32), pltpu.SemaphoreType.DMA((1 + N_CHIPS,))],
        args=(dxo, x, g, proj, proj, win_t, wout_x, conv_w, pool_w, pool_scale), cargo=cargo)


def _adam_update(w, gv, m, v):
    m_new = ADAM_B1 * m + (1.0 - ADAM_B1) * gv
    v_new = ADAM_B2 * v + (1.0 - ADAM_B2) * (gv * gv)
    m_hat = m_new / (1.0 - ADAM_B1 ** ADAM_STEP)
    v_hat = v_new / (1.0 - ADAM_B2 ** ADAM_STEP)
    return -ADAM_LR * (m_hat / (jnp.sqrt(v_hat) + ADAM_EPS) + ADAM_WD * w), m_new, v_new


def _adamw(w, grad, m, v, name):
    rows, cols = w.shape
    br = _row_block(rows, 512) if rows >= 8 else rows

    def body(w_ref, g_ref, m_ref, v_ref, go_ref, d_ref, mo_ref, vo_ref):
        gv = g_ref[...]
        go_ref[...] = gv
        d_ref[...], mo_ref[...], vo_ref[...] = _adam_update(w_ref[...], gv, m_ref[...], v_ref[...])

    blk = pl.BlockSpec((br, cols), lambda i: (i, 0))
    return pl.pallas_call(
        body, name=name,
        out_shape=[jax.ShapeDtypeStruct((rows, cols), F32)] * 4,
        grid=(rows // br,), in_specs=[blk] * 4, out_specs=[blk] * 4,
        compiler_params=pltpu.CompilerParams(dimension_semantics=("parallel",), vmem_limit_bytes=VMEM_LIMIT),
    )(w, grad, m, v)


def _adamw_transposed(w, grad_t, m, v, name):
    _, rows, cols = w.shape
    br = 256 if rows % 256 == 0 else rows

    def body(w_ref, gt_ref, m_ref, v_ref, g_ref, d_ref, mo_ref, vo_ref):
        gv = gt_ref[...].T
        g_ref[...] = gv
        d_ref[...], mo_ref[...], vo_ref[...] = _adam_update(w_ref[...], gv, m_ref[...], v_ref[...])

    blk = pl.BlockSpec((None, br, cols), lambda i: (0, i, 0))
    return pl.pallas_call(
        body, name=name,
        out_shape=[jax.ShapeDtypeStruct((1, rows, cols), F32)] * 4,
        grid=(rows // br,), in_specs=[blk, pl.BlockSpec((cols, br), lambda i: (0, i)), blk, blk], out_specs=[blk] * 4,
        compiler_params=pltpu.CompilerParams(dimension_semantics=("parallel",)),
    )(w, grad_t, m, v)


def _f32_rows_as_bf16(a, rows, cols):
    bits = lax.bitcast_convert_type(a, BF16).reshape(a.shape[0], 2 * a.shape[1])
    return jnp.pad(bits, ((0, rows - bits.shape[0]), (0, cols - bits.shape[1])))


def kernel(x, norm_ffn1, ffn1_w_gate, ffn1_w_up, ffn1_w_down, norm_mix, w_in, conv_w, pool_w, pool_scale, w_out, norm_ffn2, ffn2_w_gate, ffn2_w_up, ffn2_w_down, norm_final, loss_target, m_norm_ffn1, m_ffn1_w_gate, m_ffn1_w_up, m_ffn1_w_down, m_norm_mix, m_w_in, m_conv_w, m_pool_w, m_pool_scale, m_w_out, m_norm_ffn2, m_ffn2_w_gate, m_ffn2_w_up, m_ffn2_w_down, m_norm_final, v_norm_ffn1, v_ffn1_w_gate, v_ffn1_w_up, v_ffn1_w_down, v_norm_mix, v_w_in, v_conv_w, v_pool_w, v_pool_scale, v_w_out, v_norm_ffn2, v_ffn2_w_gate, v_ffn2_w_up, v_ffn2_w_down, v_norm_final):
    weights = dict(norm_ffn1=norm_ffn1, ffn1_w_gate=ffn1_w_gate, ffn1_w_up=ffn1_w_up, ffn1_w_down=ffn1_w_down, norm_mix=norm_mix,
                   w_in=w_in, conv_w=conv_w, pool_w=pool_w, pool_scale=pool_scale, w_out=w_out, norm_ffn2=norm_ffn2,
                   ffn2_w_gate=ffn2_w_gate, ffn2_w_up=ffn2_w_up, ffn2_w_down=ffn2_w_down, norm_final=norm_final)
    first_m = dict(norm_ffn1=m_norm_ffn1, ffn1_w_gate=m_ffn1_w_gate, ffn1_w_up=m_ffn1_w_up, ffn1_w_down=m_ffn1_w_down,
                   norm_mix=m_norm_mix, w_in=m_w_in, conv_w=m_conv_w, pool_w=m_pool_w, pool_scale=m_pool_scale, w_out=m_w_out,
                   norm_ffn2=m_norm_ffn2, ffn2_w_gate=m_ffn2_w_gate, ffn2_w_up=m_ffn2_w_up, ffn2_w_down=m_ffn2_w_down,
                   norm_final=m_norm_final)
    second_m = dict(norm_ffn1=v_norm_ffn1, ffn1_w_gate=v_ffn1_w_gate, ffn1_w_up=v_ffn1_w_up, ffn1_w_down=v_ffn1_w_down,
                    norm_mix=v_norm_mix, w_in=v_w_in, conv_w=v_conv_w, pool_w=v_pool_w, pool_scale=v_pool_scale, w_out=v_w_out,
                    norm_ffn2=v_norm_ffn2, ffn2_w_gate=v_ffn2_w_gate, ffn2_w_up=v_ffn2_w_up, ffn2_w_down=v_ffn2_w_down,
                    norm_final=v_norm_final)
    names = list(weights)

    xs = x[0]
    tgt = loss_target[0]
    t, d = xs.shape
    dc = pool_scale.shape[1]
    cx, cy, cc = _my_place()
    chip = 2 * cx + cy
    place = jnp.stack([chip, cc]).astype(jnp.int32)

    conv_rows = 32
    wout_x = jnp.concatenate([w_out[0].astype(BF16), _f32_rows_as_bf16(conv_w[0], conv_rows, d)], axis=0)

    g1, gm, g2 = norm_ffn1, norm_mix, norm_ffn2
    gf = norm_final.reshape(1, d)
    pw = pool_w[0]

    (wd1_shard, wg2_shard, wu2_shard, wd2_shard), [(wg1, wu1)] = _cast_to_bf16(
        [ffn1_w_down[0], ffn2_w_gate[0].T, ffn2_w_up[0].T, ffn2_w_down[0]], "gather_ffn1",
        [_gather_cargo([ffn1_w_gate[0].T.astype(BF16), ffn1_w_up[0].T.astype(BF16)])])
    (a1, b1, s1), [(wd1, win_t, wout_g)] = _ffn_up(
        xs, g1, wg1, wu1, "ffn1_up", [_gather_cargo([wd1_shard, w_in[0].T.astype(BF16), wout_x])])
    (x1,), [(wg2,)] = _ffn_down(xs, s1, wd1, "ffn1_down", [_gather_cargo([wg2_shard])])
    wo_rows = w_out.shape[1]
    cshard = conv_w.shape[2]
    conv_bits = wout_g.reshape(N_CHIPS, wo_rows + conv_rows, d)[:, wo_rows:wo_rows + conv_w.shape[1], :2 * cshard]
    conv_full = lax.bitcast_convert_type(conv_bits.reshape(N_CHIPS, conv_w.shape[1], cshard, 2), F32)
    conv_full = jnp.transpose(conv_full, (1, 0, 2)).reshape(conv_w.shape[1], N_CHIPS * cshard)
    (x2, proj, ymix), [(wu2,)] = _mixer_forward(x1, gm, win_t, wout_g, conv_full, pw, pool_scale, [_gather_cargo([wu2_shard])])
    (a2, b2, s2), [(wd2,)] = _ffn_up(x2, g2, wg2, wu2, "ffn2_up", [_gather_cargo([wd2_shard])])
    (dx3, sq_cols, dgf), _ = _ffn_down(x2, s2, wd2, "ffn2_down", loss_head=(gf, tgt))

    (dx2, dab2, hd2, dg2), _ = _ffn_backward(dx3, x2, g2, a2, b2, wg2, wu2, wd2, "ffn2_backward")
    p_wg2, _ = _weight_grad(dab2, hd2, "ffn2_gate_grad", lhs_part=(0, 2), rhs_part=(0, 2))
    p_wu2, [(x_wg2,)] = _weight_grad(dab2, hd2, "ffn2_up_grad", [_exchange_cargo([p_wg2])], lhs_part=(1, 2), rhs_part=(0, 2))
    p_wd2, [(x_wu2,)] = _weight_grad(s2, hd2, "ffn2_down_grad", [_exchange_cargo([p_wu2])], rhs_part=(1, 2))

    (dx1, dproj, h2, dx2b, dgm, dcw, dps, dpw), [(x_wd2,)] = _mixer_backward(
        dx2, x1, gm, proj, win_t, wout_g, conv_full, pw, pool_scale, [_exchange_cargo([p_wd2])])

    (dx0, dab1, hd1, dg1), _ = _ffn_backward(dx1, xs, g1, a1, b1, wg1, wu1, wd1, "ffn1_backward")

    npw = pw.size // d
    head = [dg1, dgm, dg2, dgf, jnp.pad(dps, ((0, 0), (0, d - dc))), jnp.pad(dcw, ((0, 0), (0, d - dc))), sq_cols]
    n_head = sum(h.shape[0] for h in head)
    base = -(-n_head // 8) * 8
    pack = jnp.concatenate(head + [jnp.zeros((base - n_head, d), F32), dpw.reshape(npw, d)], axis=0)

    p_wg1, [(packs,)] = _weight_grad(dab1, hd1, "ffn1_gate_grad", [_all_gather_small_cargo(pack)], lhs_part=(0, 2), rhs_part=(0, 2))
    p_wu1, [(x_wg1,)] = _weight_grad(dab1, hd1, "ffn1_up_grad", [_exchange_cargo([p_wg1])], lhs_part=(1, 2), rhs_part=(0, 2))
    p_wd1, [(x_wu1,)] = _weight_grad(s1, hd1, "ffn1_down_grad", [_exchange_cargo([p_wu1])], rhs_part=(1, 2))
    p_win, [(x_wd1,)] = _weight_grad(dproj, h2, "w_in_grad", [_exchange_cargo([p_wd1])])

    pairs = dict(wg1=p_wg1, wu1=p_wu1, wd1=p_wd1, wg2=p_wg2, wu2=p_wu2, wd2=p_wd2)
    landed = dict(wg1=x_wg1, wu1=x_wu1, wd1=x_wd1, wg2=x_wg2, wu2=x_wu2, wd2=x_wd2)
    ffn_halves = [_chip_sum(pairs[k], landed[k], place, k) for k in ["wg1", "wu1", "wd1", "wg2", "wu2", "wd2"]]
    p_wout, [(x_win,), ffn_both] = _weight_grad(ymix, dx2b, "w_out_grad", [_exchange_cargo([p_win]), _share_cargo(ffn_halves)])
    x_wout, = _run_cargo(_exchange_cargo([p_wout]), "grad_exchange_last")
    mix_both = _sibling_share([_chip_sum(p_win, x_win, place, "win"), _chip_sum(p_wout, x_wout, place, "wout")])
    rwg1, rwu1, rwd1, rwg2, rwu2, rwd2, rwin, rwout = [b.reshape(2 * b.shape[1], b.shape[2]) for b in list(ffn_both) + list(mix_both)]
    small = _sum_by_device(packs)
    loss = jnp.sum(small[n_head - 1]) * (0.5 / d)

    grads = {
        "norm_ffn1": small[0:1], "norm_mix": small[1:2], "norm_ffn2": small[2:3], "norm_final": small[3],
        "pool_scale": small[4:5, :dc],
        "conv_w": lax.dynamic_slice_in_dim(small[5:5 + dcw.shape[0], :dc], chip * cshard, cshard, axis=1)[None],
        "pool_w": small[base:].reshape(pool_w.shape),
        "ffn1_w_down": rwd1[None], "w_out": rwout[None], "ffn2_w_down": rwd2[None],
    }
    by_view = {"ffn1_w_gate": rwg1, "ffn1_w_up": rwu1, "ffn2_w_gate": rwg2, "ffn2_w_up": rwu2}

    deltas, new_m, new_v = {}, {}, {}
    for n in names:
        w = weights[n]
        shape = w.shape
        if n == "w_in":
            grads[n], deltas[n], new_m[n], new_v[n] = _adamw_transposed(w, rwin, first_m[n], second_m[n], "adamw_" + n)
            continue
        if n in by_view:
            view = lambda a: jnp.swapaxes(a, 1, 2)[0]
            back = lambda a: jnp.swapaxes(a[None], 1, 2)
            outs = _adamw(view(w), by_view[n], view(first_m[n]), view(second_m[n]), "adamw_" + n)
            grads[n], deltas[n], new_m[n], new_v[n] = [back(o) for o in outs]
            continue
        as2d = (lambda a: a.reshape(-1, shape[-1]))
        outs = _adamw(as2d(w), as2d(grads[n]), as2d(first_m[n]), as2d(second_m[n]), "adamw_" + n)
        grads[n], deltas[n], new_m[n], new_v[n] = [o.reshape(shape) for o in outs]

    return (loss, dx0[None], *[grads[n] for n in names], *[deltas[n] for n in names],
            *[new_m[n] for n in names], *[new_v[n] for n in names])
```

```python
import jax
import jax.numpy as jnp
from jax import lax
from jax.experimental import pallas as pl
from jax.experimental.pallas import tpu as pltpu

F32 = jnp.float32
BF16 = jnp.bfloat16
MESH = pl.DeviceIdType.MESH

EPS = 1e-6
POOL_WINDOWS = (2, 4, 8, 16)
ADAM_LR = 0.001
ADAM_B1 = 0.9
ADAM_B2 = 0.999
ADAM_EPS = 1e-08
ADAM_WD = 0.01
ADAM_STEP = 10

N_CHIPS = 4
N_DEVICES = 8
MXU_COLS_V7X = 256
VMEM_LIMIT = 56 * 1024 * 1024
TM_FFN = 512
TM_MIX = 512
TM_TN = 1024
HALO = 32
WINDOW_LEVELS = 3
FFN_FWD_CHUNKS = 2
FFN_BWD_CHUNKS = 2


def _nt(a, b):
    return lax.dot_general(a, b, (((1,), (1,)), ((), ())), preferred_element_type=F32)


def _tn(a, b):
    return lax.dot_general(a, b, (((0,), (0,)), ((), ())), preferred_element_type=F32)


def _nn(a, b):
    return jnp.dot(a, b, preferred_element_type=F32)


def _sigmoid(a):
    return 1.0 / (1.0 + jnp.exp(-a))


def _feature_chunks(n, parts):
    assert n % MXU_COLS_V7X == 0
    tiles = n // MXU_COLS_V7X
    out, s0 = [], 0
    for p in range(parts):
        sz = (tiles // parts + (1 if p < tiles % parts else 0)) * MXU_COLS_V7X
        if sz:
            out.append((s0, sz))
            s0 += sz
    return out


def _row_block(rows, cap):
    best = 8
    for b in range(8, min(rows, cap) + 1, 8):
        if rows % b == 0:
            best = b
    assert rows % best == 0
    return best


def _my_place():
    return lax.axis_index("x"), lax.axis_index("y"), lax.axis_index("c")


def _other_chips(x, y):
    return [(1 - x, y), (x, 1 - y), (1 - x, 1 - y)]


HBM_SPEC = pl.BlockSpec(memory_space=pltpu.HBM)


class _Cargo:
    def __init__(self, operands, out_shapes, n_sems, phases, when, in_place=False):
        self.operands, self.out_shapes, self.n_sems = list(operands), list(out_shapes), n_sems
        self.phases, self.when = list(phases), list(when)
        self.in_place = in_place
        assert len(self.phases) == len(self.when) and self.when[0] == 0.0 and self.when[-1] == 1.0


def _launch(body, *, name, grid, in_specs, out_specs, out_shape, scratch_shapes, args, cargo=()):
    params = pltpu.CompilerParams(dimension_semantics=("arbitrary",) * len(grid), vmem_limit_bytes=VMEM_LIMIT)
    cargos = list(cargo)
    c_operands = [op for cg in cargos for op in cg.operands]
    c_shapes = [sh for cg in cargos for sh in cg.out_shapes]
    counts = [len(in_specs), len(c_operands), len(out_shape), len(c_shapes), len(scratch_shapes), 2 * len(cargos)]

    def carrying(*refs):
        groups, pos = [], 0
        for k in counts:
            groups.append(refs[pos:pos + k])
            pos += k
        ins, c_ins, outs, c_outs, scratch, sems = groups
        parts, pi, po = [], 0, 0
        for n, cg in enumerate(cargos):
            parts.append((c_ins[pi:pi + len(cg.operands)], c_outs[po:po + len(cg.out_shapes)], sems[2 * n], sems[2 * n + 1]))
            pi += len(cg.operands)
            po += len(cg.out_shapes)
        step, steps = 0, 1
        for ax, g in enumerate(grid):
            step = step * g + pl.program_id(ax)
            steps *= g
        todo = {}
        for cg, part in zip(cargos, parts):
            for phase, frac in zip(cg.phases[:-1], cg.when[:-1]):
                todo.setdefault(int(round(frac * (steps - 1))), []).append((phase, part))

        for at in sorted(todo):
            @pl.when(step == at)
            def _(at=at):
                for phase, part in todo[at]:
                    phase(*part)

        body(*ins, *outs, *scratch)

        if cargos:
            @pl.when(step == steps - 1)
            def _():
                for cg, part in zip(cargos, parts):
                    cg.phases[-1](*part)

    sems = [pltpu.SemaphoreType.DMA((cg.n_sems,)) for cg in cargos for _ in range(2)]
    aliases, pi, po = {}, counts[0], counts[2]
    for cg in cargos:
        if cg.in_place:
            aliases.update({pi + k: po + k for k in range(len(cg.operands))})
        pi += len(cg.operands)
        po += len(cg.out_shapes)
    outs = pl.pallas_call(
        carrying, name=name, grid=grid,
        in_specs=list(in_specs) + [HBM_SPEC] * counts[1], out_specs=list(out_specs) + [HBM_SPEC] * counts[3],
        out_shape=list(out_shape) + c_shapes, scratch_shapes=list(scratch_shapes) + sems,
        input_output_aliases=aliases, compiler_params=params)(*args, *c_operands)
    own, rest = list(outs[:counts[2]]), list(outs[counts[2]:])
    carried, po = [], 0
    for cg in cargos:
        carried.append(rest[po:po + len(cg.out_shapes)])
        po += len(cg.out_shapes)
    return own, carried


def _run_cargo(cargo, name):
    n_in, n_out = len(cargo.operands), len(cargo.out_shapes)

    def body(*refs):
        c_ins, c_outs, sems = refs[:n_in], refs[n_in:n_in + n_out], refs[n_in + n_out:]
        for phase in cargo.phases:
            phase(c_ins, c_outs, *sems)

    sem = pltpu.SemaphoreType.DMA((cargo.n_sems,))
    return list(pl.pallas_call(body, name=name, out_shape=cargo.out_shapes, in_specs=[HBM_SPEC] * n_in,
                               out_specs=[HBM_SPEC] * n_out, scratch_shapes=[sem, sem])(*cargo.operands))


def _gather_cargo(shards):
    n = len(shards)
    for s in shards:
        assert s.shape[0] % 32 == 0
    slots = 8

    def steps(ins, outs, send_sems, recv_sems):
        x, y, c = _my_place()
        sibling = (x, y, 1 - c)
        over_x, over_y = (1 - x, y, c), (x, 1 - y, c)
        mine, chip_x, chip_y, chip_d = 2 * x + y, 2 * (1 - x) + y, 2 * x + (1 - y), 2 * (1 - x) + (1 - y)

        def rows_of(a, chip_index, half, part=None):
            rps = shards[a].shape[0]
            hr = rps // 2
            first = -(-hr // 32) * 16
            offset, size = {None: (0, hr), 0: (0, first), 1: (first, hr - first)}[part]
            return outs[a].at[pl.ds(pl.multiple_of(chip_index * rps + half * hr + offset, 16), size), :]

        def remote(a, slot, src, dst, to):
            return pltpu.make_async_remote_copy(
                src_ref=src, dst_ref=dst, send_sem=send_sems.at[a * slots + slot], recv_sem=recv_sems.at[a * slots + slot],
                device_id=to, device_id_type=MESH)

        def same_rows(a, slot, rows, to):
            return remote(a, slot, rows, rows, to)

        def own_copy(a):
            rps = shards[a].shape[0]
            return remote(a, 7, ins[a], outs[a].at[pl.ds(pl.multiple_of(mine * rps, 16), rps), :], sibling)

        def my_half(a):
            hr = shards[a].shape[0] // 2
            return ins[a].at[pl.ds(pl.multiple_of(c * hr, 16), hr), :]

        def start():
            for a in range(n):
                own_copy(a).start()
                remote(a, 0, my_half(a), rows_of(a, mine, c), over_x).start()
                remote(a, 1, my_half(a), rows_of(a, mine, c), over_y).start()

        def relay_neighbours():
            for a in range(n):
                same_rows(a, 0, rows_of(a, chip_x, c), over_x).wait_recv()
                same_rows(a, 4, rows_of(a, chip_x, c), sibling).start()
                same_rows(a, 2, rows_of(a, chip_x, c, 0), over_y).start()
                same_rows(a, 1, rows_of(a, chip_y, c), over_y).wait_recv()
                same_rows(a, 5, rows_of(a, chip_y, c), sibling).start()
                same_rows(a, 3, rows_of(a, chip_y, c, 1), over_x).start()

        def relay_diagonal():
            for a in range(n):
                same_rows(a, 2, rows_of(a, chip_d, c, 0), over_y).wait_recv()
                same_rows(a, 3, rows_of(a, chip_d, c, 1), over_x).wait_recv()
                same_rows(a, 6, rows_of(a, chip_d, c), sibling).start()

        def finish():
            for a in range(n):
                for slot, chip_index in ((4, chip_x), (5, chip_y), (6, chip_d)):
                    same_rows(a, slot, rows_of(a, chip_index, 1 - c), sibling).wait_recv()
            for a in range(n):
                remote(a, 0, my_half(a), rows_of(a, mine, c), over_x).wait_send()
                remote(a, 1, my_half(a), rows_of(a, mine, c), over_y).wait_send()
                same_rows(a, 2, rows_of(a, chip_x, c, 0), over_y).wait_send()
                same_rows(a, 3, rows_of(a, chip_y, c, 1), over_x).wait_send()
                for slot, chip_index in ((4, chip_x), (5, chip_y), (6, chip_d)):
                    same_rows(a, slot, rows_of(a, chip_index, c), sibling).wait_send()
                own_copy(a).wait()

        return start, relay_neighbours, relay_diagonal, finish

    phases = [lambda *r, k=k: steps(*r)[k]() for k in range(4)]
    return _Cargo(shards, [jax.ShapeDtypeStruct((N_CHIPS * s.shape[0], s.shape[1]), s.dtype) for s in shards], slots * n,
                  phases, [0.0, 0.6, 0.85, 1.0])


def _exchange_cargo(pairs):
    n = len(pairs)

    def copies(ins, outs, send_sems, recv_sems):
        x, y, c = _my_place()
        return [pltpu.make_async_remote_copy(
            src_ref=ins[a].at[2 * chip[0] + chip[1]], dst_ref=outs[a].at[j],
            send_sem=send_sems.at[3 * a + j], recv_sem=recv_sems.at[3 * a + j], device_id=(*chip, c), device_id_type=MESH)
            for a in range(n) for j, chip in enumerate(_other_chips(x, y))]

    def start(*r):
        for cp in copies(*r):
            cp.start()

    def finish(*r):
        for cp in copies(*r):
            cp.wait()

    return _Cargo(pairs, [jax.ShapeDtypeStruct((3,) + p.shape[1:], p.dtype) for p in pairs], 3 * n, [start, finish], [0.0, 1.0])


def _all_gather_small_cargo(pack):
    rows, cols = pack.shape

    def copies(ins, outs, send_sems, recv_sems):
        x, y, c = _my_place()
        me = 4 * x + 2 * y + c
        remote = []
        for f in range(1, N_DEVICES):
            fx, fy, fc = (f >> 2) & 1, (f >> 1) & 1, f & 1
            to = (1 - x if fx else x, 1 - y if fy else y, 1 - c if fc else c)
            remote.append(pltpu.make_async_remote_copy(
                src_ref=ins[0], dst_ref=outs[0].at[me], send_sem=send_sems.at[f - 1], recv_sem=recv_sems.at[f - 1],
                device_id=to, device_id_type=MESH))
        own = pltpu.make_async_copy(ins[0], outs[0].at[me], send_sems.at[N_DEVICES - 1])
        return remote, own

    def start(*r):
        remote, own = copies(*r)
        own.start()
        for cp in remote:
            cp.start()

    def finish(*r):
        remote, own = copies(*r)
        for cp in remote:
            cp.wait()
        own.wait()

    return _Cargo([pack], [jax.ShapeDtypeStruct((N_DEVICES, rows, cols), F32)], N_DEVICES, [start, finish], [0.0, 1.0])


def _sum_by_device(packs):
    n, rows, cols = packs.shape

    def body(p_ref, o_ref):
        acc = p_ref[0]
        for dev in range(1, n):
            acc = acc + p_ref[dev]
        o_ref[...] = acc

    return pl.pallas_call(body, name="small_grads_sum", out_shape=jax.ShapeDtypeStruct((rows, cols), F32))(packs)


def _chip_sum(pair, got, place, tag):
    _, hr, cols = pair.shape
    br = _row_block(hr, 512)

    def body(k_ref, p_ref, r_ref, o_ref):
        acc = p_ref[...].astype(F32)
        for j in range(3):
            acc = acc + r_ref[j].astype(F32)
        o_ref[...] = acc

    return pl.pallas_call(
        body, name="grad_chip_sum_" + tag,
        out_shape=jax.ShapeDtypeStruct((2, hr, cols), F32),
        grid_spec=pltpu.PrefetchScalarGridSpec(
            num_scalar_prefetch=1, grid=(hr // br,),
            in_specs=[pl.BlockSpec((None, br, cols), lambda r, k_ref: (k_ref[0], r, 0)),
                      pl.BlockSpec((3, br, cols), lambda r, k_ref: (0, r, 0))],
            out_specs=pl.BlockSpec((None, br, cols), lambda r, k_ref: (k_ref[1], r, 0))),
        compiler_params=pltpu.CompilerParams(dimension_semantics=("parallel",)),
    )(place, pair, got)


def _share_cargo(halves):
    n = len(halves)

    def copies(ins, outs, send_sems, recv_sems):
        x, y, c = _my_place()
        return [pltpu.make_async_remote_copy(
            src_ref=outs[a].at[c], dst_ref=outs[a].at[c], send_sem=send_sems.at[a], recv_sem=recv_sems.at[a],
            device_id=(x, y, 1 - c), device_id_type=MESH) for a in range(n)]

    def start(*r):
        for cp in copies(*r):
            cp.start()

    def finish(*r):
        for cp in copies(*r):
            cp.wait()

    return _Cargo(halves, [jax.ShapeDtypeStruct(h.shape, h.dtype) for h in halves], n, [start, finish], [0.0, 1.0], in_place=True)


def _sibling_share(halves):
    n = len(halves)

    def body(*refs):
        outs = refs[n:2 * n]
        send_sems, recv_sems = refs[2 * n:]
        x, y, c = _my_place()
        copies = []
        for a in range(n):
            cp = pltpu.make_async_remote_copy(
                src_ref=outs[a].at[c], dst_ref=outs[a].at[c], send_sem=send_sems.at[a], recv_sem=recv_sems.at[a],
                device_id=(x, y, 1 - c), device_id_type=MESH)
            cp.start()
            copies.append(cp)
        for cp in copies:
            cp.wait()

    return pl.pallas_call(
        body, name="grad_share_sibling",
        out_shape=[jax.ShapeDtypeStruct(h.shape, h.dtype) for h in halves],
        in_specs=[HBM_SPEC] * n, out_specs=[HBM_SPEC] * n,
        input_output_aliases={a: a for a in range(n)},
        scratch_shapes=[pltpu.SemaphoreType.DMA((n,)), pltpu.SemaphoreType.DMA((n,))],
    )(*halves)


def _load_rows(pairs, sems):
    cps = [pltpu.make_async_copy(src, dst, sems.at[j]) for j, (src, dst) in enumerate(pairs)]
    for cp in cps:
        cp.start()
    for cp in cps:
        cp.wait()


def _piece_rows(weights):
    return list(weights), (lambda refs, mats: list(zip(refs, mats))), len(weights)


def _cast_to_bf16(arrays, name, cargo=()):
    rows, cols = arrays[0].shape
    n = len(arrays)
    br = _row_block(rows, 256)

    def body(*refs):
        for src, dst in zip(refs[:n], refs[n:]):
            dst[...] = src[...].astype(BF16)

    blk = pl.BlockSpec((br, cols), lambda i: (i, 0))
    return _launch(body, name=name, grid=(rows // br,), in_specs=[blk] * n, out_specs=[blk] * n,
                   out_shape=[jax.ShapeDtypeStruct((rows, cols), BF16)] * n, scratch_shapes=[], args=tuple(arrays), cargo=cargo)


def _loss_head(xv, gv, tv):
    d = xv.shape[-1]
    r = lax.rsqrt(jnp.mean(xv * xv, axis=-1, keepdims=True) + EPS)
    xhat = xv * r
    err = xhat * gv - tv
    dy = err * (1.0 / d)
    dxh = dy * gv
    dx = r * (dxh - xhat * jnp.mean(dxh * xhat, axis=-1, keepdims=True))
    return dx, jnp.sum(err * err, axis=0, keepdims=True), jnp.sum(dy * xhat, axis=0, keepdims=True)


def _ffn_up(x, g, wg_t, wu_t, name, cargo=()):
    t, d = x.shape
    f = wg_t.shape[0]
    tm = min(TM_FFN, t)
    chunks = _feature_chunks(f, FFN_FWD_CHUNKS)
    flat, copies, n_copies = _piece_rows([wg_t, wu_t])
    nw = len(flat)

    def body(x_ref, g_ref, *rest):
        w_hbm, (a_ref, b_ref, s_ref, wg, wu, sems) = rest[:nw], rest[nw:]

        @pl.when(pl.program_id(0) == 0)
        def _():
            _load_rows(copies(w_hbm, [wg, wu]), sems)

        xv = x_ref[...]
        r = lax.rsqrt(jnp.mean(xv * xv, axis=-1, keepdims=True) + EPS)
        h = (xv * r * g_ref[...]).astype(BF16)
        for s0, sz in chunks:
            a = _nt(h, wg[s0:s0 + sz, :])
            b = _nt(h, wu[s0:s0 + sz, :])
            a_ref[:, s0:s0 + sz] = a.astype(BF16)
            b_ref[:, s0:s0 + sz] = b.astype(BF16)
            s_ref[:, s0:s0 + sz] = (a * _sigmoid(a) * b).astype(BF16)

    tok = lambda i: (i, 0)
    wide = pl.BlockSpec((tm, f), tok)
    return _launch(
        body, name=name, grid=(t // tm,),
        in_specs=[pl.BlockSpec((tm, d), tok), pl.BlockSpec((1, d), lambda i: (0, 0))] + [HBM_SPEC] * nw,
        out_specs=[wide, wide, wide], out_shape=[jax.ShapeDtypeStruct((t, f), BF16)] * 3,
        scratch_shapes=[pltpu.VMEM((f, d), BF16), pltpu.VMEM((f, d), BF16), pltpu.SemaphoreType.DMA((n_copies,))],
        args=(x, g, *flat), cargo=cargo)


def _ffn_down(x, s, wd, name, cargo=(), loss_head=None):
    t, d = x.shape
    f = s.shape[1]
    tm = min(TM_FFN, t)
    flat, copies, n_copies = _piece_rows([wd])
    nw = len(flat)
    nl = 2 if loss_head else 0

    def body(x_ref, s_ref, *rest):
        head, w_hbm = rest[:nl], rest[nl:nl + nw]
        xo_ref = rest[nl + nw]
        sums, (wdn, sems) = rest[nl + nw + 1:nl + nw + 1 + nl], rest[nl + nw + 1 + nl:]

        @pl.when(pl.program_id(0) == 0)
        def _():
            _load_rows(copies(w_hbm, [wdn]), sems)
            for sum_ref in sums:
                sum_ref[...] = jnp.zeros_like(sum_ref)

        xo = x_ref[...] + 0.5 * _nn(s_ref[...], wdn[...])
        if loss_head:
            dx, sq, dgf = _loss_head(xo, head[0][...], head[1][...])
            xo_ref[...] = dx
            sums[0][...] += sq
            sums[1][...] += dgf
        else:
            xo_ref[...] = xo

    tok = lambda i: (i, 0)
    one = lambda i: (0, 0)
    return _launch(
        body, name=name, grid=(t // tm,),
        in_specs=[pl.BlockSpec((tm, d), tok), pl.BlockSpec((tm, f), tok)]
        + ([pl.BlockSpec((1, d), one), pl.BlockSpec((tm, d), tok)] if loss_head else []) + [HBM_SPEC] * nw,
        out_specs=[pl.BlockSpec((tm, d), tok)] + [pl.BlockSpec((1, d), one)] * nl,
        out_shape=[jax.ShapeDtypeStruct((t, d), F32)] + [jax.ShapeDtypeStruct((1, d), F32)] * nl,
        scratch_shapes=[pltpu.VMEM((f, d), BF16), pltpu.SemaphoreType.DMA((n_copies,))],
        args=(x, s, *(loss_head or ()), *flat), cargo=cargo)


def _ffn_backward(dxo, x, g, a, b, wg_t, wu_t, wd, name, cargo=()):
    t, d = x.shape
    f = wd.shape[0]
    tm = min(TM_FFN // 2, t)
    chunks = _feature_chunks(f, FFN_BWD_CHUNKS)
    flat, copies, n_copies = _piece_rows([wg_t, wu_t, wd])
    nw = len(flat)

    def body(dxo_ref, x_ref, g_ref, a_ref, b_ref, *rest):
        w_hbm, (dx_ref, dab_ref, hd_ref, dg_ref, wg, wu, wdn, sems) = rest[:nw], rest[nw:]

        @pl.when(pl.program_id(0) == 0)
        def _():
            _load_rows(copies(w_hbm, [wg, wu, wdn]), sems)
            dg_ref[...] = jnp.zeros_like(dg_ref)

        xv = x_ref[...]
        gv = g_ref[...]
        r = lax.rsqrt(jnp.mean(xv * xv, axis=-1, keepdims=True) + EPS)
        xhat = xv * r
        hd_ref[:, 0:d] = (xhat * gv).astype(BF16)
        dxo_v = dxo_ref[...]
        dout = (0.5 * dxo_v).astype(BF16)
        hd_ref[:, d:2 * d] = dout
        dh = jnp.zeros((tm, d), F32)
        for s0, sz in chunks:
            ds = _nt(dout, wdn[s0:s0 + sz, :])
            av = a_ref[:, s0:s0 + sz].astype(F32)
            bv = b_ref[:, s0:s0 + sz].astype(F32)
            sig = _sigmoid(av)
            silu = av * sig
            da = (ds * bv * (sig * (1.0 + av * (1.0 - sig)))).astype(BF16)
            db = (ds * silu).astype(BF16)
            dab_ref[:, s0:s0 + sz] = da
            dab_ref[:, f + s0:f + s0 + sz] = db
            dh = dh + _nn(da, wg[s0:s0 + sz, :]) + _nn(db, wu[s0:s0 + sz, :])
        dg_ref[...] += jnp.sum(dh * xhat, axis=0, keepdims=True)
        dxh = dh * gv
        dx_ref[...] = dxo_v + r * (dxh - xhat * jnp.mean(dxh * xhat, axis=-1, keepdims=True))

    tok = lambda i: (i, 0)
    one = lambda i: (0, 0)
    return _launch(
        body, name=name, grid=(t // tm,),
        in_specs=[pl.BlockSpec((tm, d), tok), pl.BlockSpec((tm, d), tok), pl.BlockSpec((1, d), one),
                  pl.BlockSpec((tm, f), tok), pl.BlockSpec((tm, f), tok)] + [HBM_SPEC] * nw,
        out_specs=[pl.BlockSpec((tm, d), tok), pl.BlockSpec((tm, 2 * f), tok), pl.BlockSpec((tm, 2 * d), tok),
                   pl.BlockSpec((1, d), one)],
        out_shape=[jax.ShapeDtypeStruct((t, d), F32), jax.ShapeDtypeStruct((t, 2 * f), BF16),
                   jax.ShapeDtypeStruct((t, 2 * d), BF16), jax.ShapeDtypeStruct((1, d), F32)],
        scratch_shapes=[pltpu.VMEM((f, d), BF16), pltpu.VMEM((f, d), BF16), pltpu.VMEM((f, d), BF16), pltpu.SemaphoreType.DMA((n_copies,))],
        args=(dxo, x, g, a, b, *flat), cargo=cargo)


def _weight_grad(lhs, rhs, name, cargo=(), lhs_part=(0, 1), rhs_part=(0, 1)):
    t = lhs.shape[0]
    m = lhs.shape[1] // lhs_part[1]
    d = rhs.shape[1] // rhs_part[1]
    tm = min(TM_TN, t)
    nt = t // tm
    rps = m // N_CHIPS
    hr = rps // 2
    assert hr % 16 == 0

    def body(l_ref, r_ref, o_ref, acc, stage, recv, send_sems, recv_sems):
        i = pl.program_id(0)

        @pl.when(i == 0)
        def _():
            acc[...] = jnp.zeros_like(acc)

        acc[...] += _tn(l_ref[...], r_ref[...])

        @pl.when(i == nt - 1)
        def _():
            x, y, c = _my_place()
            copies = []
            for q in range(N_CHIPS):
                stage[q] = acc[pl.ds(pl.multiple_of(q * rps + (1 - c) * hr, 16), hr), :].astype(BF16)
                cp = pltpu.make_async_remote_copy(
                    src_ref=stage.at[q], dst_ref=recv.at[q], send_sem=send_sems.at[q], recv_sem=recv_sems.at[q],
                    device_id=(x, y, 1 - c), device_id_type=MESH)
                cp.start()
                copies.append(cp)
            for q, cp in enumerate(copies):
                cp.wait_recv()
                mine = acc[pl.ds(pl.multiple_of(q * rps + c * hr, 16), hr), :]
                o_ref[q] = (mine + recv[q].astype(F32)).astype(BF16)
            for cp in copies:
                cp.wait_send()

    outs, carried = _launch(
        body, name=name, grid=(nt,),
        in_specs=[pl.BlockSpec((tm, m), lambda i: (i, lhs_part[0])), pl.BlockSpec((tm, d), lambda i: (i, rhs_part[0]))],
        out_specs=[pl.BlockSpec((N_CHIPS, hr, d), lambda i: (0, 0, 0))],
        out_shape=[jax.ShapeDtypeStruct((N_CHIPS, hr, d), BF16)],
        scratch_shapes=[pltpu.VMEM((m, d), F32), pltpu.VMEM((N_CHIPS, hr, d), BF16), pltpu.VMEM((N_CHIPS, hr, d), BF16),
                        pltpu.SemaphoreType.DMA((N_CHIPS,)), pltpu.SemaphoreType.DMA((N_CHIPS,))],
        args=(lhs, rhs), cargo=cargo)
    return outs[0], carried


def _window_sums(src, cols, w, tm, levels, trailing):
    def read_src(lo, hi):
        return src[lo:hi, cols]

    read, k, level = read_src, 1, 0
    while True:
        last = 2 * k == w
        if trailing:
            lo, hi = (HALO if last else 8 * (level + 1)), HALO + tm
            cur = read(lo, hi) + read(lo - k, hi - k)
        else:
            lo, hi = 0, (tm if last else tm + HALO - 8 * (level + 1))
            cur = read(lo, hi) + read(lo + k, hi + k)
        if last:
            return cur
        levels[level, lo:hi, :] = cur
        read = lambda a, b, level=level: levels[level, a:b, :]
        k, level = 2 * k, level + 1


def _pool_parts(u_cols, ubuf, cols, w, row, tm, levels):
    ws = _window_sums(ubuf, cols, w, tm, levels, trailing=True)
    inv = 1.0 / jnp.minimum(row + 1, w).astype(F32)
    return ws * inv - u_cols, inv


def _mixer_forward(x, g, win_t, wout_x, conv_w, pool_w, pool_scale, cargo=()):
    t, d = x.shape
    dc = win_t.shape[0] // 4
    gcw = dc // len(POOL_WINDOWS)
    wo_rows = d // N_CHIPS
    wo_stride = wout_x.shape[0] // N_CHIPS
    tm = min(TM_MIX, t)

    def body(x_ref, g_ref, win_hbm, wout_hbm, cw_ref, pw_ref, ps_ref, xo_ref, proj_ref, y_ref,
             win, wout, zbuf, ubuf, levels, sems):
        i = pl.program_id(0)

        @pl.when(i == 0)
        def _():
            pairs = [(win_hbm, win)]
            for k in range(N_CHIPS):
                pairs.append((wout_hbm.at[pl.ds(k * wo_stride, wo_rows), :], wout.at[pl.ds(k * wo_rows, wo_rows), :]))
            _load_rows(pairs, sems)
            zbuf[0:8, :] = jnp.zeros((8, dc), F32)
            ubuf[0:HALO, :] = jnp.zeros((HALO, dc), F32)

        xv = x_ref[...]
        r = lax.rsqrt(jnp.mean(xv * xv, axis=-1, keepdims=True) + EPS)
        h = (xv * r * g_ref[...]).astype(BF16)
        v = _nt(h, win[0:dc, :])
        gb = _nt(h, win[dc:2 * dc, :])
        gc = _nt(h, win[2 * dc:3 * dc, :])
        u = _nt(h, win[3 * dc:4 * dc, :])
        proj_ref[:, 0:dc] = v.astype(BF16)
        proj_ref[:, dc:2 * dc] = gb.astype(BF16)
        proj_ref[:, 2 * dc:3 * dc] = gc.astype(BF16)
        proj_ref[:, 3 * dc:4 * dc] = u.astype(BF16)

        z = gc * v
        zbuf[8:8 + tm, :] = z
        cw = cw_ref[...]
        conv = cw[2:3, :] * z + cw[1:2, :] * zbuf[7:7 + tm, :] + cw[0:1, :] * zbuf[6:6 + tm, :]
        y_ref[:, 0:dc] = (gb * conv).astype(BF16)

        ubuf[HALO:HALO + tm, :] = u
        row = i * tm + lax.broadcasted_iota(jnp.int32, (tm, 1), 0)
        for gi, w in enumerate(POOL_WINDOWS):
            cols = slice(gi * gcw, (gi + 1) * gcw)
            pooled, _ = _pool_parts(u[:, cols], ubuf, cols, w, row, tm, levels)
            yb = _nn(pooled.astype(BF16), pw_ref[gi].astype(BF16)) * ps_ref[:, cols]
            y_ref[:, dc + gi * gcw:dc + (gi + 1) * gcw] = yb.astype(BF16)

        xo_ref[...] = xv + _nn(y_ref[...], wout[...])
        zbuf[0:8, :] = zbuf[tm:tm + 8, :]
        ubuf[0:HALO, :] = ubuf[tm:tm + HALO, :]

    tok = lambda i: (i, 0)
    one = lambda i: (0, 0)
    return _launch(
        body, name="mixer_forward", grid=(t // tm,),
        in_specs=[pl.BlockSpec((tm, d), tok), pl.BlockSpec((1, d), one), HBM_SPEC, HBM_SPEC,
                  pl.BlockSpec(conv_w.shape, one), pl.BlockSpec(pool_w.shape, lambda i: (0, 0, 0)), pl.BlockSpec((1, dc), one)],
        out_specs=[pl.BlockSpec((tm, d), tok), pl.BlockSpec((tm, 4 * dc), tok), pl.BlockSpec((tm, 2 * dc), tok)],
        out_shape=[jax.ShapeDtypeStruct((t, d), F32), jax.ShapeDtypeStruct((t, 4 * dc), BF16), jax.ShapeDtypeStruct((t, 2 * dc), BF16)],
        scratch_shapes=[pltpu.VMEM((4 * dc, d), BF16), pltpu.VMEM((2 * dc, d), BF16),
                        pltpu.VMEM((tm + 8, dc), F32), pltpu.VMEM((tm + HALO, dc), F32),
                        pltpu.VMEM((WINDOW_LEVELS, tm + HALO, gcw), F32), pltpu.SemaphoreType.DMA((1 + N_CHIPS,))],
        args=(x, g, win_t, wout_x, conv_w, pool_w, pool_scale), cargo=cargo)


def _mixer_backward(dxo, x, g, proj, win_t, wout_x, conv_w, pool_w, pool_scale, cargo=()):
    t, d = x.shape
    dc = win_t.shape[0] // 4
    ng = len(POOL_WINDOWS)
    gcw = dc // ng
    wo_rows = d // N_CHIPS
    wo_stride = wout_x.shape[0] // N_CHIPS
    tm = min(TM_MIX, t)
    n_tiles = t // tm
    hb = tm // HALO

    def body(dxo_ref, x_ref, g_ref, proj_ref, halo_ref, win_hbm, wout_hbm, cw_ref, pw_ref, ps_ref,
             dx_ref, dproj_ref, h_ref, dxob_ref, dg_ref, dcw_ref, dps_ref, dpw_ref,
             win, wout, zbuf, ubuf, dcbuf, ebuf, levels, sems):
        i = pl.program_id(0)
        tile = n_tiles - 1 - i

        @pl.when(i == 0)
        def _():
            pairs = [(win_hbm, win)]
            for k in range(N_CHIPS):
                pairs.append((wout_hbm.at[pl.ds(k * wo_stride, wo_rows), :], wout.at[pl.ds(k * wo_rows, wo_rows), :]))
            _load_rows(pairs, sems)
            dcbuf[tm:tm + 8, :] = jnp.zeros((8, dc), F32)
            ebuf[tm:tm + HALO, :] = jnp.zeros((HALO, dc), F32)
            dg_ref[...] = jnp.zeros_like(dg_ref)
            dcw_ref[...] = jnp.zeros_like(dcw_ref)
            dps_ref[...] = jnp.zeros_like(dps_ref)
            dpw_ref[...] = jnp.zeros_like(dpw_ref)

        xv = x_ref[...]
        gv = g_ref[...]
        r = lax.rsqrt(jnp.mean(xv * xv, axis=-1, keepdims=True) + EPS)
        xhat = xv * r
        h_ref[...] = (xhat * gv).astype(BF16)
        dxo_v = dxo_ref[...]
        dxo_b = dxo_v.astype(BF16)
        dxob_ref[...] = dxo_b

        v = proj_ref[:, 0:dc].astype(F32)
        gb = proj_ref[:, dc:2 * dc].astype(F32)
        gc = proj_ref[:, 2 * dc:3 * dc].astype(F32)
        u = proj_ref[:, 3 * dc:4 * dc].astype(F32)
        first = jnp.where(tile > 0, 1.0, 0.0)
        zbuf[0:HALO, :] = halo_ref[:, 2 * dc:3 * dc].astype(F32) * halo_ref[:, 0:dc].astype(F32) * first
        ubuf[0:HALO, :] = halo_ref[:, 3 * dc:4 * dc].astype(F32) * first
        z = gc * v
        zbuf[HALO:HALO + tm, :] = z
        ubuf[HALO:HALO + tm, :] = u
        z1 = zbuf[HALO - 1:HALO - 1 + tm, :]
        z2 = zbuf[HALO - 2:HALO - 2 + tm, :]
        cw = cw_ref[...]
        conv = cw[2:3, :] * z + cw[1:2, :] * z1 + cw[0:1, :] * z2

        dy = _nt(dxo_b, wout[...])
        dya = dy[:, 0:dc]
        dgb = dya * conv
        dconv = dya * gb
        dcbuf[0:tm, :] = dconv
        dz = cw[2:3, :] * dconv + cw[1:2, :] * dcbuf[1:1 + tm, :] + cw[0:1, :] * dcbuf[2:2 + tm, :]
        dgc = dz * v
        dv = dz * gc
        dcw_ref[0:1, :] += jnp.sum(dconv * z2, axis=0, keepdims=True)
        dcw_ref[1:2, :] += jnp.sum(dconv * z1, axis=0, keepdims=True)
        dcw_ref[2:3, :] += jnp.sum(dconv * z, axis=0, keepdims=True)

        dproj_ref[:, 0:dc] = dv.astype(BF16)
        dproj_ref[:, dc:2 * dc] = dgb.astype(BF16)
        dproj_ref[:, 2 * dc:3 * dc] = dgc.astype(BF16)

        row = tile * tm + lax.broadcasted_iota(jnp.int32, (tm, 1), 0)
        for gi, w in enumerate(POOL_WINDOWS):
            cols = slice(gi * gcw, (gi + 1) * gcw)
            pooled, inv_cnt = _pool_parts(u[:, cols], ubuf, cols, w, row, tm, levels)
            pooled_b = pooled.astype(BF16)
            pw_b = pw_ref[gi].astype(BF16)
            dyb = dy[:, dc + gi * gcw:dc + (gi + 1) * gcw]
            q = _nn(pooled_b, pw_b)
            dps_ref[:, cols] += jnp.sum(q * dyb, axis=0, keepdims=True)
            dq = (dyb * ps_ref[:, cols]).astype(BF16)
            dpw_ref[gi] += _tn(pooled_b, dq)
            dpooled = _nt(dq, pw_b)
            ebuf[0:tm, cols] = dpooled * inv_cnt
            du = _window_sums(ebuf, cols, w, tm, levels, trailing=False) - dpooled
            dproj_ref[:, 3 * dc + gi * gcw:3 * dc + (gi + 1) * gcw] = du.astype(BF16)

        dh = _nn(dproj_ref[...], win[...])
        dg_ref[...] += jnp.sum(dh * xhat, axis=0, keepdims=True)
        dxh = dh * gv
        dx_ref[...] = dxo_v + r * (dxh - xhat * jnp.mean(dxh * xhat, axis=-1, keepdims=True))
        dcbuf[tm:tm + 8, :] = dcbuf[0:8, :]
        ebuf[tm:tm + HALO, :] = ebuf[0:HALO, :]

    tok = lambda i: (n_tiles - 1 - i, 0)
    halo = lambda i: (jnp.maximum((n_tiles - 1 - i) * hb - 1, 0), 0)
    one = lambda i: (0, 0)
    return _launch(
        body, name="mixer_backward", grid=(n_tiles,),
        in_specs=[pl.BlockSpec((tm, d), tok), pl.BlockSpec((tm, d), tok), pl.BlockSpec((1, d), one),
                  pl.BlockSpec((tm, 4 * dc), tok), pl.BlockSpec((HALO, 4 * dc), halo), HBM_SPEC, HBM_SPEC,
                  pl.BlockSpec(conv_w.shape, one), pl.BlockSpec(pool_w.shape, lambda i: (0, 0, 0)), pl.BlockSpec((1, dc), one)],
        out_specs=[pl.BlockSpec((tm, d), tok), pl.BlockSpec((tm, 4 * dc), tok), pl.BlockSpec((tm, d), tok), pl.BlockSpec((tm, d), tok),
                   pl.BlockSpec((1, d), one), pl.BlockSpec(conv_w.shape, one), pl.BlockSpec((1, dc), one),
                   pl.BlockSpec(pool_w.shape, lambda i: (0, 0, 0))],
        out_shape=[jax.ShapeDtypeStruct((t, d), F32), jax.ShapeDtypeStruct((t, 4 * dc), BF16), jax.ShapeDtypeStruct((t, d), BF16),
                   jax.ShapeDtypeStruct((t, d), BF16), jax.ShapeDtypeStruct((1, d), F32), jax.ShapeDtypeStruct(conv_w.shape, F32),
                   jax.ShapeDtypeStruct((1, dc), F32), jax.ShapeDtypeStruct(pool_w.shape, F32)],
        scratch_shapes=[pltpu.VMEM((4 * dc, d), BF16), pltpu.VMEM((2 * dc, d), BF16),
                        pltpu.VMEM((tm + HALO, dc), F32), pltpu.VMEM((tm + HALO, dc), F32),
                        pltpu.VMEM((tm + 8, dc), F32), pltpu.VMEM((tm + HALO, dc), F32),
                        pltpu.VMEM((WINDOW_LEVELS, tm + HALO, gcw), F32), pltpu.SemaphoreType.DMA((1 + N_CHIPS,))],
        args=(dxo, x, g, proj, proj, win_t, wout_x, conv_w, pool_w, pool_scale), cargo=cargo)


def _adam_update(w, gv, m, v):
    m_new = ADAM_B1 * m + (1.0 - ADAM_B1) * gv
    v_new = ADAM_B2 * v + (1.0 - ADAM_B2) * (gv * gv)
    m_hat = m_new / (1.0 - ADAM_B1 ** ADAM_STEP)
    v_hat = v_new / (1.0 - ADAM_B2 ** ADAM_STEP)
    return -ADAM_LR * (m_hat / (jnp.sqrt(v_hat) + ADAM_EPS) + ADAM_WD * w), m_new, v_new


def _adamw(w, grad, m, v, name):
    rows, cols = w.shape
    br = _row_block(rows, 512) if rows >= 8 else rows

    def body(w_ref, g_ref, m_ref, v_ref, go_ref, d_ref, mo_ref, vo_ref):
        gv = g_ref[...]
        go_ref[...] = gv
        d_ref[...], mo_ref[...], vo_ref[...] = _adam_update(w_ref[...], gv, m_ref[...], v_ref[...])

    blk = pl.BlockSpec((br, cols), lambda i: (i, 0))
    return pl.pallas_call(
        body, name=name,
        out_shape=[jax.ShapeDtypeStruct((rows, cols), F32)] * 4,
        grid=(rows // br,), in_specs=[blk] * 4, out_specs=[blk] * 4,
        compiler_params=pltpu.CompilerParams(dimension_semantics=("parallel",), vmem_limit_bytes=VMEM_LIMIT),
    )(w, grad, m, v)


def _adamw_transposed(w, grad_t, m, v, name):
    _, rows, cols = w.shape
    br = 512 if rows % 512 == 0 else rows

    def body(w_ref, gt_ref, m_ref, v_ref, g_ref, d_ref, mo_ref, vo_ref):
        gv = gt_ref[...].T
        g_ref[...] = gv
        d_ref[...], mo_ref[...], vo_ref[...] = _adam_update(w_ref[...], gv, m_ref[...], v_ref[...])

    blk = pl.BlockSpec((None, br, cols), lambda i: (0, i, 0))
    return pl.pallas_call(
        body, name=name,
        out_shape=[jax.ShapeDtypeStruct((1, rows, cols), F32)] * 4,
        grid=(rows // br,), in_specs=[blk, pl.BlockSpec((cols, br), lambda i: (0, i)), blk, blk], out_specs=[blk] * 4,
        compiler_params=pltpu.CompilerParams(dimension_semantics=("parallel",)),
    )(w, grad_t, m, v)


def _f32_rows_as_bf16(a, rows, cols):
    bits = lax.bitcast_convert_type(a, BF16).reshape(a.shape[0], 2 * a.shape[1])
    return jnp.pad(bits, ((0, rows - bits.shape[0]), (0, cols - bits.shape[1])))


def kernel(x, norm_ffn1, ffn1_w_gate, ffn1_w_up, ffn1_w_down, norm_mix, w_in, conv_w, pool_w, pool_scale, w_out, norm_ffn2, ffn2_w_gate, ffn2_w_up, ffn2_w_down, norm_final, loss_target, m_norm_ffn1, m_ffn1_w_gate, m_ffn1_w_up, m_ffn1_w_down, m_norm_mix, m_w_in, m_conv_w, m_pool_w, m_pool_scale, m_w_out, m_norm_ffn2, m_ffn2_w_gate, m_ffn2_w_up, m_ffn2_w_down, m_norm_final, v_norm_ffn1, v_ffn1_w_gate, v_ffn1_w_up, v_ffn1_w_down, v_norm_mix, v_w_in, v_conv_w, v_pool_w, v_pool_scale, v_w_out, v_norm_ffn2, v_ffn2_w_gate, v_ffn2_w_up, v_ffn2_w_down, v_norm_final):
    weights = dict(norm_ffn1=norm_ffn1, ffn1_w_gate=ffn1_w_gate, ffn1_w_up=ffn1_w_up, ffn1_w_down=ffn1_w_down, norm_mix=norm_mix,
                   w_in=w_in, conv_w=conv_w, pool_w=pool_w, pool_scale=pool_scale, w_out=w_out, norm_ffn2=norm_ffn2,
                   ffn2_w_gate=ffn2_w_gate, ffn2_w_up=ffn2_w_up, ffn2_w_down=ffn2_w_down, norm_final=norm_final)
    first_m = dict(norm_ffn1=m_norm_ffn1, ffn1_w_gate=m_ffn1_w_gate, ffn1_w_up=m_ffn1_w_up, ffn1_w_down=m_ffn1_w_down,
                   norm_mix=m_norm_mix, w_in=m_w_in, conv_w=m_conv_w, pool_w=m_pool_w, pool_scale=m_pool_scale, w_out=m_w_out,
                   norm_ffn2=m_norm_ffn2, ffn2_w_gate=m_ffn2_w_gate, ffn2_w_up=m_ffn2_w_up, ffn2_w_down=m_ffn2_w_down,
                   norm_final=m_norm_final)
    second_m = dict(norm_ffn1=v_norm_ffn1, ffn1_w_gate=v_ffn1_w_gate, ffn1_w_up=v_ffn1_w_up, ffn1_w_down=v_ffn1_w_down,
                    norm_mix=v_norm_mix, w_in=v_w_in, conv_w=v_conv_w, pool_w=v_pool_w, pool_scale=v_pool_scale, w_out=v_w_out,
                    norm_ffn2=v_norm_ffn2, ffn2_w_gate=v_ffn2_w_gate, ffn2_w_up=v_ffn2_w_up, ffn2_w_down=v_ffn2_w_down,
                    norm_final=v_norm_final)
    names = list(weights)

    xs = x[0]
    tgt = loss_target[0]
    t, d = xs.shape
    dc = pool_scale.shape[1]
    cx, cy, cc = _my_place()
    chip = 2 * cx + cy
    place = jnp.stack([chip, cc]).astype(jnp.int32)

    conv_rows = 32
    wout_x = jnp.concatenate([w_out[0].astype(BF16), _f32_rows_as_bf16(conv_w[0], conv_rows, d)], axis=0)

    g1, gm, g2 = norm_ffn1, norm_mix, norm_ffn2
    gf = norm_final.reshape(1, d)
    pw = pool_w[0]

    (wd1_shard, wg2_shard, wu2_shard, wd2_shard), [(wg1, wu1)] = _cast_to_bf16(
        [ffn1_w_down[0], ffn2_w_gate[0].T, ffn2_w_up[0].T, ffn2_w_down[0]], "gather_ffn1",
        [_gather_cargo([ffn1_w_gate[0].T.astype(BF16), ffn1_w_up[0].T.astype(BF16)])])
    (a1, b1, s1), [(wd1, win_t, wout_g)] = _ffn_up(
        xs, g1, wg1, wu1, "ffn1_up", [_gather_cargo([wd1_shard, w_in[0].T.astype(BF16), wout_x])])
    (x1,), [(wg2,)] = _ffn_down(xs, s1, wd1, "ffn1_down", [_gather_cargo([wg2_shard])])
    wo_rows = w_out.shape[1]
    cshard = conv_w.shape[2]
    conv_bits = wout_g.reshape(N_CHIPS, wo_rows + conv_rows, d)[:, wo_rows:wo_rows + conv_w.shape[1], :2 * cshard]
    conv_full = lax.bitcast_convert_type(conv_bits.reshape(N_CHIPS, conv_w.shape[1], cshard, 2), F32)
    conv_full = jnp.transpose(conv_full, (1, 0, 2)).reshape(conv_w.shape[1], N_CHIPS * cshard)
    (x2, proj, ymix), [(wu2,)] = _mixer_forward(x1, gm, win_t, wout_g, conv_full, pw, pool_scale, [_gather_cargo([wu2_shard])])
    (a2, b2, s2), [(wd2,)] = _ffn_up(x2, g2, wg2, wu2, "ffn2_up", [_gather_cargo([wd2_shard])])
    (dx3, sq_cols, dgf), _ = _ffn_down(x2, s2, wd2, "ffn2_down", loss_head=(gf, tgt))

    (dx2, dab2, hd2, dg2), _ = _ffn_backward(dx3, x2, g2, a2, b2, wg2, wu2, wd2, "ffn2_backward")
    p_wg2, _ = _weight_grad(dab2, hd2, "ffn2_gate_grad", lhs_part=(0, 2), rhs_part=(0, 2))
    p_wu2, [(x_wg2,)] = _weight_grad(dab2, hd2, "ffn2_up_grad", [_exchange_cargo([p_wg2])], lhs_part=(1, 2), rhs_part=(0, 2))
    p_wd2, [(x_wu2,)] = _weight_grad(s2, hd2, "ffn2_down_grad", [_exchange_cargo([p_wu2])], rhs_part=(1, 2))

    (dx1, dproj, h2, dx2b, dgm, dcw, dps, dpw), [(x_wd2,)] = _mixer_backward(
        dx2, x1, gm, proj, win_t, wout_g, conv_full, pw, pool_scale, [_exchange_cargo([p_wd2])])

    (dx0, dab1, hd1, dg1), _ = _ffn_backward(dx1, xs, g1, a1, b1, wg1, wu1, wd1, "ffn1_backward")

    npw = pw.size // d
    head = [dg1, dgm, dg2, dgf, jnp.pad(dps, ((0, 0), (0, d - dc))), jnp.pad(dcw, ((0, 0), (0, d - dc))), sq_cols]
    n_head = sum(h.shape[0] for h in head)
    base = -(-n_head // 8) * 8
    pack = jnp.concatenate(head + [jnp.zeros((base - n_head, d), F32), dpw.reshape(npw, d)], axis=0)

    p_wg1, [(packs,)] = _weight_grad(dab1, hd1, "ffn1_gate_grad", [_all_gather_small_cargo(pack)], lhs_part=(0, 2), rhs_part=(0, 2))
    p_wu1, [(x_wg1,)] = _weight_grad(dab1, hd1, "ffn1_up_grad", [_exchange_cargo([p_wg1])], lhs_part=(1, 2), rhs_part=(0, 2))
    p_wd1, [(x_wu1,)] = _weight_grad(s1, hd1, "ffn1_down_grad", [_exchange_cargo([p_wu1])], rhs_part=(1, 2))
    p_win, [(x_wd1,)] = _weight_grad(dproj, h2, "w_in_grad", [_exchange_cargo([p_wd1])])

    pairs = dict(wg1=p_wg1, wu1=p_wu1, wd1=p_wd1, wg2=p_wg2, wu2=p_wu2, wd2=p_wd2)
    landed = dict(wg1=x_wg1, wu1=x_wu1, wd1=x_wd1, wg2=x_wg2, wu2=x_wu2, wd2=x_wd2)
    ffn_halves = [_chip_sum(pairs[k], landed[k], place, k) for k in ["wg1", "wu1", "wd1", "wg2", "wu2", "wd2"]]
    p_wout, [(x_win,), ffn_both] = _weight_grad(ymix, dx2b, "w_out_grad", [_exchange_cargo([p_win]), _share_cargo(ffn_halves)])
    x_wout, = _run_cargo(_exchange_cargo([p_wout]), "grad_exchange_last")
    mix_both = _sibling_share([_chip_sum(p_win, x_win, place, "win"), _chip_sum(p_wout, x_wout, place, "wout")])
    rwg1, rwu1, rwd1, rwg2, rwu2, rwd2, rwin, rwout = [b.reshape(2 * b.shape[1], b.shape[2]) for b in list(ffn_both) + list(mix_both)]
    small = _sum_by_device(packs)
    loss = jnp.sum(small[n_head - 1]) * (0.5 / d)

    grads = {
        "norm_ffn1": small[0:1], "norm_mix": small[1:2], "norm_ffn2": small[2:3], "norm_final": small[3],
        "pool_scale": small[4:5, :dc],
        "conv_w": lax.dynamic_slice_in_dim(small[5:5 + dcw.shape[0], :dc], chip * cshard, cshard, axis=1)[None],
        "pool_w": small[base:].reshape(pool_w.shape),
        "ffn1_w_down": rwd1[None], "w_out": rwout[None], "ffn2_w_down": rwd2[None],
    }
    by_view = {"ffn1_w_gate": rwg1, "ffn1_w_up": rwu1, "ffn2_w_gate": rwg2, "ffn2_w_up": rwu2}

    deltas, new_m, new_v = {}, {}, {}
    for n in names:
        w = weights[n]
        shape = w.shape
        if n == "w_in":
            grads[n], deltas[n], new_m[n], new_v[n] = _adamw_transposed(w, rwin, first_m[n], second_m[n], "adamw_" + n)
            continue
        if n in by_view:
            view = lambda a: jnp.swapaxes(a, 1, 2)[0]
            back = lambda a: jnp.swapaxes(a[None], 1, 2)
            outs = _adamw(view(w), by_view[n], view(first_m[n]), view(second_m[n]), "adamw_" + n)
            grads[n], deltas[n], new_m[n], new_v[n] = [back(o) for o in outs]
            continue
        as2d = (lambda a: a.reshape(-1, shape[-1]))
        outs = _adamw(as2d(w), as2d(grads[n]), as2d(first_m[n]), as2d(second_m[n]), "adamw_" + n)
        grads[n], deltas[n], new_m[n], new_v[n] = [o.reshape(shape) for o in outs]

    return (loss, dx0[None], *[grads[n] for n in names], *[deltas[n] for n in names],
            *[new_m[n] for n in names], *[new_v[n] for n in names])
```

```python
import jax
import jax.numpy as jnp
from jax import lax
from jax.experimental import pallas as pl
from jax.experimental.pallas import tpu as pltpu

F32 = jnp.float32
BF16 = jnp.bfloat16
MESH = pl.DeviceIdType.MESH

EPS = 1e-6
POOL_WINDOWS = (2, 4, 8, 16)
ADAM_LR = 0.001
ADAM_B1 = 0.9
ADAM_B2 = 0.999
ADAM_EPS = 1e-08
ADAM_WD = 0.01
ADAM_STEP = 10

N_CHIPS = 4
N_DEVICES = 8
MXU_COLS_V7X = 256
VMEM_LIMIT = 56 * 1024 * 1024
TM_FFN = 512
TM_MIX = 512
TM_TN = 1024
HALO = 32
WINDOW_LEVELS = 3
FFN_FWD_CHUNKS = 2
FFN_BWD_CHUNKS = 2


def _nt(a, b):
    return lax.dot_general(a, b, (((1,), (1,)), ((), ())), preferred_element_type=F32)


def _tn(a, b):
    return lax.dot_general(a, b, (((0,), (0,)), ((), ())), preferred_element_type=F32)


def _nn(a, b):
    return jnp.dot(a, b, preferred_element_type=F32)


def _sigmoid(a):
    return 1.0 / (1.0 + jnp.exp(-a))


def _feature_chunks(n, parts):
    assert n % MXU_COLS_V7X == 0
    tiles = n // MXU_COLS_V7X
    out, s0 = [], 0
    for p in range(parts):
        sz = (tiles // parts + (1 if p < tiles % parts else 0)) * MXU_COLS_V7X
        if sz:
            out.append((s0, sz))
            s0 += sz
    return out


def _row_block(rows, cap):
    best = 8
    for b in range(8, min(rows, cap) + 1, 8):
        if rows % b == 0:
            best = b
    assert rows % best == 0
    return best


def _my_place():
    return lax.axis_index("x"), lax.axis_index("y"), lax.axis_index("c")


def _other_chips(x, y):
    return [(1 - x, y), (x, 1 - y), (1 - x, 1 - y)]


HBM_SPEC = pl.BlockSpec(memory_space=pltpu.HBM)


class _Cargo:
    def __init__(self, operands, out_shapes, n_sems, phases, when, in_place=False):
        self.operands, self.out_shapes, self.n_sems = list(operands), list(out_shapes), n_sems
        self.phases, self.when = list(phases), list(when)
        self.in_place = in_place
        assert len(self.phases) == len(self.when) and self.when[0] == 0.0 and self.when[-1] == 1.0


def _launch(body, *, name, grid, in_specs, out_specs, out_shape, scratch_shapes, args, cargo=()):
    params = pltpu.CompilerParams(dimension_semantics=("arbitrary",) * len(grid), vmem_limit_bytes=VMEM_LIMIT)
    cargos = list(cargo)
    c_operands = [op for cg in cargos for op in cg.operands]
    c_shapes = [sh for cg in cargos for sh in cg.out_shapes]
    counts = [len(in_specs), len(c_operands), len(out_shape), len(c_shapes), len(scratch_shapes), 2 * len(cargos)]

    def carrying(*refs):
        groups, pos = [], 0
        for k in counts:
            groups.append(refs[pos:pos + k])
            pos += k
        ins, c_ins, outs, c_outs, scratch, sems = groups
        parts, pi, po = [], 0, 0
        for n, cg in enumerate(cargos):
            parts.append((c_ins[pi:pi + len(cg.operands)], c_outs[po:po + len(cg.out_shapes)], sems[2 * n], sems[2 * n + 1]))
            pi += len(cg.operands)
            po += len(cg.out_shapes)
        step, steps = 0, 1
        for ax, g in enumerate(grid):
            step = step * g + pl.program_id(ax)
            steps *= g
        todo = {}
        for cg, part in zip(cargos, parts):
            for phase, frac in zip(cg.phases[:-1], cg.when[:-1]):
                todo.setdefault(int(round(frac * (steps - 1))), []).append((phase, part))

        for at in sorted(todo):
            @pl.when(step == at)
            def _(at=at):
                for phase, part in todo[at]:
                    phase(*part)

        body(*ins, *outs, *scratch)

        if cargos:
            @pl.when(step == steps - 1)
            def _():
                for cg, part in zip(cargos, parts):
                    cg.phases[-1](*part)

    sems = [pltpu.SemaphoreType.DMA((cg.n_sems,)) for cg in cargos for _ in range(2)]
    aliases, pi, po = {}, counts[0], counts[2]
    for cg in cargos:
        if cg.in_place:
            aliases.update({pi + k: po + k for k in range(len(cg.operands))})
        pi += len(cg.operands)
        po += len(cg.out_shapes)
    outs = pl.pallas_call(
        carrying, name=name, grid=grid,
        in_specs=list(in_specs) + [HBM_SPEC] * counts[1], out_specs=list(out_specs) + [HBM_SPEC] * counts[3],
        out_shape=list(out_shape) + c_shapes, scratch_shapes=list(scratch_shapes) + sems,
        input_output_aliases=aliases, compiler_params=params)(*args, *c_operands)
    own, rest = list(outs[:counts[2]]), list(outs[counts[2]:])
    carried, po = [], 0
    for cg in cargos:
        carried.append(rest[po:po + len(cg.out_shapes)])
        po += len(cg.out_shapes)
    return own, carried


def _run_cargo(cargo, name):
    n_in, n_out = len(cargo.operands), len(cargo.out_shapes)

    def body(*refs):
        c_ins, c_outs, sems = refs[:n_in], refs[n_in:n_in + n_out], refs[n_in + n_out:]
        for phase in cargo.phases:
            phase(c_ins, c_outs, *sems)

    sem = pltpu.SemaphoreType.DMA((cargo.n_sems,))
    return list(pl.pallas_call(body, name=name, out_shape=cargo.out_shapes, in_specs=[HBM_SPEC] * n_in,
                               out_specs=[HBM_SPEC] * n_out, scratch_shapes=[sem, sem])(*cargo.operands))


def _gather_cargo(shards):
    n = len(shards)
    for s in shards:
        assert s.shape[0] % 32 == 0
    slots = 8

    def steps(ins, outs, send_sems, recv_sems):
        x, y, c = _my_place()
        sibling = (x, y, 1 - c)
        over_x, over_y = (1 - x, y, c), (x, 1 - y, c)
        mine, chip_x, chip_y, chip_d = 2 * x + y, 2 * (1 - x) + y, 2 * x + (1 - y), 2 * (1 - x) + (1 - y)

        def rows_of(a, chip_index, half, part=None):
            rps = shards[a].shape[0]
            hr = rps // 2
            first = -(-hr // 32) * 16
            offset, size = {None: (0, hr), 0: (0, first), 1: (first, hr - first)}[part]
            return outs[a].at[pl.ds(pl.multiple_of(chip_index * rps + half * hr + offset, 16), size), :]

        def remote(a, slot, src, dst, to):
            return pltpu.make_async_remote_copy(
                src_ref=src, dst_ref=dst, send_sem=send_sems.at[a * slots + slot], recv_sem=recv_sems.at[a * slots + slot],
                device_id=to, device_id_type=MESH)

        def same_rows(a, slot, rows, to):
            return remote(a, slot, rows, rows, to)

        def own_copy(a):
            rps = shards[a].shape[0]
            return remote(a, 7, ins[a], outs[a].at[pl.ds(pl.multiple_of(mine * rps, 16), rps), :], sibling)

        def my_half(a):
            hr = shards[a].shape[0] // 2
            return ins[a].at[pl.ds(pl.multiple_of(c * hr, 16), hr), :]

        def start():
            for a in range(n):
                own_copy(a).start()
                remote(a, 0, my_half(a), rows_of(a, mine, c), over_x).start()
                remote(a, 1, my_half(a), rows_of(a, mine, c), over_y).start()

        def relay_neighbours():
            for a in range(n):
                same_rows(a, 0, rows_of(a, chip_x, c), over_x).wait_recv()
                same_rows(a, 4, rows_of(a, chip_x, c), sibling).start()
                same_rows(a, 2, rows_of(a, chip_x, c, 0), over_y).start()
                same_rows(a, 1, rows_of(a, chip_y, c), over_y).wait_recv()
                same_rows(a, 5, rows_of(a, chip_y, c), sibling).start()
                same_rows(a, 3, rows_of(a, chip_y, c, 1), over_x).start()

        def relay_diagonal():
            for a in range(n):
                same_rows(a, 2, rows_of(a, chip_d, c, 0), over_y).wait_recv()
                same_rows(a, 3, rows_of(a, chip_d, c, 1), over_x).wait_recv()
                same_rows(a, 6, rows_of(a, chip_d, c), sibling).start()

        def finish():
            for a in range(n):
                for slot, chip_index in ((4, chip_x), (5, chip_y), (6, chip_d)):
                    same_rows(a, slot, rows_of(a, chip_index, 1 - c), sibling).wait_recv()
            for a in range(n):
                remote(a, 0, my_half(a), rows_of(a, mine, c), over_x).wait_send()
                remote(a, 1, my_half(a), rows_of(a, mine, c), over_y).wait_send()
                same_rows(a, 2, rows_of(a, chip_x, c, 0), over_y).wait_send()
                same_rows(a, 3, rows_of(a, chip_y, c, 1), over_x).wait_send()
                for slot, chip_index in ((4, chip_x), (5, chip_y), (6, chip_d)):
                    same_rows(a, slot, rows_of(a, chip_index, c), sibling).wait_send()
                own_copy(a).wait()

        return start, relay_neighbours, relay_diagonal, finish

    phases = [lambda *r, k=k: steps(*r)[k]() for k in range(4)]
    return _Cargo(shards, [jax.ShapeDtypeStruct((N_CHIPS * s.shape[0], s.shape[1]), s.dtype) for s in shards], slots * n,
                  phases, [0.0, 0.6, 0.85, 1.0])


def _exchange_cargo(pairs):
    n = len(pairs)

    def copies(ins, outs, send_sems, recv_sems):
        x, y, c = _my_place()
        return [pltpu.make_async_remote_copy(
            src_ref=ins[a].at[2 * chip[0] + chip[1]], dst_ref=outs[a].at[j],
            send_sem=send_sems.at[3 * a + j], recv_sem=recv_sems.at[3 * a + j], device_id=(*chip, c), device_id_type=MESH)
            for a in range(n) for j, chip in enumerate(_other_chips(x, y))]

    def start(*r):
        for cp in copies(*r):
            cp.start()

    def finish(*r):
        for cp in copies(*r):
            cp.wait()

    return _Cargo(pairs, [jax.ShapeDtypeStruct((3,) + p.shape[1:], p.dtype) for p in pairs], 3 * n, [start, finish], [0.0, 1.0])


def _all_gather_small_cargo(pack):
    rows, cols = pack.shape

    def copies(ins, outs, send_sems, recv_sems):
        x, y, c = _my_place()
        me = 4 * x + 2 * y + c
        remote = []
        for f in range(1, N_DEVICES):
            fx, fy, fc = (f >> 2) & 1, (f >> 1) & 1, f & 1
            to = (1 - x if fx else x, 1 - y if fy else y, 1 - c if fc else c)
            remote.append(pltpu.make_async_remote_copy(
                src_ref=ins[0], dst_ref=outs[0].at[me], send_sem=send_sems.at[f - 1], recv_sem=recv_sems.at[f - 1],
                device_id=to, device_id_type=MESH))
        own = pltpu.make_async_copy(ins[0], outs[0].at[me], send_sems.at[N_DEVICES - 1])
        return remote, own

    def start(*r):
        remote, own = copies(*r)
        own.start()
        for cp in remote:
            cp.start()

    def finish(*r):
        remote, own = copies(*r)
        for cp in remote:
            cp.wait()
        own.wait()

    return _Cargo([pack], [jax.ShapeDtypeStruct((N_DEVICES, rows, cols), F32)], N_DEVICES, [start, finish], [0.0, 1.0])


def _sum_by_device(packs):
    n, rows, cols = packs.shape

    def body(p_ref, o_ref):
        acc = p_ref[0]
        for dev in range(1, n):
            acc = acc + p_ref[dev]
        o_ref[...] = acc

    return pl.pallas_call(body, name="small_grads_sum", out_shape=jax.ShapeDtypeStruct((rows, cols), F32))(packs)


def _chip_sum(pair, got, place, tag):
    _, hr, cols = pair.shape
    br = _row_block(hr, 256)

    def body(k_ref, p_ref, r_ref, o_ref):
        acc = p_ref[...].astype(F32)
        for j in range(3):
            acc = acc + r_ref[j].astype(F32)
        o_ref[...] = acc

    return pl.pallas_call(
        body, name="grad_chip_sum_" + tag,
        out_shape=jax.ShapeDtypeStruct((2, hr, cols), F32),
        grid_spec=pltpu.PrefetchScalarGridSpec(
            num_scalar_prefetch=1, grid=(hr // br,),
            in_specs=[pl.BlockSpec((None, br, cols), lambda r, k_ref: (k_ref[0], r, 0)),
                      pl.BlockSpec((3, br, cols), lambda r, k_ref: (0, r, 0))],
            out_specs=pl.BlockSpec((None, br, cols), lambda r, k_ref: (k_ref[1], r, 0))),
        compiler_params=pltpu.CompilerParams(dimension_semantics=("parallel",)),
    )(place, pair, got)


def _share_cargo(halves):
    n = len(halves)

    def copies(ins, outs, send_sems, recv_sems):
        x, y, c = _my_place()
        return [pltpu.make_async_remote_copy(
            src_ref=outs[a].at[c], dst_ref=outs[a].at[c], send_sem=send_sems.at[a], recv_sem=recv_sems.at[a],
            device_id=(x, y, 1 - c), device_id_type=MESH) for a in range(n)]

    def start(*r):
        for cp in copies(*r):
            cp.start()

    def finish(*r):
        for cp in copies(*r):
            cp.wait()

    return _Cargo(halves, [jax.ShapeDtypeStruct(h.shape, h.dtype) for h in halves], n, [start, finish], [0.0, 1.0], in_place=True)


def _sibling_share(halves):
    n = len(halves)

    def body(*refs):
        outs = refs[n:2 * n]
        send_sems, recv_sems = refs[2 * n:]
        x, y, c = _my_place()
        copies = []
        for a in range(n):
            cp = pltpu.make_async_remote_copy(
                src_ref=outs[a].at[c], dst_ref=outs[a].at[c], send_sem=send_sems.at[a], recv_sem=recv_sems.at[a],
                device_id=(x, y, 1 - c), device_id_type=MESH)
            cp.start()
            copies.append(cp)
        for cp in copies:
            cp.wait()

    return pl.pallas_call(
        body, name="grad_share_sibling",
        out_shape=[jax.ShapeDtypeStruct(h.shape, h.dtype) for h in halves],
        in_specs=[HBM_SPEC] * n, out_specs=[HBM_SPEC] * n,
        input_output_aliases={a: a for a in range(n)},
        scratch_shapes=[pltpu.SemaphoreType.DMA((n,)), pltpu.SemaphoreType.DMA((n,))],
    )(*halves)


def _load_rows(pairs, sems):
    cps = [pltpu.make_async_copy(src, dst, sems.at[j]) for j, (src, dst) in enumerate(pairs)]
    for cp in cps:
        cp.start()
    for cp in cps:
        cp.wait()


def _piece_rows(weights):
    return list(weights), (lambda refs, mats: list(zip(refs, mats))), len(weights)


def _cast_to_bf16(arrays, name, cargo=()):
    rows, cols = arrays[0].shape
    n = len(arrays)
    br = _row_block(rows, 256)

    def body(*refs):
        for src, dst in zip(refs[:n], refs[n:]):
            dst[...] = src[...].astype(BF16)

    blk = pl.BlockSpec((br, cols), lambda i: (i, 0))
    return _launch(body, name=name, grid=(rows // br,), in_specs=[blk] * n, out_specs=[blk] * n,
                   out_shape=[jax.ShapeDtypeStruct((rows, cols), BF16)] * n, scratch_shapes=[], args=tuple(arrays), cargo=cargo)


def _loss_head(xv, gv, tv):
    d = xv.shape[-1]
    r = lax.rsqrt(jnp.mean(xv * xv, axis=-1, keepdims=True) + EPS)
    xhat = xv * r
    err = xhat * gv - tv
    dy = err * (1.0 / d)
    dxh = dy * gv
    dx = r * (dxh - xhat * jnp.mean(dxh * xhat, axis=-1, keepdims=True))
    return dx, jnp.sum(err * err, axis=0, keepdims=True), jnp.sum(dy * xhat, axis=0, keepdims=True)


def _ffn_up(x, g, wg_t, wu_t, name, cargo=()):
    t, d = x.shape
    f = wg_t.shape[0]
    tm = min(TM_FFN, t)
    chunks = _feature_chunks(f, FFN_FWD_CHUNKS)
    flat, copies, n_copies = _piece_rows([wg_t, wu_t])
    nw = len(flat)

    def body(x_ref, g_ref, *rest):
        w_hbm, (a_ref, b_ref, s_ref, wg, wu, sems) = rest[:nw], rest[nw:]

        @pl.when(pl.program_id(0) == 0)
        def _():
            _load_rows(copies(w_hbm, [wg, wu]), sems)

        xv = x_ref[...]
        r = lax.rsqrt(jnp.mean(xv * xv, axis=-1, keepdims=True) + EPS)
        h = (xv * r * g_ref[...]).astype(BF16)
        for s0, sz in chunks:
            a = _nt(h, wg[s0:s0 + sz, :])
            b = _nt(h, wu[s0:s0 + sz, :])
            a_ref[:, s0:s0 + sz] = a.astype(BF16)
            b_ref[:, s0:s0 + sz] = b.astype(BF16)
            s_ref[:, s0:s0 + sz] = (a * _sigmoid(a) * b).astype(BF16)

    tok = lambda i: (i, 0)
    wide = pl.BlockSpec((tm, f), tok)
    return _launch(
        body, name=name, grid=(t // tm,),
        in_specs=[pl.BlockSpec((tm, d), tok), pl.BlockSpec((1, d), lambda i: (0, 0))] + [HBM_SPEC] * nw,
        out_specs=[wide, wide, wide], out_shape=[jax.ShapeDtypeStruct((t, f), BF16)] * 3,
        scratch_shapes=[pltpu.VMEM((f, d), BF16), pltpu.VMEM((f, d), BF16), pltpu.SemaphoreType.DMA((n_copies,))],
        args=(x, g, *flat), cargo=cargo)


def _ffn_down(x, s, wd, name, cargo=(), loss_head=None):
    t, d = x.shape
    f = s.shape[1]
    tm = min(TM_FFN, t)
    flat, copies, n_copies = _piece_rows([wd])
    nw = len(flat)
    nl = 2 if loss_head else 0

    def body(x_ref, s_ref, *rest):
        head, w_hbm = rest[:nl], rest[nl:nl + nw]
        xo_ref = rest[nl + nw]
        sums, (wdn, sems) = rest[nl + nw + 1:nl + nw + 1 + nl], rest[nl + nw + 1 + nl:]

        @pl.when(pl.program_id(0) == 0)
        def _():
            _load_rows(copies(w_hbm, [wdn]), sems)
            for sum_ref in sums:
                sum_ref[...] = jnp.zeros_like(sum_ref)

        xo = x_ref[...] + 0.5 * _nn(s_ref[...], wdn[...])
        if loss_head:
            dx, sq, dgf = _loss_head(xo, head[0][...], head[1][...])
            xo_ref[...] = dx
            sums[0][...] += sq
            sums[1][...] += dgf
        else:
            xo_ref[...] = xo

    tok = lambda i: (i, 0)
    one = lambda i: (0, 0)
    return _launch(
        body, name=name, grid=(t // tm,),
        in_specs=[pl.BlockSpec((tm, d), tok), pl.BlockSpec((tm, f), tok)]
        + ([pl.BlockSpec((1, d), one), pl.BlockSpec((tm, d), tok)] if loss_head else []) + [HBM_SPEC] * nw,
        out_specs=[pl.BlockSpec((tm, d), tok)] + [pl.BlockSpec((1, d), one)] * nl,
        out_shape=[jax.ShapeDtypeStruct((t, d), F32)] + [jax.ShapeDtypeStruct((1, d), F32)] * nl,
        scratch_shapes=[pltpu.VMEM((f, d), BF16), pltpu.SemaphoreType.DMA((n_copies,))],
        args=(x, s, *(loss_head or ()), *flat), cargo=cargo)


def _ffn_backward(dxo, x, g, a, b, wg_t, wu_t, wd, name, cargo=()):
    t, d = x.shape
    f = wd.shape[0]
    tm = min(TM_FFN // 2, t)
    chunks = _feature_chunks(f, FFN_BWD_CHUNKS)
    flat, copies, n_copies = _piece_rows([wg_t, wu_t, wd])
    nw = len(flat)

    def body(dxo_ref, x_ref, g_ref, a_ref, b_ref, *rest):
        w_hbm, (dx_ref, dab_ref, hd_ref, dg_ref, wg, wu, wdn, sems) = rest[:nw], rest[nw:]

        @pl.when(pl.program_id(0) == 0)
        def _():
            _load_rows(copies(w_hbm, [wg, wu, wdn]), sems)
            dg_ref[...] = jnp.zeros_like(dg_ref)

        xv = x_ref[...]
        gv = g_ref[...]
        r = lax.rsqrt(jnp.mean(xv * xv, axis=-1, keepdims=True) + EPS)
        xhat = xv * r
        hd_ref[:, 0:d] = (xhat * gv).astype(BF16)
        dxo_v = dxo_ref[...]
        dout = (0.5 * dxo_v).astype(BF16)
        hd_ref[:, d:2 * d] = dout
        dh = jnp.zeros((tm, d), F32)
        for s0, sz in chunks:
            ds = _nt(dout, wdn[s0:s0 + sz, :])
            av = a_ref[:, s0:s0 + sz].astype(F32)
            bv = b_ref[:, s0:s0 + sz].astype(F32)
            sig = _sigmoid(av)
            silu = av * sig
            da = (ds * bv * (sig * (1.0 + av * (1.0 - sig)))).astype(BF16)
            db = (ds * silu).astype(BF16)
            dab_ref[:, s0:s0 + sz] = da
            dab_ref[:, f + s0:f + s0 + sz] = db
            dh = dh + _nn(da, wg[s0:s0 + sz, :]) + _nn(db, wu[s0:s0 + sz, :])
        dg_ref[...] += jnp.sum(dh * xhat, axis=0, keepdims=True)
        dxh = dh * gv
        dx_ref[...] = dxo_v + r * (dxh - xhat * jnp.mean(dxh * xhat, axis=-1, keepdims=True))

    tok = lambda i: (i, 0)
    one = lambda i: (0, 0)
    return _launch(
        body, name=name, grid=(t // tm,),
        in_specs=[pl.BlockSpec((tm, d), tok), pl.BlockSpec((tm, d), tok), pl.BlockSpec((1, d), one),
                  pl.BlockSpec((tm, f), tok), pl.BlockSpec((tm, f), tok)] + [HBM_SPEC] * nw,
        out_specs=[pl.BlockSpec((tm, d), tok), pl.BlockSpec((tm, 2 * f), tok), pl.BlockSpec((tm, 2 * d), tok),
                   pl.BlockSpec((1, d), one)],
        out_shape=[jax.ShapeDtypeStruct((t, d), F32), jax.ShapeDtypeStruct((t, 2 * f), BF16),
                   jax.ShapeDtypeStruct((t, 2 * d), BF16), jax.ShapeDtypeStruct((1, d), F32)],
        scratch_shapes=[pltpu.VMEM((f, d), BF16), pltpu.VMEM((f, d), BF16), pltpu.VMEM((f, d), BF16), pltpu.SemaphoreType.DMA((n_copies,))],
        args=(dxo, x, g, a, b, *flat), cargo=cargo)


def _weight_grad(lhs, rhs, name, cargo=(), lhs_part=(0, 1), rhs_part=(0, 1)):
    t = lhs.shape[0]
    m = lhs.shape[1] // lhs_part[1]
    d = rhs.shape[1] // rhs_part[1]
    tm = min(TM_TN, t)
    nt = t // tm
    rps = m // N_CHIPS
    hr = rps // 2
    assert hr % 16 == 0

    def body(l_ref, r_ref, o_ref, acc, stage, recv, send_sems, recv_sems):
        i = pl.program_id(0)

        @pl.when(i == 0)
        def _():
            acc[...] = jnp.zeros_like(acc)

        acc[...] += _tn(l_ref[...], r_ref[...])

        @pl.when(i == nt - 1)
        def _():
            x, y, c = _my_place()
            copies = []
            for q in range(N_CHIPS):
                stage[q] = acc[pl.ds(pl.multiple_of(q * rps + (1 - c) * hr, 16), hr), :].astype(BF16)
                cp = pltpu.make_async_remote_copy(
                    src_ref=stage.at[q], dst_ref=recv.at[q], send_sem=send_sems.at[q], recv_sem=recv_sems.at[q],
                    device_id=(x, y, 1 - c), device_id_type=MESH)
                cp.start()
                copies.append(cp)
            for q, cp in enumerate(copies):
                cp.wait_recv()
                mine = acc[pl.ds(pl.multiple_of(q * rps + c * hr, 16), hr), :]
                o_ref[q] = (mine + recv[q].astype(F32)).astype(BF16)
            for cp in copies:
                cp.wait_send()

    outs, carried = _launch(
        body, name=name, grid=(nt,),
        in_specs=[pl.BlockSpec((tm, m), lambda i: (i, lhs_part[0])), pl.BlockSpec((tm, d), lambda i: (i, rhs_part[0]))],
        out_specs=[pl.BlockSpec((N_CHIPS, hr, d), lambda i: (0, 0, 0))],
        out_shape=[jax.ShapeDtypeStruct((N_CHIPS, hr, d), BF16)],
        scratch_shapes=[pltpu.VMEM((m, d), F32), pltpu.VMEM((N_CHIPS, hr, d), BF16), pltpu.VMEM((N_CHIPS, hr, d), BF16),
                        pltpu.SemaphoreType.DMA((N_CHIPS,)), pltpu.SemaphoreType.DMA((N_CHIPS,))],
        args=(lhs, rhs), cargo=cargo)
    return outs[0], carried


def _window_sums(src, cols, w, tm, levels, trailing):
    def read_src(lo, hi):
        return src[lo:hi, cols]

    read, k, level = read_src, 1, 0
    while True:
        last = 2 * k == w
        if trailing:
            lo, hi = (HALO if last else 8 * (level + 1)), HALO + tm
            cur = read(lo, hi) + read(lo - k, hi - k)
        else:
            lo, hi = 0, (tm if last else tm + HALO - 8 * (level + 1))
            cur = read(lo, hi) + read(lo + k, hi + k)
        if last:
            return cur
        levels[level, lo:hi, :] = cur
        read = lambda a, b, level=level: levels[level, a:b, :]
        k, level = 2 * k, level + 1


def _pool_parts(u_cols, ubuf, cols, w, row, tm, levels):
    ws = _window_sums(ubuf, cols, w, tm, levels, trailing=True)
    inv = 1.0 / jnp.minimum(row + 1, w).astype(F32)
    return ws * inv - u_cols, inv


def _mixer_forward(x, g, win_t, wout_x, conv_w, pool_w, pool_scale, cargo=()):
    t, d = x.shape
    dc = win_t.shape[0] // 4
    gcw = dc // len(POOL_WINDOWS)
    wo_rows = d // N_CHIPS
    wo_stride = wout_x.shape[0] // N_CHIPS
    tm = min(TM_MIX // 2, t)

    def body(x_ref, g_ref, win_hbm, wout_hbm, cw_ref, pw_ref, ps_ref, xo_ref, proj_ref, y_ref,
             win, wout, zbuf, ubuf, levels, sems):
        i = pl.program_id(0)

        @pl.when(i == 0)
        def _():
            pairs = [(win_hbm, win)]
            for k in range(N_CHIPS):
                pairs.append((wout_hbm.at[pl.ds(k * wo_stride, wo_rows), :], wout.at[pl.ds(k * wo_rows, wo_rows), :]))
            _load_rows(pairs, sems)
            zbuf[0:8, :] = jnp.zeros((8, dc), F32)
            ubuf[0:HALO, :] = jnp.zeros((HALO, dc), F32)

        xv = x_ref[...]
        r = lax.rsqrt(jnp.mean(xv * xv, axis=-1, keepdims=True) + EPS)
        h = (xv * r * g_ref[...]).astype(BF16)
        v = _nt(h, win[0:dc, :])
        gb = _nt(h, win[dc:2 * dc, :])
        gc = _nt(h, win[2 * dc:3 * dc, :])
        u = _nt(h, win[3 * dc:4 * dc, :])
        proj_ref[:, 0:dc] = v.astype(BF16)
        proj_ref[:, dc:2 * dc] = gb.astype(BF16)
        proj_ref[:, 2 * dc:3 * dc] = gc.astype(BF16)
        proj_ref[:, 3 * dc:4 * dc] = u.astype(BF16)

        z = gc * v
        zbuf[8:8 + tm, :] = z
        cw = cw_ref[...]
        conv = cw[2:3, :] * z + cw[1:2, :] * zbuf[7:7 + tm, :] + cw[0:1, :] * zbuf[6:6 + tm, :]
        y_ref[:, 0:dc] = (gb * conv).astype(BF16)

        ubuf[HALO:HALO + tm, :] = u
        row = i * tm + lax.broadcasted_iota(jnp.int32, (tm, 1), 0)
        for gi, w in enumerate(POOL_WINDOWS):
            cols = slice(gi * gcw, (gi + 1) * gcw)
            pooled, _ = _pool_parts(u[:, cols], ubuf, cols, w, row, tm, levels)
            yb = _nn(pooled.astype(BF16), pw_ref[gi].astype(BF16)) * ps_ref[:, cols]
            y_ref[:, dc + gi * gcw:dc + (gi + 1) * gcw] = yb.astype(BF16)

        xo_ref[...] = xv + _nn(y_ref[...], wout[...])
        zbuf[0:8, :] = zbuf[tm:tm + 8, :]
        ubuf[0:HALO, :] = ubuf[tm:tm + HALO, :]

    tok = lambda i: (i, 0)
    one = lambda i: (0, 0)
    return _launch(
        body, name="mixer_forward", grid=(t // tm,),
        in_specs=[pl.BlockSpec((tm, d), tok), pl.BlockSpec((1, d), one), HBM_SPEC, HBM_SPEC,
                  pl.BlockSpec(conv_w.shape, one), pl.BlockSpec(pool_w.shape, lambda i: (0, 0, 0)), pl.BlockSpec((1, dc), one)],
        out_specs=[pl.BlockSpec((tm, d), tok), pl.BlockSpec((tm, 4 * dc), tok), pl.BlockSpec((tm, 2 * dc), tok)],
        out_shape=[jax.ShapeDtypeStruct((t, d), F32), jax.ShapeDtypeStruct((t, 4 * dc), BF16), jax.ShapeDtypeStruct((t, 2 * dc), BF16)],
        scratch_shapes=[pltpu.VMEM((4 * dc, d), BF16), pltpu.VMEM((2 * dc, d), BF16),
                        pltpu.VMEM((tm + 8, dc), F32), pltpu.VMEM((tm + HALO, dc), F32),
                        pltpu.VMEM((WINDOW_LEVELS, tm + HALO, gcw), F32), pltpu.SemaphoreType.DMA((1 + N_CHIPS,))],
        args=(x, g, win_t, wout_x, conv_w, pool_w, pool_scale), cargo=cargo)


def _mixer_backward(dxo, x, g, proj, win_t, wout_x, conv_w, pool_w, pool_scale, cargo=()):
    t, d = x.shape
    dc = win_t.shape[0] // 4
    ng = len(POOL_WINDOWS)
    gcw = dc // ng
    wo_rows = d // N_CHIPS
    wo_stride = wout_x.shape[0] // N_CHIPS
    tm = min(TM_MIX, t)
    n_tiles = t // tm
    hb = tm // HALO

    def body(dxo_ref, x_ref, g_ref, proj_ref, halo_ref, win_hbm, wout_hbm, cw_ref, pw_ref, ps_ref,
             dx_ref, dproj_ref, h_ref, dxob_ref, dg_ref, dcw_ref, dps_ref, dpw_ref,
             win, wout, zbuf, ubuf, dcbuf, ebuf, levels, sems):
        i = pl.program_id(0)
        tile = n_tiles - 1 - i

        @pl.when(i == 0)
        def _():
            pairs = [(win_hbm, win)]
            for k in range(N_CHIPS):
                pairs.append((wout_hbm.at[pl.ds(k * wo_stride, wo_rows), :], wout.at[pl.ds(k * wo_rows, wo_rows), :]))
            _load_rows(pairs, sems)
            dcbuf[tm:tm + 8, :] = jnp.zeros((8, dc), F32)
            ebuf[tm:tm + HALO, :] = jnp.zeros((HALO, dc), F32)
            dg_ref[...] = jnp.zeros_like(dg_ref)
            dcw_ref[...] = jnp.zeros_like(dcw_ref)
            dps_ref[...] = jnp.zeros_like(dps_ref)
            dpw_ref[...] = jnp.zeros_like(dpw_ref)

        xv = x_ref[...]
        gv = g_ref[...]
        r = lax.rsqrt(jnp.mean(xv * xv, axis=-1, keepdims=True) + EPS)
        xhat = xv * r
        h_ref[...] = (xhat * gv).astype(BF16)
        dxo_v = dxo_ref[...]
        dxo_b = dxo_v.astype(BF16)
        dxob_ref[...] = dxo_b

        v = proj_ref[:, 0:dc].astype(F32)
        gb = proj_ref[:, dc:2 * dc].astype(F32)
        gc = proj_ref[:, 2 * dc:3 * dc].astype(F32)
        u = proj_ref[:, 3 * dc:4 * dc].astype(F32)
        first = jnp.where(tile > 0, 1.0, 0.0)
        zbuf[0:HALO, :] = halo_ref[:, 2 * dc:3 * dc].astype(F32) * halo_ref[:, 0:dc].astype(F32) * first
        ubuf[0:HALO, :] = halo_ref[:, 3 * dc:4 * dc].astype(F32) * first
        z = gc * v
        zbuf[HALO:HALO + tm, :] = z
        ubuf[HALO:HALO + tm, :] = u
        z1 = zbuf[HALO - 1:HALO - 1 + tm, :]
        z2 = zbuf[HALO - 2:HALO - 2 + tm, :]
        cw = cw_ref[...]
        conv = cw[2:3, :] * z + cw[1:2, :] * z1 + cw[0:1, :] * z2

        dy = _nt(dxo_b, wout[...])
        dya = dy[:, 0:dc]
        dgb = dya * conv
        dconv = dya * gb
        dcbuf[0:tm, :] = dconv
        dz = cw[2:3, :] * dconv + cw[1:2, :] * dcbuf[1:1 + tm, :] + cw[0:1, :] * dcbuf[2:2 + tm, :]
        dgc = dz * v
        dv = dz * gc
        dcw_ref[0:1, :] += jnp.sum(dconv * z2, axis=0, keepdims=True)
        dcw_ref[1:2, :] += jnp.sum(dconv * z1, axis=0, keepdims=True)
        dcw_ref[2:3, :] += jnp.sum(dconv * z, axis=0, keepdims=True)

        dproj_ref[:, 0:dc] = dv.astype(BF16)
        dproj_ref[:, dc:2 * dc] = dgb.astype(BF16)
        dproj_ref[:, 2 * dc:3 * dc] = dgc.astype(BF16)

        row = tile * tm + lax.broadcasted_iota(jnp.int32, (tm, 1), 0)
        for gi, w in enumerate(POOL_WINDOWS):
            cols = slice(gi * gcw, (gi + 1) * gcw)
            pooled, inv_cnt = _pool_parts(u[:, cols], ubuf, cols, w, row, tm, levels)
            pooled_b = pooled.astype(BF16)
            pw_b = pw_ref[gi].astype(BF16)
            dyb = dy[:, dc + gi * gcw:dc + (gi + 1) * gcw]
            q = _nn(pooled_b, pw_b)
            dps_ref[:, cols] += jnp.sum(q * dyb, axis=0, keepdims=True)
            dq = (dyb * ps_ref[:, cols]).astype(BF16)
            dpw_ref[gi] += _tn(pooled_b, dq)
            dpooled = _nt(dq, pw_b)
            ebuf[0:tm, cols] = dpooled * inv_cnt
            du = _window_sums(ebuf, cols, w, tm, levels, trailing=False) - dpooled
            dproj_ref[:, 3 * dc + gi * gcw:3 * dc + (gi + 1) * gcw] = du.astype(BF16)

        dh = _nn(dproj_ref[...], win[...])
        dg_ref[...] += jnp.sum(dh * xhat, axis=0, keepdims=True)
        dxh = dh * gv
        dx_ref[...] = dxo_v + r * (dxh - xhat * jnp.mean(dxh * xhat, axis=-1, keepdims=True))
        dcbuf[tm:tm + 8, :] = dcbuf[0:8, :]
        ebuf[tm:tm + HALO, :] = ebuf[0:HALO, :]

    tok = lambda i: (n_tiles - 1 - i, 0)
    halo = lambda i: (jnp.maximum((n_tiles - 1 - i) * hb - 1, 0), 0)
    one = lambda i: (0, 0)
    return _launch(
        body, name="mixer_backward", grid=(n_tiles,),
        in_specs=[pl.BlockSpec((tm, d), tok), pl.BlockSpec((tm, d), tok), pl.BlockSpec((1, d), one),
                  pl.BlockSpec((tm, 4 * dc), tok), pl.BlockSpec((HALO, 4 * dc), halo), HBM_SPEC, HBM_SPEC,
                  pl.BlockSpec(conv_w.shape, one), pl.BlockSpec(pool_w.shape, lambda i: (0, 0, 0)), pl.BlockSpec((1, dc), one)],
        out_specs=[pl.BlockSpec((tm, d), tok), pl.BlockSpec((tm, 4 * dc), tok), pl.BlockSpec((tm, d), tok), pl.BlockSpec((tm, d), tok),
                   pl.BlockSpec((1, d), one), pl.BlockSpec(conv_w.shape, one), pl.BlockSpec((1, dc), one),
                   pl.BlockSpec(pool_w.shape, lambda i: (0, 0, 0))],
        out_shape=[jax.ShapeDtypeStruct((t, d), F32), jax.ShapeDtypeStruct((t, 4 * dc), BF16), jax.ShapeDtypeStruct((t, d), BF16),
                   jax.ShapeDtypeStruct((t, d), BF16), jax.ShapeDtypeStruct((1, d), F32), jax.ShapeDtypeStruct(conv_w.shape, F32),
                   jax.ShapeDtypeStruct((1, dc), F32), jax.ShapeDtypeStruct(pool_w.shape, F32)],
        scratch_shapes=[pltpu.VMEM((4 * dc, d), BF16), pltpu.VMEM((2 * dc, d), BF16),
                        pltpu.VMEM((tm + HALO, dc), F32), pltpu.VMEM((tm + HALO, dc), F32),
                        pltpu.VMEM((tm + 8, dc), F32), pltpu.VMEM((tm + HALO, dc), F32),
                        pltpu.VMEM((WINDOW_LEVELS, tm + HALO, gcw), F32), pltpu.SemaphoreType.DMA((1 + N_CHIPS,))],
        args=(dxo, x, g, proj, proj, win_t, wout_x, conv_w, pool_w, pool_scale), cargo=cargo)


def _adam_update(w, gv, m, v):
    m_new = ADAM_B1 * m + (1.0 - ADAM_B1) * gv
    v_new = ADAM_B2 * v + (1.0 - ADAM_B2) * (gv * gv)
    m_hat = m_new / (1.0 - ADAM_B1 ** ADAM_STEP)
    v_hat = v_new / (1.0 - ADAM_B2 ** ADAM_STEP)
    return -ADAM_LR * (m_hat / (jnp.sqrt(v_hat) + ADAM_EPS) + ADAM_WD * w), m_new, v_new


def _adamw(w, grad, m, v, name):
    rows, cols = w.shape
    br = _row_block(rows, 512) if rows >= 8 else rows

    def body(w_ref, g_ref, m_ref, v_ref, go_ref, d_ref, mo_ref, vo_ref):
        gv = g_ref[...]
        go_ref[...] = gv
        d_ref[...], mo_ref[...], vo_ref[...] = _adam_update(w_ref[...], gv, m_ref[...], v_ref[...])

    blk = pl.BlockSpec((br, cols), lambda i: (i, 0))
    return pl.pallas_call(
        body, name=name,
        out_shape=[jax.ShapeDtypeStruct((rows, cols), F32)] * 4,
        grid=(rows // br,), in_specs=[blk] * 4, out_specs=[blk] * 4,
        compiler_params=pltpu.CompilerParams(dimension_semantics=("parallel",), vmem_limit_bytes=VMEM_LIMIT),
    )(w, grad, m, v)


def _adamw_transposed(w, grad_t, m, v, name):
    _, rows, cols = w.shape
    br = 256 if rows % 256 == 0 else rows

    def body(w_ref, gt_ref, m_ref, v_ref, g_ref, d_ref, mo_ref, vo_ref):
        gv = gt_ref[...].T
        g_ref[...] = gv
        d_ref[...], mo_ref[...], vo_ref[...] = _adam_update(w_ref[...], gv, m_ref[...], v_ref[...])

    blk = pl.BlockSpec((None, br, cols), lambda i: (0, i, 0))
    return pl.pallas_call(
        body, name=name,
        out_shape=[jax.ShapeDtypeStruct((1, rows, cols), F32)] * 4,
        grid=(rows // br,), in_specs=[blk, pl.BlockSpec((cols, br), lambda i: (0, i)), blk, blk], out_specs=[blk] * 4,
        compiler_params=pltpu.CompilerParams(dimension_semantics=("parallel",)),
    )(w, grad_t, m, v)


def _f32_rows_as_bf16(a, rows, cols):
    bits = lax.bitcast_convert_type(a, BF16).reshape(a.shape[0], 2 * a.shape[1])
    return jnp.pad(bits, ((0, rows - bits.shape[0]), (0, cols - bits.shape[1])))


def kernel(x, norm_ffn1, ffn1_w_gate, ffn1_w_up, ffn1_w_down, norm_mix, w_in, conv_w, pool_w, pool_scale, w_out, norm_ffn2, ffn2_w_gate, ffn2_w_up, ffn2_w_down, norm_final, loss_target, m_norm_ffn1, m_ffn1_w_gate, m_ffn1_w_up, m_ffn1_w_down, m_norm_mix, m_w_in, m_conv_w, m_pool_w, m_pool_scale, m_w_out, m_norm_ffn2, m_ffn2_w_gate, m_ffn2_w_up, m_ffn2_w_down, m_norm_final, v_norm_ffn1, v_ffn1_w_gate, v_ffn1_w_up, v_ffn1_w_down, v_norm_mix, v_w_in, v_conv_w, v_pool_w, v_pool_scale, v_w_out, v_norm_ffn2, v_ffn2_w_gate, v_ffn2_w_up, v_ffn2_w_down, v_norm_final):
    weights = dict(norm_ffn1=norm_ffn1, ffn1_w_gate=ffn1_w_gate, ffn1_w_up=ffn1_w_up, ffn1_w_down=ffn1_w_down, norm_mix=norm_mix,
                   w_in=w_in, conv_w=conv_w, pool_w=pool_w, pool_scale=pool_scale, w_out=w_out, norm_ffn2=norm_ffn2,
                   ffn2_w_gate=ffn2_w_gate, ffn2_w_up=ffn2_w_up, ffn2_w_down=ffn2_w_down, norm_final=norm_final)
    first_m = dict(norm_ffn1=m_norm_ffn1, ffn1_w_gate=m_ffn1_w_gate, ffn1_w_up=m_ffn1_w_up, ffn1_w_down=m_ffn1_w_down,
                   norm_mix=m_norm_mix, w_in=m_w_in, conv_w=m_conv_w, pool_w=m_pool_w, pool_scale=m_pool_scale, w_out=m_w_out,
                   norm_ffn2=m_norm_ffn2, ffn2_w_gate=m_ffn2_w_gate, ffn2_w_up=m_ffn2_w_up, ffn2_w_down=m_ffn2_w_down,
                   norm_final=m_norm_final)
    second_m = dict(norm_ffn1=v_norm_ffn1, ffn1_w_gate=v_ffn1_w_gate, ffn1_w_up=v_ffn1_w_up, ffn1_w_down=v_ffn1_w_down,
                    norm_mix=v_norm_mix, w_in=v_w_in, conv_w=v_conv_w, pool_w=v_pool_w, pool_scale=v_pool_scale, w_out=v_w_out,
                    norm_ffn2=v_norm_ffn2, ffn2_w_gate=v_ffn2_w_gate, ffn2_w_up=v_ffn2_w_up, ffn2_w_down=v_ffn2_w_down,
                    norm_final=v_norm_final)
    names = list(weights)

    xs = x[0]
    tgt = loss_target[0]
    t, d = xs.shape
    dc = pool_scale.shape[1]
    cx, cy, cc = _my_place()
    chip = 2 * cx + cy
    place = jnp.stack([chip, cc]).astype(jnp.int32)

    conv_rows = 32
    wout_x = jnp.concatenate([w_out[0].astype(BF16), _f32_rows_as_bf16(conv_w[0], conv_rows, d)], axis=0)

    g1, gm, g2 = norm_ffn1, norm_mix, norm_ffn2
    gf = norm_final.reshape(1, d)
    pw = pool_w[0]

    (wd1_shard, wg2_shard, wu2_shard, wd2_shard), [(wg1, wu1)] = _cast_to_bf16(
        [ffn1_w_down[0], ffn2_w_gate[0].T, ffn2_w_up[0].T, ffn2_w_down[0]], "gather_ffn1",
        [_gather_cargo([ffn1_w_gate[0].T.astype(BF16), ffn1_w_up[0].T.astype(BF16)])])
    (a1, b1, s1), [(wd1, win_t, wout_g)] = _ffn_up(
        xs, g1, wg1, wu1, "ffn1_up", [_gather_cargo([wd1_shard, w_in[0].T.astype(BF16), wout_x])])
    (x1,), [(wg2,)] = _ffn_down(xs, s1, wd1, "ffn1_down", [_gather_cargo([wg2_shard])])
    wo_rows = w_out.shape[1]
    cshard = conv_w.shape[2]
    conv_bits = wout_g.reshape(N_CHIPS, wo_rows + conv_rows, d)[:, wo_rows:wo_rows + conv_w.shape[1], :2 * cshard]
    conv_full = lax.bitcast_convert_type(conv_bits.reshape(N_CHIPS, conv_w.shape[1], cshard, 2), F32)
    conv_full = jnp.transpose(conv_full, (1, 0, 2)).reshape(conv_w.shape[1], N_CHIPS * cshard)
    (x2, proj, ymix), [(wu2,)] = _mixer_forward(x1, gm, win_t, wout_g, conv_full, pw, pool_scale, [_gather_cargo([wu2_shard])])
    (a2, b2, s2), [(wd2,)] = _ffn_up(x2, g2, wg2, wu2, "ffn2_up", [_gather_cargo([wd2_shard])])
    (dx3, sq_cols, dgf), _ = _ffn_down(x2, s2, wd2, "ffn2_down", loss_head=(gf, tgt))

    (dx2, dab2, hd2, dg2), _ = _ffn_backward(dx3, x2, g2, a2, b2, wg2, wu2, wd2, "ffn2_backward")
    p_wg2, _ = _weight_grad(dab2, hd2, "ffn2_gate_grad", lhs_part=(0, 2), rhs_part=(0, 2))
    p_wu2, [(x_wg2,)] = _weight_grad(dab2, hd2, "ffn2_up_grad", [_exchange_cargo([p_wg2])], lhs_part=(1, 2), rhs_part=(0, 2))
    p_wd2, [(x_wu2,)] = _weight_grad(s2, hd2, "ffn2_down_grad", [_exchange_cargo([p_wu2])], rhs_part=(1, 2))

    (dx1, dproj, h2, dx2b, dgm, dcw, dps, dpw), [(x_wd2,)] = _mixer_backward(
        dx2, x1, gm, proj, win_t, wout_g, conv_full, pw, pool_scale, [_exchange_cargo([p_wd2])])

    (dx0, dab1, hd1, dg1), _ = _ffn_backward(dx1, xs, g1, a1, b1, wg1, wu1, wd1, "ffn1_backward")

    npw = pw.size // d
    head = [dg1, dgm, dg2, dgf, jnp.pad(dps, ((0, 0), (0, d - dc))), jnp.pad(dcw, ((0, 0), (0, d - dc))), sq_cols]
    n_head = sum(h.shape[0] for h in head)
    base = -(-n_head // 8) * 8
    pack = jnp.concatenate(head + [jnp.zeros((base - n_head, d), F32), dpw.reshape(npw, d)], axis=0)

    p_wg1, [(packs,)] = _weight_grad(dab1, hd1, "ffn1_gate_grad", [_all_gather_small_cargo(pack)], lhs_part=(0, 2), rhs_part=(0, 2))
    p_wu1, [(x_wg1,)] = _weight_grad(dab1, hd1, "ffn1_up_grad", [_exchange_cargo([p_wg1])], lhs_part=(1, 2), rhs_part=(0, 2))
    p_wd1, [(x_wu1,)] = _weight_grad(s1, hd1, "ffn1_down_grad", [_exchange_cargo([p_wu1])], rhs_part=(1, 2))
    p_win, [(x_wd1,)] = _weight_grad(dproj, h2, "w_in_grad", [_exchange_cargo([p_wd1])])

    pairs = dict(wg1=p_wg1, wu1=p_wu1, wd1=p_wd1, wg2=p_wg2, wu2=p_wu2, wd2=p_wd2)
    landed = dict(wg1=x_wg1, wu1=x_wu1, wd1=x_wd1, wg2=x_wg2, wu2=x_wu2, wd2=x_wd2)
    ffn_halves = [_chip_sum(pairs[k], landed[k], place, k) for k in ["wg1", "wu1", "wd1", "wg2", "wu2", "wd2"]]
    p_wout, [(x_win,), ffn_both] = _weight_grad(ymix, dx2b, "w_out_grad", [_exchange_cargo([p_win]), _share_cargo(ffn_halves)])
    x_wout, = _run_cargo(_exchange_cargo([p_wout]), "grad_exchange_last")
    mix_both = _sibling_share([_chip_sum(p_win, x_win, place, "win"), _chip_sum(p_wout, x_wout, place, "wout")])
    rwg1, rwu1, rwd1, rwg2, rwu2, rwd2, rwin, rwout = [b.reshape(2 * b.shape[1], b.shape[2]) for b in list(ffn_both) + list(mix_both)]
    small = _sum_by_device(packs)
    loss = jnp.sum(small[n_head - 1]) * (0.5 / d)

    grads = {
        "norm_ffn1": small[0:1], "norm_mix": small[1:2], "norm_ffn2": small[2:3], "norm_final": small[3],
        "pool_scale": small[4:5, :dc],
        "conv_w": lax.dynamic_slice_in_dim(small[5:5 + dcw.shape[0], :dc], chip * cshard, cshard, axis=1)[None],
        "pool_w": small[base:].reshape(pool_w.shape),
        "ffn1_w_down": rwd1[None], "w_out": rwout[None], "ffn2_w_down": rwd2[None],
    }
    by_view = {"ffn1_w_gate": rwg1, "ffn1_w_up": rwu1, "ffn2_w_gate": rwg2, "ffn2_w_up": rwu2}

    deltas, new_m, new_v = {}, {}, {}
    for n in names:
        w = weights[n]
        shape = w.shape
        if n == "w_in":
            grads[n], deltas[n], new_m[n], new_v[n] = _adamw_transposed(w, rwin, first_m[n], second_m[n], "adamw_" + n)
            continue
        if n in by_view:
            view = lambda a: jnp.swapaxes(a, 1, 2)[0]
            back = lambda a: jnp.swapaxes(a[None], 1, 2)
            outs = _adamw(view(w), by_view[n], view(first_m[n]), view(second_m[n]), "adamw_" + n)
            grads[n], deltas[n], new_m[n], new_v[n] = [back(o) for o in outs]
            continue
        as2d = (lambda a: a.reshape(-1, shape[-1]))
        outs = _adamw(as2d(w), as2d(grads[n]), as2d(first_m[n]), as2d(second_m[n]), "adamw_" + n)
        grads[n], deltas[n], new_m[n], new_v[n] = [o.reshape(shape) for o in outs]

    return (loss, dx0[None], *[grads[n] for n in names], *[deltas[n] for n in names],
            *[new_m[n] for n in names], *[new_v[n] for n in names])
```

```python
import jax
import jax.numpy as jnp
from jax import lax
from jax.experimental import pallas as pl
from jax.experimental.pallas import tpu as pltpu

F32 = jnp.float32
BF16 = jnp.bfloat16
MESH = pl.DeviceIdType.MESH

EPS = 1e-6
POOL_WINDOWS = (2, 4, 8, 16)
ADAM_LR = 0.001
ADAM_B1 = 0.9
ADAM_B2 = 0.999
ADAM_EPS = 1e-08
ADAM_WD = 0.01
ADAM_STEP = 10

N_CHIPS = 4
N_DEVICES = 8
MXU_COLS_V7X = 256
VMEM_LIMIT = 56 * 1024 * 1024
TM_FFN = 512
TM_MIX = 512
TM_TN = 512
HALO = 32
WINDOW_LEVELS = 3
FFN_FWD_CHUNKS = 2
FFN_BWD_CHUNKS = 2


def _nt(a, b):
    return lax.dot_general(a, b, (((1,), (1,)), ((), ())), preferred_element_type=F32)


def _tn(a, b):
    return lax.dot_general(a, b, (((0,), (0,)), ((), ())), preferred_element_type=F32)


def _nn(a, b):
    return jnp.dot(a, b, preferred_element_type=F32)


def _sigmoid(a):
    return 1.0 / (1.0 + jnp.exp(-a))


def _feature_chunks(n, parts):
    assert n % MXU_COLS_V7X == 0
    tiles = n // MXU_COLS_V7X
    out, s0 = [], 0
    for p in range(parts):
        sz = (tiles // parts + (1 if p < tiles % parts else 0)) * MXU_COLS_V7X
        if sz:
            out.append((s0, sz))
            s0 += sz
    return out


def _row_block(rows, cap):
    best = 8
    for b in range(8, min(rows, cap) + 1, 8):
        if rows % b == 0:
            best = b
    assert rows % best == 0
    return best


def _my_place():
    return lax.axis_index("x"), lax.axis_index("y"), lax.axis_index("c")


def _other_chips(x, y):
    return [(1 - x, y), (x, 1 - y), (1 - x, 1 - y)]


HBM_SPEC = pl.BlockSpec(memory_space=pltpu.HBM)


class _Cargo:
    def __init__(self, operands, out_shapes, n_sems, phases, when, in_place=False):
        self.operands, self.out_shapes, self.n_sems = list(operands), list(out_shapes), n_sems
        self.phases, self.when = list(phases), list(when)
        self.in_place = in_place
        assert len(self.phases) == len(self.when) and self.when[0] == 0.0 and self.when[-1] == 1.0


def _launch(body, *, name, grid, in_specs, out_specs, out_shape, scratch_shapes, args, cargo=()):
    params = pltpu.CompilerParams(dimension_semantics=("arbitrary",) * len(grid), vmem_limit_bytes=VMEM_LIMIT)
    cargos = list(cargo)
    c_operands = [op for cg in cargos for op in cg.operands]
    c_shapes = [sh for cg in cargos for sh in cg.out_shapes]
    counts = [len(in_specs), len(c_operands), len(out_shape), len(c_shapes), len(scratch_shapes), 2 * len(cargos)]

    def carrying(*refs):
        groups, pos = [], 0
        for k in counts:
            groups.append(refs[pos:pos + k])
            pos += k
        ins, c_ins, outs, c_outs, scratch, sems = groups
        parts, pi, po = [], 0, 0
        for n, cg in enumerate(cargos):
            parts.append((c_ins[pi:pi + len(cg.operands)], c_outs[po:po + len(cg.out_shapes)], sems[2 * n], sems[2 * n + 1]))
            pi += len(cg.operands)
            po += len(cg.out_shapes)
        step, steps = 0, 1
        for ax, g in enumerate(grid):
            step = step * g + pl.program_id(ax)
            steps *= g
        todo = {}
        for cg, part in zip(cargos, parts):
            for phase, frac in zip(cg.phases[:-1], cg.when[:-1]):
                todo.setdefault(int(round(frac * (steps - 1))), []).append((phase, part))

        for at in sorted(todo):
            @pl.when(step == at)
            def _(at=at):
                for phase, part in todo[at]:
                    phase(*part)

        body(*ins, *outs, *scratch)

        if cargos:
            @pl.when(step == steps - 1)
            def _():
                for cg, part in zip(cargos, parts):
                    cg.phases[-1](*part)

    sems = [pltpu.SemaphoreType.DMA((cg.n_sems,)) for cg in cargos for _ in range(2)]
    aliases, pi, po = {}, counts[0], counts[2]
    for cg in cargos:
        if cg.in_place:
            aliases.update({pi + k: po + k for k in range(len(cg.operands))})
        pi += len(cg.operands)
        po += len(cg.out_shapes)
    outs = pl.pallas_call(
        carrying, name=name, grid=grid,
        in_specs=list(in_specs) + [HBM_SPEC] * counts[1], out_specs=list(out_specs) + [HBM_SPEC] * counts[3],
        out_shape=list(out_shape) + c_shapes, scratch_shapes=list(scratch_shapes) + sems,
        input_output_aliases=aliases, compiler_params=params)(*args, *c_operands)
    own, rest = list(outs[:counts[2]]), list(outs[counts[2]:])
    carried, po = [], 0
    for cg in cargos:
        carried.append(rest[po:po + len(cg.out_shapes)])
        po += len(cg.out_shapes)
    return own, carried


def _run_cargo(cargo, name):
    n_in, n_out = len(cargo.operands), len(cargo.out_shapes)

    def body(*refs):
        c_ins, c_outs, sems = refs[:n_in], refs[n_in:n_in + n_out], refs[n_in + n_out:]
        for phase in cargo.phases:
            phase(c_ins, c_outs, *sems)

    sem = pltpu.SemaphoreType.DMA((cargo.n_sems,))
    return list(pl.pallas_call(body, name=name, out_shape=cargo.out_shapes, in_specs=[HBM_SPEC] * n_in,
                               out_specs=[HBM_SPEC] * n_out, scratch_shapes=[sem, sem])(*cargo.operands))


def _gather_cargo(shards):
    n = len(shards)
    for s in shards:
        assert s.shape[0] % 32 == 0
    slots = 8

    def steps(ins, outs, send_sems, recv_sems):
        x, y, c = _my_place()
        sibling = (x, y, 1 - c)
        over_x, over_y = (1 - x, y, c), (x, 1 - y, c)
        mine, chip_x, chip_y, chip_d = 2 * x + y, 2 * (1 - x) + y, 2 * x + (1 - y), 2 * (1 - x) + (1 - y)

        def rows_of(a, chip_index, half, part=None):
            rps = shards[a].shape[0]
            hr = rps // 2
            first = -(-hr // 32) * 16
            offset, size = {None: (0, hr), 0: (0, first), 1: (first, hr - first)}[part]
            return outs[a].at[pl.ds(pl.multiple_of(chip_index * rps + half * hr + offset, 16), size), :]

        def remote(a, slot, src, dst, to):
            return pltpu.make_async_remote_copy(
                src_ref=src, dst_ref=dst, send_sem=send_sems.at[a * slots + slot], recv_sem=recv_sems.at[a * slots + slot],
                device_id=to, device_id_type=MESH)

        def same_rows(a, slot, rows, to):
            return remote(a, slot, rows, rows, to)

        def own_copy(a):
            rps = shards[a].shape[0]
            return remote(a, 7, ins[a], outs[a].at[pl.ds(pl.multiple_of(mine * rps, 16), rps), :], sibling)

        def my_half(a):
            hr = shards[a].shape[0] // 2
            return ins[a].at[pl.ds(pl.multiple_of(c * hr, 16), hr), :]

        def start():
            for a in range(n):
                own_copy(a).start()
                remote(a, 0, my_half(a), rows_of(a, mine, c), over_x).start()
                remote(a, 1, my_half(a), rows_of(a, mine, c), over_y).start()

        def relay_neighbours():
            for a in range(n):
                same_rows(a, 0, rows_of(a, chip_x, c), over_x).wait_recv()
                same_rows(a, 4, rows_of(a, chip_x, c), sibling).start()
                same_rows(a, 2, rows_of(a, chip_x, c, 0), over_y).start()
                same_rows(a, 1, rows_of(a, chip_y, c), over_y).wait_recv()
                same_rows(a, 5, rows_of(a, chip_y, c), sibling).start()
                same_rows(a, 3, rows_of(a, chip_y, c, 1), over_x).start()

        def relay_diagonal():
            for a in range(n):
                same_rows(a, 2, rows_of(a, chip_d, c, 0), over_y).wait_recv()
                same_rows(a, 3, rows_of(a, chip_d, c, 1), over_x).wait_recv()
                same_rows(a, 6, rows_of(a, chip_d, c), sibling).start()

        def finish():
            for a in range(n):
                for slot, chip_index in ((4, chip_x), (5, chip_y), (6, chip_d)):
                    same_rows(a, slot, rows_of(a, chip_index, 1 - c), sibling).wait_recv()
            for a in range(n):
                remote(a, 0, my_half(a), rows_of(a, mine, c), over_x).wait_send()
                remote(a, 1, my_half(a), rows_of(a, mine, c), over_y).wait_send()
                same_rows(a, 2, rows_of(a, chip_x, c, 0), over_y).wait_send()
                same_rows(a, 3, rows_of(a, chip_y, c, 1), over_x).wait_send()
                for slot, chip_index in ((4, chip_x), (5, chip_y), (6, chip_d)):
                    same_rows(a, slot, rows_of(a, chip_index, c), sibling).wait_send()
                own_copy(a).wait()

        return start, relay_neighbours, relay_diagonal, finish

    phases = [lambda *r, k=k: steps(*r)[k]() for k in range(4)]
    return _Cargo(shards, [jax.ShapeDtypeStruct((N_CHIPS * s.shape[0], s.shape[1]), s.dtype) for s in shards], slots * n,
                  phases, [0.0, 0.6, 0.85, 1.0])


def _exchange_cargo(pairs):
    n = len(pairs)

    def copies(ins, outs, send_sems, recv_sems):
        x, y, c = _my_place()
        return [pltpu.make_async_remote_copy(
            src_ref=ins[a].at[2 * chip[0] + chip[1]], dst_ref=outs[a].at[j],
            send_sem=send_sems.at[3 * a + j], recv_sem=recv_sems.at[3 * a + j], device_id=(*chip, c), device_id_type=MESH)
            for a in range(n) for j, chip in enumerate(_other_chips(x, y))]

    def start(*r):
        for cp in copies(*r):
            cp.start()

    def finish(*r):
        for cp in copies(*r):
            cp.wait()

    return _Cargo(pairs, [jax.ShapeDtypeStruct((3,) + p.shape[1:], p.dtype) for p in pairs], 3 * n, [start, finish], [0.0, 1.0])


def _all_gather_small_cargo(pack):
    rows, cols = pack.shape

    def copies(ins, outs, send_sems, recv_sems):
        x, y, c = _my_place()
        me = 4 * x + 2 * y + c
        remote = []
        for f in range(1, N_DEVICES):
            fx, fy, fc = (f >> 2) & 1, (f >> 1) & 1, f & 1
            to = (1 - x if fx else x, 1 - y if fy else y, 1 - c if fc else c)
            remote.append(pltpu.make_async_remote_copy(
                src_ref=ins[0], dst_ref=outs[0].at[me], send_sem=send_sems.at[f - 1], recv_sem=recv_sems.at[f - 1],
                device_id=to, device_id_type=MESH))
        own = pltpu.make_async_copy(ins[0], outs[0].at[me], send_sems.at[N_DEVICES - 1])
        return remote, own

    def start(*r):
        remote, own = copies(*r)
        own.start()
        for cp in remote:
            cp.start()

    def finish(*r):
        remote, own = copies(*r)
        for cp in remote:
            cp.wait()
        own.wait()

    return _Cargo([pack], [jax.ShapeDtypeStruct((N_DEVICES, rows, cols), F32)], N_DEVICES, [start, finish], [0.0, 1.0])


def _sum_by_device(packs):
    n, rows, cols = packs.shape

    def body(p_ref, o_ref):
        acc = p_ref[0]
        for dev in range(1, n):
            acc = acc + p_ref[dev]
        o_ref[...] = acc

    return pl.pallas_call(body, name="small_grads_sum", out_shape=jax.ShapeDtypeStruct((rows, cols), F32))(packs)


def _chip_sum(pair, got, place, tag):
    _, hr, cols = pair.shape
    br = _row_block(hr, 256)

    def body(k_ref, p_ref, r_ref, o_ref):
        acc = p_ref[...].astype(F32)
        for j in range(3):
            acc = acc + r_ref[j].astype(F32)
        o_ref[...] = acc

    return pl.pallas_call(
        body, name="grad_chip_sum_" + tag,
        out_shape=jax.ShapeDtypeStruct((2, hr, cols), F32),
        grid_spec=pltpu.PrefetchScalarGridSpec(
            num_scalar_prefetch=1, grid=(hr // br,),
            in_specs=[pl.BlockSpec((None, br, cols), lambda r, k_ref: (k_ref[0], r, 0)),
                      pl.BlockSpec((3, br, cols), lambda r, k_ref: (0, r, 0))],
            out_specs=pl.BlockSpec((None, br, cols), lambda r, k_ref: (k_ref[1], r, 0))),
        compiler_params=pltpu.CompilerParams(dimension_semantics=("parallel",)),
    )(place, pair, got)


def _share_cargo(halves):
    n = len(halves)

    def copies(ins, outs, send_sems, recv_sems):
        x, y, c = _my_place()
        return [pltpu.make_async_remote_copy(
            src_ref=outs[a].at[c], dst_ref=outs[a].at[c], send_sem=send_sems.at[a], recv_sem=recv_sems.at[a],
            device_id=(x, y, 1 - c), device_id_type=MESH) for a in range(n)]

    def start(*r):
        for cp in copies(*r):
            cp.start()

    def finish(*r):
        for cp in copies(*r):
            cp.wait()

    return _Cargo(halves, [jax.ShapeDtypeStruct(h.shape, h.dtype) for h in halves], n, [start, finish], [0.0, 1.0], in_place=True)


def _sibling_share(halves):
    n = len(halves)

    def body(*refs):
        outs = refs[n:2 * n]
        send_sems, recv_sems = refs[2 * n:]
        x, y, c = _my_place()
        copies = []
        for a in range(n):
            cp = pltpu.make_async_remote_copy(
                src_ref=outs[a].at[c], dst_ref=outs[a].at[c], send_sem=send_sems.at[a], recv_sem=recv_sems.at[a],
                device_id=(x, y, 1 - c), device_id_type=MESH)
            cp.start()
            copies.append(cp)
        for cp in copies:
            cp.wait()

    return pl.pallas_call(
        body, name="grad_share_sibling",
        out_shape=[jax.ShapeDtypeStruct(h.shape, h.dtype) for h in halves],
        in_specs=[HBM_SPEC] * n, out_specs=[HBM_SPEC] * n,
        input_output_aliases={a: a for a in range(n)},
        scratch_shapes=[pltpu.SemaphoreType.DMA((n,)), pltpu.SemaphoreType.DMA((n,))],
    )(*halves)


def _load_rows(pairs, sems):
    cps = [pltpu.make_async_copy(src, dst, sems.at[j]) for j, (src, dst) in enumerate(pairs)]
    for cp in cps:
        cp.start()
    for cp in cps:
        cp.wait()


def _piece_rows(weights):
    return list(weights), (lambda refs, mats: list(zip(refs, mats))), len(weights)


def _cast_to_bf16(arrays, name, cargo=()):
    rows, cols = arrays[0].shape
    n = len(arrays)
    br = _row_block(rows, 256)

    def body(*refs):
        for src, dst in zip(refs[:n], refs[n:]):
            dst[...] = src[...].astype(BF16)

    blk = pl.BlockSpec((br, cols), lambda i: (i, 0))
    return _launch(body, name=name, grid=(rows // br,), in_specs=[blk] * n, out_specs=[blk] * n,
                   out_shape=[jax.ShapeDtypeStruct((rows, cols), BF16)] * n, scratch_shapes=[], args=tuple(arrays), cargo=cargo)


def _loss_head(xv, gv, tv):
    d = xv.shape[-1]
    r = lax.rsqrt(jnp.mean(xv * xv, axis=-1, keepdims=True) + EPS)
    xhat = xv * r
    err = xhat * gv - tv
    dy = err * (1.0 / d)
    dxh = dy * gv
    dx = r * (dxh - xhat * jnp.mean(dxh * xhat, axis=-1, keepdims=True))
    return dx, jnp.sum(err * err, axis=0, keepdims=True), jnp.sum(dy * xhat, axis=0, keepdims=True)


def _ffn_up(x, g, wg_t, wu_t, name, cargo=()):
    t, d = x.shape
    f = wg_t.shape[0]
    tm = min(TM_FFN, t)
    chunks = _feature_chunks(f, FFN_FWD_CHUNKS)
    flat, copies, n_copies = _piece_rows([wg_t, wu_t])
    nw = len(flat)

    def body(x_ref, g_ref, *rest):
        w_hbm, (a_ref, b_ref, s_ref, wg, wu, sems) = rest[:nw], rest[nw:]

        @pl.when(pl.program_id(0) == 0)
        def _():
            _load_rows(copies(w_hbm, [wg, wu]), sems)

        xv = x_ref[...]
        r = lax.rsqrt(jnp.mean(xv * xv, axis=-1, keepdims=True) + EPS)
        h = (xv * r * g_ref[...]).astype(BF16)
        for s0, sz in chunks:
            a = _nt(h, wg[s0:s0 + sz, :])
            b = _nt(h, wu[s0:s0 + sz, :])
            a_ref[:, s0:s0 + sz] = a.astype(BF16)
            b_ref[:, s0:s0 + sz] = b.astype(BF16)
            s_ref[:, s0:s0 + sz] = (a * _sigmoid(a) * b).astype(BF16)

    tok = lambda i: (i, 0)
    wide = pl.BlockSpec((tm, f), tok)
    return _launch(
        body, name=name, grid=(t // tm,),
        in_specs=[pl.BlockSpec((tm, d), tok), pl.BlockSpec((1, d), lambda i: (0, 0))] + [HBM_SPEC] * nw,
        out_specs=[wide, wide, wide], out_shape=[jax.ShapeDtypeStruct((t, f), BF16)] * 3,
        scratch_shapes=[pltpu.VMEM((f, d), BF16), pltpu.VMEM((f, d), BF16), pltpu.SemaphoreType.DMA((n_copies,))],
        args=(x, g, *flat), cargo=cargo)


def _ffn_down(x, s, wd, name, cargo=(), loss_head=None):
    t, d = x.shape
    f = s.shape[1]
    tm = min(TM_FFN, t)
    flat, copies, n_copies = _piece_rows([wd])
    nw = len(flat)
    nl = 2 if loss_head else 0

    def body(x_ref, s_ref, *rest):
        head, w_hbm = rest[:nl], rest[nl:nl + nw]
        xo_ref = rest[nl + nw]
        sums, (wdn, sems) = rest[nl + nw + 1:nl + nw + 1 + nl], rest[nl + nw + 1 + nl:]

        @pl.when(pl.program_id(0) == 0)
        def _():
            _load_rows(copies(w_hbm, [wdn]), sems)
            for sum_ref in sums:
                sum_ref[...] = jnp.zeros_like(sum_ref)

        xo = x_ref[...] + 0.5 * _nn(s_ref[...], wdn[...])
        if loss_head:
            dx, sq, dgf = _loss_head(xo, head[0][...], head[1][...])
            xo_ref[...] = dx
            sums[0][...] += sq
            sums[1][...] += dgf
        else:
            xo_ref[...] = xo

    tok = lambda i: (i, 0)
    one = lambda i: (0, 0)
    return _launch(
        body, name=name, grid=(t // tm,),
        in_specs=[pl.BlockSpec((tm, d), tok), pl.BlockSpec((tm, f), tok)]
        + ([pl.BlockSpec((1, d), one), pl.BlockSpec((tm, d), tok)] if loss_head else []) + [HBM_SPEC] * nw,
        out_specs=[pl.BlockSpec((tm, d), tok)] + [pl.BlockSpec((1, d), one)] * nl,
        out_shape=[jax.ShapeDtypeStruct((t, d), F32)] + [jax.ShapeDtypeStruct((1, d), F32)] * nl,
        scratch_shapes=[pltpu.VMEM((f, d), BF16), pltpu.SemaphoreType.DMA((n_copies,))],
        args=(x, s, *(loss_head or ()), *flat), cargo=cargo)


def _ffn_backward(dxo, x, g, a, b, wg_t, wu_t, wd, name, cargo=()):
    t, d = x.shape
    f = wd.shape[0]
    tm = min(TM_FFN // 2, t)
    chunks = _feature_chunks(f, FFN_BWD_CHUNKS)
    flat, copies, n_copies = _piece_rows([wg_t, wu_t, wd])
    nw = len(flat)

    def body(dxo_ref, x_ref, g_ref, a_ref, b_ref, *rest):
        w_hbm, (dx_ref, dab_ref, hd_ref, dg_ref, wg, wu, wdn, sems) = rest[:nw], rest[nw:]

        @pl.when(pl.program_id(0) == 0)
        def _():
            _load_rows(copies(w_hbm, [wg, wu, wdn]), sems)
            dg_ref[...] = jnp.zeros_like(dg_ref)

        xv = x_ref[...]
        gv = g_ref[...]
        r = lax.rsqrt(jnp.mean(xv * xv, axis=-1, keepdims=True) + EPS)
        xhat = xv * r
        hd_ref[:, 0:d] = (xhat * gv).astype(BF16)
        dxo_v = dxo_ref[...]
        dout = (0.5 * dxo_v).astype(BF16)
        hd_ref[:, d:2 * d] = dout
        dh = jnp.zeros((tm, d), F32)
        for s0, sz in chunks:
            ds = _nt(dout, wdn[s0:s0 + sz, :])
            av = a_ref[:, s0:s0 + sz].astype(F32)
            bv = b_ref[:, s0:s0 + sz].astype(F32)
            sig = _sigmoid(av)
            silu = av * sig
            da = (ds * bv * (sig * (1.0 + av * (1.0 - sig)))).astype(BF16)
            db = (ds * silu).astype(BF16)
            dab_ref[:, s0:s0 + sz] = da
            dab_ref[:, f + s0:f + s0 + sz] = db
            dh = dh + _nn(da, wg[s0:s0 + sz, :]) + _nn(db, wu[s0:s0 + sz, :])
        dg_ref[...] += jnp.sum(dh * xhat, axis=0, keepdims=True)
        dxh = dh * gv
        dx_ref[...] = dxo_v + r * (dxh - xhat * jnp.mean(dxh * xhat, axis=-1, keepdims=True))

    tok = lambda i: (i, 0)
    one = lambda i: (0, 0)
    return _launch(
        body, name=name, grid=(t // tm,),
        in_specs=[pl.BlockSpec((tm, d), tok), pl.BlockSpec((tm, d), tok), pl.BlockSpec((1, d), one),
                  pl.BlockSpec((tm, f), tok), pl.BlockSpec((tm, f), tok)] + [HBM_SPEC] * nw,
        out_specs=[pl.BlockSpec((tm, d), tok), pl.BlockSpec((tm, 2 * f), tok), pl.BlockSpec((tm, 2 * d), tok),
                   pl.BlockSpec((1, d), one)],
        out_shape=[jax.ShapeDtypeStruct((t, d), F32), jax.ShapeDtypeStruct((t, 2 * f), BF16),
                   jax.ShapeDtypeStruct((t, 2 * d), BF16), jax.ShapeDtypeStruct((1, d), F32)],
        scratch_shapes=[pltpu.VMEM((f, d), BF16), pltpu.VMEM((f, d), BF16), pltpu.VMEM((f, d), BF16), pltpu.SemaphoreType.DMA((n_copies,))],
        args=(dxo, x, g, a, b, *flat), cargo=cargo)


def _weight_grad(lhs, rhs, name, cargo=(), lhs_part=(0, 1), rhs_part=(0, 1)):
    t = lhs.shape[0]
    m = lhs.shape[1] // lhs_part[1]
    d = rhs.shape[1] // rhs_part[1]
    tm = min(TM_TN, t)
    nt = t // tm
    rps = m // N_CHIPS
    hr = rps // 2
    assert hr % 16 == 0

    def body(l_ref, r_ref, o_ref, acc, stage, recv, send_sems, recv_sems):
        i = pl.program_id(0)

        @pl.when(i == 0)
        def _():
            acc[...] = jnp.zeros_like(acc)

        acc[...] += _tn(l_ref[...], r_ref[...])

        @pl.when(i == nt - 1)
        def _():
            x, y, c = _my_place()
            copies = []
            for q in range(N_CHIPS):
                stage[q] = acc[pl.ds(pl.multiple_of(q * rps + (1 - c) * hr, 16), hr), :].astype(BF16)
                cp = pltpu.make_async_remote_copy(
                    src_ref=stage.at[q], dst_ref=recv.at[q], send_sem=send_sems.at[q], recv_sem=recv_sems.at[q],
                    device_id=(x, y, 1 - c), device_id_type=MESH)
                cp.start()
                copies.append(cp)
            for q, cp in enumerate(copies):
                cp.wait_recv()
                mine = acc[pl.ds(pl.multiple_of(q * rps + c * hr, 16), hr), :]
                o_ref[q] = (mine + recv[q].astype(F32)).astype(BF16)
            for cp in copies:
                cp.wait_send()

    outs, carried = _launch(
        body, name=name, grid=(nt,),
        in_specs=[pl.BlockSpec((tm, m), lambda i: (i, lhs_part[0])), pl.BlockSpec((tm, d), lambda i: (i, rhs_part[0]))],
        out_specs=[pl.BlockSpec((N_CHIPS, hr, d), lambda i: (0, 0, 0))],
        out_shape=[jax.ShapeDtypeStruct((N_CHIPS, hr, d), BF16)],
        scratch_shapes=[pltpu.VMEM((m, d), F32), pltpu.VMEM((N_CHIPS, hr, d), BF16), pltpu.VMEM((N_CHIPS, hr, d), BF16),
                        pltpu.SemaphoreType.DMA((N_CHIPS,)), pltpu.SemaphoreType.DMA((N_CHIPS,))],
        args=(lhs, rhs), cargo=cargo)
    return outs[0], carried


def _window_sums(src, cols, w, tm, levels, trailing):
    def read_src(lo, hi):
        return src[lo:hi, cols]

    read, k, level = read_src, 1, 0
    while True:
        last = 2 * k == w
        if trailing:
            lo, hi = (HALO if last else 8 * (level + 1)), HALO + tm
            cur = read(lo, hi) + read(lo - k, hi - k)
        else:
            lo, hi = 0, (tm if last else tm + HALO - 8 * (level + 1))
            cur = read(lo, hi) + read(lo + k, hi + k)
        if last:
            return cur
        levels[level, lo:hi, :] = cur
        read = lambda a, b, level=level: levels[level, a:b, :]
        k, level = 2 * k, level + 1


def _pool_parts(u_cols, ubuf, cols, w, row, tm, levels):
    ws = _window_sums(ubuf, cols, w, tm, levels, trailing=True)
    inv = 1.0 / jnp.minimum(row + 1, w).astype(F32)
    return ws * inv - u_cols, inv


def _mixer_forward(x, g, win_t, wout_x, conv_w, pool_w, pool_scale, cargo=()):
    t, d = x.shape
    dc = win_t.shape[0] // 4
    gcw = dc // len(POOL_WINDOWS)
    wo_rows = d // N_CHIPS
    wo_stride = wout_x.shape[0] // N_CHIPS
    tm = min(TM_MIX, t)

    def body(x_ref, g_ref, win_hbm, wout_hbm, cw_ref, pw_ref, ps_ref, xo_ref, proj_ref, y_ref,
             win, wout, zbuf, ubuf, levels, sems):
        i = pl.program_id(0)

        @pl.when(i == 0)
        def _():
            pairs = [(win_hbm, win)]
            for k in range(N_CHIPS):
                pairs.append((wout_hbm.at[pl.ds(k * wo_stride, wo_rows), :], wout.at[pl.ds(k * wo_rows, wo_rows), :]))
            _load_rows(pairs, sems)
            zbuf[0:8, :] = jnp.zeros((8, dc), F32)
            ubuf[0:HALO, :] = jnp.zeros((HALO, dc), F32)

        xv = x_ref[...]
        r = lax.rsqrt(jnp.mean(xv * xv, axis=-1, keepdims=True) + EPS)
        h = (xv * r * g_ref[...]).astype(BF16)
        v = _nt(h, win[0:dc, :])
        gb = _nt(h, win[dc:2 * dc, :])
        gc = _nt(h, win[2 * dc:3 * dc, :])
        u = _nt(h, win[3 * dc:4 * dc, :])
        proj_ref[:, 0:dc] = v.astype(BF16)
        proj_ref[:, dc:2 * dc] = gb.astype(BF16)
        proj_ref[:, 2 * dc:3 * dc] = gc.astype(BF16)
        proj_ref[:, 3 * dc:4 * dc] = u.astype(BF16)

        z = gc * v
        zbuf[8:8 + tm, :] = z
        cw = cw_ref[...]
        conv = cw[2:3, :] * z + cw[1:2, :] * zbuf[7:7 + tm, :] + cw[0:1, :] * zbuf[6:6 + tm, :]
        y_ref[:, 0:dc] = (gb * conv).astype(BF16)

        ubuf[HALO:HALO + tm, :] = u
        row = i * tm + lax.broadcasted_iota(jnp.int32, (tm, 1), 0)
        for gi, w in enumerate(POOL_WINDOWS):
            cols = slice(gi * gcw, (gi + 1) * gcw)
            pooled, _ = _pool_parts(u[:, cols], ubuf, cols, w, row, tm, levels)
            yb = _nn(pooled.astype(BF16), pw_ref[gi].astype(BF16)) * ps_ref[:, cols]
            y_ref[:, dc + gi * gcw:dc + (gi + 1) * gcw] = yb.astype(BF16)

        xo_ref[...] = xv + _nn(y_ref[...], wout[...])
        zbuf[0:8, :] = zbuf[tm:tm + 8, :]
        ubuf[0:HALO, :] = ubuf[tm:tm + HALO, :]

    tok = lambda i: (i, 0)
    one = lambda i: (0, 0)
    return _launch(
        body, name="mixer_forward", grid=(t // tm,),
        in_specs=[pl.BlockSpec((tm, d), tok), pl.BlockSpec((1, d), one), HBM_SPEC, HBM_SPEC,
                  pl.BlockSpec(conv_w.shape, one), pl.BlockSpec(pool_w.shape, lambda i: (0, 0, 0)), pl.BlockSpec((1, dc), one)],
        out_specs=[pl.BlockSpec((tm, d), tok), pl.BlockSpec((tm, 4 * dc), tok), pl.BlockSpec((tm, 2 * dc), tok)],
        out_shape=[jax.ShapeDtypeStruct((t, d), F32), jax.ShapeDtypeStruct((t, 4 * dc), BF16), jax.ShapeDtypeStruct((t, 2 * dc), BF16)],
        scratch_shapes=[pltpu.VMEM((4 * dc, d), BF16), pltpu.VMEM((2 * dc, d), BF16),
                        pltpu.VMEM((tm + 8, dc), F32), pltpu.VMEM((tm + HALO, dc), F32),
                        pltpu.VMEM((WINDOW_LEVELS, tm + HALO, gcw), F32), pltpu.SemaphoreType.DMA((1 + N_CHIPS,))],
        args=(x, g, win_t, wout_x, conv_w, pool_w, pool_scale), cargo=cargo)


def _mixer_backward(dxo, x, g, proj, win_t, wout_x, conv_w, pool_w, pool_scale, cargo=()):
    t, d = x.shape
    dc = win_t.shape[0] // 4
    ng = len(POOL_WINDOWS)
    gcw = dc // ng
    wo_rows = d // N_CHIPS
    wo_stride = wout_x.shape[0] // N_CHIPS
    tm = min(TM_MIX, t)
    n_tiles = t // tm
    hb = tm // HALO

    def body(dxo_ref, x_ref, g_ref, proj_ref, halo_ref, win_hbm, wout_hbm, cw_ref, pw_ref, ps_ref,
             dx_ref, dproj_ref, h_ref, dxob_ref, dg_ref, dcw_ref, dps_ref, dpw_ref,
             win, wout, zbuf, ubuf, dcbuf, ebuf, levels, sems):
        i = pl.program_id(0)
        tile = n_tiles - 1 - i

        @pl.when(i == 0)
        def _():
            pairs = [(win_hbm, win)]
            for k in range(N_CHIPS):
                pairs.append((wout_hbm.at[pl.ds(k * wo_stride, wo_rows), :], wout.at[pl.ds(k * wo_rows, wo_rows), :]))
            _load_rows(pairs, sems)
            dcbuf[tm:tm + 8, :] = jnp.zeros((8, dc), F32)
            ebuf[tm:tm + HALO, :] = jnp.zeros((HALO, dc), F32)
            dg_ref[...] = jnp.zeros_like(dg_ref)
            dcw_ref[...] = jnp.zeros_like(dcw_ref)
            dps_ref[...] = jnp.zeros_like(dps_ref)
            dpw_ref[...] = jnp.zeros_like(dpw_ref)

        xv = x_ref[...]
        gv = g_ref[...]
        r = lax.rsqrt(jnp.mean(xv * xv, axis=-1, keepdims=True) + EPS)
        xhat = xv * r
        h_ref[...] = (xhat * gv).astype(BF16)
        dxo_v = dxo_ref[...]
        dxo_b = dxo_v.astype(BF16)
        dxob_ref[...] = dxo_b

        v = proj_ref[:, 0:dc].astype(F32)
        gb = proj_ref[:, dc:2 * dc].astype(F32)
        gc = proj_ref[:, 2 * dc:3 * dc].astype(F32)
        u = proj_ref[:, 3 * dc:4 * dc].astype(F32)
        first = jnp.where(tile > 0, 1.0, 0.0)
        zbuf[0:HALO, :] = halo_ref[:, 2 * dc:3 * dc].astype(F32) * halo_ref[:, 0:dc].astype(F32) * first
        ubuf[0:HALO, :] = halo_ref[:, 3 * dc:4 * dc].astype(F32) * first
        z = gc * v
        zbuf[HALO:HALO + tm, :] = z
        ubuf[HALO:HALO + tm, :] = u
        z1 = zbuf[HALO - 1:HALO - 1 + tm, :]
        z2 = zbuf[HALO - 2:HALO - 2 + tm, :]
        cw = cw_ref[...]
        conv = cw[2:3, :] * z + cw[1:2, :] * z1 + cw[0:1, :] * z2

        dy = _nt(dxo_b, wout[...])
        dya = dy[:, 0:dc]
        dgb = dya * conv
        dconv = dya * gb
        dcbuf[0:tm, :] = dconv
        dz = cw[2:3, :] * dconv + cw[1:2, :] * dcbuf[1:1 + tm, :] + cw[0:1, :] * dcbuf[2:2 + tm, :]
        dgc = dz * v
        dv = dz * gc
        dcw_ref[0:1, :] += jnp.sum(dconv * z2, axis=0, keepdims=True)
        dcw_ref[1:2, :] += jnp.sum(dconv * z1, axis=0, keepdims=True)
        dcw_ref[2:3, :] += jnp.sum(dconv * z, axis=0, keepdims=True)

        dproj_ref[:, 0:dc] = dv.astype(BF16)
        dproj_ref[:, dc:2 * dc] = dgb.astype(BF16)
        dproj_ref[:, 2 * dc:3 * dc] = dgc.astype(BF16)

        row = tile * tm + lax.broadcasted_iota(jnp.int32, (tm, 1), 0)
        for gi, w in enumerate(POOL_WINDOWS):
            cols = slice(gi * gcw, (gi + 1) * gcw)
            pooled, inv_cnt = _pool_parts(u[:, cols], ubuf, cols, w, row, tm, levels)
            pooled_b = pooled.astype(BF16)
            pw_b = pw_ref[gi].astype(BF16)
            dyb = dy[:, dc + gi * gcw:dc + (gi + 1) * gcw]
            q = _nn(pooled_b, pw_b)
            dps_ref[:, cols] += jnp.sum(q * dyb, axis=0, keepdims=True)
            dq = (dyb * ps_ref[:, cols]).astype(BF16)
            dpw_ref[gi] += _tn(pooled_b, dq)
            dpooled = _nt(dq, pw_b)
            ebuf[0:tm, cols] = dpooled * inv_cnt
            du = _window_sums(ebuf, cols, w, tm, levels, trailing=False) - dpooled
            dproj_ref[:, 3 * dc + gi * gcw:3 * dc + (gi + 1) * gcw] = du.astype(BF16)

        dh = _nn(dproj_ref[...], win[...])
        dg_ref[...] += jnp.sum(dh * xhat, axis=0, keepdims=True)
        dxh = dh * gv
        dx_ref[...] = dxo_v + r * (dxh - xhat * jnp.mean(dxh * xhat, axis=-1, keepdims=True))
        dcbuf[tm:tm + 8, :] = dcbuf[0:8, :]
        ebuf[tm:tm + HALO, :] = ebuf[0:HALO, :]

    tok = lambda i: (n_tiles - 1 - i, 0)
    halo = lambda i: (jnp.maximum((n_tiles - 1 - i) * hb - 1, 0), 0)
    one = lambda i: (0, 0)
    return _launch(
        body, name="mixer_backward", grid=(n_tiles,),
        in_specs=[pl.BlockSpec((tm, d), tok), pl.BlockSpec((tm, d), tok), pl.BlockSpec((1, d), one),
                  pl.BlockSpec((tm, 4 * dc), tok), pl.BlockSpec((HALO, 4 * dc), halo), HBM_SPEC, HBM_SPEC,
                  pl.BlockSpec(conv_w.shape, one), pl.BlockSpec(pool_w.shape, lambda i: (0, 0, 0)), pl.BlockSpec((1, dc), one)],
        out_specs=[pl.BlockSpec((tm, d), tok), pl.BlockSpec((tm, 4 * dc), tok), pl.BlockSpec((tm, d), tok), pl.BlockSpec((tm, d), tok),
                   pl.BlockSpec((1, d), one), pl.BlockSpec(conv_w.shape, one), pl.BlockSpec((1, dc), one),
                   pl.BlockSpec(pool_w.shape, lambda i: (0, 0, 0))],
        out_shape=[jax.ShapeDtypeStruct((t, d), F32), jax.ShapeDtypeStruct((t, 4 * dc), BF16), jax.ShapeDtypeStruct((t, d), BF16),
                   jax.ShapeDtypeStruct((t, d), BF16), jax.ShapeDtypeStruct((1, d), F32), jax.ShapeDtypeStruct(conv_w.shape, F32),
                   jax.ShapeDtypeStruct((1, dc), F32), jax.ShapeDtypeStruct(pool_w.shape, F32)],
        scratch_shapes=[pltpu.VMEM((4 * dc, d), BF16), pltpu.VMEM((2 * dc, d), BF16),
                        pltpu.VMEM((tm + HALO, dc), F32), pltpu.VMEM((tm + HALO, dc), F32),
                        pltpu.VMEM((tm + 8, dc), F32), pltpu.VMEM((tm + HALO, dc), F32),
                        pltpu.VMEM((WINDOW_LEVELS, tm + HALO, gcw), F32), pltpu.SemaphoreType.DMA((1 + N_CHIPS,))],
        args=(dxo, x, g, proj, proj, win_t, wout_x, conv_w, pool_w, pool_scale), cargo=cargo)


def _adam_update(w, gv, m, v):
    m_new = ADAM_B1 * m + (1.0 - ADAM_B1) * gv
    v_new = ADAM_B2 * v + (1.0 - ADAM_B2) * (gv * gv)
    m_hat = m_new / (1.0 - ADAM_B1 ** ADAM_STEP)
    v_hat = v_new / (1.0 - ADAM_B2 ** ADAM_STEP)
    return -ADAM_LR * (m_hat / (jnp.sqrt(v_hat) + ADAM_EPS) + ADAM_WD * w), m_new, v_new


def _adamw(w, grad, m, v, name):
    rows, cols = w.shape
    br = _row_block(rows, 512) if rows >= 8 else rows

    def body(w_ref, g_ref, m_ref, v_ref, go_ref, d_ref, mo_ref, vo_ref):
        gv = g_ref[...]
        go_ref[...] = gv
        d_ref[...], mo_ref[...], vo_ref[...] = _adam_update(w_ref[...], gv, m_ref[...], v_ref[...])

    blk = pl.BlockSpec((br, cols), lambda i: (i, 0))
    return pl.pallas_call(
        body, name=name,
        out_shape=[jax.ShapeDtypeStruct((rows, cols), F32)] * 4,
        grid=(rows // br,), in_specs=[blk] * 4, out_specs=[blk] * 4,
        compiler_params=pltpu.CompilerParams(dimension_semantics=("parallel",), vmem_limit_bytes=VMEM_LIMIT),
    )(w, grad, m, v)


def _adamw_transposed(w, grad_t, m, v, name):
    _, rows, cols = w.shape
    br = 256 if rows % 256 == 0 else rows

    def body(w_ref, gt_ref, m_ref, v_ref, g_ref, d_ref, mo_ref, vo_ref):
        gv = gt_ref[...].T
        g_ref[...] = gv
        d_ref[...], mo_ref[...], vo_ref[...] = _adam_update(w_ref[...], gv, m_ref[...], v_ref[...])

    blk = pl.BlockSpec((None, br, cols), lambda i: (0, i, 0))
    return pl.pallas_call(
        body, name=name,
        out_shape=[jax.ShapeDtypeStruct((1, rows, cols), F32)] * 4,
        grid=(rows // br,), in_specs=[blk, pl.BlockSpec((cols, br), lambda i: (0, i)), blk, blk], out_specs=[blk] * 4,
        compiler_params=pltpu.CompilerParams(dimension_semantics=("parallel",)),
    )(w, grad_t, m, v)


def _f32_rows_as_bf16(a, rows, cols):
    bits = lax.bitcast_convert_type(a, BF16).reshape(a.shape[0], 2 * a.shape[1])
    return jnp.pad(bits, ((0, rows - bits.shape[0]), (0, cols - bits.shape[1])))


def kernel(x, norm_ffn1, ffn1_w_gate, ffn1_w_up, ffn1_w_down, norm_mix, w_in, conv_w, pool_w, pool_scale, w_out, norm_ffn2, ffn2_w_gate, ffn2_w_up, ffn2_w_down, norm_final, loss_target, m_norm_ffn1, m_ffn1_w_gate, m_ffn1_w_up, m_ffn1_w_down, m_norm_mix, m_w_in, m_conv_w, m_pool_w, m_pool_scale, m_w_out, m_norm_ffn2, m_ffn2_w_gate, m_ffn2_w_up, m_ffn2_w_down, m_norm_final, v_norm_ffn1, v_ffn1_w_gate, v_ffn1_w_up, v_ffn1_w_down, v_norm_mix, v_w_in, v_conv_w, v_pool_w, v_pool_scale, v_w_out, v_norm_ffn2, v_ffn2_w_gate, v_ffn2_w_up, v_ffn2_w_down, v_norm_final):
    weights = dict(norm_ffn1=norm_ffn1, ffn1_w_gate=ffn1_w_gate, ffn1_w_up=ffn1_w_up, ffn1_w_down=ffn1_w_down, norm_mix=norm_mix,
                   w_in=w_in, conv_w=conv_w, pool_w=pool_w, pool_scale=pool_scale, w_out=w_out, norm_ffn2=norm_ffn2,
                   ffn2_w_gate=ffn2_w_gate, ffn2_w_up=ffn2_w_up, ffn2_w_down=ffn2_w_down, norm_final=norm_final)
    first_m = dict(norm_ffn1=m_norm_ffn1, ffn1_w_gate=m_ffn1_w_gate, ffn1_w_up=m_ffn1_w_up, ffn1_w_down=m_ffn1_w_down,
                   norm_mix=m_norm_mix, w_in=m_w_in, conv_w=m_conv_w, pool_w=m_pool_w, pool_scale=m_pool_scale, w_out=m_w_out,
                   norm_ffn2=m_norm_ffn2, ffn2_w_gate=m_ffn2_w_gate, ffn2_w_up=m_ffn2_w_up, ffn2_w_down=m_ffn2_w_down,
                   norm_final=m_norm_final)
    second_m = dict(norm_ffn1=v_norm_ffn1, ffn1_w_gate=v_ffn1_w_gate, ffn1_w_up=v_ffn1_w_up, ffn1_w_down=v_ffn1_w_down,
                    norm_mix=v_norm_mix, w_in=v_w_in, conv_w=v_conv_w, pool_w=v_pool_w, pool_scale=v_pool_scale, w_out=v_w_out,
                    norm_ffn2=v_norm_ffn2, ffn2_w_gate=v_ffn2_w_gate, ffn2_w_up=v_ffn2_w_up, ffn2_w_down=v_ffn2_w_down,
                    norm_final=v_norm_final)
    names = list(weights)

    xs = x[0]
    tgt = loss_target[0]
    t, d = xs.shape
    dc = pool_scale.shape[1]
    cx, cy, cc = _my_place()
    chip = 2 * cx + cy
    place = jnp.stack([chip, cc]).astype(jnp.int32)

    conv_rows = 32
    wout_x = jnp.concatenate([w_out[0].astype(BF16), _f32_rows_as_bf16(conv_w[0], conv_rows, d)], axis=0)

    g1, gm, g2 = norm_ffn1, norm_mix, norm_ffn2
    gf = norm_final.reshape(1, d)
    pw = pool_w[0]

    (wd1_shard, wg2_shard, wu2_shard, wd2_shard), [(wg1, wu1)] = _cast_to_bf16(
        [ffn1_w_down[0], ffn2_w_gate[0].T, ffn2_w_up[0].T, ffn2_w_down[0]], "gather_ffn1",
        [_gather_cargo([ffn1_w_gate[0].T.astype(BF16), ffn1_w_up[0].T.astype(BF16)])])
    (a1, b1, s1), [(wd1, win_t, wout_g)] = _ffn_up(
        xs, g1, wg1, wu1, "ffn1_up", [_gather_cargo([wd1_shard, w_in[0].T.astype(BF16), wout_x])])
    (x1,), [(wg2,)] = _ffn_down(xs, s1, wd1, "ffn1_down", [_gather_cargo([wg2_shard])])
    wo_rows = w_out.shape[1]
    cshard = conv_w.shape[2]
    conv_bits = wout_g.reshape(N_CHIPS, wo_rows + conv_rows, d)[:, wo_rows:wo_rows + conv_w.shape[1], :2 * cshard]
    conv_full = lax.bitcast_convert_type(conv_bits.reshape(N_CHIPS, conv_w.shape[1], cshard, 2), F32)
    conv_full = jnp.transpose(conv_full, (1, 0, 2)).reshape(conv_w.shape[1], N_CHIPS * cshard)
    (x2, proj, ymix), [(wu2,)] = _mixer_forward(x1, gm, win_t, wout_g, conv_full, pw, pool_scale, [_gather_cargo([wu2_shard])])
    (a2, b2, s2), [(wd2,)] = _ffn_up(x2, g2, wg2, wu2, "ffn2_up", [_gather_cargo([wd2_shard])])
    (dx3, sq_cols, dgf), _ = _ffn_down(x2, s2, wd2, "ffn2_down", loss_head=(gf, tgt))

    (dx2, dab2, hd2, dg2), _ = _ffn_backward(dx3, x2, g2, a2, b2, wg2, wu2, wd2, "ffn2_backward")
    p_wg2, _ = _weight_grad(dab2, hd2, "ffn2_gate_grad", lhs_part=(0, 2), rhs_part=(0, 2))
    p_wu2, [(x_wg2,)] = _weight_grad(dab2, hd2, "ffn2_up_grad", [_exchange_cargo([p_wg2])], lhs_part=(1, 2), rhs_part=(0, 2))
    p_wd2, [(x_wu2,)] = _weight_grad(s2, hd2, "ffn2_down_grad", [_exchange_cargo([p_wu2])], rhs_part=(1, 2))

    (dx1, dproj, h2, dx2b, dgm, dcw, dps, dpw), [(x_wd2,)] = _mixer_backward(
        dx2, x1, gm, proj, win_t, wout_g, conv_full, pw, pool_scale, [_exchange_cargo([p_wd2])])

    (dx0, dab1, hd1, dg1), _ = _ffn_backward(dx1, xs, g1, a1, b1, wg1, wu1, wd1, "ffn1_backward")

    npw = pw.size // d
    head = [dg1, dgm, dg2, dgf, jnp.pad(dps, ((0, 0), (0, d - dc))), jnp.pad(dcw, ((0, 0), (0, d - dc))), sq_cols]
    n_head = sum(h.shape[0] for h in head)
    base = -(-n_head // 8) * 8
    pack = jnp.concatenate(head + [jnp.zeros((base - n_head, d), F32), dpw.reshape(npw, d)], axis=0)

    p_wg1, [(packs,)] = _weight_grad(dab1, hd1, "ffn1_gate_grad", [_all_gather_small_cargo(pack)], lhs_part=(0, 2), rhs_part=(0, 2))
    p_wu1, [(x_wg1,)] = _weight_grad(dab1, hd1, "ffn1_up_grad", [_exchange_cargo([p_wg1])], lhs_part=(1, 2), rhs_part=(0, 2))
    p_wd1, [(x_wu1,)] = _weight_grad(s1, hd1, "ffn1_down_grad", [_exchange_cargo([p_wu1])], rhs_part=(1, 2))
    p_win, [(x_wd1,)] = _weight_grad(dproj, h2, "w_in_grad", [_exchange_cargo([p_wd1])])

    pairs = dict(wg1=p_wg1, wu1=p_wu1, wd1=p_wd1, wg2=p_wg2, wu2=p_wu2, wd2=p_wd2)
    landed = dict(wg1=x_wg1, wu1=x_wu1, wd1=x_wd1, wg2=x_wg2, wu2=x_wu2, wd2=x_wd2)
    ffn_halves = [_chip_sum(pairs[k], landed[k], place, k) for k in ["wg1", "wu1", "wd1", "wg2", "wu2", "wd2"]]
    p_wout, [(x_win,), ffn_both] = _weight_grad(ymix, dx2b, "w_out_grad", [_exchange_cargo([p_win]), _share_cargo(ffn_halves)])
    x_wout, = _run_cargo(_exchange_cargo([p_wout]), "grad_exchange_last")
    mix_both = _sibling_share([_chip_sum(p_win, x_win, place, "win"), _chip_sum(p_wout, x_wout, place, "wout")])
    rwg1, rwu1, rwd1, rwg2, rwu2, rwd2, rwin, rwout = [b.reshape(2 * b.shape[1], b.shape[2]) for b in list(ffn_both) + list(mix_both)]
    small = _sum_by_device(packs)
    loss = jnp.sum(small[n_head - 1]) * (0.5 / d)

    grads = {
        "norm_ffn1": small[0:1], "norm_mix": small[1:2], "norm_ffn2": small[2:3], "norm_final": small[3],
        "pool_scale": small[4:5, :dc],
        "conv_w": lax.dynamic_slice_in_dim(small[5:5 + dcw.shape[0], :dc], chip * cshard, cshard, axis=1)[None],
        "pool_w": small[base:].reshape(pool_w.shape),
        "ffn1_w_down": rwd1[None], "w_out": rwout[None], "ffn2_w_down": rwd2[None],
    }
    by_view = {"ffn1_w_gate": rwg1, "ffn1_w_up": rwu1, "ffn2_w_gate": rwg2, "ffn2_w_up": rwu2}

    deltas, new_m, new_v = {}, {}, {}
    for n in names:
        w = weights[n]
        shape = w.shape
        if n == "w_in":
            grads[n], deltas[n], new_m[n], new_v[n] = _adamw_transposed(w, rwin, first_m[n], second_m[n], "adamw_" + n)
            continue
        if n in by_view:
            view = lambda a: jnp.swapaxes(a, 1, 2)[0]
            back = lambda a: jnp.swapaxes(a[None], 1, 2)
            outs = _adamw(view(w), by_view[n], view(first_m[n]), view(second_m[n]), "adamw_" + n)
            grads[n], deltas[n], new_m[n], new_v[n] = [back(o) for o in outs]
            continue
        as2d = (lambda a: a.reshape(-1, shape[-1]))
        outs = _adamw(as2d(w), as2d(grads[n]), as2d(first_m[n]), as2d(second_m[n]), "adamw_" + n)
        grads[n], deltas[n], new_m[n], new_v[n] = [o.reshape(shape) for o in outs]

    return (loss, dx0[None], *[grads[n] for n in names], *[deltas[n] for n in names],
            *[new_m[n] for n in names], *[new_v[n] for n in names])
```

```python
import jax
import jax.numpy as jnp
from jax import lax
from jax.experimental import pallas as pl
from jax.experimental.pallas import tpu as pltpu

F32 = jnp.float32
BF16 = jnp.bfloat16
MESH = pl.DeviceIdType.MESH

EPS = 1e-6
POOL_WINDOWS = (2, 4, 8, 16)
ADAM_LR = 0.001
ADAM_B1 = 0.9
ADAM_B2 = 0.999
ADAM_EPS = 1e-08
ADAM_WD = 0.01
ADAM_STEP = 10

N_CHIPS = 4
N_DEVICES = 8
MXU_COLS_V7X = 256
VMEM_LIMIT = 56 * 1024 * 1024
TM_FFN = 512
TM_MIX = 512
TM_TN = 256
HALO = 32
WINDOW_LEVELS = 3
FFN_FWD_CHUNKS = 2
FFN_BWD_CHUNKS = 2


def _nt(a, b):
    return lax.dot_general(a, b, (((1,), (1,)), ((), ())), preferred_element_type=F32)


def _tn(a, b):
    return lax.dot_general(a, b, (((0,), (0,)), ((), ())), preferred_element_type=F32)


def _nn(a, b):
    return jnp.dot(a, b, preferred_element_type=F32)


def _sigmoid(a):
    return 1.0 / (1.0 + jnp.exp(-a))


def _feature_chunks(n, parts):
    assert n % MXU_COLS_V7X == 0
    tiles = n // MXU_COLS_V7X
    out, s0 = [], 0
    for p in range(parts):
        sz = (tiles // parts + (1 if p < tiles % parts else 0)) * MXU_COLS_V7X
        if sz:
            out.append((s0, sz))
            s0 += sz
    return out


def _row_block(rows, cap):
    best = 8
    for b in range(8, min(rows, cap) + 1, 8):
        if rows % b == 0:
            best = b
    assert rows % best == 0
    return best


def _my_place():
    return lax.axis_index("x"), lax.axis_index("y"), lax.axis_index("c")


def _other_chips(x, y):
    return [(1 - x, y), (x, 1 - y), (1 - x, 1 - y)]


HBM_SPEC = pl.BlockSpec(memory_space=pltpu.HBM)


class _Cargo:
    def __init__(self, operands, out_shapes, n_sems, phases, when, in_place=False):
        self.operands, self.out_shapes, self.n_sems = list(operands), list(out_shapes), n_sems
        self.phases, self.when = list(phases), list(when)
        self.in_place = in_place
        assert len(self.phases) == len(self.when) and self.when[0] == 0.0 and self.when[-1] == 1.0


def _launch(body, *, name, grid, in_specs, out_specs, out_shape, scratch_shapes, args, cargo=()):
    params = pltpu.CompilerParams(dimension_semantics=("arbitrary",) * len(grid), vmem_limit_bytes=VMEM_LIMIT)
    cargos = list(cargo)
    c_operands = [op for cg in cargos for op in cg.operands]
    c_shapes = [sh for cg in cargos for sh in cg.out_shapes]
    counts = [len(in_specs), len(c_operands), len(out_shape), len(c_shapes), len(scratch_shapes), 2 * len(cargos)]

    def carrying(*refs):
        groups, pos = [], 0
        for k in counts:
            groups.append(refs[pos:pos + k])
            pos += k
        ins, c_ins, outs, c_outs, scratch, sems = groups
        parts, pi, po = [], 0, 0
        for n, cg in enumerate(cargos):
            parts.append((c_ins[pi:pi + len(cg.operands)], c_outs[po:po + len(cg.out_shapes)], sems[2 * n], sems[2 * n + 1]))
            pi += len(cg.operands)
            po += len(cg.out_shapes)
        step, steps = 0, 1
        for ax, g in enumerate(grid):
            step = step * g + pl.program_id(ax)
            steps *= g
        todo = {}
        for cg, part in zip(cargos, parts):
            for phase, frac in zip(cg.phases[:-1], cg.when[:-1]):
                todo.setdefault(int(round(frac * (steps - 1))), []).append((phase, part))

        for at in sorted(todo):
            @pl.when(step == at)
            def _(at=at):
                for phase, part in todo[at]:
                    phase(*part)

        body(*ins, *outs, *scratch)

        if cargos:
            @pl.when(step == steps - 1)
            def _():
                for cg, part in zip(cargos, parts):
                    cg.phases[-1](*part)

    sems = [pltpu.SemaphoreType.DMA((cg.n_sems,)) for cg in cargos for _ in range(2)]
    aliases, pi, po = {}, counts[0], counts[2]
    for cg in cargos:
        if cg.in_place:
            aliases.update({pi + k: po + k for k in range(len(cg.operands))})
        pi += len(cg.operands)
        po += len(cg.out_shapes)
    outs = pl.pallas_call(
        carrying, name=name, grid=grid,
        in_specs=list(in_specs) + [HBM_SPEC] * counts[1], out_specs=list(out_specs) + [HBM_SPEC] * counts[3],
        out_shape=list(out_shape) + c_shapes, scratch_shapes=list(scratch_shapes) + sems,
        input_output_aliases=aliases, compiler_params=params)(*args, *c_operands)
    own, rest = list(outs[:counts[2]]), list(outs[counts[2]:])
    carried, po = [], 0
    for cg in cargos:
        carried.append(rest[po:po + len(cg.out_shapes)])
        po += len(cg.out_shapes)
    return own, carried


def _run_cargo(cargo, name):
    n_in, n_out = len(cargo.operands), len(cargo.out_shapes)

    def body(*refs):
        c_ins, c_outs, sems = refs[:n_in], refs[n_in:n_in + n_out], refs[n_in + n_out:]
        for phase in cargo.phases:
            phase(c_ins, c_outs, *sems)

    sem = pltpu.SemaphoreType.DMA((cargo.n_sems,))
    return list(pl.pallas_call(body, name=name, out_shape=cargo.out_shapes, in_specs=[HBM_SPEC] * n_in,
                               out_specs=[HBM_SPEC] * n_out, scratch_shapes=[sem, sem])(*cargo.operands))


def _gather_cargo(shards):
    n = len(shards)
    for s in shards:
        assert s.shape[0] % 32 == 0
    slots = 8

    def steps(ins, outs, send_sems, recv_sems):
        x, y, c = _my_place()
        sibling = (x, y, 1 - c)
        over_x, over_y = (1 - x, y, c), (x, 1 - y, c)
        mine, chip_x, chip_y, chip_d = 2 * x + y, 2 * (1 - x) + y, 2 * x + (1 - y), 2 * (1 - x) + (1 - y)

        def rows_of(a, chip_index, half, part=None):
            rps = shards[a].shape[0]
            hr = rps // 2
            first = -(-hr // 32) * 16
            offset, size = {None: (0, hr), 0: (0, first), 1: (first, hr - first)}[part]
            return outs[a].at[pl.ds(pl.multiple_of(chip_index * rps + half * hr + offset, 16), size), :]

        def remote(a, slot, src, dst, to):
            return pltpu.make_async_remote_copy(
                src_ref=src, dst_ref=dst, send_sem=send_sems.at[a * slots + slot], recv_sem=recv_sems.at[a * slots + slot],
                device_id=to, device_id_type=MESH)

        def same_rows(a, slot, rows, to):
            return remote(a, slot, rows, rows, to)

        def own_copy(a):
            rps = shards[a].shape[0]
            return remote(a, 7, ins[a], outs[a].at[pl.ds(pl.multiple_of(mine * rps, 16), rps), :], sibling)

        def my_half(a):
            hr = shards[a].shape[0] // 2
            return ins[a].at[pl.ds(pl.multiple_of(c * hr, 16), hr), :]

        def start():
            for a in range(n):
                own_copy(a).start()
                remote(a, 0, my_half(a), rows_of(a, mine, c), over_x).start()
                remote(a, 1, my_half(a), rows_of(a, mine, c), over_y).start()

        def relay_neighbours():
            for a in range(n):
                same_rows(a, 0, rows_of(a, chip_x, c), over_x).wait_recv()
                same_rows(a, 4, rows_of(a, chip_x, c), sibling).start()
                same_rows(a, 2, rows_of(a, chip_x, c, 0), over_y).start()
                same_rows(a, 1, rows_of(a, chip_y, c), over_y).wait_recv()
                same_rows(a, 5, rows_of(a, chip_y, c), sibling).start()
                same_rows(a, 3, rows_of(a, chip_y, c, 1), over_x).start()

        def relay_diagonal():
            for a in range(n):
                same_rows(a, 2, rows_of(a, chip_d, c, 0), over_y).wait_recv()
                same_rows(a, 3, rows_of(a, chip_d, c, 1), over_x).wait_recv()
                same_rows(a, 6, rows_of(a, chip_d, c), sibling).start()

        def finish():
            for a in range(n):
                for slot, chip_index in ((4, chip_x), (5, chip_y), (6, chip_d)):
                    same_rows(a, slot, rows_of(a, chip_index, 1 - c), sibling).wait_recv()
            for a in range(n):
                remote(a, 0, my_half(a), rows_of(a, mine, c), over_x).wait_send()
                remote(a, 1, my_half(a), rows_of(a, mine, c), over_y).wait_send()
                same_rows(a, 2, rows_of(a, chip_x, c, 0), over_y).wait_send()
                same_rows(a, 3, rows_of(a, chip_y, c, 1), over_x).wait_send()
                for slot, chip_index in ((4, chip_x), (5, chip_y), (6, chip_d)):
                    same_rows(a, slot, rows_of(a, chip_index, c), sibling).wait_send()
                own_copy(a).wait()

        return start, relay_neighbours, relay_diagonal, finish

    phases = [lambda *r, k=k: steps(*r)[k]() for k in range(4)]
    return _Cargo(shards, [jax.ShapeDtypeStruct((N_CHIPS * s.shape[0], s.shape[1]), s.dtype) for s in shards], slots * n,
                  phases, [0.0, 0.6, 0.85, 1.0])


def _exchange_cargo(pairs):
    n = len(pairs)

    def copies(ins, outs, send_sems, recv_sems):
        x, y, c = _my_place()
        return [pltpu.make_async_remote_copy(
            src_ref=ins[a].at[2 * chip[0] + chip[1]], dst_ref=outs[a].at[j],
            send_sem=send_sems.at[3 * a + j], recv_sem=recv_sems.at[3 * a + j], device_id=(*chip, c), device_id_type=MESH)
            for a in range(n) for j, chip in enumerate(_other_chips(x, y))]

    def start(*r):
        for cp in copies(*r):
            cp.start()

    def finish(*r):
        for cp in copies(*r):
            cp.wait()

    return _Cargo(pairs, [jax.ShapeDtypeStruct((3,) + p.shape[1:], p.dtype) for p in pairs], 3 * n, [start, finish], [0.0, 1.0])


def _all_gather_small_cargo(pack):
    rows, cols = pack.shape

    def copies(ins, outs, send_sems, recv_sems):
        x, y, c = _my_place()
        me = 4 * x + 2 * y + c
        remote = []
        for f in range(1, N_DEVICES):
            fx, fy, fc = (f >> 2) & 1, (f >> 1) & 1, f & 1
            to = (1 - x if fx else x, 1 - y if fy else y, 1 - c if fc else c)
            remote.append(pltpu.make_async_remote_copy(
                src_ref=ins[0], dst_ref=outs[0].at[me], send_sem=send_sems.at[f - 1], recv_sem=recv_sems.at[f - 1],
                device_id=to, device_id_type=MESH))
        own = pltpu.make_async_copy(ins[0], outs[0].at[me], send_sems.at[N_DEVICES - 1])
        return remote, own

    def start(*r):
        remote, own = copies(*r)
        own.start()
        for cp in remote:
            cp.start()

    def finish(*r):
        remote, own = copies(*r)
        for cp in remote:
            cp.wait()
        own.wait()

    return _Cargo([pack], [jax.ShapeDtypeStruct((N_DEVICES, rows, cols), F32)], N_DEVICES, [start, finish], [0.0, 1.0])


def _sum_by_device(packs):
    n, rows, cols = packs.shape

    def body(p_ref, o_ref):
        acc = p_ref[0]
        for dev in range(1, n):
            acc = acc + p_ref[dev]
        o_ref[...] = acc

    return pl.pallas_call(body, name="small_grads_sum", out_shape=jax.ShapeDtypeStruct((rows, cols), F32))(packs)


def _chip_sum(pair, got, place, tag):
    _, hr, cols = pair.shape
    br = _row_block(hr, 256)

    def body(k_ref, p_ref, r_ref, o_ref):
        acc = p_ref[...].astype(F32)
        for j in range(3):
            acc = acc + r_ref[j].astype(F32)
        o_ref[...] = acc

    return pl.pallas_call(
        body, name="grad_chip_sum_" + tag,
        out_shape=jax.ShapeDtypeStruct((2, hr, cols), F32),
        grid_spec=pltpu.PrefetchScalarGridSpec(
            num_scalar_prefetch=1, grid=(hr // br,),
            in_specs=[pl.BlockSpec((None, br, cols), lambda r, k_ref: (k_ref[0], r, 0)),
                      pl.BlockSpec((3, br, cols), lambda r, k_ref: (0, r, 0))],
            out_specs=pl.BlockSpec((None, br, cols), lambda r, k_ref: (k_ref[1], r, 0))),
        compiler_params=pltpu.CompilerParams(dimension_semantics=("parallel",)),
    )(place, pair, got)


def _share_cargo(halves):
    n = len(halves)

    def copies(ins, outs, send_sems, recv_sems):
        x, y, c = _my_place()
        return [pltpu.make_async_remote_copy(
            src_ref=outs[a].at[c], dst_ref=outs[a].at[c], send_sem=send_sems.at[a], recv_sem=recv_sems.at[a],
            device_id=(x, y, 1 - c), device_id_type=MESH) for a in range(n)]

    def start(*r):
        for cp in copies(*r):
            cp.start()

    def finish(*r):
        for cp in copies(*r):
            cp.wait()

    return _Cargo(halves, [jax.ShapeDtypeStruct(h.shape, h.dtype) for h in halves], n, [start, finish], [0.0, 1.0], in_place=True)


def _sibling_share(halves):
    n = len(halves)

    def body(*refs):
        outs = refs[n:2 * n]
        send_sems, recv_sems = refs[2 * n:]
        x, y, c = _my_place()
        copies = []
        for a in range(n):
            cp = pltpu.make_async_remote_copy(
                src_ref=outs[a].at[c], dst_ref=outs[a].at[c], send_sem=send_sems.at[a], recv_sem=recv_sems.at[a],
                device_id=(x, y, 1 - c), device_id_type=MESH)
            cp.start()
            copies.append(cp)
        for cp in copies:
            cp.wait()

    return pl.pallas_call(
        body, name="grad_share_sibling",
        out_shape=[jax.ShapeDtypeStruct(h.shape, h.dtype) for h in halves],
        in_specs=[HBM_SPEC] * n, out_specs=[HBM_SPEC] * n,
        input_output_aliases={a: a for a in range(n)},
        scratch_shapes=[pltpu.SemaphoreType.DMA((n,)), pltpu.SemaphoreType.DMA((n,))],
    )(*halves)


def _load_rows(pairs, sems):
    cps = [pltpu.make_async_copy(src, dst, sems.at[j]) for j, (src, dst) in enumerate(pairs)]
    for cp in cps:
        cp.start()
    for cp in cps:
        cp.wait()


def _piece_rows(weights):
    return list(weights), (lambda refs, mats: list(zip(refs, mats))), len(weights)


def _cast_to_bf16(arrays, name, cargo=()):
    rows, cols = arrays[0].shape
    n = len(arrays)
    br = _row_block(rows, 256)

    def body(*refs):
        for src, dst in zip(refs[:n], refs[n:]):
            dst[...] = src[...].astype(BF16)

    blk = pl.BlockSpec((br, cols), lambda i: (i, 0))
    return _launch(body, name=name, grid=(rows // br,), in_specs=[blk] * n, out_specs=[blk] * n,
                   out_shape=[jax.ShapeDtypeStruct((rows, cols), BF16)] * n, scratch_shapes=[], args=tuple(arrays), cargo=cargo)


def _loss_head(xv, gv, tv):
    d = xv.shape[-1]
    r = lax.rsqrt(jnp.mean(xv * xv, axis=-1, keepdims=True) + EPS)
    xhat = xv * r
    err = xhat * gv - tv
    dy = err * (1.0 / d)
    dxh = dy * gv
    dx = r * (dxh - xhat * jnp.mean(dxh * xhat, axis=-1, keepdims=True))
    return dx, jnp.sum(err * err, axis=0, keepdims=True), jnp.sum(dy * xhat, axis=0, keepdims=True)


def _ffn_up(x, g, wg_t, wu_t, name, cargo=()):
    t, d = x.shape
    f = wg_t.shape[0]
    tm = min(TM_FFN, t)
    chunks = _feature_chunks(f, FFN_FWD_CHUNKS)
    flat, copies, n_copies = _piece_rows([wg_t, wu_t])
    nw = len(flat)

    def body(x_ref, g_ref, *rest):
        w_hbm, (a_ref, b_ref, s_ref, wg, wu, sems) = rest[:nw], rest[nw:]

        @pl.when(pl.program_id(0) == 0)
        def _():
            _load_rows(copies(w_hbm, [wg, wu]), sems)

        xv = x_ref[...]
        r = lax.rsqrt(jnp.mean(xv * xv, axis=-1, keepdims=True) + EPS)
        h = (xv * r * g_ref[...]).astype(BF16)
        for s0, sz in chunks:
            a = _nt(h, wg[s0:s0 + sz, :])
            b = _nt(h, wu[s0:s0 + sz, :])
            a_ref[:, s0:s0 + sz] = a.astype(BF16)
            b_ref[:, s0:s0 + sz] = b.astype(BF16)
            s_ref[:, s0:s0 + sz] = (a * _sigmoid(a) * b).astype(BF16)

    tok = lambda i: (i, 0)
    wide = pl.BlockSpec((tm, f), tok)
    return _launch(
        body, name=name, grid=(t // tm,),
        in_specs=[pl.BlockSpec((tm, d), tok), pl.BlockSpec((1, d), lambda i: (0, 0))] + [HBM_SPEC] * nw,
        out_specs=[wide, wide, wide], out_shape=[jax.ShapeDtypeStruct((t, f), BF16)] * 3,
        scratch_shapes=[pltpu.VMEM((f, d), BF16), pltpu.VMEM((f, d), BF16), pltpu.SemaphoreType.DMA((n_copies,))],
        args=(x, g, *flat), cargo=cargo)


def _ffn_down(x, s, wd, name, cargo=(), loss_head=None):
    t, d = x.shape
    f = s.shape[1]
    tm = min(TM_FFN, t)
    flat, copies, n_copies = _piece_rows([wd])
    nw = len(flat)
    nl = 2 if loss_head else 0

    def body(x_ref, s_ref, *rest):
        head, w_hbm = rest[:nl], rest[nl:nl + nw]
        xo_ref = rest[nl + nw]
        sums, (wdn, sems) = rest[nl + nw + 1:nl + nw + 1 + nl], rest[nl + nw + 1 + nl:]

        @pl.when(pl.program_id(0) == 0)
        def _():
            _load_rows(copies(w_hbm, [wdn]), sems)
            for sum_ref in sums:
                sum_ref[...] = jnp.zeros_like(sum_ref)

        xo = x_ref[...] + 0.5 * _nn(s_ref[...], wdn[...])
        if loss_head:
            dx, sq, dgf = _loss_head(xo, head[0][...], head[1][...])
            xo_ref[...] = dx
            sums[0][...] += sq
            sums[1][...] += dgf
        else:
            xo_ref[...] = xo

    tok = lambda i: (i, 0)
    one = lambda i: (0, 0)
    return _launch(
        body, name=name, grid=(t // tm,),
        in_specs=[pl.BlockSpec((tm, d), tok), pl.BlockSpec((tm, f), tok)]
        + ([pl.BlockSpec((1, d), one), pl.BlockSpec((tm, d), tok)] if loss_head else []) + [HBM_SPEC] * nw,
        out_specs=[pl.BlockSpec((tm, d), tok)] + [pl.BlockSpec((1, d), one)] * nl,
        out_shape=[jax.ShapeDtypeStruct((t, d), F32)] + [jax.ShapeDtypeStruct((1, d), F32)] * nl,
        scratch_shapes=[pltpu.VMEM((f, d), BF16), pltpu.SemaphoreType.DMA((n_copies,))],
        args=(x, s, *(loss_head or ()), *flat), cargo=cargo)


def _ffn_backward(dxo, x, g, a, b, wg_t, wu_t, wd, name, cargo=()):
    t, d = x.shape
    f = wd.shape[0]
    tm = min(TM_FFN // 2, t)
    chunks = _feature_chunks(f, FFN_BWD_CHUNKS)
    flat, copies, n_copies = _piece_rows([wg_t, wu_t, wd])
    nw = len(flat)

    def body(dxo_ref, x_ref, g_ref, a_ref, b_ref, *rest):
        w_hbm, (dx_ref, dab_ref, hd_ref, dg_ref, wg, wu, wdn, sems) = rest[:nw], rest[nw:]

        @pl.when(pl.program_id(0) == 0)
        def _():
            _load_rows(copies(w_hbm, [wg, wu, wdn]), sems)
            dg_ref[...] = jnp.zeros_like(dg_ref)

        xv = x_ref[...]
        gv = g_ref[...]
        r = lax.rsqrt(jnp.mean(xv * xv, axis=-1, keepdims=True) + EPS)
        xhat = xv * r
        hd_ref[:, 0:d] = (xhat * gv).astype(BF16)
        dxo_v = dxo_ref[...]
        dout = (0.5 * dxo_v).astype(BF16)
        hd_ref[:, d:2 * d] = dout
        dh = jnp.zeros((tm, d), F32)
        for s0, sz in chunks:
            ds = _nt(dout, wdn[s0:s0 + sz, :])
            av = a_ref[:, s0:s0 + sz].astype(F32)
            bv = b_ref[:, s0:s0 + sz].astype(F32)
            sig = _sigmoid(av)
            silu = av * sig
            da = (ds * bv * (sig * (1.0 + av * (1.0 - sig)))).astype(BF16)
            db = (ds * silu).astype(BF16)
            dab_ref[:, s0:s0 + sz] = da
            dab_ref[:, f + s0:f + s0 + sz] = db
            dh = dh + _nn(da, wg[s0:s0 + sz, :]) + _nn(db, wu[s0:s0 + sz, :])
        dg_ref[...] += jnp.sum(dh * xhat, axis=0, keepdims=True)
        dxh = dh * gv
        dx_ref[...] = dxo_v + r * (dxh - xhat * jnp.mean(dxh * xhat, axis=-1, keepdims=True))

    tok = lambda i: (i, 0)
    one = lambda i: (0, 0)
    return _launch(
        body, name=name, grid=(t // tm,),
        in_specs=[pl.BlockSpec((tm, d), tok), pl.BlockSpec((tm, d), tok), pl.BlockSpec((1, d), one),
                  pl.BlockSpec((tm, f), tok), pl.BlockSpec((tm, f), tok)] + [HBM_SPEC] * nw,
        out_specs=[pl.BlockSpec((tm, d), tok), pl.BlockSpec((tm, 2 * f), tok), pl.BlockSpec((tm, 2 * d), tok),
                   pl.BlockSpec((1, d), one)],
        out_shape=[jax.ShapeDtypeStruct((t, d), F32), jax.ShapeDtypeStruct((t, 2 * f), BF16),
                   jax.ShapeDtypeStruct((t, 2 * d), BF16), jax.ShapeDtypeStruct((1, d), F32)],
        scratch_shapes=[pltpu.VMEM((f, d), BF16), pltpu.VMEM((f, d), BF16), pltpu.VMEM((f, d), BF16), pltpu.SemaphoreType.DMA((n_copies,))],
        args=(dxo, x, g, a, b, *flat), cargo=cargo)


def _weight_grad(lhs, rhs, name, cargo=(), lhs_part=(0, 1), rhs_part=(0, 1)):
    t = lhs.shape[0]
    m = lhs.shape[1] // lhs_part[1]
    d = rhs.shape[1] // rhs_part[1]
    tm = min(TM_TN, t)
    nt = t // tm
    rps = m // N_CHIPS
    hr = rps // 2
    assert hr % 16 == 0

    def body(l_ref, r_ref, o_ref, acc, stage, recv, send_sems, recv_sems):
        i = pl.program_id(0)

        @pl.when(i == 0)
        def _():
            acc[...] = jnp.zeros_like(acc)

        acc[...] += _tn(l_ref[...], r_ref[...])

        @pl.when(i == nt - 1)
        def _():
            x, y, c = _my_place()
            copies = []
            for q in range(N_CHIPS):
                stage[q] = acc[pl.ds(pl.multiple_of(q * rps + (1 - c) * hr, 16), hr), :].astype(BF16)
                cp = pltpu.make_async_remote_copy(
                    src_ref=stage.at[q], dst_ref=recv.at[q], send_sem=send_sems.at[q], recv_sem=recv_sems.at[q],
                    device_id=(x, y, 1 - c), device_id_type=MESH)
                cp.start()
                copies.append(cp)
            for q, cp in enumerate(copies):
                cp.wait_recv()
                mine = acc[pl.ds(pl.multiple_of(q * rps + c * hr, 16), hr), :]
                o_ref[q] = (mine + recv[q].astype(F32)).astype(BF16)
            for cp in copies:
                cp.wait_send()

    outs, carried = _launch(
        body, name=name, grid=(nt,),
        in_specs=[pl.BlockSpec((tm, m), lambda i: (i, lhs_part[0])), pl.BlockSpec((tm, d), lambda i: (i, rhs_part[0]))],
        out_specs=[pl.BlockSpec((N_CHIPS, hr, d), lambda i: (0, 0, 0))],
        out_shape=[jax.ShapeDtypeStruct((N_CHIPS, hr, d), BF16)],
        scratch_shapes=[pltpu.VMEM((m, d), F32), pltpu.VMEM((N_CHIPS, hr, d), BF16), pltpu.VMEM((N_CHIPS, hr, d), BF16),
                        pltpu.SemaphoreType.DMA((N_CHIPS,)), pltpu.SemaphoreType.DMA((N_CHIPS,))],
        args=(lhs, rhs), cargo=cargo)
    return outs[0], carried


def _window_sums(src, cols, w, tm, levels, trailing):
    def read_src(lo, hi):
        return src[lo:hi, cols]

    read, k, level = read_src, 1, 0
    while True:
        last = 2 * k == w
        if trailing:
            lo, hi = (HALO if last else 8 * (level + 1)), HALO + tm
            cur = read(lo, hi) + read(lo - k, hi - k)
        else:
            lo, hi = 0, (tm if last else tm + HALO - 8 * (level + 1))
            cur = read(lo, hi) + read(lo + k, hi + k)
        if last:
            return cur
        levels[level, lo:hi, :] = cur
        read = lambda a, b, level=level: levels[level, a:b, :]
        k, level = 2 * k, level + 1


def _pool_parts(u_cols, ubuf, cols, w, row, tm, levels):
    ws = _window_sums(ubuf, cols, w, tm, levels, trailing=True)
    inv = 1.0 / jnp.minimum(row + 1, w).astype(F32)
    return ws * inv - u_cols, inv


def _mixer_forward(x, g, win_t, wout_x, conv_w, pool_w, pool_scale, cargo=()):
    t, d = x.shape
    dc = win_t.shape[0] // 4
    gcw = dc // len(POOL_WINDOWS)
    wo_rows = d // N_CHIPS
    wo_stride = wout_x.shape[0] // N_CHIPS
    tm = min(TM_MIX, t)

    def body(x_ref, g_ref, win_hbm, wout_hbm, cw_ref, pw_ref, ps_ref, xo_ref, proj_ref, y_ref,
             win, wout, zbuf, ubuf, levels, sems):
        i = pl.program_id(0)

        @pl.when(i == 0)
        def _():
            pairs = [(win_hbm, win)]
            for k in range(N_CHIPS):
                pairs.append((wout_hbm.at[pl.ds(k * wo_stride, wo_rows), :], wout.at[pl.ds(k * wo_rows, wo_rows), :]))
            _load_rows(pairs, sems)
            zbuf[0:8, :] = jnp.zeros((8, dc), F32)
            ubuf[0:HALO, :] = jnp.zeros((HALO, dc), F32)

        xv = x_ref[...]
        r = lax.rsqrt(jnp.mean(xv * xv, axis=-1, keepdims=True) + EPS)
        h = (xv * r * g_ref[...]).astype(BF16)
        v = _nt(h, win[0:dc, :])
        gb = _nt(h, win[dc:2 * dc, :])
        gc = _nt(h, win[2 * dc:3 * dc, :])
        u = _nt(h, win[3 * dc:4 * dc, :])
        proj_ref[:, 0:dc] = v.astype(BF16)
        proj_ref[:, dc:2 * dc] = gb.astype(BF16)
        proj_ref[:, 2 * dc:3 * dc] = gc.astype(BF16)
        proj_ref[:, 3 * dc:4 * dc] = u.astype(BF16)

        z = gc * v
        zbuf[8:8 + tm, :] = z
        cw = cw_ref[...]
        conv = cw[2:3, :] * z + cw[1:2, :] * zbuf[7:7 + tm, :] + cw[0:1, :] * zbuf[6:6 + tm, :]
        y_ref[:, 0:dc] = (gb * conv).astype(BF16)

        ubuf[HALO:HALO + tm, :] = u
        row = i * tm + lax.broadcasted_iota(jnp.int32, (tm, 1), 0)
        for gi, w in enumerate(POOL_WINDOWS):
            cols = slice(gi * gcw, (gi + 1) * gcw)
            pooled, _ = _pool_parts(u[:, cols], ubuf, cols, w, row, tm, levels)
            yb = _nn(pooled.astype(BF16), pw_ref[gi].astype(BF16)) * ps_ref[:, cols]
            y_ref[:, dc + gi * gcw:dc + (gi + 1) * gcw] = yb.astype(BF16)

        xo_ref[...] = xv + _nn(y_ref[...], wout[...])
        zbuf[0:8, :] = zbuf[tm:tm + 8, :]
        ubuf[0:HALO, :] = ubuf[tm:tm + HALO, :]

    tok = lambda i: (i, 0)
    one = lambda i: (0, 0)
    return _launch(
        body, name="mixer_forward", grid=(t // tm,),
        in_specs=[pl.BlockSpec((tm, d), tok), pl.BlockSpec((1, d), one), HBM_SPEC, HBM_SPEC,
                  pl.BlockSpec(conv_w.shape, one), pl.BlockSpec(pool_w.shape, lambda i: (0, 0, 0)), pl.BlockSpec((1, dc), one)],
        out_specs=[pl.BlockSpec((tm, d), tok), pl.BlockSpec((tm, 4 * dc), tok), pl.BlockSpec((tm, 2 * dc), tok)],
        out_shape=[jax.ShapeDtypeStruct((t, d), F32), jax.ShapeDtypeStruct((t, 4 * dc), BF16), jax.ShapeDtypeStruct((t, 2 * dc), BF16)],
        scratch_shapes=[pltpu.VMEM((4 * dc, d), BF16), pltpu.VMEM((2 * dc, d), BF16),
                        pltpu.VMEM((tm + 8, dc), F32), pltpu.VMEM((tm + HALO, dc), F32),
                        pltpu.VMEM((WINDOW_LEVELS, tm + HALO, gcw), F32), pltpu.SemaphoreType.DMA((1 + N_CHIPS,))],
        args=(x, g, win_t, wout_x, conv_w, pool_w, pool_scale), cargo=cargo)


def _mixer_backward(dxo, x, g, proj, win_t, wout_x, conv_w, pool_w, pool_scale, cargo=()):
    t, d = x.shape
    dc = win_t.shape[0] // 4
    ng = len(POOL_WINDOWS)
    gcw = dc // ng
    wo_rows = d // N_CHIPS
    wo_stride = wout_x.shape[0] // N_CHIPS
    tm = min(TM_MIX, t)
    n_tiles = t // tm
    hb = tm // HALO

    def body(dxo_ref, x_ref, g_ref, proj_ref, halo_ref, win_hbm, wout_hbm, cw_ref, pw_ref, ps_ref,
             dx_ref, dproj_ref, h_ref, dxob_ref, dg_ref, dcw_ref, dps_ref, dpw_ref,
             win, wout, zbuf, ubuf, dcbuf, ebuf, levels, sems):
        i = pl.program_id(0)
        tile = n_tiles - 1 - i

        @pl.when(i == 0)
        def _():
            pairs = [(win_hbm, win)]
            for k in range(N_CHIPS):
                pairs.append((wout_hbm.at[pl.ds(k * wo_stride, wo_rows), :], wout.at[pl.ds(k * wo_rows, wo_rows), :]))
            _load_rows(pairs, sems)
            dcbuf[tm:tm + 8, :] = jnp.zeros((8, dc), F32)
            ebuf[tm:tm + HALO, :] = jnp.zeros((HALO, dc), F32)
            dg_ref[...] = jnp.zeros_like(dg_ref)
            dcw_ref[...] = jnp.zeros_like(dcw_ref)
            dps_ref[...] = jnp.zeros_like(dps_ref)
            dpw_ref[...] = jnp.zeros_like(dpw_ref)

        xv = x_ref[...]
        gv = g_ref[...]
        r = lax.rsqrt(jnp.mean(xv * xv, axis=-1, keepdims=True) + EPS)
        xhat = xv * r
        h_ref[...] = (xhat * gv).astype(BF16)
        dxo_v = dxo_ref[...]
        dxo_b = dxo_v.astype(BF16)
        dxob_ref[...] = dxo_b

        v = proj_ref[:, 0:dc].astype(F32)
        gb = proj_ref[:, dc:2 * dc].astype(F32)
        gc = proj_ref[:, 2 * dc:3 * dc].astype(F32)
        u = proj_ref[:, 3 * dc:4 * dc].astype(F32)
        first = jnp.where(tile > 0, 1.0, 0.0)
        zbuf[0:HALO, :] = halo_ref[:, 2 * dc:3 * dc].astype(F32) * halo_ref[:, 0:dc].astype(F32) * first
        ubuf[0:HALO, :] = halo_ref[:, 3 * dc:4 * dc].astype(F32) * first
        z = gc * v
        zbuf[HALO:HALO + tm, :] = z
        ubuf[HALO:HALO + tm, :] = u
        z1 = zbuf[HALO - 1:HALO - 1 + tm, :]
        z2 = zbuf[HALO - 2:HALO - 2 + tm, :]
        cw = cw_ref[...]
        conv = cw[2:3, :] * z + cw[1:2, :] * z1 + cw[0:1, :] * z2

        dy = _nt(dxo_b, wout[...])
        dya = dy[:, 0:dc]
        dgb = dya * conv
        dconv = dya * gb
        dcbuf[0:tm, :] = dconv
        dz = cw[2:3, :] * dconv + cw[1:2, :] * dcbuf[1:1 + tm, :] + cw[0:1, :] * dcbuf[2:2 + tm, :]
        dgc = dz * v
        dv = dz * gc
        dcw_ref[0:1, :] += jnp.sum(dconv * z2, axis=0, keepdims=True)
        dcw_ref[1:2, :] += jnp.sum(dconv * z1, axis=0, keepdims=True)
        dcw_ref[2:3, :] += jnp.sum(dconv * z, axis=0, keepdims=True)

        dproj_ref[:, 0:dc] = dv.astype(BF16)
        dproj_ref[:, dc:2 * dc] = dgb.astype(BF16)
        dproj_ref[:, 2 * dc:3 * dc] = dgc.astype(BF16)

        row = tile * tm + lax.broadcasted_iota(jnp.int32, (tm, 1), 0)
        for gi, w in enumerate(POOL_WINDOWS):
            cols = slice(gi * gcw, (gi + 1) * gcw)
            pooled, inv_cnt = _pool_parts(u[:, cols], ubuf, cols, w, row, tm, levels)
            pooled_b = pooled.astype(BF16)
            pw_b = pw_ref[gi].astype(BF16)
            dyb = dy[:, dc + gi * gcw:dc + (gi + 1) * gcw]
            q = _nn(pooled_b, pw_b)
            dps_ref[:, cols] += jnp.sum(q * dyb, axis=0, keepdims=True)
            dq = (dyb * ps_ref[:, cols]).astype(BF16)
            dpw_ref[gi] += _tn(pooled_b, dq)
            dpooled = _nt(dq, pw_b)
            ebuf[0:tm, cols] = dpooled * inv_cnt
            du = _window_sums(ebuf, cols, w, tm, levels, trailing=False) - dpooled
            dproj_ref[:, 3 * dc + gi * gcw:3 * dc + (gi + 1) * gcw] = du.astype(BF16)

        dh = _nn(dproj_ref[...], win[...])
        dg_ref[...] += jnp.sum(dh * xhat, axis=0, keepdims=True)
        dxh = dh * gv
        dx_ref[...] = dxo_v + r * (dxh - xhat * jnp.mean(dxh * xhat, axis=-1, keepdims=True))
        dcbuf[tm:tm + 8, :] = dcbuf[0:8, :]
        ebuf[tm:tm + HALO, :] = ebuf[0:HALO, :]

    tok = lambda i: (n_tiles - 1 - i, 0)
    halo = lambda i: (jnp.maximum((n_tiles - 1 - i) * hb - 1, 0), 0)
    one = lambda i: (0, 0)
    return _launch(
        body, name="mixer_backward", grid=(n_tiles,),
        in_specs=[pl.BlockSpec((tm, d), tok), pl.BlockSpec((tm, d), tok), pl.BlockSpec((1, d), one),
                  pl.BlockSpec((tm, 4 * dc), tok), pl.BlockSpec((HALO, 4 * dc), halo), HBM_SPEC, HBM_SPEC,
                  pl.BlockSpec(conv_w.shape, one), pl.BlockSpec(pool_w.shape, lambda i: (0, 0, 0)), pl.BlockSpec((1, dc), one)],
        out_specs=[pl.BlockSpec((tm, d), tok), pl.BlockSpec((tm, 4 * dc), tok), pl.BlockSpec((tm, d), tok), pl.BlockSpec((tm, d), tok),
                   pl.BlockSpec((1, d), one), pl.BlockSpec(conv_w.shape, one), pl.BlockSpec((1, dc), one),
                   pl.BlockSpec(pool_w.shape, lambda i: (0, 0, 0))],
        out_shape=[jax.ShapeDtypeStruct((t, d), F32), jax.ShapeDtypeStruct((t, 4 * dc), BF16), jax.ShapeDtypeStruct((t, d), BF16),
                   jax.ShapeDtypeStruct((t, d), BF16), jax.ShapeDtypeStruct((1, d), F32), jax.ShapeDtypeStruct(conv_w.shape, F32),
                   jax.ShapeDtypeStruct((1, dc), F32), jax.ShapeDtypeStruct(pool_w.shape, F32)],
        scratch_shapes=[pltpu.VMEM((4 * dc, d), BF16), pltpu.VMEM((2 * dc, d), BF16),
                        pltpu.VMEM((tm + HALO, dc), F32), pltpu.VMEM((tm + HALO, dc), F32),
                        pltpu.VMEM((tm + 8, dc), F32), pltpu.VMEM((tm + HALO, dc), F32),
                        pltpu.VMEM((WINDOW_LEVELS, tm + HALO, gcw), F32), pltpu.SemaphoreType.DMA((1 + N_CHIPS,))],
        args=(dxo, x, g, proj, proj, win_t, wout_x, conv_w, pool_w, pool_scale), cargo=cargo)


def _adam_update(w, gv, m, v):
    m_new = ADAM_B1 * m + (1.0 - ADAM_B1) * gv
    v_new = ADAM_B2 * v + (1.0 - ADAM_B2) * (gv * gv)
    m_hat = m_new / (1.0 - ADAM_B1 ** ADAM_STEP)
    v_hat = v_new / (1.0 - ADAM_B2 ** ADAM_STEP)
    return -ADAM_LR * (m_hat / (jnp.sqrt(v_hat) + ADAM_EPS) + ADAM_WD * w), m_new, v_new


def _adamw(w, grad, m, v, name):
    rows, cols = w.shape
    br = _row_block(rows, 512) if rows >= 8 else rows

    def body(w_ref, g_ref, m_ref, v_ref, go_ref, d_ref, mo_ref, vo_ref):
        gv = g_ref[...]
        go_ref[...] = gv
        d_ref[...], mo_ref[...], vo_ref[...] = _adam_update(w_ref[...], gv, m_ref[...], v_ref[...])

    blk = pl.BlockSpec((br, cols), lambda i: (i, 0))
    return pl.pallas_call(
        body, name=name,
        out_shape=[jax.ShapeDtypeStruct((rows, cols), F32)] * 4,
        grid=(rows // br,), in_specs=[blk] * 4, out_specs=[blk] * 4,
        compiler_params=pltpu.CompilerParams(dimension_semantics=("parallel",), vmem_limit_bytes=VMEM_LIMIT),
    )(w, grad, m, v)


def _adamw_transposed(w, grad_t, m, v, name):
    _, rows, cols = w.shape
    br = 256 if rows % 256 == 0 else rows

    def body(w_ref, gt_ref, m_ref, v_ref, g_ref, d_ref, mo_ref, vo_ref):
        gv = gt_ref[...].T
        g_ref[...] = gv
        d_ref[...], mo_ref[...], vo_ref[...] = _adam_update(w_ref[...], gv, m_ref[...], v_ref[...])

    blk = pl.BlockSpec((None, br, cols), lambda i: (0, i, 0))
    return pl.pallas_call(
        body, name=name,
        out_shape=[jax.ShapeDtypeStruct((1, rows, cols), F32)] * 4,
        grid=(rows // br,), in_specs=[blk, pl.BlockSpec((cols, br), lambda i: (0, i)), blk, blk], out_specs=[blk] * 4,
        compiler_params=pltpu.CompilerParams(dimension_semantics=("parallel",)),
    )(w, grad_t, m, v)


def _f32_rows_as_bf16(a, rows, cols):
    bits = lax.bitcast_convert_type(a, BF16).reshape(a.shape[0], 2 * a.shape[1])
    return jnp.pad(bits, ((0, rows - bits.shape[0]), (0, cols - bits.shape[1])))


def kernel(x, norm_ffn1, ffn1_w_gate, ffn1_w_up, ffn1_w_down, norm_mix, w_in, conv_w, pool_w, pool_scale, w_out, norm_ffn2, ffn2_w_gate, ffn2_w_up, ffn2_w_down, norm_final, loss_target, m_norm_ffn1, m_ffn1_w_gate, m_ffn1_w_up, m_ffn1_w_down, m_norm_mix, m_w_in, m_conv_w, m_pool_w, m_pool_scale, m_w_out, m_norm_ffn2, m_ffn2_w_gate, m_ffn2_w_up, m_ffn2_w_down, m_norm_final, v_norm_ffn1, v_ffn1_w_gate, v_ffn1_w_up, v_ffn1_w_down, v_norm_mix, v_w_in, v_conv_w, v_pool_w, v_pool_scale, v_w_out, v_norm_ffn2, v_ffn2_w_gate, v_ffn2_w_up, v_ffn2_w_down, v_norm_final):
    weights = dict(norm_ffn1=norm_ffn1, ffn1_w_gate=ffn1_w_gate, ffn1_w_up=ffn1_w_up, ffn1_w_down=ffn1_w_down, norm_mix=norm_mix,
                   w_in=w_in, conv_w=conv_w, pool_w=pool_w, pool_scale=pool_scale, w_out=w_out, norm_ffn2=norm_ffn2,
                   ffn2_w_gate=ffn2_w_gate, ffn2_w_up=ffn2_w_up, ffn2_w_down=ffn2_w_down, norm_final=norm_final)
    first_m = dict(norm_ffn1=m_norm_ffn1, ffn1_w_gate=m_ffn1_w_gate, ffn1_w_up=m_ffn1_w_up, ffn1_w_down=m_ffn1_w_down,
                   norm_mix=m_norm_mix, w_in=m_w_in, conv_w=m_conv_w, pool_w=m_pool_w, pool_scale=m_pool_scale, w_out=m_w_out,
                   norm_ffn2=m_norm_ffn2, ffn2_w_gate=m_ffn2_w_gate, ffn2_w_up=m_ffn2_w_up, ffn2_w_down=m_ffn2_w_down,
                   norm_final=m_norm_final)
    second_m = dict(norm_ffn1=v_norm_ffn1, ffn1_w_gate=v_ffn1_w_gate, ffn1_w_up=v_ffn1_w_up, ffn1_w_down=v_ffn1_w_down,
                    norm_mix=v_norm_mix, w_in=v_w_in, conv_w=v_conv_w, pool_w=v_pool_w, pool_scale=v_pool_scale, w_out=v_w_out,
                    norm_ffn2=v_norm_ffn2, ffn2_w_gate=v_ffn2_w_gate, ffn2_w_up=v_ffn2_w_up, ffn2_w_down=v_ffn2_w_down,
                    norm_final=v_norm_final)
    names = list(weights)

    xs = x[0]
    tgt = loss_target[0]
    t, d = xs.shape
    dc = pool_scale.shape[1]
    cx, cy, cc = _my_place()
    chip = 2 * cx + cy
    place = jnp.stack([chip, cc]).astype(jnp.int32)

    conv_rows = 32
    wout_x = jnp.concatenate([w_out[0].astype(BF16), _f32_rows_as_bf16(conv_w[0], conv_rows, d)], axis=0)

    g1, gm, g2 = norm_ffn1, norm_mix, norm_ffn2
    gf = norm_final.reshape(1, d)
    pw = pool_w[0]

    (wd1_shard, wg2_shard, wu2_shard, wd2_shard), [(wg1, wu1)] = _cast_to_bf16(
        [ffn1_w_down[0], ffn2_w_gate[0].T, ffn2_w_up[0].T, ffn2_w_down[0]], "gather_ffn1",
        [_gather_cargo([ffn1_w_gate[0].T.astype(BF16), ffn1_w_up[0].T.astype(BF16)])])
    (a1, b1, s1), [(wd1, win_t, wout_g)] = _ffn_up(
        xs, g1, wg1, wu1, "ffn1_up", [_gather_cargo([wd1_shard, w_in[0].T.astype(BF16), wout_x])])
    (x1,), [(wg2,)] = _ffn_down(xs, s1, wd1, "ffn1_down", [_gather_cargo([wg2_shard])])
    wo_rows = w_out.shape[1]
    cshard = conv_w.shape[2]
    conv_bits = wout_g.reshape(N_CHIPS, wo_rows + conv_rows, d)[:, wo_rows:wo_rows + conv_w.shape[1], :2 * cshard]
    conv_full = lax.bitcast_convert_type(conv_bits.reshape(N_CHIPS, conv_w.shape[1], cshard, 2), F32)
    conv_full = jnp.transpose(conv_full, (1, 0, 2)).reshape(conv_w.shape[1], N_CHIPS * cshard)
    (x2, proj, ymix), [(wu2,)] = _mixer_forward(x1, gm, win_t, wout_g, conv_full, pw, pool_scale, [_gather_cargo([wu2_shard])])
    (a2, b2, s2), [(wd2,)] = _ffn_up(x2, g2, wg2, wu2, "ffn2_up", [_gather_cargo([wd2_shard])])
    (dx3, sq_cols, dgf), _ = _ffn_down(x2, s2, wd2, "ffn2_down", loss_head=(gf, tgt))

    (dx2, dab2, hd2, dg2), _ = _ffn_backward(dx3, x2, g2, a2, b2, wg2, wu2, wd2, "ffn2_backward")
    p_wg2, _ = _weight_grad(dab2, hd2, "ffn2_gate_grad", lhs_part=(0, 2), rhs_part=(0, 2))
    p_wu2, [(x_wg2,)] = _weight_grad(dab2, hd2, "ffn2_up_grad", [_exchange_cargo([p_wg2])], lhs_part=(1, 2), rhs_part=(0, 2))
    p_wd2, [(x_wu2,)] = _weight_grad(s2, hd2, "ffn2_down_grad", [_exchange_cargo([p_wu2])], rhs_part=(1, 2))

    (dx1, dproj, h2, dx2b, dgm, dcw, dps, dpw), [(x_wd2,)] = _mixer_backward(
        dx2, x1, gm, proj, win_t, wout_g, conv_full, pw, pool_scale, [_exchange_cargo([p_wd2])])

    (dx0, dab1, hd1, dg1), _ = _ffn_backward(dx1, xs, g1, a1, b1, wg1, wu1, wd1, "ffn1_backward")

    npw = pw.size // d
    head = [dg1, dgm, dg2, dgf, jnp.pad(dps, ((0, 0), (0, d - dc))), jnp.pad(dcw, ((0, 0), (0, d - dc))), sq_cols]
    n_head = sum(h.shape[0] for h in head)
    base = -(-n_head // 8) * 8
    pack = jnp.concatenate(head + [jnp.zeros((base - n_head, d), F32), dpw.reshape(npw, d)], axis=0)

    p_wg1, [(packs,)] = _weight_grad(dab1, hd1, "ffn1_gate_grad", [_all_gather_small_cargo(pack)], lhs_part=(0, 2), rhs_part=(0, 2))
    p_wu1, [(x_wg1,)] = _weight_grad(dab1, hd1, "ffn1_up_grad", [_exchange_cargo([p_wg1])], lhs_part=(1, 2), rhs_part=(0, 2))
    p_wd1, [(x_wu1,)] = _weight_grad(s1, hd1, "ffn1_down_grad", [_exchange_cargo([p_wu1])], rhs_part=(1, 2))
    p_win, [(x_wd1,)] = _weight_grad(dproj, h2, "w_in_grad", [_exchange_cargo([p_wd1])])

    pairs = dict(wg1=p_wg1, wu1=p_wu1, wd1=p_wd1, wg2=p_wg2, wu2=p_wu2, wd2=p_wd2)
    landed = dict(wg1=x_wg1, wu1=x_wu1, wd1=x_wd1, wg2=x_wg2, wu2=x_wu2, wd2=x_wd2)
    ffn_halves = [_chip_sum(pairs[k], landed[k], place, k) for k in ["wg1", "wu1", "wd1", "wg2", "wu2", "wd2"]]
    p_wout, [(x_win,), ffn_both] = _weight_grad(ymix, dx2b, "w_out_grad", [_exchange_cargo([p_win]), _share_cargo(ffn_halves)])
    x_wout, = _run_cargo(_exchange_cargo([p_wout]), "grad_exchange_last")
    mix_both = _sibling_share([_chip_sum(p_win, x_win, place, "win"), _chip_sum(p_wout, x_wout, place, "wout")])
    rwg1, rwu1, rwd1, rwg2, rwu2, rwd2, rwin, rwout = [b.reshape(2 * b.shape[1], b.shape[2]) for b in list(ffn_both) + list(mix_both)]
    small = _sum_by_device(packs)
    loss = jnp.sum(small[n_head - 1]) * (0.5 / d)

    grads = {
        "norm_ffn1": small[0:1], "norm_mix": small[1:2], "norm_ffn2": small[2:3], "norm_final": small[3],
        "pool_scale": small[4:5, :dc],
        "conv_w": lax.dynamic_slice_in_dim(small[5:5 + dcw.shape[0], :dc], chip * cshard, cshard, axis=1)[None],
        "pool_w": small[base:].reshape(pool_w.shape),
        "ffn1_w_down": rwd1[None], "w_out": rwout[None], "ffn2_w_down": rwd2[None],
    }
    by_view = {"ffn1_w_gate": rwg1, "ffn1_w_up": rwu1, "ffn2_w_gate": rwg2, "ffn2_w_up": rwu2}

    deltas, new_m, new_v = {}, {}, {}
    for n in names:
        w = weights[n]
        shape = w.shape
        if n == "w_in":
            grads[n], deltas[n], new_m[n], new_v[n] = _adamw_transposed(w, rwin, first_m[n], second_m[n], "adamw_" + n)
            continue
        if n in by_view:
            view = lambda a: jnp.swapaxes(a, 1, 2)[0]
            back = lambda a: jnp.swapaxes(a[None], 1, 2)
            outs = _adamw(view(w), by_view[n], view(first_m[n]), view(second_m[n]), "adamw_" + n)
            grads[n], deltas[n], new_m[n], new_v[n] = [back(o) for o in outs]
            continue
        as2d = (lambda a: a.reshape(-1, shape[-1]))
        outs = _adamw(as2d(w), as2d(grads[n]), as2d(first_m[n]), as2d(second_m[n]), "adamw_" + n)
        grads[n], deltas[n], new_m[n], new_v[n] = [o.reshape(shape) for o in outs]

    return (loss, dx0[None], *[grads[n] for n in names], *[deltas[n] for n in names],
            *[new_m[n] for n in names], *[new_v[n] for n in names])
```
